```python
import jax
import jax.numpy as jnp
from jax import lax
import numpy as np

D_MODEL = 2048
BATCH = 8
SEQ = 2048
DEPTH = 2

GRID_W = 64
CTX_LEN = 256
HEAD_DIM = 128
N_HEADS = 8
N_KV_HEADS = 2
KV_GROUP = N_HEADS // N_KV_HEADS
Q_W = N_HEADS * HEAD_DIM
KV_W = N_KV_HEADS * HEAD_DIM
Q_BLOCK = 128
ROPE_THETA = 10000.0
ROPE_AXIS_DIM = HEAD_DIM // 2
POOL_WINDOWS = (2, 4, 8, 16)
POOL_GROUPS = len(POOL_WINDOWS)
POOL_W = D_MODEL // 4
POOL_GC = POOL_W // POOL_GROUPS
SGU_W = D_MODEL // 4
SGU_GROUPS = 4
SGU_GC = SGU_W // SGU_GROUPS
SGU_CHUNK = 128
CONV_W = D_MODEL // 4
CONV_K = 3
N_BRANCH = 4
D_FF = -(-8 * D_MODEL // (3 * 256)) * 256
ALPHA = (2 * DEPTH) ** 0.25
BETA = (8 * DEPTH) ** -0.25
LN_EPS = 1e-5
RMS_EPS = 1e-6

OFF_K = Q_W
OFF_V = OFF_K + KV_W
OFF_POOL = OFF_V + KV_W
OFF_U = OFF_POOL + POOL_W
OFF_VG = OFF_U + SGU_W
OFF_CB = OFF_VG + SGU_W
OFF_CC = OFF_CB + CONV_W
OFF_CX = OFF_CC + CONV_W
IN_W = OFF_CX + CONV_W

kernel_name = 'hybrid_diffusion_parallel_mixer'


def layer_norm(x, g, b):
    xf = x.astype(jnp.float32)
    mu = jnp.mean(xf, axis=-1, keepdims=True)
    var = jnp.mean(jnp.square(xf - mu), axis=-1, keepdims=True)
    return ((xf - mu) * lax.rsqrt(var + LN_EPS) * g + b).astype(x.dtype)


def rms_norm(x, g):
    xf = x.astype(jnp.float32)
    ms = jnp.mean(jnp.square(xf), axis=-1, keepdims=True)
    return (xf * lax.rsqrt(ms + RMS_EPS) * g).astype(x.dtype)


def axial_rope_tables(n, dtype):
    rows = n // GRID_W
    row = jnp.repeat(jnp.arange(rows), GRID_W).astype(jnp.float32)
    col = jnp.tile(jnp.arange(GRID_W), rows).astype(jnp.float32)
    inv = ROPE_THETA ** (-jnp.arange(0, ROPE_AXIS_DIM, 2, dtype=jnp.float32) / ROPE_AXIS_DIM)
    ang_r = row[:, None] * inv
    ang_c = col[:, None] * inv
    f = lambda a: a[None, :, None, :].astype(dtype)
    return (f(jnp.cos(ang_r)), f(jnp.sin(ang_r)), f(jnp.cos(ang_c)), f(jnp.sin(ang_c)))


def rotate(x, cos, sin):
    h = x.shape[-1] // 2
    x1, x2 = x[..., :h], x[..., h:]
    return jnp.concatenate([x1 * cos - x2 * sin, x2 * cos + x1 * sin], axis=-1)


def apply_axial_rope(x, tables):
    cr, sr, cc, sc = tables
    return jnp.concatenate([rotate(x[..., :ROPE_AXIS_DIM], cr, sr),
                            rotate(x[..., ROPE_AXIS_DIM:], cc, sc)], axis=-1)


def block_attention(q, k, v):
    b, n, h, d = q.shape
    nb = n // Q_BLOCK
    qb = q.reshape(b, nb, Q_BLOCK, N_KV_HEADS, KV_GROUP, d).transpose(1, 0, 2, 3, 4, 5)
    scale = d ** -0.5

    def one_block(qi):
        s = jnp.einsum('bqkgd,bskd->bkgqs', qi, k, preferred_element_type=jnp.float32) * scale
        p = jax.nn.softmax(s, axis=-1).astype(v.dtype)
        return jnp.einsum('bkgqs,bskd->bqkgd', p, v)

    o = lax.map(one_block, qb)
    return o.transpose(1, 0, 2, 3, 4, 5).reshape(b, n, h * d)


def kv_heads(pkv, k_g):
    b, n, _ = pkv.shape
    k = rms_norm(pkv[..., :KV_W].reshape(b, n, N_KV_HEADS, HEAD_DIM), k_g)
    v = pkv[..., KV_W:].reshape(b, n, N_KV_HEADS, HEAD_DIM)
    return k, v


def multiscale_pool(z):
    b, n, ch = z.shape
    cs = jnp.cumsum(z.astype(jnp.float32), axis=1)
    cs = jnp.concatenate([jnp.zeros((b, 1, ch), jnp.float32), cs], axis=1)
    t = jnp.arange(n)
    means = []
    for g, w in enumerate(POOL_WINDOWS):
        left = w // 2
        right = w - 1 - left
        hi = jnp.clip(t + right + 1, 0, n)
        lo = jnp.clip(t - left, 0, n)
        seg = cs[:, :, g * POOL_GC:(g + 1) * POOL_GC]
        total = jnp.take(seg, hi, axis=1) - jnp.take(seg, lo, axis=1)
        means.append(total / (hi - lo).astype(jnp.float32)[None, :, None])
    return jnp.concatenate(means, axis=-1).astype(z.dtype) - z


def spatial_gating(u, v, ln_g, ln_b, w_s, b_s):
    b, n, _ = v.shape
    v = layer_norm(v, ln_g, ln_b)
    vc = v.reshape(b, n // SGU_CHUNK, SGU_CHUNK, SGU_GROUPS, SGU_GC)
    s = jnp.einsum('gpq,bnqgc->bnpgc', w_s, vc) + b_s.T[None, None, :, :, None]
    return u * s.reshape(b, n, SGU_W)


def short_conv(z, w):
    zp = jnp.pad(z, ((0, 0), (1, 1), (0, 0)))
    return zp[:, :-2] * w[0] + zp[:, 1:-1] * w[1] + zp[:, 2:] * w[2]


def local_branches(pr, pool_w, pool_scale, sgu_ln_g, sgu_ln_b, sgu_w, sgu_b, conv_w):
    b, n, _ = pr.shape
    d = multiscale_pool(pr[..., OFF_POOL:OFF_U])
    y_pool = jnp.einsum('bngc,gce->bnge', d.reshape(b, n, POOL_GROUPS, POOL_GC), pool_w).reshape(b, n, POOL_W) * pool_scale
    y_sgu = spatial_gating(jax.nn.gelu(pr[..., OFF_U:OFF_VG]), jax.nn.gelu(pr[..., OFF_VG:OFF_CB]),
                           sgu_ln_g, sgu_ln_b, sgu_w, sgu_b)
    y_conv = pr[..., OFF_CB:OFF_CC] * short_conv(pr[..., OFF_CC:OFF_CX] * pr[..., OFF_CX:IN_W], conv_w)
    return y_pool, y_sgu, y_conv


def merge_branches(h, branches, w_brs, w_gate, b_gate, w_o):
    terms = []
    for k in range(N_BRANCH):
        g = jax.nn.sigmoid(h @ w_gate[:, k * D_MODEL:(k + 1) * D_MODEL] + b_gate[k * D_MODEL:(k + 1) * D_MODEL])
        terms.append(g * (branches[k] @ w_brs[k]))
    merged = terms[0] + terms[1] + terms[2] + terms[3]
    return merged @ w_o


def hybrid_mixer(h_c, h_l, rope, ctx_out, w_in, q_g, k_g, pool_w, pool_scale, sgu_ln_g, sgu_ln_b,
                 sgu_w, sgu_b, conv_w, w_br_attn, w_br_pool, w_br_sgu, w_br_conv, w_gate, b_gate, w_o):
    b, n, _ = h_l.shape
    nc = h_c.shape[1]
    w_brs = (w_br_attn, w_br_pool, w_br_sgu, w_br_conv)
    p_l = h_l @ w_in
    q_l = apply_axial_rope(rms_norm(p_l[..., :Q_W].reshape(b, n, N_HEADS, HEAD_DIM), q_g), rope)
    k_l, v_l = kv_heads(p_l[..., OFF_K:OFF_POOL], k_g)
    k_l = apply_axial_rope(k_l, rope)
    if ctx_out:
        p_c = h_c @ w_in
        pkv_c = p_c[..., OFF_K:OFF_POOL]
    else:
        pkv_c = h_c @ w_in[:, OFF_K:OFF_POOL]
    k_c, v_c = kv_heads(pkv_c, k_g)
    a_l = block_attention(q_l, jnp.concatenate([k_c, k_l], axis=1), jnp.concatenate([v_c, v_l], axis=1))
    out_l = merge_branches(h_l, (a_l,) + local_branches(p_l, pool_w, pool_scale, sgu_ln_g, sgu_ln_b, sgu_w, sgu_b, conv_w),
                           w_brs, w_gate, b_gate, w_o)
    if not ctx_out:
        return None, out_l
    q_c = rms_norm(p_c[..., :Q_W].reshape(b, nc, N_HEADS, HEAD_DIM), q_g)
    a_c = block_attention(q_c, k_c, v_c)
    out_c = merge_branches(h_c, (a_c,) + local_branches(p_c, pool_w, pool_scale, sgu_ln_g, sgu_ln_b, sgu_w, sgu_b, conv_w),
                           w_brs, w_gate, b_gate, w_o)
    return out_c, out_l


def swiglu(h, w_g, w_u, w_d):
    return (jax.nn.silu(h @ w_g) * (h @ w_u)) @ w_d


def _fwd_setup_inputs(seed: int = 0) -> dict:
    key = jax.random.key(seed)
    ks = jax.random.split(key, 32)

    def nrm(k, shape, std):
        return jax.random.normal(k, shape, jnp.float32) * std

    def gain(k, shape):
        return 1.0 + nrm(k, shape, 0.02)

    L = DEPTH
    d = D_MODEL
    return {
        'x': nrm(ks[0], (BATCH, SEQ, d), 1.0),
        'c': nrm(ks[1], (BATCH, d), 1.0),
        'ctx': nrm(ks[2], (BATCH, CTX_LEN, d), 1.0),
        'c_ctx': nrm(ks[3], (d,), 1.0),
        'w_ada': nrm(ks[4], (L, d, 6 * d), 0.5 * d ** -0.5),
        'b_ada': nrm(ks[5], (L, 6 * d), 0.01),
        'w_in': nrm(ks[6], (L, d, IN_W), d ** -0.5),
        'q_norm_g': gain(ks[7], (L, HEAD_DIM)),
        'k_norm_g': gain(ks[8], (L, HEAD_DIM)),
        'pool_w': nrm(ks[9], (L, POOL_GROUPS, POOL_GC, POOL_GC), POOL_GC ** -0.5),
        'pool_scale': 1.0 + nrm(ks[10], (L, POOL_W), 0.1),
        'sgu_ln_g': gain(ks[11], (L, SGU_W)),
        'sgu_ln_b': nrm(ks[12], (L, SGU_W), 0.02),
        'sgu_w': nrm(ks[13], (L, SGU_GROUPS, SGU_CHUNK, SGU_CHUNK), SGU_CHUNK ** -0.5),
        'sgu_b': 1.0 + nrm(ks[14], (L, SGU_GROUPS, SGU_CHUNK), 0.02),
        'conv_w': nrm(ks[15], (L, CONV_K, CONV_W), CONV_K ** -0.5),
        'w_br_attn': nrm(ks[16], (L, Q_W, d), Q_W ** -0.5),
        'w_br_pool': nrm(ks[17], (L, POOL_W, d), POOL_W ** -0.5),
        'w_br_sgu': nrm(ks[18], (L, SGU_W, d), SGU_W ** -0.5),
        'w_br_conv': nrm(ks[19], (L, CONV_W, d), CONV_W ** -0.5),
        'w_gate': nrm(ks[20], (L, d, N_BRANCH * d), d ** -0.5),
        'b_gate': nrm(ks[21], (L, N_BRANCH * d), 0.01),
        'w_o': nrm(ks[22], (L, d, d), BETA * d ** -0.5),
        'ln1_g': gain(ks[23], (L, d)),
        'ln1_b': nrm(ks[24], (L, d), 0.02),
        'w_ff_gate': nrm(ks[25], (L, d, D_FF), d ** -0.5),
        'w_ff_up': nrm(ks[26], (L, d, D_FF), d ** -0.5),
        'w_ff_down': nrm(ks[27], (L, D_FF, d), BETA * D_FF ** -0.5),
        'ln2_g': gain(ks[28], (L, d)),
        'ln2_b': nrm(ks[29], (L, d), 0.02),
    }


def _fwd_reference(x, c, ctx, c_ctx, w_ada, b_ada, w_in, q_norm_g, k_norm_g, pool_w, pool_scale,
              sgu_ln_g, sgu_ln_b, sgu_w, sgu_b, conv_w, w_br_attn, w_br_pool, w_br_sgu, w_br_conv,
              w_gate, b_gate, w_o, ln1_g, ln1_b, w_ff_gate, w_ff_up, w_ff_down, ln2_g, ln2_b):
    rope = axial_rope_tables(x.shape[1], x.dtype)
    x_l, x_c = x, ctx
    for i in range(DEPTH):
        last = i == DEPTH - 1
        mod_l = jax.nn.silu(c) @ w_ada[i] + b_ada[i]
        mod_c = jax.nn.silu(c_ctx) @ w_ada[i] + b_ada[i]
        sh1_l, sc1_l, g1_l, sh2_l, sc2_l, g2_l = jnp.split(mod_l[:, None, :], 6, axis=-1)
        sh1_c, sc1_c, g1_c, sh2_c, sc2_c, g2_c = jnp.split(mod_c, 6, axis=-1)
        h_l = x_l * (1.0 + sc1_l) + sh1_l
        h_c = x_c * (1.0 + sc1_c) + sh1_c
        m_c, m_l = hybrid_mixer(h_c, h_l, rope, not last, w_in[i], q_norm_g[i], k_norm_g[i], pool_w[i],
                                pool_scale[i], sgu_ln_g[i], sgu_ln_b[i], sgu_w[i], sgu_b[i], conv_w[i],
                                w_br_attn[i], w_br_pool[i], w_br_sgu[i], w_br_conv[i], w_gate[i], b_gate[i], w_o[i])
        x_l = layer_norm(ALPHA * x_l + g1_l * m_l, ln1_g[i], ln1_b[i])
        h_l = x_l * (1.0 + sc2_l) + sh2_l
        x_l = layer_norm(ALPHA * x_l + g2_l * swiglu(h_l, w_ff_gate[i], w_ff_up[i], w_ff_down[i]), ln2_g[i], ln2_b[i])
        if not last:
            x_c = layer_norm(ALPHA * x_c + g1_c * m_c, ln1_g[i], ln1_b[i])
            h_c = x_c * (1.0 + sc2_c) + sh2_c
            x_c = layer_norm(ALPHA * x_c + g2_c * swiglu(h_c, w_ff_gate[i], w_ff_up[i], w_ff_down[i]), ln2_g[i], ln2_b[i])
    return x_l


import jax as _jax
import jax.numpy as _jnp

TWIN_FORMAT = 'train_step'
FWD_PARAMS = ['x', 'c', 'ctx', 'c_ctx', 'w_ada', 'b_ada', 'w_in', 'q_norm_g', 'k_norm_g', 'pool_w', 'pool_scale', 'sgu_ln_g', 'sgu_ln_b', 'sgu_w', 'sgu_b', 'conv_w', 'w_br_attn', 'w_br_pool', 'w_br_sgu', 'w_br_conv', 'w_gate', 'b_gate', 'w_o', 'ln1_g', 'ln1_b', 'w_ff_gate', 'w_ff_up', 'w_ff_down', 'ln2_g', 'ln2_b']
TWIN_WEIGHTS = ['c_ctx', 'w_ada', 'b_ada', 'w_in', 'q_norm_g', 'k_norm_g', 'pool_w', 'pool_scale', 'sgu_ln_g', 'sgu_ln_b', 'sgu_w', 'sgu_b', 'conv_w', 'w_br_attn', 'w_br_pool', 'w_br_sgu', 'w_br_conv', 'w_gate', 'b_gate', 'w_o', 'ln1_g', 'ln1_b', 'w_ff_gate', 'w_ff_up', 'w_ff_down', 'ln2_g', 'ln2_b']
TWIN_DIFF_INPUT = 'x'
TWIN_INPUTS = ['x', 'c', 'ctx', 'c_ctx', 'w_ada', 'b_ada', 'w_in', 'q_norm_g', 'k_norm_g', 'pool_w', 'pool_scale', 'sgu_ln_g', 'sgu_ln_b', 'sgu_w', 'sgu_b', 'conv_w', 'w_br_attn', 'w_br_pool', 'w_br_sgu', 'w_br_conv', 'w_gate', 'b_gate', 'w_o', 'ln1_g', 'ln1_b', 'w_ff_gate', 'w_ff_up', 'w_ff_down', 'ln2_g', 'ln2_b', 'loss_target', 'm_c_ctx', 'm_w_ada', 'm_b_ada', 'm_w_in', 'm_q_norm_g', 'm_k_norm_g', 'm_pool_w', 'm_pool_scale', 'm_sgu_ln_g', 'm_sgu_ln_b', 'm_sgu_w', 'm_sgu_b', 'm_conv_w', 'm_w_br_attn', 'm_w_br_pool', 'm_w_br_sgu', 'm_w_br_conv', 'm_w_gate', 'm_b_gate', 'm_w_o', 'm_ln1_g', 'm_ln1_b', 'm_w_ff_gate', 'm_w_ff_up', 'm_w_ff_down', 'm_ln2_g', 'm_ln2_b', 'v_c_ctx', 'v_w_ada', 'v_b_ada', 'v_w_in', 'v_q_norm_g', 'v_k_norm_g', 'v_pool_w', 'v_pool_scale', 'v_sgu_ln_g', 'v_sgu_ln_b', 'v_sgu_w', 'v_sgu_b', 'v_conv_w', 'v_w_br_attn', 'v_w_br_pool', 'v_w_br_sgu', 'v_w_br_conv', 'v_w_gate', 'v_b_gate', 'v_w_o', 'v_ln1_g', 'v_ln1_b', 'v_w_ff_gate', 'v_w_ff_up', 'v_w_ff_down', 'v_ln2_g', 'v_ln2_b']
TWIN_OUTPUTS = ['loss', 'grad_x', 'grad_c_ctx', 'grad_w_ada', 'grad_b_ada', 'grad_w_in', 'grad_q_norm_g', 'grad_k_norm_g', 'grad_pool_w', 'grad_pool_scale', 'grad_sgu_ln_g', 'grad_sgu_ln_b', 'grad_sgu_w', 'grad_sgu_b', 'grad_conv_w', 'grad_w_br_attn', 'grad_w_br_pool', 'grad_w_br_sgu', 'grad_w_br_conv', 'grad_w_gate', 'grad_b_gate', 'grad_w_o', 'grad_ln1_g', 'grad_ln1_b', 'grad_w_ff_gate', 'grad_w_ff_up', 'grad_w_ff_down', 'grad_ln2_g', 'grad_ln2_b', 'delta_c_ctx', 'delta_w_ada', 'delta_b_ada', 'delta_w_in', 'delta_q_norm_g', 'delta_k_norm_g', 'delta_pool_w', 'delta_pool_scale', 'delta_sgu_ln_g', 'delta_sgu_ln_b', 'delta_sgu_w', 'delta_sgu_b', 'delta_conv_w', 'delta_w_br_attn', 'delta_w_br_pool', 'delta_w_br_sgu', 'delta_w_br_conv', 'delta_w_gate', 'delta_b_gate', 'delta_w_o', 'delta_ln1_g', 'delta_ln1_b', 'delta_w_ff_gate', 'delta_w_ff_up', 'delta_w_ff_down', 'delta_ln2_g', 'delta_ln2_b', 'new_m_c_ctx', 'new_m_w_ada', 'new_m_b_ada', 'new_m_w_in', 'new_m_q_norm_g', 'new_m_k_norm_g', 'new_m_pool_w', 'new_m_pool_scale', 'new_m_sgu_ln_g', 'new_m_sgu_ln_b', 'new_m_sgu_w', 'new_m_sgu_b', 'new_m_conv_w', 'new_m_w_br_attn', 'new_m_w_br_pool', 'new_m_w_br_sgu', 'new_m_w_br_conv', 'new_m_w_gate', 'new_m_b_gate', 'new_m_w_o', 'new_m_ln1_g', 'new_m_ln1_b', 'new_m_w_ff_gate', 'new_m_w_ff_up', 'new_m_w_ff_down', 'new_m_ln2_g', 'new_m_ln2_b', 'new_v_c_ctx', 'new_v_w_ada', 'new_v_b_ada', 'new_v_w_in', 'new_v_q_norm_g', 'new_v_k_norm_g', 'new_v_pool_w', 'new_v_pool_scale', 'new_v_sgu_ln_g', 'new_v_sgu_ln_b', 'new_v_sgu_w', 'new_v_sgu_b', 'new_v_conv_w', 'new_v_w_br_attn', 'new_v_w_br_pool', 'new_v_w_br_sgu', 'new_v_w_br_conv', 'new_v_w_gate', 'new_v_b_gate', 'new_v_w_o', 'new_v_ln1_g', 'new_v_ln1_b', 'new_v_w_ff_gate', 'new_v_w_ff_up', 'new_v_w_ff_down', 'new_v_ln2_g', 'new_v_ln2_b']
TWIN_LEAF_KINDS = {'loss': 'loss', 'grad_x': 'grad_x', 'grad_c_ctx': 'grad_w', 'grad_w_ada': 'grad_w', 'grad_b_ada': 'grad_w', 'grad_w_in': 'grad_w', 'grad_q_norm_g': 'grad_w', 'grad_k_norm_g': 'grad_w', 'grad_pool_w': 'grad_w', 'grad_pool_scale': 'grad_w', 'grad_sgu_ln_g': 'grad_w', 'grad_sgu_ln_b': 'grad_w', 'grad_sgu_w': 'grad_w', 'grad_sgu_b': 'grad_w', 'grad_conv_w': 'grad_w', 'grad_w_br_attn': 'grad_w', 'grad_w_br_pool': 'grad_w', 'grad_w_br_sgu': 'grad_w', 'grad_w_br_conv': 'grad_w', 'grad_w_gate': 'grad_w', 'grad_b_gate': 'grad_w', 'grad_w_o': 'grad_w', 'grad_ln1_g': 'grad_w', 'grad_ln1_b': 'grad_w', 'grad_w_ff_gate': 'grad_w', 'grad_w_ff_up': 'grad_w', 'grad_w_ff_down': 'grad_w', 'grad_ln2_g': 'grad_w', 'grad_ln2_b': 'grad_w', 'delta_c_ctx': 'delta_w', 'delta_w_ada': 'delta_w', 'delta_b_ada': 'delta_w', 'delta_w_in': 'delta_w', 'delta_q_norm_g': 'delta_w', 'delta_k_norm_g': 'delta_w', 'delta_pool_w': 'delta_w', 'delta_pool_scale': 'delta_w', 'delta_sgu_ln_g': 'delta_w', 'delta_sgu_ln_b': 'delta_w', 'delta_sgu_w': 'delta_w', 'delta_sgu_b': 'delta_w', 'delta_conv_w': 'delta_w', 'delta_w_br_attn': 'delta_w', 'delta_w_br_pool': 'delta_w', 'delta_w_br_sgu': 'delta_w', 'delta_w_br_conv': 'delta_w', 'delta_w_gate': 'delta_w', 'delta_b_gate': 'delta_w', 'delta_w_o': 'delta_w', 'delta_ln1_g': 'delta_w', 'delta_ln1_b': 'delta_w', 'delta_w_ff_gate': 'delta_w', 'delta_w_ff_up': 'delta_w', 'delta_w_ff_down': 'delta_w', 'delta_ln2_g': 'delta_w', 'delta_ln2_b': 'delta_w', 'new_m_c_ctx': 'new_m', 'new_m_w_ada': 'new_m', 'new_m_b_ada': 'new_m', 'new_m_w_in': 'new_m', 'new_m_q_norm_g': 'new_m', 'new_m_k_norm_g': 'new_m', 'new_m_pool_w': 'new_m', 'new_m_pool_scale': 'new_m', 'new_m_sgu_ln_g': 'new_m', 'new_m_sgu_ln_b': 'new_m', 'new_m_sgu_w': 'new_m', 'new_m_sgu_b': 'new_m', 'new_m_conv_w': 'new_m', 'new_m_w_br_attn': 'new_m', 'new_m_w_br_pool': 'new_m', 'new_m_w_br_sgu': 'new_m', 'new_m_w_br_conv': 'new_m', 'new_m_w_gate': 'new_m', 'new_m_b_gate': 'new_m', 'new_m_w_o': 'new_m', 'new_m_ln1_g': 'new_m', 'new_m_ln1_b': 'new_m', 'new_m_w_ff_gate': 'new_m', 'new_m_w_ff_up': 'new_m', 'new_m_w_ff_down': 'new_m', 'new_m_ln2_g': 'new_m', 'new_m_ln2_b': 'new_m', 'new_v_c_ctx': 'new_v', 'new_v_w_ada': 'new_v', 'new_v_b_ada': 'new_v', 'new_v_w_in': 'new_v', 'new_v_q_norm_g': 'new_v', 'new_v_k_norm_g': 'new_v', 'new_v_pool_w': 'new_v', 'new_v_pool_scale': 'new_v', 'new_v_sgu_ln_g': 'new_v', 'new_v_sgu_ln_b': 'new_v', 'new_v_sgu_w': 'new_v', 'new_v_sgu_b': 'new_v', 'new_v_conv_w': 'new_v', 'new_v_w_br_attn': 'new_v', 'new_v_w_br_pool': 'new_v', 'new_v_w_br_sgu': 'new_v', 'new_v_w_br_conv': 'new_v', 'new_v_w_gate': 'new_v', 'new_v_b_gate': 'new_v', 'new_v_w_o': 'new_v', 'new_v_ln1_g': 'new_v', 'new_v_ln1_b': 'new_v', 'new_v_w_ff_gate': 'new_v', 'new_v_w_ff_up': 'new_v', 'new_v_w_ff_down': 'new_v', 'new_v_ln2_g': 'new_v', 'new_v_ln2_b': 'new_v'}


def _forward(args):
    return _fwd_reference(*[args[k] for k in FWD_PARAMS])


def _output_shape():
    out = _jax.eval_shape(lambda: _forward(_fwd_setup_inputs(0)))
    return out.shape, out.dtype

N_MICROBATCH = 1
ADAM_LR = 0.001
ADAM_B1 = 0.9
ADAM_B2 = 0.999
ADAM_EPS = 1e-08
ADAM_WD = 0.01
ADAM_STEP = 10
PER_EXAMPLE_BATCH_AXIS = {'x': 0, 'c': 0, 'ctx': 0, 'loss_target': 0}
SHARED_INPUTS = []
_WEIGHT_DTYPES = {'c_ctx': _jnp.float32, 'w_ada': _jnp.float32, 'b_ada': _jnp.float32, 'w_in': _jnp.float32, 'q_norm_g': _jnp.float32, 'k_norm_g': _jnp.float32, 'pool_w': _jnp.float32, 'pool_scale': _jnp.float32, 'sgu_ln_g': _jnp.float32, 'sgu_ln_b': _jnp.float32, 'sgu_w': _jnp.float32, 'sgu_b': _jnp.float32, 'conv_w': _jnp.float32, 'w_br_attn': _jnp.float32, 'w_br_pool': _jnp.float32, 'w_br_sgu': _jnp.float32, 'w_br_conv': _jnp.float32, 'w_gate': _jnp.float32, 'b_gate': _jnp.float32, 'w_o': _jnp.float32, 'ln1_g': _jnp.float32, 'ln1_b': _jnp.float32, 'w_ff_gate': _jnp.float32, 'w_ff_up': _jnp.float32, 'w_ff_down': _jnp.float32, 'ln2_g': _jnp.float32, 'ln2_b': _jnp.float32}
MOMENT_SCALE = {'c_ctx': 8.764238e-04, 'w_ada': 8.222578e-03, 'b_ada': 1.412256e-02, 'w_in': 6.795291e-03, 'q_norm_g': 1.240305e-03, 'k_norm_g': 1.237074e-03, 'pool_w': 6.766257e-03, 'pool_scale': 6.791369e-03, 'sgu_ln_g': 5.248590e-03, 'sgu_ln_b': 5.297259e-03, 'sgu_w': 5.213494e-03, 'sgu_b': 5.307000e-03, 'conv_w': 9.638163e-03, 'w_br_attn': 1.413528e-03, 'w_br_pool': 3.377929e-03, 'w_br_sgu': 3.845473e-03, 'w_br_conv': 4.735072e-03, 'w_gate': 1.439013e-03, 'b_gate': 1.355401e-03, 'w_o': 1.420651e-02, 'ln1_g': 2.818876e-01, 'ln1_b': 1.440006e-01, 'w_ff_gate': 2.988919e-03, 'w_ff_up': 2.898575e-03, 'w_ff_down': 9.621637e-03, 'ln2_g': 5.673959e+00, 'ln2_b': 2.456881e-01}


def _to_microbatches(a, axis):
    t = _jnp.moveaxis(a, axis, 0)
    t = t.reshape((N_MICROBATCH, t.shape[0] // N_MICROBATCH) + t.shape[1:])
    return _jnp.moveaxis(t, 1, axis + 1)


def setup_inputs(seed: int = 0) -> dict:
    inp = _fwd_setup_inputs(seed)
    key = _jax.random.fold_in(_jax.random.key(seed), 7919)
    shape, _ = _output_shape()
    out = dict(inp)
    out["loss_target"] = _jax.random.normal(_jax.random.fold_in(key, 0), shape, _jnp.float32)
    for i, name in enumerate(TWIN_WEIGHTS):
        w = inp[name].astype(_jnp.float32)
        if MOMENT_SCALE is None:
            s = _jnp.sqrt(_jnp.mean(_jnp.square(w)) + 1e-30)
        else:
            s = MOMENT_SCALE[name]
        km, kv = _jax.random.split(_jax.random.fold_in(key, i + 1))
        out[name] = w
        out["m_" + name] = s * _jax.random.normal(km, w.shape, _jnp.float32)
        out["v_" + name] = (s * s) * _jax.random.uniform(kv, w.shape, _jnp.float32, 0.5, 1.5)
    if N_MICROBATCH > 1:
        for name, axis in PER_EXAMPLE_BATCH_AXIS.items():
            out[name] = _to_microbatches(out[name], axis)
    return {'x': out['x'], 'c': out['c'], 'ctx': out['ctx'], 'c_ctx': out['c_ctx'], 'w_ada': out['w_ada'], 'b_ada': out['b_ada'], 'w_in': out['w_in'], 'q_norm_g': out['q_norm_g'], 'k_norm_g': out['k_norm_g'], 'pool_w': out['pool_w'], 'pool_scale': out['pool_scale'], 'sgu_ln_g': out['sgu_ln_g'], 'sgu_ln_b': out['sgu_ln_b'], 'sgu_w': out['sgu_w'], 'sgu_b': out['sgu_b'], 'conv_w': out['conv_w'], 'w_br_attn': out['w_br_attn'], 'w_br_pool': out['w_br_pool'], 'w_br_sgu': out['w_br_sgu'], 'w_br_conv': out['w_br_conv'], 'w_gate': out['w_gate'], 'b_gate': out['b_gate'], 'w_o': out['w_o'], 'ln1_g': out['ln1_g'], 'ln1_b': out['ln1_b'], 'w_ff_gate': out['w_ff_gate'], 'w_ff_up': out['w_ff_up'], 'w_ff_down': out['w_ff_down'], 'ln2_g': out['ln2_g'], 'ln2_b': out['ln2_b'], 'loss_target': out['loss_target'], 'm_c_ctx': out['m_c_ctx'], 'm_w_ada': out['m_w_ada'], 'm_b_ada': out['m_b_ada'], 'm_w_in': out['m_w_in'], 'm_q_norm_g': out['m_q_norm_g'], 'm_k_norm_g': out['m_k_norm_g'], 'm_pool_w': out['m_pool_w'], 'm_pool_scale': out['m_pool_scale'], 'm_sgu_ln_g': out['m_sgu_ln_g'], 'm_sgu_ln_b': out['m_sgu_ln_b'], 'm_sgu_w': out['m_sgu_w'], 'm_sgu_b': out['m_sgu_b'], 'm_conv_w': out['m_conv_w'], 'm_w_br_attn': out['m_w_br_attn'], 'm_w_br_pool': out['m_w_br_pool'], 'm_w_br_sgu': out['m_w_br_sgu'], 'm_w_br_conv': out['m_w_br_conv'], 'm_w_gate': out['m_w_gate'], 'm_b_gate': out['m_b_gate'], 'm_w_o': out['m_w_o'], 'm_ln1_g': out['m_ln1_g'], 'm_ln1_b': out['m_ln1_b'], 'm_w_ff_gate': out['m_w_ff_gate'], 'm_w_ff_up': out['m_w_ff_up'], 'm_w_ff_down': out['m_w_ff_down'], 'm_ln2_g': out['m_ln2_g'], 'm_ln2_b': out['m_ln2_b'], 'v_c_ctx': out['v_c_ctx'], 'v_w_ada': out['v_w_ada'], 'v_b_ada': out['v_b_ada'], 'v_w_in': out['v_w_in'], 'v_q_norm_g': out['v_q_norm_g'], 'v_k_norm_g': out['v_k_norm_g'], 'v_pool_w': out['v_pool_w'], 'v_pool_scale': out['v_pool_scale'], 'v_sgu_ln_g': out['v_sgu_ln_g'], 'v_sgu_ln_b': out['v_sgu_ln_b'], 'v_sgu_w': out['v_sgu_w'], 'v_sgu_b': out['v_sgu_b'], 'v_conv_w': out['v_conv_w'], 'v_w_br_attn': out['v_w_br_attn'], 'v_w_br_pool': out['v_w_br_pool'], 'v_w_br_sgu': out['v_w_br_sgu'], 'v_w_br_conv': out['v_w_br_conv'], 'v_w_gate': out['v_w_gate'], 'v_b_gate': out['v_b_gate'], 'v_w_o': out['v_w_o'], 'v_ln1_g': out['v_ln1_g'], 'v_ln1_b': out['v_ln1_b'], 'v_w_ff_gate': out['v_w_ff_gate'], 'v_w_ff_up': out['v_w_ff_up'], 'v_w_ff_down': out['v_w_ff_down'], 'v_ln2_g': out['v_ln2_g'], 'v_ln2_b': out['v_ln2_b']}


def _loss(weights, diff, rest, loss_target):
    with _jax.named_scope("forward"):
        args = {**rest, TWIN_DIFF_INPUT: diff, **{k: w.astype(_WEIGHT_DTYPES[k]) for k, w in weights.items()}}
        y = _forward(args)
    with _jax.named_scope("loss_head"):
        err = _jnp.square(y.astype(_jnp.float32) - loss_target)
        return 0.5 * _jnp.sum(_jnp.mean(err, axis=-1)) if err.ndim else 0.5 * err


def _adamw(w, g, m, v):
    m = ADAM_B1 * m + (1.0 - ADAM_B1) * g
    v = ADAM_B2 * v + (1.0 - ADAM_B2) * _jnp.square(g)
    m_hat = m / (1.0 - ADAM_B1 ** ADAM_STEP)
    v_hat = v / (1.0 - ADAM_B2 ** ADAM_STEP)
    delta = -ADAM_LR * (m_hat / (_jnp.sqrt(v_hat) + ADAM_EPS) + ADAM_WD * w)
    return delta, m, v


def reference(x, c, ctx, c_ctx, w_ada, b_ada, w_in, q_norm_g, k_norm_g, pool_w, pool_scale, sgu_ln_g, sgu_ln_b, sgu_w, sgu_b, conv_w, w_br_attn, w_br_pool, w_br_sgu, w_br_conv, w_gate, b_gate, w_o, ln1_g, ln1_b, w_ff_gate, w_ff_up, w_ff_down, ln2_g, ln2_b, loss_target, m_c_ctx, m_w_ada, m_b_ada, m_w_in, m_q_norm_g, m_k_norm_g, m_pool_w, m_pool_scale, m_sgu_ln_g, m_sgu_ln_b, m_sgu_w, m_sgu_b, m_conv_w, m_w_br_attn, m_w_br_pool, m_w_br_sgu, m_w_br_conv, m_w_gate, m_b_gate, m_w_o, m_ln1_g, m_ln1_b, m_w_ff_gate, m_w_ff_up, m_w_ff_down, m_ln2_g, m_ln2_b, v_c_ctx, v_w_ada, v_b_ada, v_w_in, v_q_norm_g, v_k_norm_g, v_pool_w, v_pool_scale, v_sgu_ln_g, v_sgu_ln_b, v_sgu_w, v_sgu_b, v_conv_w, v_w_br_attn, v_w_br_pool, v_w_br_sgu, v_w_br_conv, v_w_gate, v_b_gate, v_w_o, v_ln1_g, v_ln1_b, v_w_ff_gate, v_w_ff_up, v_w_ff_down, v_ln2_g, v_ln2_b):
    given = dict(x=x, c=c, ctx=ctx, c_ctx=c_ctx, w_ada=w_ada, b_ada=b_ada, w_in=w_in, q_norm_g=q_norm_g, k_norm_g=k_norm_g, pool_w=pool_w, pool_scale=pool_scale, sgu_ln_g=sgu_ln_g, sgu_ln_b=sgu_ln_b, sgu_w=sgu_w, sgu_b=sgu_b, conv_w=conv_w, w_br_attn=w_br_attn, w_br_pool=w_br_pool, w_br_sgu=w_br_sgu, w_br_conv=w_br_conv, w_gate=w_gate, b_gate=b_gate, w_o=w_o, ln1_g=ln1_g, ln1_b=ln1_b, w_ff_gate=w_ff_gate, w_ff_up=w_ff_up, w_ff_down=w_ff_down, ln2_g=ln2_g, ln2_b=ln2_b, loss_target=loss_target, m_c_ctx=m_c_ctx, m_w_ada=m_w_ada, m_b_ada=m_b_ada, m_w_in=m_w_in, m_q_norm_g=m_q_norm_g, m_k_norm_g=m_k_norm_g, m_pool_w=m_pool_w, m_pool_scale=m_pool_scale, m_sgu_ln_g=m_sgu_ln_g, m_sgu_ln_b=m_sgu_ln_b, m_sgu_w=m_sgu_w, m_sgu_b=m_sgu_b, m_conv_w=m_conv_w, m_w_br_attn=m_w_br_attn, m_w_br_pool=m_w_br_pool, m_w_br_sgu=m_w_br_sgu, m_w_br_conv=m_w_br_conv, m_w_gate=m_w_gate, m_b_gate=m_b_gate, m_w_o=m_w_o, m_ln1_g=m_ln1_g, m_ln1_b=m_ln1_b, m_w_ff_gate=m_w_ff_gate, m_w_ff_up=m_w_ff_up, m_w_ff_down=m_w_ff_down, m_ln2_g=m_ln2_g, m_ln2_b=m_ln2_b, v_c_ctx=v_c_ctx, v_w_ada=v_w_ada, v_b_ada=v_b_ada, v_w_in=v_w_in, v_q_norm_g=v_q_norm_g, v_k_norm_g=v_k_norm_g, v_pool_w=v_pool_w, v_pool_scale=v_pool_scale, v_sgu_ln_g=v_sgu_ln_g, v_sgu_ln_b=v_sgu_ln_b, v_sgu_w=v_sgu_w, v_sgu_b=v_sgu_b, v_conv_w=v_conv_w, v_w_br_attn=v_w_br_attn, v_w_br_pool=v_w_br_pool, v_w_br_sgu=v_w_br_sgu, v_w_br_conv=v_w_br_conv, v_w_gate=v_w_gate, v_b_gate=v_b_gate, v_w_o=v_w_o, v_ln1_g=v_ln1_g, v_ln1_b=v_ln1_b, v_w_ff_gate=v_w_ff_gate, v_w_ff_up=v_w_ff_up, v_w_ff_down=v_w_ff_down, v_ln2_g=v_ln2_g, v_ln2_b=v_ln2_b)
    weights = {n: given[n] for n in TWIN_WEIGHTS}
    shared = {n: given[n] for n in SHARED_INPUTS}
    per_example = {n: given[n] for n in ['x', 'c', 'ctx']}
    grad_fn = _jax.value_and_grad(_loss, argnums=(0, 1))

    def one_microbatch(ex, loss_target):
        ex = dict(ex)
        diff = ex.pop(TWIN_DIFF_INPUT)
        return grad_fn(weights, diff, {**shared, **ex}, loss_target)

    if N_MICROBATCH == 1:
        loss, (grad_w, grad_x) = one_microbatch(per_example, given["loss_target"])
    else:
        def body(carry, xs):
            loss_sum, grad_sum = carry
            l_k, (gw_k, gx_k) = one_microbatch(xs[0], xs[1])
            with _jax.named_scope("update"):
                return (loss_sum + l_k, _jax.tree.map(_jnp.add, grad_sum, gw_k)), gx_k

        init = (_jnp.zeros((), _jnp.float32), _jax.tree.map(_jnp.zeros_like, weights))
        (loss, grad_w), grad_x = _jax.lax.scan(body, init, (per_example, given["loss_target"]))
    with _jax.named_scope("update"):
        delta_w, new_m, new_v = {}, {}, {}
        for n in TWIN_WEIGHTS:
            delta_w[n], new_m[n], new_v[n] = _adamw(weights[n], grad_w[n], given["m_" + n], given["v_" + n])
    return (loss, grad_x, *[grad_w[n] for n in TWIN_WEIGHTS], *[delta_w[n] for n in TWIN_WEIGHTS],
            *[new_m[n] for n in TWIN_WEIGHTS], *[new_v[n] for n in TWIN_WEIGHTS])
```

```python
import functools
import math

import jax
import jax.numpy as jnp
from jax import lax
from jax.experimental import pallas as pl
from jax.experimental.pallas import tpu as pltpu

F32 = jnp.float32
BF16 = jnp.bfloat16
MESH = pl.DeviceIdType.MESH

N_DEV = 8
N_CHIP = 4
GRID_W = 64
HEAD_DIM = 128
N_HEADS = 8
N_KV = 2
KV_GROUP = N_HEADS // N_KV
Q_W = N_HEADS * HEAD_DIM
KV_W = N_KV * HEAD_DIM
QKV_W = Q_W + 2 * KV_W
GC = 128
N_GROUP = 4
BR_W = N_GROUP * GC
POOL_WINDOWS = (2, 4, 8, 16)
OFF_POOL = QKV_W
OFF_U = OFF_POOL + BR_W
OFF_VG = OFF_U + BR_W
OFF_CB = OFF_VG + BR_W
OFF_CC = OFF_CB + BR_W
OFF_CX = OFF_CC + BR_W
IN_W = OFF_CX + BR_W
Q_BLOCK = 128
ROPE_THETA = 10000.0
LN_EPS = 1e-5
RMS_EPS = 1e-6
ADAM_LR = 0.001
ADAM_B1 = 0.9
ADAM_B2 = 0.999
ADAM_EPS = 1e-08
ADAM_WD = 0.01
ADAM_STEP = 10

ROW_TM = 256
LANE = 128
SUBLANE = 8
VMEM_BIG = 56 * 1024 * 1024
VMEM_MID = 40 * 1024 * 1024


def _cp(sem=None, vmem=VMEM_MID):
    return pltpu.CompilerParams(dimension_semantics=sem, vmem_limit_bytes=vmem)


def _tile(n, pref, unit=LANE, whole_ok=False):
    if n <= pref:
        return n
    best = None
    for t in range(unit, pref + 1, unit):
        if n % t == 0:
            best = t
    if whole_ok and (best is None or best * 4 < pref):
        return n
    assert best is not None, (n, pref, unit)
    return best


def _mm_nn(a, w, l, kind, *, name, out_dtype=F32, bias=None, act=None, tm_pref=768, tn_pref=1152):
    T, K = a.shape
    tm = _tile(T, tm_pref, SUBLANE * 2)
    if kind == "col":
        nq, _, kw, ns = w.shape
        assert kw == K
        tn = _tile(ns, tn_pref, whole_ok=True)
        nj = ns // tn
        n_total = nq * ns
        w_spec = pl.BlockSpec((None, None, K, tn), lambda j, i: (j // nj, l, 0, j % nj))
        grid_n = nq * nj
    else:
        nq, _, kc, n_total = w.shape
        assert nq * kc == K
        tn = _tile(n_total, tn_pref)
        w_spec = pl.BlockSpec((nq, None, kc, tn), lambda j, i: (0, l, 0, j))
        grid_n = n_total // tn

    def body(*refs):
        if bias is not None:
            a_ref, w_ref, b_ref, o_ref = refs
        else:
            a_ref, w_ref, o_ref = refs
        wv = w_ref[...]
        if kind == "row":
            wv = wv.reshape(K, tn)
        acc = jnp.dot(a_ref[...], wv, preferred_element_type=F32)
        if bias is not None:
            acc = acc + b_ref[...]
        if act is not None:
            acc = act(acc)
        o_ref[...] = acc.astype(out_dtype)

    in_specs = [pl.BlockSpec((tm, K), lambda j, i: (i, 0)), w_spec]
    args = [a, w]
    if bias is not None:
        in_specs.append(pl.BlockSpec((1, tn), lambda j, i: (0, j)))
        args.append(bias)
    return pl.pallas_call(
        body, grid=(grid_n, T // tm), in_specs=in_specs,
        out_specs=pl.BlockSpec((tm, tn), lambda j, i: (i, j)),
        out_shape=jax.ShapeDtypeStruct((T, n_total), out_dtype),
        name=name, compiler_params=_cp(("parallel", "parallel"), VMEM_BIG),
    )(*args)


def _mm_nt(dc, w, l, kind, *, name, out_dtype=F32, dc_off=0, tm_pref=768, tk_pref=1024):
    T = dc.shape[0]
    tm = _tile(T, tm_pref, SUBLANE * 2)
    if kind == "col":
        nq, _, K, ns = w.shape
        tk = _tile(K, tk_pref)
        assert dc_off % ns == 0
        offb = dc_off // ns

        def body(dc_ref, w_ref, o_ref, acc_ref):
            q = pl.program_id(2)
            part = lax.dot_general(dc_ref[...], w_ref[...], (((1,), (1,)), ((), ())),
                                   preferred_element_type=F32)

            @pl.when(q == 0)
            def _():
                acc_ref[...] = part

            @pl.when(q > 0)
            def _():
                acc_ref[...] += part

            @pl.when(q == nq - 1)
            def _():
                o_ref[...] = acc_ref[...].astype(out_dtype)

        return pl.pallas_call(
            body, grid=(K // tk, T // tm, nq),
            in_specs=[pl.BlockSpec((tm, ns), lambda kk, i, q: (i, offb + q)),
                      pl.BlockSpec((None, None, tk, ns), lambda kk, i, q: (q, l, kk, 0))],
            out_specs=pl.BlockSpec((tm, tk), lambda kk, i, q: (i, kk)),
            out_shape=jax.ShapeDtypeStruct((T, K), out_dtype),
            scratch_shapes=[pltpu.VMEM((tm, tk), F32)],
            name=name, compiler_params=_cp(("parallel", "parallel", "arbitrary"), VMEM_BIG),
        )(dc, w)

    nq, _, kc, n = w.shape
    assert dc_off % n == 0
    offb = dc_off // n
    tk = _tile(kc, tk_pref, whole_ok=True)
    nkk = kc // tk

    def body(dc_ref, w_ref, o_ref):
        o_ref[...] = lax.dot_general(dc_ref[...], w_ref[...], (((1,), (1,)), ((), ())),
                                     preferred_element_type=F32).astype(out_dtype)

    return pl.pallas_call(
        body, grid=(nq * nkk, T // tm),
        in_specs=[pl.BlockSpec((tm, n), lambda j, i: (i, offb)),
                  pl.BlockSpec((None, None, tk, n), lambda j, i: (j // nkk, l, j % nkk, 0))],
        out_specs=pl.BlockSpec((tm, tk), lambda j, i: (i, j)),
        out_shape=jax.ShapeDtypeStruct((T, nq * kc), out_dtype),
        name=name, compiler_params=_cp(("parallel", "parallel"), VMEM_BIG),
    )(dc, w)


def _mm_tn(a, dc, kind, *, nq, kdim, ndim, name, a_off=0, dc_off=0, tk_pref=512, tn_pref=1152):
    T = a.shape[0]
    assert dc.shape[0] == T
    tk = _tile(kdim, tk_pref, whole_ok=True)
    tn = _tile(ndim, tn_pref, whole_ok=True)
    nkk, njn = kdim // tk, ndim // tn
    assert a_off % tk == 0 and dc_off % tn == 0
    aoffb, doffb = a_off // tk, dc_off // tn
    if kind == "col":
        a_map = lambda q, kk, jn: (0, aoffb + kk)
        d_map = lambda q, kk, jn: (0, doffb + q * njn + jn)
    else:
        a_map = lambda q, kk, jn: (0, aoffb + q * nkk + kk)
        d_map = lambda q, kk, jn: (0, doffb + jn)

    def body(a_ref, d_ref, o_ref):
        o_ref[...] = lax.dot_general(a_ref[...], d_ref[...], (((0,), (0,)), ((), ())),
                                     preferred_element_type=F32)

    return pl.pallas_call(
        body, grid=(nq, nkk, njn),
        in_specs=[pl.BlockSpec((T, tk), a_map), pl.BlockSpec((T, tn), d_map)],
        out_specs=pl.BlockSpec((None, tk, tn), lambda q, kk, jn: (q, kk, jn)),
        out_shape=jax.ShapeDtypeStruct((nq, kdim, ndim), F32),
        name=name, compiler_params=_cp(("parallel", "parallel", "parallel"), VMEM_BIG),
    )(a, dc)


def _row_spec(d):
    return pl.BlockSpec((ROW_TM, d), lambda i: (i, 0))


def _mod_spec(d, ncb):
    return pl.BlockSpec((None, 1, d), lambda i: (jnp.where(i >= ncb, 1, 0), 0, 0))


def _vec_spec(d):
    return pl.BlockSpec((1, d), lambda i: (0, 0))


def _part_spec(d):
    return pl.BlockSpec((None, SUBLANE, d), lambda i: (i, 0, 0))


def _modulate(x, sc, sh, ncb, name):
    T, D = x.shape

    def body(x_ref, sc_ref, sh_ref, o_ref):
        o_ref[...] = (x_ref[...] * (1.0 + sc_ref[...]) + sh_ref[...]).astype(BF16)

    return pl.pallas_call(
        body, grid=(T // ROW_TM,),
        in_specs=[_row_spec(D), _mod_spec(D, ncb), _mod_spec(D, ncb)],
        out_specs=_row_spec(D), out_shape=jax.ShapeDtypeStruct((T, D), BF16),
        name=name, compiler_params=_cp(("parallel",)),
    )(x, sc, sh)


def _ln_stats(r):
    mu = jnp.mean(r, axis=-1, keepdims=True)
    rc = r - mu
    var = jnp.mean(rc * rc, axis=-1, keepdims=True)
    rstd = lax.rsqrt(var + LN_EPS)
    return rc * rstd, rstd


def _resid_ln(x, y, gate, g, b, sc, sh, ncb, alpha, name):
    T, D = x.shape

    def body(x_ref, y_ref, gate_ref, g_ref, b_ref, sc_ref, sh_ref, xo_ref, h_ref):
        xhat, _ = _ln_stats(alpha * x_ref[...] + gate_ref[...] * y_ref[...])
        xo = xhat * g_ref[...] + b_ref[...]
        xo_ref[...] = xo
        h_ref[...] = (xo * (1.0 + sc_ref[...]) + sh_ref[...]).astype(BF16)

    return pl.pallas_call(
        body, grid=(T // ROW_TM,),
        in_specs=[_row_spec(D), _row_spec(D), _mod_spec(D, ncb), _vec_spec(D), _vec_spec(D),
                  _mod_spec(D, ncb), _mod_spec(D, ncb)],
        out_specs=[_row_spec(D), _row_spec(D)],
        out_shape=[jax.ShapeDtypeStruct((T, D), F32), jax.ShapeDtypeStruct((T, D), BF16)],
        name=name, compiler_params=_cp(("parallel",)),
    )(x, y, gate, g, b, sc, sh)


def _resid_ln_loss(x, y, gate, g, b, target, ncb, alpha, name):
    T, D = x.shape
    nblk = T // ROW_TM

    def body(x_ref, y_ref, gate_ref, g_ref, b_ref, t_ref, dy_ref, loss_ref):
        i = pl.program_id(0)
        xhat, _ = _ln_stats(alpha * x_ref[...] + gate_ref[...] * y_ref[...])
        xo = xhat * g_ref[...] + b_ref[...]
        live = (i >= ncb).astype(F32)
        err = (xo - t_ref[...]) * live
        dy_ref[...] = err * (1.0 / D)
        loss_ref[...] = jnp.full((SUBLANE, LANE), 0.5 / D, F32) * jnp.sum(err * err)

    return pl.pallas_call(
        body, grid=(nblk,),
        in_specs=[_row_spec(D), _row_spec(D), _mod_spec(D, ncb), _vec_spec(D), _vec_spec(D),
                  pl.BlockSpec((ROW_TM, D), lambda i: (jnp.maximum(i - ncb, 0), 0))],
        out_specs=[_row_spec(D), _part_spec(LANE)],
        out_shape=[jax.ShapeDtypeStruct((T, D), F32), jax.ShapeDtypeStruct((nblk, SUBLANE, LANE), F32)],
        name=name, compiler_params=_cp(("parallel",)),
    )(x, y, gate, g, b, target)


def _write_parts(part_ref, rows, d):
    for k, r in enumerate(rows):
        part_ref[pl.ds(k, 1), :] = jnp.sum(r, axis=0, keepdims=True)
    if len(rows) < SUBLANE:
        part_ref[pl.ds(len(rows), SUBLANE - len(rows)), :] = jnp.zeros((SUBLANE - len(rows), d), F32)


def _ln_bwd(dxa, dhs, sc, x, y, gate, g, b, ncb, alpha, name):
    T, D = x.shape
    nblk = T // ROW_TM
    ndh = len(dhs)

    def body(*refs):
        dxa_ref = refs[0]
        dh_refs = refs[1:1 + ndh]
        sc_ref, x_ref, y_ref, gate_ref, g_ref, b_ref, dx_ref, dy_ref, part_ref = refs[1 + ndh:]
        yv = y_ref[...]
        xhat, rstd = _ln_stats(alpha * x_ref[...] + gate_ref[...] * yv)
        dxo = dxa_ref[...]
        rows = []
        if ndh:
            dh = dh_refs[0][...]
            for r in dh_refs[1:]:
                dh = dh + r[...]
            dxo = dxo + dh * (1.0 + sc_ref[...])
            xo = xhat * g_ref[...] + b_ref[...]
            rows = [dh * xo, dh]
        dxhat = dxo * g_ref[...]
        m1 = jnp.mean(dxhat, axis=-1, keepdims=True)
        m2 = jnp.mean(dxhat * xhat, axis=-1, keepdims=True)
        dr = rstd * (dxhat - m1 - xhat * m2)
        dx_ref[...] = alpha * dr
        dy_ref[...] = (gate_ref[...] * dr).astype(BF16)
        _write_parts(part_ref, [dxo * xhat, dxo, dr * yv] + rows, D)

    in_specs = ([_row_spec(D)] * (1 + ndh)
                + [_mod_spec(D, ncb), _row_spec(D), _row_spec(D), _mod_spec(D, ncb), _vec_spec(D), _vec_spec(D)])
    return pl.pallas_call(
        body, grid=(nblk,), in_specs=in_specs,
        out_specs=[_row_spec(D), _row_spec(D), _part_spec(D)],
        out_shape=[jax.ShapeDtypeStruct((T, D), F32), jax.ShapeDtypeStruct((T, D), BF16),
                   jax.ShapeDtypeStruct((nblk, SUBLANE, D), F32)],
        name=name, compiler_params=_cp(("parallel",)),
    )(dxa, *dhs, sc, x, y, gate, g, b)


def _mod_bwd(dxa, dhs, sc, x, ncb, name):
    T, D = x.shape
    nblk = T // ROW_TM
    ndh = len(dhs)

    def body(*refs):
        dxa_ref = refs[0]
        dh_refs = refs[1:1 + ndh]
        sc_ref, x_ref, dx_ref, part_ref = refs[1 + ndh:]
        dh = dh_refs[0][...]
        for r in dh_refs[1:]:
            dh = dh + r[...]
        dx_ref[...] = dxa_ref[...] + dh * (1.0 + sc_ref[...])
        _write_parts(part_ref, [dh * x_ref[...], dh], D)

    return pl.pallas_call(
        body, grid=(nblk,),
        in_specs=[_row_spec(D)] * (1 + ndh) + [_mod_spec(D, ncb), _row_spec(D)],
        out_specs=[_row_spec(D), _part_spec(D)],
        out_shape=[jax.ShapeDtypeStruct((T, D), F32), jax.ShapeDtypeStruct((nblk, SUBLANE, D), F32)],
        name=name, compiler_params=_cp(("parallel",)),
    )(dxa, *dhs, sc, x)


def _merge_fwd(gates, ts, name):
    T, D = ts[0].shape
    tm = 128

    def body(g_ref, t0, t1, t2, t3, o_ref):
        acc = g_ref[:, 0:D] * t0[...]
        for k, t in enumerate((t1, t2, t3), start=1):
            acc = acc + g_ref[:, k * D:(k + 1) * D] * t[...]
        o_ref[...] = acc.astype(BF16)

    rs = pl.BlockSpec((tm, D), lambda i: (i, 0))
    return pl.pallas_call(
        body, grid=(T // tm,),
        in_specs=[pl.BlockSpec((tm, 4 * D), lambda i: (i, 0)), rs, rs, rs, rs],
        out_specs=rs, out_shape=jax.ShapeDtypeStruct((T, D), BF16),
        name=name, compiler_params=_cp(("parallel",)),
    )(gates, *ts)


def _merge_bwd(dmg, gates, ts, name):
    T, D = dmg.shape
    tm = 128
    nblk = T // tm

    def body(d_ref, g_ref, t0, t1, t2, t3, dpre_ref, dt0, dt1, dt2, dt3, part_ref):
        d = d_ref[...]
        for k, (t, dt) in enumerate(zip((t0, t1, t2, t3), (dt0, dt1, dt2, dt3))):
            gk = g_ref[:, k * D:(k + 1) * D]
            dt[...] = (d * gk).astype(BF16)
            dpre = d * t[...] * gk * (1.0 - gk)
            dpre_ref[:, k * D:(k + 1) * D] = dpre.astype(BF16)
            part_ref[:, k * D:(k + 1) * D] = jnp.sum(dpre, axis=0, keepdims=True)

    rs = pl.BlockSpec((tm, D), lambda i: (i, 0))
    wide = pl.BlockSpec((tm, 4 * D), lambda i: (i, 0))
    return pl.pallas_call(
        body, grid=(nblk,),
        in_specs=[rs, wide, rs, rs, rs, rs],
        out_specs=[wide, rs, rs, rs, rs, pl.BlockSpec((None, 1, 4 * D), lambda i: (i, 0, 0))],
        out_shape=[jax.ShapeDtypeStruct((T, 4 * D), BF16)] + [jax.ShapeDtypeStruct((T, D), BF16)] * 4
                  + [jax.ShapeDtypeStruct((nblk, 1, 4 * D), F32)],
        name=name, compiler_params=_cp(("parallel",)),
    )(dmg, gates, *ts)


def _swiglu_fwd(gg, uu, name):
    T, F = gg.shape
    tm = 128

    def body(g_ref, u_ref, o_ref):
        g = g_ref[...]
        o_ref[...] = (g * jax.nn.sigmoid(g) * u_ref[...]).astype(BF16)

    rs = pl.BlockSpec((tm, F), lambda i: (i, 0))
    return pl.pallas_call(
        body, grid=(T // tm,), in_specs=[rs, rs], out_specs=rs,
        out_shape=jax.ShapeDtypeStruct((T, F), BF16),
        name=name, compiler_params=_cp(("parallel",)),
    )(gg, uu)


def _swiglu_bwd(da, gg, uu, name):
    T, F = gg.shape
    tm = 128

    def body(da_ref, g_ref, u_ref, dg_ref, du_ref):
        g = g_ref[...]
        da = da_ref[...]
        sig = jax.nn.sigmoid(g)
        dg_ref[...] = (da * u_ref[...] * sig * (1.0 + g * (1.0 - sig))).astype(BF16)
        du_ref[...] = (da * g * sig).astype(BF16)

    rs = pl.BlockSpec((tm, F), lambda i: (i, 0))
    return pl.pallas_call(
        body, grid=(T // tm,), in_specs=[rs, rs, rs], out_specs=[rs, rs],
        out_shape=[jax.ShapeDtypeStruct((T, F), BF16)] * 2,
        name=name, compiler_params=_cp(("parallel",)),
    )(da, gg, uu)


def _swap_halves(v):
    lane = lax.broadcasted_iota(jnp.int32, v.shape, 1)
    return jnp.where((lane % 64) < 32, pltpu.roll(v, 96, 1), pltpu.roll(v, 32, 1))


def _rope_tables(n, nc):
    rows = n // GRID_W
    row = jnp.repeat(jnp.arange(rows), GRID_W).astype(F32)
    col = jnp.tile(jnp.arange(GRID_W), rows).astype(F32)
    inv = ROPE_THETA ** (-jnp.arange(0, 64, 2, dtype=F32) / 64)
    ang_r = row[:, None] * inv
    ang_c = col[:, None] * inv
    cos = jnp.concatenate([jnp.cos(ang_r), jnp.cos(ang_r), jnp.cos(ang_c), jnp.cos(ang_c)], axis=-1)
    sin = jnp.concatenate([-jnp.sin(ang_r), jnp.sin(ang_r), -jnp.sin(ang_c), jnp.sin(ang_c)], axis=-1)
    cos = jnp.concatenate([jnp.ones((nc, HEAD_DIM), F32), cos], axis=0)
    sin = jnp.concatenate([jnp.zeros((nc, HEAD_DIM), F32), sin], axis=0)
    return cos, sin


def _qk_fwd(p, cos, sin, qg, kg, name):
    T = p.shape[0]

    def body(p_ref, c_ref, s_ref, qg_ref, kg_ref, q_ref, k_ref, v_ref):
        c, s = c_ref[...], s_ref[...]
        for h in range(N_HEADS + N_KV):
            x = p_ref[:, h * HEAD_DIM:(h + 1) * HEAD_DIM]
            rs = lax.rsqrt(jnp.mean(x * x, axis=-1, keepdims=True) + RMS_EPS)
            gain = qg_ref[...] if h < N_HEADS else kg_ref[...]
            yv = x * rs * gain
            out = (yv * c + _swap_halves(yv) * s).astype(BF16)
            if h < N_HEADS:
                q_ref[:, h * HEAD_DIM:(h + 1) * HEAD_DIM] = out
            else:
                k_ref[:, (h - N_HEADS) * HEAD_DIM:(h - N_HEADS + 1) * HEAD_DIM] = out
        v_ref[...] = p_ref[:, Q_W + KV_W:QKV_W].astype(BF16)

    return pl.pallas_call(
        body, grid=(T // ROW_TM,),
        in_specs=[_row_spec(QKV_W), _row_spec(HEAD_DIM), _row_spec(HEAD_DIM), _vec_spec(HEAD_DIM), _vec_spec(HEAD_DIM)],
        out_specs=[_row_spec(Q_W), _row_spec(KV_W), _row_spec(KV_W)],
        out_shape=[jax.ShapeDtypeStruct((T, Q_W), BF16), jax.ShapeDtypeStruct((T, KV_W), BF16),
                   jax.ShapeDtypeStruct((T, KV_W), BF16)],
        name=name, compiler_params=_cp(("parallel",)),
    )(p, cos, sin, qg, kg)


def _qk_bwd(p, dq, dk, dv, cos, sin, qg, kg, name):
    T = p.shape[0]
    nblk = T // ROW_TM

    def body(p_ref, dq_ref, dk_ref, dv_ref, c_ref, s_ref, qg_ref, kg_ref, dp_ref, part_ref):
        c, s = c_ref[...], s_ref[...]
        dgq = jnp.zeros((1, HEAD_DIM), F32)
        dgk = jnp.zeros((1, HEAD_DIM), F32)
        for h in range(N_HEADS + N_KV):
            x = p_ref[:, h * HEAD_DIM:(h + 1) * HEAD_DIM]
            if h < N_HEADS:
                d = dq_ref[:, h * HEAD_DIM:(h + 1) * HEAD_DIM]
                gain = qg_ref[...]
            else:
                d = dk_ref[:, (h - N_HEADS) * HEAD_DIM:(h - N_HEADS + 1) * HEAD_DIM]
                gain = kg_ref[...]
            dyv = d * c + _swap_halves(d * s)
            rs = lax.rsqrt(jnp.mean(x * x, axis=-1, keepdims=True) + RMS_EPS)
            xn = x * rs
            dgsum = jnp.sum(dyv * xn, axis=0, keepdims=True)
            if h < N_HEADS:
                dgq = dgq + dgsum
            else:
                dgk = dgk + dgsum
            dxg = dyv * gain
            dx = rs * (dxg - xn * jnp.mean(dxg * xn, axis=-1, keepdims=True))
            dp_ref[:, h * HEAD_DIM:(h + 1) * HEAD_DIM] = dx.astype(BF16)
        dp_ref[:, Q_W + KV_W:QKV_W] = dv_ref[...].astype(BF16)
        part_ref[pl.ds(0, 1), :] = dgq
        part_ref[pl.ds(1, 1), :] = dgk
        part_ref[pl.ds(2, SUBLANE - 2), :] = jnp.zeros((SUBLANE - 2, HEAD_DIM), F32)

    return pl.pallas_call(
        body, grid=(nblk,),
        in_specs=[_row_spec(QKV_W), _row_spec(Q_W), _row_spec(KV_W), _row_spec(KV_W),
                  _row_spec(HEAD_DIM), _row_spec(HEAD_DIM), _vec_spec(HEAD_DIM), _vec_spec(HEAD_DIM)],
        out_specs=[_row_spec(QKV_W), _part_spec(HEAD_DIM)],
        out_shape=[jax.ShapeDtypeStruct((T, QKV_W), BF16), jax.ShapeDtypeStruct((nblk, SUBLANE, HEAD_DIM), F32)],
        name=name, compiler_params=_cp(("parallel",)),
    )(p, dq, dk, dv, cos, sin, qg, kg)


def _attn_probs(q, k, qblk, nc, T):
    s = lax.dot_general(q, k, (((1,), (1,)), ((), ())), preferred_element_type=F32) * (HEAD_DIM ** -0.5)
    col = lax.broadcasted_iota(jnp.int32, s.shape, 1)
    limit = jnp.where(qblk < nc // Q_BLOCK, nc, T)
    s = jnp.where(col < limit, s, -1e30)
    e = jnp.exp(s - jnp.max(s, axis=-1, keepdims=True))
    return e / jnp.sum(e, axis=-1, keepdims=True)


def _attn_fwd(q, k, v, nc, name):
    T = q.shape[0]

    def body(q_ref, k_ref, v_ref, o_ref):
        i = pl.program_id(0)
        for h in range(N_HEADS):
            g = h // KV_GROUP
            kk = k_ref[:, g * HEAD_DIM:(g + 1) * HEAD_DIM]
            vv = v_ref[:, g * HEAD_DIM:(g + 1) * HEAD_DIM]
            pr = _attn_probs(q_ref[:, h * HEAD_DIM:(h + 1) * HEAD_DIM], kk, i, nc, T)
            o = jnp.dot(pr.astype(BF16), vv, preferred_element_type=F32)
            o_ref[:, h * HEAD_DIM:(h + 1) * HEAD_DIM] = o.astype(BF16)

    whole = pl.BlockSpec((T, KV_W), lambda i: (0, 0))
    qs = pl.BlockSpec((Q_BLOCK, Q_W), lambda i: (i, 0))
    return pl.pallas_call(
        body, grid=(T // Q_BLOCK,), in_specs=[qs, whole, whole], out_specs=qs,
        out_shape=jax.ShapeDtypeStruct((T, Q_W), BF16),
        name=name, compiler_params=_cp(("parallel",)),
    )(q, k, v)


def _attn_bwd(q, k, v, do, nc, name):
    T = q.shape[0]
    scale = HEAD_DIM ** -0.5

    def body(q_ref, k_ref, v_ref, do_ref, dq_ref, dk_ref, dv_ref):
        i = pl.program_id(0)

        @pl.when(i == 0)
        def _():
            dk_ref[...] = jnp.zeros_like(dk_ref)
            dv_ref[...] = jnp.zeros_like(dv_ref)

        for h in range(N_HEADS):
            g = h // KV_GROUP
            cols = slice(g * HEAD_DIM, (g + 1) * HEAD_DIM)
            hc = slice(h * HEAD_DIM, (h + 1) * HEAD_DIM)
            qh, kk, vv, doh = q_ref[:, hc], k_ref[:, cols], v_ref[:, cols], do_ref[:, hc]
            pr = _attn_probs(qh, kk, i, nc, T)
            dpr = lax.dot_general(doh, vv, (((1,), (1,)), ((), ())), preferred_element_type=F32)
            ds = (pr * (dpr - jnp.sum(pr * dpr, axis=-1, keepdims=True)) * scale).astype(BF16)
            dq_ref[:, hc] = jnp.dot(ds, kk, preferred_element_type=F32)
            dk_ref[:, cols] += lax.dot_general(ds, qh, (((0,), (0,)), ((), ())), preferred_element_type=F32)
            dv_ref[:, cols] += lax.dot_general(pr.astype(BF16), doh, (((0,), (0,)), ((), ())),
                                               preferred_element_type=F32)

    whole = pl.BlockSpec((T, KV_W), lambda i: (0, 0))
    qs = pl.BlockSpec((Q_BLOCK, Q_W), lambda i: (i, 0))
    return pl.pallas_call(
        body, grid=(T // Q_BLOCK,), in_specs=[qs, whole, whole, qs], out_specs=[qs, whole, whole],
        out_shape=[jax.ShapeDtypeStruct((T, Q_W), F32), jax.ShapeDtypeStruct((T, KV_W), F32),
                   jax.ShapeDtypeStruct((T, KV_W), F32)],
        name=name, compiler_params=_cp(("arbitrary",)),
    )(q, k, v, do)


def _shift_rows(z, o, nc):
    T = z.shape[0]
    t = lax.broadcasted_iota(jnp.int32, (T, 1), 0)
    lo = jnp.where(t < nc, 0, nc)
    hi = jnp.where(t < nc, nc, T)
    ok = jnp.logical_and(t + o >= lo, t + o < hi)
    rolled = z if o == 0 else pltpu.roll(z, (-o) % T, 0)
    return jnp.where(ok, rolled, 0.0), ok


def _pool_window(w):
    left = w // 2
    return -left, w - 1 - left


def _pool_d(z, w, nc):
    o0, o1 = _pool_window(w)
    tot = jnp.zeros_like(z)
    cnt = jnp.zeros((z.shape[0], 1), F32)
    for o in range(o0, o1 + 1):
        sh, ok = _shift_rows(z, o, nc)
        tot = tot + sh
        cnt = cnt + ok.astype(F32)
    return tot / cnt - z, cnt


def _pool_fwd(p, pool_w, pool_scale, nc, name):
    T = p.shape[0]

    def body(z_ref, w_ref, s_ref, o_ref):
        for g, w in enumerate(POOL_WINDOWS):
            cs = slice(g * GC, (g + 1) * GC)
            d, _ = _pool_d(z_ref[:, cs], w, nc)
            yv = jnp.dot(d.astype(BF16), w_ref[g].astype(BF16), preferred_element_type=F32)
            o_ref[:, cs] = (yv * s_ref[:, cs]).astype(BF16)

    return pl.pallas_call(
        body, grid=(1,),
        in_specs=[pl.BlockSpec((T, BR_W), lambda i: (0, OFF_POOL // BR_W)),
                  pl.BlockSpec((N_GROUP, GC, GC), lambda i: (0, 0, 0)), _vec_spec(BR_W)],
        out_specs=pl.BlockSpec((T, BR_W), lambda i: (0, 0)),
        out_shape=jax.ShapeDtypeStruct((T, BR_W), BF16),
        name=name, compiler_params=_cp(("arbitrary",), VMEM_BIG),
    )(p, pool_w, pool_scale)


def _pool_bwd(p, dy, pool_w, pool_scale, nc, name):
    T = p.shape[0]

    def body(z_ref, dy_ref, w_ref, s_ref, dz_ref, dw_ref, ds_ref):
        for g, w in enumerate(POOL_WINDOWS):
            cs = slice(g * GC, (g + 1) * GC)
            d, cnt = _pool_d(z_ref[:, cs], w, nc)
            db = d.astype(BF16)
            wb = w_ref[g].astype(BF16)
            dyv = dy_ref[:, cs]
            e = (dyv * s_ref[:, cs]).astype(BF16)
            dw_ref[g] = lax.dot_general(db, e, (((0,), (0,)), ((), ())), preferred_element_type=F32)
            yraw = jnp.dot(db, wb, preferred_element_type=F32)
            ds_ref[:, cs] = jnp.sum(dyv * yraw, axis=0, keepdims=True)
            dd = lax.dot_general(e, wb, (((1,), (1,)), ((), ())), preferred_element_type=F32)
            ec = dd / cnt
            o0, o1 = _pool_window(w)
            tot = jnp.zeros_like(dd)
            for o in range(o0, o1 + 1):
                tot = tot + _shift_rows(ec, -o, nc)[0]
            dz_ref[:, cs] = (tot - dd).astype(BF16)

    return pl.pallas_call(
        body, grid=(1,),
        in_specs=[pl.BlockSpec((T, BR_W), lambda i: (0, OFF_POOL // BR_W)),
                  pl.BlockSpec((T, BR_W), lambda i: (0, 0)),
                  pl.BlockSpec((N_GROUP, GC, GC), lambda i: (0, 0, 0)), _vec_spec(BR_W)],
        out_specs=[pl.BlockSpec((T, BR_W), lambda i: (0, 0)),
                   pl.BlockSpec((N_GROUP, GC, GC), lambda i: (0, 0, 0)), _vec_spec(BR_W)],
        out_shape=[jax.ShapeDtypeStruct((T, BR_W), BF16), jax.ShapeDtypeStruct((N_GROUP, GC, GC), F32),
                   jax.ShapeDtypeStruct((1, BR_W), F32)],
        name=name, compiler_params=_cp(("arbitrary",), VMEM_BIG),
    )(p, dy, pool_w, pool_scale)


_GELU_C = math.sqrt(2.0 / math.pi)


def _gelu(x):
    return 0.5 * x * (1.0 + jnp.tanh(_GELU_C * (x + 0.044715 * x * x * x)))


def _gelu_grad(x):
    th = jnp.tanh(_GELU_C * (x + 0.044715 * x * x * x))
    return 0.5 * (1.0 + th) + 0.5 * x * (1.0 - th * th) * _GELU_C * (1.0 + 3.0 * 0.044715 * x * x)


def _sgu_fwd(p, ln_g, ln_b, w_s, b_st, name):
    T = p.shape[0]

    def body(pu_ref, pv_ref, g_ref, b_ref, w_ref, bs_ref, o_ref):
        u = _gelu(pu_ref[...])
        vhat, _ = _ln_stats(_gelu(pv_ref[...]))
        vn = (vhat * g_ref[...] + b_ref[...]).astype(BF16)
        for g in range(N_GROUP):
            cs = slice(g * GC, (g + 1) * GC)
            s = jnp.dot(w_ref[g].astype(BF16), vn[:, cs], preferred_element_type=F32) + bs_ref[:, g:g + 1]
            o_ref[:, cs] = (u[:, cs] * s).astype(BF16)

    chunk = lambda off: pl.BlockSpec((GC, BR_W), lambda i: (i, off // BR_W))
    return pl.pallas_call(
        body, grid=(T // GC,),
        in_specs=[chunk(OFF_U), chunk(OFF_VG), _vec_spec(BR_W), _vec_spec(BR_W),
                  pl.BlockSpec((N_GROUP, GC, GC), lambda i: (0, 0, 0)),
                  pl.BlockSpec((GC, N_GROUP), lambda i: (0, 0))],
        out_specs=pl.BlockSpec((GC, BR_W), lambda i: (i, 0)),
        out_shape=jax.ShapeDtypeStruct((T, BR_W), BF16),
        name=name, compiler_params=_cp(("parallel",)),
    )(p, p, ln_g, ln_b, w_s, b_st)


def _sgu_bwd(p, dy, ln_g, ln_b, w_s, b_st, name):
    T = p.shape[0]

    def body(pu_ref, pv_ref, dy_ref, g_ref, b_ref, w_ref, bs_ref, dp_ref, dw_ref, dsacc_ref, dln_ref):
        i = pl.program_id(0)

        @pl.when(i == 0)
        def _():
            dw_ref[...] = jnp.zeros_like(dw_ref)
            dsacc_ref[...] = jnp.zeros_like(dsacc_ref)
            dln_ref[...] = jnp.zeros_like(dln_ref)

        pu, pv, dyv = pu_ref[...], pv_ref[...], dy_ref[...]
        u = _gelu(pu)
        vhat, rstd = _ln_stats(_gelu(pv))
        vn = (vhat * g_ref[...] + b_ref[...]).astype(BF16)
        ds = dyv * u
        dsb = ds.astype(BF16)
        dsacc_ref[...] += ds
        dvn_parts = []
        for g in range(N_GROUP):
            cs = slice(g * GC, (g + 1) * GC)
            wb = w_ref[g].astype(BF16)
            s = jnp.dot(wb, vn[:, cs], preferred_element_type=F32) + bs_ref[:, g:g + 1]
            dp_ref[:, cs] = (dyv[:, cs] * s * _gelu_grad(pu[:, cs])).astype(BF16)
            dvn_parts.append(lax.dot_general(wb, dsb[:, cs], (((0,), (0,)), ((), ())),
                                             preferred_element_type=F32))
            dw_ref[g] += lax.dot_general(dsb[:, cs], vn[:, cs], (((1,), (1,)), ((), ())),
                                         preferred_element_type=F32)
        dvn = jnp.concatenate(dvn_parts, axis=-1)
        dln_ref[pl.ds(0, 1), :] += jnp.sum(dvn * vhat, axis=0, keepdims=True)
        dln_ref[pl.ds(1, 1), :] += jnp.sum(dvn, axis=0, keepdims=True)
        dvhat = dvn * g_ref[...]
        m1 = jnp.mean(dvhat, axis=-1, keepdims=True)
        m2 = jnp.mean(dvhat * vhat, axis=-1, keepdims=True)
        dv = rstd * (dvhat - m1 - vhat * m2)
        dp_ref[:, BR_W:2 * BR_W] = (dv * _gelu_grad(pv)).astype(BF16)

    chunk = lambda off: pl.BlockSpec((GC, BR_W), lambda i: (i, off // BR_W))
    return pl.pallas_call(
        body, grid=(T // GC,),
        in_specs=[chunk(OFF_U), chunk(OFF_VG), pl.BlockSpec((GC, BR_W), lambda i: (i, 0)),
                  _vec_spec(BR_W), _vec_spec(BR_W),
                  pl.BlockSpec((N_GROUP, GC, GC), lambda i: (0, 0, 0)),
                  pl.BlockSpec((GC, N_GROUP), lambda i: (0, 0))],
        out_specs=[pl.BlockSpec((GC, 2 * BR_W), lambda i: (i, 0)),
                   pl.BlockSpec((N_GROUP, GC, GC), lambda i: (0, 0, 0)),
                   pl.BlockSpec((GC, BR_W), lambda i: (0, 0)),
                   pl.BlockSpec((SUBLANE, BR_W), lambda i: (0, 0))],
        out_shape=[jax.ShapeDtypeStruct((T, 2 * BR_W), BF16), jax.ShapeDtypeStruct((N_GROUP, GC, GC), F32),
                   jax.ShapeDtypeStruct((GC, BR_W), F32), jax.ShapeDtypeStruct((SUBLANE, BR_W), F32)],
        name=name, compiler_params=_cp(("arbitrary",)),
    )(p, p, dy, ln_g, ln_b, w_s, b_st)


def _conv_fwd(p, conv_w, nc, name):
    T = p.shape[0]

    def body(cb_ref, cc_ref, cx_ref, w_ref, o_ref):
        zz = cc_ref[...] * cx_ref[...]
        conv = (w_ref[0:1, :] * _shift_rows(zz, -1, nc)[0] + w_ref[1:2, :] * zz
                + w_ref[2:3, :] * _shift_rows(zz, 1, nc)[0])
        o_ref[...] = (cb_ref[...] * conv).astype(BF16)

    col = lambda off: pl.BlockSpec((T, GC), lambda j: (0, off // GC + j))
    return pl.pallas_call(
        body, grid=(N_GROUP,),
        in_specs=[col(OFF_CB), col(OFF_CC), col(OFF_CX), pl.BlockSpec((3, GC), lambda j: (0, j))],
        out_specs=pl.BlockSpec((T, GC), lambda j: (0, j)),
        out_shape=jax.ShapeDtypeStruct((T, BR_W), BF16),
        name=name, compiler_params=_cp(("parallel",)),
    )(p, p, p, conv_w)


def _conv_bwd(p, dy, conv_w, nc, name):
    T = p.shape[0]

    def body(cb_ref, cc_ref, cx_ref, dy_ref, w_ref, dp_ref, dw_ref):
        cc, cx, dyv = cc_ref[...], cx_ref[...], dy_ref[...]
        zz = cc * cx
        zm, zp = _shift_rows(zz, -1, nc)[0], _shift_rows(zz, 1, nc)[0]
        conv = w_ref[0:1, :] * zm + w_ref[1:2, :] * zz + w_ref[2:3, :] * zp
        dconv = dyv * cb_ref[...]
        dw_ref[0:1, :] = jnp.sum(dconv * zm, axis=0, keepdims=True)
        dw_ref[1:2, :] = jnp.sum(dconv * zz, axis=0, keepdims=True)
        dw_ref[2:3, :] = jnp.sum(dconv * zp, axis=0, keepdims=True)
        dzz = (w_ref[0:1, :] * _shift_rows(dconv, 1, nc)[0] + w_ref[1:2, :] * dconv
               + w_ref[2:3, :] * _shift_rows(dconv, -1, nc)[0])
        dp_ref[0] = (dyv * conv).astype(BF16)
        dp_ref[1] = (dzz * cx).astype(BF16)
        dp_ref[2] = (dzz * cc).astype(BF16)

    col = lambda off: pl.BlockSpec((T, GC), lambda j: (0, off // GC + j))
    return pl.pallas_call(
        body, grid=(N_GROUP,),
        in_specs=[col(OFF_CB), col(OFF_CC), col(OFF_CX), pl.BlockSpec((T, GC), lambda j: (0, j)),
                  pl.BlockSpec((3, GC), lambda j: (0, j))],
        out_specs=[pl.BlockSpec((3, T, GC), lambda j: (0, 0, j)), pl.BlockSpec((3, GC), lambda j: (0, j))],
        out_shape=[jax.ShapeDtypeStruct((3, T, BR_W), BF16), jax.ShapeDtypeStruct((3, BR_W), F32)],
        name=name, compiler_params=_cp(("parallel",)),
    )(p, p, p, dy, conv_w)


def _rows_tile(rows, cols, n_arrays):
    budget = 24 * 1024 * 1024 // (2 * 4 * n_arrays * cols)
    return _tile(rows, max(SUBLANE * 2, min(budget, 1024)), SUBLANE * 2) if rows % (SUBLANE * 2) == 0 else rows


def _cast_bf16(w, name):
    shape = w.shape
    cols = shape[-1]
    w2 = w.reshape(-1, cols)
    rows = w2.shape[0]
    tr = _rows_tile(rows, cols, 2)

    def body(w_ref, o_ref):
        o_ref[...] = w_ref[...].astype(BF16)

    rs = pl.BlockSpec((tr, cols), lambda i: (i, 0))
    out = pl.pallas_call(
        body, grid=(rows // tr,), in_specs=[rs], out_specs=rs,
        out_shape=jax.ShapeDtypeStruct((rows, cols), BF16),
        name=name, compiler_params=_cp(("parallel",)),
    )(w2)
    return out.reshape(shape)


def _adamw(w, g, m, v, name):
    shape = w.shape
    cols = shape[-1]
    rows = w.size // cols
    tr = _rows_tile(rows, cols, 7)
    c1 = 1.0 - ADAM_B1 ** ADAM_STEP
    c2 = 1.0 - ADAM_B2 ** ADAM_STEP

    def body(w_ref, g_ref, m_ref, v_ref, d_ref, mo_ref, vo_ref):
        gv = g_ref[...]
        mn = ADAM_B1 * m_ref[...] + (1.0 - ADAM_B1) * gv
        vn = ADAM_B2 * v_ref[...] + (1.0 - ADAM_B2) * (gv * gv)
        mo_ref[...] = mn
        vo_ref[...] = vn
        d_ref[...] = -ADAM_LR * ((mn / c1) / (jnp.sqrt(vn / c2) + ADAM_EPS) + ADAM_WD * w_ref[...])

    rs = pl.BlockSpec((tr, cols), lambda i: (i, 0))
    outs = pl.pallas_call(
        body, grid=(rows // tr,), in_specs=[rs] * 4, out_specs=[rs] * 3,
        out_shape=[jax.ShapeDtypeStruct((rows, cols), F32)] * 3,
        name=name, compiler_params=_cp(("parallel",)),
    )(*[a.reshape(rows, cols) for a in (w, g, m, v)])
    return [o.reshape(shape) for o in outs]


def _pair_sum(core, dw0, dw1, recv, name):
    shape = recv.shape
    cols = shape[-1]
    rows = recv.size // cols
    tr = _rows_tile(rows, cols, 4)

    def body(c_ref, a0_ref, a1_ref, r_ref, o_ref):
        mine = jnp.where(c_ref[0] == 0, a0_ref[...], a1_ref[...])
        o_ref[...] = (mine + r_ref[...]).astype(BF16)

    rs = pl.BlockSpec((tr, cols), lambda i, c: (i, 0))
    out = pl.pallas_call(
        body,
        grid_spec=pltpu.PrefetchScalarGridSpec(num_scalar_prefetch=1, grid=(rows // tr,),
                                               in_specs=[rs, rs, rs], out_specs=rs),
        out_shape=jax.ShapeDtypeStruct((rows, cols), BF16),
        name=name, compiler_params=_cp(("parallel",)),
    )(core, dw0.reshape(rows, cols), dw1.reshape(rows, cols), recv.reshape(rows, cols))
    return out.reshape(shape)


def _chip_sum(chip, psum, recv, name):
    _, kc, cols = psum.shape
    tr = _rows_tile(kc, cols, 5)

    def body(q_ref, p_ref, r_ref, o_ref):
        o_ref[...] = (p_ref[...].astype(F32) + r_ref[0].astype(F32) + r_ref[1].astype(F32)
                      + r_ref[2].astype(F32))

    return pl.pallas_call(
        body,
        grid_spec=pltpu.PrefetchScalarGridSpec(
            num_scalar_prefetch=1, grid=(kc // tr,),
            in_specs=[pl.BlockSpec((None, tr, cols), lambda i, q: (q[0], i, 0)),
                      pl.BlockSpec((3, tr, cols), lambda i, q: (0, i, 0))],
            out_specs=pl.BlockSpec((tr, cols), lambda i, q: (i, 0))),
        out_shape=jax.ShapeDtypeStruct((kc, cols), F32),
        name=name, compiler_params=_cp(("parallel",)),
    )(chip, psum, recv)


def _slab_sum(gathered, name):
    n, rows, cols = gathered.shape
    tr = _tile(rows, 512, SUBLANE)

    def body(g_ref, o_ref):
        acc = g_ref[0]
        for d in range(1, n):
            acc = acc + g_ref[d]
        o_ref[...] = acc

    return pl.pallas_call(
        body, grid=(rows // tr,),
        in_specs=[pl.BlockSpec((n, tr, cols), lambda i: (0, i, 0))],
        out_specs=pl.BlockSpec((tr, cols), lambda i: (i, 0)),
        out_shape=jax.ShapeDtypeStruct((rows, cols), F32),
        name=name, compiler_params=_cp(("parallel",)),
    )(gathered)


def _silu(x):
    return x * jax.nn.sigmoid(x)


def _ada_fwd(cvec, w_ada, b_cols, name):
    L, D, ns = w_ada.shape
    tn = _tile(ns, 768)

    def body(c_ref, w_ref, b_ref, o_ref):
        s = _silu(c_ref[...]).astype(BF16)
        o_ref[...] = jnp.dot(s, w_ref[...].astype(BF16), preferred_element_type=F32) + b_ref[...]

    return pl.pallas_call(
        body, grid=(L, ns // tn),
        in_specs=[pl.BlockSpec((16, D), lambda l, j: (0, 0)),
                  pl.BlockSpec((None, D, tn), lambda l, j: (l, 0, j)),
                  pl.BlockSpec((None, 1, tn), lambda l, j: (l, 0, j))],
        out_specs=pl.BlockSpec((None, 16, tn), lambda l, j: (l, 0, j)),
        out_shape=jax.ShapeDtypeStruct((L, 16, ns), F32),
        name=name, compiler_params=_cp(("parallel", "parallel")),
    )(cvec, w_ada, b_cols)


def _ada_bwd(cvec, dmod, w_ada, name):
    L, D, ns = w_ada.shape
    tn = _tile(ns, 768)

    def body(c_ref, d_ref, w_ref, gw_ref, ds_ref):
        first = jnp.logical_and(pl.program_id(0) == 0, pl.program_id(1) == 0)

        @pl.when(first)
        def _():
            ds_ref[...] = jnp.zeros_like(ds_ref)

        s = _silu(c_ref[...]).astype(BF16)
        db = d_ref[...].astype(BF16)
        gw_ref[...] = lax.dot_general(s, db, (((0,), (0,)), ((), ())), preferred_element_type=F32)
        ds_ref[...] += lax.dot_general(db, w_ref[...].astype(BF16), (((1,), (1,)), ((), ())),
                                       preferred_element_type=F32)

    return pl.pallas_call(
        body, grid=(L, ns // tn),
        in_specs=[pl.BlockSpec((16, D), lambda l, j: (0, 0)),
                  pl.BlockSpec((None, 16, tn), lambda l, j: (l, 0, j)),
                  pl.BlockSpec((None, D, tn), lambda l, j: (l, 0, j))],
        out_specs=[pl.BlockSpec((None, D, tn), lambda l, j: (l, 0, j)),
                   pl.BlockSpec((16, D), lambda l, j: (0, 0))],
        out_shape=[jax.ShapeDtypeStruct((L, D, ns), F32), jax.ShapeDtypeStruct((16, D), F32)],
        name=name, compiler_params=_cp(("arbitrary", "arbitrary")),
    )(cvec, dmod, w_ada)


def _cctx_grad(gathered, c_ctx, name):
    rows = c_ctx.shape[0]

    def body(g_ref, c_ref, o_ref):
        ds = g_ref[0] + g_ref[2] + g_ref[4] + g_ref[6]
        cv = c_ref[...]
        sig = jax.nn.sigmoid(cv)
        o_ref[...] = ds * sig * (1.0 + cv * (1.0 - sig))

    return pl.pallas_call(
        body, out_shape=jax.ShapeDtypeStruct((rows, LANE), F32), name=name,
    )(gathered, c_ctx)


def _place():
    x, y, c = lax.axis_index("x"), lax.axis_index("y"), lax.axis_index("c")
    chips = [(1 - x, y), (x, 1 - y), (1 - x, 1 - y)]
    return x, y, c, chips


def _all_gather_small(slab, name):
    rows, cols = slab.shape

    def body(x_ref, out_ref, send_sems, recv_sems, local_sem):
        x, y, c, chips = _place()
        me, sibling = (x, y, c), (x, y, 1 - c)

        def blk(px, py, pc):
            return out_ref.at[4 * px + 2 * py + pc]

        def copy(k, block, to, src=None):
            return pltpu.make_async_remote_copy(
                src_ref=blk(*block) if src is None else src, dst_ref=blk(*block),
                send_sem=send_sems.at[k], recv_sem=recv_sems.at[k], device_id=to, device_id_type=MESH)

        mine = pltpu.make_async_copy(x_ref, blk(*me), local_sem)
        mine.start()
        first = [copy(0, me, sibling, src=x_ref)]
        first += [copy(1 + j, me, (*chip, c), src=x_ref) for j, chip in enumerate(chips)]
        for cp in first:
            cp.start()
        passed = [copy(4 + j, (*chip, c), sibling) for j, chip in enumerate(chips)]
        for j, chip in enumerate(chips):
            copy(1 + j, (*chip, c), me).wait_recv()
            passed[j].start()
        copy(0, sibling, me).wait_recv()
        for j, chip in enumerate(chips):
            copy(4 + j, (*chip, 1 - c), me).wait_recv()
        for cp in first + passed:
            cp.wait_send()
        mine.wait()

    return pl.pallas_call(
        body, out_shape=jax.ShapeDtypeStruct((N_DEV, rows, cols), slab.dtype),
        in_specs=[pl.BlockSpec(memory_space=pltpu.VMEM)],
        out_specs=pl.BlockSpec(memory_space=pltpu.VMEM),
        scratch_shapes=[pltpu.SemaphoreType.DMA((7,)), pltpu.SemaphoreType.DMA((7,)), pltpu.SemaphoreType.DMA],
        name=name, compiler_params=pltpu.CompilerParams(vmem_limit_bytes=VMEM_MID),
    )(slab)


_ANY = pl.BlockSpec(memory_space=pl.ANY)


def _gather_weights(shards, name):
    n = len(shards)

    def body(*refs):
        ins, outs = refs[:n], refs[n:2 * n]
        send_sems, recv_sems, local_sems = refs[2 * n:]
        x, y, c, chips = _place()
        mine_q = 2 * x + y
        sibling = (x, y, 1 - c)
        local, sends = [], []
        for a in range(n):
            for l in range(2):
                cp = pltpu.make_async_copy(ins[a].at[l], outs[a].at[mine_q, l], local_sems.at[a, l])
                cp.start()
                local.append(cp)

        def remote(a, k, q, layer, to, src=None):
            dst = outs[a].at[q, layer]
            return pltpu.make_async_remote_copy(
                src_ref=dst if src is None else src, dst_ref=dst,
                send_sem=send_sems.at[a, k], recv_sem=recv_sems.at[a, k], device_id=to, device_id_type=MESH)

        for a in range(n):
            for j, chip in enumerate(chips):
                cp = remote(a, j, mine_q, c, (*chip, c), src=ins[a].at[c])
                cp.start()
                sends.append(cp)
        for a in range(n):
            for j, (cx, cy) in enumerate(chips):
                q = 2 * cx + cy
                remote(a, j, q, c, (x, y, c)).wait_recv()
                cp = remote(a, 3 + j, q, c, sibling)
                cp.start()
                sends.append(cp)
        for a in range(n):
            for j, (cx, cy) in enumerate(chips):
                remote(a, 3 + j, 2 * cx + cy, 1 - c, (x, y, c)).wait_recv()
        for cp in sends:
            cp.wait_send()
        for cp in local:
            cp.wait()

    return pl.pallas_call(
        body,
        out_shape=[jax.ShapeDtypeStruct((N_CHIP,) + s.shape, s.dtype) for s in shards],
        in_specs=[_ANY] * n, out_specs=[_ANY] * n,
        scratch_shapes=[pltpu.SemaphoreType.DMA((n, 6)), pltpu.SemaphoreType.DMA((n, 6)),
                        pltpu.SemaphoreType.DMA((n, 2))],
        name=name,
    )(*shards)


def _swap_layers(dws0, dws1, name):
    n = len(dws0)

    def body(*refs):
        a0, a1, outs = refs[:n], refs[n:2 * n], refs[2 * n:3 * n]
        send_sems, recv_sems = refs[3 * n:]
        x, y, c, _ = _place()
        sibling = (x, y, 1 - c)

        def copy(a, src):
            return pltpu.make_async_remote_copy(
                src_ref=src, dst_ref=outs[a], send_sem=send_sems.at[a], recv_sem=recv_sems.at[a],
                device_id=sibling, device_id_type=MESH)

        @pl.when(c == 0)
        def _():
            for a in range(n):
                copy(a, a1[a]).start()

        @pl.when(c == 1)
        def _():
            for a in range(n):
                copy(a, a0[a]).start()

        for a in range(n):
            copy(a, a0[a]).wait()

    return pl.pallas_call(
        body, out_shape=[jax.ShapeDtypeStruct(d.shape, d.dtype) for d in dws0],
        in_specs=[_ANY] * (2 * n), out_specs=[_ANY] * n,
        scratch_shapes=[pltpu.SemaphoreType.DMA((n,)), pltpu.SemaphoreType.DMA((n,))],
        name=name,
    )(*dws0, *dws1)


def _scatter_chunks(psums, name):
    n = len(psums)

    def body(*refs):
        ins, outs = refs[:n], refs[n:2 * n]
        send_sems, recv_sems = refs[2 * n:]
        x, y, c, chips = _place()
        sends = []
        for a in range(n):
            for j, (cx, cy) in enumerate(chips):
                cp = pltpu.make_async_remote_copy(
                    src_ref=ins[a].at[2 * cx + cy], dst_ref=outs[a].at[j],
                    send_sem=send_sems.at[a, j], recv_sem=recv_sems.at[a, j],
                    device_id=(cx, cy, c), device_id_type=MESH)
                cp.start()
                sends.append(cp)
        for cp in sends:
            cp.wait()

    return pl.pallas_call(
        body, out_shape=[jax.ShapeDtypeStruct((3,) + p.shape[1:], p.dtype) for p in psums],
        in_specs=[_ANY] * n, out_specs=[_ANY] * n,
        scratch_shapes=[pltpu.SemaphoreType.DMA((n, 3)), pltpu.SemaphoreType.DMA((n, 3))],
        name=name,
    )(*psums)


def _swap_grads(grads, name):
    n = len(grads)

    def body(*refs):
        ins, outs = refs[:n], refs[n:2 * n]
        send_sems, recv_sems, local_sems = refs[2 * n:]
        x, y, c, _ = _place()
        cps = []
        for a in range(n):
            loc = pltpu.make_async_copy(ins[a], outs[a].at[c], local_sems.at[a])
            loc.start()
            cp = pltpu.make_async_remote_copy(
                src_ref=ins[a], dst_ref=outs[a].at[c], send_sem=send_sems.at[a], recv_sem=recv_sems.at[a],
                device_id=(x, y, 1 - c), device_id_type=MESH)
            cp.start()
            cps.append((loc, cp))
        for a, (loc, cp) in enumerate(cps):
            cp.wait_send()
            pltpu.make_async_remote_copy(
                src_ref=ins[a], dst_ref=outs[a].at[1 - c], send_sem=send_sems.at[a], recv_sem=recv_sems.at[a],
                device_id=(x, y, 1 - c), device_id_type=MESH).wait_recv()
            loc.wait()

    return pl.pallas_call(
        body, out_shape=[jax.ShapeDtypeStruct((2,) + g.shape, g.dtype) for g in grads],
        in_specs=[_ANY] * n, out_specs=[_ANY] * n,
        scratch_shapes=[pltpu.SemaphoreType.DMA((n,)), pltpu.SemaphoreType.DMA((n,)),
                        pltpu.SemaphoreType.DMA((n,))],
        name=name,
    )(*grads)


def _pack(arrs):
    flat = jnp.concatenate([a.reshape(-1).astype(F32) for a in arrs])
    pad = (-flat.shape[0]) % (SUBLANE * LANE)
    return jnp.pad(flat, (0, pad)).reshape(-1, LANE)


def _unpack(slab, shapes):
    flat = slab.reshape(-1)
    out, off = [], 0
    for s in shapes:
        n = math.prod(s)
        out.append(flat[off:off + n].reshape(s))
        off += n
    return out


def _split6(v):
    d = v.shape[-1] // 6
    return [v[:, k * d:(k + 1) * d].reshape(2, 1, d) for k in range(6)]


BIG = ("w_in", "w_br_attn", "w_br_pool", "w_br_sgu", "w_br_conv", "w_gate", "w_o", "w_ff_gate", "w_ff_up", "w_ff_down")
KIND = {"w_in": "col", "w_br_attn": "col", "w_br_pool": "col", "w_br_sgu": "col", "w_br_conv": "col",
        "w_gate": "col", "w_o": "row", "w_ff_gate": "col", "w_ff_up": "col", "w_ff_down": "row"}
SMALL = ("c_ctx", "b_ada", "q_norm_g", "k_norm_g", "pool_w", "pool_scale", "sgu_ln_g", "sgu_ln_b", "sgu_w",
         "sgu_b", "conv_w", "b_gate", "ln1_g", "ln1_b", "ln2_g", "ln2_b")
WEIGHTS = ("c_ctx", "w_ada", "b_ada", "w_in", "q_norm_g", "k_norm_g", "pool_w", "pool_scale", "sgu_ln_g", "sgu_ln_b",
           "sgu_w", "sgu_b", "conv_w", "w_br_attn", "w_br_pool", "w_br_sgu", "w_br_conv", "w_gate", "b_gate", "w_o",
           "ln1_g", "ln1_b", "w_ff_gate", "w_ff_up", "w_ff_down", "ln2_g", "ln2_b")


def _step(x, c, ctx, loss_target, W, M, V):
    L = W["w_ada"].shape[0]
    assert L == 2, "core c of a chip carries layer c of the weight traffic"
    N, D = x.shape[1], x.shape[2]
    NC = ctx.shape[1]
    T = NC + N
    FF = W["w_ff_down"].shape[1] * N_CHIP
    assert NC % ROW_TM == 0 and N % ROW_TM == 0 and N % GRID_W == 0 and D % LANE == 0
    ncb = NC // ROW_TM
    nblk = T // ROW_TM
    alpha = (2 * L) ** 0.25
    ax, ay, ac = lax.axis_index("x"), lax.axis_index("y"), lax.axis_index("c")
    chip = 2 * ax + ay
    dev = 2 * chip + ac
    chip_arr = jnp.reshape(chip, (1,)).astype(jnp.int32)
    core_arr = jnp.reshape(ac, (1,)).astype(jnp.int32)
    ns_ada = W["w_ada"].shape[2]
    chip_devs = (0, 2, 4, 6)

    full = dict(zip(BIG, _gather_weights([_cast_bf16(W[k], "cast_" + k) for k in BIG], "gather_weights")))

    conv_shape = W["conv_w"].shape
    g0 = _all_gather_small(_pack([c, W["conv_w"]]), "gather_c")
    c_all = g0[:, :D // LANE, :].reshape(N_DEV, D)
    conv_parts = [g0[d].reshape(-1)[D:D + math.prod(conv_shape)].reshape(conv_shape) for d in chip_devs]
    conv_full = jnp.concatenate(conv_parts, axis=-1)
    cvec = jnp.concatenate([c_all, W["c_ctx"][None], jnp.zeros((16 - N_DEV - 1, D), F32)], axis=0)
    b_cols = lax.dynamic_slice_in_dim(W["b_ada"], chip * ns_ada, ns_ada, axis=1).reshape(L, 1, ns_ada)
    mod_part = _ada_fwd(cvec, W["w_ada"], b_cols, "ada_fwd")
    g1 = _all_gather_small(mod_part.reshape(-1, LANE), "gather_mod")
    mod_all = jnp.concatenate([g1[d].reshape(L, 16, ns_ada) for d in chip_devs], axis=-1)
    mod_lat = lax.dynamic_index_in_dim(mod_all, dev, axis=1, keepdims=True)
    mods = jnp.concatenate([mod_all[:, N_DEV:N_DEV + 1], mod_lat], axis=1)

    cos, sin = _rope_tables(N, NC)
    x0 = jnp.concatenate([ctx[0], x[0]], axis=0)

    saved = []
    xin = x0
    h1 = None
    dy_top = loss_parts = None
    for l in range(L):
        sh1, sc1, g1m, sh2, sc2, g2m = _split6(mods[l])
        tag = "_l%d" % l
        if l == 0:
            h1 = _modulate(xin, sc1, sh1, ncb, "modulate" + tag)
        p = _mm_nn(h1, full["w_in"], l, "col", name="mm_in" + tag)
        qg, kg = W["q_norm_g"][l][None], W["k_norm_g"][l][None]
        q, k, v = _qk_fwd(p, cos, sin, qg, kg, "qk_fwd" + tag)
        att = _attn_fwd(q, k, v, NC, "attn_fwd" + tag)
        ps = W["pool_scale"][l][None]
        ypool = _pool_fwd(p, W["pool_w"][l], ps, NC, "pool_fwd" + tag)
        lg, lb = W["sgu_ln_g"][l][None], W["sgu_ln_b"][l][None]
        b_st = W["sgu_b"][l].T
        ysgu = _sgu_fwd(p, lg, lb, W["sgu_w"][l], b_st, "sgu_fwd" + tag)
        yconv = _conv_fwd(p, conv_full[l], NC, "conv_fwd" + tag)
        brs = (att, ypool, ysgu, yconv)
        gates = _mm_nn(h1, full["w_gate"], l, "col", name="mm_gate" + tag, bias=W["b_gate"][l][None],
                       act=jax.nn.sigmoid)
        ts = [_mm_nn(b, full[wk], l, "col", name="mm_" + wk + tag)
              for b, wk in zip(brs, ("w_br_attn", "w_br_pool", "w_br_sgu", "w_br_conv"))]
        mg = _merge_fwd(gates, ts, "merge_fwd" + tag)
        m = _mm_nn(mg, full["w_o"], l, "row", name="mm_o" + tag, tn_pref=1024)
        ln1g, ln1b = W["ln1_g"][l][None], W["ln1_b"][l][None]
        x1, h2 = _resid_ln(xin, m, g1m, ln1g, ln1b, sc2, sh2, ncb, alpha, "resid_ln1" + tag)
        gg = _mm_nn(h2, full["w_ff_gate"], l, "col", name="mm_ffg" + tag, tn_pref=1408)
        uu = _mm_nn(h2, full["w_ff_up"], l, "col", name="mm_ffu" + tag, tn_pref=1408)
        act = _swiglu_fwd(gg, uu, "swiglu_fwd" + tag)
        ff = _mm_nn(act, full["w_ff_down"], l, "row", name="mm_ffd" + tag, tm_pref=384, tn_pref=512)
        ln2g, ln2b = W["ln2_g"][l][None], W["ln2_b"][l][None]
        saved.append(dict(xin=xin, h1=h1, p=p, q=q, k=k, v=v, brs=brs, gates=gates, ts=ts, mg=mg, m=m, x1=x1, h2=h2,
                          gg=gg, uu=uu, act=act, ff=ff))
        if l + 1 < L:
            nsh1, nsc1 = _split6(mods[l + 1])[:2]
            xin, h1 = _resid_ln(x1, ff, g2m, ln2g, ln2b, nsc1, nsh1, ncb, alpha, "resid_ln2" + tag)
        else:
            dy_top, loss_parts = _resid_ln_loss(x1, ff, g2m, ln2g, ln2b, loss_target[0], ncb, alpha,
                                                "resid_ln2_loss" + tag)
    loss_dev = jnp.sum(loss_parts[:, 0, 0])

    def lat_ctx(part_rows):
        return jnp.stack([jnp.sum(part_rows[:ncb], axis=0), jnp.sum(part_rows[ncb:], axis=0)])

    dW = {k: [None] * L for k in BIG}
    small = {k: [None] * L for k in ("q_norm_g", "k_norm_g", "pool_w", "pool_scale", "sgu_ln_g", "sgu_ln_b", "sgu_w",
                                     "sgu_b", "conv_w", "b_gate", "ln1_g", "ln1_b", "ln2_g", "ln2_b")}
    dmods = [None] * L
    dxa, dhs, sc_prev = dy_top, [], None
    for l in reversed(range(L)):
        s = saved[l]
        sh1, sc1, g1m, sh2, sc2, g2m = _split6(mods[l])
        tag = "_l%d" % l
        ln1g, ln1b = W["ln1_g"][l][None], W["ln1_b"][l][None]
        ln2g, ln2b = W["ln2_g"][l][None], W["ln2_b"][l][None]
        dx1a, dff, part2 = _ln_bwd(dxa, dhs, sc_prev if dhs else sc1, s["x1"], s["ff"], g2m, ln2g, ln2b, ncb, alpha,
                                   "ln2_bwd" + tag)
        small["ln2_g"][l] = jnp.sum(part2[:, 0], axis=0)
        small["ln2_b"][l] = jnp.sum(part2[:, 1], axis=0)
        dg2 = lat_ctx(part2[:, 2])
        if dhs:
            dmods[l + 1][1], dmods[l + 1][0] = lat_ctx(part2[:, 3]), lat_ctx(part2[:, 4])
        da = _mm_nt(dff, full["w_ff_down"], l, "row", name="nt_ffd" + tag)
        dW["w_ff_down"][l] = _mm_tn(s["act"], dff, "row", nq=N_CHIP, kdim=FF // N_CHIP, ndim=D, name="tn_ffd" + tag,
                                    tk_pref=1408, tn_pref=512)
        dgg, duu = _swiglu_bwd(da, s["gg"], s["uu"], "swiglu_bwd" + tag)
        dh2a = _mm_nt(dgg, full["w_ff_gate"], l, "col", name="nt_ffg" + tag)
        dh2b = _mm_nt(duu, full["w_ff_up"], l, "col", name="nt_ffu" + tag)
        dW["w_ff_gate"][l] = _mm_tn(s["h2"], dgg, "col", nq=N_CHIP, kdim=D, ndim=FF // N_CHIP, name="tn_ffg" + tag,
                                    tn_pref=1408)
        dW["w_ff_up"][l] = _mm_tn(s["h2"], duu, "col", nq=N_CHIP, kdim=D, ndim=FF // N_CHIP, name="tn_ffu" + tag,
                                  tn_pref=1408)
        dx0a, dm, part1 = _ln_bwd(dx1a, [dh2a, dh2b], sc2, s["xin"], s["m"], g1m, ln1g, ln1b, ncb, alpha,
                                  "ln1_bwd" + tag)
        small["ln1_g"][l] = jnp.sum(part1[:, 0], axis=0)
        small["ln1_b"][l] = jnp.sum(part1[:, 1], axis=0)
        dg1 = lat_ctx(part1[:, 2])
        dsc2, dsh2 = lat_ctx(part1[:, 3]), lat_ctx(part1[:, 4])
        dmods[l] = [None, None, dg1, dsh2, dsc2, dg2]
        dmg = _mm_nt(dm, full["w_o"], l, "row", name="nt_o" + tag)
        dW["w_o"][l] = _mm_tn(s["mg"], dm, "row", nq=N_CHIP, kdim=D // N_CHIP, ndim=D, name="tn_o" + tag, tn_pref=1024)
        dpre, dt0, dt1, dt2, dt3, bpart = _merge_bwd(dmg, s["gates"], s["ts"], "merge_bwd" + tag)
        small["b_gate"][l] = jnp.sum(bpart[:, 0], axis=0)
        dh1a = _mm_nt(dpre, full["w_gate"], l, "col", name="nt_gate" + tag)
        dW["w_gate"][l] = _mm_tn(s["h1"], dpre, "col", nq=N_CHIP, kdim=D, ndim=D, name="tn_gate" + tag, tn_pref=1024)
        dbrs = []
        for b, dt, wk, odt in zip(s["brs"], (dt0, dt1, dt2, dt3), ("w_br_attn", "w_br_pool", "w_br_sgu", "w_br_conv"),
                                  (BF16, F32, F32, F32)):
            dbrs.append(_mm_nt(dt, full[wk], l, "col", name="nt_" + wk + tag, out_dtype=odt))
            dW[wk][l] = _mm_tn(b, dt, "col", nq=N_CHIP, kdim=b.shape[1], ndim=D // N_CHIP, name="tn_" + wk + tag)
        qg, kg = W["q_norm_g"][l][None], W["k_norm_g"][l][None]
        dq, dk, dv = _attn_bwd(s["q"], s["k"], s["v"], dbrs[0], NC, "attn_bwd" + tag)
        dp_qkv, qkpart = _qk_bwd(s["p"], dq, dk, dv, cos, sin, qg, kg, "qk_bwd" + tag)
        small["q_norm_g"][l] = jnp.sum(qkpart[:, 0], axis=0)
        small["k_norm_g"][l] = jnp.sum(qkpart[:, 1], axis=0)
        ps = W["pool_scale"][l][None]
        dp_pool, small["pool_w"][l], dps = _pool_bwd(s["p"], dbrs[1], W["pool_w"][l], ps, NC, "pool_bwd" + tag)
        small["pool_scale"][l] = dps[0]
        lg, lb = W["sgu_ln_g"][l][None], W["sgu_ln_b"][l][None]
        dp_sgu, small["sgu_w"][l], dsacc, dln = _sgu_bwd(s["p"], dbrs[2], lg, lb, W["sgu_w"][l], W["sgu_b"][l].T,
                                                         "sgu_bwd" + tag)
        small["sgu_b"][l] = jnp.sum(dsacc.reshape(GC, N_GROUP, GC), axis=-1).T
        small["sgu_ln_g"][l], small["sgu_ln_b"][l] = dln[0], dln[1]
        dp_conv, small["conv_w"][l] = _conv_bwd(s["p"], dbrs[3], conv_full[l], NC, "conv_bwd" + tag)
        dp = jnp.concatenate([dp_qkv, dp_pool, dp_sgu, dp_conv[0], dp_conv[1], dp_conv[2]], axis=-1)
        dh1b = _mm_nt(dp, full["w_in"], l, "col", name="nt_in" + tag)
        dW["w_in"][l] = _mm_tn(s["h1"], dp, "col", nq=N_CHIP, kdim=D, ndim=IN_W // N_CHIP, name="tn_in" + tag,
                               tn_pref=1152)
        dxa, dhs, sc_prev = dx0a, [dh1a, dh1b], sc1
    dx0, part0 = _mod_bwd(dxa, dhs, sc_prev, x0, ncb, "mod_bwd")
    dmods[0][1], dmods[0][0] = lat_ctx(part0[:, 0]), lat_ctx(part0[:, 1])
    grad_x = dx0[NC:][None]
    dmod = jnp.stack([jnp.concatenate(dmods[l], axis=-1) for l in range(L)])

    recv_sib = _swap_layers([dW[k][0] for k in BIG], [dW[k][1] for k in BIG], "swap_layers")
    psums = [_pair_sum(core_arr, dW[k][0], dW[k][1], r, "pair_sum_" + k) for k, r in zip(BIG, recv_sib)]
    recv_chips = _scatter_chunks(psums, "scatter_chunks")
    mine = [_chip_sum(chip_arr, p_, r, "chip_sum_" + k) for k, p_, r in zip(BIG, psums, recv_chips)]
    grads_big = dict(zip(BIG, _swap_grads(mine, "swap_grads")))

    small_names = [k for k in SMALL if k not in ("c_ctx", "b_ada")]
    small_arrs = [jnp.stack(small[k]) for k in small_names]
    small_shapes = [a.shape for a in small_arrs]
    slab = _pack([loss_dev.reshape(1), jnp.zeros((LANE - 1,), F32), dmod] + small_arrs)
    g2 = _all_gather_small(slab, "gather_small")
    total = _slab_sum(g2, "slab_sum")
    flat_total = total.reshape(-1)
    loss = flat_total[0]
    nmod = L * 2 * 6 * D
    dmod_sum = flat_total[LANE:LANE + nmod].reshape(L, 2, 6 * D)
    small_grads = dict(zip(small_names, _unpack(flat_total[LANE + nmod:], small_shapes)))
    dmod_lat = g2.reshape(N_DEV, -1)[:, LANE:LANE + nmod].reshape(N_DEV, L, 2, 6 * D)[:, :, 1]
    dm16 = jnp.concatenate([jnp.transpose(dmod_lat, (1, 0, 2)), dmod_sum[:, 0:1],
                            jnp.zeros((L, 16 - N_DEV - 1, 6 * D), F32)], axis=1)
    small_grads["b_ada"] = dmod_sum[:, 0] + dmod_sum[:, 1]
    dm16_cols = lax.dynamic_slice_in_dim(dm16, chip * ns_ada, ns_ada, axis=2)
    grad_w_ada, ds16 = _ada_bwd(cvec, dm16_cols, W["w_ada"], "ada_bwd")
    g3 = _all_gather_small(ds16[N_DEV].reshape(-1, LANE), "gather_dsilu")
    small_grads["c_ctx"] = _cctx_grad(g3, W["c_ctx"].reshape(-1, LANE), "cctx_grad").reshape(D)
    conv_grad_full = small_grads["conv_w"]
    small_grads["conv_w"] = lax.dynamic_slice_in_dim(conv_grad_full, chip * GC, GC, axis=2)

    grads = dict(grads_big)
    grads["w_ada"] = grad_w_ada
    grads.update(small_grads)
    delta, new_m, new_v = {}, {}, {}
    for k in BIG + ("w_ada",):
        delta[k], new_m[k], new_v[k] = _adamw(W[k], grads[k], M[k], V[k], "adamw_" + k)
    shapes = [W[k].shape for k in SMALL]
    sd, sm, sv = _adamw(_pack([W[k] for k in SMALL]), _pack([grads[k] for k in SMALL]),
                        _pack([M[k] for k in SMALL]), _pack([V[k] for k in SMALL]), "adamw_small")
    for k, d_, m_, v_ in zip(SMALL, _unpack(sd, shapes), _unpack(sm, shapes), _unpack(sv, shapes)):
        delta[k], new_m[k], new_v[k] = d_, m_, v_
    return (loss, grad_x, *[grads[k] for k in WEIGHTS], *[delta[k] for k in WEIGHTS],
            *[new_m[k] for k in WEIGHTS], *[new_v[k] for k in WEIGHTS])


def kernel(x, c, ctx, c_ctx, w_ada, b_ada, w_in, q_norm_g, k_norm_g, pool_w, pool_scale, sgu_ln_g, sgu_ln_b, sgu_w, sgu_b, conv_w, w_br_attn, w_br_pool, w_br_sgu, w_br_conv, w_gate, b_gate, w_o, ln1_g, ln1_b, w_ff_gate, w_ff_up, w_ff_down, ln2_g, ln2_b, loss_target, m_c_ctx, m_w_ada, m_b_ada, m_w_in, m_q_norm_g, m_k_norm_g, m_pool_w, m_pool_scale, m_sgu_ln_g, m_sgu_ln_b, m_sgu_w, m_sgu_b, m_conv_w, m_w_br_attn, m_w_br_pool, m_w_br_sgu, m_w_br_conv, m_w_gate, m_b_gate, m_w_o, m_ln1_g, m_ln1_b, m_w_ff_gate, m_w_ff_up, m_w_ff_down, m_ln2_g, m_ln2_b, v_c_ctx, v_w_ada, v_b_ada, v_w_in, v_q_norm_g, v_k_norm_g, v_pool_w, v_pool_scale, v_sgu_ln_g, v_sgu_ln_b, v_sgu_w, v_sgu_b, v_conv_w, v_w_br_attn, v_w_br_pool, v_w_br_sgu, v_w_br_conv, v_w_gate, v_b_gate, v_w_o, v_ln1_g, v_ln1_b, v_w_ff_gate, v_w_ff_up, v_w_ff_down, v_ln2_g, v_ln2_b):
    args = locals()
    W = {k: args[k] for k in WEIGHTS}
    M = {k: args["m_" + k] for k in WEIGHTS}
    V = {k: args["v_" + k] for k in WEIGHTS}
    return _step(x, c, ctx, loss_target, W, M, V)
```

```python
import functools
import math

import jax
import jax.numpy as jnp
from jax import lax
from jax.experimental import pallas as pl
from jax.experimental.pallas import tpu as pltpu

F32 = jnp.float32
BF16 = jnp.bfloat16
MESH = pl.DeviceIdType.MESH

N_DEV = 8
N_CHIP = 4
GRID_W = 64
HEAD_DIM = 128
N_HEADS = 8
N_KV = 2
KV_GROUP = N_HEADS // N_KV
Q_W = N_HEADS * HEAD_DIM
KV_W = N_KV * HEAD_DIM
QKV_W = Q_W + 2 * KV_W
GC = 128
N_GROUP = 4
BR_W = N_GROUP * GC
POOL_WINDOWS = (2, 4, 8, 16)
OFF_POOL = QKV_W
OFF_U = OFF_POOL + BR_W
OFF_VG = OFF_U + BR_W
OFF_CB = OFF_VG + BR_W
OFF_CC = OFF_CB + BR_W
OFF_CX = OFF_CC + BR_W
IN_W = OFF_CX + BR_W
Q_BLOCK = 128
ROPE_THETA = 10000.0
LN_EPS = 1e-5
RMS_EPS = 1e-6
ADAM_LR = 0.001
ADAM_B1 = 0.9
ADAM_B2 = 0.999
ADAM_EPS = 1e-08
ADAM_WD = 0.01
ADAM_STEP = 10

ROW_TM = 256
LANE = 128
SUBLANE = 8
VMEM_BIG = 56 * 1024 * 1024
VMEM_MID = 40 * 1024 * 1024


def _cp(sem=None, vmem=VMEM_MID):
    return pltpu.CompilerParams(dimension_semantics=sem, vmem_limit_bytes=vmem)


_ANY = pl.BlockSpec(memory_space=pl.ANY)
_HBM = pl.BlockSpec(memory_space=pltpu.HBM)
_SEM = pl.BlockSpec(memory_space=pltpu.SEMAPHORE)
_EFFECT = pltpu.SideEffectType.DATAFLOW_SIDE_EFFECTING


def _tile(n, pref, unit=LANE, whole_ok=False):
    if n <= pref:
        return n
    best = None
    for t in range(unit, pref + 1, unit):
        if n % t == 0:
            best = t
    if whole_ok and (best is None or best * 4 < pref):
        return n
    assert best is not None, (n, pref, unit)
    return best


def _layer_of(w, l):
    if isinstance(w, (list, tuple)):
        return w[l][:, None], 0
    return w, l


def _mm_nn(a, w, l, kind, *, name, out_dtype=F32, bias=None, act=None, tm_pref=768, tn_pref=1152):
    w, l = _layer_of(w, l)
    T, K = a.shape
    tm = _tile(T, tm_pref, SUBLANE * 2)
    if kind == "col":
        nq, _, kw, ns = w.shape
        assert kw == K
        tn = _tile(ns, tn_pref, whole_ok=True)
        nj = ns // tn
        n_total = nq * ns
        w_spec = pl.BlockSpec((None, None, K, tn), lambda j, i: (j // nj, l, 0, j % nj))
        grid_n = nq * nj
    else:
        nq, _, kc, n_total = w.shape
        assert nq * kc == K
        tn = _tile(n_total, tn_pref)
        w_spec = pl.BlockSpec((nq, None, kc, tn), lambda j, i: (0, l, 0, j))
        grid_n = n_total // tn

    def body(*refs):
        if bias is not None:
            a_ref, w_ref, b_ref, o_ref = refs
        else:
            a_ref, w_ref, o_ref = refs
        wv = w_ref[...]
        if kind == "row":
            wv = wv.reshape(K, tn)
        acc = jnp.dot(a_ref[...], wv, preferred_element_type=F32)
        if bias is not None:
            acc = acc + b_ref[...]
        if act is not None:
            acc = act(acc)
        o_ref[...] = acc.astype(out_dtype)

    in_specs = [pl.BlockSpec((tm, K), lambda j, i: (i, 0)), w_spec]
    args = [a, w]
    if bias is not None:
        in_specs.append(pl.BlockSpec((1, tn), lambda j, i: (0, j)))
        args.append(bias)
    return pl.pallas_call(
        body, grid=(grid_n, T // tm), in_specs=in_specs,
        out_specs=pl.BlockSpec((tm, tn), lambda j, i: (i, j)),
        out_shape=jax.ShapeDtypeStruct((T, n_total), out_dtype),
        name=name, compiler_params=_cp(("parallel", "parallel"), VMEM_BIG),
    )(*args)


def _mm_nt(dc, w, l, kind, *, name, out_dtype=F32, dc_off=0, tm_pref=768, tk_pref=1024):
    w, l = _layer_of(w, l)
    T = dc.shape[0]
    tm = _tile(T, tm_pref, SUBLANE * 2)
    if kind == "col":
        nq, _, K, ns = w.shape
        tk = _tile(K, tk_pref)
        assert dc_off % ns == 0
        offb = dc_off // ns

        def body(dc_ref, w_ref, o_ref, acc_ref):
            q = pl.program_id(2)
            part = lax.dot_general(dc_ref[...], w_ref[...], (((1,), (1,)), ((), ())),
                                   preferred_element_type=F32)

            @pl.when(q == 0)
            def _():
                acc_ref[...] = part

            @pl.when(q > 0)
            def _():
                acc_ref[...] += part

            @pl.when(q == nq - 1)
            def _():
                o_ref[...] = acc_ref[...].astype(out_dtype)

        return pl.pallas_call(
            body, grid=(K // tk, T // tm, nq),
            in_specs=[pl.BlockSpec((tm, ns), lambda kk, i, q: (i, offb + q)),
                      pl.BlockSpec((None, None, tk, ns), lambda kk, i, q: (q, l, kk, 0))],
            out_specs=pl.BlockSpec((tm, tk), lambda kk, i, q: (i, kk)),
            out_shape=jax.ShapeDtypeStruct((T, K), out_dtype),
            scratch_shapes=[pltpu.VMEM((tm, tk), F32)],
            name=name, compiler_params=_cp(("parallel", "parallel", "arbitrary"), VMEM_BIG),
        )(dc, w)

    nq, _, kc, n = w.shape
    assert dc_off % n == 0
    offb = dc_off // n
    tk = _tile(kc, tk_pref, whole_ok=True)
    nkk = kc // tk

    def body(dc_ref, w_ref, o_ref):
        o_ref[...] = lax.dot_general(dc_ref[...], w_ref[...], (((1,), (1,)), ((), ())),
                                     preferred_element_type=F32).astype(out_dtype)

    return pl.pallas_call(
        body, grid=(nq * nkk, T // tm),
        in_specs=[pl.BlockSpec((tm, n), lambda j, i: (i, offb)),
                  pl.BlockSpec((None, None, tk, n), lambda j, i: (j // nkk, l, j % nkk, 0))],
        out_specs=pl.BlockSpec((tm, tk), lambda j, i: (i, j)),
        out_shape=jax.ShapeDtypeStruct((T, nq * kc), out_dtype),
        name=name, compiler_params=_cp(("parallel", "parallel"), VMEM_BIG),
    )(dc, w)


def _mm_tn(a, dc, kind, *, nq, kdim, ndim, name, a_off=0, dc_off=0, tk_pref=512, tn_pref=1152, out_dtype=BF16):
    T = a.shape[0]
    assert dc.shape[0] == T
    tk = _tile(kdim, tk_pref, whole_ok=True)
    tn = _tile(ndim, tn_pref, whole_ok=True)
    nkk, njn = kdim // tk, ndim // tn
    assert a_off % tk == 0 and dc_off % tn == 0
    aoffb, doffb = a_off // tk, dc_off // tn
    if kind == "col":
        a_map = lambda q, kk, jn: (0, aoffb + kk)
        d_map = lambda q, kk, jn: (0, doffb + q * njn + jn)
    else:
        a_map = lambda q, kk, jn: (0, aoffb + q * nkk + kk)
        d_map = lambda q, kk, jn: (0, doffb + jn)

    def body(a_ref, d_ref, o_ref):
        o_ref[...] = lax.dot_general(a_ref[...], d_ref[...], (((0,), (0,)), ((), ())),
                                     preferred_element_type=F32).astype(out_dtype)

    return pl.pallas_call(
        body, grid=(nq, nkk, njn),
        in_specs=[pl.BlockSpec((T, tk), a_map), pl.BlockSpec((T, tn), d_map)],
        out_specs=pl.BlockSpec((None, tk, tn), lambda q, kk, jn: (q, kk, jn)),
        out_shape=jax.ShapeDtypeStruct((nq, kdim, ndim), out_dtype),
        name=name, compiler_params=_cp(("parallel", "parallel", "parallel"), VMEM_BIG),
    )(a, dc)


def _row_spec(d):
    return pl.BlockSpec((ROW_TM, d), lambda i: (i, 0))


def _mod_spec(d, ncb):
    return pl.BlockSpec((None, 1, d), lambda i: (jnp.where(i >= ncb, 1, 0), 0, 0))


def _vec_spec(d):
    return pl.BlockSpec((1, d), lambda i: (0, 0))


def _part_spec(d):
    return pl.BlockSpec((None, SUBLANE, d), lambda i: (i, 0, 0))


def _modulate(x, sc, sh, ncb, name):
    T, D = x.shape

    def body(x_ref, sc_ref, sh_ref, o_ref):
        o_ref[...] = (x_ref[...] * (1.0 + sc_ref[...]) + sh_ref[...]).astype(BF16)

    return pl.pallas_call(
        body, grid=(T // ROW_TM,),
        in_specs=[_row_spec(D), _mod_spec(D, ncb), _mod_spec(D, ncb)],
        out_specs=_row_spec(D), out_shape=jax.ShapeDtypeStruct((T, D), BF16),
        name=name, compiler_params=_cp(("parallel",)),
    )(x, sc, sh)


def _ln_stats(r):
    mu = jnp.mean(r, axis=-1, keepdims=True)
    rc = r - mu
    var = jnp.mean(rc * rc, axis=-1, keepdims=True)
    rstd = lax.rsqrt(var + LN_EPS)
    return rc * rstd, rstd


def _resid_ln(x, y, gate, g, b, sc, sh, ncb, alpha, name):
    T, D = x.shape

    def body(x_ref, y_ref, gate_ref, g_ref, b_ref, sc_ref, sh_ref, xo_ref, h_ref):
        xhat, _ = _ln_stats(alpha * x_ref[...] + gate_ref[...] * y_ref[...])
        xo = xhat * g_ref[...] + b_ref[...]
        xo_ref[...] = xo
        h_ref[...] = (xo * (1.0 + sc_ref[...]) + sh_ref[...]).astype(BF16)

    return pl.pallas_call(
        body, grid=(T // ROW_TM,),
        in_specs=[_row_spec(D), _row_spec(D), _mod_spec(D, ncb), _vec_spec(D), _vec_spec(D),
                  _mod_spec(D, ncb), _mod_spec(D, ncb)],
        out_specs=[_row_spec(D), _row_spec(D)],
        out_shape=[jax.ShapeDtypeStruct((T, D), F32), jax.ShapeDtypeStruct((T, D), BF16)],
        name=name, compiler_params=_cp(("parallel",)),
    )(x, y, gate, g, b, sc, sh)


def _resid_ln_loss(x, y, gate, g, b, target, ncb, alpha, name):
    T, D = x.shape
    nblk = T // ROW_TM

    def body(x_ref, y_ref, gate_ref, g_ref, b_ref, t_ref, dy_ref, loss_ref):
        i = pl.program_id(0)
        xhat, _ = _ln_stats(alpha * x_ref[...] + gate_ref[...] * y_ref[...])
        xo = xhat * g_ref[...] + b_ref[...]
        live = (i >= ncb).astype(F32)
        err = (xo - t_ref[...]) * live
        dy_ref[...] = err * (1.0 / D)
        loss_ref[...] = jnp.full((SUBLANE, LANE), 0.5 / D, F32) * jnp.sum(err * err)

    return pl.pallas_call(
        body, grid=(nblk,),
        in_specs=[_row_spec(D), _row_spec(D), _mod_spec(D, ncb), _vec_spec(D), _vec_spec(D),
                  pl.BlockSpec((ROW_TM, D), lambda i: (jnp.maximum(i - ncb, 0), 0))],
        out_specs=[_row_spec(D), _part_spec(LANE)],
        out_shape=[jax.ShapeDtypeStruct((T, D), F32), jax.ShapeDtypeStruct((nblk, SUBLANE, LANE), F32)],
        name=name, compiler_params=_cp(("parallel",)),
    )(x, y, gate, g, b, target)


def _write_parts(part_ref, rows, d):
    for k, r in enumerate(rows):
        part_ref[pl.ds(k, 1), :] = jnp.sum(r, axis=0, keepdims=True)
    if len(rows) < SUBLANE:
        part_ref[pl.ds(len(rows), SUBLANE - len(rows)), :] = jnp.zeros((SUBLANE - len(rows), d), F32)


def _ln_bwd(dxa, dhs, sc, x, y, gate, g, b, ncb, alpha, name, after=None):
    T, D = x.shape
    nblk = T // ROW_TM
    ndh = len(dhs)

    def body(*refs):
        dxa_ref = refs[0]
        dh_refs = refs[1:1 + ndh]
        sc_ref, x_ref, y_ref, gate_ref, g_ref, b_ref = refs[1 + ndh:7 + ndh]
        dx_ref, dy_ref, part_ref = refs[-3:]
        yv = y_ref[...]
        xhat, rstd = _ln_stats(alpha * x_ref[...] + gate_ref[...] * yv)
        dxo = dxa_ref[...]
        rows = []
        if ndh:
            dh = dh_refs[0][...]
            for r in dh_refs[1:]:
                dh = dh + r[...]
            dxo = dxo + dh * (1.0 + sc_ref[...])
            xo = xhat * g_ref[...] + b_ref[...]
            rows = [dh * xo, dh]
        dxhat = dxo * g_ref[...]
        m1 = jnp.mean(dxhat, axis=-1, keepdims=True)
        m2 = jnp.mean(dxhat * xhat, axis=-1, keepdims=True)
        dr = rstd * (dxhat - m1 - xhat * m2)
        dx_ref[...] = alpha * dr
        dy_ref[...] = (gate_ref[...] * dr).astype(BF16)
        _write_parts(part_ref, [dxo * xhat, dxo, dr * yv] + rows, D)

    in_specs = ([_row_spec(D)] * (1 + ndh)
                + [_mod_spec(D, ncb), _row_spec(D), _row_spec(D), _mod_spec(D, ncb), _vec_spec(D), _vec_spec(D)])
    extra = [] if after is None else [after]
    return pl.pallas_call(
        body, grid=(nblk,), in_specs=in_specs + [_ANY] * len(extra),
        out_specs=[_row_spec(D), _row_spec(D), _part_spec(D)],
        out_shape=[jax.ShapeDtypeStruct((T, D), F32), jax.ShapeDtypeStruct((T, D), BF16),
                   jax.ShapeDtypeStruct((nblk, SUBLANE, D), F32)],
        name=name, compiler_params=_cp(("parallel",)),
    )(dxa, *dhs, sc, x, y, gate, g, b, *extra)


def _mod_bwd(dxa, dhs, sc, x, ncb, name):
    T, D = x.shape
    nblk = T // ROW_TM
    ndh = len(dhs)

    def body(*refs):
        dxa_ref = refs[0]
        dh_refs = refs[1:1 + ndh]
        sc_ref, x_ref, dx_ref, part_ref = refs[1 + ndh:]
        dh = dh_refs[0][...]
        for r in dh_refs[1:]:
            dh = dh + r[...]
        dx_ref[...] = dxa_ref[...] + dh * (1.0 + sc_ref[...])
        _write_parts(part_ref, [dh * x_ref[...], dh], D)

    return pl.pallas_call(
        body, grid=(nblk,),
        in_specs=[_row_spec(D)] * (1 + ndh) + [_mod_spec(D, ncb), _row_spec(D)],
        out_specs=[_row_spec(D), _part_spec(D)],
        out_shape=[jax.ShapeDtypeStruct((T, D), F32), jax.ShapeDtypeStruct((nblk, SUBLANE, D), F32)],
        name=name, compiler_params=_cp(("parallel",)),
    )(dxa, *dhs, sc, x)


def _merge_fwd(gates, ts, name):
    T, D = ts[0].shape
    tm = 128

    def body(g_ref, t0, t1, t2, t3, o_ref):
        acc = g_ref[:, 0:D] * t0[...]
        for k, t in enumerate((t1, t2, t3), start=1):
            acc = acc + g_ref[:, k * D:(k + 1) * D] * t[...]
        o_ref[...] = acc.astype(BF16)

    rs = pl.BlockSpec((tm, D), lambda i: (i, 0))
    return pl.pallas_call(
        body, grid=(T // tm,),
        in_specs=[pl.BlockSpec((tm, 4 * D), lambda i: (i, 0)), rs, rs, rs, rs],
        out_specs=rs, out_shape=jax.ShapeDtypeStruct((T, D), BF16),
        name=name, compiler_params=_cp(("parallel",)),
    )(gates, *ts)


def _merge_bwd(dmg, gates, ts, name):
    T, D = dmg.shape
    tm = 128
    nblk = T // tm

    def body(d_ref, g_ref, t0, t1, t2, t3, dpre_ref, dt0, dt1, dt2, dt3, part_ref):
        d = d_ref[...]
        for k, (t, dt) in enumerate(zip((t0, t1, t2, t3), (dt0, dt1, dt2, dt3))):
            gk = g_ref[:, k * D:(k + 1) * D]
            dt[...] = (d * gk).astype(BF16)
            dpre = d * t[...] * gk * (1.0 - gk)
            dpre_ref[:, k * D:(k + 1) * D] = dpre.astype(BF16)
            part_ref[:, k * D:(k + 1) * D] = jnp.sum(dpre, axis=0, keepdims=True)

    rs = pl.BlockSpec((tm, D), lambda i: (i, 0))
    wide = pl.BlockSpec((tm, 4 * D), lambda i: (i, 0))
    return pl.pallas_call(
        body, grid=(nblk,),
        in_specs=[rs, wide, rs, rs, rs, rs],
        out_specs=[wide, rs, rs, rs, rs, pl.BlockSpec((None, 1, 4 * D), lambda i: (i, 0, 0))],
        out_shape=[jax.ShapeDtypeStruct((T, 4 * D), BF16)] + [jax.ShapeDtypeStruct((T, D), BF16)] * 4
                  + [jax.ShapeDtypeStruct((nblk, 1, 4 * D), F32)],
        name=name, compiler_params=_cp(("parallel",)),
    )(dmg, gates, *ts)


def _swiglu_fwd(gg, uu, name):
    T, F = gg.shape
    tm = 128

    def body(g_ref, u_ref, o_ref):
        g = g_ref[...]
        o_ref[...] = (g * jax.nn.sigmoid(g) * u_ref[...]).astype(BF16)

    rs = pl.BlockSpec((tm, F), lambda i: (i, 0))
    return pl.pallas_call(
        body, grid=(T // tm,), in_specs=[rs, rs], out_specs=rs,
        out_shape=jax.ShapeDtypeStruct((T, F), BF16),
        name=name, compiler_params=_cp(("parallel",)),
    )(gg, uu)


def _swiglu_bwd(da, gg, uu, name):
    T, F = gg.shape
    tm = 128

    def body(da_ref, g_ref, u_ref, dg_ref, du_ref):
        g = g_ref[...]
        da = da_ref[...]
        sig = jax.nn.sigmoid(g)
        dg_ref[...] = (da * u_ref[...] * sig * (1.0 + g * (1.0 - sig))).astype(BF16)
        du_ref[...] = (da * g * sig).astype(BF16)

    rs = pl.BlockSpec((tm, F), lambda i: (i, 0))
    return pl.pallas_call(
        body, grid=(T // tm,), in_specs=[rs, rs, rs], out_specs=[rs, rs],
        out_shape=[jax.ShapeDtypeStruct((T, F), BF16)] * 2,
        name=name, compiler_params=_cp(("parallel",)),
    )(da, gg, uu)


def _swap_halves(v):
    lane = lax.broadcasted_iota(jnp.int32, v.shape, 1)
    return jnp.where((lane % 64) < 32, pltpu.roll(v, 96, 1), pltpu.roll(v, 32, 1))


def _rope_tables(n, nc):
    rows = n // GRID_W
    row = jnp.repeat(jnp.arange(rows), GRID_W).astype(F32)
    col = jnp.tile(jnp.arange(GRID_W), rows).astype(F32)
    inv = ROPE_THETA ** (-jnp.arange(0, 64, 2, dtype=F32) / 64)
    ang_r = row[:, None] * inv
    ang_c = col[:, None] * inv
    cos = jnp.concatenate([jnp.cos(ang_r), jnp.cos(ang_r), jnp.cos(ang_c), jnp.cos(ang_c)], axis=-1)
    sin = jnp.concatenate([-jnp.sin(ang_r), jnp.sin(ang_r), -jnp.sin(ang_c), jnp.sin(ang_c)], axis=-1)
    cos = jnp.concatenate([jnp.ones((nc, HEAD_DIM), F32), cos], axis=0)
    sin = jnp.concatenate([jnp.zeros((nc, HEAD_DIM), F32), sin], axis=0)
    return cos, sin


def _qk_fwd(p, cos, sin, qg, kg, name):
    T = p.shape[0]

    def body(p_ref, c_ref, s_ref, qg_ref, kg_ref, q_ref, k_ref, v_ref):
        c, s = c_ref[...], s_ref[...]
        for h in range(N_HEADS + N_KV):
            x = p_ref[:, h * HEAD_DIM:(h + 1) * HEAD_DIM]
            rs = lax.rsqrt(jnp.mean(x * x, axis=-1, keepdims=True) + RMS_EPS)
            gain = qg_ref[...] if h < N_HEADS else kg_ref[...]
            yv = x * rs * gain
            out = (yv * c + _swap_halves(yv) * s).astype(BF16)
            if h < N_HEADS:
                q_ref[:, h * HEAD_DIM:(h + 1) * HEAD_DIM] = out
            else:
                k_ref[:, (h - N_HEADS) * HEAD_DIM:(h - N_HEADS + 1) * HEAD_DIM] = out
        v_ref[...] = p_ref[:, Q_W + KV_W:QKV_W].astype(BF16)

    return pl.pallas_call(
        body, grid=(T // ROW_TM,),
        in_specs=[_row_spec(QKV_W), _row_spec(HEAD_DIM), _row_spec(HEAD_DIM), _vec_spec(HEAD_DIM), _vec_spec(HEAD_DIM)],
        out_specs=[_row_spec(Q_W), _row_spec(KV_W), _row_spec(KV_W)],
        out_shape=[jax.ShapeDtypeStruct((T, Q_W), BF16), jax.ShapeDtypeStruct((T, KV_W), BF16),
                   jax.ShapeDtypeStruct((T, KV_W), BF16)],
        name=name, compiler_params=_cp(("parallel",)),
    )(p, cos, sin, qg, kg)


def _qk_bwd(p, dq, dk, dv, cos, sin, qg, kg, name):
    T = p.shape[0]
    nblk = T // ROW_TM

    def body(p_ref, dq_ref, dk_ref, dv_ref, c_ref, s_ref, qg_ref, kg_ref, dp_ref, part_ref):
        c, s = c_ref[...], s_ref[...]
        dgq = jnp.zeros((1, HEAD_DIM), F32)
        dgk = jnp.zeros((1, HEAD_DIM), F32)
        for h in range(N_HEADS + N_KV):
            x = p_ref[:, h * HEAD_DIM:(h + 1) * HEAD_DIM]
            if h < N_HEADS:
                d = dq_ref[:, h * HEAD_DIM:(h + 1) * HEAD_DIM]
                gain = qg_ref[...]
            else:
                d = dk_ref[:, (h - N_HEADS) * HEAD_DIM:(h - N_HEADS + 1) * HEAD_DIM]
                gain = kg_ref[...]
            dyv = d * c + _swap_halves(d * s)
            rs = lax.rsqrt(jnp.mean(x * x, axis=-1, keepdims=True) + RMS_EPS)
            xn = x * rs
            dgsum = jnp.sum(dyv * xn, axis=0, keepdims=True)
            if h < N_HEADS:
                dgq = dgq + dgsum
            else:
                dgk = dgk + dgsum
            dxg = dyv * gain
            dx = rs * (dxg - xn * jnp.mean(dxg * xn, axis=-1, keepdims=True))
            dp_ref[:, h * HEAD_DIM:(h + 1) * HEAD_DIM] = dx.astype(BF16)
        dp_ref[:, Q_W + KV_W:QKV_W] = dv_ref[...].astype(BF16)
        part_ref[pl.ds(0, 1), :] = dgq
        part_ref[pl.ds(1, 1), :] = dgk
        part_ref[pl.ds(2, SUBLANE - 2), :] = jnp.zeros((SUBLANE - 2, HEAD_DIM), F32)

    return pl.pallas_call(
        body, grid=(nblk,),
        in_specs=[_row_spec(QKV_W), _row_spec(Q_W), _row_spec(KV_W), _row_spec(KV_W),
                  _row_spec(HEAD_DIM), _row_spec(HEAD_DIM), _vec_spec(HEAD_DIM), _vec_spec(HEAD_DIM)],
        out_specs=[_row_spec(QKV_W), _part_spec(HEAD_DIM)],
        out_shape=[jax.ShapeDtypeStruct((T, QKV_W), BF16), jax.ShapeDtypeStruct((nblk, SUBLANE, HEAD_DIM), F32)],
        name=name, compiler_params=_cp(("parallel",)),
    )(p, dq, dk, dv, cos, sin, qg, kg)


def _attn_probs(q, k, qblk, nc, T):
    s = lax.dot_general(q, k, (((1,), (1,)), ((), ())), preferred_element_type=F32) * (HEAD_DIM ** -0.5)
    col = lax.broadcasted_iota(jnp.int32, s.shape, 1)
    limit = jnp.where(qblk < nc // Q_BLOCK, nc, T)
    s = jnp.where(col < limit, s, -1e30)
    e = jnp.exp(s - jnp.max(s, axis=-1, keepdims=True))
    return e / jnp.sum(e, axis=-1, keepdims=True)


def _attn_fwd(q, k, v, nc, name):
    T = q.shape[0]

    def body(q_ref, k_ref, v_ref, o_ref):
        i = pl.program_id(0)
        for h in range(N_HEADS):
            g = h // KV_GROUP
            kk = k_ref[:, g * HEAD_DIM:(g + 1) * HEAD_DIM]
            vv = v_ref[:, g * HEAD_DIM:(g + 1) * HEAD_DIM]
            pr = _attn_probs(q_ref[:, h * HEAD_DIM:(h + 1) * HEAD_DIM], kk, i, nc, T)
            o = jnp.dot(pr.astype(BF16), vv, preferred_element_type=F32)
            o_ref[:, h * HEAD_DIM:(h + 1) * HEAD_DIM] = o.astype(BF16)

    whole = pl.BlockSpec((T, KV_W), lambda i: (0, 0))
    qs = pl.BlockSpec((Q_BLOCK, Q_W), lambda i: (i, 0))
    return pl.pallas_call(
        body, grid=(T // Q_BLOCK,), in_specs=[qs, whole, whole], out_specs=qs,
        out_shape=jax.ShapeDtypeStruct((T, Q_W), BF16),
        name=name, compiler_params=_cp(("parallel",)),
    )(q, k, v)


def _attn_bwd(q, k, v, do, nc, name):
    T = q.shape[0]
    scale = HEAD_DIM ** -0.5

    def body(q_ref, k_ref, v_ref, do_ref, dq_ref, dk_ref, dv_ref):
        i = pl.program_id(0)

        @pl.when(i == 0)
        def _():
            dk_ref[...] = jnp.zeros_like(dk_ref)
            dv_ref[...] = jnp.zeros_like(dv_ref)

        for h in range(N_HEADS):
            g = h // KV_GROUP
            cols = slice(g * HEAD_DIM, (g + 1) * HEAD_DIM)
            hc = slice(h * HEAD_DIM, (h + 1) * HEAD_DIM)
            qh, kk, vv, doh = q_ref[:, hc], k_ref[:, cols], v_ref[:, cols], do_ref[:, hc]
            pr = _attn_probs(qh, kk, i, nc, T)
            dpr = lax.dot_general(doh, vv, (((1,), (1,)), ((), ())), preferred_element_type=F32)
            ds = (pr * (dpr - jnp.sum(pr * dpr, axis=-1, keepdims=True)) * scale).astype(BF16)
            dq_ref[:, hc] = jnp.dot(ds, kk, preferred_element_type=F32)
            dk_ref[:, cols] += lax.dot_general(ds, qh, (((0,), (0,)), ((), ())), preferred_element_type=F32)
            dv_ref[:, cols] += lax.dot_general(pr.astype(BF16), doh, (((0,), (0,)), ((), ())),
                                               preferred_element_type=F32)

    whole = pl.BlockSpec((T, KV_W), lambda i: (0, 0))
    qs = pl.BlockSpec((Q_BLOCK, Q_W), lambda i: (i, 0))
    return pl.pallas_call(
        body, grid=(T // Q_BLOCK,), in_specs=[qs, whole, whole, qs], out_specs=[qs, whole, whole],
        out_shape=[jax.ShapeDtypeStruct((T, Q_W), F32), jax.ShapeDtypeStruct((T, KV_W), F32),
                   jax.ShapeDtypeStruct((T, KV_W), F32)],
        name=name, compiler_params=_cp(("arbitrary",)),
    )(q, k, v, do)


def _shift_rows(z, o, nc):
    T = z.shape[0]
    t = lax.broadcasted_iota(jnp.int32, (T, 1), 0)
    lo = jnp.where(t < nc, 0, nc)
    hi = jnp.where(t < nc, nc, T)
    ok = jnp.logical_and(t + o >= lo, t + o < hi)
    rolled = z if o == 0 else pltpu.roll(z, (-o) % T, 0)
    return jnp.where(ok, rolled, 0.0), ok


def _pool_window(w):
    left = w // 2
    return -left, w - 1 - left


def _pool_d(z, w, nc):
    o0, o1 = _pool_window(w)
    tot = jnp.zeros_like(z)
    cnt = jnp.zeros((z.shape[0], 1), F32)
    for o in range(o0, o1 + 1):
        sh, ok = _shift_rows(z, o, nc)
        tot = tot + sh
        cnt = cnt + ok.astype(F32)
    return tot / cnt - z, cnt


def _pool_fwd(p, pool_w, pool_scale, nc, name):
    T = p.shape[0]

    def body(z_ref, w_ref, s_ref, o_ref):
        for g, w in enumerate(POOL_WINDOWS):
            cs = slice(g * GC, (g + 1) * GC)
            d, _ = _pool_d(z_ref[:, cs], w, nc)
            yv = jnp.dot(d.astype(BF16), w_ref[g].astype(BF16), preferred_element_type=F32)
            o_ref[:, cs] = (yv * s_ref[:, cs]).astype(BF16)

    return pl.pallas_call(
        body, grid=(1,),
        in_specs=[pl.BlockSpec((T, BR_W), lambda i: (0, OFF_POOL // BR_W)),
                  pl.BlockSpec((N_GROUP, GC, GC), lambda i: (0, 0, 0)), _vec_spec(BR_W)],
        out_specs=pl.BlockSpec((T, BR_W), lambda i: (0, 0)),
        out_shape=jax.ShapeDtypeStruct((T, BR_W), BF16),
        name=name, compiler_params=_cp(("arbitrary",), VMEM_BIG),
    )(p, pool_w, pool_scale)


def _pool_bwd(p, dy, pool_w, pool_scale, nc, name):
    T = p.shape[0]

    def body(z_ref, dy_ref, w_ref, s_ref, dz_ref, dw_ref, ds_ref):
        for g, w in enumerate(POOL_WINDOWS):
            cs = slice(g * GC, (g + 1) * GC)
            d, cnt = _pool_d(z_ref[:, cs], w, nc)
            db = d.astype(BF16)
            wb = w_ref[g].astype(BF16)
            dyv = dy_ref[:, cs]
            e = (dyv * s_ref[:, cs]).astype(BF16)
            dw_ref[g] = lax.dot_general(db, e, (((0,), (0,)), ((), ())), preferred_element_type=F32)
            yraw = jnp.dot(db, wb, preferred_element_type=F32)
            ds_ref[:, cs] = jnp.sum(dyv * yraw, axis=0, keepdims=True)
            dd = lax.dot_general(e, wb, (((1,), (1,)), ((), ())), preferred_element_type=F32)
            ec = dd / cnt
            o0, o1 = _pool_window(w)
            tot = jnp.zeros_like(dd)
            for o in range(o0, o1 + 1):
                tot = tot + _shift_rows(ec, -o, nc)[0]
            dz_ref[:, cs] = (tot - dd).astype(BF16)

    return pl.pallas_call(
        body, grid=(1,),
        in_specs=[pl.BlockSpec((T, BR_W), lambda i: (0, OFF_POOL // BR_W)),
                  pl.BlockSpec((T, BR_W), lambda i: (0, 0)),
                  pl.BlockSpec((N_GROUP, GC, GC), lambda i: (0, 0, 0)), _vec_spec(BR_W)],
        out_specs=[pl.BlockSpec((T, BR_W), lambda i: (0, 0)),
                   pl.BlockSpec((N_GROUP, GC, GC), lambda i: (0, 0, 0)), _vec_spec(BR_W)],
        out_shape=[jax.ShapeDtypeStruct((T, BR_W), BF16), jax.ShapeDtypeStruct((N_GROUP, GC, GC), F32),
                   jax.ShapeDtypeStruct((1, BR_W), F32)],
        name=name, compiler_params=_cp(("arbitrary",), VMEM_BIG),
    )(p, dy, pool_w, pool_scale)


_GELU_C = math.sqrt(2.0 / math.pi)


def _gelu(x):
    return 0.5 * x * (1.0 + jnp.tanh(_GELU_C * (x + 0.044715 * x * x * x)))


def _gelu_grad(x):
    th = jnp.tanh(_GELU_C * (x + 0.044715 * x * x * x))
    return 0.5 * (1.0 + th) + 0.5 * x * (1.0 - th * th) * _GELU_C * (1.0 + 3.0 * 0.044715 * x * x)


def _sgu_fwd(p, ln_g, ln_b, w_s, b_st, name):
    T = p.shape[0]

    def body(pu_ref, pv_ref, g_ref, b_ref, w_ref, bs_ref, o_ref):
        u = _gelu(pu_ref[...])
        vhat, _ = _ln_stats(_gelu(pv_ref[...]))
        vn = (vhat * g_ref[...] + b_ref[...]).astype(BF16)
        for g in range(N_GROUP):
            cs = slice(g * GC, (g + 1) * GC)
            s = jnp.dot(w_ref[g].astype(BF16), vn[:, cs], preferred_element_type=F32) + bs_ref[:, g:g + 1]
            o_ref[:, cs] = (u[:, cs] * s).astype(BF16)

    chunk = lambda off: pl.BlockSpec((GC, BR_W), lambda i: (i, off // BR_W))
    return pl.pallas_call(
        body, grid=(T // GC,),
        in_specs=[chunk(OFF_U), chunk(OFF_VG), _vec_spec(BR_W), _vec_spec(BR_W),
                  pl.BlockSpec((N_GROUP, GC, GC), lambda i: (0, 0, 0)),
                  pl.BlockSpec((GC, N_GROUP), lambda i: (0, 0))],
        out_specs=pl.BlockSpec((GC, BR_W), lambda i: (i, 0)),
        out_shape=jax.ShapeDtypeStruct((T, BR_W), BF16),
        name=name, compiler_params=_cp(("parallel",)),
    )(p, p, ln_g, ln_b, w_s, b_st)


def _sgu_bwd(p, dy, ln_g, ln_b, w_s, b_st, name):
    T = p.shape[0]

    def body(pu_ref, pv_ref, dy_ref, g_ref, b_ref, w_ref, bs_ref, dp_ref, dw_ref, dsacc_ref, dln_ref):
        i = pl.program_id(0)

        @pl.when(i == 0)
        def _():
            dw_ref[...] = jnp.zeros_like(dw_ref)
            dsacc_ref[...] = jnp.zeros_like(dsacc_ref)
            dln_ref[...] = jnp.zeros_like(dln_ref)

        pu, pv, dyv = pu_ref[...], pv_ref[...], dy_ref[...]
        u = _gelu(pu)
        vhat, rstd = _ln_stats(_gelu(pv))
        vn = (vhat * g_ref[...] + b_ref[...]).astype(BF16)
        ds = dyv * u
        dsb = ds.astype(BF16)
        dsacc_ref[...] += ds
        dvn_parts = []
        for g in range(N_GROUP):
            cs = slice(g * GC, (g + 1) * GC)
            wb = w_ref[g].astype(BF16)
            s = jnp.dot(wb, vn[:, cs], preferred_element_type=F32) + bs_ref[:, g:g + 1]
            dp_ref[:, cs] = (dyv[:, cs] * s * _gelu_grad(pu[:, cs])).astype(BF16)
            dvn_parts.append(lax.dot_general(wb, dsb[:, cs], (((0,), (0,)), ((), ())),
                                             preferred_element_type=F32))
            dw_ref[g] += lax.dot_general(dsb[:, cs], vn[:, cs], (((1,), (1,)), ((), ())),
                                         preferred_element_type=F32)
        dvn = jnp.concatenate(dvn_parts, axis=-1)
        dln_ref[pl.ds(0, 1), :] += jnp.sum(dvn * vhat, axis=0, keepdims=True)
        dln_ref[pl.ds(1, 1), :] += jnp.sum(dvn, axis=0, keepdims=True)
        dvhat = dvn * g_ref[...]
        m1 = jnp.mean(dvhat, axis=-1, keepdims=True)
        m2 = jnp.mean(dvhat * vhat, axis=-1, keepdims=True)
        dv = rstd * (dvhat - m1 - vhat * m2)
        dp_ref[:, BR_W:2 * BR_W] = (dv * _gelu_grad(pv)).astype(BF16)

    chunk = lambda off: pl.BlockSpec((GC, BR_W), lambda i: (i, off // BR_W))
    return pl.pallas_call(
        body, grid=(T // GC,),
        in_specs=[chunk(OFF_U), chunk(OFF_VG), pl.BlockSpec((GC, BR_W), lambda i: (i, 0)),
                  _vec_spec(BR_W), _vec_spec(BR_W),
                  pl.BlockSpec((N_GROUP, GC, GC), lambda i: (0, 0, 0)),
                  pl.BlockSpec((GC, N_GROUP), lambda i: (0, 0))],
        out_specs=[pl.BlockSpec((GC, 2 * BR_W), lambda i: (i, 0)),
                   pl.BlockSpec((N_GROUP, GC, GC), lambda i: (0, 0, 0)),
                   pl.BlockSpec((GC, BR_W), lambda i: (0, 0)),
                   pl.BlockSpec((SUBLANE, BR_W), lambda i: (0, 0))],
        out_shape=[jax.ShapeDtypeStruct((T, 2 * BR_W), BF16), jax.ShapeDtypeStruct((N_GROUP, GC, GC), F32),
                   jax.ShapeDtypeStruct((GC, BR_W), F32), jax.ShapeDtypeStruct((SUBLANE, BR_W), F32)],
        name=name, compiler_params=_cp(("arbitrary",)),
    )(p, p, dy, ln_g, ln_b, w_s, b_st)


def _conv_fwd(p, conv_w, nc, name):
    T = p.shape[0]

    def body(cb_ref, cc_ref, cx_ref, w_ref, o_ref):
        zz = cc_ref[...] * cx_ref[...]
        conv = (w_ref[0:1, :] * _shift_rows(zz, -1, nc)[0] + w_ref[1:2, :] * zz
                + w_ref[2:3, :] * _shift_rows(zz, 1, nc)[0])
        o_ref[...] = (cb_ref[...] * conv).astype(BF16)

    col = lambda off: pl.BlockSpec((T, GC), lambda j: (0, off // GC + j))
    return pl.pallas_call(
        body, grid=(N_GROUP,),
        in_specs=[col(OFF_CB), col(OFF_CC), col(OFF_CX), pl.BlockSpec((3, GC), lambda j: (0, j))],
        out_specs=pl.BlockSpec((T, GC), lambda j: (0, j)),
        out_shape=jax.ShapeDtypeStruct((T, BR_W), BF16),
        name=name, compiler_params=_cp(("parallel",)),
    )(p, p, p, conv_w)


def _conv_bwd(p, dy, conv_w, nc, name):
    T = p.shape[0]

    def body(cb_ref, cc_ref, cx_ref, dy_ref, w_ref, dp_ref, dw_ref):
        cc, cx, dyv = cc_ref[...], cx_ref[...], dy_ref[...]
        zz = cc * cx
        zm, zp = _shift_rows(zz, -1, nc)[0], _shift_rows(zz, 1, nc)[0]
        conv = w_ref[0:1, :] * zm + w_ref[1:2, :] * zz + w_ref[2:3, :] * zp
        dconv = dyv * cb_ref[...]
        dw_ref[0:1, :] = jnp.sum(dconv * zm, axis=0, keepdims=True)
        dw_ref[1:2, :] = jnp.sum(dconv * zz, axis=0, keepdims=True)
        dw_ref[2:3, :] = jnp.sum(dconv * zp, axis=0, keepdims=True)
        dzz = (w_ref[0:1, :] * _shift_rows(dconv, 1, nc)[0] + w_ref[1:2, :] * dconv
               + w_ref[2:3, :] * _shift_rows(dconv, -1, nc)[0])
        dp_ref[0] = (dyv * conv).astype(BF16)
        dp_ref[1] = (dzz * cx).astype(BF16)
        dp_ref[2] = (dzz * cc).astype(BF16)

    col = lambda off: pl.BlockSpec((T, GC), lambda j: (0, off // GC + j))
    return pl.pallas_call(
        body, grid=(N_GROUP,),
        in_specs=[col(OFF_CB), col(OFF_CC), col(OFF_CX), pl.BlockSpec((T, GC), lambda j: (0, j)),
                  pl.BlockSpec((3, GC), lambda j: (0, j))],
        out_specs=[pl.BlockSpec((3, T, GC), lambda j: (0, 0, j)), pl.BlockSpec((3, GC), lambda j: (0, j))],
        out_shape=[jax.ShapeDtypeStruct((3, T, BR_W), BF16), jax.ShapeDtypeStruct((3, BR_W), F32)],
        name=name, compiler_params=_cp(("parallel",)),
    )(p, p, p, dy, conv_w)


def _rows_tile(rows, cols, n_arrays):
    budget = 24 * 1024 * 1024 // (2 * 4 * n_arrays * cols)
    return _tile(rows, max(SUBLANE * 2, min(budget, 1024)), SUBLANE * 2) if rows % (SUBLANE * 2) == 0 else rows


def _cast_into(chip, w, l, name):
    _, K, cols = w.shape
    tr = _rows_tile(K, cols, 2)

    def body(q_ref, w_ref, o_ref):
        o_ref[...] = w_ref[...].astype(BF16)

    return pl.pallas_call(
        body,
        grid_spec=pltpu.PrefetchScalarGridSpec(
            num_scalar_prefetch=1, grid=(K // tr,),
            in_specs=[pl.BlockSpec((None, tr, cols), lambda i, q: (l, i, 0))],
            out_specs=pl.BlockSpec((None, tr, cols), lambda i, q: (q[0], i, 0))),
        out_shape=jax.ShapeDtypeStruct((N_CHIP, K, cols), BF16),
        name=name, compiler_params=_cp(("parallel",)),
    )(chip, w)


def _adamw(w, g, m, v, name):
    shape = w.shape
    cols = shape[-1]
    rows = w.size // cols
    tr = _rows_tile(rows, cols, 7)
    c1 = 1.0 - ADAM_B1 ** ADAM_STEP
    c2 = 1.0 - ADAM_B2 ** ADAM_STEP

    def body(w_ref, g_ref, m_ref, v_ref, d_ref, mo_ref, vo_ref):
        gv = g_ref[...]
        mn = ADAM_B1 * m_ref[...] + (1.0 - ADAM_B1) * gv
        vn = ADAM_B2 * v_ref[...] + (1.0 - ADAM_B2) * (gv * gv)
        mo_ref[...] = mn
        vo_ref[...] = vn
        d_ref[...] = -ADAM_LR * ((mn / c1) / (jnp.sqrt(vn / c2) + ADAM_EPS) + ADAM_WD * w_ref[...])

    rs = pl.BlockSpec((tr, cols), lambda i: (i, 0))
    outs = pl.pallas_call(
        body, grid=(rows // tr,), in_specs=[rs] * 4, out_specs=[rs] * 3,
        out_shape=[jax.ShapeDtypeStruct((rows, cols), F32)] * 3,
        name=name, compiler_params=_cp(("parallel",)),
    )(*[a.reshape(rows, cols) for a in (w, g, m, v)])
    return [o.reshape(shape) for o in outs]


def _adamw_layer(w, g, m, v, l, prev, after, name):
    L, K, cols = w.shape
    tr = _rows_tile(K, cols, 7)
    c1 = 1.0 - ADAM_B1 ** ADAM_STEP
    c2 = 1.0 - ADAM_B2 ** ADAM_STEP
    nprev = 0 if prev is None else 3

    def body(*refs):
        w_ref, g_ref, m_ref, v_ref = refs[:4]
        d_ref, mo_ref, vo_ref = refs[-3:]
        gv = g_ref[...]
        mn = ADAM_B1 * m_ref[...] + (1.0 - ADAM_B1) * gv
        vn = ADAM_B2 * v_ref[...] + (1.0 - ADAM_B2) * (gv * gv)
        mo_ref[...] = mn
        vo_ref[...] = vn
        d_ref[...] = -ADAM_LR * ((mn / c1) / (jnp.sqrt(vn / c2) + ADAM_EPS) + ADAM_WD * w_ref[...])

    rs = pl.BlockSpec((None, tr, cols), lambda i: (l, i, 0))
    extra = list(prev or ()) + list(after)
    return pl.pallas_call(
        body, grid=(K // tr,), in_specs=[rs] * 4 + [_ANY] * len(extra), out_specs=[rs] * 3,
        out_shape=[jax.ShapeDtypeStruct((L, K, cols), F32)] * 3,
        input_output_aliases={4 + k: k for k in range(nprev)},
        name=name, compiler_params=_cp(("parallel",)),
    )(w, g, m, v, *extra)


def _pair_sum(core, dw, recv, name):
    nq, half, cols = recv.shape
    tr = _rows_tile(half, cols, 3)
    nb = half // tr

    def body(c_ref, a_ref, r_ref, o_ref):
        o_ref[...] = (a_ref[...].astype(F32) + r_ref[...].astype(F32)).astype(BF16)

    rs = pl.BlockSpec((None, tr, cols), lambda q, i, c: (q, i, 0))
    return pl.pallas_call(
        body,
        grid_spec=pltpu.PrefetchScalarGridSpec(
            num_scalar_prefetch=1, grid=(nq, nb),
            in_specs=[pl.BlockSpec((None, tr, cols), lambda q, i, c: (q, c[0] * nb + i, 0)), rs],
            out_specs=rs),
        out_shape=jax.ShapeDtypeStruct((nq, half, cols), BF16),
        name=name, compiler_params=_cp(("parallel", "parallel")),
    )(core, dw, recv)


def _chip_sum(chip, core, psum, land, l, n_layers, prev, name):
    _, half, cols = psum.shape
    tr = _rows_tile(half, cols, 5)
    nb = half // tr

    def body(*refs):
        p_ref, r_ref, o_ref = refs[2], refs[3], refs[-1]
        o_ref[...] = (p_ref[...].astype(F32) + r_ref[0].astype(F32) + r_ref[1].astype(F32)
                      + r_ref[2].astype(F32))

    extra = [] if prev is None else [prev]
    return pl.pallas_call(
        body,
        grid_spec=pltpu.PrefetchScalarGridSpec(
            num_scalar_prefetch=2, grid=(nb,),
            in_specs=[pl.BlockSpec((None, tr, cols), lambda i, q, c: (q[0], i, 0)),
                      pl.BlockSpec((3, tr, cols), lambda i, q, c: (0, i, 0))] + [_ANY] * len(extra),
            out_specs=pl.BlockSpec((None, tr, cols), lambda i, q, c: (l, c[0] * nb + i, 0))),
        out_shape=jax.ShapeDtypeStruct((n_layers, 2 * half, cols), F32),
        input_output_aliases={4: 0} if prev is not None else {},
        name=name, compiler_params=_cp(("parallel",)),
    )(chip, core, psum, land, *extra)


def _slab_sum(gathered, name):
    n, rows, cols = gathered.shape
    tr = _tile(rows, 512, SUBLANE)

    def body(g_ref, o_ref):
        acc = g_ref[0]
        for d in range(1, n):
            acc = acc + g_ref[d]
        o_ref[...] = acc

    return pl.pallas_call(
        body, grid=(rows // tr,),
        in_specs=[pl.BlockSpec((n, tr, cols), lambda i: (0, i, 0))],
        out_specs=pl.BlockSpec((tr, cols), lambda i: (i, 0)),
        out_shape=jax.ShapeDtypeStruct((rows, cols), F32),
        name=name, compiler_params=_cp(("parallel",)),
    )(gathered)


def _silu(x):
    return x * jax.nn.sigmoid(x)


def _ada_fwd(cvec, w_ada, b_cols, name):
    L, D, ns = w_ada.shape
    tn = _tile(ns, 768)

    def body(c_ref, w_ref, b_ref, o_ref):
        s = _silu(c_ref[...]).astype(BF16)
        o_ref[...] = jnp.dot(s, w_ref[...].astype(BF16), preferred_element_type=F32) + b_ref[...]

    return pl.pallas_call(
        body, grid=(L, ns // tn),
        in_specs=[pl.BlockSpec((16, D), lambda l, j: (0, 0)),
                  pl.BlockSpec((None, D, tn), lambda l, j: (l, 0, j)),
                  pl.BlockSpec((None, 1, tn), lambda l, j: (l, 0, j))],
        out_specs=pl.BlockSpec((None, 16, tn), lambda l, j: (l, 0, j)),
        out_shape=jax.ShapeDtypeStruct((L, 16, ns), F32),
        name=name, compiler_params=_cp(("parallel", "parallel")),
    )(cvec, w_ada, b_cols)


def _ada_bwd(cvec, dmod, w_ada, name):
    L, D, ns = w_ada.shape
    tn = _tile(ns, 768)

    def body(c_ref, d_ref, w_ref, gw_ref, ds_ref):
        first = jnp.logical_and(pl.program_id(0) == 0, pl.program_id(1) == 0)

        @pl.when(first)
        def _():
            ds_ref[...] = jnp.zeros_like(ds_ref)

        s = _silu(c_ref[...]).astype(BF16)
        db = d_ref[...].astype(BF16)
        gw_ref[...] = lax.dot_general(s, db, (((0,), (0,)), ((), ())), preferred_element_type=F32)
        ds_ref[...] += lax.dot_general(db, w_ref[...].astype(BF16), (((1,), (1,)), ((), ())),
                                       preferred_element_type=F32)

    return pl.pallas_call(
        body, grid=(L, ns // tn),
        in_specs=[pl.BlockSpec((16, D), lambda l, j: (0, 0)),
                  pl.BlockSpec((None, 16, tn), lambda l, j: (l, 0, j)),
                  pl.BlockSpec((None, D, tn), lambda l, j: (l, 0, j))],
        out_specs=[pl.BlockSpec((None, D, tn), lambda l, j: (l, 0, j)),
                   pl.BlockSpec((16, D), lambda l, j: (0, 0))],
        out_shape=[jax.ShapeDtypeStruct((L, D, ns), F32), jax.ShapeDtypeStruct((16, D), F32)],
        name=name, compiler_params=_cp(("arbitrary", "arbitrary")),
    )(cvec, dmod, w_ada)


def _cctx_grad(gathered, c_ctx, name):
    rows = c_ctx.shape[0]

    def body(g_ref, c_ref, o_ref):
        ds = g_ref[0] + g_ref[2] + g_ref[4] + g_ref[6]
        cv = c_ref[...]
        sig = jax.nn.sigmoid(cv)
        o_ref[...] = ds * sig * (1.0 + cv * (1.0 - sig))

    return pl.pallas_call(
        body, out_shape=jax.ShapeDtypeStruct((rows, LANE), F32), name=name,
    )(gathered, c_ctx)


def _place():
    x, y, c = lax.axis_index("x"), lax.axis_index("y"), lax.axis_index("c")
    chips = [(1 - x, y), (x, 1 - y), (1 - x, 1 - y)]
    return x, y, c, chips


def _all_gather_small(slab, name):
    rows, cols = slab.shape

    def body(x_ref, out_ref, send_sems, recv_sems, local_sem):
        x, y, c, chips = _place()
        me, sibling = (x, y, c), (x, y, 1 - c)

        def blk(px, py, pc):
            return out_ref.at[4 * px + 2 * py + pc]

        def copy(k, block, to, src=None):
            return pltpu.make_async_remote_copy(
                src_ref=blk(*block) if src is None else src, dst_ref=blk(*block),
                send_sem=send_sems.at[k], recv_sem=recv_sems.at[k], device_id=to, device_id_type=MESH)

        mine = pltpu.make_async_copy(x_ref, blk(*me), local_sem)
        mine.start()
        first = [copy(0, me, sibling, src=x_ref)]
        first += [copy(1 + j, me, (*chip, c), src=x_ref) for j, chip in enumerate(chips)]
        for cp in first:
            cp.start()
        passed = [copy(4 + j, (*chip, c), sibling) for j, chip in enumerate(chips)]
        for j, chip in enumerate(chips):
            copy(1 + j, (*chip, c), me).wait_recv()
            passed[j].start()
        copy(0, sibling, me).wait_recv()
        for j, chip in enumerate(chips):
            copy(4 + j, (*chip, 1 - c), me).wait_recv()
        for cp in first + passed:
            cp.wait_send()
        mine.wait()

    return pl.pallas_call(
        body, out_shape=jax.ShapeDtypeStruct((N_DEV, rows, cols), slab.dtype),
        in_specs=[pl.BlockSpec(memory_space=pltpu.VMEM)],
        out_specs=pl.BlockSpec(memory_space=pltpu.VMEM),
        scratch_shapes=[pltpu.SemaphoreType.DMA((7,)), pltpu.SemaphoreType.DMA((7,)), pltpu.SemaphoreType.DMA],
        name=name, compiler_params=pltpu.CompilerParams(vmem_limit_bytes=VMEM_MID),
    )(slab)


def _hbm(a):
    return pltpu.with_memory_space_constraint(a, pltpu.HBM)


def _half_rows(ref, q, c):
    half = ref.shape[1] // 2
    return ref.at[q, pl.ds(c * half, half)]


def _ici_copy(src, dst, send_sems, recv_sems, k, to):
    return pltpu.make_async_remote_copy(src_ref=src, dst_ref=dst, send_sem=send_sems.at[k], recv_sem=recv_sems.at[k],
                                        device_id=to, device_id_type=MESH)


def _gather_start(bufs, sizes, name):
    n = len(bufs)
    ng = len(sizes)

    def body(*refs):
        ins = refs[:n]
        sems = refs[n:n + 2 * ng]
        token = refs[-1]
        x, y, c, chips = _place()
        a = 0
        for g, sz in enumerate(sizes):
            for i in range(sz):
                part = _half_rows(ins[a], 2 * x + y, c)
                for j, chip in enumerate(chips):
                    _ici_copy(part, part, sems[2 * g], sems[2 * g + 1], 3 * i + j, (*chip, c)).start()
                a += 1
        token[...] = jnp.zeros_like(token)

    sem_shapes = []
    for sz in sizes:
        sem_shapes += [pltpu.SemaphoreType.DMA((3 * sz,)), pltpu.SemaphoreType.DMA((3 * sz,))]
    outs = pl.pallas_call(
        body, name=name,
        out_shape=tuple(sem_shapes) + tuple(pltpu.HBM(b.shape, b.dtype) for b in bufs)
                  + (jax.ShapeDtypeStruct((SUBLANE, LANE), F32),),
        in_specs=[_HBM] * n,
        out_specs=tuple([_SEM] * (2 * ng) + [_HBM] * n + [pl.BlockSpec(memory_space=pltpu.VMEM)]),
        input_output_aliases={i: 2 * ng + i for i in range(n)},
        compiler_params=pltpu.CompilerParams(has_side_effects=_EFFECT),
    )(*[_hbm(b) for b in bufs])
    sems = [(outs[2 * g], outs[2 * g + 1]) for g in range(ng)]
    return sems, list(outs[2 * ng:2 * ng + n]), outs[-1]


def _gather_wait(bufs, sems, after, name):
    n = len(bufs)
    na = len(after)

    def body(*refs):
        ins = refs[:n]
        send_sems, recv_sems = refs[n], refs[n + 1]
        x, y, c, chips = _place()
        for i in range(n):
            sent = _half_rows(ins[i], 2 * x + y, c)
            for j, (cx, cy) in enumerate(chips):
                cp = _ici_copy(sent, _half_rows(ins[i], 2 * cx + cy, c), send_sems, recv_sems, 3 * i + j, (cx, cy, c))
                cp.wait_send()
                cp.wait_recv()

    outs = pl.pallas_call(
        body, name=name,
        out_shape=tuple(pltpu.HBM(b.shape, b.dtype) for b in bufs),
        in_specs=[_HBM] * n + [_SEM, _SEM] + [_ANY] * na,
        out_specs=tuple([_HBM] * n),
        input_output_aliases={i: i for i in range(n)},
        compiler_params=pltpu.CompilerParams(has_side_effects=_EFFECT),
    )(*bufs, sems[0], sems[1], *after)
    return list(outs)


def _exchange_halves(bufs, name):
    n = len(bufs)

    def body(*refs):
        ins = refs[:n]
        send_sems, recv_sems = refs[-2:]
        x, y, c, chips = _place()
        sends = []
        for i in range(n):
            for j, (cx, cy) in enumerate(chips):
                part = _half_rows(ins[i], 2 * cx + cy, c)
                cp = _ici_copy(part, part, send_sems, recv_sems, 3 * i + j, (x, y, 1 - c))
                cp.start()
                sends.append(cp)
        for i in range(n):
            for j, (cx, cy) in enumerate(chips):
                part = _half_rows(ins[i], 2 * cx + cy, 1 - c)
                _ici_copy(part, part, send_sems, recv_sems, 3 * i + j, (x, y, 1 - c)).wait_recv()
        for cp in sends:
            cp.wait_send()

    outs = pl.pallas_call(
        body, name=name,
        out_shape=[jax.ShapeDtypeStruct(b.shape, b.dtype) for b in bufs],
        in_specs=[_ANY] * n, out_specs=[_ANY] * n,
        input_output_aliases={i: i for i in range(n)},
        scratch_shapes=[pltpu.SemaphoreType.DMA((3 * n,)), pltpu.SemaphoreType.DMA((3 * n,))],
    )(*bufs)
    return list(outs)


def _other_half(ref, c):
    half = ref.shape[1] // 2
    return ref.at[:, pl.ds((1 - c) * half, half)]


def _prereduce_exchange(dws, name):
    n = len(dws)

    def body(*refs):
        ins, outs = refs[:n], refs[n:2 * n]
        send_sems, recv_sems = refs[2 * n:]
        x, y, c, _ = _place()
        cps = []
        for i in range(n):
            cp = _ici_copy(_other_half(ins[i], c), outs[i], send_sems, recv_sems, i, (x, y, 1 - c))
            cp.start()
            cps.append(cp)
        for cp in cps:
            cp.wait()

    return pl.pallas_call(
        body, name=name,
        out_shape=[jax.ShapeDtypeStruct((d.shape[0], d.shape[1] // 2, d.shape[2]), d.dtype) for d in dws],
        in_specs=[_ANY] * n, out_specs=[_ANY] * n,
        scratch_shapes=[pltpu.SemaphoreType.DMA((n,)), pltpu.SemaphoreType.DMA((n,))],
    )(*dws)


def _scatter_start(psums, name):
    n = len(psums)

    def body(*refs):
        srcs, lands = refs[:n], refs[n:2 * n]
        send_sems, recv_sems = refs[2 * n], refs[2 * n + 1]
        token = refs[-1]
        x, y, c, chips = _place()
        for i in range(n):
            for j, (cx, cy) in enumerate(chips):
                _ici_copy(srcs[i].at[2 * cx + cy], lands[i].at[j], send_sems, recv_sems, 3 * i + j, (cx, cy, c)).start()
        token[...] = jnp.zeros_like(token)

    land_shapes = [(3,) + p.shape[1:] for p in psums]
    outs = pl.pallas_call(
        body, name=name,
        out_shape=(pltpu.SemaphoreType.DMA((3 * n,)), pltpu.SemaphoreType.DMA((3 * n,)))
                  + tuple(pltpu.HBM(p.shape, p.dtype) for p in psums)
                  + tuple(pltpu.HBM(s, BF16) for s in land_shapes)
                  + (jax.ShapeDtypeStruct((SUBLANE, LANE), F32),),
        in_specs=[_HBM] * (2 * n),
        out_specs=tuple([_SEM, _SEM] + [_HBM] * (2 * n) + [pl.BlockSpec(memory_space=pltpu.VMEM)]),
        input_output_aliases={i: 2 + i for i in range(2 * n)},
        compiler_params=pltpu.CompilerParams(has_side_effects=_EFFECT),
    )(*[_hbm(p) for p in psums], *[_hbm(lax.empty(s, BF16)) for s in land_shapes])
    return (outs[0], outs[1]), list(outs[2:2 + n]), list(outs[2 + n:2 + 2 * n]), outs[-1]


def _scatter_wait(psums, lands, sems, after, name):
    n = len(psums)
    na = len(after)

    def body(*refs):
        srcs, lnds = refs[:n], refs[n:2 * n]
        send_sems, recv_sems = refs[2 * n], refs[2 * n + 1]
        x, y, c, chips = _place()
        for i in range(n):
            for j, (cx, cy) in enumerate(chips):
                cp = _ici_copy(srcs[i].at[2 * cx + cy], lnds[i].at[j], send_sems, recv_sems, 3 * i + j, (cx, cy, c))
                cp.wait_send()
                cp.wait_recv()

    outs = pl.pallas_call(
        body, name=name,
        out_shape=tuple(pltpu.HBM(a.shape, a.dtype) for a in list(psums) + list(lands)),
        in_specs=[_HBM] * (2 * n) + [_SEM, _SEM] + [_ANY] * na,
        out_specs=tuple([_HBM] * (2 * n)),
        input_output_aliases={i: i for i in range(2 * n)},
        compiler_params=pltpu.CompilerParams(has_side_effects=_EFFECT),
    )(*psums, *lands, sems[0], sems[1], *after)
    return list(outs[:n]), list(outs[n:])


def _exchange_grads(grads, l, name):
    n = len(grads)

    def body(*refs):
        ins = refs[:n]
        send_sems, recv_sems = refs[-2:]
        x, y, c, _ = _place()

        def rows(ref, core):
            half = ref.shape[1] // 2
            return ref.at[l, pl.ds(core * half, half)]

        cps = []
        for i in range(n):
            cp = _ici_copy(rows(ins[i], c), rows(ins[i], c), send_sems, recv_sems, i, (x, y, 1 - c))
            cp.start()
            cps.append(cp)
        for i, cp in enumerate(cps):
            cp.wait_send()
            _ici_copy(rows(ins[i], 1 - c), rows(ins[i], 1 - c), send_sems, recv_sems, i, (x, y, 1 - c)).wait_recv()

    outs = pl.pallas_call(
        body, name=name,
        out_shape=[jax.ShapeDtypeStruct(g.shape, g.dtype) for g in grads],
        in_specs=[_ANY] * n, out_specs=[_ANY] * n,
        input_output_aliases={i: i for i in range(n)},
        scratch_shapes=[pltpu.SemaphoreType.DMA((n,)), pltpu.SemaphoreType.DMA((n,))],
    )(*grads)
    return list(outs)


def _pack(arrs):
    flat = jnp.concatenate([a.reshape(-1).astype(F32) for a in arrs])
    pad = (-flat.shape[0]) % (SUBLANE * LANE)
    return jnp.pad(flat, (0, pad)).reshape(-1, LANE)


def _unpack(slab, shapes):
    flat = slab.reshape(-1)
    out, off = [], 0
    for s in shapes:
        n = math.prod(s)
        out.append(flat[off:off + n].reshape(s))
        off += n
    return out


def _split6(v):
    d = v.shape[-1] // 6
    return [v[:, k * d:(k + 1) * d].reshape(2, 1, d) for k in range(6)]


BIG = ("w_in", "w_br_attn", "w_br_pool", "w_br_sgu", "w_br_conv", "w_gate", "w_o", "w_ff_gate", "w_ff_up", "w_ff_down")
KIND = {"w_in": "col", "w_br_attn": "col", "w_br_pool": "col", "w_br_sgu": "col", "w_br_conv": "col",
        "w_gate": "col", "w_o": "row", "w_ff_gate": "col", "w_ff_up": "col", "w_ff_down": "row"}
SMALL = ("c_ctx", "b_ada", "q_norm_g", "k_norm_g", "pool_w", "pool_scale", "sgu_ln_g", "sgu_ln_b", "sgu_w",
         "sgu_b", "conv_w", "b_gate", "ln1_g", "ln1_b", "ln2_g", "ln2_b")
WEIGHTS = ("c_ctx", "w_ada", "b_ada", "w_in", "q_norm_g", "k_norm_g", "pool_w", "pool_scale", "sgu_ln_g", "sgu_ln_b",
           "sgu_w", "sgu_b", "conv_w", "w_br_attn", "w_br_pool", "w_br_sgu", "w_br_conv", "w_gate", "b_gate", "w_o",
           "ln1_g", "ln1_b", "w_ff_gate", "w_ff_up", "w_ff_down", "ln2_g", "ln2_b")


def _step(x, c, ctx, loss_target, W, M, V):
    L = W["w_ada"].shape[0]
    assert L == 2, "core c of a chip carries layer c of the weight traffic"
    N, D = x.shape[1], x.shape[2]
    NC = ctx.shape[1]
    T = NC + N
    FF = W["w_ff_down"].shape[1] * N_CHIP
    assert NC % ROW_TM == 0 and N % ROW_TM == 0 and N % GRID_W == 0 and D % LANE == 0
    ncb = NC // ROW_TM
    nblk = T // ROW_TM
    alpha = (2 * L) ** 0.25
    ax, ay, ac = lax.axis_index("x"), lax.axis_index("y"), lax.axis_index("c")
    chip = 2 * ax + ay
    dev = 2 * chip + ac
    chip_arr = jnp.reshape(chip, (1,)).astype(jnp.int32)
    core_arr = jnp.reshape(ac, (1,)).astype(jnp.int32)
    ns_ada = W["w_ada"].shape[2]
    chip_devs = (0, 2, 4, 6)

    first = ("w_in", "w_gate")
    order = [(k, 0) for k in first] + [(k, 0) for k in BIG if k not in first] + [(k, 1) for k in BIG]
    sizes = (len(first), len(BIG) - len(first), len(BIG))
    gsems, flying, _ = _gather_start([_cast_into(chip_arr, W[k], l, "cast_%s_l%d" % (k, l)) for k, l in order],
                                     sizes, "gather_start")
    full = {k: [None] * L for k in BIG}

    def weights_ready(g, after):
        lo = sum(sizes[:g])
        got = _gather_wait(flying[lo:lo + sizes[g]], gsems[g], after, "gather_wait_%d" % g)
        for (k, l), buf in zip(order[lo:lo + sizes[g]], _exchange_halves(got, "exchange_halves_%d" % g)):
            full[k][l] = buf

    conv_shape = W["conv_w"].shape
    g0 = _all_gather_small(_pack([c, W["conv_w"]]), "gather_c")
    c_all = g0[:, :D // LANE, :].reshape(N_DEV, D)
    conv_parts = [g0[d].reshape(-1)[D:D + math.prod(conv_shape)].reshape(conv_shape) for d in chip_devs]
    conv_full = jnp.concatenate(conv_parts, axis=-1)
    cvec = jnp.concatenate([c_all, W["c_ctx"][None], jnp.zeros((16 - N_DEV - 1, D), F32)], axis=0)
    b_cols = lax.dynamic_slice_in_dim(W["b_ada"], chip * ns_ada, ns_ada, axis=1).reshape(L, 1, ns_ada)
    mod_part = _ada_fwd(cvec, W["w_ada"], b_cols, "ada_fwd")
    g1 = _all_gather_small(mod_part.reshape(-1, LANE), "gather_mod")
    mod_all = jnp.concatenate([g1[d].reshape(L, 16, ns_ada) for d in chip_devs], axis=-1)
    mod_lat = lax.dynamic_index_in_dim(mod_all, dev, axis=1, keepdims=True)
    mods = jnp.concatenate([mod_all[:, N_DEV:N_DEV + 1], mod_lat], axis=1)

    cos, sin = _rope_tables(N, NC)
    x0 = jnp.concatenate([ctx[0], x[0]], axis=0)

    saved = []
    xin = x0
    h1 = None
    dy_top = loss_parts = None
    for l in range(L):
        sh1, sc1, g1m, sh2, sc2, g2m = _split6(mods[l])
        tag = "_l%d" % l
        if l == 0:
            h1 = _modulate(xin, sc1, sh1, ncb, "modulate" + tag)
            weights_ready(0, [h1])
        p = _mm_nn(h1, full["w_in"], l, "col", name="mm_in" + tag)
        qg, kg = W["q_norm_g"][l][None], W["k_norm_g"][l][None]
        q, k, v = _qk_fwd(p, cos, sin, qg, kg, "qk_fwd" + tag)
        att = _attn_fwd(q, k, v, NC, "attn_fwd" + tag)
        ps = W["pool_scale"][l][None]
        ypool = _pool_fwd(p, W["pool_w"][l], ps, NC, "pool_fwd" + tag)
        lg, lb = W["sgu_ln_g"][l][None], W["sgu_ln_b"][l][None]
        b_st = W["sgu_b"][l].T
        ysgu = _sgu_fwd(p, lg, lb, W["sgu_w"][l], b_st, "sgu_fwd" + tag)
        yconv = _conv_fwd(p, conv_full[l], NC, "conv_fwd" + tag)
        brs = (att, ypool, ysgu, yconv)
        gates = _mm_nn(h1, full["w_gate"], l, "col", name="mm_gate" + tag, bias=W["b_gate"][l][None],
                       act=jax.nn.sigmoid)
        if l == 0:
            weights_ready(1, [yconv, gates])
        ts = [_mm_nn(b, full[wk], l, "col", name="mm_" + wk + tag)
              for b, wk in zip(brs, ("w_br_attn", "w_br_pool", "w_br_sgu", "w_br_conv"))]
        mg = _merge_fwd(gates, ts, "merge_fwd" + tag)
        m = _mm_nn(mg, full["w_o"], l, "row", name="mm_o" + tag, tn_pref=1024)
        ln1g, ln1b = W["ln1_g"][l][None], W["ln1_b"][l][None]
        x1, h2 = _resid_ln(xin, m, g1m, ln1g, ln1b, sc2, sh2, ncb, alpha, "resid_ln1" + tag)
        gg = _mm_nn(h2, full["w_ff_gate"], l, "col", name="mm_ffg" + tag, tn_pref=1408)
        uu = _mm_nn(h2, full["w_ff_up"], l, "col", name="mm_ffu" + tag, tn_pref=1408)
        act = _swiglu_fwd(gg, uu, "swiglu_fwd" + tag)
        ff = _mm_nn(act, full["w_ff_down"], l, "row", name="mm_ffd" + tag, tm_pref=384, tn_pref=512)
        if l == 0:
            weights_ready(2, [ff])
        ln2g, ln2b = W["ln2_g"][l][None], W["ln2_b"][l][None]
        saved.append(dict(xin=xin, h1=h1, p=p, q=q, k=k, v=v, brs=brs, gates=gates, ts=ts, mg=mg, m=m, x1=x1, h2=h2,
                          gg=gg, uu=uu, act=act, ff=ff))
        if l + 1 < L:
            nsh1, nsc1 = _split6(mods[l + 1])[:2]
            xin, h1 = _resid_ln(x1, ff, g2m, ln2g, ln2b, nsc1, nsh1, ncb, alpha, "resid_ln2" + tag)
        else:
            dy_top, loss_parts = _resid_ln_loss(x1, ff, g2m, ln2g, ln2b, loss_target[0], ncb, alpha,
                                                "resid_ln2_loss" + tag)
    loss_dev = jnp.sum(loss_parts[:, 0, 0])

    def lat_ctx(part_rows):
        return jnp.stack([jnp.sum(part_rows[:ncb], axis=0), jnp.sum(part_rows[ncb:], axis=0)])

    dW = {k: [None] * L for k in BIG}
    small = {k: [None] * L for k in ("q_norm_g", "k_norm_g", "pool_w", "pool_scale", "sgu_ln_g", "sgu_ln_b", "sgu_w",
                                     "sgu_b", "conv_w", "b_gate", "ln1_g", "ln1_b", "ln2_g", "ln2_b")}
    dmods = [None] * L
    ffn_keys = ("w_ff_down", "w_ff_gate", "w_ff_up")
    mix_keys = tuple(k for k in BIG if k not in ffn_keys)
    grads_big = {k: None for k in BIG}
    adam_out = {k: None for k in BIG}

    def reduce_start(keys, l, tg):
        dws = [dW[k][l] for k in keys]
        recv = _prereduce_exchange(dws, "prereduce_" + tg)
        psums = [_pair_sum(core_arr, d, r, "pair_sum_%s_l%d" % (k, l)) for k, d, r in zip(keys, dws, recv)]
        sems, src, land, token = _scatter_start(psums, "scatter_start_" + tg)
        return dict(keys=keys, l=l, sems=sems, src=src, land=land, token=token, tg=tg)

    def reduce_finish(st, after):
        src, land = _scatter_wait(st["src"], st["land"], st["sems"], after, "scatter_wait_" + st["tg"])
        for k, p_, r in zip(st["keys"], src, land):
            grads_big[k] = _chip_sum(chip_arr, core_arr, p_, r, st["l"], L, grads_big[k],
                                     "chip_sum_%s_l%d" % (k, st["l"]))
        done = _exchange_grads([grads_big[k] for k in st["keys"]], st["l"], "exchange_grads_" + st["tg"])
        grads_big.update(zip(st["keys"], done))

    def adam(keys, l, after):
        for k in keys:
            adam_out[k] = _adamw_layer(W[k], grads_big[k], M[k], V[k], l, adam_out[k], after,
                                       "adamw_%s_l%d" % (k, l))

    dxa, dhs, sc_prev = dy_top, [], None
    red_l1 = red_ffn = None
    for l in reversed(range(L)):
        s = saved[l]
        sh1, sc1, g1m, sh2, sc2, g2m = _split6(mods[l])
        tag = "_l%d" % l
        ln1g, ln1b = W["ln1_g"][l][None], W["ln1_b"][l][None]
        ln2g, ln2b = W["ln2_g"][l][None], W["ln2_b"][l][None]
        dx1a, dff, part2 = _ln_bwd(dxa, dhs, sc_prev if dhs else sc1, s["x1"], s["ff"], g2m, ln2g, ln2b, ncb, alpha,
                                   "ln2_bwd" + tag, after=None if red_l1 is None else red_l1["token"])
        small["ln2_g"][l] = jnp.sum(part2[:, 0], axis=0)
        small["ln2_b"][l] = jnp.sum(part2[:, 1], axis=0)
        dg2 = lat_ctx(part2[:, 2])
        if dhs:
            dmods[l + 1][1], dmods[l + 1][0] = lat_ctx(part2[:, 3]), lat_ctx(part2[:, 4])
        da = _mm_nt(dff, full["w_ff_down"], l, "row", name="nt_ffd" + tag)
        dW["w_ff_down"][l] = _mm_tn(s["act"], dff, "row", nq=N_CHIP, kdim=FF // N_CHIP, ndim=D, name="tn_ffd" + tag,
                                    tk_pref=1408, tn_pref=512)
        dgg, duu = _swiglu_bwd(da, s["gg"], s["uu"], "swiglu_bwd" + tag)
        dh2a = _mm_nt(dgg, full["w_ff_gate"], l, "col", name="nt_ffg" + tag)
        dh2b = _mm_nt(duu, full["w_ff_up"], l, "col", name="nt_ffu" + tag)
        dW["w_ff_gate"][l] = _mm_tn(s["h2"], dgg, "col", nq=N_CHIP, kdim=D, ndim=FF // N_CHIP, name="tn_ffg" + tag,
                                    tn_pref=1408)
        dW["w_ff_up"][l] = _mm_tn(s["h2"], duu, "col", nq=N_CHIP, kdim=D, ndim=FF // N_CHIP, name="tn_ffu" + tag,
                                  tn_pref=1408)
        if l == 0:
            red_ffn = reduce_start(ffn_keys, 0, "l0_ffn")
            reduce_finish(red_l1, [red_ffn["token"]])
        dx0a, dm, part1 = _ln_bwd(dx1a, [dh2a, dh2b], sc2, s["xin"], s["m"], g1m, ln1g, ln1b, ncb, alpha,
                                  "ln1_bwd" + tag, after=None if red_ffn is None else red_ffn["token"])
        small["ln1_g"][l] = jnp.sum(part1[:, 0], axis=0)
        small["ln1_b"][l] = jnp.sum(part1[:, 1], axis=0)
        dg1 = lat_ctx(part1[:, 2])
        dsc2, dsh2 = lat_ctx(part1[:, 3]), lat_ctx(part1[:, 4])
        dmods[l] = [None, None, dg1, dsh2, dsc2, dg2]
        dmg = _mm_nt(dm, full["w_o"], l, "row", name="nt_o" + tag)
        dW["w_o"][l] = _mm_tn(s["mg"], dm, "row", nq=N_CHIP, kdim=D // N_CHIP, ndim=D, name="tn_o" + tag, tn_pref=1024)
        dpre, dt0, dt1, dt2, dt3, bpart = _merge_bwd(dmg, s["gates"], s["ts"], "merge_bwd" + tag)
        small["b_gate"][l] = jnp.sum(bpart[:, 0], axis=0)
        dh1a = _mm_nt(dpre, full["w_gate"], l, "col", name="nt_gate" + tag)
        dW["w_gate"][l] = _mm_tn(s["h1"], dpre, "col", nq=N_CHIP, kdim=D, ndim=D, name="tn_gate" + tag, tn_pref=1024)
        dbrs = []
        for b, dt, wk, odt in zip(s["brs"], (dt0, dt1, dt2, dt3), ("w_br_attn", "w_br_pool", "w_br_sgu", "w_br_conv"),
                                  (BF16, F32, F32, F32)):
            dbrs.append(_mm_nt(dt, full[wk], l, "col", name="nt_" + wk + tag, out_dtype=odt))
            dW[wk][l] = _mm_tn(b, dt, "col", nq=N_CHIP, kdim=b.shape[1], ndim=D // N_CHIP, name="tn_" + wk + tag)
        qg, kg = W["q_norm_g"][l][None], W["k_norm_g"][l][None]
        dq, dk, dv = _attn_bwd(s["q"], s["k"], s["v"], dbrs[0], NC, "attn_bwd" + tag)
        dp_qkv, qkpart = _qk_bwd(s["p"], dq, dk, dv, cos, sin, qg, kg, "qk_bwd" + tag)
        small["q_norm_g"][l] = jnp.sum(qkpart[:, 0], axis=0)
        small["k_norm_g"][l] = jnp.sum(qkpart[:, 1], axis=0)
        ps = W["pool_scale"][l][None]
        dp_pool, small["pool_w"][l], dps = _pool_bwd(s["p"], dbrs[1], W["pool_w"][l], ps, NC, "pool_bwd" + tag)
        small["pool_scale"][l] = dps[0]
        lg, lb = W["sgu_ln_g"][l][None], W["sgu_ln_b"][l][None]
        dp_sgu, small["sgu_w"][l], dsacc, dln = _sgu_bwd(s["p"], dbrs[2], lg, lb, W["sgu_w"][l], W["sgu_b"][l].T,
                                                         "sgu_bwd" + tag)
        small["sgu_b"][l] = jnp.sum(dsacc.reshape(GC, N_GROUP, GC), axis=-1).T
        small["sgu_ln_g"][l], small["sgu_ln_b"][l] = dln[0], dln[1]
        dp_conv, small["conv_w"][l] = _conv_bwd(s["p"], dbrs[3], conv_full[l], NC, "conv_bwd" + tag)
        dp = jnp.concatenate([dp_qkv, dp_pool, dp_sgu, dp_conv[0], dp_conv[1], dp_conv[2]], axis=-1)
        dh1b = _mm_nt(dp, full["w_in"], l, "col", name="nt_in" + tag)
        dW["w_in"][l] = _mm_tn(s["h1"], dp, "col", nq=N_CHIP, kdim=D, ndim=IN_W // N_CHIP, name="tn_in" + tag,
                               tn_pref=1152)
        dxa, dhs, sc_prev = dx0a, [dh1a, dh1b], sc1
        if l == 1:
            red_l1 = reduce_start(BIG, 1, "l1")
    dx0, part0 = _mod_bwd(dxa, dhs, sc_prev, x0, ncb, "mod_bwd")
    dmods[0][1], dmods[0][0] = lat_ctx(part0[:, 0]), lat_ctx(part0[:, 1])
    grad_x = dx0[NC:][None]
    dmod = jnp.stack([jnp.concatenate(dmods[l], axis=-1) for l in range(L)])

    red_mix = reduce_start(mix_keys, 0, "l0_mix")
    adam(BIG, 1, [red_mix["token"]])
    reduce_finish(red_ffn, [red_mix["token"]])
    adam(ffn_keys, 0, [])

    small_names = [k for k in SMALL if k not in ("c_ctx", "b_ada")]
    small_arrs = [jnp.stack(small[k]) for k in small_names]
    small_shapes = [a.shape for a in small_arrs]
    slab = _pack([loss_dev.reshape(1), jnp.zeros((LANE - 1,), F32), dmod] + small_arrs)
    g2 = _all_gather_small(slab, "gather_small")
    total = _slab_sum(g2, "slab_sum")
    flat_total = total.reshape(-1)
    loss = flat_total[0]
    nmod = L * 2 * 6 * D
    dmod_sum = flat_total[LANE:LANE + nmod].reshape(L, 2, 6 * D)
    small_grads = dict(zip(small_names, _unpack(flat_total[LANE + nmod:], small_shapes)))
    dmod_lat = g2.reshape(N_DEV, -1)[:, LANE:LANE + nmod].reshape(N_DEV, L, 2, 6 * D)[:, :, 1]
    dm16 = jnp.concatenate([jnp.transpose(dmod_lat, (1, 0, 2)), dmod_sum[:, 0:1],
                            jnp.zeros((L, 16 - N_DEV - 1, 6 * D), F32)], axis=1)
    small_grads["b_ada"] = dmod_sum[:, 0] + dmod_sum[:, 1]
    dm16_cols = lax.dynamic_slice_in_dim(dm16, chip * ns_ada, ns_ada, axis=2)
    grad_w_ada, ds16 = _ada_bwd(cvec, dm16_cols, W["w_ada"], "ada_bwd")
    g3 = _all_gather_small(ds16[N_DEV].reshape(-1, LANE), "gather_dsilu")
    small_grads["c_ctx"] = _cctx_grad(g3, W["c_ctx"].reshape(-1, LANE), "cctx_grad").reshape(D)
    conv_grad_full = small_grads["conv_w"]
    small_grads["conv_w"] = lax.dynamic_slice_in_dim(conv_grad_full, chip * GC, GC, axis=2)

    delta, new_m, new_v = {}, {}, {}
    delta["w_ada"], new_m["w_ada"], new_v["w_ada"] = _adamw(W["w_ada"], grad_w_ada, M["w_ada"], V["w_ada"],
                                                            "adamw_w_ada")
    shapes = [W[k].shape for k in SMALL]
    sd, sm, sv = _adamw(_pack([W[k] for k in SMALL]), _pack([small_grads[k] for k in SMALL]),
                        _pack([M[k] for k in SMALL]), _pack([V[k] for k in SMALL]), "adamw_small")
    for k, d_, m_, v_ in zip(SMALL, _unpack(sd, shapes), _unpack(sm, shapes), _unpack(sv, shapes)):
        delta[k], new_m[k], new_v[k] = d_, m_, v_
    reduce_finish(red_mix, [delta["w_ada"], sd] + [adam_out[k][0] for k in BIG])
    adam(mix_keys, 0, [])
    for k in BIG:
        delta[k], new_m[k], new_v[k] = adam_out[k]
    grads = dict(grads_big)
    grads["w_ada"] = grad_w_ada
    grads.update(small_grads)
    return (loss, grad_x, *[grads[k] for k in WEIGHTS], *[delta[k] for k in WEIGHTS],
            *[new_m[k] for k in WEIGHTS], *[new_v[k] for k in WEIGHTS])


def kernel(x, c, ctx, c_ctx, w_ada, b_ada, w_in, q_norm_g, k_norm_g, pool_w, pool_scale, sgu_ln_g, sgu_ln_b, sgu_w, sgu_b, conv_w, w_br_attn, w_br_pool, w_br_sgu, w_br_conv, w_gate, b_gate, w_o, ln1_g, ln1_b, w_ff_gate, w_ff_up, w_ff_down, ln2_g, ln2_b, loss_target, m_c_ctx, m_w_ada, m_b_ada, m_w_in, m_q_norm_g, m_k_norm_g, m_pool_w, m_pool_scale, m_sgu_ln_g, m_sgu_ln_b, m_sgu_w, m_sgu_b, m_conv_w, m_w_br_attn, m_w_br_pool, m_w_br_sgu, m_w_br_conv, m_w_gate, m_b_gate, m_w_o, m_ln1_g, m_ln1_b, m_w_ff_gate, m_w_ff_up, m_w_ff_down, m_ln2_g, m_ln2_b, v_c_ctx, v_w_ada, v_b_ada, v_w_in, v_q_norm_g, v_k_norm_g, v_pool_w, v_pool_scale, v_sgu_ln_g, v_sgu_ln_b, v_sgu_w, v_sgu_b, v_conv_w, v_w_br_attn, v_w_br_pool, v_w_br_sgu, v_w_br_conv, v_w_gate, v_b_gate, v_w_o, v_ln1_g, v_ln1_b, v_w_ff_gate, v_w_ff_up, v_w_ff_down, v_ln2_g, v_ln2_b):
    args = locals()
    W = {k: args[k] for k in WEIGHTS}
    M = {k: args["m_" + k] for k in WEIGHTS}
    V = {k: args["v_" + k] for k in WEIGHTS}
    return _step(x, c, ctx, loss_target, W, M, V)
```

```python
import functools
import math

import jax
import jax.numpy as jnp
from jax import lax
from jax.experimental import pallas as pl
from jax.experimental.pallas import tpu as pltpu

F32 = jnp.float32
BF16 = jnp.bfloat16
MESH = pl.DeviceIdType.MESH

N_DEV = 8
N_CHIP = 4
GRID_W = 64
HEAD_DIM = 128
N_HEADS = 8
N_KV = 2
KV_GROUP = N_HEADS // N_KV
Q_W = N_HEADS * HEAD_DIM
KV_W = N_KV * HEAD_DIM
QKV_W = Q_W + 2 * KV_W
GC = 128
N_GROUP = 4
BR_W = N_GROUP * GC
POOL_WINDOWS = (2, 4, 8, 16)
OFF_POOL = QKV_W
OFF_U = OFF_POOL + BR_W
OFF_VG = OFF_U + BR_W
OFF_CB = OFF_VG + BR_W
OFF_CC = OFF_CB + BR_W
OFF_CX = OFF_CC + BR_W
IN_W = OFF_CX + BR_W
Q_BLOCK = 128
ROPE_THETA = 10000.0
LN_EPS = 1e-5
RMS_EPS = 1e-6
ADAM_LR = 0.001
ADAM_B1 = 0.9
ADAM_B2 = 0.999
ADAM_EPS = 1e-08
ADAM_WD = 0.01
ADAM_STEP = 10

ROW_TM = 256
LANE = 128
SUBLANE = 8
VMEM_BIG = 56 * 1024 * 1024
VMEM_MID = 40 * 1024 * 1024


def _cp(sem=None, vmem=VMEM_MID):
    return pltpu.CompilerParams(dimension_semantics=sem, vmem_limit_bytes=vmem)


_ANY = pl.BlockSpec(memory_space=pl.ANY)
_HBM = pl.BlockSpec(memory_space=pltpu.HBM)
_SEM = pl.BlockSpec(memory_space=pltpu.SEMAPHORE)
_EFFECT = pltpu.SideEffectType.DATAFLOW_SIDE_EFFECTING


def _tile(n, pref, unit=LANE, whole_ok=False):
    if n <= pref:
        return n
    best = None
    for t in range(unit, pref + 1, unit):
        if n % t == 0:
            best = t
    if whole_ok and (best is None or best * 4 < pref):
        return n
    assert best is not None, (n, pref, unit)
    return best


def _layer_of(w, l):
    if isinstance(w, (list, tuple)):
        return w[l][:, None], 0
    return w, l


def _mm_nn(a, w, l, kind, *, name, out_dtype=F32, bias=None, act=None, tm_pref=768, tn_pref=1152):
    w, l = _layer_of(w, l)
    T, K = a.shape
    tm = _tile(T, tm_pref, SUBLANE * 2)
    if kind == "col":
        nq, _, kw, ns = w.shape
        assert kw == K
        tn = _tile(ns, tn_pref, whole_ok=True)
        nj = ns // tn
        n_total = nq * ns
        w_spec = pl.BlockSpec((None, None, K, tn), lambda j, i: (j // nj, l, 0, j % nj))
        grid_n = nq * nj
    else:
        nq, _, kc, n_total = w.shape
        assert nq * kc == K
        tn = _tile(n_total, tn_pref)
        w_spec = pl.BlockSpec((nq, None, kc, tn), lambda j, i: (0, l, 0, j))
        grid_n = n_total // tn

    def body(*refs):
        if bias is not None:
            a_ref, w_ref, b_ref, o_ref = refs
        else:
            a_ref, w_ref, o_ref = refs
        wv = w_ref[...]
        if kind == "row":
            wv = wv.reshape(K, tn)
        acc = jnp.dot(a_ref[...], wv, preferred_element_type=F32)
        if bias is not None:
            acc = acc + b_ref[...]
        if act is not None:
            acc = act(acc)
        o_ref[...] = acc.astype(out_dtype)

    in_specs = [pl.BlockSpec((tm, K), lambda j, i: (i, 0)), w_spec]
    args = [a, w]
    if bias is not None:
        in_specs.append(pl.BlockSpec((1, tn), lambda j, i: (0, j)))
        args.append(bias)
    return pl.pallas_call(
        body, grid=(grid_n, T // tm), in_specs=in_specs,
        out_specs=pl.BlockSpec((tm, tn), lambda j, i: (i, j)),
        out_shape=jax.ShapeDtypeStruct((T, n_total), out_dtype),
        name=name, compiler_params=_cp(("parallel", "parallel"), VMEM_BIG),
    )(*args)


def _mm_nt(dc, w, l, kind, *, name, out_dtype=F32, dc_off=0, tm_pref=768, tk_pref=1024):
    w, l = _layer_of(w, l)
    T = dc.shape[0]
    tm = _tile(T, tm_pref, SUBLANE * 2)
    if kind == "col":
        nq, _, K, ns = w.shape
        tk = _tile(K, tk_pref)
        assert dc_off % ns == 0
        offb = dc_off // ns

        def body(dc_ref, w_ref, o_ref, acc_ref):
            q = pl.program_id(2)
            part = lax.dot_general(dc_ref[...], w_ref[...], (((1,), (1,)), ((), ())),
                                   preferred_element_type=F32)

            @pl.when(q == 0)
            def _():
                acc_ref[...] = part

            @pl.when(q > 0)
            def _():
                acc_ref[...] += part

            @pl.when(q == nq - 1)
            def _():
                o_ref[...] = acc_ref[...].astype(out_dtype)

        return pl.pallas_call(
            body, grid=(K // tk, T // tm, nq),
            in_specs=[pl.BlockSpec((tm, ns), lambda kk, i, q: (i, offb + q)),
                      pl.BlockSpec((None, None, tk, ns), lambda kk, i, q: (q, l, kk, 0))],
            out_specs=pl.BlockSpec((tm, tk), lambda kk, i, q: (i, kk)),
            out_shape=jax.ShapeDtypeStruct((T, K), out_dtype),
            scratch_shapes=[pltpu.VMEM((tm, tk), F32)],
            name=name, compiler_params=_cp(("parallel", "parallel", "arbitrary"), VMEM_BIG),
        )(dc, w)

    nq, _, kc, n = w.shape
    assert dc_off % n == 0
    offb = dc_off // n
    tk = _tile(kc, tk_pref, whole_ok=True)
    nkk = kc // tk

    def body(dc_ref, w_ref, o_ref):
        o_ref[...] = lax.dot_general(dc_ref[...], w_ref[...], (((1,), (1,)), ((), ())),
                                     preferred_element_type=F32).astype(out_dtype)

    return pl.pallas_call(
        body, grid=(nq * nkk, T // tm),
        in_specs=[pl.BlockSpec((tm, n), lambda j, i: (i, offb)),
                  pl.BlockSpec((None, None, tk, n), lambda j, i: (j // nkk, l, j % nkk, 0))],
        out_specs=pl.BlockSpec((tm, tk), lambda j, i: (i, j)),
        out_shape=jax.ShapeDtypeStruct((T, nq * kc), out_dtype),
        name=name, compiler_params=_cp(("parallel", "parallel"), VMEM_BIG),
    )(dc, w)


def _mm_tn(a, dc, kind, *, nq, kdim, ndim, name, a_off=0, dc_off=0, tk_pref=512, tn_pref=1152, out_dtype=BF16):
    T = a.shape[0]
    assert dc.shape[0] == T
    tk = _tile(kdim, tk_pref, whole_ok=True)
    tn = _tile(ndim, tn_pref, whole_ok=True)
    nkk, njn = kdim // tk, ndim // tn
    assert a_off % tk == 0 and dc_off % tn == 0
    aoffb, doffb = a_off // tk, dc_off // tn
    if kind == "col":
        a_map = lambda q, kk, jn: (0, aoffb + kk)
        d_map = lambda q, kk, jn: (0, doffb + q * njn + jn)
    else:
        a_map = lambda q, kk, jn: (0, aoffb + q * nkk + kk)
        d_map = lambda q, kk, jn: (0, doffb + jn)

    def body(a_ref, d_ref, o_ref):
        o_ref[...] = lax.dot_general(a_ref[...], d_ref[...], (((0,), (0,)), ((), ())),
                                     preferred_element_type=F32).astype(out_dtype)

    return pl.pallas_call(
        body, grid=(nq, nkk, njn),
        in_specs=[pl.BlockSpec((T, tk), a_map), pl.BlockSpec((T, tn), d_map)],
        out_specs=pl.BlockSpec((None, tk, tn), lambda q, kk, jn: (q, kk, jn)),
        out_shape=jax.ShapeDtypeStruct((nq, kdim, ndim), out_dtype),
        name=name, compiler_params=_cp(("parallel", "parallel", "parallel"), VMEM_BIG),
    )(a, dc)


def _row_spec(d):
    return pl.BlockSpec((ROW_TM, d), lambda i: (i, 0))


def _mod_spec(d, ncb):
    return pl.BlockSpec((None, 1, d), lambda i: (jnp.where(i >= ncb, 1, 0), 0, 0))


def _vec_spec(d):
    return pl.BlockSpec((1, d), lambda i: (0, 0))


def _part_spec(d):
    return pl.BlockSpec((None, SUBLANE, d), lambda i: (i, 0, 0))


def _modulate(x, sc, sh, ncb, name):
    T, D = x.shape

    def body(x_ref, sc_ref, sh_ref, o_ref):
        o_ref[...] = (x_ref[...] * (1.0 + sc_ref[...]) + sh_ref[...]).astype(BF16)

    return pl.pallas_call(
        body, grid=(T // ROW_TM,),
        in_specs=[_row_spec(D), _mod_spec(D, ncb), _mod_spec(D, ncb)],
        out_specs=_row_spec(D), out_shape=jax.ShapeDtypeStruct((T, D), BF16),
        name=name, compiler_params=_cp(("parallel",)),
    )(x, sc, sh)


def _ln_stats(r):
    mu = jnp.mean(r, axis=-1, keepdims=True)
    rc = r - mu
    var = jnp.mean(rc * rc, axis=-1, keepdims=True)
    rstd = lax.rsqrt(var + LN_EPS)
    return rc * rstd, rstd


def _resid_ln(x, y, gate, g, b, sc, sh, ncb, alpha, name):
    T, D = x.shape

    def body(x_ref, y_ref, gate_ref, g_ref, b_ref, sc_ref, sh_ref, xo_ref, h_ref):
        xhat, _ = _ln_stats(alpha * x_ref[...] + gate_ref[...] * y_ref[...])
        xo = xhat * g_ref[...] + b_ref[...]
        xo_ref[...] = xo
        h_ref[...] = (xo * (1.0 + sc_ref[...]) + sh_ref[...]).astype(BF16)

    return pl.pallas_call(
        body, grid=(T // ROW_TM,),
        in_specs=[_row_spec(D), _row_spec(D), _mod_spec(D, ncb), _vec_spec(D), _vec_spec(D),
                  _mod_spec(D, ncb), _mod_spec(D, ncb)],
        out_specs=[_row_spec(D), _row_spec(D)],
        out_shape=[jax.ShapeDtypeStruct((T, D), F32), jax.ShapeDtypeStruct((T, D), BF16)],
        name=name, compiler_params=_cp(("parallel",)),
    )(x, y, gate, g, b, sc, sh)


def _resid_ln_loss(x, y, gate, g, b, target, ncb, alpha, name):
    T, D = x.shape
    nblk = T // ROW_TM

    def body(x_ref, y_ref, gate_ref, g_ref, b_ref, t_ref, dy_ref, loss_ref):
        i = pl.program_id(0)
        xhat, _ = _ln_stats(alpha * x_ref[...] + gate_ref[...] * y_ref[...])
        xo = xhat * g_ref[...] + b_ref[...]
        live = (i >= ncb).astype(F32)
        err = (xo - t_ref[...]) * live
        dy_ref[...] = err * (1.0 / D)
        loss_ref[...] = jnp.full((SUBLANE, LANE), 0.5 / D, F32) * jnp.sum(err * err)

    return pl.pallas_call(
        body, grid=(nblk,),
        in_specs=[_row_spec(D), _row_spec(D), _mod_spec(D, ncb), _vec_spec(D), _vec_spec(D),
                  pl.BlockSpec((ROW_TM, D), lambda i: (jnp.maximum(i - ncb, 0), 0))],
        out_specs=[_row_spec(D), _part_spec(LANE)],
        out_shape=[jax.ShapeDtypeStruct((T, D), F32), jax.ShapeDtypeStruct((nblk, SUBLANE, LANE), F32)],
        name=name, compiler_params=_cp(("parallel",)),
    )(x, y, gate, g, b, target)


def _write_parts(part_ref, rows, d):
    for k, r in enumerate(rows):
        part_ref[pl.ds(k, 1), :] = jnp.sum(r, axis=0, keepdims=True)
    if len(rows) < SUBLANE:
        part_ref[pl.ds(len(rows), SUBLANE - len(rows)), :] = jnp.zeros((SUBLANE - len(rows), d), F32)


def _ln_bwd(dxa, dhs, sc, x, y, gate, g, b, ncb, alpha, name, after=None):
    T, D = x.shape
    nblk = T // ROW_TM
    ndh = len(dhs)

    def body(*refs):
        dxa_ref = refs[0]
        dh_refs = refs[1:1 + ndh]
        sc_ref, x_ref, y_ref, gate_ref, g_ref, b_ref = refs[1 + ndh:7 + ndh]
        dx_ref, dy_ref, part_ref = refs[-3:]
        yv = y_ref[...]
        xhat, rstd = _ln_stats(alpha * x_ref[...] + gate_ref[...] * yv)
        dxo = dxa_ref[...]
        rows = []
        if ndh:
            dh = dh_refs[0][...]
            for r in dh_refs[1:]:
                dh = dh + r[...]
            dxo = dxo + dh * (1.0 + sc_ref[...])
            xo = xhat * g_ref[...] + b_ref[...]
            rows = [dh * xo, dh]
        dxhat = dxo * g_ref[...]
        m1 = jnp.mean(dxhat, axis=-1, keepdims=True)
        m2 = jnp.mean(dxhat * xhat, axis=-1, keepdims=True)
        dr = rstd * (dxhat - m1 - xhat * m2)
        dx_ref[...] = alpha * dr
        dy_ref[...] = (gate_ref[...] * dr).astype(BF16)
        _write_parts(part_ref, [dxo * xhat, dxo, dr * yv] + rows, D)

    in_specs = ([_row_spec(D)] * (1 + ndh)
                + [_mod_spec(D, ncb), _row_spec(D), _row_spec(D), _mod_spec(D, ncb), _vec_spec(D), _vec_spec(D)])
    extra = [] if after is None else [after]
    return pl.pallas_call(
        body, grid=(nblk,), in_specs=in_specs + [_ANY] * len(extra),
        out_specs=[_row_spec(D), _row_spec(D), _part_spec(D)],
        out_shape=[jax.ShapeDtypeStruct((T, D), F32), jax.ShapeDtypeStruct((T, D), BF16),
                   jax.ShapeDtypeStruct((nblk, SUBLANE, D), F32)],
        name=name, compiler_params=_cp(("parallel",)),
    )(dxa, *dhs, sc, x, y, gate, g, b, *extra)


def _mod_bwd(dxa, dhs, sc, x, ncb, name):
    T, D = x.shape
    nblk = T // ROW_TM
    ndh = len(dhs)

    def body(*refs):
        dxa_ref = refs[0]
        dh_refs = refs[1:1 + ndh]
        sc_ref, x_ref, dx_ref, part_ref = refs[1 + ndh:]
        dh = dh_refs[0][...]
        for r in dh_refs[1:]:
            dh = dh + r[...]
        dx_ref[...] = dxa_ref[...] + dh * (1.0 + sc_ref[...])
        _write_parts(part_ref, [dh * x_ref[...], dh], D)

    return pl.pallas_call(
        body, grid=(nblk,),
        in_specs=[_row_spec(D)] * (1 + ndh) + [_mod_spec(D, ncb), _row_spec(D)],
        out_specs=[_row_spec(D), _part_spec(D)],
        out_shape=[jax.ShapeDtypeStruct((T, D), F32), jax.ShapeDtypeStruct((nblk, SUBLANE, D), F32)],
        name=name, compiler_params=_cp(("parallel",)),
    )(dxa, *dhs, sc, x)


def _merge_fwd(gates, ts, name):
    T, D = ts[0].shape
    tm = 128

    def body(g_ref, t0, t1, t2, t3, o_ref):
        acc = g_ref[:, 0:D].astype(F32) * t0[...].astype(F32)
        for k, t in enumerate((t1, t2, t3), start=1):
            acc = acc + g_ref[:, k * D:(k + 1) * D].astype(F32) * t[...].astype(F32)
        o_ref[...] = acc.astype(BF16)

    rs = pl.BlockSpec((tm, D), lambda i: (i, 0))
    return pl.pallas_call(
        body, grid=(T // tm,),
        in_specs=[pl.BlockSpec((tm, 4 * D), lambda i: (i, 0)), rs, rs, rs, rs],
        out_specs=rs, out_shape=jax.ShapeDtypeStruct((T, D), BF16),
        name=name, compiler_params=_cp(("parallel",)),
    )(gates, *ts)


def _merge_bwd(dmg, gates, ts, name):
    T, D = dmg.shape
    tm = 128
    nblk = T // tm

    def body(d_ref, g_ref, t0, t1, t2, t3, dpre_ref, dt0, dt1, dt2, dt3, part_ref):
        d = d_ref[...]
        for k, (t, dt) in enumerate(zip((t0, t1, t2, t3), (dt0, dt1, dt2, dt3))):
            gk = g_ref[:, k * D:(k + 1) * D].astype(F32)
            dt[...] = (d * gk).astype(BF16)
            dpre = d * t[...].astype(F32) * gk * (1.0 - gk)
            dpre_ref[:, k * D:(k + 1) * D] = dpre.astype(BF16)
            part_ref[:, k * D:(k + 1) * D] = jnp.sum(dpre, axis=0, keepdims=True)

    rs = pl.BlockSpec((tm, D), lambda i: (i, 0))
    wide = pl.BlockSpec((tm, 4 * D), lambda i: (i, 0))
    return pl.pallas_call(
        body, grid=(nblk,),
        in_specs=[rs, wide, rs, rs, rs, rs],
        out_specs=[wide, rs, rs, rs, rs, pl.BlockSpec((None, 1, 4 * D), lambda i: (i, 0, 0))],
        out_shape=[jax.ShapeDtypeStruct((T, 4 * D), BF16)] + [jax.ShapeDtypeStruct((T, D), BF16)] * 4
                  + [jax.ShapeDtypeStruct((nblk, 1, 4 * D), F32)],
        name=name, compiler_params=_cp(("parallel",)),
    )(dmg, gates, *ts)


def _ffn_up(a, wg, wu, name, tm_pref=384):
    T, K = a.shape
    nq, _, ns = wg.shape
    tm = _tile(T, tm_pref, SUBLANE * 2)

    def body(a_ref, g_ref, u_ref, gg_ref, uu_ref, act_ref):
        av = a_ref[...]
        g = jnp.dot(av, g_ref[...], preferred_element_type=F32)
        u = jnp.dot(av, u_ref[...], preferred_element_type=F32)
        gg_ref[...] = g.astype(BF16)
        uu_ref[...] = u.astype(BF16)
        act_ref[...] = (g * jax.nn.sigmoid(g) * u).astype(BF16)

    ws = pl.BlockSpec((None, K, ns), lambda q, i: (q, 0, 0))
    os = pl.BlockSpec((tm, ns), lambda q, i: (i, q))
    return pl.pallas_call(
        body, grid=(nq, T // tm),
        in_specs=[pl.BlockSpec((tm, K), lambda q, i: (i, 0)), ws, ws], out_specs=[os, os, os],
        out_shape=[jax.ShapeDtypeStruct((T, nq * ns), BF16)] * 3,
        name=name, compiler_params=_cp(("parallel", "parallel"), VMEM_BIG),
    )(a, wg, wu)


def _ffn_down_bwd(dff, wd, gg, uu, name, tm_pref=384):
    T, N = dff.shape
    nq, kc, _ = wd.shape
    tm = _tile(T, tm_pref, SUBLANE * 2)

    def body(d_ref, w_ref, g_ref, u_ref, dg_ref, du_ref):
        da = lax.dot_general(d_ref[...], w_ref[...], (((1,), (1,)), ((), ())), preferred_element_type=F32)
        g = g_ref[...].astype(F32)
        sig = jax.nn.sigmoid(g)
        dg_ref[...] = (da * u_ref[...].astype(F32) * sig * (1.0 + g * (1.0 - sig))).astype(BF16)
        du_ref[...] = (da * g * sig).astype(BF16)

    ts = pl.BlockSpec((tm, kc), lambda q, i: (i, q))
    return pl.pallas_call(
        body, grid=(nq, T // tm),
        in_specs=[pl.BlockSpec((tm, N), lambda q, i: (i, 0)), pl.BlockSpec((None, kc, N), lambda q, i: (q, 0, 0)),
                  ts, ts],
        out_specs=[ts, ts], out_shape=[jax.ShapeDtypeStruct((T, nq * kc), BF16)] * 2,
        name=name, compiler_params=_cp(("parallel", "parallel"), VMEM_BIG),
    )(dff, wd, gg, uu)


def _swap_halves(v):
    lane = lax.broadcasted_iota(jnp.int32, v.shape, 1)
    return jnp.where((lane % 64) < 32, pltpu.roll(v, 96, 1), pltpu.roll(v, 32, 1))


def _rope_tables(n, nc):
    rows = n // GRID_W
    row = jnp.repeat(jnp.arange(rows), GRID_W).astype(F32)
    col = jnp.tile(jnp.arange(GRID_W), rows).astype(F32)
    inv = ROPE_THETA ** (-jnp.arange(0, 64, 2, dtype=F32) / 64)
    ang_r = row[:, None] * inv
    ang_c = col[:, None] * inv
    cos = jnp.concatenate([jnp.cos(ang_r), jnp.cos(ang_r), jnp.cos(ang_c), jnp.cos(ang_c)], axis=-1)
    sin = jnp.concatenate([-jnp.sin(ang_r), jnp.sin(ang_r), -jnp.sin(ang_c), jnp.sin(ang_c)], axis=-1)
    cos = jnp.concatenate([jnp.ones((nc, HEAD_DIM), F32), cos], axis=0)
    sin = jnp.concatenate([jnp.zeros((nc, HEAD_DIM), F32), sin], axis=0)
    return cos, sin


def _qk_fwd(p, cos, sin, qg, kg, name):
    T = p.shape[0]

    def body(p_ref, c_ref, s_ref, qg_ref, kg_ref, q_ref, k_ref, v_ref):
        c, s = c_ref[...], s_ref[...]
        for h in range(N_HEADS + N_KV):
            x = p_ref[:, h * HEAD_DIM:(h + 1) * HEAD_DIM]
            rs = lax.rsqrt(jnp.mean(x * x, axis=-1, keepdims=True) + RMS_EPS)
            gain = qg_ref[...] if h < N_HEADS else kg_ref[...]
            yv = x * rs * gain
            out = (yv * c + _swap_halves(yv) * s).astype(BF16)
            if h < N_HEADS:
                q_ref[:, h * HEAD_DIM:(h + 1) * HEAD_DIM] = out
            else:
                k_ref[:, (h - N_HEADS) * HEAD_DIM:(h - N_HEADS + 1) * HEAD_DIM] = out
        v_ref[...] = p_ref[:, Q_W + KV_W:QKV_W].astype(BF16)

    return pl.pallas_call(
        body, grid=(T // ROW_TM,),
        in_specs=[_row_spec(QKV_W), _row_spec(HEAD_DIM), _row_spec(HEAD_DIM), _vec_spec(HEAD_DIM), _vec_spec(HEAD_DIM)],
        out_specs=[_row_spec(Q_W), _row_spec(KV_W), _row_spec(KV_W)],
        out_shape=[jax.ShapeDtypeStruct((T, Q_W), BF16), jax.ShapeDtypeStruct((T, KV_W), BF16),
                   jax.ShapeDtypeStruct((T, KV_W), BF16)],
        name=name, compiler_params=_cp(("parallel",)),
    )(p, cos, sin, qg, kg)


def _qk_bwd(p, dq, dk, dv, cos, sin, qg, kg, name):
    T = p.shape[0]
    nblk = T // ROW_TM

    def body(p_ref, dq_ref, dk_ref, dv_ref, c_ref, s_ref, qg_ref, kg_ref, dp_ref, part_ref):
        c, s = c_ref[...], s_ref[...]
        dgq = jnp.zeros((1, HEAD_DIM), F32)
        dgk = jnp.zeros((1, HEAD_DIM), F32)
        for h in range(N_HEADS + N_KV):
            x = p_ref[:, h * HEAD_DIM:(h + 1) * HEAD_DIM]
            if h < N_HEADS:
                d = dq_ref[:, h * HEAD_DIM:(h + 1) * HEAD_DIM]
                gain = qg_ref[...]
            else:
                d = dk_ref[:, (h - N_HEADS) * HEAD_DIM:(h - N_HEADS + 1) * HEAD_DIM]
                gain = kg_ref[...]
            dyv = d * c + _swap_halves(d * s)
            rs = lax.rsqrt(jnp.mean(x * x, axis=-1, keepdims=True) + RMS_EPS)
            xn = x * rs
            dgsum = jnp.sum(dyv * xn, axis=0, keepdims=True)
            if h < N_HEADS:
                dgq = dgq + dgsum
            else:
                dgk = dgk + dgsum
            dxg = dyv * gain
            dx = rs * (dxg - xn * jnp.mean(dxg * xn, axis=-1, keepdims=True))
            dp_ref[:, h * HEAD_DIM:(h + 1) * HEAD_DIM] = dx.astype(BF16)
        dp_ref[:, Q_W + KV_W:QKV_W] = dv_ref[...].astype(BF16)
        part_ref[pl.ds(0, 1), :] = dgq
        part_ref[pl.ds(1, 1), :] = dgk
        part_ref[pl.ds(2, SUBLANE - 2), :] = jnp.zeros((SUBLANE - 2, HEAD_DIM), F32)

    return pl.pallas_call(
        body, grid=(nblk,),
        in_specs=[_row_spec(QKV_W), _row_spec(Q_W), _row_spec(KV_W), _row_spec(KV_W),
                  _row_spec(HEAD_DIM), _row_spec(HEAD_DIM), _vec_spec(HEAD_DIM), _vec_spec(HEAD_DIM)],
        out_specs=[_row_spec(QKV_W), _part_spec(HEAD_DIM)],
        out_shape=[jax.ShapeDtypeStruct((T, QKV_W), BF16), jax.ShapeDtypeStruct((nblk, SUBLANE, HEAD_DIM), F32)],
        name=name, compiler_params=_cp(("parallel",)),
    )(p, dq, dk, dv, cos, sin, qg, kg)


def _attn_probs(q, k, qblk, nc, T):
    s = lax.dot_general(q, k, (((1,), (1,)), ((), ())), preferred_element_type=F32) * (HEAD_DIM ** -0.5)
    col = lax.broadcasted_iota(jnp.int32, s.shape, 1)
    limit = jnp.where(qblk < nc // Q_BLOCK, nc, T)
    s = jnp.where(col < limit, s, -1e30)
    e = jnp.exp(s - jnp.max(s, axis=-1, keepdims=True))
    return e / jnp.sum(e, axis=-1, keepdims=True)


def _attn_fwd(q, k, v, nc, name):
    T = q.shape[0]

    def body(q_ref, k_ref, v_ref, o_ref):
        i = pl.program_id(0)
        for h in range(N_HEADS):
            g = h // KV_GROUP
            kk = k_ref[:, g * HEAD_DIM:(g + 1) * HEAD_DIM]
            vv = v_ref[:, g * HEAD_DIM:(g + 1) * HEAD_DIM]
            pr = _attn_probs(q_ref[:, h * HEAD_DIM:(h + 1) * HEAD_DIM], kk, i, nc, T)
            o = jnp.dot(pr.astype(BF16), vv, preferred_element_type=F32)
            o_ref[:, h * HEAD_DIM:(h + 1) * HEAD_DIM] = o.astype(BF16)

    whole = pl.BlockSpec((T, KV_W), lambda i: (0, 0))
    qs = pl.BlockSpec((Q_BLOCK, Q_W), lambda i: (i, 0))
    return pl.pallas_call(
        body, grid=(T // Q_BLOCK,), in_specs=[qs, whole, whole], out_specs=qs,
        out_shape=jax.ShapeDtypeStruct((T, Q_W), BF16),
        name=name, compiler_params=_cp(("parallel",)),
    )(q, k, v)


def _attn_bwd(q, k, v, do, nc, name):
    T = q.shape[0]
    scale = HEAD_DIM ** -0.5

    def body(q_ref, k_ref, v_ref, do_ref, dq_ref, dk_ref, dv_ref):
        i = pl.program_id(0)

        @pl.when(i == 0)
        def _():
            dk_ref[...] = jnp.zeros_like(dk_ref)
            dv_ref[...] = jnp.zeros_like(dv_ref)

        for h in range(N_HEADS):
            g = h // KV_GROUP
            cols = slice(g * HEAD_DIM, (g + 1) * HEAD_DIM)
            hc = slice(h * HEAD_DIM, (h + 1) * HEAD_DIM)
            qh, kk, vv, doh = q_ref[:, hc], k_ref[:, cols], v_ref[:, cols], do_ref[:, hc]
            pr = _attn_probs(qh, kk, i, nc, T)
            dpr = lax.dot_general(doh, vv, (((1,), (1,)), ((), ())), preferred_element_type=F32)
            ds = (pr * (dpr - jnp.sum(pr * dpr, axis=-1, keepdims=True)) * scale).astype(BF16)
            dq_ref[:, hc] = jnp.dot(ds, kk, preferred_element_type=F32)
            dk_ref[:, cols] += lax.dot_general(ds, qh, (((0,), (0,)), ((), ())), preferred_element_type=F32)
            dv_ref[:, cols] += lax.dot_general(pr.astype(BF16), doh, (((0,), (0,)), ((), ())),
                                               preferred_element_type=F32)

    whole = pl.BlockSpec((T, KV_W), lambda i: (0, 0))
    qs = pl.BlockSpec((Q_BLOCK, Q_W), lambda i: (i, 0))
    return pl.pallas_call(
        body, grid=(T // Q_BLOCK,), in_specs=[qs, whole, whole, qs], out_specs=[qs, whole, whole],
        out_shape=[jax.ShapeDtypeStruct((T, Q_W), F32), jax.ShapeDtypeStruct((T, KV_W), F32),
                   jax.ShapeDtypeStruct((T, KV_W), F32)],
        name=name, compiler_params=_cp(("arbitrary",)),
    )(q, k, v, do)


def _shift_rows(z, o, nc):
    T = z.shape[0]
    t = lax.broadcasted_iota(jnp.int32, (T, 1), 0)
    lo = jnp.where(t < nc, 0, nc)
    hi = jnp.where(t < nc, nc, T)
    ok = jnp.logical_and(t + o >= lo, t + o < hi)
    rolled = z if o == 0 else pltpu.roll(z, (-o) % T, 0)
    return jnp.where(ok, rolled, 0.0), ok


def _pool_window(w):
    left = w // 2
    return -left, w - 1 - left


def _pool_d(z, w, nc):
    o0, o1 = _pool_window(w)
    tot = jnp.zeros_like(z)
    cnt = jnp.zeros((z.shape[0], 1), F32)
    for o in range(o0, o1 + 1):
        sh, ok = _shift_rows(z, o, nc)
        tot = tot + sh
        cnt = cnt + ok.astype(F32)
    return tot / cnt - z, cnt


def _pool_fwd(p, pool_w, pool_scale, nc, name):
    T = p.shape[0]

    def body(z_ref, w_ref, s_ref, o_ref):
        for g, w in enumerate(POOL_WINDOWS):
            cs = slice(g * GC, (g + 1) * GC)
            d, _ = _pool_d(z_ref[:, cs], w, nc)
            yv = jnp.dot(d.astype(BF16), w_ref[g].astype(BF16), preferred_element_type=F32)
            o_ref[:, cs] = (yv * s_ref[:, cs]).astype(BF16)

    return pl.pallas_call(
        body, grid=(1,),
        in_specs=[pl.BlockSpec((T, BR_W), lambda i: (0, OFF_POOL // BR_W)),
                  pl.BlockSpec((N_GROUP, GC, GC), lambda i: (0, 0, 0)), _vec_spec(BR_W)],
        out_specs=pl.BlockSpec((T, BR_W), lambda i: (0, 0)),
        out_shape=jax.ShapeDtypeStruct((T, BR_W), BF16),
        name=name, compiler_params=_cp(("arbitrary",), VMEM_BIG),
    )(p, pool_w, pool_scale)


def _pool_bwd(p, dy, pool_w, pool_scale, nc, name):
    T = p.shape[0]

    def body(z_ref, dy_ref, w_ref, s_ref, dz_ref, dw_ref, ds_ref):
        for g, w in enumerate(POOL_WINDOWS):
            cs = slice(g * GC, (g + 1) * GC)
            d, cnt = _pool_d(z_ref[:, cs], w, nc)
            db = d.astype(BF16)
            wb = w_ref[g].astype(BF16)
            dyv = dy_ref[:, cs]
            e = (dyv * s_ref[:, cs]).astype(BF16)
            dw_ref[g] = lax.dot_general(db, e, (((0,), (0,)), ((), ())), preferred_element_type=F32)
            yraw = jnp.dot(db, wb, preferred_element_type=F32)
            ds_ref[:, cs] = jnp.sum(dyv * yraw, axis=0, keepdims=True)
            dd = lax.dot_general(e, wb, (((1,), (1,)), ((), ())), preferred_element_type=F32)
            ec = dd / cnt
            o0, o1 = _pool_window(w)
            tot = jnp.zeros_like(dd)
            for o in range(o0, o1 + 1):
                tot = tot + _shift_rows(ec, -o, nc)[0]
            dz_ref[:, cs] = (tot - dd).astype(BF16)

    return pl.pallas_call(
        body, grid=(1,),
        in_specs=[pl.BlockSpec((T, BR_W), lambda i: (0, OFF_POOL // BR_W)),
                  pl.BlockSpec((T, BR_W), lambda i: (0, 0)),
                  pl.BlockSpec((N_GROUP, GC, GC), lambda i: (0, 0, 0)), _vec_spec(BR_W)],
        out_specs=[pl.BlockSpec((T, BR_W), lambda i: (0, 0)),
                   pl.BlockSpec((N_GROUP, GC, GC), lambda i: (0, 0, 0)), _vec_spec(BR_W)],
        out_shape=[jax.ShapeDtypeStruct((T, BR_W), BF16), jax.ShapeDtypeStruct((N_GROUP, GC, GC), F32),
                   jax.ShapeDtypeStruct((1, BR_W), F32)],
        name=name, compiler_params=_cp(("arbitrary",), VMEM_BIG),
    )(p, dy, pool_w, pool_scale)


_GELU_C = math.sqrt(2.0 / math.pi)


def _gelu(x):
    return 0.5 * x * (1.0 + jnp.tanh(_GELU_C * (x + 0.044715 * x * x * x)))


def _gelu_grad(x):
    th = jnp.tanh(_GELU_C * (x + 0.044715 * x * x * x))
    return 0.5 * (1.0 + th) + 0.5 * x * (1.0 - th * th) * _GELU_C * (1.0 + 3.0 * 0.044715 * x * x)


def _sgu_fwd(p, ln_g, ln_b, w_s, b_st, name):
    T = p.shape[0]

    def body(pu_ref, pv_ref, g_ref, b_ref, w_ref, bs_ref, o_ref):
        u = _gelu(pu_ref[...])
        vhat, _ = _ln_stats(_gelu(pv_ref[...]))
        vn = (vhat * g_ref[...] + b_ref[...]).astype(BF16)
        for g in range(N_GROUP):
            cs = slice(g * GC, (g + 1) * GC)
            s = jnp.dot(w_ref[g].astype(BF16), vn[:, cs], preferred_element_type=F32) + bs_ref[:, g:g + 1]
            o_ref[:, cs] = (u[:, cs] * s).astype(BF16)

    chunk = lambda off: pl.BlockSpec((GC, BR_W), lambda i: (i, off // BR_W))
    return pl.pallas_call(
        body, grid=(T // GC,),
        in_specs=[chunk(OFF_U), chunk(OFF_VG), _vec_spec(BR_W), _vec_spec(BR_W),
                  pl.BlockSpec((N_GROUP, GC, GC), lambda i: (0, 0, 0)),
                  pl.BlockSpec((GC, N_GROUP), lambda i: (0, 0))],
        out_specs=pl.BlockSpec((GC, BR_W), lambda i: (i, 0)),
        out_shape=jax.ShapeDtypeStruct((T, BR_W), BF16),
        name=name, compiler_params=_cp(("parallel",)),
    )(p, p, ln_g, ln_b, w_s, b_st)


def _sgu_bwd(p, dy, ln_g, ln_b, w_s, b_st, name):
    T = p.shape[0]

    def body(pu_ref, pv_ref, dy_ref, g_ref, b_ref, w_ref, bs_ref, dp_ref, dw_ref, dsacc_ref, dln_ref):
        i = pl.program_id(0)

        @pl.when(i == 0)
        def _():
            dw_ref[...] = jnp.zeros_like(dw_ref)
            dsacc_ref[...] = jnp.zeros_like(dsacc_ref)
            dln_ref[...] = jnp.zeros_like(dln_ref)

        pu, pv, dyv = pu_ref[...], pv_ref[...], dy_ref[...]
        u = _gelu(pu)
        vhat, rstd = _ln_stats(_gelu(pv))
        vn = (vhat * g_ref[...] + b_ref[...]).astype(BF16)
        ds = dyv * u
        dsb = ds.astype(BF16)
        dsacc_ref[...] += ds
        dvn_parts = []
        for g in range(N_GROUP):
            cs = slice(g * GC, (g + 1) * GC)
            wb = w_ref[g].astype(BF16)
            s = jnp.dot(wb, vn[:, cs], preferred_element_type=F32) + bs_ref[:, g:g + 1]
            dp_ref[:, cs] = (dyv[:, cs] * s * _gelu_grad(pu[:, cs])).astype(BF16)
            dvn_parts.append(lax.dot_general(wb, dsb[:, cs], (((0,), (0,)), ((), ())),
                                             preferred_element_type=F32))
            dw_ref[g] += lax.dot_general(dsb[:, cs], vn[:, cs], (((1,), (1,)), ((), ())),
                                         preferred_element_type=F32)
        dvn = jnp.concatenate(dvn_parts, axis=-1)
        dln_ref[pl.ds(0, 1), :] += jnp.sum(dvn * vhat, axis=0, keepdims=True)
        dln_ref[pl.ds(1, 1), :] += jnp.sum(dvn, axis=0, keepdims=True)
        dvhat = dvn * g_ref[...]
        m1 = jnp.mean(dvhat, axis=-1, keepdims=True)
        m2 = jnp.mean(dvhat * vhat, axis=-1, keepdims=True)
        dv = rstd * (dvhat - m1 - vhat * m2)
        dp_ref[:, BR_W:2 * BR_W] = (dv * _gelu_grad(pv)).astype(BF16)

    chunk = lambda off: pl.BlockSpec((GC, BR_W), lambda i: (i, off // BR_W))
    return pl.pallas_call(
        body, grid=(T // GC,),
        in_specs=[chunk(OFF_U), chunk(OFF_VG), pl.BlockSpec((GC, BR_W), lambda i: (i, 0)),
                  _vec_spec(BR_W), _vec_spec(BR_W),
                  pl.BlockSpec((N_GROUP, GC, GC), lambda i: (0, 0, 0)),
                  pl.BlockSpec((GC, N_GROUP), lambda i: (0, 0))],
        out_specs=[pl.BlockSpec((GC, 2 * BR_W), lambda i: (i, 0)),
                   pl.BlockSpec((N_GROUP, GC, GC), lambda i: (0, 0, 0)),
                   pl.BlockSpec((GC, BR_W), lambda i: (0, 0)),
                   pl.BlockSpec((SUBLANE, BR_W), lambda i: (0, 0))],
        out_shape=[jax.ShapeDtypeStruct((T, 2 * BR_W), BF16), jax.ShapeDtypeStruct((N_GROUP, GC, GC), F32),
                   jax.ShapeDtypeStruct((GC, BR_W), F32), jax.ShapeDtypeStruct((SUBLANE, BR_W), F32)],
        name=name, compiler_params=_cp(("arbitrary",)),
    )(p, p, dy, ln_g, ln_b, w_s, b_st)


def _conv_fwd(p, conv_w, nc, name):
    T = p.shape[0]

    def body(cb_ref, cc_ref, cx_ref, w_ref, o_ref):
        zz = cc_ref[...] * cx_ref[...]
        conv = (w_ref[0:1, :] * _shift_rows(zz, -1, nc)[0] + w_ref[1:2, :] * zz
                + w_ref[2:3, :] * _shift_rows(zz, 1, nc)[0])
        o_ref[...] = (cb_ref[...] * conv).astype(BF16)

    col = lambda off: pl.BlockSpec((T, GC), lambda j: (0, off // GC + j))
    return pl.pallas_call(
        body, grid=(N_GROUP,),
        in_specs=[col(OFF_CB), col(OFF_CC), col(OFF_CX), pl.BlockSpec((3, GC), lambda j: (0, j))],
        out_specs=pl.BlockSpec((T, GC), lambda j: (0, j)),
        out_shape=jax.ShapeDtypeStruct((T, BR_W), BF16),
        name=name, compiler_params=_cp(("parallel",)),
    )(p, p, p, conv_w)


def _conv_bwd(p, dy, conv_w, nc, name):
    T = p.shape[0]

    def body(cb_ref, cc_ref, cx_ref, dy_ref, w_ref, dp_ref, dw_ref):
        cc, cx, dyv = cc_ref[...], cx_ref[...], dy_ref[...]
        zz = cc * cx
        zm, zp = _shift_rows(zz, -1, nc)[0], _shift_rows(zz, 1, nc)[0]
        conv = w_ref[0:1, :] * zm + w_ref[1:2, :] * zz + w_ref[2:3, :] * zp
        dconv = dyv * cb_ref[...]
        dw_ref[0:1, :] = jnp.sum(dconv * zm, axis=0, keepdims=True)
        dw_ref[1:2, :] = jnp.sum(dconv * zz, axis=0, keepdims=True)
        dw_ref[2:3, :] = jnp.sum(dconv * zp, axis=0, keepdims=True)
        dzz = (w_ref[0:1, :] * _shift_rows(dconv, 1, nc)[0] + w_ref[1:2, :] * dconv
               + w_ref[2:3, :] * _shift_rows(dconv, -1, nc)[0])
        dp_ref[0] = (dyv * conv).astype(BF16)
        dp_ref[1] = (dzz * cx).astype(BF16)
        dp_ref[2] = (dzz * cc).astype(BF16)

    col = lambda off: pl.BlockSpec((T, GC), lambda j: (0, off // GC + j))
    return pl.pallas_call(
        body, grid=(N_GROUP,),
        in_specs=[col(OFF_CB), col(OFF_CC), col(OFF_CX), pl.BlockSpec((T, GC), lambda j: (0, j)),
                  pl.BlockSpec((3, GC), lambda j: (0, j))],
        out_specs=[pl.BlockSpec((3, T, GC), lambda j: (0, 0, j)), pl.BlockSpec((3, GC), lambda j: (0, j))],
        out_shape=[jax.ShapeDtypeStruct((3, T, BR_W), BF16), jax.ShapeDtypeStruct((3, BR_W), F32)],
        name=name, compiler_params=_cp(("parallel",)),
    )(p, p, p, dy, conv_w)


def _rows_tile(rows, cols, n_arrays):
    budget = 24 * 1024 * 1024 // (2 * 4 * n_arrays * cols)
    return _tile(rows, max(SUBLANE * 2, min(budget, 1024)), SUBLANE * 2) if rows % (SUBLANE * 2) == 0 else rows


def _cast_into(chip, w, l, name):
    _, K, cols = w.shape
    tr = _rows_tile(K, cols, 2)

    def body(q_ref, w_ref, o_ref):
        o_ref[...] = w_ref[...].astype(BF16)

    return pl.pallas_call(
        body,
        grid_spec=pltpu.PrefetchScalarGridSpec(
            num_scalar_prefetch=1, grid=(K // tr,),
            in_specs=[pl.BlockSpec((None, tr, cols), lambda i, q: (l, i, 0))],
            out_specs=pl.BlockSpec((None, tr, cols), lambda i, q: (q[0], i, 0))),
        out_shape=jax.ShapeDtypeStruct((N_CHIP, K, cols), BF16),
        name=name, compiler_params=_cp(("parallel",)),
    )(chip, w)


def _adamw(w, g, m, v, name):
    shape = w.shape
    cols = shape[-1]
    rows = w.size // cols
    tr = _rows_tile(rows, cols, 7)
    c1 = 1.0 - ADAM_B1 ** ADAM_STEP
    c2 = 1.0 - ADAM_B2 ** ADAM_STEP

    def body(w_ref, g_ref, m_ref, v_ref, d_ref, mo_ref, vo_ref):
        gv = g_ref[...]
        mn = ADAM_B1 * m_ref[...] + (1.0 - ADAM_B1) * gv
        vn = ADAM_B2 * v_ref[...] + (1.0 - ADAM_B2) * (gv * gv)
        mo_ref[...] = mn
        vo_ref[...] = vn
        d_ref[...] = -ADAM_LR * ((mn / c1) / (jnp.sqrt(vn / c2) + ADAM_EPS) + ADAM_WD * w_ref[...])

    rs = pl.BlockSpec((tr, cols), lambda i: (i, 0))
    outs = pl.pallas_call(
        body, grid=(rows // tr,), in_specs=[rs] * 4, out_specs=[rs] * 3,
        out_shape=[jax.ShapeDtypeStruct((rows, cols), F32)] * 3,
        name=name, compiler_params=_cp(("parallel",)),
    )(*[a.reshape(rows, cols) for a in (w, g, m, v)])
    return [o.reshape(shape) for o in outs]


def _adamw_layer(w, g, m, v, l, prev, after, name):
    L, K, cols = w.shape
    tr = _rows_tile(K, cols, 7)
    c1 = 1.0 - ADAM_B1 ** ADAM_STEP
    c2 = 1.0 - ADAM_B2 ** ADAM_STEP
    nprev = 0 if prev is None else 3

    def body(*refs):
        w_ref, g_ref, m_ref, v_ref = refs[:4]
        d_ref, mo_ref, vo_ref = refs[-3:]
        gv = g_ref[...]
        mn = ADAM_B1 * m_ref[...] + (1.0 - ADAM_B1) * gv
        vn = ADAM_B2 * v_ref[...] + (1.0 - ADAM_B2) * (gv * gv)
        mo_ref[...] = mn
        vo_ref[...] = vn
        d_ref[...] = -ADAM_LR * ((mn / c1) / (jnp.sqrt(vn / c2) + ADAM_EPS) + ADAM_WD * w_ref[...])

    rs = pl.BlockSpec((None, tr, cols), lambda i: (l, i, 0))
    extra = list(prev or ()) + list(after)
    return pl.pallas_call(
        body, grid=(K // tr,), in_specs=[rs] * 4 + [_ANY] * len(extra), out_specs=[rs] * 3,
        out_shape=[jax.ShapeDtypeStruct((L, K, cols), F32)] * 3,
        input_output_aliases={4 + k: k for k in range(nprev)},
        name=name, compiler_params=_cp(("parallel",)),
    )(w, g, m, v, *extra)


def _pair_sum(core, dw, recv, name):
    nq, half, cols = recv.shape
    tr = _rows_tile(half, cols, 3)
    nb = half // tr

    def body(c_ref, a_ref, r_ref, o_ref):
        o_ref[...] = (a_ref[...].astype(F32) + r_ref[...].astype(F32)).astype(BF16)

    rs = pl.BlockSpec((None, tr, cols), lambda q, i, c: (q, i, 0))
    return pl.pallas_call(
        body,
        grid_spec=pltpu.PrefetchScalarGridSpec(
            num_scalar_prefetch=1, grid=(nq, nb),
            in_specs=[pl.BlockSpec((None, tr, cols), lambda q, i, c: (q, c[0] * nb + i, 0)), rs],
            out_specs=rs),
        out_shape=jax.ShapeDtypeStruct((nq, half, cols), BF16),
        name=name, compiler_params=_cp(("parallel", "parallel")),
    )(core, dw, recv)


def _chip_sum(chip, core, psum, land, l, n_layers, prev, name):
    _, half, cols = psum.shape
    tr = _rows_tile(half, cols, 5)
    nb = half // tr

    def body(*refs):
        p_ref, r_ref, o_ref = refs[2], refs[3], refs[-1]
        o_ref[...] = (p_ref[...].astype(F32) + r_ref[0].astype(F32) + r_ref[1].astype(F32)
                      + r_ref[2].astype(F32))

    extra = [] if prev is None else [prev]
    return pl.pallas_call(
        body,
        grid_spec=pltpu.PrefetchScalarGridSpec(
            num_scalar_prefetch=2, grid=(nb,),
            in_specs=[pl.BlockSpec((None, tr, cols), lambda i, q, c: (q[0], i, 0)),
                      pl.BlockSpec((3, tr, cols), lambda i, q, c: (0, i, 0))] + [_ANY] * len(extra),
            out_specs=pl.BlockSpec((None, tr, cols), lambda i, q, c: (l, c[0] * nb + i, 0))),
        out_shape=jax.ShapeDtypeStruct((n_layers, 2 * half, cols), F32),
        input_output_aliases={4: 0} if prev is not None else {},
        name=name, compiler_params=_cp(("parallel",)),
    )(chip, core, psum, land, *extra)


def _slab_sum(gathered, name):
    n, rows, cols = gathered.shape
    tr = _tile(rows, 512, SUBLANE)

    def body(g_ref, o_ref):
        acc = g_ref[0]
        for d in range(1, n):
            acc = acc + g_ref[d]
        o_ref[...] = acc

    return pl.pallas_call(
        body, grid=(rows // tr,),
        in_specs=[pl.BlockSpec((n, tr, cols), lambda i: (0, i, 0))],
        out_specs=pl.BlockSpec((tr, cols), lambda i: (i, 0)),
        out_shape=jax.ShapeDtypeStruct((rows, cols), F32),
        name=name, compiler_params=_cp(("parallel",)),
    )(gathered)


def _silu(x):
    return x * jax.nn.sigmoid(x)


def _ada_fwd(cvec, w_ada, b_cols, name):
    L, D, ns = w_ada.shape
    tn = _tile(ns, 768)

    def body(c_ref, w_ref, b_ref, o_ref):
        s = _silu(c_ref[...]).astype(BF16)
        o_ref[...] = jnp.dot(s, w_ref[...].astype(BF16), preferred_element_type=F32) + b_ref[...]

    return pl.pallas_call(
        body, grid=(L, ns // tn),
        in_specs=[pl.BlockSpec((16, D), lambda l, j: (0, 0)),
                  pl.BlockSpec((None, D, tn), lambda l, j: (l, 0, j)),
                  pl.BlockSpec((None, 1, tn), lambda l, j: (l, 0, j))],
        out_specs=pl.BlockSpec((None, 16, tn), lambda l, j: (l, 0, j)),
        out_shape=jax.ShapeDtypeStruct((L, 16, ns), F32),
        name=name, compiler_params=_cp(("parallel", "parallel")),
    )(cvec, w_ada, b_cols)


def _ada_bwd(cvec, dmod, w_ada, name):
    L, D, ns = w_ada.shape
    tn = _tile(ns, 768)

    def body(c_ref, d_ref, w_ref, gw_ref, ds_ref):
        first = jnp.logical_and(pl.program_id(0) == 0, pl.program_id(1) == 0)

        @pl.when(first)
        def _():
            ds_ref[...] = jnp.zeros_like(ds_ref)

        s = _silu(c_ref[...]).astype(BF16)
        db = d_ref[...].astype(BF16)
        gw_ref[...] = lax.dot_general(s, db, (((0,), (0,)), ((), ())), preferred_element_type=F32)
        ds_ref[...] += lax.dot_general(db, w_ref[...].astype(BF16), (((1,), (1,)), ((), ())),
                                       preferred_element_type=F32)

    return pl.pallas_call(
        body, grid=(L, ns // tn),
        in_specs=[pl.BlockSpec((16, D), lambda l, j: (0, 0)),
                  pl.BlockSpec((None, 16, tn), lambda l, j: (l, 0, j)),
                  pl.BlockSpec((None, D, tn), lambda l, j: (l, 0, j))],
        out_specs=[pl.BlockSpec((None, D, tn), lambda l, j: (l, 0, j)),
                   pl.BlockSpec((16, D), lambda l, j: (0, 0))],
        out_shape=[jax.ShapeDtypeStruct((L, D, ns), F32), jax.ShapeDtypeStruct((16, D), F32)],
        name=name, compiler_params=_cp(("arbitrary", "arbitrary")),
    )(cvec, dmod, w_ada)


def _cctx_grad(gathered, c_ctx, name):
    rows = c_ctx.shape[0]

    def body(g_ref, c_ref, o_ref):
        ds = g_ref[0] + g_ref[2] + g_ref[4] + g_ref[6]
        cv = c_ref[...]
        sig = jax.nn.sigmoid(cv)
        o_ref[...] = ds * sig * (1.0 + cv * (1.0 - sig))

    return pl.pallas_call(
        body, out_shape=jax.ShapeDtypeStruct((rows, LANE), F32), name=name,
    )(gathered, c_ctx)


def _place():
    x, y, c = lax.axis_index("x"), lax.axis_index("y"), lax.axis_index("c")
    chips = [(1 - x, y), (x, 1 - y), (1 - x, 1 - y)]
    return x, y, c, chips


def _all_gather_small(slab, name):
    rows, cols = slab.shape

    def body(x_ref, out_ref, send_sems, recv_sems, local_sem):
        x, y, c, chips = _place()
        me, sibling = (x, y, c), (x, y, 1 - c)

        def blk(px, py, pc):
            return out_ref.at[4 * px + 2 * py + pc]

        def copy(k, block, to, src=None):
            return pltpu.make_async_remote_copy(
                src_ref=blk(*block) if src is None else src, dst_ref=blk(*block),
                send_sem=send_sems.at[k], recv_sem=recv_sems.at[k], device_id=to, device_id_type=MESH)

        mine = pltpu.make_async_copy(x_ref, blk(*me), local_sem)
        mine.start()
        first = [copy(0, me, sibling, src=x_ref)]
        first += [copy(1 + j, me, (*chip, c), src=x_ref) for j, chip in enumerate(chips)]
        for cp in first:
            cp.start()
        passed = [copy(4 + j, (*chip, c), sibling) for j, chip in enumerate(chips)]
        for j, chip in enumerate(chips):
            copy(1 + j, (*chip, c), me).wait_recv()
            passed[j].start()
        copy(0, sibling, me).wait_recv()
        for j, chip in enumerate(chips):
            copy(4 + j, (*chip, 1 - c), me).wait_recv()
        for cp in first + passed:
            cp.wait_send()
        mine.wait()

    return pl.pallas_call(
        body, out_shape=jax.ShapeDtypeStruct((N_DEV, rows, cols), slab.dtype),
        in_specs=[pl.BlockSpec(memory_space=pltpu.VMEM)],
        out_specs=pl.BlockSpec(memory_space=pltpu.VMEM),
        scratch_shapes=[pltpu.SemaphoreType.DMA((7,)), pltpu.SemaphoreType.DMA((7,)), pltpu.SemaphoreType.DMA],
        name=name, compiler_params=pltpu.CompilerParams(vmem_limit_bytes=VMEM_MID),
    )(slab)


def _hbm(a):
    return pltpu.with_memory_space_constraint(a, pltpu.HBM)


def _half_rows(ref, q, c):
    half = ref.shape[1] // 2
    return ref.at[q, pl.ds(c * half, half)]


def _ici_copy(src, dst, send_sems, recv_sems, k, to):
    return pltpu.make_async_remote_copy(src_ref=src, dst_ref=dst, send_sem=send_sems.at[k], recv_sem=recv_sems.at[k],
                                        device_id=to, device_id_type=MESH)


def _gather_start(bufs, sizes, name):
    n = len(bufs)
    ng = len(sizes)

    def body(*refs):
        ins = refs[:n]
        sems = refs[n:n + 2 * ng]
        token = refs[-1]
        x, y, c, chips = _place()
        a = 0
        for g, sz in enumerate(sizes):
            for i in range(sz):
                part = _half_rows(ins[a], 2 * x + y, c)
                for j, chip in enumerate(chips):
                    _ici_copy(part, part, sems[2 * g], sems[2 * g + 1], 3 * i + j, (*chip, c)).start()
                a += 1
        token[...] = jnp.zeros_like(token)

    sem_shapes = []
    for sz in sizes:
        sem_shapes += [pltpu.SemaphoreType.DMA((3 * sz,)), pltpu.SemaphoreType.DMA((3 * sz,))]
    outs = pl.pallas_call(
        body, name=name,
        out_shape=tuple(sem_shapes) + tuple(pltpu.HBM(b.shape, b.dtype) for b in bufs)
                  + (jax.ShapeDtypeStruct((SUBLANE, LANE), F32),),
        in_specs=[_HBM] * n,
        out_specs=tuple([_SEM] * (2 * ng) + [_HBM] * n + [pl.BlockSpec(memory_space=pltpu.VMEM)]),
        input_output_aliases={i: 2 * ng + i for i in range(n)},
        compiler_params=pltpu.CompilerParams(has_side_effects=_EFFECT),
    )(*[_hbm(b) for b in bufs])
    sems = [(outs[2 * g], outs[2 * g + 1]) for g in range(ng)]
    return sems, list(outs[2 * ng:2 * ng + n]), outs[-1]


def _gather_wait(bufs, sems, after, name):
    n = len(bufs)
    na = len(after)

    def body(*refs):
        ins = refs[:n]
        send_sems, recv_sems = refs[n], refs[n + 1]
        x, y, c, chips = _place()
        for i in range(n):
            sent = _half_rows(ins[i], 2 * x + y, c)
            for j, (cx, cy) in enumerate(chips):
                cp = _ici_copy(sent, _half_rows(ins[i], 2 * cx + cy, c), send_sems, recv_sems, 3 * i + j, (cx, cy, c))
                cp.wait_send()
                cp.wait_recv()

    outs = pl.pallas_call(
        body, name=name,
        out_shape=tuple(pltpu.HBM(b.shape, b.dtype) for b in bufs),
        in_specs=[_HBM] * n + [_SEM, _SEM] + [_ANY] * na,
        out_specs=tuple([_HBM] * n),
        input_output_aliases={i: i for i in range(n)},
        compiler_params=pltpu.CompilerParams(has_side_effects=_EFFECT),
    )(*bufs, sems[0], sems[1], *after)
    return list(outs)


def _exchange_halves(bufs, name):
    n = len(bufs)

    def body(*refs):
        ins = refs[:n]
        send_sems, recv_sems = refs[-2:]
        x, y, c, chips = _place()
        sends = []
        for i in range(n):
            for j, (cx, cy) in enumerate(chips):
                part = _half_rows(ins[i], 2 * cx + cy, c)
                cp = _ici_copy(part, part, send_sems, recv_sems, 3 * i + j, (x, y, 1 - c))
                cp.start()
                sends.append(cp)
        for i in range(n):
            for j, (cx, cy) in enumerate(chips):
                part = _half_rows(ins[i], 2 * cx + cy, 1 - c)
                _ici_copy(part, part, send_sems, recv_sems, 3 * i + j, (x, y, 1 - c)).wait_recv()
        for cp in sends:
            cp.wait_send()

    outs = pl.pallas_call(
        body, name=name,
        out_shape=[jax.ShapeDtypeStruct(b.shape, b.dtype) for b in bufs],
        in_specs=[_ANY] * n, out_specs=[_ANY] * n,
        input_output_aliases={i: i for i in range(n)},
        scratch_shapes=[pltpu.SemaphoreType.DMA((3 * n,)), pltpu.SemaphoreType.DMA((3 * n,))],
    )(*bufs)
    return list(outs)


def _other_half(ref, c):
    half = ref.shape[1] // 2
    return ref.at[:, pl.ds((1 - c) * half, half)]


def _prereduce_exchange(dws, name):
    n = len(dws)

    def body(*refs):
        ins, outs = refs[:n], refs[n:2 * n]
        send_sems, recv_sems = refs[2 * n:]
        x, y, c, _ = _place()
        cps = []
        for i in range(n):
            cp = _ici_copy(_other_half(ins[i], c), outs[i], send_sems, recv_sems, i, (x, y, 1 - c))
            cp.start()
            cps.append(cp)
        for cp in cps:
            cp.wait()

    return pl.pallas_call(
        body, name=name,
        out_shape=[jax.ShapeDtypeStruct((d.shape[0], d.shape[1] // 2, d.shape[2]), d.dtype) for d in dws],
        in_specs=[_ANY] * n, out_specs=[_ANY] * n,
        scratch_shapes=[pltpu.SemaphoreType.DMA((n,)), pltpu.SemaphoreType.DMA((n,))],
    )(*dws)


def _scatter_start(psums, name):
    n = len(psums)

    def body(*refs):
        srcs, lands = refs[:n], refs[n:2 * n]
        send_sems, recv_sems = refs[2 * n], refs[2 * n + 1]
        token = refs[-1]
        x, y, c, chips = _place()
        for i in range(n):
            for j, (cx, cy) in enumerate(chips):
                _ici_copy(srcs[i].at[2 * cx + cy], lands[i].at[j], send_sems, recv_sems, 3 * i + j, (cx, cy, c)).start()
        token[...] = jnp.zeros_like(token)

    land_shapes = [(3,) + p.shape[1:] for p in psums]
    outs = pl.pallas_call(
        body, name=name,
        out_shape=(pltpu.SemaphoreType.DMA((3 * n,)), pltpu.SemaphoreType.DMA((3 * n,)))
                  + tuple(pltpu.HBM(p.shape, p.dtype) for p in psums)
                  + tuple(pltpu.HBM(s, BF16) for s in land_shapes)
                  + (jax.ShapeDtypeStruct((SUBLANE, LANE), F32),),
        in_specs=[_HBM] * (2 * n),
        out_specs=tuple([_SEM, _SEM] + [_HBM] * (2 * n) + [pl.BlockSpec(memory_space=pltpu.VMEM)]),
        input_output_aliases={i: 2 + i for i in range(2 * n)},
        compiler_params=pltpu.CompilerParams(has_side_effects=_EFFECT),
    )(*[_hbm(p) for p in psums], *[_hbm(lax.empty(s, BF16)) for s in land_shapes])
    return (outs[0], outs[1]), list(outs[2:2 + n]), list(outs[2 + n:2 + 2 * n]), outs[-1]


def _scatter_wait(psums, lands, sems, after, name):
    n = len(psums)
    na = len(after)

    def body(*refs):
        srcs, lnds = refs[:n], refs[n:2 * n]
        send_sems, recv_sems = refs[2 * n], refs[2 * n + 1]
        x, y, c, chips = _place()
        for i in range(n):
            for j, (cx, cy) in enumerate(chips):
                cp = _ici_copy(srcs[i].at[2 * cx + cy], lnds[i].at[j], send_sems, recv_sems, 3 * i + j, (cx, cy, c))
                cp.wait_send()
                cp.wait_recv()

    outs = pl.pallas_call(
        body, name=name,
        out_shape=tuple(pltpu.HBM(a.shape, a.dtype) for a in list(psums) + list(lands)),
        in_specs=[_HBM] * (2 * n) + [_SEM, _SEM] + [_ANY] * na,
        out_specs=tuple([_HBM] * (2 * n)),
        input_output_aliases={i: i for i in range(2 * n)},
        compiler_params=pltpu.CompilerParams(has_side_effects=_EFFECT),
    )(*psums, *lands, sems[0], sems[1], *after)
    return list(outs[:n]), list(outs[n:])


def _exchange_grads(grads, l, name):
    n = len(grads)

    def body(*refs):
        ins = refs[:n]
        send_sems, recv_sems = refs[-2:]
        x, y, c, _ = _place()

        def rows(ref, core):
            half = ref.shape[1] // 2
            return ref.at[l, pl.ds(core * half, half)]

        cps = []
        for i in range(n):
            cp = _ici_copy(rows(ins[i], c), rows(ins[i], c), send_sems, recv_sems, i, (x, y, 1 - c))
            cp.start()
            cps.append(cp)
        for i, cp in enumerate(cps):
            cp.wait_send()
            _ici_copy(rows(ins[i], 1 - c), rows(ins[i], 1 - c), send_sems, recv_sems, i, (x, y, 1 - c)).wait_recv()

    outs = pl.pallas_call(
        body, name=name,
        out_shape=[jax.ShapeDtypeStruct(g.shape, g.dtype) for g in grads],
        in_specs=[_ANY] * n, out_specs=[_ANY] * n,
        input_output_aliases={i: i for i in range(n)},
        scratch_shapes=[pltpu.SemaphoreType.DMA((n,)), pltpu.SemaphoreType.DMA((n,))],
    )(*grads)
    return list(outs)


def _pack(arrs):
    flat = jnp.concatenate([a.reshape(-1).astype(F32) for a in arrs])
    pad = (-flat.shape[0]) % (SUBLANE * LANE)
    return jnp.pad(flat, (0, pad)).reshape(-1, LANE)


def _unpack(slab, shapes):
    flat = slab.reshape(-1)
    out, off = [], 0
    for s in shapes:
        n = math.prod(s)
        out.append(flat[off:off + n].reshape(s))
        off += n
    return out


def _split6(v):
    d = v.shape[-1] // 6
    return [v[:, k * d:(k + 1) * d].reshape(2, 1, d) for k in range(6)]


BIG = ("w_in", "w_br_attn", "w_br_pool", "w_br_sgu", "w_br_conv", "w_gate", "w_o", "w_ff_gate", "w_ff_up", "w_ff_down")
KIND = {"w_in": "col", "w_br_attn": "col", "w_br_pool": "col", "w_br_sgu": "col", "w_br_conv": "col",
        "w_gate": "col", "w_o": "row", "w_ff_gate": "col", "w_ff_up": "col", "w_ff_down": "row"}
SMALL = ("c_ctx", "b_ada", "q_norm_g", "k_norm_g", "pool_w", "pool_scale", "sgu_ln_g", "sgu_ln_b", "sgu_w",
         "sgu_b", "conv_w", "b_gate", "ln1_g", "ln1_b", "ln2_g", "ln2_b")
WEIGHTS = ("c_ctx", "w_ada", "b_ada", "w_in", "q_norm_g", "k_norm_g", "pool_w", "pool_scale", "sgu_ln_g", "sgu_ln_b",
           "sgu_w", "sgu_b", "conv_w", "w_br_attn", "w_br_pool", "w_br_sgu", "w_br_conv", "w_gate", "b_gate", "w_o",
           "ln1_g", "ln1_b", "w_ff_gate", "w_ff_up", "w_ff_down", "ln2_g", "ln2_b")


def _step(x, c, ctx, loss_target, W, M, V):
    L = W["w_ada"].shape[0]
    assert L == 2, "core c of a chip carries layer c of the weight traffic"
    N, D = x.shape[1], x.shape[2]
    NC = ctx.shape[1]
    T = NC + N
    FF = W["w_ff_down"].shape[1] * N_CHIP
    assert NC % ROW_TM == 0 and N % ROW_TM == 0 and N % GRID_W == 0 and D % LANE == 0
    ncb = NC // ROW_TM
    nblk = T // ROW_TM
    alpha = (2 * L) ** 0.25
    ax, ay, ac = lax.axis_index("x"), lax.axis_index("y"), lax.axis_index("c")
    chip = 2 * ax + ay
    dev = 2 * chip + ac
    chip_arr = jnp.reshape(chip, (1,)).astype(jnp.int32)
    core_arr = jnp.reshape(ac, (1,)).astype(jnp.int32)
    ns_ada = W["w_ada"].shape[2]
    chip_devs = (0, 2, 4, 6)

    ffn_keys = ("w_ff_gate", "w_ff_up", "w_ff_down")
    mid_keys = ("w_br_attn", "w_br_pool", "w_br_sgu", "w_br_conv", "w_o")
    groups = [[("w_in", 0)], [("w_gate", 0)], [(k, 0) for k in mid_keys], [(k, 0) for k in ffn_keys],
              [(k, 1) for k in BIG]]
    flying, gsems, gtokens = [None] * len(groups), [None] * len(groups), []
    for n_start, gs in enumerate(((0, 1), (2, 3), (4,))):
        members = [kl for g in gs for kl in groups[g]]
        sems, bufs, token = _gather_start([_cast_into(chip_arr, W[k], l, "cast_%s_l%d" % (k, l)) for k, l in members],
                                          [len(groups[g]) for g in gs], "gather_start_%d" % n_start)
        gtokens.append(token)
        lo = 0
        for g, sem in zip(gs, sems):
            flying[g], gsems[g] = bufs[lo:lo + len(groups[g])], sem
            lo += len(groups[g])
    full = {k: [None] * L for k in BIG}

    def weights_ready(g, after):
        got = _gather_wait(flying[g], gsems[g], after, "gather_wait_%d" % g)
        for (k, l), buf in zip(groups[g], _exchange_halves(got, "exchange_halves_%d" % g)):
            full[k][l] = buf

    conv_shape = W["conv_w"].shape
    g0 = _all_gather_small(_pack([c, W["conv_w"]]), "gather_c")
    c_all = g0[:, :D // LANE, :].reshape(N_DEV, D)
    conv_parts = [g0[d].reshape(-1)[D:D + math.prod(conv_shape)].reshape(conv_shape) for d in chip_devs]
    conv_full = jnp.concatenate(conv_parts, axis=-1)
    cvec = jnp.concatenate([c_all, W["c_ctx"][None], jnp.zeros((16 - N_DEV - 1, D), F32)], axis=0)
    b_cols = lax.dynamic_slice_in_dim(W["b_ada"], chip * ns_ada, ns_ada, axis=1).reshape(L, 1, ns_ada)
    mod_part = _ada_fwd(cvec, W["w_ada"], b_cols, "ada_fwd")
    g1 = _all_gather_small(mod_part.reshape(-1, LANE), "gather_mod")
    mod_all = jnp.concatenate([g1[d].reshape(L, 16, ns_ada) for d in chip_devs], axis=-1)
    mod_lat = lax.dynamic_index_in_dim(mod_all, dev, axis=1, keepdims=True)
    mods = jnp.concatenate([mod_all[:, N_DEV:N_DEV + 1], mod_lat], axis=1)

    cos, sin = _rope_tables(N, NC)
    x0 = jnp.concatenate([ctx[0], x[0]], axis=0)

    saved = []
    xin = x0
    h1 = None
    dy_top = loss_parts = None
    for l in range(L):
        sh1, sc1, g1m, sh2, sc2, g2m = _split6(mods[l])
        tag = "_l%d" % l
        if l == 0:
            h1 = _modulate(xin, sc1, sh1, ncb, "modulate" + tag)
            weights_ready(0, [h1] + gtokens[1:])
        p = _mm_nn(h1, full["w_in"], l, "col", name="mm_in" + tag)
        qg, kg = W["q_norm_g"][l][None], W["k_norm_g"][l][None]
        q, k, v = _qk_fwd(p, cos, sin, qg, kg, "qk_fwd" + tag)
        att = _attn_fwd(q, k, v, NC, "attn_fwd" + tag)
        ps = W["pool_scale"][l][None]
        ypool = _pool_fwd(p, W["pool_w"][l], ps, NC, "pool_fwd" + tag)
        lg, lb = W["sgu_ln_g"][l][None], W["sgu_ln_b"][l][None]
        b_st = W["sgu_b"][l].T
        ysgu = _sgu_fwd(p, lg, lb, W["sgu_w"][l], b_st, "sgu_fwd" + tag)
        yconv = _conv_fwd(p, conv_full[l], NC, "conv_fwd" + tag)
        brs = (att, ypool, ysgu, yconv)
        if l == 0:
            weights_ready(1, list(brs))
        gates = _mm_nn(h1, full["w_gate"], l, "col", name="mm_gate" + tag, bias=W["b_gate"][l][None],
                       act=jax.nn.sigmoid, out_dtype=BF16)
        if l == 0:
            weights_ready(2, [gates])
        ts = [_mm_nn(b, full[wk], l, "col", name="mm_" + wk + tag, out_dtype=BF16)
              for b, wk in zip(brs, ("w_br_attn", "w_br_pool", "w_br_sgu", "w_br_conv"))]
        mg = _merge_fwd(gates, ts, "merge_fwd" + tag)
        m = _mm_nn(mg, full["w_o"], l, "row", name="mm_o" + tag, tn_pref=1024)
        ln1g, ln1b = W["ln1_g"][l][None], W["ln1_b"][l][None]
        x1, h2 = _resid_ln(xin, m, g1m, ln1g, ln1b, sc2, sh2, ncb, alpha, "resid_ln1" + tag)
        if l == 0:
            weights_ready(3, [h2])
        gg, uu, act = _ffn_up(h2, full["w_ff_gate"][l], full["w_ff_up"][l], "ffn_up" + tag)
        ff = _mm_nn(act, full["w_ff_down"], l, "row", name="mm_ffd" + tag, tm_pref=384, tn_pref=512)
        if l == 0:
            weights_ready(4, [ff])
        ln2g, ln2b = W["ln2_g"][l][None], W["ln2_b"][l][None]
        saved.append(dict(xin=xin, h1=h1, p=p, q=q, k=k, v=v, brs=brs, gates=gates, ts=ts, mg=mg, m=m, x1=x1, h2=h2,
                          gg=gg, uu=uu, act=act, ff=ff))
        if l + 1 < L:
            nsh1, nsc1 = _split6(mods[l + 1])[:2]
            xin, h1 = _resid_ln(x1, ff, g2m, ln2g, ln2b, nsc1, nsh1, ncb, alpha, "resid_ln2" + tag)
        else:
            dy_top, loss_parts = _resid_ln_loss(x1, ff, g2m, ln2g, ln2b, loss_target[0], ncb, alpha,
                                                "resid_ln2_loss" + tag)
    loss_dev = jnp.sum(loss_parts[:, 0, 0])

    def lat_ctx(part_rows):
        return jnp.stack([jnp.sum(part_rows[:ncb], axis=0), jnp.sum(part_rows[ncb:], axis=0)])

    dW = {k: [None] * L for k in BIG}
    small = {k: [None] * L for k in ("q_norm_g", "k_norm_g", "pool_w", "pool_scale", "sgu_ln_g", "sgu_ln_b", "sgu_w",
                                     "sgu_b", "conv_w", "b_gate", "ln1_g", "ln1_b", "ln2_g", "ln2_b")}
    dmods = [None] * L
    mix_keys = tuple(k for k in BIG if k not in ffn_keys)
    grads_big = {k: None for k in BIG}
    adam_out = {k: None for k in BIG}

    def reduce_start(keys, l, tg):
        dws = [dW[k][l] for k in keys]
        recv = _prereduce_exchange(dws, "prereduce_" + tg)
        psums = [_pair_sum(core_arr, d, r, "pair_sum_%s_l%d" % (k, l)) for k, d, r in zip(keys, dws, recv)]
        sems, src, land, token = _scatter_start(psums, "scatter_start_" + tg)
        return dict(keys=keys, l=l, sems=sems, src=src, land=land, token=token, tg=tg)

    def reduce_finish(st, after):
        src, land = _scatter_wait(st["src"], st["land"], st["sems"], after, "scatter_wait_" + st["tg"])
        for k, p_, r in zip(st["keys"], src, land):
            grads_big[k] = _chip_sum(chip_arr, core_arr, p_, r, st["l"], L, grads_big[k],
                                     "chip_sum_%s_l%d" % (k, st["l"]))
        done = _exchange_grads([grads_big[k] for k in st["keys"]], st["l"], "exchange_grads_" + st["tg"])
        grads_big.update(zip(st["keys"], done))

    def adam(keys, l, after):
        for k in keys:
            adam_out[k] = _adamw_layer(W[k], grads_big[k], M[k], V[k], l, adam_out[k], after,
                                       "adamw_%s_l%d" % (k, l))

    dxa, dhs, sc_prev = dy_top, [], None
    red_l1 = red_ffn = None
    for l in reversed(range(L)):
        s = saved[l]
        sh1, sc1, g1m, sh2, sc2, g2m = _split6(mods[l])
        tag = "_l%d" % l
        ln1g, ln1b = W["ln1_g"][l][None], W["ln1_b"][l][None]
        ln2g, ln2b = W["ln2_g"][l][None], W["ln2_b"][l][None]
        dx1a, dff, part2 = _ln_bwd(dxa, dhs, sc_prev if dhs else sc1, s["x1"], s["ff"], g2m, ln2g, ln2b, ncb, alpha,
                                   "ln2_bwd" + tag, after=None if red_l1 is None else red_l1["token"])
        small["ln2_g"][l] = jnp.sum(part2[:, 0], axis=0)
        small["ln2_b"][l] = jnp.sum(part2[:, 1], axis=0)
        dg2 = lat_ctx(part2[:, 2])
        if dhs:
            dmods[l + 1][1], dmods[l + 1][0] = lat_ctx(part2[:, 3]), lat_ctx(part2[:, 4])
        dgg, duu = _ffn_down_bwd(dff, full["w_ff_down"][l], s["gg"], s["uu"], "ffn_down_bwd" + tag)
        dW["w_ff_down"][l] = _mm_tn(s["act"], dff, "row", nq=N_CHIP, kdim=FF // N_CHIP, ndim=D, name="tn_ffd" + tag,
                                    tk_pref=1408, tn_pref=512)
        dh2a = _mm_nt(dgg, full["w_ff_gate"], l, "col", name="nt_ffg" + tag)
        dh2b = _mm_nt(duu, full["w_ff_up"], l, "col", name="nt_ffu" + tag)
        dW["w_ff_gate"][l] = _mm_tn(s["h2"], dgg, "col", nq=N_CHIP, kdim=D, ndim=FF // N_CHIP, name="tn_ffg" + tag,
                                    tn_pref=1408)
        dW["w_ff_up"][l] = _mm_tn(s["h2"], duu, "col", nq=N_CHIP, kdim=D, ndim=FF // N_CHIP, name="tn_ffu" + tag,
                                  tn_pref=1408)
        if l == 0:
            red_ffn = reduce_start(ffn_keys, 0, "l0_ffn")
            reduce_finish(red_l1, [red_ffn["token"]])
        dx0a, dm, part1 = _ln_bwd(dx1a, [dh2a, dh2b], sc2, s["xin"], s["m"], g1m, ln1g, ln1b, ncb, alpha,
                                  "ln1_bwd" + tag, after=None if red_ffn is None else red_ffn["token"])
        small["ln1_g"][l] = jnp.sum(part1[:, 0], axis=0)
        small["ln1_b"][l] = jnp.sum(part1[:, 1], axis=0)
        dg1 = lat_ctx(part1[:, 2])
        dsc2, dsh2 = lat_ctx(part1[:, 3]), lat_ctx(part1[:, 4])
        dmods[l] = [None, None, dg1, dsh2, dsc2, dg2]
        dmg = _mm_nt(dm, full["w_o"], l, "row", name="nt_o" + tag)
        dW["w_o"][l] = _mm_tn(s["mg"], dm, "row", nq=N_CHIP, kdim=D // N_CHIP, ndim=D, name="tn_o" + tag, tn_pref=1024)
        dpre, dt0, dt1, dt2, dt3, bpart = _merge_bwd(dmg, s["gates"], s["ts"], "merge_bwd" + tag)
        small["b_gate"][l] = jnp.sum(bpart[:, 0], axis=0)
        dh1a = _mm_nt(dpre, full["w_gate"], l, "col", name="nt_gate" + tag)
        dW["w_gate"][l] = _mm_tn(s["h1"], dpre, "col", nq=N_CHIP, kdim=D, ndim=D, name="tn_gate" + tag, tn_pref=1024)
        dbrs = []
        for b, dt, wk, odt in zip(s["brs"], (dt0, dt1, dt2, dt3), ("w_br_attn", "w_br_pool", "w_br_sgu", "w_br_conv"),
                                  (BF16, F32, F32, F32)):
            dbrs.append(_mm_nt(dt, full[wk], l, "col", name="nt_" + wk + tag, out_dtype=odt))
            dW[wk][l] = _mm_tn(b, dt, "col", nq=N_CHIP, kdim=b.shape[1], ndim=D // N_CHIP, name="tn_" + wk + tag)
        qg, kg = W["q_norm_g"][l][None], W["k_norm_g"][l][None]
        dq, dk, dv = _attn_bwd(s["q"], s["k"], s["v"], dbrs[0], NC, "attn_bwd" + tag)
        dp_qkv, qkpart = _qk_bwd(s["p"], dq, dk, dv, cos, sin, qg, kg, "qk_bwd" + tag)
        small["q_norm_g"][l] = jnp.sum(qkpart[:, 0], axis=0)
        small["k_norm_g"][l] = jnp.sum(qkpart[:, 1], axis=0)
        ps = W["pool_scale"][l][None]
        dp_pool, small["pool_w"][l], dps = _pool_bwd(s["p"], dbrs[1], W["pool_w"][l], ps, NC, "pool_bwd" + tag)
        small["pool_scale"][l] = dps[0]
        lg, lb = W["sgu_ln_g"][l][None], W["sgu_ln_b"][l][None]
        dp_sgu, small["sgu_w"][l], dsacc, dln = _sgu_bwd(s["p"], dbrs[2], lg, lb, W["sgu_w"][l], W["sgu_b"][l].T,
                                                         "sgu_bwd" + tag)
        small["sgu_b"][l] = jnp.sum(dsacc.reshape(GC, N_GROUP, GC), axis=-1).T
        small["sgu_ln_g"][l], small["sgu_ln_b"][l] = dln[0], dln[1]
        dp_conv, small["conv_w"][l] = _conv_bwd(s["p"], dbrs[3], conv_full[l], NC, "conv_bwd" + tag)
        dp = jnp.concatenate([dp_qkv, dp_pool, dp_sgu, dp_conv[0], dp_conv[1], dp_conv[2]], axis=-1)
        dh1b = _mm_nt(dp, full["w_in"], l, "col", name="nt_in" + tag)
        dW["w_in"][l] = _mm_tn(s["h1"], dp, "col", nq=N_CHIP, kdim=D, ndim=IN_W // N_CHIP, name="tn_in" + tag,
                               tn_pref=1152)
        dxa, dhs, sc_prev = dx0a, [dh1a, dh1b], sc1
        if l == 1:
            red_l1 = reduce_start(BIG, 1, "l1")
    dx0, part0 = _mod_bwd(dxa, dhs, sc_prev, x0, ncb, "mod_bwd")
    dmods[0][1], dmods[0][0] = lat_ctx(part0[:, 0]), lat_ctx(part0[:, 1])
    grad_x = dx0[NC:][None]
    dmod = jnp.stack([jnp.concatenate(dmods[l], axis=-1) for l in range(L)])

    red_mix = reduce_start(mix_keys, 0, "l0_mix")
    adam(BIG, 1, [red_mix["token"]])
    reduce_finish(red_ffn, [red_mix["token"]])
    adam(ffn_keys, 0, [])

    small_names = [k for k in SMALL if k not in ("c_ctx", "b_ada")]
    small_arrs = [jnp.stack(small[k]) for k in small_names]
    small_shapes = [a.shape for a in small_arrs]
    slab = _pack([loss_dev.reshape(1), jnp.zeros((LANE - 1,), F32), dmod] + small_arrs)
    g2 = _all_gather_small(slab, "gather_small")
    total = _slab_sum(g2, "slab_sum")
    flat_total = total.reshape(-1)
    loss = flat_total[0]
    nmod = L * 2 * 6 * D
    dmod_sum = flat_total[LANE:LANE + nmod].reshape(L, 2, 6 * D)
    small_grads = dict(zip(small_names, _unpack(flat_total[LANE + nmod:], small_shapes)))
    dmod_lat = g2.reshape(N_DEV, -1)[:, LANE:LANE + nmod].reshape(N_DEV, L, 2, 6 * D)[:, :, 1]
    dm16 = jnp.concatenate([jnp.transpose(dmod_lat, (1, 0, 2)), dmod_sum[:, 0:1],
                            jnp.zeros((L, 16 - N_DEV - 1, 6 * D), F32)], axis=1)
    small_grads["b_ada"] = dmod_sum[:, 0] + dmod_sum[:, 1]
    dm16_cols = lax.dynamic_slice_in_dim(dm16, chip * ns_ada, ns_ada, axis=2)
    grad_w_ada, ds16 = _ada_bwd(cvec, dm16_cols, W["w_ada"], "ada_bwd")
    g3 = _all_gather_small(ds16[N_DEV].reshape(-1, LANE), "gather_dsilu")
    small_grads["c_ctx"] = _cctx_grad(g3, W["c_ctx"].reshape(-1, LANE), "cctx_grad").reshape(D)
    conv_grad_full = small_grads["conv_w"]
    small_grads["conv_w"] = lax.dynamic_slice_in_dim(conv_grad_full, chip * GC, GC, axis=2)

    delta, new_m, new_v = {}, {}, {}
    delta["w_ada"], new_m["w_ada"], new_v["w_ada"] = _adamw(W["w_ada"], grad_w_ada, M["w_ada"], V["w_ada"],
                                                            "adamw_w_ada")
    shapes = [W[k].shape for k in SMALL]
    sd, sm, sv = _adamw(_pack([W[k] for k in SMALL]), _pack([small_grads[k] for k in SMALL]),
                        _pack([M[k] for k in SMALL]), _pack([V[k] for k in SMALL]), "adamw_small")
    for k, d_, m_, v_ in zip(SMALL, _unpack(sd, shapes), _unpack(sm, shapes), _unpack(sv, shapes)):
        delta[k], new_m[k], new_v[k] = d_, m_, v_
    reduce_finish(red_mix, [delta["w_ada"], sd] + [adam_out[k][0] for k in BIG])
    adam(mix_keys, 0, [])
    for k in BIG:
        delta[k], new_m[k], new_v[k] = adam_out[k]
    grads = dict(grads_big)
    grads["w_ada"] = grad_w_ada
    grads.update(small_grads)
    return (loss, grad_x, *[grads[k] for k in WEIGHTS], *[delta[k] for k in WEIGHTS],
            *[new_m[k] for k in WEIGHTS], *[new_v[k] for k in WEIGHTS])


def kernel(x, c, ctx, c_ctx, w_ada, b_ada, w_in, q_norm_g, k_norm_g, pool_w, pool_scale, sgu_ln_g, sgu_ln_b, sgu_w, sgu_b, conv_w, w_br_attn, w_br_pool, w_br_sgu, w_br_conv, w_gate, b_gate, w_o, ln1_g, ln1_b, w_ff_gate, w_ff_up, w_ff_down, ln2_g, ln2_b, loss_target, m_c_ctx, m_w_ada, m_b_ada, m_w_in, m_q_norm_g, m_k_norm_g, m_pool_w, m_pool_scale, m_sgu_ln_g, m_sgu_ln_b, m_sgu_w, m_sgu_b, m_conv_w, m_w_br_attn, m_w_br_pool, m_w_br_sgu, m_w_br_conv, m_w_gate, m_b_gate, m_w_o, m_ln1_g, m_ln1_b, m_w_ff_gate, m_w_ff_up, m_w_ff_down, m_ln2_g, m_ln2_b, v_c_ctx, v_w_ada, v_b_ada, v_w_in, v_q_norm_g, v_k_norm_g, v_pool_w, v_pool_scale, v_sgu_ln_g, v_sgu_ln_b, v_sgu_w, v_sgu_b, v_conv_w, v_w_br_attn, v_w_br_pool, v_w_br_sgu, v_w_br_conv, v_w_gate, v_b_gate, v_w_o, v_ln1_g, v_ln1_b, v_w_ff_gate, v_w_ff_up, v_w_ff_down, v_ln2_g, v_ln2_b):
    args = locals()
    W = {k: args[k] for k in WEIGHTS}
    M = {k: args["m_" + k] for k in WEIGHTS}
    V = {k: args["v_" + k] for k in WEIGHTS}
    return _step(x, c, ctx, loss_target, W, M, V)
```

```python
import functools
import math

import jax
import jax.numpy as jnp
from jax import lax
from jax.experimental import pallas as pl
from jax.experimental.pallas import tpu as pltpu

F32 = jnp.float32
BF16 = jnp.bfloat16
MESH = pl.DeviceIdType.MESH

N_DEV = 8
N_CHIP = 4
GRID_W = 64
HEAD_DIM = 128
N_HEADS = 8
N_KV = 2
KV_GROUP = N_HEADS // N_KV
Q_W = N_HEADS * HEAD_DIM
KV_W = N_KV * HEAD_DIM
QKV_W = Q_W + 2 * KV_W
GC = 128
N_GROUP = 4
BR_W = N_GROUP * GC
POOL_WINDOWS = (2, 4, 8, 16)
OFF_POOL = QKV_W
OFF_U = OFF_POOL + BR_W
OFF_VG = OFF_U + BR_W
OFF_CB = OFF_VG + BR_W
OFF_CC = OFF_CB + BR_W
OFF_CX = OFF_CC + BR_W
IN_W = OFF_CX + BR_W
Q_BLOCK = 128
ROPE_THETA = 10000.0
LN_EPS = 1e-5
RMS_EPS = 1e-6
ADAM_LR = 0.001
ADAM_B1 = 0.9
ADAM_B2 = 0.999
ADAM_EPS = 1e-08
ADAM_WD = 0.01
ADAM_STEP = 10

ROW_TM = 256
LANE = 128
SUBLANE = 8
VMEM_BIG = 56 * 1024 * 1024
VMEM_MID = 40 * 1024 * 1024


def _cp(sem=None, vmem=VMEM_MID):
    return pltpu.CompilerParams(dimension_semantics=sem, vmem_limit_bytes=vmem)


_ANY = pl.BlockSpec(memory_space=pl.ANY)
_HBM = pl.BlockSpec(memory_space=pltpu.HBM)
_SEM = pl.BlockSpec(memory_space=pltpu.SEMAPHORE)
_EFFECT = pltpu.SideEffectType.DATAFLOW_SIDE_EFFECTING


def _tile(n, pref, unit=LANE, whole_ok=False):
    if n <= pref:
        return n
    best = None
    for t in range(unit, pref + 1, unit):
        if n % t == 0:
            best = t
    if whole_ok and (best is None or best * 4 < pref):
        return n
    assert best is not None, (n, pref, unit)
    return best


def _layer_of(w, l):
    if isinstance(w, (list, tuple)):
        return w[l][:, None], 0
    return w, l


def _mm_nn(a, w, l, kind, *, name, out_dtype=F32, bias=None, act=None, tm_pref=768, tn_pref=1152):
    w, l = _layer_of(w, l)
    T, K = a.shape
    tm = _tile(T, tm_pref, SUBLANE * 2)
    if kind == "col":
        nq, _, kw, ns = w.shape
        assert kw == K
        tn = _tile(ns, tn_pref, whole_ok=True)
        nj = ns // tn
        n_total = nq * ns
        w_spec = pl.BlockSpec((None, None, K, tn), lambda j, i: (j // nj, l, 0, j % nj))
        grid_n = nq * nj
    else:
        nq, _, kc, n_total = w.shape
        assert nq * kc == K
        tn = _tile(n_total, tn_pref)
        w_spec = pl.BlockSpec((nq, None, kc, tn), lambda j, i: (0, l, 0, j))
        grid_n = n_total // tn

    def body(*refs):
        if bias is not None:
            a_ref, w_ref, b_ref, o_ref = refs
        else:
            a_ref, w_ref, o_ref = refs
        wv = w_ref[...]
        if kind == "row":
            wv = wv.reshape(K, tn)
        acc = jnp.dot(a_ref[...], wv, preferred_element_type=F32)
        if bias is not None:
            acc = acc + b_ref[...]
        if act is not None:
            acc = act(acc)
        o_ref[...] = acc.astype(out_dtype)

    in_specs = [pl.BlockSpec((tm, K), lambda j, i: (i, 0)), w_spec]
    args = [a, w]
    if bias is not None:
        in_specs.append(pl.BlockSpec((1, tn), lambda j, i: (0, j)))
        args.append(bias)
    return pl.pallas_call(
        body, grid=(grid_n, T // tm), in_specs=in_specs,
        out_specs=pl.BlockSpec((tm, tn), lambda j, i: (i, j)),
        out_shape=jax.ShapeDtypeStruct((T, n_total), out_dtype),
        name=name, compiler_params=_cp(("parallel", "parallel"), VMEM_BIG),
    )(*args)


def _mm_nt(dc, w, l, kind, *, name, out_dtype=F32, dc_off=0, tm_pref=768, tk_pref=1024):
    w, l = _layer_of(w, l)
    T = dc.shape[0]
    tm = _tile(T, tm_pref, SUBLANE * 2)
    if kind == "col":
        nq, _, K, ns = w.shape
        tk = _tile(K, tk_pref)
        assert dc_off % ns == 0
        offb = dc_off // ns

        def body(dc_ref, w_ref, o_ref, acc_ref):
            q = pl.program_id(2)
            part = lax.dot_general(dc_ref[...], w_ref[...], (((1,), (1,)), ((), ())),
                                   preferred_element_type=F32)

            @pl.when(q == 0)
            def _():
                acc_ref[...] = part

            @pl.when(q > 0)
            def _():
                acc_ref[...] += part

            @pl.when(q == nq - 1)
            def _():
                o_ref[...] = acc_ref[...].astype(out_dtype)

        return pl.pallas_call(
            body, grid=(K // tk, T // tm, nq),
            in_specs=[pl.BlockSpec((tm, ns), lambda kk, i, q: (i, offb + q)),
                      pl.BlockSpec((None, None, tk, ns), lambda kk, i, q: (q, l, kk, 0))],
            out_specs=pl.BlockSpec((tm, tk), lambda kk, i, q: (i, kk)),
            out_shape=jax.ShapeDtypeStruct((T, K), out_dtype),
            scratch_shapes=[pltpu.VMEM((tm, tk), F32)],
            name=name, compiler_params=_cp(("parallel", "parallel", "arbitrary"), VMEM_BIG),
        )(dc, w)

    nq, _, kc, n = w.shape
    assert dc_off % n == 0
    offb = dc_off // n
    tk = _tile(kc, tk_pref, whole_ok=True)
    nkk = kc // tk

    def body(dc_ref, w_ref, o_ref):
        o_ref[...] = lax.dot_general(dc_ref[...], w_ref[...], (((1,), (1,)), ((), ())),
                                     preferred_element_type=F32).astype(out_dtype)

    return pl.pallas_call(
        body, grid=(nq * nkk, T // tm),
        in_specs=[pl.BlockSpec((tm, n), lambda j, i: (i, offb)),
                  pl.BlockSpec((None, None, tk, n), lambda j, i: (j // nkk, l, j % nkk, 0))],
        out_specs=pl.BlockSpec((tm, tk), lambda j, i: (i, j)),
        out_shape=jax.ShapeDtypeStruct((T, nq * kc), out_dtype),
        name=name, compiler_params=_cp(("parallel", "parallel"), VMEM_BIG),
    )(dc, w)


def _mm_tn(a, dc, kind, *, nq, kdim, ndim, name, a_off=0, dc_off=0, tk_pref=512, tn_pref=1152, out_dtype=BF16):
    T = a.shape[0]
    assert dc.shape[0] == T
    tk = _tile(kdim, tk_pref, whole_ok=True)
    tn = _tile(ndim, tn_pref, whole_ok=True)
    nkk, njn = kdim // tk, ndim // tn
    assert a_off % tk == 0 and dc_off % tn == 0
    aoffb, doffb = a_off // tk, dc_off // tn
    if kind == "col":
        a_map = lambda q, kk, jn: (0, aoffb + kk)
        d_map = lambda q, kk, jn: (0, doffb + q * njn + jn)
    else:
        a_map = lambda q, kk, jn: (0, aoffb + q * nkk + kk)
        d_map = lambda q, kk, jn: (0, doffb + jn)

    def body(a_ref, d_ref, o_ref):
        o_ref[...] = lax.dot_general(a_ref[...], d_ref[...], (((0,), (0,)), ((), ())),
                                     preferred_element_type=F32).astype(out_dtype)

    return pl.pallas_call(
        body, grid=(nq, nkk, njn),
        in_specs=[pl.BlockSpec((T, tk), a_map), pl.BlockSpec((T, tn), d_map)],
        out_specs=pl.BlockSpec((None, tk, tn), lambda q, kk, jn: (q, kk, jn)),
        out_shape=jax.ShapeDtypeStruct((nq, kdim, ndim), out_dtype),
        name=name, compiler_params=_cp(("parallel", "parallel", "parallel"), VMEM_BIG),
    )(a, dc)


def _row_spec(d):
    return pl.BlockSpec((ROW_TM, d), lambda i: (i, 0))


def _mod_spec(d, ncb):
    return pl.BlockSpec((None, 1, d), lambda i: (jnp.where(i >= ncb, 1, 0), 0, 0))


def _vec_spec(d):
    return pl.BlockSpec((1, d), lambda i: (0, 0))


def _part_spec(d):
    return pl.BlockSpec((None, SUBLANE, d), lambda i: (i, 0, 0))


def _modulate(x, sc, sh, ncb, name):
    T, D = x.shape

    def body(x_ref, sc_ref, sh_ref, o_ref):
        o_ref[...] = (x_ref[...] * (1.0 + sc_ref[...]) + sh_ref[...]).astype(BF16)

    return pl.pallas_call(
        body, grid=(T // ROW_TM,),
        in_specs=[_row_spec(D), _mod_spec(D, ncb), _mod_spec(D, ncb)],
        out_specs=_row_spec(D), out_shape=jax.ShapeDtypeStruct((T, D), BF16),
        name=name, compiler_params=_cp(("parallel",)),
    )(x, sc, sh)


def _ln_stats(r):
    mu = jnp.mean(r, axis=-1, keepdims=True)
    rc = r - mu
    var = jnp.mean(rc * rc, axis=-1, keepdims=True)
    rstd = lax.rsqrt(var + LN_EPS)
    return rc * rstd, rstd


def _resid_ln(x, y, gate, g, b, sc, sh, ncb, alpha, name):
    T, D = x.shape

    def body(x_ref, y_ref, gate_ref, g_ref, b_ref, sc_ref, sh_ref, xo_ref, h_ref):
        xhat, _ = _ln_stats(alpha * x_ref[...] + gate_ref[...] * y_ref[...])
        xo = xhat * g_ref[...] + b_ref[...]
        xo_ref[...] = xo
        h_ref[...] = (xo * (1.0 + sc_ref[...]) + sh_ref[...]).astype(BF16)

    return pl.pallas_call(
        body, grid=(T // ROW_TM,),
        in_specs=[_row_spec(D), _row_spec(D), _mod_spec(D, ncb), _vec_spec(D), _vec_spec(D),
                  _mod_spec(D, ncb), _mod_spec(D, ncb)],
        out_specs=[_row_spec(D), _row_spec(D)],
        out_shape=[jax.ShapeDtypeStruct((T, D), F32), jax.ShapeDtypeStruct((T, D), BF16)],
        name=name, compiler_params=_cp(("parallel",)),
    )(x, y, gate, g, b, sc, sh)


def _resid_ln_loss(x, y, gate, g, b, target, ncb, alpha, name):
    T, D = x.shape
    nblk = T // ROW_TM

    def body(x_ref, y_ref, gate_ref, g_ref, b_ref, t_ref, dy_ref, loss_ref):
        i = pl.program_id(0)
        xhat, _ = _ln_stats(alpha * x_ref[...] + gate_ref[...] * y_ref[...])
        xo = xhat * g_ref[...] + b_ref[...]
        live = (i >= ncb).astype(F32)
        err = (xo - t_ref[...]) * live
        dy_ref[...] = err * (1.0 / D)
        loss_ref[...] = jnp.full((SUBLANE, LANE), 0.5 / D, F32) * jnp.sum(err * err)

    return pl.pallas_call(
        body, grid=(nblk,),
        in_specs=[_row_spec(D), _row_spec(D), _mod_spec(D, ncb), _vec_spec(D), _vec_spec(D),
                  pl.BlockSpec((ROW_TM, D), lambda i: (jnp.maximum(i - ncb, 0), 0))],
        out_specs=[_row_spec(D), _part_spec(LANE)],
        out_shape=[jax.ShapeDtypeStruct((T, D), F32), jax.ShapeDtypeStruct((nblk, SUBLANE, LANE), F32)],
        name=name, compiler_params=_cp(("parallel",)),
    )(x, y, gate, g, b, target)


def _write_parts(part_ref, rows, d):
    for k, r in enumerate(rows):
        part_ref[pl.ds(k, 1), :] = jnp.sum(r, axis=0, keepdims=True)
    if len(rows) < SUBLANE:
        part_ref[pl.ds(len(rows), SUBLANE - len(rows)), :] = jnp.zeros((SUBLANE - len(rows), d), F32)


def _ln_bwd(dxa, dhs, sc, x, y, gate, g, b, ncb, alpha, name, after=None):
    T, D = x.shape
    nblk = T // ROW_TM
    ndh = len(dhs)

    def body(*refs):
        dxa_ref = refs[0]
        dh_refs = refs[1:1 + ndh]
        sc_ref, x_ref, y_ref, gate_ref, g_ref, b_ref = refs[1 + ndh:7 + ndh]
        dx_ref, dy_ref, part_ref = refs[-3:]
        yv = y_ref[...]
        xhat, rstd = _ln_stats(alpha * x_ref[...] + gate_ref[...] * yv)
        dxo = dxa_ref[...]
        rows = []
        if ndh:
            dh = dh_refs[0][...]
            for r in dh_refs[1:]:
                dh = dh + r[...]
            dxo = dxo + dh * (1.0 + sc_ref[...])
            xo = xhat * g_ref[...] + b_ref[...]
            rows = [dh * xo, dh]
        dxhat = dxo * g_ref[...]
        m1 = jnp.mean(dxhat, axis=-1, keepdims=True)
        m2 = jnp.mean(dxhat * xhat, axis=-1, keepdims=True)
        dr = rstd * (dxhat - m1 - xhat * m2)
        dx_ref[...] = alpha * dr
        dy_ref[...] = (gate_ref[...] * dr).astype(BF16)
        _write_parts(part_ref, [dxo * xhat, dxo, dr * yv] + rows, D)

    in_specs = ([_row_spec(D)] * (1 + ndh)
                + [_mod_spec(D, ncb), _row_spec(D), _row_spec(D), _mod_spec(D, ncb), _vec_spec(D), _vec_spec(D)])
    extra = list(after or ())
    return pl.pallas_call(
        body, grid=(nblk,), in_specs=in_specs + [_ANY] * len(extra),
        out_specs=[_row_spec(D), _row_spec(D), _part_spec(D)],
        out_shape=[jax.ShapeDtypeStruct((T, D), F32), jax.ShapeDtypeStruct((T, D), BF16),
                   jax.ShapeDtypeStruct((nblk, SUBLANE, D), F32)],
        name=name, compiler_params=_cp(("parallel",)),
    )(dxa, *dhs, sc, x, y, gate, g, b, *extra)


def _mod_bwd(dxa, dhs, sc, x, ncb, name, after=()):
    T, D = x.shape
    nblk = T // ROW_TM
    ndh = len(dhs)

    def body(*refs):
        dxa_ref = refs[0]
        dh_refs = refs[1:1 + ndh]
        sc_ref, x_ref = refs[1 + ndh:3 + ndh]
        dx_ref, part_ref = refs[-2:]
        dh = dh_refs[0][...]
        for r in dh_refs[1:]:
            dh = dh + r[...]
        dx_ref[...] = dxa_ref[...] + dh * (1.0 + sc_ref[...])
        _write_parts(part_ref, [dh * x_ref[...], dh], D)

    return pl.pallas_call(
        body, grid=(nblk,),
        in_specs=[_row_spec(D)] * (1 + ndh) + [_mod_spec(D, ncb), _row_spec(D)] + [_ANY] * len(after),
        out_specs=[_row_spec(D), _part_spec(D)],
        out_shape=[jax.ShapeDtypeStruct((T, D), F32), jax.ShapeDtypeStruct((nblk, SUBLANE, D), F32)],
        name=name, compiler_params=_cp(("parallel",)),
    )(dxa, *dhs, sc, x, *after)


def _merge_fwd(gates, ts, name):
    T, D = ts[0].shape
    tm = 128

    def body(g_ref, t0, t1, t2, t3, o_ref):
        acc = g_ref[:, 0:D].astype(F32) * t0[...].astype(F32)
        for k, t in enumerate((t1, t2, t3), start=1):
            acc = acc + g_ref[:, k * D:(k + 1) * D].astype(F32) * t[...].astype(F32)
        o_ref[...] = acc.astype(BF16)

    rs = pl.BlockSpec((tm, D), lambda i: (i, 0))
    return pl.pallas_call(
        body, grid=(T // tm,),
        in_specs=[pl.BlockSpec((tm, 4 * D), lambda i: (i, 0)), rs, rs, rs, rs],
        out_specs=rs, out_shape=jax.ShapeDtypeStruct((T, D), BF16),
        name=name, compiler_params=_cp(("parallel",)),
    )(gates, *ts)


def _merge_bwd(dmg, gates, ts, name):
    T, D = dmg.shape
    tm = 128
    nblk = T // tm

    def body(d_ref, g_ref, t0, t1, t2, t3, dpre_ref, dt0, dt1, dt2, dt3, part_ref):
        d = d_ref[...]
        for k, (t, dt) in enumerate(zip((t0, t1, t2, t3), (dt0, dt1, dt2, dt3))):
            gk = g_ref[:, k * D:(k + 1) * D].astype(F32)
            dt[...] = (d * gk).astype(BF16)
            dpre = d * t[...].astype(F32) * gk * (1.0 - gk)
            dpre_ref[:, k * D:(k + 1) * D] = dpre.astype(BF16)
            part_ref[:, k * D:(k + 1) * D] = jnp.sum(dpre, axis=0, keepdims=True)

    rs = pl.BlockSpec((tm, D), lambda i: (i, 0))
    wide = pl.BlockSpec((tm, 4 * D), lambda i: (i, 0))
    return pl.pallas_call(
        body, grid=(nblk,),
        in_specs=[rs, wide, rs, rs, rs, rs],
        out_specs=[wide, rs, rs, rs, rs, pl.BlockSpec((None, 1, 4 * D), lambda i: (i, 0, 0))],
        out_shape=[jax.ShapeDtypeStruct((T, 4 * D), BF16)] + [jax.ShapeDtypeStruct((T, D), BF16)] * 4
                  + [jax.ShapeDtypeStruct((nblk, 1, 4 * D), F32)],
        name=name, compiler_params=_cp(("parallel",)),
    )(dmg, gates, *ts)


def _ffn_up(a, wg, wu, name, tm_pref=384):
    T, K = a.shape
    nq, _, ns = wg.shape
    tm = _tile(T, tm_pref, SUBLANE * 2)

    def body(a_ref, g_ref, u_ref, gg_ref, uu_ref, act_ref):
        av = a_ref[...]
        g = jnp.dot(av, g_ref[...], preferred_element_type=F32)
        u = jnp.dot(av, u_ref[...], preferred_element_type=F32)
        gg_ref[...] = g.astype(BF16)
        uu_ref[...] = u.astype(BF16)
        act_ref[...] = (g * jax.nn.sigmoid(g) * u).astype(BF16)

    ws = pl.BlockSpec((None, K, ns), lambda q, i: (q, 0, 0))
    os = pl.BlockSpec((tm, ns), lambda q, i: (i, q))
    return pl.pallas_call(
        body, grid=(nq, T // tm),
        in_specs=[pl.BlockSpec((tm, K), lambda q, i: (i, 0)), ws, ws], out_specs=[os, os, os],
        out_shape=[jax.ShapeDtypeStruct((T, nq * ns), BF16)] * 3,
        name=name, compiler_params=_cp(("parallel", "parallel"), VMEM_BIG),
    )(a, wg, wu)


def _ffn_down_bwd(dff, wd, gg, uu, name, tm_pref=384):
    T, N = dff.shape
    nq, kc, _ = wd.shape
    tm = _tile(T, tm_pref, SUBLANE * 2)

    def body(d_ref, w_ref, g_ref, u_ref, dg_ref, du_ref):
        da = lax.dot_general(d_ref[...], w_ref[...], (((1,), (1,)), ((), ())), preferred_element_type=F32)
        g = g_ref[...].astype(F32)
        sig = jax.nn.sigmoid(g)
        dg_ref[...] = (da * u_ref[...].astype(F32) * sig * (1.0 + g * (1.0 - sig))).astype(BF16)
        du_ref[...] = (da * g * sig).astype(BF16)

    ts = pl.BlockSpec((tm, kc), lambda q, i: (i, q))
    return pl.pallas_call(
        body, grid=(nq, T // tm),
        in_specs=[pl.BlockSpec((tm, N), lambda q, i: (i, 0)), pl.BlockSpec((None, kc, N), lambda q, i: (q, 0, 0)),
                  ts, ts],
        out_specs=[ts, ts], out_shape=[jax.ShapeDtypeStruct((T, nq * kc), BF16)] * 2,
        name=name, compiler_params=_cp(("parallel", "parallel"), VMEM_BIG),
    )(dff, wd, gg, uu)


def _swap_halves(v):
    lane = lax.broadcasted_iota(jnp.int32, v.shape, 1)
    return jnp.where((lane % 64) < 32, pltpu.roll(v, 96, 1), pltpu.roll(v, 32, 1))


def _rope_tables(n, nc):
    rows = n // GRID_W
    row = jnp.repeat(jnp.arange(rows), GRID_W).astype(F32)
    col = jnp.tile(jnp.arange(GRID_W), rows).astype(F32)
    inv = ROPE_THETA ** (-jnp.arange(0, 64, 2, dtype=F32) / 64)
    ang_r = row[:, None] * inv
    ang_c = col[:, None] * inv
    cos = jnp.concatenate([jnp.cos(ang_r), jnp.cos(ang_r), jnp.cos(ang_c), jnp.cos(ang_c)], axis=-1)
    sin = jnp.concatenate([-jnp.sin(ang_r), jnp.sin(ang_r), -jnp.sin(ang_c), jnp.sin(ang_c)], axis=-1)
    cos = jnp.concatenate([jnp.ones((nc, HEAD_DIM), F32), cos], axis=0)
    sin = jnp.concatenate([jnp.zeros((nc, HEAD_DIM), F32), sin], axis=0)
    return cos, sin


def _qk_fwd(p, cos, sin, qg, kg, name):
    T = p.shape[0]

    def body(p_ref, c_ref, s_ref, qg_ref, kg_ref, q_ref, k_ref, v_ref):
        c, s = c_ref[...], s_ref[...]
        for h in range(N_HEADS + N_KV):
            x = p_ref[:, h * HEAD_DIM:(h + 1) * HEAD_DIM]
            rs = lax.rsqrt(jnp.mean(x * x, axis=-1, keepdims=True) + RMS_EPS)
            gain = qg_ref[...] if h < N_HEADS else kg_ref[...]
            yv = x * rs * gain
            out = (yv * c + _swap_halves(yv) * s).astype(BF16)
            if h < N_HEADS:
                q_ref[:, h * HEAD_DIM:(h + 1) * HEAD_DIM] = out
            else:
                k_ref[:, (h - N_HEADS) * HEAD_DIM:(h - N_HEADS + 1) * HEAD_DIM] = out
        v_ref[...] = p_ref[:, Q_W + KV_W:QKV_W].astype(BF16)

    return pl.pallas_call(
        body, grid=(T // ROW_TM,),
        in_specs=[_row_spec(QKV_W), _row_spec(HEAD_DIM), _row_spec(HEAD_DIM), _vec_spec(HEAD_DIM), _vec_spec(HEAD_DIM)],
        out_specs=[_row_spec(Q_W), _row_spec(KV_W), _row_spec(KV_W)],
        out_shape=[jax.ShapeDtypeStruct((T, Q_W), BF16), jax.ShapeDtypeStruct((T, KV_W), BF16),
                   jax.ShapeDtypeStruct((T, KV_W), BF16)],
        name=name, compiler_params=_cp(("parallel",)),
    )(p, cos, sin, qg, kg)


def _qk_bwd(p, dq, dk, dv, cos, sin, qg, kg, name):
    T = p.shape[0]
    nblk = T // ROW_TM

    def body(p_ref, dq_ref, dk_ref, dv_ref, c_ref, s_ref, qg_ref, kg_ref, dp_ref, part_ref):
        c, s = c_ref[...], s_ref[...]
        dgq = jnp.zeros((1, HEAD_DIM), F32)
        dgk = jnp.zeros((1, HEAD_DIM), F32)
        for h in range(N_HEADS + N_KV):
            x = p_ref[:, h * HEAD_DIM:(h + 1) * HEAD_DIM]
            if h < N_HEADS:
                d = dq_ref[:, h * HEAD_DIM:(h + 1) * HEAD_DIM]
                gain = qg_ref[...]
            else:
                d = dk_ref[:, (h - N_HEADS) * HEAD_DIM:(h - N_HEADS + 1) * HEAD_DIM]
                gain = kg_ref[...]
            dyv = d * c + _swap_halves(d * s)
            rs = lax.rsqrt(jnp.mean(x * x, axis=-1, keepdims=True) + RMS_EPS)
            xn = x * rs
            dgsum = jnp.sum(dyv * xn, axis=0, keepdims=True)
            if h < N_HEADS:
                dgq = dgq + dgsum
            else:
                dgk = dgk + dgsum
            dxg = dyv * gain
            dx = rs * (dxg - xn * jnp.mean(dxg * xn, axis=-1, keepdims=True))
            dp_ref[:, h * HEAD_DIM:(h + 1) * HEAD_DIM] = dx.astype(BF16)
        dp_ref[:, Q_W + KV_W:QKV_W] = dv_ref[...].astype(BF16)
        part_ref[pl.ds(0, 1), :] = dgq
        part_ref[pl.ds(1, 1), :] = dgk
        part_ref[pl.ds(2, SUBLANE - 2), :] = jnp.zeros((SUBLANE - 2, HEAD_DIM), F32)

    return pl.pallas_call(
        body, grid=(nblk,),
        in_specs=[_row_spec(QKV_W), _row_spec(Q_W), _row_spec(KV_W), _row_spec(KV_W),
                  _row_spec(HEAD_DIM), _row_spec(HEAD_DIM), _vec_spec(HEAD_DIM), _vec_spec(HEAD_DIM)],
        out_specs=[_row_spec(QKV_W), _part_spec(HEAD_DIM)],
        out_shape=[jax.ShapeDtypeStruct((T, QKV_W), BF16), jax.ShapeDtypeStruct((nblk, SUBLANE, HEAD_DIM), F32)],
        name=name, compiler_params=_cp(("parallel",)),
    )(p, dq, dk, dv, cos, sin, qg, kg)


def _attn_probs(q, k, qblk, nc, T):
    s = lax.dot_general(q, k, (((1,), (1,)), ((), ())), preferred_element_type=F32) * (HEAD_DIM ** -0.5)
    col = lax.broadcasted_iota(jnp.int32, s.shape, 1)
    limit = jnp.where(qblk < nc // Q_BLOCK, nc, T)
    s = jnp.where(col < limit, s, -1e30)
    e = jnp.exp(s - jnp.max(s, axis=-1, keepdims=True))
    return e / jnp.sum(e, axis=-1, keepdims=True)


def _attn_fwd(q, k, v, nc, name):
    T = q.shape[0]

    def body(q_ref, k_ref, v_ref, o_ref):
        i = pl.program_id(0)
        for h in range(N_HEADS):
            g = h // KV_GROUP
            kk = k_ref[:, g * HEAD_DIM:(g + 1) * HEAD_DIM]
            vv = v_ref[:, g * HEAD_DIM:(g + 1) * HEAD_DIM]
            pr = _attn_probs(q_ref[:, h * HEAD_DIM:(h + 1) * HEAD_DIM], kk, i, nc, T)
            o = jnp.dot(pr.astype(BF16), vv, preferred_element_type=F32)
            o_ref[:, h * HEAD_DIM:(h + 1) * HEAD_DIM] = o.astype(BF16)

    whole = pl.BlockSpec((T, KV_W), lambda i: (0, 0))
    qs = pl.BlockSpec((Q_BLOCK, Q_W), lambda i: (i, 0))
    return pl.pallas_call(
        body, grid=(T // Q_BLOCK,), in_specs=[qs, whole, whole], out_specs=qs,
        out_shape=jax.ShapeDtypeStruct((T, Q_W), BF16),
        name=name, compiler_params=_cp(("parallel",)),
    )(q, k, v)


def _attn_bwd(q, k, v, do, nc, name):
    T = q.shape[0]
    scale = HEAD_DIM ** -0.5

    def body(q_ref, k_ref, v_ref, do_ref, dq_ref, dk_ref, dv_ref):
        i = pl.program_id(0)

        @pl.when(i == 0)
        def _():
            dk_ref[...] = jnp.zeros_like(dk_ref)
            dv_ref[...] = jnp.zeros_like(dv_ref)

        for h in range(N_HEADS):
            g = h // KV_GROUP
            cols = slice(g * HEAD_DIM, (g + 1) * HEAD_DIM)
            hc = slice(h * HEAD_DIM, (h + 1) * HEAD_DIM)
            qh, kk, vv, doh = q_ref[:, hc], k_ref[:, cols], v_ref[:, cols], do_ref[:, hc]
            pr = _attn_probs(qh, kk, i, nc, T)
            dpr = lax.dot_general(doh, vv, (((1,), (1,)), ((), ())), preferred_element_type=F32)
            ds = (pr * (dpr - jnp.sum(pr * dpr, axis=-1, keepdims=True)) * scale).astype(BF16)
            dq_ref[:, hc] = jnp.dot(ds, kk, preferred_element_type=F32)
            dk_ref[:, cols] += lax.dot_general(ds, qh, (((0,), (0,)), ((), ())), preferred_element_type=F32)
            dv_ref[:, cols] += lax.dot_general(pr.astype(BF16), doh, (((0,), (0,)), ((), ())),
                                               preferred_element_type=F32)

    whole = pl.BlockSpec((T, KV_W), lambda i: (0, 0))
    qs = pl.BlockSpec((Q_BLOCK, Q_W), lambda i: (i, 0))
    return pl.pallas_call(
        body, grid=(T // Q_BLOCK,), in_specs=[qs, whole, whole, qs], out_specs=[qs, whole, whole],
        out_shape=[jax.ShapeDtypeStruct((T, Q_W), F32), jax.ShapeDtypeStruct((T, KV_W), F32),
                   jax.ShapeDtypeStruct((T, KV_W), F32)],
        name=name, compiler_params=_cp(("arbitrary",)),
    )(q, k, v, do)


def _shift_rows(z, o, nc):
    T = z.shape[0]
    t = lax.broadcasted_iota(jnp.int32, (T, 1), 0)
    lo = jnp.where(t < nc, 0, nc)
    hi = jnp.where(t < nc, nc, T)
    ok = jnp.logical_and(t + o >= lo, t + o < hi)
    rolled = z if o == 0 else pltpu.roll(z, (-o) % T, 0)
    return jnp.where(ok, rolled, 0.0), ok


def _pool_window(w):
    left = w // 2
    return -left, w - 1 - left


def _pool_d(z, w, nc):
    o0, o1 = _pool_window(w)
    tot = jnp.zeros_like(z)
    cnt = jnp.zeros((z.shape[0], 1), F32)
    for o in range(o0, o1 + 1):
        sh, ok = _shift_rows(z, o, nc)
        tot = tot + sh
        cnt = cnt + ok.astype(F32)
    return tot / cnt - z, cnt


def _pool_fwd(p, pool_w, pool_scale, nc, name):
    T = p.shape[0]

    def body(z_ref, w_ref, s_ref, o_ref):
        for g, w in enumerate(POOL_WINDOWS):
            cs = slice(g * GC, (g + 1) * GC)
            d, _ = _pool_d(z_ref[:, cs], w, nc)
            yv = jnp.dot(d.astype(BF16), w_ref[g].astype(BF16), preferred_element_type=F32)
            o_ref[:, cs] = (yv * s_ref[:, cs]).astype(BF16)

    return pl.pallas_call(
        body, grid=(1,),
        in_specs=[pl.BlockSpec((T, BR_W), lambda i: (0, OFF_POOL // BR_W)),
                  pl.BlockSpec((N_GROUP, GC, GC), lambda i: (0, 0, 0)), _vec_spec(BR_W)],
        out_specs=pl.BlockSpec((T, BR_W), lambda i: (0, 0)),
        out_shape=jax.ShapeDtypeStruct((T, BR_W), BF16),
        name=name, compiler_params=_cp(("arbitrary",), VMEM_BIG),
    )(p, pool_w, pool_scale)


def _pool_bwd(p, dy, pool_w, pool_scale, nc, name):
    T = p.shape[0]

    def body(z_ref, dy_ref, w_ref, s_ref, dz_ref, dw_ref, ds_ref):
        for g, w in enumerate(POOL_WINDOWS):
            cs = slice(g * GC, (g + 1) * GC)
            d, cnt = _pool_d(z_ref[:, cs], w, nc)
            db = d.astype(BF16)
            wb = w_ref[g].astype(BF16)
            dyv = dy_ref[:, cs]
            e = (dyv * s_ref[:, cs]).astype(BF16)
            dw_ref[g] = lax.dot_general(db, e, (((0,), (0,)), ((), ())), preferred_element_type=F32)
            yraw = jnp.dot(db, wb, preferred_element_type=F32)
            ds_ref[:, cs] = jnp.sum(dyv * yraw, axis=0, keepdims=True)
            dd = lax.dot_general(e, wb, (((1,), (1,)), ((), ())), preferred_element_type=F32)
            ec = dd / cnt
            o0, o1 = _pool_window(w)
            tot = jnp.zeros_like(dd)
            for o in range(o0, o1 + 1):
                tot = tot + _shift_rows(ec, -o, nc)[0]
            dz_ref[:, cs] = (tot - dd).astype(BF16)

    return pl.pallas_call(
        body, grid=(1,),
        in_specs=[pl.BlockSpec((T, BR_W), lambda i: (0, OFF_POOL // BR_W)),
                  pl.BlockSpec((T, BR_W), lambda i: (0, 0)),
                  pl.BlockSpec((N_GROUP, GC, GC), lambda i: (0, 0, 0)), _vec_spec(BR_W)],
        out_specs=[pl.BlockSpec((T, BR_W), lambda i: (0, 0)),
                   pl.BlockSpec((N_GROUP, GC, GC), lambda i: (0, 0, 0)), _vec_spec(BR_W)],
        out_shape=[jax.ShapeDtypeStruct((T, BR_W), BF16), jax.ShapeDtypeStruct((N_GROUP, GC, GC), F32),
                   jax.ShapeDtypeStruct((1, BR_W), F32)],
        name=name, compiler_params=_cp(("arbitrary",), VMEM_BIG),
    )(p, dy, pool_w, pool_scale)


_GELU_C = math.sqrt(2.0 / math.pi)


def _gelu(x):
    return 0.5 * x * (1.0 + jnp.tanh(_GELU_C * (x + 0.044715 * x * x * x)))


def _gelu_grad(x):
    th = jnp.tanh(_GELU_C * (x + 0.044715 * x * x * x))
    return 0.5 * (1.0 + th) + 0.5 * x * (1.0 - th * th) * _GELU_C * (1.0 + 3.0 * 0.044715 * x * x)


def _sgu_fwd(p, ln_g, ln_b, w_s, b_st, name):
    T = p.shape[0]

    def body(pu_ref, pv_ref, g_ref, b_ref, w_ref, bs_ref, o_ref):
        u = _gelu(pu_ref[...])
        vhat, _ = _ln_stats(_gelu(pv_ref[...]))
        vn = (vhat * g_ref[...] + b_ref[...]).astype(BF16)
        for g in range(N_GROUP):
            cs = slice(g * GC, (g + 1) * GC)
            s = jnp.dot(w_ref[g].astype(BF16), vn[:, cs], preferred_element_type=F32) + bs_ref[:, g:g + 1]
            o_ref[:, cs] = (u[:, cs] * s).astype(BF16)

    chunk = lambda off: pl.BlockSpec((GC, BR_W), lambda i: (i, off // BR_W))
    return pl.pallas_call(
        body, grid=(T // GC,),
        in_specs=[chunk(OFF_U), chunk(OFF_VG), _vec_spec(BR_W), _vec_spec(BR_W),
                  pl.BlockSpec((N_GROUP, GC, GC), lambda i: (0, 0, 0)),
                  pl.BlockSpec((GC, N_GROUP), lambda i: (0, 0))],
        out_specs=pl.BlockSpec((GC, BR_W), lambda i: (i, 0)),
        out_shape=jax.ShapeDtypeStruct((T, BR_W), BF16),
        name=name, compiler_params=_cp(("parallel",)),
    )(p, p, ln_g, ln_b, w_s, b_st)


def _sgu_bwd(p, dy, ln_g, ln_b, w_s, b_st, name):
    T = p.shape[0]

    def body(pu_ref, pv_ref, dy_ref, g_ref, b_ref, w_ref, bs_ref, dp_ref, dw_ref, dsacc_ref, dln_ref):
        i = pl.program_id(0)

        @pl.when(i == 0)
        def _():
            dw_ref[...] = jnp.zeros_like(dw_ref)
            dsacc_ref[...] = jnp.zeros_like(dsacc_ref)
            dln_ref[...] = jnp.zeros_like(dln_ref)

        pu, pv, dyv = pu_ref[...], pv_ref[...], dy_ref[...]
        u = _gelu(pu)
        vhat, rstd = _ln_stats(_gelu(pv))
        vn = (vhat * g_ref[...] + b_ref[...]).astype(BF16)
        ds = dyv * u
        dsb = ds.astype(BF16)
        dsacc_ref[...] += ds
        dvn_parts = []
        for g in range(N_GROUP):
            cs = slice(g * GC, (g + 1) * GC)
            wb = w_ref[g].astype(BF16)
            s = jnp.dot(wb, vn[:, cs], preferred_element_type=F32) + bs_ref[:, g:g + 1]
            dp_ref[:, cs] = (dyv[:, cs] * s * _gelu_grad(pu[:, cs])).astype(BF16)
            dvn_parts.append(lax.dot_general(wb, dsb[:, cs], (((0,), (0,)), ((), ())),
                                             preferred_element_type=F32))
            dw_ref[g] += lax.dot_general(dsb[:, cs], vn[:, cs], (((1,), (1,)), ((), ())),
                                         preferred_element_type=F32)
        dvn = jnp.concatenate(dvn_parts, axis=-1)
        dln_ref[pl.ds(0, 1), :] += jnp.sum(dvn * vhat, axis=0, keepdims=True)
        dln_ref[pl.ds(1, 1), :] += jnp.sum(dvn, axis=0, keepdims=True)
        dvhat = dvn * g_ref[...]
        m1 = jnp.mean(dvhat, axis=-1, keepdims=True)
        m2 = jnp.mean(dvhat * vhat, axis=-1, keepdims=True)
        dv = rstd * (dvhat - m1 - vhat * m2)
        dp_ref[:, BR_W:2 * BR_W] = (dv * _gelu_grad(pv)).astype(BF16)

    chunk = lambda off: pl.BlockSpec((GC, BR_W), lambda i: (i, off // BR_W))
    return pl.pallas_call(
        body, grid=(T // GC,),
        in_specs=[chunk(OFF_U), chunk(OFF_VG), pl.BlockSpec((GC, BR_W), lambda i: (i, 0)),
                  _vec_spec(BR_W), _vec_spec(BR_W),
                  pl.BlockSpec((N_GROUP, GC, GC), lambda i: (0, 0, 0)),
                  pl.BlockSpec((GC, N_GROUP), lambda i: (0, 0))],
        out_specs=[pl.BlockSpec((GC, 2 * BR_W), lambda i: (i, 0)),
                   pl.BlockSpec((N_GROUP, GC, GC), lambda i: (0, 0, 0)),
                   pl.BlockSpec((GC, BR_W), lambda i: (0, 0)),
                   pl.BlockSpec((SUBLANE, BR_W), lambda i: (0, 0))],
        out_shape=[jax.ShapeDtypeStruct((T, 2 * BR_W), BF16), jax.ShapeDtypeStruct((N_GROUP, GC, GC), F32),
                   jax.ShapeDtypeStruct((GC, BR_W), F32), jax.ShapeDtypeStruct((SUBLANE, BR_W), F32)],
        name=name, compiler_params=_cp(("arbitrary",)),
    )(p, p, dy, ln_g, ln_b, w_s, b_st)


def _conv_fwd(p, conv_w, nc, name):
    T = p.shape[0]

    def body(cb_ref, cc_ref, cx_ref, w_ref, o_ref):
        zz = cc_ref[...] * cx_ref[...]
        conv = (w_ref[0:1, :] * _shift_rows(zz, -1, nc)[0] + w_ref[1:2, :] * zz
                + w_ref[2:3, :] * _shift_rows(zz, 1, nc)[0])
        o_ref[...] = (cb_ref[...] * conv).astype(BF16)

    col = lambda off: pl.BlockSpec((T, GC), lambda j: (0, off // GC + j))
    return pl.pallas_call(
        body, grid=(N_GROUP,),
        in_specs=[col(OFF_CB), col(OFF_CC), col(OFF_CX), pl.BlockSpec((3, GC), lambda j: (0, j))],
        out_specs=pl.BlockSpec((T, GC), lambda j: (0, j)),
        out_shape=jax.ShapeDtypeStruct((T, BR_W), BF16),
        name=name, compiler_params=_cp(("parallel",)),
    )(p, p, p, conv_w)


def _conv_bwd(p, dy, conv_w, nc, name):
    T = p.shape[0]

    def body(cb_ref, cc_ref, cx_ref, dy_ref, w_ref, dp_ref, dw_ref):
        cc, cx, dyv = cc_ref[...], cx_ref[...], dy_ref[...]
        zz = cc * cx
        zm, zp = _shift_rows(zz, -1, nc)[0], _shift_rows(zz, 1, nc)[0]
        conv = w_ref[0:1, :] * zm + w_ref[1:2, :] * zz + w_ref[2:3, :] * zp
        dconv = dyv * cb_ref[...]
        dw_ref[0:1, :] = jnp.sum(dconv * zm, axis=0, keepdims=True)
        dw_ref[1:2, :] = jnp.sum(dconv * zz, axis=0, keepdims=True)
        dw_ref[2:3, :] = jnp.sum(dconv * zp, axis=0, keepdims=True)
        dzz = (w_ref[0:1, :] * _shift_rows(dconv, 1, nc)[0] + w_ref[1:2, :] * dconv
               + w_ref[2:3, :] * _shift_rows(dconv, -1, nc)[0])
        dp_ref[0] = (dyv * conv).astype(BF16)
        dp_ref[1] = (dzz * cx).astype(BF16)
        dp_ref[2] = (dzz * cc).astype(BF16)

    col = lambda off: pl.BlockSpec((T, GC), lambda j: (0, off // GC + j))
    return pl.pallas_call(
        body, grid=(N_GROUP,),
        in_specs=[col(OFF_CB), col(OFF_CC), col(OFF_CX), pl.BlockSpec((T, GC), lambda j: (0, j)),
                  pl.BlockSpec((3, GC), lambda j: (0, j))],
        out_specs=[pl.BlockSpec((3, T, GC), lambda j: (0, 0, j)), pl.BlockSpec((3, GC), lambda j: (0, j))],
        out_shape=[jax.ShapeDtypeStruct((3, T, BR_W), BF16), jax.ShapeDtypeStruct((3, BR_W), F32)],
        name=name, compiler_params=_cp(("parallel",)),
    )(p, p, p, dy, conv_w)


def _rows_tile(rows, cols, n_arrays):
    budget = 24 * 1024 * 1024 // (2 * 4 * n_arrays * cols)
    return _tile(rows, max(SUBLANE * 2, min(budget, 1024)), SUBLANE * 2) if rows % (SUBLANE * 2) == 0 else rows


def _cast_into(chip, w, l, name):
    _, K, cols = w.shape
    tr = _rows_tile(K, cols, 2)

    def body(q_ref, w_ref, o_ref):
        o_ref[...] = w_ref[...].astype(BF16)

    return pl.pallas_call(
        body,
        grid_spec=pltpu.PrefetchScalarGridSpec(
            num_scalar_prefetch=1, grid=(K // tr,),
            in_specs=[pl.BlockSpec((None, tr, cols), lambda i, q: (l, i, 0))],
            out_specs=pl.BlockSpec((None, tr, cols), lambda i, q: (q[0], i, 0))),
        out_shape=jax.ShapeDtypeStruct((N_CHIP, K, cols), BF16),
        name=name, compiler_params=_cp(("parallel",)),
    )(chip, w)


def _adamw(w, g, m, v, name):
    shape = w.shape
    cols = shape[-1]
    rows = w.size // cols
    tr = _rows_tile(rows, cols, 7)
    c1 = 1.0 - ADAM_B1 ** ADAM_STEP
    c2 = 1.0 - ADAM_B2 ** ADAM_STEP

    def body(w_ref, g_ref, m_ref, v_ref, d_ref, mo_ref, vo_ref):
        gv = g_ref[...]
        mn = ADAM_B1 * m_ref[...] + (1.0 - ADAM_B1) * gv
        vn = ADAM_B2 * v_ref[...] + (1.0 - ADAM_B2) * (gv * gv)
        mo_ref[...] = mn
        vo_ref[...] = vn
        d_ref[...] = -ADAM_LR * ((mn / c1) / (jnp.sqrt(vn / c2) + ADAM_EPS) + ADAM_WD * w_ref[...])

    rs = pl.BlockSpec((tr, cols), lambda i: (i, 0))
    outs = pl.pallas_call(
        body, grid=(rows // tr,), in_specs=[rs] * 4, out_specs=[rs] * 3,
        out_shape=[jax.ShapeDtypeStruct((rows, cols), F32)] * 3,
        name=name, compiler_params=_cp(("parallel",)),
    )(*[a.reshape(rows, cols) for a in (w, g, m, v)])
    return [o.reshape(shape) for o in outs]


def _adamw_layer(w, g, m, v, l, prev, after, name):
    L, K, cols = w.shape
    tr = _rows_tile(K, cols, 7)
    c1 = 1.0 - ADAM_B1 ** ADAM_STEP
    c2 = 1.0 - ADAM_B2 ** ADAM_STEP
    nprev = 0 if prev is None else 3

    def body(*refs):
        w_ref, g_ref, m_ref, v_ref = refs[:4]
        d_ref, mo_ref, vo_ref = refs[-3:]
        gv = g_ref[...]
        mn = ADAM_B1 * m_ref[...] + (1.0 - ADAM_B1) * gv
        vn = ADAM_B2 * v_ref[...] + (1.0 - ADAM_B2) * (gv * gv)
        mo_ref[...] = mn
        vo_ref[...] = vn
        d_ref[...] = -ADAM_LR * ((mn / c1) / (jnp.sqrt(vn / c2) + ADAM_EPS) + ADAM_WD * w_ref[...])

    rs = pl.BlockSpec((None, tr, cols), lambda i: (l, i, 0))
    extra = list(prev or ()) + list(after)
    return pl.pallas_call(
        body, grid=(K // tr,), in_specs=[rs] * 4 + [_ANY] * len(extra), out_specs=[rs] * 3,
        out_shape=[jax.ShapeDtypeStruct((L, K, cols), F32)] * 3,
        input_output_aliases={4 + k: k for k in range(nprev)},
        name=name, compiler_params=_cp(("parallel",)),
    )(w, g, m, v, *extra)


def _pair_sum(core, dw, recv, name):
    nq, half, cols = recv.shape
    tr = _rows_tile(half, cols, 3)
    nb = half // tr

    def body(c_ref, a_ref, r_ref, o_ref):
        o_ref[...] = (a_ref[...].astype(F32) + r_ref[...].astype(F32)).astype(BF16)

    rs = pl.BlockSpec((None, tr, cols), lambda q, i, c: (q, i, 0))
    return pl.pallas_call(
        body,
        grid_spec=pltpu.PrefetchScalarGridSpec(
            num_scalar_prefetch=1, grid=(nq, nb),
            in_specs=[pl.BlockSpec((None, tr, cols), lambda q, i, c: (q, c[0] * nb + i, 0)), rs],
            out_specs=rs),
        out_shape=jax.ShapeDtypeStruct((nq, half, cols), BF16),
        name=name, compiler_params=_cp(("parallel", "parallel")),
    )(core, dw, recv)


def _chip_sum(chip, core, psum, land, l, n_layers, prev, name):
    _, half, cols = psum.shape
    tr = _rows_tile(half, cols, 5)
    nb = half // tr

    def body(*refs):
        p_ref, r_ref, o_ref = refs[2], refs[3], refs[-1]
        o_ref[...] = (p_ref[...].astype(F32) + r_ref[0].astype(F32) + r_ref[1].astype(F32)
                      + r_ref[2].astype(F32))

    extra = [] if prev is None else [prev]
    return pl.pallas_call(
        body,
        grid_spec=pltpu.PrefetchScalarGridSpec(
            num_scalar_prefetch=2, grid=(nb,),
            in_specs=[pl.BlockSpec((None, tr, cols), lambda i, q, c: (q[0], i, 0)),
                      pl.BlockSpec((3, tr, cols), lambda i, q, c: (0, i, 0))] + [_ANY] * len(extra),
            out_specs=pl.BlockSpec((None, tr, cols), lambda i, q, c: (l, c[0] * nb + i, 0))),
        out_shape=jax.ShapeDtypeStruct((n_layers, 2 * half, cols), F32),
        input_output_aliases={4: 0} if prev is not None else {},
        name=name, compiler_params=_cp(("parallel",)),
    )(chip, core, psum, land, *extra)


def _slab_sum(gathered, name):
    n, rows, cols = gathered.shape
    tr = _tile(rows, 512, SUBLANE)

    def body(g_ref, o_ref):
        acc = g_ref[0]
        for d in range(1, n):
            acc = acc + g_ref[d]
        o_ref[...] = acc

    return pl.pallas_call(
        body, grid=(rows // tr,),
        in_specs=[pl.BlockSpec((n, tr, cols), lambda i: (0, i, 0))],
        out_specs=pl.BlockSpec((tr, cols), lambda i: (i, 0)),
        out_shape=jax.ShapeDtypeStruct((rows, cols), F32),
        name=name, compiler_params=_cp(("parallel",)),
    )(gathered)


def _silu(x):
    return x * jax.nn.sigmoid(x)


def _ada_fwd(cvec, w_ada, b_cols, name):
    L, D, ns = w_ada.shape
    tn = _tile(ns, 768)

    def body(c_ref, w_ref, b_ref, o_ref):
        s = _silu(c_ref[...]).astype(BF16)
        o_ref[...] = jnp.dot(s, w_ref[...].astype(BF16), preferred_element_type=F32) + b_ref[...]

    return pl.pallas_call(
        body, grid=(L, ns // tn),
        in_specs=[pl.BlockSpec((16, D), lambda l, j: (0, 0)),
                  pl.BlockSpec((None, D, tn), lambda l, j: (l, 0, j)),
                  pl.BlockSpec((None, 1, tn), lambda l, j: (l, 0, j))],
        out_specs=pl.BlockSpec((None, 16, tn), lambda l, j: (l, 0, j)),
        out_shape=jax.ShapeDtypeStruct((L, 16, ns), F32),
        name=name, compiler_params=_cp(("parallel", "parallel")),
    )(cvec, w_ada, b_cols)


def _ada_bwd(cvec, dmod, w_ada, name):
    L, D, ns = w_ada.shape
    tn = _tile(ns, 768)

    def body(c_ref, d_ref, w_ref, gw_ref, ds_ref):
        first = jnp.logical_and(pl.program_id(0) == 0, pl.program_id(1) == 0)

        @pl.when(first)
        def _():
            ds_ref[...] = jnp.zeros_like(ds_ref)

        s = _silu(c_ref[...]).astype(BF16)
        db = d_ref[...].astype(BF16)
        gw_ref[...] = lax.dot_general(s, db, (((0,), (0,)), ((), ())), preferred_element_type=F32)
        ds_ref[...] += lax.dot_general(db, w_ref[...].astype(BF16), (((1,), (1,)), ((), ())),
                                       preferred_element_type=F32)

    return pl.pallas_call(
        body, grid=(L, ns // tn),
        in_specs=[pl.BlockSpec((16, D), lambda l, j: (0, 0)),
                  pl.BlockSpec((None, 16, tn), lambda l, j: (l, 0, j)),
                  pl.BlockSpec((None, D, tn), lambda l, j: (l, 0, j))],
        out_specs=[pl.BlockSpec((None, D, tn), lambda l, j: (l, 0, j)),
                   pl.BlockSpec((16, D), lambda l, j: (0, 0))],
        out_shape=[jax.ShapeDtypeStruct((L, D, ns), F32), jax.ShapeDtypeStruct((16, D), F32)],
        name=name, compiler_params=_cp(("arbitrary", "arbitrary")),
    )(cvec, dmod, w_ada)


def _cctx_grad(gathered, c_ctx, name):
    rows = c_ctx.shape[0]

    def body(g_ref, c_ref, o_ref):
        ds = g_ref[0] + g_ref[2] + g_ref[4] + g_ref[6]
        cv = c_ref[...]
        sig = jax.nn.sigmoid(cv)
        o_ref[...] = ds * sig * (1.0 + cv * (1.0 - sig))

    return pl.pallas_call(
        body, out_shape=jax.ShapeDtypeStruct((rows, LANE), F32), name=name,
    )(gathered, c_ctx)


def _place():
    x, y, c = lax.axis_index("x"), lax.axis_index("y"), lax.axis_index("c")
    chips = [(1 - x, y), (x, 1 - y), (1 - x, 1 - y)]
    return x, y, c, chips


def _all_gather_small(slab, name):
    rows, cols = slab.shape

    def body(x_ref, out_ref, send_sems, recv_sems, local_sem):
        x, y, c, chips = _place()
        me, sibling = (x, y, c), (x, y, 1 - c)

        def blk(px, py, pc):
            return out_ref.at[4 * px + 2 * py + pc]

        def copy(k, block, to, src=None):
            return pltpu.make_async_remote_copy(
                src_ref=blk(*block) if src is None else src, dst_ref=blk(*block),
                send_sem=send_sems.at[k], recv_sem=recv_sems.at[k], device_id=to, device_id_type=MESH)

        mine = pltpu.make_async_copy(x_ref, blk(*me), local_sem)
        mine.start()
        first = [copy(0, me, sibling, src=x_ref)]
        first += [copy(1 + j, me, (*chip, c), src=x_ref) for j, chip in enumerate(chips)]
        for cp in first:
            cp.start()
        passed = [copy(4 + j, (*chip, c), sibling) for j, chip in enumerate(chips)]
        for j, chip in enumerate(chips):
            copy(1 + j, (*chip, c), me).wait_recv()
            passed[j].start()
        copy(0, sibling, me).wait_recv()
        for j, chip in enumerate(chips):
            copy(4 + j, (*chip, 1 - c), me).wait_recv()
        for cp in first + passed:
            cp.wait_send()
        mine.wait()

    return pl.pallas_call(
        body, out_shape=jax.ShapeDtypeStruct((N_DEV, rows, cols), slab.dtype),
        in_specs=[pl.BlockSpec(memory_space=pltpu.VMEM)],
        out_specs=pl.BlockSpec(memory_space=pltpu.VMEM),
        scratch_shapes=[pltpu.SemaphoreType.DMA((7,)), pltpu.SemaphoreType.DMA((7,)), pltpu.SemaphoreType.DMA],
        name=name, compiler_params=pltpu.CompilerParams(vmem_limit_bytes=VMEM_MID),
    )(slab)


def _hbm(a):
    return pltpu.with_memory_space_constraint(a, pltpu.HBM)


def _half_rows(ref, q, c):
    half = ref.shape[1] // 2
    return ref.at[q, pl.ds(c * half, half)]


def _ici_copy(src, dst, send_sems, recv_sems, k, to):
    return pltpu.make_async_remote_copy(src_ref=src, dst_ref=dst, send_sem=send_sems.at[k], recv_sem=recv_sems.at[k],
                                        device_id=to, device_id_type=MESH)


def _gather_start(bufs, sizes, name):
    n = len(bufs)
    ng = len(sizes)

    def body(*refs):
        ins = refs[:n]
        sems = refs[n:n + 2 * ng]
        token = refs[-1]
        x, y, c, chips = _place()
        a = 0
        for g, sz in enumerate(sizes):
            for i in range(sz):
                part = _half_rows(ins[a], 2 * x + y, c)
                for j, chip in enumerate(chips):
                    _ici_copy(part, part, sems[2 * g], sems[2 * g + 1], 3 * i + j, (*chip, c)).start()
                a += 1
        token[...] = jnp.zeros_like(token)

    sem_shapes = []
    for sz in sizes:
        sem_shapes += [pltpu.SemaphoreType.DMA((3 * sz,)), pltpu.SemaphoreType.DMA((3 * sz,))]
    outs = pl.pallas_call(
        body, name=name,
        out_shape=tuple(sem_shapes) + tuple(pltpu.HBM(b.shape, b.dtype) for b in bufs)
                  + (jax.ShapeDtypeStruct((SUBLANE, LANE), F32),),
        in_specs=[_HBM] * n,
        out_specs=tuple([_SEM] * (2 * ng) + [_HBM] * n + [pl.BlockSpec(memory_space=pltpu.VMEM)]),
        input_output_aliases={i: 2 * ng + i for i in range(n)},
        compiler_params=pltpu.CompilerParams(has_side_effects=_EFFECT),
    )(*[_hbm(b) for b in bufs])
    sems = [(outs[2 * g], outs[2 * g + 1]) for g in range(ng)]
    return sems, list(outs[2 * ng:2 * ng + n]), outs[-1]


def _gather_wait(bufs, sems, after, name):
    n = len(bufs)
    na = len(after)

    def body(*refs):
        ins = refs[:n]
        send_sems, recv_sems = refs[n], refs[n + 1]
        x, y, c, chips = _place()
        for i in range(n):
            sent = _half_rows(ins[i], 2 * x + y, c)
            for j, (cx, cy) in enumerate(chips):
                cp = _ici_copy(sent, _half_rows(ins[i], 2 * cx + cy, c), send_sems, recv_sems, 3 * i + j, (cx, cy, c))
                cp.wait_send()
                cp.wait_recv()

    outs = pl.pallas_call(
        body, name=name,
        out_shape=tuple(pltpu.HBM(b.shape, b.dtype) for b in bufs),
        in_specs=[_HBM] * n + [_SEM, _SEM] + [_ANY] * na,
        out_specs=tuple([_HBM] * n),
        input_output_aliases={i: i for i in range(n)},
        compiler_params=pltpu.CompilerParams(has_side_effects=_EFFECT),
    )(*bufs, sems[0], sems[1], *after)
    return list(outs)


def _exchange_halves(bufs, name):
    n = len(bufs)

    def body(*refs):
        ins = refs[:n]
        send_sems, recv_sems = refs[-2:]
        x, y, c, chips = _place()
        sends = []
        for i in range(n):
            for j, (cx, cy) in enumerate(chips):
                part = _half_rows(ins[i], 2 * cx + cy, c)
                cp = _ici_copy(part, part, send_sems, recv_sems, 3 * i + j, (x, y, 1 - c))
                cp.start()
                sends.append(cp)
        for i in range(n):
            for j, (cx, cy) in enumerate(chips):
                part = _half_rows(ins[i], 2 * cx + cy, 1 - c)
                _ici_copy(part, part, send_sems, recv_sems, 3 * i + j, (x, y, 1 - c)).wait_recv()
        for cp in sends:
            cp.wait_send()

    outs = pl.pallas_call(
        body, name=name,
        out_shape=[jax.ShapeDtypeStruct(b.shape, b.dtype) for b in bufs],
        in_specs=[_ANY] * n, out_specs=[_ANY] * n,
        input_output_aliases={i: i for i in range(n)},
        scratch_shapes=[pltpu.SemaphoreType.DMA((3 * n,)), pltpu.SemaphoreType.DMA((3 * n,))],
    )(*bufs)
    return list(outs)


def _other_half(ref, c):
    half = ref.shape[1] // 2
    return ref.at[:, pl.ds((1 - c) * half, half)]


def _prereduce_plan(src, land, x, y, c, chips):
    return [(_other_half(src, c), land, (x, y, 1 - c))]


def _prereduce_land(d):
    return (d.shape[0], d.shape[1] // 2, d.shape[2])


def _scatter_plan(src, land, x, y, c, chips):
    return [(src.at[2 * cx + cy], land.at[j], (cx, cy, c)) for j, (cx, cy) in enumerate(chips)]


def _scatter_land(p):
    return (3,) + p.shape[1:]


_COPIES_PER_SOURCE = {_prereduce_plan: 1, _scatter_plan: 3}


def _split_start(srcs, plan, land_shape, name):
    n = len(srcs)
    land_shapes = [land_shape(s) for s in srcs]
    per = _COPIES_PER_SOURCE[plan]

    def body(*refs):
        src_refs, lands = refs[:n], refs[n:2 * n]
        send_sems, recv_sems = refs[2 * n], refs[2 * n + 1]
        token = refs[-1]
        x, y, c, chips = _place()
        for i in range(n):
            for j, (s, d, to) in enumerate(plan(src_refs[i], lands[i], x, y, c, chips)):
                _ici_copy(s, d, send_sems, recv_sems, per * i + j, to).start()
        token[...] = jnp.zeros_like(token)

    outs = pl.pallas_call(
        body, name=name,
        out_shape=(pltpu.SemaphoreType.DMA((per * n,)), pltpu.SemaphoreType.DMA((per * n,)))
                  + tuple(pltpu.HBM(p.shape, p.dtype) for p in srcs)
                  + tuple(pltpu.HBM(s, BF16) for s in land_shapes)
                  + (jax.ShapeDtypeStruct((SUBLANE, LANE), F32),),
        in_specs=[_HBM] * (2 * n),
        out_specs=tuple([_SEM, _SEM] + [_HBM] * (2 * n) + [pl.BlockSpec(memory_space=pltpu.VMEM)]),
        input_output_aliases={i: 2 + i for i in range(2 * n)},
        compiler_params=pltpu.CompilerParams(has_side_effects=_EFFECT),
    )(*[_hbm(p) for p in srcs], *[_hbm(lax.empty(s, BF16)) for s in land_shapes])
    return (outs[0], outs[1]), list(outs[2:2 + n]), list(outs[2 + n:2 + 2 * n]), outs[-1]


def _split_wait(psums, lands, sems, plan, after, name):
    n = len(psums)
    na = len(after)
    per = _COPIES_PER_SOURCE[plan]

    def body(*refs):
        srcs, lnds = refs[:n], refs[n:2 * n]
        send_sems, recv_sems = refs[2 * n], refs[2 * n + 1]
        x, y, c, chips = _place()
        for i in range(n):
            for j, (s, d, to) in enumerate(plan(srcs[i], lnds[i], x, y, c, chips)):
                cp = _ici_copy(s, d, send_sems, recv_sems, per * i + j, to)
                cp.wait_send()
                cp.wait_recv()

    outs = pl.pallas_call(
        body, name=name,
        out_shape=tuple(pltpu.HBM(a.shape, a.dtype) for a in list(psums) + list(lands)),
        in_specs=[_HBM] * (2 * n) + [_SEM, _SEM] + [_ANY] * na,
        out_specs=tuple([_HBM] * (2 * n)),
        input_output_aliases={i: i for i in range(2 * n)},
        compiler_params=pltpu.CompilerParams(has_side_effects=_EFFECT),
    )(*psums, *lands, sems[0], sems[1], *after)
    return list(outs[:n]), list(outs[n:])


def _exchange_grads(grads, l, name):
    n = len(grads)

    def body(*refs):
        ins = refs[:n]
        send_sems, recv_sems = refs[-2:]
        x, y, c, _ = _place()

        def rows(ref, core):
            half = ref.shape[1] // 2
            return ref.at[l, pl.ds(core * half, half)]

        cps = []
        for i in range(n):
            cp = _ici_copy(rows(ins[i], c), rows(ins[i], c), send_sems, recv_sems, i, (x, y, 1 - c))
            cp.start()
            cps.append(cp)
        for i, cp in enumerate(cps):
            cp.wait_send()
            _ici_copy(rows(ins[i], 1 - c), rows(ins[i], 1 - c), send_sems, recv_sems, i, (x, y, 1 - c)).wait_recv()

    outs = pl.pallas_call(
        body, name=name,
        out_shape=[jax.ShapeDtypeStruct(g.shape, g.dtype) for g in grads],
        in_specs=[_ANY] * n, out_specs=[_ANY] * n,
        input_output_aliases={i: i for i in range(n)},
        scratch_shapes=[pltpu.SemaphoreType.DMA((n,)), pltpu.SemaphoreType.DMA((n,))],
    )(*grads)
    return list(outs)


def _pack(arrs):
    flat = jnp.concatenate([a.reshape(-1).astype(F32) for a in arrs])
    pad = (-flat.shape[0]) % (SUBLANE * LANE)
    return jnp.pad(flat, (0, pad)).reshape(-1, LANE)


def _unpack(slab, shapes):
    flat = slab.reshape(-1)
    out, off = [], 0
    for s in shapes:
        n = math.prod(s)
        out.append(flat[off:off + n].reshape(s))
        off += n
    return out


def _split6(v):
    d = v.shape[-1] // 6
    return [v[:, k * d:(k + 1) * d].reshape(2, 1, d) for k in range(6)]


BIG = ("w_in", "w_br_attn", "w_br_pool", "w_br_sgu", "w_br_conv", "w_gate", "w_o", "w_ff_gate", "w_ff_up", "w_ff_down")
KIND = {"w_in": "col", "w_br_attn": "col", "w_br_pool": "col", "w_br_sgu": "col", "w_br_conv": "col",
        "w_gate": "col", "w_o": "row", "w_ff_gate": "col", "w_ff_up": "col", "w_ff_down": "row"}
SMALL = ("c_ctx", "b_ada", "q_norm_g", "k_norm_g", "pool_w", "pool_scale", "sgu_ln_g", "sgu_ln_b", "sgu_w",
         "sgu_b", "conv_w", "b_gate", "ln1_g", "ln1_b", "ln2_g", "ln2_b")
WEIGHTS = ("c_ctx", "w_ada", "b_ada", "w_in", "q_norm_g", "k_norm_g", "pool_w", "pool_scale", "sgu_ln_g", "sgu_ln_b",
           "sgu_w", "sgu_b", "conv_w", "w_br_attn", "w_br_pool", "w_br_sgu", "w_br_conv", "w_gate", "b_gate", "w_o",
           "ln1_g", "ln1_b", "w_ff_gate", "w_ff_up", "w_ff_down", "ln2_g", "ln2_b")


def _step(x, c, ctx, loss_target, W, M, V):
    L = W["w_ada"].shape[0]
    assert L == 2, "core c of a chip carries layer c of the weight traffic"
    N, D = x.shape[1], x.shape[2]
    NC = ctx.shape[1]
    T = NC + N
    FF = W["w_ff_down"].shape[1] * N_CHIP
    assert NC % ROW_TM == 0 and N % ROW_TM == 0 and N % GRID_W == 0 and D % LANE == 0
    ncb = NC // ROW_TM
    nblk = T // ROW_TM
    alpha = (2 * L) ** 0.25
    ax, ay, ac = lax.axis_index("x"), lax.axis_index("y"), lax.axis_index("c")
    chip = 2 * ax + ay
    dev = 2 * chip + ac
    chip_arr = jnp.reshape(chip, (1,)).astype(jnp.int32)
    core_arr = jnp.reshape(ac, (1,)).astype(jnp.int32)
    ns_ada = W["w_ada"].shape[2]
    chip_devs = (0, 2, 4, 6)

    ffn_keys = ("w_ff_gate", "w_ff_up", "w_ff_down")
    mid_keys = ("w_br_attn", "w_br_pool", "w_br_sgu", "w_br_conv", "w_o")
    groups = [grp for l in range(L) for grp in ([("w_in", l)], [("w_gate", l)], [(k, l) for k in mid_keys],
                                                [(k, l) for k in ffn_keys])]
    flying, gsems, gtokens = [None] * len(groups), [None] * len(groups), []
    for n_start, gs in enumerate(((0, 1), (2, 3), (4, 5, 6, 7))):
        members = [kl for g in gs for kl in groups[g]]
        sems, bufs, token = _gather_start([_cast_into(chip_arr, W[k], l, "cast_%s_l%d" % (k, l)) for k, l in members],
                                          [len(groups[g]) for g in gs], "gather_start_%d" % n_start)
        gtokens.append(token)
        lo = 0
        for g, sem in zip(gs, sems):
            flying[g], gsems[g] = bufs[lo:lo + len(groups[g])], sem
            lo += len(groups[g])
    full = {k: [None] * L for k in BIG}

    def weights_ready(g, after):
        got = _gather_wait(flying[g], gsems[g], after, "gather_wait_%d" % g)
        for (k, l), buf in zip(groups[g], _exchange_halves(got, "exchange_halves_%d" % g)):
            full[k][l] = buf

    conv_shape = W["conv_w"].shape
    g0 = _all_gather_small(_pack([c, W["conv_w"]]), "gather_c")
    c_all = g0[:, :D // LANE, :].reshape(N_DEV, D)
    conv_parts = [g0[d].reshape(-1)[D:D + math.prod(conv_shape)].reshape(conv_shape) for d in chip_devs]
    conv_full = jnp.concatenate(conv_parts, axis=-1)
    cvec = jnp.concatenate([c_all, W["c_ctx"][None], jnp.zeros((16 - N_DEV - 1, D), F32)], axis=0)
    b_cols = lax.dynamic_slice_in_dim(W["b_ada"], chip * ns_ada, ns_ada, axis=1).reshape(L, 1, ns_ada)
    mod_part = _ada_fwd(cvec, W["w_ada"], b_cols, "ada_fwd")
    g1 = _all_gather_small(mod_part.reshape(-1, LANE), "gather_mod")
    mod_all = jnp.concatenate([g1[d].reshape(L, 16, ns_ada) for d in chip_devs], axis=-1)
    mod_lat = lax.dynamic_index_in_dim(mod_all, dev, axis=1, keepdims=True)
    mods = jnp.concatenate([mod_all[:, N_DEV:N_DEV + 1], mod_lat], axis=1)

    cos, sin = _rope_tables(N, NC)
    x0 = jnp.concatenate([ctx[0], x[0]], axis=0)

    saved = []
    xin = x0
    h1 = None
    dy_top = loss_parts = None
    for l in range(L):
        sh1, sc1, g1m, sh2, sc2, g2m = _split6(mods[l])
        tag = "_l%d" % l
        if l == 0:
            h1 = _modulate(xin, sc1, sh1, ncb, "modulate" + tag)
        weights_ready(4 * l, [h1] + (gtokens[1:] if l == 0 else []))
        p = _mm_nn(h1, full["w_in"], l, "col", name="mm_in" + tag)
        qg, kg = W["q_norm_g"][l][None], W["k_norm_g"][l][None]
        q, k, v = _qk_fwd(p, cos, sin, qg, kg, "qk_fwd" + tag)
        att = _attn_fwd(q, k, v, NC, "attn_fwd" + tag)
        ps = W["pool_scale"][l][None]
        ypool = _pool_fwd(p, W["pool_w"][l], ps, NC, "pool_fwd" + tag)
        lg, lb = W["sgu_ln_g"][l][None], W["sgu_ln_b"][l][None]
        b_st = W["sgu_b"][l].T
        ysgu = _sgu_fwd(p, lg, lb, W["sgu_w"][l], b_st, "sgu_fwd" + tag)
        yconv = _conv_fwd(p, conv_full[l], NC, "conv_fwd" + tag)
        brs = (att, ypool, ysgu, yconv)
        weights_ready(4 * l + 1, list(brs))
        gates = _mm_nn(h1, full["w_gate"], l, "col", name="mm_gate" + tag, bias=W["b_gate"][l][None],
                       act=jax.nn.sigmoid, out_dtype=BF16)
        weights_ready(4 * l + 2, [gates])
        ts = [_mm_nn(b, full[wk], l, "col", name="mm_" + wk + tag, out_dtype=BF16)
              for b, wk in zip(brs, ("w_br_attn", "w_br_pool", "w_br_sgu", "w_br_conv"))]
        mg = _merge_fwd(gates, ts, "merge_fwd" + tag)
        m = _mm_nn(mg, full["w_o"], l, "row", name="mm_o" + tag, tn_pref=1024)
        ln1g, ln1b = W["ln1_g"][l][None], W["ln1_b"][l][None]
        x1, h2 = _resid_ln(xin, m, g1m, ln1g, ln1b, sc2, sh2, ncb, alpha, "resid_ln1" + tag)
        weights_ready(4 * l + 3, [h2])
        gg, uu, act = _ffn_up(h2, full["w_ff_gate"][l], full["w_ff_up"][l], "ffn_up" + tag)
        ff = _mm_nn(act, full["w_ff_down"], l, "row", name="mm_ffd" + tag, tm_pref=384, tn_pref=512)
        ln2g, ln2b = W["ln2_g"][l][None], W["ln2_b"][l][None]
        saved.append(dict(xin=xin, h1=h1, p=p, q=q, k=k, v=v, brs=brs, gates=gates, ts=ts, mg=mg, m=m, x1=x1, h2=h2,
                          gg=gg, uu=uu, act=act, ff=ff))
        if l + 1 < L:
            nsh1, nsc1 = _split6(mods[l + 1])[:2]
            xin, h1 = _resid_ln(x1, ff, g2m, ln2g, ln2b, nsc1, nsh1, ncb, alpha, "resid_ln2" + tag)
        else:
            dy_top, loss_parts = _resid_ln_loss(x1, ff, g2m, ln2g, ln2b, loss_target[0], ncb, alpha,
                                                "resid_ln2_loss" + tag)
    loss_dev = jnp.sum(loss_parts[:, 0, 0])

    def lat_ctx(part_rows):
        return jnp.stack([jnp.sum(part_rows[:ncb], axis=0), jnp.sum(part_rows[ncb:], axis=0)])

    dW = {k: [None] * L for k in BIG}
    small = {k: [None] * L for k in ("q_norm_g", "k_norm_g", "pool_w", "pool_scale", "sgu_ln_g", "sgu_ln_b", "sgu_w",
                                     "sgu_b", "conv_w", "b_gate", "ln1_g", "ln1_b", "ln2_g", "ln2_b")}
    dmods = [None] * L
    mix_keys = tuple(k for k in BIG if k not in ffn_keys)
    grads_big = {k: None for k in BIG}
    adam_out = {k: None for k in BIG}

    def reduce_begin(keys, l, tg):
        sems, src, land, token = _split_start([dW[k][l] for k in keys], _prereduce_plan, _prereduce_land,
                                              "prereduce_start_" + tg)
        return dict(keys=keys, l=l, sems=sems, src=src, land=land, token=token, tg=tg)

    def reduce_scatter(st, after):
        dws, recv = _split_wait(st["src"], st["land"], st["sems"], _prereduce_plan, after,
                                "prereduce_wait_" + st["tg"])
        psums = [_pair_sum(core_arr, d, r, "pair_sum_%s_l%d" % (k, st["l"]))
                 for k, d, r in zip(st["keys"], dws, recv)]
        st["sems"], st["src"], st["land"], st["token"] = _split_start(psums, _scatter_plan, _scatter_land,
                                                                      "scatter_start_" + st["tg"])

    def reduce_finish(st, after):
        src, land = _split_wait(st["src"], st["land"], st["sems"], _scatter_plan, after, "scatter_wait_" + st["tg"])
        for k, p_, r in zip(st["keys"], src, land):
            grads_big[k] = _chip_sum(chip_arr, core_arr, p_, r, st["l"], L, grads_big[k],
                                     "chip_sum_%s_l%d" % (k, st["l"]))
        done = _exchange_grads([grads_big[k] for k in st["keys"]], st["l"], "exchange_grads_" + st["tg"])
        grads_big.update(zip(st["keys"], done))

    def adam(keys, l, after):
        for k in keys:
            adam_out[k] = _adamw_layer(W[k], grads_big[k], M[k], V[k], l, adam_out[k], after,
                                       "adamw_%s_l%d" % (k, l))

    dxa, dhs, sc_prev = dy_top, [], None
    red_l1 = red_ffn = None
    for l in reversed(range(L)):
        s = saved[l]
        sh1, sc1, g1m, sh2, sc2, g2m = _split6(mods[l])
        tag = "_l%d" % l
        ln1g, ln1b = W["ln1_g"][l][None], W["ln1_b"][l][None]
        ln2g, ln2b = W["ln2_g"][l][None], W["ln2_b"][l][None]
        dx1a, dff, part2 = _ln_bwd(dxa, dhs, sc_prev if dhs else sc1, s["x1"], s["ff"], g2m, ln2g, ln2b, ncb, alpha,
                                   "ln2_bwd" + tag, after=[] if red_l1 is None else [red_l1["token"]])
        small["ln2_g"][l] = jnp.sum(part2[:, 0], axis=0)
        small["ln2_b"][l] = jnp.sum(part2[:, 1], axis=0)
        dg2 = lat_ctx(part2[:, 2])
        if dhs:
            dmods[l + 1][1], dmods[l + 1][0] = lat_ctx(part2[:, 3]), lat_ctx(part2[:, 4])
        dgg, duu = _ffn_down_bwd(dff, full["w_ff_down"][l], s["gg"], s["uu"], "ffn_down_bwd" + tag)
        dW["w_ff_down"][l] = _mm_tn(s["act"], dff, "row", nq=N_CHIP, kdim=FF // N_CHIP, ndim=D, name="tn_ffd" + tag,
                                    tk_pref=1408, tn_pref=512)
        dh2a = _mm_nt(dgg, full["w_ff_gate"], l, "col", name="nt_ffg" + tag)
        dh2b = _mm_nt(duu, full["w_ff_up"], l, "col", name="nt_ffu" + tag)
        dW["w_ff_gate"][l] = _mm_tn(s["h2"], dgg, "col", nq=N_CHIP, kdim=D, ndim=FF // N_CHIP, name="tn_ffg" + tag,
                                    tn_pref=1408)
        dW["w_ff_up"][l] = _mm_tn(s["h2"], duu, "col", nq=N_CHIP, kdim=D, ndim=FF // N_CHIP, name="tn_ffu" + tag,
                                  tn_pref=1408)
        ties = []
        if l == 0:
            reduce_scatter(red_l1, [dh2a, dh2b])
            red_ffn = reduce_begin(ffn_keys, 0, "l0_ffn")
            ties = [red_l1["token"], red_ffn["token"]]
        dx0a, dm, part1 = _ln_bwd(dx1a, [dh2a, dh2b], sc2, s["xin"], s["m"], g1m, ln1g, ln1b, ncb, alpha,
                                  "ln1_bwd" + tag, after=ties)
        small["ln1_g"][l] = jnp.sum(part1[:, 0], axis=0)
        small["ln1_b"][l] = jnp.sum(part1[:, 1], axis=0)
        dg1 = lat_ctx(part1[:, 2])
        dsc2, dsh2 = lat_ctx(part1[:, 3]), lat_ctx(part1[:, 4])
        dmods[l] = [None, None, dg1, dsh2, dsc2, dg2]
        dmg = _mm_nt(dm, full["w_o"], l, "row", name="nt_o" + tag)
        dW["w_o"][l] = _mm_tn(s["mg"], dm, "row", nq=N_CHIP, kdim=D // N_CHIP, ndim=D, name="tn_o" + tag, tn_pref=1024)
        dpre, dt0, dt1, dt2, dt3, bpart = _merge_bwd(dmg, s["gates"], s["ts"], "merge_bwd" + tag)
        small["b_gate"][l] = jnp.sum(bpart[:, 0], axis=0)
        dh1a = _mm_nt(dpre, full["w_gate"], l, "col", name="nt_gate" + tag)
        dW["w_gate"][l] = _mm_tn(s["h1"], dpre, "col", nq=N_CHIP, kdim=D, ndim=D, name="tn_gate" + tag, tn_pref=1024)
        dbrs = []
        for b, dt, wk, odt in zip(s["brs"], (dt0, dt1, dt2, dt3), ("w_br_attn", "w_br_pool", "w_br_sgu", "w_br_conv"),
                                  (BF16, F32, F32, F32)):
            dbrs.append(_mm_nt(dt, full[wk], l, "col", name="nt_" + wk + tag, out_dtype=odt))
            dW[wk][l] = _mm_tn(b, dt, "col", nq=N_CHIP, kdim=b.shape[1], ndim=D // N_CHIP, name="tn_" + wk + tag)
        qg, kg = W["q_norm_g"][l][None], W["k_norm_g"][l][None]
        dq, dk, dv = _attn_bwd(s["q"], s["k"], s["v"], dbrs[0], NC, "attn_bwd" + tag)
        dp_qkv, qkpart = _qk_bwd(s["p"], dq, dk, dv, cos, sin, qg, kg, "qk_bwd" + tag)
        small["q_norm_g"][l] = jnp.sum(qkpart[:, 0], axis=0)
        small["k_norm_g"][l] = jnp.sum(qkpart[:, 1], axis=0)
        ps = W["pool_scale"][l][None]
        dp_pool, small["pool_w"][l], dps = _pool_bwd(s["p"], dbrs[1], W["pool_w"][l], ps, NC, "pool_bwd" + tag)
        small["pool_scale"][l] = dps[0]
        lg, lb = W["sgu_ln_g"][l][None], W["sgu_ln_b"][l][None]
        dp_sgu, small["sgu_w"][l], dsacc, dln = _sgu_bwd(s["p"], dbrs[2], lg, lb, W["sgu_w"][l], W["sgu_b"][l].T,
                                                         "sgu_bwd" + tag)
        small["sgu_b"][l] = jnp.sum(dsacc.reshape(GC, N_GROUP, GC), axis=-1).T
        small["sgu_ln_g"][l], small["sgu_ln_b"][l] = dln[0], dln[1]
        dp_conv, small["conv_w"][l] = _conv_bwd(s["p"], dbrs[3], conv_full[l], NC, "conv_bwd" + tag)
        dp = jnp.concatenate([dp_qkv, dp_pool, dp_sgu, dp_conv[0], dp_conv[1], dp_conv[2]], axis=-1)
        dh1b = _mm_nt(dp, full["w_in"], l, "col", name="nt_in" + tag)
        dW["w_in"][l] = _mm_tn(s["h1"], dp, "col", nq=N_CHIP, kdim=D, ndim=IN_W // N_CHIP, name="tn_in" + tag,
                               tn_pref=1152)
        dxa, dhs, sc_prev = dx0a, [dh1a, dh1b], sc1
        if l == 1:
            red_l1 = reduce_begin(BIG, 1, "l1")
    reduce_scatter(red_ffn, list(dhs))
    dx0, part0 = _mod_bwd(dxa, dhs, sc_prev, x0, ncb, "mod_bwd", after=[red_ffn["token"]])
    dmods[0][1], dmods[0][0] = lat_ctx(part0[:, 0]), lat_ctx(part0[:, 1])
    grad_x = dx0[NC:][None]
    dmod = jnp.stack([jnp.concatenate(dmods[l], axis=-1) for l in range(L)])

    red_mix = reduce_begin(mix_keys, 0, "l0_mix")
    reduce_finish(red_l1, [red_mix["token"], dx0])
    reduce_scatter(red_mix, [grads_big[k] for k in BIG])
    adam(BIG, 1, [red_mix["token"]])
    reduce_finish(red_ffn, [red_mix["token"]])
    adam(ffn_keys, 0, [])

    small_names = [k for k in SMALL if k not in ("c_ctx", "b_ada")]
    small_arrs = [jnp.stack(small[k]) for k in small_names]
    small_shapes = [a.shape for a in small_arrs]
    slab = _pack([loss_dev.reshape(1), jnp.zeros((LANE - 1,), F32), dmod] + small_arrs)
    g2 = _all_gather_small(slab, "gather_small")
    total = _slab_sum(g2, "slab_sum")
    flat_total = total.reshape(-1)
    loss = flat_total[0]
    nmod = L * 2 * 6 * D
    dmod_sum = flat_total[LANE:LANE + nmod].reshape(L, 2, 6 * D)
    small_grads = dict(zip(small_names, _unpack(flat_total[LANE + nmod:], small_shapes)))
    dmod_lat = g2.reshape(N_DEV, -1)[:, LANE:LANE + nmod].reshape(N_DEV, L, 2, 6 * D)[:, :, 1]
    dm16 = jnp.concatenate([jnp.transpose(dmod_lat, (1, 0, 2)), dmod_sum[:, 0:1],
                            jnp.zeros((L, 16 - N_DEV - 1, 6 * D), F32)], axis=1)
    small_grads["b_ada"] = dmod_sum[:, 0] + dmod_sum[:, 1]
    dm16_cols = lax.dynamic_slice_in_dim(dm16, chip * ns_ada, ns_ada, axis=2)
    grad_w_ada, ds16 = _ada_bwd(cvec, dm16_cols, W["w_ada"], "ada_bwd")
    g3 = _all_gather_small(ds16[N_DEV].reshape(-1, LANE), "gather_dsilu")
    small_grads["c_ctx"] = _cctx_grad(g3, W["c_ctx"].reshape(-1, LANE), "cctx_grad").reshape(D)
    conv_grad_full = small_grads["conv_w"]
    small_grads["conv_w"] = lax.dynamic_slice_in_dim(conv_grad_full, chip * GC, GC, axis=2)

    delta, new_m, new_v = {}, {}, {}
    delta["w_ada"], new_m["w_ada"], new_v["w_ada"] = _adamw(W["w_ada"], grad_w_ada, M["w_ada"], V["w_ada"],
                                                            "adamw_w_ada")
    shapes = [W[k].shape for k in SMALL]
    sd, sm, sv = _adamw(_pack([W[k] for k in SMALL]), _pack([small_grads[k] for k in SMALL]),
                        _pack([M[k] for k in SMALL]), _pack([V[k] for k in SMALL]), "adamw_small")
    for k, d_, m_, v_ in zip(SMALL, _unpack(sd, shapes), _unpack(sm, shapes), _unpack(sv, shapes)):
        delta[k], new_m[k], new_v[k] = d_, m_, v_
    reduce_finish(red_mix, [delta["w_ada"], sd] + [adam_out[k][0] for k in BIG])
    adam(mix_keys, 0, [])
    for k in BIG:
        delta[k], new_m[k], new_v[k] = adam_out[k]
    grads = dict(grads_big)
    grads["w_ada"] = grad_w_ada
    grads.update(small_grads)
    return (loss, grad_x, *[grads[k] for k in WEIGHTS], *[delta[k] for k in WEIGHTS],
            *[new_m[k] for k in WEIGHTS], *[new_v[k] for k in WEIGHTS])


def kernel(x, c, ctx, c_ctx, w_ada, b_ada, w_in, q_norm_g, k_norm_g, pool_w, pool_scale, sgu_ln_g, sgu_ln_b, sgu_w, sgu_b, conv_w, w_br_attn, w_br_pool, w_br_sgu, w_br_conv, w_gate, b_gate, w_o, ln1_g, ln1_b, w_ff_gate, w_ff_up, w_ff_down, ln2_g, ln2_b, loss_target, m_c_ctx, m_w_ada, m_b_ada, m_w_in, m_q_norm_g, m_k_norm_g, m_pool_w, m_pool_scale, m_sgu_ln_g, m_sgu_ln_b, m_sgu_w, m_sgu_b, m_conv_w, m_w_br_attn, m_w_br_pool, m_w_br_sgu, m_w_br_conv, m_w_gate, m_b_gate, m_w_o, m_ln1_g, m_ln1_b, m_w_ff_gate, m_w_ff_up, m_w_ff_down, m_ln2_g, m_ln2_b, v_c_ctx, v_w_ada, v_b_ada, v_w_in, v_q_norm_g, v_k_norm_g, v_pool_w, v_pool_scale, v_sgu_ln_g, v_sgu_ln_b, v_sgu_w, v_sgu_b, v_conv_w, v_w_br_attn, v_w_br_pool, v_w_br_sgu, v_w_br_conv, v_w_gate, v_b_gate, v_w_o, v_ln1_g, v_ln1_b, v_w_ff_gate, v_w_ff_up, v_w_ff_down, v_ln2_g, v_ln2_b):
    args = locals()
    W = {k: args[k] for k in WEIGHTS}
    M = {k: args["m_" + k] for k in WEIGHTS}
    V = {k: args["v_" + k] for k in WEIGHTS}
    return _step(x, c, ctx, loss_target, W, M, V)
```

```python
import functools
import math

import jax
import jax.numpy as jnp
from jax import lax
from jax.experimental import pallas as pl
from jax.experimental.pallas import tpu as pltpu

F32 = jnp.float32
BF16 = jnp.bfloat16
MESH = pl.DeviceIdType.MESH

N_DEV = 8
N_CHIP = 4
GRID_W = 64
HEAD_DIM = 128
N_HEADS = 8
N_KV = 2
KV_GROUP = N_HEADS // N_KV
Q_W = N_HEADS * HEAD_DIM
KV_W = N_KV * HEAD_DIM
QKV_W = Q_W + 2 * KV_W
GC = 128
N_GROUP = 4
BR_W = N_GROUP * GC
POOL_WINDOWS = (2, 4, 8, 16)
OFF_POOL = QKV_W
OFF_U = OFF_POOL + BR_W
OFF_VG = OFF_U + BR_W
OFF_CB = OFF_VG + BR_W
OFF_CC = OFF_CB + BR_W
OFF_CX = OFF_CC + BR_W
IN_W = OFF_CX + BR_W
Q_BLOCK = 128
ROPE_THETA = 10000.0
LN_EPS = 1e-5
RMS_EPS = 1e-6
ADAM_LR = 0.001
ADAM_B1 = 0.9
ADAM_B2 = 0.999
ADAM_EPS = 1e-08
ADAM_WD = 0.01
ADAM_STEP = 10

ROW_TM = 256
LANE = 128
SUBLANE = 8
VMEM_BIG = 56 * 1024 * 1024
VMEM_MID = 40 * 1024 * 1024


def _cp(sem=None, vmem=VMEM_MID):
    return pltpu.CompilerParams(dimension_semantics=sem, vmem_limit_bytes=vmem)


_ANY = pl.BlockSpec(memory_space=pl.ANY)
_HBM = pl.BlockSpec(memory_space=pltpu.HBM)
_SEM = pl.BlockSpec(memory_space=pltpu.SEMAPHORE)
_EFFECT = pltpu.SideEffectType.DATAFLOW_SIDE_EFFECTING


def _tile(n, pref, unit=LANE, whole_ok=False):
    if n <= pref:
        return n
    best = None
    for t in range(unit, pref + 1, unit):
        if n % t == 0:
            best = t
    if whole_ok and (best is None or best * 4 < pref):
        return n
    assert best is not None, (n, pref, unit)
    return best


def _layer_of(w, l):
    if isinstance(w, (list, tuple)):
        return w[l][:, None], 0
    return w, l


def _mm_nn(a, w, l, kind, *, name, out_dtype=F32, bias=None, act=None, tm_pref=768, tn_pref=1152):
    w, l = _layer_of(w, l)
    T, K = a.shape
    tm = _tile(T, tm_pref, SUBLANE * 2)
    if kind == "col":
        nq, _, kw, ns = w.shape
        assert kw == K
        tn = _tile(ns, tn_pref, whole_ok=True)
        nj = ns // tn
        n_total = nq * ns
        w_spec = pl.BlockSpec((None, None, K, tn), lambda j, i: (j // nj, l, 0, j % nj))
        grid_n = nq * nj
    else:
        nq, _, kc, n_total = w.shape
        assert nq * kc == K
        tn = _tile(n_total, tn_pref)
        w_spec = pl.BlockSpec((nq, None, kc, tn), lambda j, i: (0, l, 0, j))
        grid_n = n_total // tn

    def body(*refs):
        if bias is not None:
            a_ref, w_ref, b_ref, o_ref = refs
        else:
            a_ref, w_ref, o_ref = refs
        wv = w_ref[...]
        if kind == "row":
            wv = wv.reshape(K, tn)
        acc = jnp.dot(a_ref[...], wv, preferred_element_type=F32)
        if bias is not None:
            acc = acc + b_ref[...]
        if act is not None:
            acc = act(acc)
        o_ref[...] = acc.astype(out_dtype)

    in_specs = [pl.BlockSpec((tm, K), lambda j, i: (i, 0)), w_spec]
    args = [a, w]
    if bias is not None:
        in_specs.append(pl.BlockSpec((1, tn), lambda j, i: (0, j)))
        args.append(bias)
    return pl.pallas_call(
        body, grid=(grid_n, T // tm), in_specs=in_specs,
        out_specs=pl.BlockSpec((tm, tn), lambda j, i: (i, j)),
        out_shape=jax.ShapeDtypeStruct((T, n_total), out_dtype),
        name=name, compiler_params=_cp(("parallel", "parallel"), VMEM_BIG),
    )(*args)


def _mm_nt(dc, w, l, kind, *, name, out_dtype=F32, dc_off=0, tm_pref=768, tk_pref=1024):
    w, l = _layer_of(w, l)
    T = dc.shape[0]
    tm = _tile(T, tm_pref, SUBLANE * 2)
    if kind == "col":
        nq, _, K, ns = w.shape
        tk = _tile(K, tk_pref)
        assert dc_off % ns == 0
        offb = dc_off // ns

        def body(dc_ref, w_ref, o_ref, acc_ref):
            q = pl.program_id(2)
            part = lax.dot_general(dc_ref[...], w_ref[...], (((1,), (1,)), ((), ())),
                                   preferred_element_type=F32)

            @pl.when(q == 0)
            def _():
                acc_ref[...] = part

            @pl.when(q > 0)
            def _():
                acc_ref[...] += part

            @pl.when(q == nq - 1)
            def _():
                o_ref[...] = acc_ref[...].astype(out_dtype)

        return pl.pallas_call(
            body, grid=(K // tk, T // tm, nq),
            in_specs=[pl.BlockSpec((tm, ns), lambda kk, i, q: (i, offb + q)),
                      pl.BlockSpec((None, None, tk, ns), lambda kk, i, q: (q, l, kk, 0))],
            out_specs=pl.BlockSpec((tm, tk), lambda kk, i, q: (i, kk)),
            out_shape=jax.ShapeDtypeStruct((T, K), out_dtype),
            scratch_shapes=[pltpu.VMEM((tm, tk), F32)],
            name=name, compiler_params=_cp(("parallel", "parallel", "arbitrary"), VMEM_BIG),
        )(dc, w)

    nq, _, kc, n = w.shape
    assert dc_off % n == 0
    offb = dc_off // n
    tk = _tile(kc, tk_pref, whole_ok=True)
    nkk = kc // tk

    def body(dc_ref, w_ref, o_ref):
        o_ref[...] = lax.dot_general(dc_ref[...], w_ref[...], (((1,), (1,)), ((), ())),
                                     preferred_element_type=F32).astype(out_dtype)

    return pl.pallas_call(
        body, grid=(nq * nkk, T // tm),
        in_specs=[pl.BlockSpec((tm, n), lambda j, i: (i, offb)),
                  pl.BlockSpec((None, None, tk, n), lambda j, i: (j // nkk, l, j % nkk, 0))],
        out_specs=pl.BlockSpec((tm, tk), lambda j, i: (i, j)),
        out_shape=jax.ShapeDtypeStruct((T, nq * kc), out_dtype),
        name=name, compiler_params=_cp(("parallel", "parallel"), VMEM_BIG),
    )(dc, w)


def _mm_tn(a, dc, kind, *, nq, kdim, ndim, name, a_off=0, dc_off=0, tk_pref=512, tn_pref=1152, out_dtype=BF16):
    T = a.shape[0]
    assert dc.shape[0] == T
    tk = _tile(kdim, tk_pref, whole_ok=True)
    tn = _tile(ndim, tn_pref, whole_ok=True)
    nkk, njn = kdim // tk, ndim // tn
    assert a_off % tk == 0 and dc_off % tn == 0
    aoffb, doffb = a_off // tk, dc_off // tn
    if kind == "col":
        a_map = lambda q, kk, jn: (0, aoffb + kk)
        d_map = lambda q, kk, jn: (0, doffb + q * njn + jn)
    else:
        a_map = lambda q, kk, jn: (0, aoffb + q * nkk + kk)
        d_map = lambda q, kk, jn: (0, doffb + jn)

    def body(a_ref, d_ref, o_ref):
        o_ref[...] = lax.dot_general(a_ref[...], d_ref[...], (((0,), (0,)), ((), ())),
                                     preferred_element_type=F32).astype(out_dtype)

    return pl.pallas_call(
        body, grid=(nq, nkk, njn),
        in_specs=[pl.BlockSpec((T, tk), a_map), pl.BlockSpec((T, tn), d_map)],
        out_specs=pl.BlockSpec((None, tk, tn), lambda q, kk, jn: (q, kk, jn)),
        out_shape=jax.ShapeDtypeStruct((nq, kdim, ndim), out_dtype),
        name=name, compiler_params=_cp(("parallel", "parallel", "parallel"), VMEM_BIG),
    )(a, dc)


def _row_spec(d):
    return pl.BlockSpec((ROW_TM, d), lambda i: (i, 0))


def _mod_spec(d, ncb):
    return pl.BlockSpec((None, 1, d), lambda i: (jnp.where(i >= ncb, 1, 0), 0, 0))


def _vec_spec(d):
    return pl.BlockSpec((1, d), lambda i: (0, 0))


def _part_spec(d):
    return pl.BlockSpec((None, SUBLANE, d), lambda i: (i, 0, 0))


def _modulate(x, sc, sh, ncb, name):
    T, D = x.shape

    def body(x_ref, sc_ref, sh_ref, o_ref):
        o_ref[...] = (x_ref[...] * (1.0 + sc_ref[...]) + sh_ref[...]).astype(BF16)

    return pl.pallas_call(
        body, grid=(T // ROW_TM,),
        in_specs=[_row_spec(D), _mod_spec(D, ncb), _mod_spec(D, ncb)],
        out_specs=_row_spec(D), out_shape=jax.ShapeDtypeStruct((T, D), BF16),
        name=name, compiler_params=_cp(("parallel",)),
    )(x, sc, sh)


def _ln_stats(r):
    mu = jnp.mean(r, axis=-1, keepdims=True)
    rc = r - mu
    var = jnp.mean(rc * rc, axis=-1, keepdims=True)
    rstd = lax.rsqrt(var + LN_EPS)
    return rc * rstd, rstd


def _resid_ln(x, y, gate, g, b, sc, sh, ncb, alpha, name):
    T, D = x.shape

    def body(x_ref, y_ref, gate_ref, g_ref, b_ref, sc_ref, sh_ref, xo_ref, h_ref):
        xhat, _ = _ln_stats(alpha * x_ref[...] + gate_ref[...] * y_ref[...])
        xo = xhat * g_ref[...] + b_ref[...]
        xo_ref[...] = xo
        h_ref[...] = (xo * (1.0 + sc_ref[...]) + sh_ref[...]).astype(BF16)

    return pl.pallas_call(
        body, grid=(T // ROW_TM,),
        in_specs=[_row_spec(D), _row_spec(D), _mod_spec(D, ncb), _vec_spec(D), _vec_spec(D),
                  _mod_spec(D, ncb), _mod_spec(D, ncb)],
        out_specs=[_row_spec(D), _row_spec(D)],
        out_shape=[jax.ShapeDtypeStruct((T, D), F32), jax.ShapeDtypeStruct((T, D), BF16)],
        name=name, compiler_params=_cp(("parallel",)),
    )(x, y, gate, g, b, sc, sh)


def _resid_ln_loss(x, y, gate, g, b, target, ncb, alpha, name):
    T, D = x.shape
    nblk = T // ROW_TM

    def body(x_ref, y_ref, gate_ref, g_ref, b_ref, t_ref, dy_ref, loss_ref):
        i = pl.program_id(0)
        xhat, _ = _ln_stats(alpha * x_ref[...] + gate_ref[...] * y_ref[...])
        xo = xhat * g_ref[...] + b_ref[...]
        live = (i >= ncb).astype(F32)
        err = (xo - t_ref[...]) * live
        dy_ref[...] = err * (1.0 / D)
        loss_ref[...] = jnp.full((SUBLANE, LANE), 0.5 / D, F32) * jnp.sum(err * err)

    return pl.pallas_call(
        body, grid=(nblk,),
        in_specs=[_row_spec(D), _row_spec(D), _mod_spec(D, ncb), _vec_spec(D), _vec_spec(D),
                  pl.BlockSpec((ROW_TM, D), lambda i: (jnp.maximum(i - ncb, 0), 0))],
        out_specs=[_row_spec(D), _part_spec(LANE)],
        out_shape=[jax.ShapeDtypeStruct((T, D), F32), jax.ShapeDtypeStruct((nblk, SUBLANE, LANE), F32)],
        name=name, compiler_params=_cp(("parallel",)),
    )(x, y, gate, g, b, target)


def _write_parts(part_ref, rows, d):
    for k, r in enumerate(rows):
        part_ref[pl.ds(k, 1), :] = jnp.sum(r, axis=0, keepdims=True)
    if len(rows) < SUBLANE:
        part_ref[pl.ds(len(rows), SUBLANE - len(rows)), :] = jnp.zeros((SUBLANE - len(rows), d), F32)


def _ln_bwd(dxa, dhs, sc, x, y, gate, g, b, ncb, alpha, name, after=None):
    T, D = x.shape
    nblk = T // ROW_TM
    ndh = len(dhs)

    def body(*refs):
        dxa_ref = refs[0]
        dh_refs = refs[1:1 + ndh]
        sc_ref, x_ref, y_ref, gate_ref, g_ref, b_ref = refs[1 + ndh:7 + ndh]
        dx_ref, dy_ref, part_ref = refs[-3:]
        yv = y_ref[...]
        xhat, rstd = _ln_stats(alpha * x_ref[...] + gate_ref[...] * yv)
        dxo = dxa_ref[...]
        rows = []
        if ndh:
            dh = dh_refs[0][...]
            for r in dh_refs[1:]:
                dh = dh + r[...]
            dxo = dxo + dh * (1.0 + sc_ref[...])
            xo = xhat * g_ref[...] + b_ref[...]
            rows = [dh * xo, dh]
        dxhat = dxo * g_ref[...]
        m1 = jnp.mean(dxhat, axis=-1, keepdims=True)
        m2 = jnp.mean(dxhat * xhat, axis=-1, keepdims=True)
        dr = rstd * (dxhat - m1 - xhat * m2)
        dx_ref[...] = alpha * dr
        dy_ref[...] = (gate_ref[...] * dr).astype(BF16)
        _write_parts(part_ref, [dxo * xhat, dxo, dr * yv] + rows, D)

    in_specs = ([_row_spec(D)] * (1 + ndh)
                + [_mod_spec(D, ncb), _row_spec(D), _row_spec(D), _mod_spec(D, ncb), _vec_spec(D), _vec_spec(D)])
    extra = list(after or ())
    return pl.pallas_call(
        body, grid=(nblk,), in_specs=in_specs + [_ANY] * len(extra),
        out_specs=[_row_spec(D), _row_spec(D), _part_spec(D)],
        out_shape=[jax.ShapeDtypeStruct((T, D), F32), jax.ShapeDtypeStruct((T, D), BF16),
                   jax.ShapeDtypeStruct((nblk, SUBLANE, D), F32)],
        name=name, compiler_params=_cp(("parallel",)),
    )(dxa, *dhs, sc, x, y, gate, g, b, *extra)


def _mod_bwd(dxa, dhs, sc, x, ncb, name, after=()):
    T, D = x.shape
    nblk = T // ROW_TM
    ndh = len(dhs)

    def body(*refs):
        dxa_ref = refs[0]
        dh_refs = refs[1:1 + ndh]
        sc_ref, x_ref = refs[1 + ndh:3 + ndh]
        dx_ref, part_ref = refs[-2:]
        dh = dh_refs[0][...]
        for r in dh_refs[1:]:
            dh = dh + r[...]
        dx_ref[...] = dxa_ref[...] + dh * (1.0 + sc_ref[...])
        _write_parts(part_ref, [dh * x_ref[...], dh], D)

    return pl.pallas_call(
        body, grid=(nblk,),
        in_specs=[_row_spec(D)] * (1 + ndh) + [_mod_spec(D, ncb), _row_spec(D)] + [_ANY] * len(after),
        out_specs=[_row_spec(D), _part_spec(D)],
        out_shape=[jax.ShapeDtypeStruct((T, D), F32), jax.ShapeDtypeStruct((nblk, SUBLANE, D), F32)],
        name=name, compiler_params=_cp(("parallel",)),
    )(dxa, *dhs, sc, x, *after)


def _merge_fwd(gates, ts, name):
    T, D = ts[0].shape
    tm = 128

    def body(g_ref, t0, t1, t2, t3, o_ref):
        acc = g_ref[:, 0:D].astype(F32) * t0[...].astype(F32)
        for k, t in enumerate((t1, t2, t3), start=1):
            acc = acc + g_ref[:, k * D:(k + 1) * D].astype(F32) * t[...].astype(F32)
        o_ref[...] = acc.astype(BF16)

    rs = pl.BlockSpec((tm, D), lambda i: (i, 0))
    return pl.pallas_call(
        body, grid=(T // tm,),
        in_specs=[pl.BlockSpec((tm, 4 * D), lambda i: (i, 0)), rs, rs, rs, rs],
        out_specs=rs, out_shape=jax.ShapeDtypeStruct((T, D), BF16),
        name=name, compiler_params=_cp(("parallel",)),
    )(gates, *ts)


def _merge_bwd(dmg, gates, ts, name):
    T, D = dmg.shape
    tm = 128
    nblk = T // tm

    def body(d_ref, g_ref, t0, t1, t2, t3, dpre_ref, dt0, dt1, dt2, dt3, part_ref):
        d = d_ref[...]
        for k, (t, dt) in enumerate(zip((t0, t1, t2, t3), (dt0, dt1, dt2, dt3))):
            gk = g_ref[:, k * D:(k + 1) * D].astype(F32)
            dt[...] = (d * gk).astype(BF16)
            dpre = d * t[...].astype(F32) * gk * (1.0 - gk)
            dpre_ref[:, k * D:(k + 1) * D] = dpre.astype(BF16)
            part_ref[:, k * D:(k + 1) * D] = jnp.sum(dpre, axis=0, keepdims=True)

    rs = pl.BlockSpec((tm, D), lambda i: (i, 0))
    wide = pl.BlockSpec((tm, 4 * D), lambda i: (i, 0))
    return pl.pallas_call(
        body, grid=(nblk,),
        in_specs=[rs, wide, rs, rs, rs, rs],
        out_specs=[wide, rs, rs, rs, rs, pl.BlockSpec((None, 1, 4 * D), lambda i: (i, 0, 0))],
        out_shape=[jax.ShapeDtypeStruct((T, 4 * D), BF16)] + [jax.ShapeDtypeStruct((T, D), BF16)] * 4
                  + [jax.ShapeDtypeStruct((nblk, 1, 4 * D), F32)],
        name=name, compiler_params=_cp(("parallel",)),
    )(dmg, gates, *ts)


def _ffn_up(a, wg, wu, name, tm_pref=384):
    T, K = a.shape
    nq, _, ns = wg.shape
    tm = _tile(T, tm_pref, SUBLANE * 2)

    def body(a_ref, g_ref, u_ref, gg_ref, uu_ref, act_ref):
        av = a_ref[...]
        g = jnp.dot(av, g_ref[...], preferred_element_type=F32)
        u = jnp.dot(av, u_ref[...], preferred_element_type=F32)
        gg_ref[...] = g.astype(BF16)
        uu_ref[...] = u.astype(BF16)
        act_ref[...] = (g * jax.nn.sigmoid(g) * u).astype(BF16)

    ws = pl.BlockSpec((None, K, ns), lambda q, i: (q, 0, 0))
    os = pl.BlockSpec((tm, ns), lambda q, i: (i, q))
    return pl.pallas_call(
        body, grid=(nq, T // tm),
        in_specs=[pl.BlockSpec((tm, K), lambda q, i: (i, 0)), ws, ws], out_specs=[os, os, os],
        out_shape=[jax.ShapeDtypeStruct((T, nq * ns), BF16)] * 3,
        name=name, compiler_params=_cp(("parallel", "parallel"), VMEM_BIG),
    )(a, wg, wu)


def _ffn_down_bwd(dff, wd, gg, uu, name, tm_pref=384):
    T, N = dff.shape
    nq, kc, _ = wd.shape
    tm = _tile(T, tm_pref, SUBLANE * 2)

    def body(d_ref, w_ref, g_ref, u_ref, dg_ref, du_ref):
        da = lax.dot_general(d_ref[...], w_ref[...], (((1,), (1,)), ((), ())), preferred_element_type=F32)
        g = g_ref[...].astype(F32)
        sig = jax.nn.sigmoid(g)
        dg_ref[...] = (da * u_ref[...].astype(F32) * sig * (1.0 + g * (1.0 - sig))).astype(BF16)
        du_ref[...] = (da * g * sig).astype(BF16)

    ts = pl.BlockSpec((tm, kc), lambda q, i: (i, q))
    return pl.pallas_call(
        body, grid=(nq, T // tm),
        in_specs=[pl.BlockSpec((tm, N), lambda q, i: (i, 0)), pl.BlockSpec((None, kc, N), lambda q, i: (q, 0, 0)),
                  ts, ts],
        out_specs=[ts, ts], out_shape=[jax.ShapeDtypeStruct((T, nq * kc), BF16)] * 2,
        name=name, compiler_params=_cp(("parallel", "parallel"), VMEM_BIG),
    )(dff, wd, gg, uu)


def _swap_halves(v):
    lane = lax.broadcasted_iota(jnp.int32, v.shape, 1)
    return jnp.where((lane % 64) < 32, pltpu.roll(v, 96, 1), pltpu.roll(v, 32, 1))


def _rope_tables(n, nc):
    rows = n // GRID_W
    row = jnp.repeat(jnp.arange(rows), GRID_W).astype(F32)
    col = jnp.tile(jnp.arange(GRID_W), rows).astype(F32)
    inv = ROPE_THETA ** (-jnp.arange(0, 64, 2, dtype=F32) / 64)
    ang_r = row[:, None] * inv
    ang_c = col[:, None] * inv
    cos = jnp.concatenate([jnp.cos(ang_r), jnp.cos(ang_r), jnp.cos(ang_c), jnp.cos(ang_c)], axis=-1)
    sin = jnp.concatenate([-jnp.sin(ang_r), jnp.sin(ang_r), -jnp.sin(ang_c), jnp.sin(ang_c)], axis=-1)
    cos = jnp.concatenate([jnp.ones((nc, HEAD_DIM), F32), cos], axis=0)
    sin = jnp.concatenate([jnp.zeros((nc, HEAD_DIM), F32), sin], axis=0)
    return cos, sin


def _qk_fwd(p, cos, sin, qg, kg, name):
    T = p.shape[0]

    def body(p_ref, c_ref, s_ref, qg_ref, kg_ref, q_ref, k_ref, v_ref):
        c, s = c_ref[...], s_ref[...]
        for h in range(N_HEADS + N_KV):
            x = p_ref[:, h * HEAD_DIM:(h + 1) * HEAD_DIM]
            rs = lax.rsqrt(jnp.mean(x * x, axis=-1, keepdims=True) + RMS_EPS)
            gain = qg_ref[...] if h < N_HEADS else kg_ref[...]
            yv = x * rs * gain
            out = (yv * c + _swap_halves(yv) * s).astype(BF16)
            if h < N_HEADS:
                q_ref[:, h * HEAD_DIM:(h + 1) * HEAD_DIM] = out
            else:
                k_ref[:, (h - N_HEADS) * HEAD_DIM:(h - N_HEADS + 1) * HEAD_DIM] = out
        v_ref[...] = p_ref[:, Q_W + KV_W:QKV_W].astype(BF16)

    return pl.pallas_call(
        body, grid=(T // ROW_TM,),
        in_specs=[_row_spec(QKV_W), _row_spec(HEAD_DIM), _row_spec(HEAD_DIM), _vec_spec(HEAD_DIM), _vec_spec(HEAD_DIM)],
        out_specs=[_row_spec(Q_W), _row_spec(KV_W), _row_spec(KV_W)],
        out_shape=[jax.ShapeDtypeStruct((T, Q_W), BF16), jax.ShapeDtypeStruct((T, KV_W), BF16),
                   jax.ShapeDtypeStruct((T, KV_W), BF16)],
        name=name, compiler_params=_cp(("parallel",)),
    )(p, cos, sin, qg, kg)


def _qk_bwd(p, dq, dk, dv, cos, sin, qg, kg, name):
    T = p.shape[0]
    nblk = T // ROW_TM

    def body(p_ref, dq_ref, dk_ref, dv_ref, c_ref, s_ref, qg_ref, kg_ref, dp_ref, part_ref):
        c, s = c_ref[...], s_ref[...]
        dgq = jnp.zeros((1, HEAD_DIM), F32)
        dgk = jnp.zeros((1, HEAD_DIM), F32)
        for h in range(N_HEADS + N_KV):
            x = p_ref[:, h * HEAD_DIM:(h + 1) * HEAD_DIM]
            if h < N_HEADS:
                d = dq_ref[:, h * HEAD_DIM:(h + 1) * HEAD_DIM]
                gain = qg_ref[...]
            else:
                d = dk_ref[:, (h - N_HEADS) * HEAD_DIM:(h - N_HEADS + 1) * HEAD_DIM]
                gain = kg_ref[...]
            dyv = d * c + _swap_halves(d * s)
            rs = lax.rsqrt(jnp.mean(x * x, axis=-1, keepdims=True) + RMS_EPS)
            xn = x * rs
            dgsum = jnp.sum(dyv * xn, axis=0, keepdims=True)
            if h < N_HEADS:
                dgq = dgq + dgsum
            else:
                dgk = dgk + dgsum
            dxg = dyv * gain
            dx = rs * (dxg - xn * jnp.mean(dxg * xn, axis=-1, keepdims=True))
            dp_ref[:, h * HEAD_DIM:(h + 1) * HEAD_DIM] = dx.astype(BF16)
        dp_ref[:, Q_W + KV_W:QKV_W] = dv_ref[...].astype(BF16)
        part_ref[pl.ds(0, 1), :] = dgq
        part_ref[pl.ds(1, 1), :] = dgk
        part_ref[pl.ds(2, SUBLANE - 2), :] = jnp.zeros((SUBLANE - 2, HEAD_DIM), F32)

    return pl.pallas_call(
        body, grid=(nblk,),
        in_specs=[_row_spec(QKV_W), _row_spec(Q_W), _row_spec(KV_W), _row_spec(KV_W),
                  _row_spec(HEAD_DIM), _row_spec(HEAD_DIM), _vec_spec(HEAD_DIM), _vec_spec(HEAD_DIM)],
        out_specs=[_row_spec(QKV_W), _part_spec(HEAD_DIM)],
        out_shape=[jax.ShapeDtypeStruct((T, QKV_W), BF16), jax.ShapeDtypeStruct((nblk, SUBLANE, HEAD_DIM), F32)],
        name=name, compiler_params=_cp(("parallel",)),
    )(p, dq, dk, dv, cos, sin, qg, kg)


def _attn_probs(q, k, qblk, nc, T):
    s = lax.dot_general(q, k, (((1,), (1,)), ((), ())), preferred_element_type=F32) * (HEAD_DIM ** -0.5)
    col = lax.broadcasted_iota(jnp.int32, s.shape, 1)
    limit = jnp.where(qblk < nc // Q_BLOCK, nc, T)
    s = jnp.where(col < limit, s, -1e30)
    e = jnp.exp(s - jnp.max(s, axis=-1, keepdims=True))
    return e / jnp.sum(e, axis=-1, keepdims=True)


def _attn_fwd(q, k, v, nc, name):
    T = q.shape[0]

    def body(q_ref, k_ref, v_ref, o_ref):
        i = pl.program_id(0)
        for h in range(N_HEADS):
            g = h // KV_GROUP
            kk = k_ref[:, g * HEAD_DIM:(g + 1) * HEAD_DIM]
            vv = v_ref[:, g * HEAD_DIM:(g + 1) * HEAD_DIM]
            pr = _attn_probs(q_ref[:, h * HEAD_DIM:(h + 1) * HEAD_DIM], kk, i, nc, T)
            o = jnp.dot(pr.astype(BF16), vv, preferred_element_type=F32)
            o_ref[:, h * HEAD_DIM:(h + 1) * HEAD_DIM] = o.astype(BF16)

    whole = pl.BlockSpec((T, KV_W), lambda i: (0, 0))
    qs = pl.BlockSpec((Q_BLOCK, Q_W), lambda i: (i, 0))
    return pl.pallas_call(
        body, grid=(T // Q_BLOCK,), in_specs=[qs, whole, whole], out_specs=qs,
        out_shape=jax.ShapeDtypeStruct((T, Q_W), BF16),
        name=name, compiler_params=_cp(("parallel",)),
    )(q, k, v)


def _attn_bwd(q, k, v, do, nc, name):
    T = q.shape[0]
    scale = HEAD_DIM ** -0.5

    def body(q_ref, k_ref, v_ref, do_ref, dq_ref, dk_ref, dv_ref):
        i = pl.program_id(0)

        @pl.when(i == 0)
        def _():
            dk_ref[...] = jnp.zeros_like(dk_ref)
            dv_ref[...] = jnp.zeros_like(dv_ref)

        for h in range(N_HEADS):
            g = h // KV_GROUP
            cols = slice(g * HEAD_DIM, (g + 1) * HEAD_DIM)
            hc = slice(h * HEAD_DIM, (h + 1) * HEAD_DIM)
            qh, kk, vv, doh = q_ref[:, hc], k_ref[:, cols], v_ref[:, cols], do_ref[:, hc]
            pr = _attn_probs(qh, kk, i, nc, T)
            dpr = lax.dot_general(doh, vv, (((1,), (1,)), ((), ())), preferred_element_type=F32)
            ds = (pr * (dpr - jnp.sum(pr * dpr, axis=-1, keepdims=True)) * scale).astype(BF16)
            dq_ref[:, hc] = jnp.dot(ds, kk, preferred_element_type=F32)
            dk_ref[:, cols] += lax.dot_general(ds, qh, (((0,), (0,)), ((), ())), preferred_element_type=F32)
            dv_ref[:, cols] += lax.dot_general(pr.astype(BF16), doh, (((0,), (0,)), ((), ())),
                                               preferred_element_type=F32)

    whole = pl.BlockSpec((T, KV_W), lambda i: (0, 0))
    qs = pl.BlockSpec((Q_BLOCK, Q_W), lambda i: (i, 0))
    return pl.pallas_call(
        body, grid=(T // Q_BLOCK,), in_specs=[qs, whole, whole, qs], out_specs=[qs, whole, whole],
        out_shape=[jax.ShapeDtypeStruct((T, Q_W), F32), jax.ShapeDtypeStruct((T, KV_W), F32),
                   jax.ShapeDtypeStruct((T, KV_W), F32)],
        name=name, compiler_params=_cp(("arbitrary",)),
    )(q, k, v, do)


def _shift_rows(z, o, nc):
    T = z.shape[0]
    t = lax.broadcasted_iota(jnp.int32, (T, 1), 0)
    lo = jnp.where(t < nc, 0, nc)
    hi = jnp.where(t < nc, nc, T)
    ok = jnp.logical_and(t + o >= lo, t + o < hi)
    rolled = z if o == 0 else pltpu.roll(z, (-o) % T, 0)
    return jnp.where(ok, rolled, 0.0), ok


def _pool_window(w):
    left = w // 2
    return -left, w - 1 - left


def _pool_d(z, w, nc):
    o0, o1 = _pool_window(w)
    tot = jnp.zeros_like(z)
    cnt = jnp.zeros((z.shape[0], 1), F32)
    for o in range(o0, o1 + 1):
        sh, ok = _shift_rows(z, o, nc)
        tot = tot + sh
        cnt = cnt + ok.astype(F32)
    return tot / cnt - z, cnt


def _pool_fwd(p, pool_w, pool_scale, nc, name):
    T = p.shape[0]

    def body(z_ref, w_ref, s_ref, o_ref):
        for g, w in enumerate(POOL_WINDOWS):
            cs = slice(g * GC, (g + 1) * GC)
            d, _ = _pool_d(z_ref[:, cs], w, nc)
            yv = jnp.dot(d.astype(BF16), w_ref[g].astype(BF16), preferred_element_type=F32)
            o_ref[:, cs] = (yv * s_ref[:, cs]).astype(BF16)

    return pl.pallas_call(
        body, grid=(1,),
        in_specs=[pl.BlockSpec((T, BR_W), lambda i: (0, OFF_POOL // BR_W)),
                  pl.BlockSpec((N_GROUP, GC, GC), lambda i: (0, 0, 0)), _vec_spec(BR_W)],
        out_specs=pl.BlockSpec((T, BR_W), lambda i: (0, 0)),
        out_shape=jax.ShapeDtypeStruct((T, BR_W), BF16),
        name=name, compiler_params=_cp(("arbitrary",), VMEM_BIG),
    )(p, pool_w, pool_scale)


def _pool_bwd(p, dy, pool_w, pool_scale, nc, name):
    T = p.shape[0]

    def body(z_ref, dy_ref, w_ref, s_ref, dz_ref, dw_ref, ds_ref):
        for g, w in enumerate(POOL_WINDOWS):
            cs = slice(g * GC, (g + 1) * GC)
            d, cnt = _pool_d(z_ref[:, cs], w, nc)
            db = d.astype(BF16)
            wb = w_ref[g].astype(BF16)
            dyv = dy_ref[:, cs]
            e = (dyv * s_ref[:, cs]).astype(BF16)
            dw_ref[g] = lax.dot_general(db, e, (((0,), (0,)), ((), ())), preferred_element_type=F32)
            yraw = jnp.dot(db, wb, preferred_element_type=F32)
            ds_ref[:, cs] = jnp.sum(dyv * yraw, axis=0, keepdims=True)
            dd = lax.dot_general(e, wb, (((1,), (1,)), ((), ())), preferred_element_type=F32)
            ec = dd / cnt
            o0, o1 = _pool_window(w)
            tot = jnp.zeros_like(dd)
            for o in range(o0, o1 + 1):
                tot = tot + _shift_rows(ec, -o, nc)[0]
            dz_ref[:, cs] = (tot - dd).astype(BF16)

    return pl.pallas_call(
        body, grid=(1,),
        in_specs=[pl.BlockSpec((T, BR_W), lambda i: (0, OFF_POOL // BR_W)),
                  pl.BlockSpec((T, BR_W), lambda i: (0, 0)),
                  pl.BlockSpec((N_GROUP, GC, GC), lambda i: (0, 0, 0)), _vec_spec(BR_W)],
        out_specs=[pl.BlockSpec((T, BR_W), lambda i: (0, 0)),
                   pl.BlockSpec((N_GROUP, GC, GC), lambda i: (0, 0, 0)), _vec_spec(BR_W)],
        out_shape=[jax.ShapeDtypeStruct((T, BR_W), BF16), jax.ShapeDtypeStruct((N_GROUP, GC, GC), F32),
                   jax.ShapeDtypeStruct((1, BR_W), F32)],
        name=name, compiler_params=_cp(("arbitrary",), VMEM_BIG),
    )(p, dy, pool_w, pool_scale)


_GELU_C = math.sqrt(2.0 / math.pi)


def _gelu(x):
    return 0.5 * x * (1.0 + jnp.tanh(_GELU_C * (x + 0.044715 * x * x * x)))


def _gelu_grad(x):
    th = jnp.tanh(_GELU_C * (x + 0.044715 * x * x * x))
    return 0.5 * (1.0 + th) + 0.5 * x * (1.0 - th * th) * _GELU_C * (1.0 + 3.0 * 0.044715 * x * x)


def _sgu_fwd(p, ln_g, ln_b, w_s, b_st, name):
    T = p.shape[0]

    def body(pu_ref, pv_ref, g_ref, b_ref, w_ref, bs_ref, o_ref):
        u = _gelu(pu_ref[...])
        vhat, _ = _ln_stats(_gelu(pv_ref[...]))
        vn = (vhat * g_ref[...] + b_ref[...]).astype(BF16)
        for g in range(N_GROUP):
            cs = slice(g * GC, (g + 1) * GC)
            s = jnp.dot(w_ref[g].astype(BF16), vn[:, cs], preferred_element_type=F32) + bs_ref[:, g:g + 1]
            o_ref[:, cs] = (u[:, cs] * s).astype(BF16)

    chunk = lambda off: pl.BlockSpec((GC, BR_W), lambda i: (i, off // BR_W))
    return pl.pallas_call(
        body, grid=(T // GC,),
        in_specs=[chunk(OFF_U), chunk(OFF_VG), _vec_spec(BR_W), _vec_spec(BR_W),
                  pl.BlockSpec((N_GROUP, GC, GC), lambda i: (0, 0, 0)),
                  pl.BlockSpec((GC, N_GROUP), lambda i: (0, 0))],
        out_specs=pl.BlockSpec((GC, BR_W), lambda i: (i, 0)),
        out_shape=jax.ShapeDtypeStruct((T, BR_W), BF16),
        name=name, compiler_params=_cp(("parallel",)),
    )(p, p, ln_g, ln_b, w_s, b_st)


def _sgu_bwd(p, dy, ln_g, ln_b, w_s, b_st, name):
    T = p.shape[0]

    def body(pu_ref, pv_ref, dy_ref, g_ref, b_ref, w_ref, bs_ref, dp_ref, dw_ref, dsacc_ref, dln_ref):
        i = pl.program_id(0)

        @pl.when(i == 0)
        def _():
            dw_ref[...] = jnp.zeros_like(dw_ref)
            dsacc_ref[...] = jnp.zeros_like(dsacc_ref)
            dln_ref[...] = jnp.zeros_like(dln_ref)

        pu, pv, dyv = pu_ref[...], pv_ref[...], dy_ref[...]
        u = _gelu(pu)
        vhat, rstd = _ln_stats(_gelu(pv))
        vn = (vhat * g_ref[...] + b_ref[...]).astype(BF16)
        ds = dyv * u
        dsb = ds.astype(BF16)
        dsacc_ref[...] += ds
        dvn_parts = []
        for g in range(N_GROUP):
            cs = slice(g * GC, (g + 1) * GC)
            wb = w_ref[g].astype(BF16)
            s = jnp.dot(wb, vn[:, cs], preferred_element_type=F32) + bs_ref[:, g:g + 1]
            dp_ref[:, cs] = (dyv[:, cs] * s * _gelu_grad(pu[:, cs])).astype(BF16)
            dvn_parts.append(lax.dot_general(wb, dsb[:, cs], (((0,), (0,)), ((), ())),
                                             preferred_element_type=F32))
            dw_ref[g] += lax.dot_general(dsb[:, cs], vn[:, cs], (((1,), (1,)), ((), ())),
                                         preferred_element_type=F32)
        dvn = jnp.concatenate(dvn_parts, axis=-1)
        dln_ref[pl.ds(0, 1), :] += jnp.sum(dvn * vhat, axis=0, keepdims=True)
        dln_ref[pl.ds(1, 1), :] += jnp.sum(dvn, axis=0, keepdims=True)
        dvhat = dvn * g_ref[...]
        m1 = jnp.mean(dvhat, axis=-1, keepdims=True)
        m2 = jnp.mean(dvhat * vhat, axis=-1, keepdims=True)
        dv = rstd * (dvhat - m1 - vhat * m2)
        dp_ref[:, BR_W:2 * BR_W] = (dv * _gelu_grad(pv)).astype(BF16)

    chunk = lambda off: pl.BlockSpec((GC, BR_W), lambda i: (i, off // BR_W))
    return pl.pallas_call(
        body, grid=(T // GC,),
        in_specs=[chunk(OFF_U), chunk(OFF_VG), pl.BlockSpec((GC, BR_W), lambda i: (i, 0)),
                  _vec_spec(BR_W), _vec_spec(BR_W),
                  pl.BlockSpec((N_GROUP, GC, GC), lambda i: (0, 0, 0)),
                  pl.BlockSpec((GC, N_GROUP), lambda i: (0, 0))],
        out_specs=[pl.BlockSpec((GC, 2 * BR_W), lambda i: (i, 0)),
                   pl.BlockSpec((N_GROUP, GC, GC), lambda i: (0, 0, 0)),
                   pl.BlockSpec((GC, BR_W), lambda i: (0, 0)),
                   pl.BlockSpec((SUBLANE, BR_W), lambda i: (0, 0))],
        out_shape=[jax.ShapeDtypeStruct((T, 2 * BR_W), BF16), jax.ShapeDtypeStruct((N_GROUP, GC, GC), F32),
                   jax.ShapeDtypeStruct((GC, BR_W), F32), jax.ShapeDtypeStruct((SUBLANE, BR_W), F32)],
        name=name, compiler_params=_cp(("arbitrary",)),
    )(p, p, dy, ln_g, ln_b, w_s, b_st)


def _conv_fwd(p, conv_w, nc, name):
    T = p.shape[0]

    def body(cb_ref, cc_ref, cx_ref, w_ref, o_ref):
        zz = cc_ref[...] * cx_ref[...]
        conv = (w_ref[0:1, :] * _shift_rows(zz, -1, nc)[0] + w_ref[1:2, :] * zz
                + w_ref[2:3, :] * _shift_rows(zz, 1, nc)[0])
        o_ref[...] = (cb_ref[...] * conv).astype(BF16)

    col = lambda off: pl.BlockSpec((T, GC), lambda j: (0, off // GC + j))
    return pl.pallas_call(
        body, grid=(N_GROUP,),
        in_specs=[col(OFF_CB), col(OFF_CC), col(OFF_CX), pl.BlockSpec((3, GC), lambda j: (0, j))],
        out_specs=pl.BlockSpec((T, GC), lambda j: (0, j)),
        out_shape=jax.ShapeDtypeStruct((T, BR_W), BF16),
        name=name, compiler_params=_cp(("parallel",)),
    )(p, p, p, conv_w)


def _conv_bwd(p, dy, conv_w, nc, name):
    T = p.shape[0]

    def body(cb_ref, cc_ref, cx_ref, dy_ref, w_ref, dp_ref, dw_ref):
        cc, cx, dyv = cc_ref[...], cx_ref[...], dy_ref[...]
        zz = cc * cx
        zm, zp = _shift_rows(zz, -1, nc)[0], _shift_rows(zz, 1, nc)[0]
        conv = w_ref[0:1, :] * zm + w_ref[1:2, :] * zz + w_ref[2:3, :] * zp
        dconv = dyv * cb_ref[...]
        dw_ref[0:1, :] = jnp.sum(dconv * zm, axis=0, keepdims=True)
        dw_ref[1:2, :] = jnp.sum(dconv * zz, axis=0, keepdims=True)
        dw_ref[2:3, :] = jnp.sum(dconv * zp, axis=0, keepdims=True)
        dzz = (w_ref[0:1, :] * _shift_rows(dconv, 1, nc)[0] + w_ref[1:2, :] * dconv
               + w_ref[2:3, :] * _shift_rows(dconv, -1, nc)[0])
        dp_ref[0] = (dyv * conv).astype(BF16)
        dp_ref[1] = (dzz * cx).astype(BF16)
        dp_ref[2] = (dzz * cc).astype(BF16)

    col = lambda off: pl.BlockSpec((T, GC), lambda j: (0, off // GC + j))
    return pl.pallas_call(
        body, grid=(N_GROUP,),
        in_specs=[col(OFF_CB), col(OFF_CC), col(OFF_CX), pl.BlockSpec((T, GC), lambda j: (0, j)),
                  pl.BlockSpec((3, GC), lambda j: (0, j))],
        out_specs=[pl.BlockSpec((3, T, GC), lambda j: (0, 0, j)), pl.BlockSpec((3, GC), lambda j: (0, j))],
        out_shape=[jax.ShapeDtypeStruct((3, T, BR_W), BF16), jax.ShapeDtypeStruct((3, BR_W), F32)],
        name=name, compiler_params=_cp(("parallel",)),
    )(p, p, p, dy, conv_w)


def _rows_tile(rows, cols, n_arrays):
    budget = 24 * 1024 * 1024 // (2 * 4 * n_arrays * cols)
    return _tile(rows, max(SUBLANE * 2, min(budget, 1024)), SUBLANE * 2) if rows % (SUBLANE * 2) == 0 else rows


def _cast_into(chip, w, l, name, after=()):
    _, K, cols = w.shape
    tr = _rows_tile(K, cols, 2)

    def body(q_ref, w_ref, *rest):
        rest[-1][...] = w_ref[...].astype(BF16)

    return pl.pallas_call(
        body,
        grid_spec=pltpu.PrefetchScalarGridSpec(
            num_scalar_prefetch=1, grid=(K // tr,),
            in_specs=[pl.BlockSpec((None, tr, cols), lambda i, q: (l, i, 0))] + [_ANY] * len(after),
            out_specs=pl.BlockSpec((None, tr, cols), lambda i, q: (q[0], i, 0))),
        out_shape=jax.ShapeDtypeStruct((N_CHIP, K, cols), BF16),
        name=name, compiler_params=_cp(("parallel",)),
    )(chip, w, *after)


def _adamw(w, g, m, v, name):
    shape = w.shape
    cols = shape[-1]
    rows = w.size // cols
    tr = _rows_tile(rows, cols, 7)
    c1 = 1.0 - ADAM_B1 ** ADAM_STEP
    c2 = 1.0 - ADAM_B2 ** ADAM_STEP

    def body(w_ref, g_ref, m_ref, v_ref, d_ref, mo_ref, vo_ref):
        gv = g_ref[...]
        mn = ADAM_B1 * m_ref[...] + (1.0 - ADAM_B1) * gv
        vn = ADAM_B2 * v_ref[...] + (1.0 - ADAM_B2) * (gv * gv)
        mo_ref[...] = mn
        vo_ref[...] = vn
        d_ref[...] = -ADAM_LR * ((mn / c1) / (jnp.sqrt(vn / c2) + ADAM_EPS) + ADAM_WD * w_ref[...])

    rs = pl.BlockSpec((tr, cols), lambda i: (i, 0))
    outs = pl.pallas_call(
        body, grid=(rows // tr,), in_specs=[rs] * 4, out_specs=[rs] * 3,
        out_shape=[jax.ShapeDtypeStruct((rows, cols), F32)] * 3,
        name=name, compiler_params=_cp(("parallel",)),
    )(*[a.reshape(rows, cols) for a in (w, g, m, v)])
    return [o.reshape(shape) for o in outs]


def _adamw_layer(w, g, m, v, l, prev, after, name):
    L, K, cols = w.shape
    tr = _rows_tile(K, cols, 7)
    c1 = 1.0 - ADAM_B1 ** ADAM_STEP
    c2 = 1.0 - ADAM_B2 ** ADAM_STEP
    nprev = 0 if prev is None else 3

    def body(*refs):
        w_ref, g_ref, m_ref, v_ref = refs[:4]
        d_ref, mo_ref, vo_ref = refs[-3:]
        gv = g_ref[...]
        mn = ADAM_B1 * m_ref[...] + (1.0 - ADAM_B1) * gv
        vn = ADAM_B2 * v_ref[...] + (1.0 - ADAM_B2) * (gv * gv)
        mo_ref[...] = mn
        vo_ref[...] = vn
        d_ref[...] = -ADAM_LR * ((mn / c1) / (jnp.sqrt(vn / c2) + ADAM_EPS) + ADAM_WD * w_ref[...])

    rs = pl.BlockSpec((None, tr, cols), lambda i: (l, i, 0))
    extra = list(prev or ()) + list(after)
    return pl.pallas_call(
        body, grid=(K // tr,), in_specs=[rs] * 4 + [_ANY] * len(extra), out_specs=[rs] * 3,
        out_shape=[jax.ShapeDtypeStruct((L, K, cols), F32)] * 3,
        input_output_aliases={4 + k: k for k in range(nprev)},
        name=name, compiler_params=_cp(("parallel",)),
    )(w, g, m, v, *extra)


def _pair_sum(core, dw, recv, name):
    nq, half, cols = recv.shape
    tr = _rows_tile(half, cols, 3)
    nb = half // tr

    def body(c_ref, a_ref, r_ref, o_ref):
        o_ref[...] = (a_ref[...].astype(F32) + r_ref[...].astype(F32)).astype(BF16)

    rs = pl.BlockSpec((None, tr, cols), lambda q, i, c: (q, i, 0))
    return pl.pallas_call(
        body,
        grid_spec=pltpu.PrefetchScalarGridSpec(
            num_scalar_prefetch=1, grid=(nq, nb),
            in_specs=[pl.BlockSpec((None, tr, cols), lambda q, i, c: (q, c[0] * nb + i, 0)), rs],
            out_specs=rs),
        out_shape=jax.ShapeDtypeStruct((nq, half, cols), BF16),
        name=name, compiler_params=_cp(("parallel", "parallel")),
    )(core, dw, recv)


def _chip_sum(chip, core, psum, land, l, n_layers, prev, name):
    _, half, cols = psum.shape
    tr = _rows_tile(half, cols, 5)
    nb = half // tr

    def body(*refs):
        p_ref, r_ref, o_ref = refs[2], refs[3], refs[-1]
        o_ref[...] = (p_ref[...].astype(F32) + r_ref[0].astype(F32) + r_ref[1].astype(F32)
                      + r_ref[2].astype(F32))

    extra = [] if prev is None else [prev]
    return pl.pallas_call(
        body,
        grid_spec=pltpu.PrefetchScalarGridSpec(
            num_scalar_prefetch=2, grid=(nb,),
            in_specs=[pl.BlockSpec((None, tr, cols), lambda i, q, c: (q[0], i, 0)),
                      pl.BlockSpec((3, tr, cols), lambda i, q, c: (0, i, 0))] + [_ANY] * len(extra),
            out_specs=pl.BlockSpec((None, tr, cols), lambda i, q, c: (l, c[0] * nb + i, 0))),
        out_shape=jax.ShapeDtypeStruct((n_layers, 2 * half, cols), F32),
        input_output_aliases={4: 0} if prev is not None else {},
        name=name, compiler_params=_cp(("parallel",)),
    )(chip, core, psum, land, *extra)


def _slab_sum(gathered, name, after=()):
    n, rows, cols = gathered.shape
    tr = _tile(rows, 512, SUBLANE)

    def body(g_ref, *rest):
        acc = g_ref[0]
        for d in range(1, n):
            acc = acc + g_ref[d]
        rest[-1][...] = acc

    return pl.pallas_call(
        body, grid=(rows // tr,),
        in_specs=[pl.BlockSpec((n, tr, cols), lambda i: (0, i, 0))] + [_ANY] * len(after),
        out_specs=pl.BlockSpec((tr, cols), lambda i: (i, 0)),
        out_shape=jax.ShapeDtypeStruct((rows, cols), F32),
        name=name, compiler_params=_cp(("parallel",)),
    )(gathered, *after)


def _silu(x):
    return x * jax.nn.sigmoid(x)


def _ada_fwd(cvec, w_ada, b_cols, name):
    L, D, ns = w_ada.shape
    tn = _tile(ns, 768)

    def body(c_ref, w_ref, b_ref, o_ref):
        s = _silu(c_ref[...]).astype(BF16)
        o_ref[...] = jnp.dot(s, w_ref[...].astype(BF16), preferred_element_type=F32) + b_ref[...]

    return pl.pallas_call(
        body, grid=(L, ns // tn),
        in_specs=[pl.BlockSpec((16, D), lambda l, j: (0, 0)),
                  pl.BlockSpec((None, D, tn), lambda l, j: (l, 0, j)),
                  pl.BlockSpec((None, 1, tn), lambda l, j: (l, 0, j))],
        out_specs=pl.BlockSpec((None, 16, tn), lambda l, j: (l, 0, j)),
        out_shape=jax.ShapeDtypeStruct((L, 16, ns), F32),
        name=name, compiler_params=_cp(("parallel", "parallel")),
    )(cvec, w_ada, b_cols)


def _ada_bwd(cvec, dmod, w_ada, name):
    L, D, ns = w_ada.shape
    tn = _tile(ns, 768)

    def body(c_ref, d_ref, w_ref, gw_ref, ds_ref):
        first = jnp.logical_and(pl.program_id(0) == 0, pl.program_id(1) == 0)

        @pl.when(first)
        def _():
            ds_ref[...] = jnp.zeros_like(ds_ref)

        s = _silu(c_ref[...]).astype(BF16)
        db = d_ref[...].astype(BF16)
        gw_ref[...] = lax.dot_general(s, db, (((0,), (0,)), ((), ())), preferred_element_type=F32)
        ds_ref[...] += lax.dot_general(db, w_ref[...].astype(BF16), (((1,), (1,)), ((), ())),
                                       preferred_element_type=F32)

    return pl.pallas_call(
        body, grid=(L, ns // tn),
        in_specs=[pl.BlockSpec((16, D), lambda l, j: (0, 0)),
                  pl.BlockSpec((None, 16, tn), lambda l, j: (l, 0, j)),
                  pl.BlockSpec((None, D, tn), lambda l, j: (l, 0, j))],
        out_specs=[pl.BlockSpec((None, D, tn), lambda l, j: (l, 0, j)),
                   pl.BlockSpec((16, D), lambda l, j: (0, 0))],
        out_shape=[jax.ShapeDtypeStruct((L, D, ns), F32), jax.ShapeDtypeStruct((16, D), F32)],
        name=name, compiler_params=_cp(("arbitrary", "arbitrary")),
    )(cvec, dmod, w_ada)


def _cctx_grad(gathered, c_ctx, name):
    rows = c_ctx.shape[0]

    def body(g_ref, c_ref, o_ref):
        ds = g_ref[0] + g_ref[2] + g_ref[4] + g_ref[6]
        cv = c_ref[...]
        sig = jax.nn.sigmoid(cv)
        o_ref[...] = ds * sig * (1.0 + cv * (1.0 - sig))

    return pl.pallas_call(
        body, out_shape=jax.ShapeDtypeStruct((rows, LANE), F32), name=name,
    )(gathered, c_ctx)


def _place():
    x, y, c = lax.axis_index("x"), lax.axis_index("y"), lax.axis_index("c")
    chips = [(1 - x, y), (x, 1 - y), (1 - x, 1 - y)]
    return x, y, c, chips


def _all_gather_small(slab, name, after=()):
    rows, cols = slab.shape

    def body(x_ref, *rest):
        out_ref, send_sems, recv_sems, local_sem = rest[len(after):]
        x, y, c, chips = _place()
        me, sibling = (x, y, c), (x, y, 1 - c)

        def blk(px, py, pc):
            return out_ref.at[4 * px + 2 * py + pc]

        def copy(k, block, to, src=None):
            return pltpu.make_async_remote_copy(
                src_ref=blk(*block) if src is None else src, dst_ref=blk(*block),
                send_sem=send_sems.at[k], recv_sem=recv_sems.at[k], device_id=to, device_id_type=MESH)

        mine = pltpu.make_async_copy(x_ref, blk(*me), local_sem)
        mine.start()
        first = [copy(0, me, sibling, src=x_ref)]
        first += [copy(1 + j, me, (*chip, c), src=x_ref) for j, chip in enumerate(chips)]
        for cp in first:
            cp.start()
        passed = [copy(4 + j, (*chip, c), sibling) for j, chip in enumerate(chips)]
        for j, chip in enumerate(chips):
            copy(1 + j, (*chip, c), me).wait_recv()
            passed[j].start()
        copy(0, sibling, me).wait_recv()
        for j, chip in enumerate(chips):
            copy(4 + j, (*chip, 1 - c), me).wait_recv()
        for cp in first + passed:
            cp.wait_send()
        mine.wait()

    return pl.pallas_call(
        body, out_shape=jax.ShapeDtypeStruct((N_DEV, rows, cols), slab.dtype),
        in_specs=[pl.BlockSpec(memory_space=pltpu.VMEM)] + [_ANY] * len(after),
        out_specs=pl.BlockSpec(memory_space=pltpu.VMEM),
        scratch_shapes=[pltpu.SemaphoreType.DMA((7,)), pltpu.SemaphoreType.DMA((7,)), pltpu.SemaphoreType.DMA],
        name=name, compiler_params=pltpu.CompilerParams(vmem_limit_bytes=VMEM_MID),
    )(slab, *after)


def _hbm(a):
    return pltpu.with_memory_space_constraint(a, pltpu.HBM)


def _half_rows(ref, q, c):
    half = ref.shape[1] // 2
    return ref.at[q, pl.ds(c * half, half)]


def _ici_copy(src, dst, send_sems, recv_sems, k, to):
    return pltpu.make_async_remote_copy(src_ref=src, dst_ref=dst, send_sem=send_sems.at[k], recv_sem=recv_sems.at[k],
                                        device_id=to, device_id_type=MESH)


def _gather_start(bufs, sizes, name):
    n = len(bufs)
    ng = len(sizes)

    def body(*refs):
        ins = refs[:n]
        sems = refs[n:n + 2 * ng]
        token = refs[-1]
        x, y, c, chips = _place()
        a = 0
        for g, sz in enumerate(sizes):
            for i in range(sz):
                part = _half_rows(ins[a], 2 * x + y, c)
                for j, chip in enumerate(chips):
                    _ici_copy(part, part, sems[2 * g], sems[2 * g + 1], 3 * i + j, (*chip, c)).start()
                a += 1
        token[...] = jnp.zeros_like(token)

    sem_shapes = []
    for sz in sizes:
        sem_shapes += [pltpu.SemaphoreType.DMA((3 * sz,)), pltpu.SemaphoreType.DMA((3 * sz,))]
    outs = pl.pallas_call(
        body, name=name,
        out_shape=tuple(sem_shapes) + tuple(pltpu.HBM(b.shape, b.dtype) for b in bufs)
                  + (jax.ShapeDtypeStruct((SUBLANE, LANE), F32),),
        in_specs=[_HBM] * n,
        out_specs=tuple([_SEM] * (2 * ng) + [_HBM] * n + [pl.BlockSpec(memory_space=pltpu.VMEM)]),
        input_output_aliases={i: 2 * ng + i for i in range(n)},
        compiler_params=pltpu.CompilerParams(has_side_effects=_EFFECT),
    )(*[_hbm(b) for b in bufs])
    sems = [(outs[2 * g], outs[2 * g + 1]) for g in range(ng)]
    return sems, list(outs[2 * ng:2 * ng + n]), outs[-1]


def _gather_wait(bufs, sems, after, name):
    n = len(bufs)
    na = len(after)

    def body(*refs):
        ins = refs[:n]
        send_sems, recv_sems = refs[n], refs[n + 1]
        x, y, c, chips = _place()
        for i in range(n):
            sent = _half_rows(ins[i], 2 * x + y, c)
            for j, (cx, cy) in enumerate(chips):
                cp = _ici_copy(sent, _half_rows(ins[i], 2 * cx + cy, c), send_sems, recv_sems, 3 * i + j, (cx, cy, c))
                cp.wait_send()
                cp.wait_recv()

    outs = pl.pallas_call(
        body, name=name,
        out_shape=tuple(pltpu.HBM(b.shape, b.dtype) for b in bufs),
        in_specs=[_HBM] * n + [_SEM, _SEM] + [_ANY] * na,
        out_specs=tuple([_HBM] * n),
        input_output_aliases={i: i for i in range(n)},
        compiler_params=pltpu.CompilerParams(has_side_effects=_EFFECT),
    )(*bufs, sems[0], sems[1], *after)
    return list(outs)


def _exchange_halves(bufs, name):
    n = len(bufs)

    def body(*refs):
        ins = refs[:n]
        send_sems, recv_sems = refs[-2:]
        x, y, c, chips = _place()
        sends = []
        for i in range(n):
            for j, (cx, cy) in enumerate(chips):
                part = _half_rows(ins[i], 2 * cx + cy, c)
                cp = _ici_copy(part, part, send_sems, recv_sems, 3 * i + j, (x, y, 1 - c))
                cp.start()
                sends.append(cp)
        for i in range(n):
            for j, (cx, cy) in enumerate(chips):
                part = _half_rows(ins[i], 2 * cx + cy, 1 - c)
                _ici_copy(part, part, send_sems, recv_sems, 3 * i + j, (x, y, 1 - c)).wait_recv()
        for cp in sends:
            cp.wait_send()

    outs = pl.pallas_call(
        body, name=name,
        out_shape=[jax.ShapeDtypeStruct(b.shape, b.dtype) for b in bufs],
        in_specs=[_ANY] * n, out_specs=[_ANY] * n,
        input_output_aliases={i: i for i in range(n)},
        scratch_shapes=[pltpu.SemaphoreType.DMA((3 * n,)), pltpu.SemaphoreType.DMA((3 * n,))],
    )(*bufs)
    return list(outs)


def _other_half(ref, c):
    half = ref.shape[1] // 2
    return ref.at[:, pl.ds((1 - c) * half, half)]


def _prereduce_plan(src, land, x, y, c, chips):
    return [(_other_half(src, c), land, (x, y, 1 - c))]


def _prereduce_land(d):
    return (d.shape[0], d.shape[1] // 2, d.shape[2])


def _scatter_plan(src, land, x, y, c, chips):
    return [(src.at[2 * cx + cy], land.at[j], (cx, cy, c)) for j, (cx, cy) in enumerate(chips)]


def _scatter_land(p):
    return (3,) + p.shape[1:]


_COPIES_PER_SOURCE = {_prereduce_plan: 1, _scatter_plan: 3}


def _split_start(srcs, plan, land_shape, name):
    n = len(srcs)
    land_shapes = [land_shape(s) for s in srcs]
    per = _COPIES_PER_SOURCE[plan]

    def body(*refs):
        src_refs, lands = refs[:n], refs[n:2 * n]
        send_sems, recv_sems = refs[2 * n], refs[2 * n + 1]
        token = refs[-1]
        x, y, c, chips = _place()
        for i in range(n):
            for j, (s, d, to) in enumerate(plan(src_refs[i], lands[i], x, y, c, chips)):
                _ici_copy(s, d, send_sems, recv_sems, per * i + j, to).start()
        token[...] = jnp.zeros_like(token)

    outs = pl.pallas_call(
        body, name=name,
        out_shape=(pltpu.SemaphoreType.DMA((per * n,)), pltpu.SemaphoreType.DMA((per * n,)))
                  + tuple(pltpu.HBM(p.shape, p.dtype) for p in srcs)
                  + tuple(pltpu.HBM(s, BF16) for s in land_shapes)
                  + (jax.ShapeDtypeStruct((SUBLANE, LANE), F32),),
        in_specs=[_HBM] * (2 * n),
        out_specs=tuple([_SEM, _SEM] + [_HBM] * (2 * n) + [pl.BlockSpec(memory_space=pltpu.VMEM)]),
        input_output_aliases={i: 2 + i for i in range(2 * n)},
        compiler_params=pltpu.CompilerParams(has_side_effects=_EFFECT),
    )(*[_hbm(p) for p in srcs], *[_hbm(lax.empty(s, BF16)) for s in land_shapes])
    return (outs[0], outs[1]), list(outs[2:2 + n]), list(outs[2 + n:2 + 2 * n]), outs[-1]


def _split_wait(psums, lands, sems, plan, after, name):
    n = len(psums)
    na = len(after)
    per = _COPIES_PER_SOURCE[plan]

    def body(*refs):
        srcs, lnds = refs[:n], refs[n:2 * n]
        send_sems, recv_sems = refs[2 * n], refs[2 * n + 1]
        x, y, c, chips = _place()
        for i in range(n):
            for j, (s, d, to) in enumerate(plan(srcs[i], lnds[i], x, y, c, chips)):
                cp = _ici_copy(s, d, send_sems, recv_sems, per * i + j, to)
                cp.wait_send()
                cp.wait_recv()

    outs = pl.pallas_call(
        body, name=name,
        out_shape=tuple(pltpu.HBM(a.shape, a.dtype) for a in list(psums) + list(lands)),
        in_specs=[_HBM] * (2 * n) + [_SEM, _SEM] + [_ANY] * na,
        out_specs=tuple([_HBM] * (2 * n)),
        input_output_aliases={i: i for i in range(2 * n)},
        compiler_params=pltpu.CompilerParams(has_side_effects=_EFFECT),
    )(*psums, *lands, sems[0], sems[1], *after)
    return list(outs[:n]), list(outs[n:])


def _exchange_grads(grads, l, name):
    n = len(grads)

    def body(*refs):
        ins = refs[:n]
        send_sems, recv_sems = refs[-2:]
        x, y, c, _ = _place()

        def rows(ref, core):
            half = ref.shape[1] // 2
            return ref.at[l, pl.ds(core * half, half)]

        cps = []
        for i in range(n):
            cp = _ici_copy(rows(ins[i], c), rows(ins[i], c), send_sems, recv_sems, i, (x, y, 1 - c))
            cp.start()
            cps.append(cp)
        for i, cp in enumerate(cps):
            cp.wait_send()
            _ici_copy(rows(ins[i], 1 - c), rows(ins[i], 1 - c), send_sems, recv_sems, i, (x, y, 1 - c)).wait_recv()

    outs = pl.pallas_call(
        body, name=name,
        out_shape=[jax.ShapeDtypeStruct(g.shape, g.dtype) for g in grads],
        in_specs=[_ANY] * n, out_specs=[_ANY] * n,
        input_output_aliases={i: i for i in range(n)},
        scratch_shapes=[pltpu.SemaphoreType.DMA((n,)), pltpu.SemaphoreType.DMA((n,))],
    )(*grads)
    return list(outs)


def _pack(arrs):
    flat = jnp.concatenate([a.reshape(-1).astype(F32) for a in arrs])
    pad = (-flat.shape[0]) % (SUBLANE * LANE)
    return jnp.pad(flat, (0, pad)).reshape(-1, LANE)


def _unpack(slab, shapes):
    flat = slab.reshape(-1)
    out, off = [], 0
    for s in shapes:
        n = math.prod(s)
        out.append(flat[off:off + n].reshape(s))
        off += n
    return out


def _split6(v):
    d = v.shape[-1] // 6
    return [v[:, k * d:(k + 1) * d].reshape(2, 1, d) for k in range(6)]


BIG = ("w_in", "w_br_attn", "w_br_pool", "w_br_sgu", "w_br_conv", "w_gate", "w_o", "w_ff_gate", "w_ff_up", "w_ff_down")
KIND = {"w_in": "col", "w_br_attn": "col", "w_br_pool": "col", "w_br_sgu": "col", "w_br_conv": "col",
        "w_gate": "col", "w_o": "row", "w_ff_gate": "col", "w_ff_up": "col", "w_ff_down": "row"}
SMALL = ("c_ctx", "b_ada", "q_norm_g", "k_norm_g", "pool_w", "pool_scale", "sgu_ln_g", "sgu_ln_b", "sgu_w",
         "sgu_b", "conv_w", "b_gate", "ln1_g", "ln1_b", "ln2_g", "ln2_b")
WEIGHTS = ("c_ctx", "w_ada", "b_ada", "w_in", "q_norm_g", "k_norm_g", "pool_w", "pool_scale", "sgu_ln_g", "sgu_ln_b",
           "sgu_w", "sgu_b", "conv_w", "w_br_attn", "w_br_pool", "w_br_sgu", "w_br_conv", "w_gate", "b_gate", "w_o",
           "ln1_g", "ln1_b", "w_ff_gate", "w_ff_up", "w_ff_down", "ln2_g", "ln2_b")


def _step(x, c, ctx, loss_target, W, M, V):
    L = W["w_ada"].shape[0]
    assert L == 2, "core c of a chip carries layer c of the weight traffic"
    N, D = x.shape[1], x.shape[2]
    NC = ctx.shape[1]
    T = NC + N
    FF = W["w_ff_down"].shape[1] * N_CHIP
    assert NC % ROW_TM == 0 and N % ROW_TM == 0 and N % GRID_W == 0 and D % LANE == 0
    ncb = NC // ROW_TM
    nblk = T // ROW_TM
    alpha = (2 * L) ** 0.25
    ax, ay, ac = lax.axis_index("x"), lax.axis_index("y"), lax.axis_index("c")
    chip = 2 * ax + ay
    dev = 2 * chip + ac
    chip_arr = jnp.reshape(chip, (1,)).astype(jnp.int32)
    core_arr = jnp.reshape(ac, (1,)).astype(jnp.int32)
    ns_ada = W["w_ada"].shape[2]
    chip_devs = (0, 2, 4, 6)

    ffn_keys = ("w_ff_gate", "w_ff_up", "w_ff_down")
    mid_keys = ("w_br_attn", "w_br_pool", "w_br_sgu", "w_br_conv", "w_o")
    groups = [grp for l in range(L) for grp in ([("w_in", l)], [("w_gate", l)], [(k, l) for k in mid_keys],
                                                [(k, l) for k in ffn_keys])]
    flying, gsems, gtokens = [None] * len(groups), [None] * len(groups), []
    for n_start, gs in enumerate(((0, 1), (2, 3), (4, 5, 6, 7))):
        members = [kl for g in gs for kl in groups[g]]
        casts = [_cast_into(chip_arr, W[k], l, "cast_%s_l%d" % (k, l), after=gtokens[-1:]) for k, l in members]
        sems, bufs, token = _gather_start(casts, [len(groups[g]) for g in gs], "gather_start_%d" % n_start)
        gtokens.append(token)
        lo = 0
        for g, sem in zip(gs, sems):
            flying[g], gsems[g] = bufs[lo:lo + len(groups[g])], sem
            lo += len(groups[g])
    full = {k: [None] * L for k in BIG}

    def weights_ready(g, after):
        got = _gather_wait(flying[g], gsems[g], after, "gather_wait_%d" % g)
        for (k, l), buf in zip(groups[g], _exchange_halves(got, "exchange_halves_%d" % g)):
            full[k][l] = buf

    conv_shape = W["conv_w"].shape
    g0 = _all_gather_small(_pack([c, W["conv_w"]]), "gather_c", after=gtokens[-1:])
    c_all = g0[:, :D // LANE, :].reshape(N_DEV, D)
    conv_parts = [g0[d].reshape(-1)[D:D + math.prod(conv_shape)].reshape(conv_shape) for d in chip_devs]
    conv_full = jnp.concatenate(conv_parts, axis=-1)
    cvec = jnp.concatenate([c_all, W["c_ctx"][None], jnp.zeros((16 - N_DEV - 1, D), F32)], axis=0)
    b_cols = lax.dynamic_slice_in_dim(W["b_ada"], chip * ns_ada, ns_ada, axis=1).reshape(L, 1, ns_ada)
    mod_part = _ada_fwd(cvec, W["w_ada"], b_cols, "ada_fwd")
    g1 = _all_gather_small(mod_part.reshape(-1, LANE), "gather_mod")
    mod_all = jnp.concatenate([g1[d].reshape(L, 16, ns_ada) for d in chip_devs], axis=-1)
    mod_lat = lax.dynamic_index_in_dim(mod_all, dev, axis=1, keepdims=True)
    mods = jnp.concatenate([mod_all[:, N_DEV:N_DEV + 1], mod_lat], axis=1)

    cos, sin = _rope_tables(N, NC)
    x0 = jnp.concatenate([ctx[0], x[0]], axis=0)

    saved = []
    xin = x0
    h1 = None
    dy_top = loss_parts = None
    for l in range(L):
        sh1, sc1, g1m, sh2, sc2, g2m = _split6(mods[l])
        tag = "_l%d" % l
        if l == 0:
            h1 = _modulate(xin, sc1, sh1, ncb, "modulate" + tag)
        weights_ready(4 * l, [h1])
        p = _mm_nn(h1, full["w_in"], l, "col", name="mm_in" + tag)
        qg, kg = W["q_norm_g"][l][None], W["k_norm_g"][l][None]
        q, k, v = _qk_fwd(p, cos, sin, qg, kg, "qk_fwd" + tag)
        att = _attn_fwd(q, k, v, NC, "attn_fwd" + tag)
        ps = W["pool_scale"][l][None]
        ypool = _pool_fwd(p, W["pool_w"][l], ps, NC, "pool_fwd" + tag)
        lg, lb = W["sgu_ln_g"][l][None], W["sgu_ln_b"][l][None]
        b_st = W["sgu_b"][l].T
        ysgu = _sgu_fwd(p, lg, lb, W["sgu_w"][l], b_st, "sgu_fwd" + tag)
        yconv = _conv_fwd(p, conv_full[l], NC, "conv_fwd" + tag)
        brs = (att, ypool, ysgu, yconv)
        weights_ready(4 * l + 1, list(brs))
        gates = _mm_nn(h1, full["w_gate"], l, "col", name="mm_gate" + tag, bias=W["b_gate"][l][None],
                       act=jax.nn.sigmoid, out_dtype=BF16)
        weights_ready(4 * l + 2, [gates])
        ts = [_mm_nn(b, full[wk], l, "col", name="mm_" + wk + tag, out_dtype=BF16)
              for b, wk in zip(brs, ("w_br_attn", "w_br_pool", "w_br_sgu", "w_br_conv"))]
        mg = _merge_fwd(gates, ts, "merge_fwd" + tag)
        m = _mm_nn(mg, full["w_o"], l, "row", name="mm_o" + tag, tn_pref=1024)
        ln1g, ln1b = W["ln1_g"][l][None], W["ln1_b"][l][None]
        x1, h2 = _resid_ln(xin, m, g1m, ln1g, ln1b, sc2, sh2, ncb, alpha, "resid_ln1" + tag)
        weights_ready(4 * l + 3, [h2])
        gg, uu, act = _ffn_up(h2, full["w_ff_gate"][l], full["w_ff_up"][l], "ffn_up" + tag)
        ff = _mm_nn(act, full["w_ff_down"], l, "row", name="mm_ffd" + tag, tm_pref=384, tn_pref=512)
        ln2g, ln2b = W["ln2_g"][l][None], W["ln2_b"][l][None]
        saved.append(dict(xin=xin, h1=h1, p=p, q=q, k=k, v=v, brs=brs, gates=gates, ts=ts, mg=mg, m=m, x1=x1, h2=h2,
                          gg=gg, uu=uu, act=act, ff=ff))
        if l + 1 < L:
            nsh1, nsc1 = _split6(mods[l + 1])[:2]
            xin, h1 = _resid_ln(x1, ff, g2m, ln2g, ln2b, nsc1, nsh1, ncb, alpha, "resid_ln2" + tag)
        else:
            dy_top, loss_parts = _resid_ln_loss(x1, ff, g2m, ln2g, ln2b, loss_target[0], ncb, alpha,
                                                "resid_ln2_loss" + tag)
    loss_dev = jnp.sum(loss_parts[:, 0, 0])

    def lat_ctx(part_rows):
        return jnp.stack([jnp.sum(part_rows[:ncb], axis=0), jnp.sum(part_rows[ncb:], axis=0)])

    dW = {k: [None] * L for k in BIG}
    small = {k: [None] * L for k in ("q_norm_g", "k_norm_g", "pool_w", "pool_scale", "sgu_ln_g", "sgu_ln_b", "sgu_w",
                                     "sgu_b", "conv_w", "b_gate", "ln1_g", "ln1_b", "ln2_g", "ln2_b")}
    dmods = [None] * L
    mix_keys = tuple(k for k in BIG if k not in ffn_keys)
    grads_big = {k: None for k in BIG}
    adam_out = {k: None for k in BIG}

    def reduce_begin(keys, l, tg):
        sems, src, land, token = _split_start([dW[k][l] for k in keys], _prereduce_plan, _prereduce_land,
                                              "prereduce_start_" + tg)
        return dict(keys=keys, l=l, sems=sems, src=src, land=land, token=token, tg=tg)

    def reduce_scatter(st, after):
        dws, recv = _split_wait(st["src"], st["land"], st["sems"], _prereduce_plan, after,
                                "prereduce_wait_" + st["tg"])
        psums = [_pair_sum(core_arr, d, r, "pair_sum_%s_l%d" % (k, st["l"]))
                 for k, d, r in zip(st["keys"], dws, recv)]
        st["sems"], st["src"], st["land"], st["token"] = _split_start(psums, _scatter_plan, _scatter_land,
                                                                      "scatter_start_" + st["tg"])

    def reduce_finish(st, after):
        src, land = _split_wait(st["src"], st["land"], st["sems"], _scatter_plan, after, "scatter_wait_" + st["tg"])
        for k, p_, r in zip(st["keys"], src, land):
            grads_big[k] = _chip_sum(chip_arr, core_arr, p_, r, st["l"], L, grads_big[k],
                                     "chip_sum_%s_l%d" % (k, st["l"]))
        done = _exchange_grads([grads_big[k] for k in st["keys"]], st["l"], "exchange_grads_" + st["tg"])
        grads_big.update(zip(st["keys"], done))

    def adam(keys, l, after):
        for k in keys:
            adam_out[k] = _adamw_layer(W[k], grads_big[k], M[k], V[k], l, adam_out[k], after,
                                       "adamw_%s_l%d" % (k, l))

    dxa, dhs, sc_prev = dy_top, [], None
    red_l1 = red_ffn = None
    for l in reversed(range(L)):
        s = saved[l]
        sh1, sc1, g1m, sh2, sc2, g2m = _split6(mods[l])
        tag = "_l%d" % l
        ln1g, ln1b = W["ln1_g"][l][None], W["ln1_b"][l][None]
        ln2g, ln2b = W["ln2_g"][l][None], W["ln2_b"][l][None]
        dx1a, dff, part2 = _ln_bwd(dxa, dhs, sc_prev if dhs else sc1, s["x1"], s["ff"], g2m, ln2g, ln2b, ncb, alpha,
                                   "ln2_bwd" + tag, after=[] if red_l1 is None else [red_l1["token"]])
        small["ln2_g"][l] = jnp.sum(part2[:, 0], axis=0)
        small["ln2_b"][l] = jnp.sum(part2[:, 1], axis=0)
        dg2 = lat_ctx(part2[:, 2])
        if dhs:
            dmods[l + 1][1], dmods[l + 1][0] = lat_ctx(part2[:, 3]), lat_ctx(part2[:, 4])
        dgg, duu = _ffn_down_bwd(dff, full["w_ff_down"][l], s["gg"], s["uu"], "ffn_down_bwd" + tag)
        if l == 0:
            reduce_scatter(red_l1, [dgg])
        dW["w_ff_down"][l] = _mm_tn(s["act"], dff, "row", nq=N_CHIP, kdim=FF // N_CHIP, ndim=D, name="tn_ffd" + tag,
                                    tk_pref=1408, tn_pref=512)
        dh2a = _mm_nt(dgg, full["w_ff_gate"], l, "col", name="nt_ffg" + tag)
        dh2b = _mm_nt(duu, full["w_ff_up"], l, "col", name="nt_ffu" + tag)
        dW["w_ff_gate"][l] = _mm_tn(s["h2"], dgg, "col", nq=N_CHIP, kdim=D, ndim=FF // N_CHIP, name="tn_ffg" + tag,
                                    tn_pref=1408)
        dW["w_ff_up"][l] = _mm_tn(s["h2"], duu, "col", nq=N_CHIP, kdim=D, ndim=FF // N_CHIP, name="tn_ffu" + tag,
                                  tn_pref=1408)
        ties = []
        if l == 0:
            red_ffn = reduce_begin(ffn_keys, 0, "l0_ffn")
            ties = [red_l1["token"], red_ffn["token"]]
        dx0a, dm, part1 = _ln_bwd(dx1a, [dh2a, dh2b], sc2, s["xin"], s["m"], g1m, ln1g, ln1b, ncb, alpha,
                                  "ln1_bwd" + tag, after=ties)
        small["ln1_g"][l] = jnp.sum(part1[:, 0], axis=0)
        small["ln1_b"][l] = jnp.sum(part1[:, 1], axis=0)
        dg1 = lat_ctx(part1[:, 2])
        dsc2, dsh2 = lat_ctx(part1[:, 3]), lat_ctx(part1[:, 4])
        dmods[l] = [None, None, dg1, dsh2, dsc2, dg2]
        dmg = _mm_nt(dm, full["w_o"], l, "row", name="nt_o" + tag)
        dW["w_o"][l] = _mm_tn(s["mg"], dm, "row", nq=N_CHIP, kdim=D // N_CHIP, ndim=D, name="tn_o" + tag, tn_pref=1024)
        dpre, dt0, dt1, dt2, dt3, bpart = _merge_bwd(dmg, s["gates"], s["ts"], "merge_bwd" + tag)
        small["b_gate"][l] = jnp.sum(bpart[:, 0], axis=0)
        dh1a = _mm_nt(dpre, full["w_gate"], l, "col", name="nt_gate" + tag)
        dW["w_gate"][l] = _mm_tn(s["h1"], dpre, "col", nq=N_CHIP, kdim=D, ndim=D, name="tn_gate" + tag, tn_pref=1024)
        dbrs = []
        for b, dt, wk, odt in zip(s["brs"], (dt0, dt1, dt2, dt3), ("w_br_attn", "w_br_pool", "w_br_sgu", "w_br_conv"),
                                  (BF16, F32, F32, F32)):
            dbrs.append(_mm_nt(dt, full[wk], l, "col", name="nt_" + wk + tag, out_dtype=odt))
            dW[wk][l] = _mm_tn(b, dt, "col", nq=N_CHIP, kdim=b.shape[1], ndim=D // N_CHIP, name="tn_" + wk + tag)
        qg, kg = W["q_norm_g"][l][None], W["k_norm_g"][l][None]
        dq, dk, dv = _attn_bwd(s["q"], s["k"], s["v"], dbrs[0], NC, "attn_bwd" + tag)
        dp_qkv, qkpart = _qk_bwd(s["p"], dq, dk, dv, cos, sin, qg, kg, "qk_bwd" + tag)
        small["q_norm_g"][l] = jnp.sum(qkpart[:, 0], axis=0)
        small["k_norm_g"][l] = jnp.sum(qkpart[:, 1], axis=0)
        ps = W["pool_scale"][l][None]
        dp_pool, small["pool_w"][l], dps = _pool_bwd(s["p"], dbrs[1], W["pool_w"][l], ps, NC, "pool_bwd" + tag)
        small["pool_scale"][l] = dps[0]
        lg, lb = W["sgu_ln_g"][l][None], W["sgu_ln_b"][l][None]
        dp_sgu, small["sgu_w"][l], dsacc, dln = _sgu_bwd(s["p"], dbrs[2], lg, lb, W["sgu_w"][l], W["sgu_b"][l].T,
                                                         "sgu_bwd" + tag)
        small["sgu_b"][l] = jnp.sum(dsacc.reshape(GC, N_GROUP, GC), axis=-1).T
        small["sgu_ln_g"][l], small["sgu_ln_b"][l] = dln[0], dln[1]
        dp_conv, small["conv_w"][l] = _conv_bwd(s["p"], dbrs[3], conv_full[l], NC, "conv_bwd" + tag)
        dp = jnp.concatenate([dp_qkv, dp_pool, dp_sgu, dp_conv[0], dp_conv[1], dp_conv[2]], axis=-1)
        dh1b = _mm_nt(dp, full["w_in"], l, "col", name="nt_in" + tag)
        dW["w_in"][l] = _mm_tn(s["h1"], dp, "col", nq=N_CHIP, kdim=D, ndim=IN_W // N_CHIP, name="tn_in" + tag,
                               tn_pref=1152)
        dxa, dhs, sc_prev = dx0a, [dh1a, dh1b], sc1
        if l == 1:
            red_l1 = reduce_begin(BIG, 1, "l1")
    dx0, part0 = _mod_bwd(dxa, dhs, sc_prev, x0, ncb, "mod_bwd")
    dmods[0][1], dmods[0][0] = lat_ctx(part0[:, 0]), lat_ctx(part0[:, 1])
    grad_x = dx0[NC:][None]
    dmod = jnp.stack([jnp.concatenate(dmods[l], axis=-1) for l in range(L)])

    small_names = [k for k in SMALL if k not in ("c_ctx", "b_ada")]
    small_arrs = [jnp.stack(small[k]) for k in small_names]
    small_shapes = [a.shape for a in small_arrs]
    slab = _pack([loss_dev.reshape(1), jnp.zeros((LANE - 1,), F32), dmod] + small_arrs)
    g2 = _all_gather_small(slab, "gather_small")
    reduce_scatter(red_ffn, [g2])
    total = _slab_sum(g2, "slab_sum", after=[red_ffn["token"]])

    red_mix = reduce_begin(mix_keys, 0, "l0_mix")
    reduce_finish(red_l1, [red_mix["token"], total])
    reduce_scatter(red_mix, [grads_big[k] for k in BIG])
    adam(BIG, 1, [red_mix["token"]])
    reduce_finish(red_ffn, [red_mix["token"]])
    adam(ffn_keys, 0, [])

    flat_total = total.reshape(-1)
    loss = flat_total[0]
    nmod = L * 2 * 6 * D
    dmod_sum = flat_total[LANE:LANE + nmod].reshape(L, 2, 6 * D)
    small_grads = dict(zip(small_names, _unpack(flat_total[LANE + nmod:], small_shapes)))
    dmod_lat = g2.reshape(N_DEV, -1)[:, LANE:LANE + nmod].reshape(N_DEV, L, 2, 6 * D)[:, :, 1]
    dm16 = jnp.concatenate([jnp.transpose(dmod_lat, (1, 0, 2)), dmod_sum[:, 0:1],
                            jnp.zeros((L, 16 - N_DEV - 1, 6 * D), F32)], axis=1)
    small_grads["b_ada"] = dmod_sum[:, 0] + dmod_sum[:, 1]
    dm16_cols = lax.dynamic_slice_in_dim(dm16, chip * ns_ada, ns_ada, axis=2)
    grad_w_ada, ds16 = _ada_bwd(cvec, dm16_cols, W["w_ada"], "ada_bwd")
    g3 = _all_gather_small(ds16[N_DEV].reshape(-1, LANE), "gather_dsilu")
    small_grads["c_ctx"] = _cctx_grad(g3, W["c_ctx"].reshape(-1, LANE), "cctx_grad").reshape(D)
    conv_grad_full = small_grads["conv_w"]
    small_grads["conv_w"] = lax.dynamic_slice_in_dim(conv_grad_full, chip * GC, GC, axis=2)

    delta, new_m, new_v = {}, {}, {}
    delta["w_ada"], new_m["w_ada"], new_v["w_ada"] = _adamw(W["w_ada"], grad_w_ada, M["w_ada"], V["w_ada"],
                                                            "adamw_w_ada")
    shapes = [W[k].shape for k in SMALL]
    sd, sm, sv = _adamw(_pack([W[k] for k in SMALL]), _pack([small_grads[k] for k in SMALL]),
                        _pack([M[k] for k in SMALL]), _pack([V[k] for k in SMALL]), "adamw_small")
    for k, d_, m_, v_ in zip(SMALL, _unpack(sd, shapes), _unpack(sm, shapes), _unpack(sv, shapes)):
        delta[k], new_m[k], new_v[k] = d_, m_, v_
    reduce_finish(red_mix, [delta["w_ada"], sd] + [adam_out[k][0] for k in BIG])
    adam(mix_keys, 0, [])
    for k in BIG:
        delta[k], new_m[k], new_v[k] = adam_out[k]
    grads = dict(grads_big)
    grads["w_ada"] = grad_w_ada
    grads.update(small_grads)
    return (loss, grad_x, *[grads[k] for k in WEIGHTS], *[delta[k] for k in WEIGHTS],
            *[new_m[k] for k in WEIGHTS], *[new_v[k] for k in WEIGHTS])


def kernel(x, c, ctx, c_ctx, w_ada, b_ada, w_in, q_norm_g, k_norm_g, pool_w, pool_scale, sgu_ln_g, sgu_ln_b, sgu_w, sgu_b, conv_w, w_br_attn, w_br_pool, w_br_sgu, w_br_conv, w_gate, b_gate, w_o, ln1_g, ln1_b, w_ff_gate, w_ff_up, w_ff_down, ln2_g, ln2_b, loss_target, m_c_ctx, m_w_ada, m_b_ada, m_w_in, m_q_norm_g, m_k_norm_g, m_pool_w, m_pool_scale, m_sgu_ln_g, m_sgu_ln_b, m_sgu_w, m_sgu_b, m_conv_w, m_w_br_attn, m_w_br_pool, m_w_br_sgu, m_w_br_conv, m_w_gate, m_b_gate, m_w_o, m_ln1_g, m_ln1_b, m_w_ff_gate, m_w_ff_up, m_w_ff_down, m_ln2_g, m_ln2_b, v_c_ctx, v_w_ada, v_b_ada, v_w_in, v_q_norm_g, v_k_norm_g, v_pool_w, v_pool_scale, v_sgu_ln_g, v_sgu_ln_b, v_sgu_w, v_sgu_b, v_conv_w, v_w_br_attn, v_w_br_pool, v_w_br_sgu, v_w_br_conv, v_w_gate, v_b_gate, v_w_o, v_ln1_g, v_ln1_b, v_w_ff_gate, v_w_ff_up, v_w_ff_down, v_ln2_g, v_ln2_b):
    args = locals()
    W = {k: args[k] for k in WEIGHTS}
    M = {k: args["m_" + k] for k in WEIGHTS}
    V = {k: args["v_" + k] for k in WEIGHTS}
    return _step(x, c, ctx, loss_target, W, M, V)
```

```python
import functools
import math

import jax
import jax.numpy as jnp
from jax import lax
from jax.experimental import pallas as pl
from jax.experimental.pallas import tpu as pltpu

F32 = jnp.float32
BF16 = jnp.bfloat16
MESH = pl.DeviceIdType.MESH

N_DEV = 8
N_CHIP = 4
GRID_W = 64
HEAD_DIM = 128
N_HEADS = 8
N_KV = 2
KV_GROUP = N_HEADS // N_KV
Q_W = N_HEADS * HEAD_DIM
KV_W = N_KV * HEAD_DIM
QKV_W = Q_W + 2 * KV_W
GC = 128
N_GROUP = 4
BR_W = N_GROUP * GC
POOL_WINDOWS = (2, 4, 8, 16)
OFF_POOL = QKV_W
OFF_U = OFF_POOL + BR_W
OFF_VG = OFF_U + BR_W
OFF_CB = OFF_VG + BR_W
OFF_CC = OFF_CB + BR_W
OFF_CX = OFF_CC + BR_W
IN_W = OFF_CX + BR_W
Q_BLOCK = 128
ROPE_THETA = 10000.0
LN_EPS = 1e-5
RMS_EPS = 1e-6
ADAM_LR = 0.001
ADAM_B1 = 0.9
ADAM_B2 = 0.999
ADAM_EPS = 1e-08
ADAM_WD = 0.01
ADAM_STEP = 10

ROW_TM = 256
LANE = 128
SUBLANE = 8
VMEM_BIG = 56 * 1024 * 1024
VMEM_MID = 40 * 1024 * 1024


def _cp(sem=None, vmem=VMEM_MID):
    return pltpu.CompilerParams(dimension_semantics=sem, vmem_limit_bytes=vmem)


_ANY = pl.BlockSpec(memory_space=pl.ANY)
_HBM = pl.BlockSpec(memory_space=pltpu.HBM)
_SEM = pl.BlockSpec(memory_space=pltpu.SEMAPHORE)
_EFFECT = pltpu.SideEffectType.DATAFLOW_SIDE_EFFECTING


def _tile(n, pref, unit=LANE, whole_ok=False):
    if n <= pref:
        return n
    best = None
    for t in range(unit, pref + 1, unit):
        if n % t == 0:
            best = t
    if whole_ok and (best is None or best * 4 < pref):
        return n
    assert best is not None, (n, pref, unit)
    return best


def _layer_of(w, l):
    if isinstance(w, (list, tuple)):
        return w[l][:, None], 0
    return w, l


def _mm_nn(a, w, l, kind, *, name, out_dtype=F32, bias=None, act=None, tm_pref=768, tn_pref=1152):
    w, l = _layer_of(w, l)
    T, K = a.shape
    tm = _tile(T, tm_pref, SUBLANE * 2)
    if kind == "col":
        nq, _, kw, ns = w.shape
        assert kw == K
        tn = _tile(ns, tn_pref, whole_ok=True)
        nj = ns // tn
        n_total = nq * ns
        w_spec = pl.BlockSpec((None, None, K, tn), lambda j, i: (j // nj, l, 0, j % nj))
        grid_n = nq * nj
    else:
        nq, _, kc, n_total = w.shape
        assert nq * kc == K
        tn = _tile(n_total, tn_pref)
        w_spec = pl.BlockSpec((nq, None, kc, tn), lambda j, i: (0, l, 0, j))
        grid_n = n_total // tn

    def body(*refs):
        if bias is not None:
            a_ref, w_ref, b_ref, o_ref = refs
        else:
            a_ref, w_ref, o_ref = refs
        wv = w_ref[...]
        if kind == "row":
            wv = wv.reshape(K, tn)
        acc = jnp.dot(a_ref[...], wv, preferred_element_type=F32)
        if bias is not None:
            acc = acc + b_ref[...]
        if act is not None:
            acc = act(acc)
        o_ref[...] = acc.astype(out_dtype)

    in_specs = [pl.BlockSpec((tm, K), lambda j, i: (i, 0)), w_spec]
    args = [a, w]
    if bias is not None:
        in_specs.append(pl.BlockSpec((1, tn), lambda j, i: (0, j)))
        args.append(bias)
    return pl.pallas_call(
        body, grid=(grid_n, T // tm), in_specs=in_specs,
        out_specs=pl.BlockSpec((tm, tn), lambda j, i: (i, j)),
        out_shape=jax.ShapeDtypeStruct((T, n_total), out_dtype),
        name=name, compiler_params=_cp(("parallel", "parallel"), VMEM_BIG),
    )(*args)


def _mm_nt(dc, w, l, kind, *, name, out_dtype=F32, dc_off=0, tm_pref=768, tk_pref=1024):
    w, l = _layer_of(w, l)
    T = dc.shape[0]
    tm = _tile(T, tm_pref, SUBLANE * 2)
    if kind == "col":
        nq, _, K, ns = w.shape
        tk = _tile(K, tk_pref)
        assert dc_off % ns == 0
        offb = dc_off // ns

        def body(dc_ref, w_ref, o_ref, acc_ref):
            q = pl.program_id(2)
            part = lax.dot_general(dc_ref[...], w_ref[...], (((1,), (1,)), ((), ())),
                                   preferred_element_type=F32)

            @pl.when(q == 0)
            def _():
                acc_ref[...] = part

            @pl.when(q > 0)
            def _():
                acc_ref[...] += part

            @pl.when(q == nq - 1)
            def _():
                o_ref[...] = acc_ref[...].astype(out_dtype)

        return pl.pallas_call(
            body, grid=(K // tk, T // tm, nq),
            in_specs=[pl.BlockSpec((tm, ns), lambda kk, i, q: (i, offb + q)),
                      pl.BlockSpec((None, None, tk, ns), lambda kk, i, q: (q, l, kk, 0))],
            out_specs=pl.BlockSpec((tm, tk), lambda kk, i, q: (i, kk)),
            out_shape=jax.ShapeDtypeStruct((T, K), out_dtype),
            scratch_shapes=[pltpu.VMEM((tm, tk), F32)],
            name=name, compiler_params=_cp(("parallel", "parallel", "arbitrary"), VMEM_BIG),
        )(dc, w)

    nq, _, kc, n = w.shape
    assert dc_off % n == 0
    offb = dc_off // n
    tk = _tile(kc, tk_pref, whole_ok=True)
    nkk = kc // tk

    def body(dc_ref, w_ref, o_ref):
        o_ref[...] = lax.dot_general(dc_ref[...], w_ref[...], (((1,), (1,)), ((), ())),
                                     preferred_element_type=F32).astype(out_dtype)

    return pl.pallas_call(
        body, grid=(nq * nkk, T // tm),
        in_specs=[pl.BlockSpec((tm, n), lambda j, i: (i, offb)),
                  pl.BlockSpec((None, None, tk, n), lambda j, i: (j // nkk, l, j % nkk, 0))],
        out_specs=pl.BlockSpec((tm, tk), lambda j, i: (i, j)),
        out_shape=jax.ShapeDtypeStruct((T, nq * kc), out_dtype),
        name=name, compiler_params=_cp(("parallel", "parallel"), VMEM_BIG),
    )(dc, w)


def _mm_tn(a, dc, kind, *, nq, kdim, ndim, name, a_off=0, dc_off=0, tk_pref=512, tn_pref=1152, out_dtype=BF16):
    T = a.shape[0]
    assert dc.shape[0] == T
    tk = _tile(kdim, tk_pref, whole_ok=True)
    tn = _tile(ndim, tn_pref, whole_ok=True)
    nkk, njn = kdim // tk, ndim // tn
    assert a_off % tk == 0 and dc_off % tn == 0
    aoffb, doffb = a_off // tk, dc_off // tn
    if kind == "col":
        a_map = lambda q, kk, jn: (0, aoffb + kk)
        d_map = lambda q, kk, jn: (0, doffb + q * njn + jn)
    else:
        a_map = lambda q, kk, jn: (0, aoffb + q * nkk + kk)
        d_map = lambda q, kk, jn: (0, doffb + jn)

    def body(a_ref, d_ref, o_ref):
        o_ref[...] = lax.dot_general(a_ref[...], d_ref[...], (((0,), (0,)), ((), ())),
                                     preferred_element_type=F32).astype(out_dtype)

    return pl.pallas_call(
        body, grid=(nq, nkk, njn),
        in_specs=[pl.BlockSpec((T, tk), a_map), pl.BlockSpec((T, tn), d_map)],
        out_specs=pl.BlockSpec((None, tk, tn), lambda q, kk, jn: (q, kk, jn)),
        out_shape=jax.ShapeDtypeStruct((nq, kdim, ndim), out_dtype),
        name=name, compiler_params=_cp(("parallel", "parallel", "parallel"), VMEM_BIG),
    )(a, dc)


def _row_spec(d):
    return pl.BlockSpec((ROW_TM, d), lambda i: (i, 0))


def _mod_spec(d, ncb):
    return pl.BlockSpec((None, 1, d), lambda i: (jnp.where(i >= ncb, 1, 0), 0, 0))


def _vec_spec(d):
    return pl.BlockSpec((1, d), lambda i: (0, 0))


def _part_spec(d):
    return pl.BlockSpec((None, SUBLANE, d), lambda i: (i, 0, 0))


def _modulate(x, sc, sh, ncb, name):
    T, D = x.shape

    def body(x_ref, sc_ref, sh_ref, o_ref):
        o_ref[...] = (x_ref[...] * (1.0 + sc_ref[...]) + sh_ref[...]).astype(BF16)

    return pl.pallas_call(
        body, grid=(T // ROW_TM,),
        in_specs=[_row_spec(D), _mod_spec(D, ncb), _mod_spec(D, ncb)],
        out_specs=_row_spec(D), out_shape=jax.ShapeDtypeStruct((T, D), BF16),
        name=name, compiler_params=_cp(("parallel",)),
    )(x, sc, sh)


def _ln_stats(r):
    mu = jnp.mean(r, axis=-1, keepdims=True)
    rc = r - mu
    var = jnp.mean(rc * rc, axis=-1, keepdims=True)
    rstd = lax.rsqrt(var + LN_EPS)
    return rc * rstd, rstd


def _resid_ln(x, y, gate, g, b, sc, sh, ncb, alpha, name):
    T, D = x.shape

    def body(x_ref, y_ref, gate_ref, g_ref, b_ref, sc_ref, sh_ref, xo_ref, h_ref):
        xhat, _ = _ln_stats(alpha * x_ref[...] + gate_ref[...] * y_ref[...])
        xo = xhat * g_ref[...] + b_ref[...]
        xo_ref[...] = xo
        h_ref[...] = (xo * (1.0 + sc_ref[...]) + sh_ref[...]).astype(BF16)

    return pl.pallas_call(
        body, grid=(T // ROW_TM,),
        in_specs=[_row_spec(D), _row_spec(D), _mod_spec(D, ncb), _vec_spec(D), _vec_spec(D),
                  _mod_spec(D, ncb), _mod_spec(D, ncb)],
        out_specs=[_row_spec(D), _row_spec(D)],
        out_shape=[jax.ShapeDtypeStruct((T, D), F32), jax.ShapeDtypeStruct((T, D), BF16)],
        name=name, compiler_params=_cp(("parallel",)),
    )(x, y, gate, g, b, sc, sh)


def _resid_ln_loss(x, y, gate, g, b, target, ncb, alpha, name):
    T, D = x.shape
    nblk = T // ROW_TM

    def body(x_ref, y_ref, gate_ref, g_ref, b_ref, t_ref, dy_ref, loss_ref):
        i = pl.program_id(0)
        xhat, _ = _ln_stats(alpha * x_ref[...] + gate_ref[...] * y_ref[...])
        xo = xhat * g_ref[...] + b_ref[...]
        live = (i >= ncb).astype(F32)
        err = (xo - t_ref[...]) * live
        dy_ref[...] = err * (1.0 / D)
        loss_ref[...] = jnp.full((SUBLANE, LANE), 0.5 / D, F32) * jnp.sum(err * err)

    return pl.pallas_call(
        body, grid=(nblk,),
        in_specs=[_row_spec(D), _row_spec(D), _mod_spec(D, ncb), _vec_spec(D), _vec_spec(D),
                  pl.BlockSpec((ROW_TM, D), lambda i: (jnp.maximum(i - ncb, 0), 0))],
        out_specs=[_row_spec(D), _part_spec(LANE)],
        out_shape=[jax.ShapeDtypeStruct((T, D), F32), jax.ShapeDtypeStruct((nblk, SUBLANE, LANE), F32)],
        name=name, compiler_params=_cp(("parallel",)),
    )(x, y, gate, g, b, target)


def _write_parts(part_ref, rows, d):
    for k, r in enumerate(rows):
        part_ref[pl.ds(k, 1), :] = jnp.sum(r, axis=0, keepdims=True)
    if len(rows) < SUBLANE:
        part_ref[pl.ds(len(rows), SUBLANE - len(rows)), :] = jnp.zeros((SUBLANE - len(rows), d), F32)


def _ln_bwd(dxa, dhs, sc, x, y, gate, g, b, ncb, alpha, name, after=None):
    T, D = x.shape
    nblk = T // ROW_TM
    ndh = len(dhs)

    def body(*refs):
        dxa_ref = refs[0]
        dh_refs = refs[1:1 + ndh]
        sc_ref, x_ref, y_ref, gate_ref, g_ref, b_ref = refs[1 + ndh:7 + ndh]
        dx_ref, dy_ref, part_ref = refs[-3:]
        yv = y_ref[...]
        xhat, rstd = _ln_stats(alpha * x_ref[...] + gate_ref[...] * yv)
        dxo = dxa_ref[...]
        rows = []
        if ndh:
            dh = dh_refs[0][...]
            for r in dh_refs[1:]:
                dh = dh + r[...]
            dxo = dxo + dh * (1.0 + sc_ref[...])
            xo = xhat * g_ref[...] + b_ref[...]
            rows = [dh * xo, dh]
        dxhat = dxo * g_ref[...]
        m1 = jnp.mean(dxhat, axis=-1, keepdims=True)
        m2 = jnp.mean(dxhat * xhat, axis=-1, keepdims=True)
        dr = rstd * (dxhat - m1 - xhat * m2)
        dx_ref[...] = alpha * dr
        dy_ref[...] = (gate_ref[...] * dr).astype(BF16)
        _write_parts(part_ref, [dxo * xhat, dxo, dr * yv] + rows, D)

    in_specs = ([_row_spec(D)] * (1 + ndh)
                + [_mod_spec(D, ncb), _row_spec(D), _row_spec(D), _mod_spec(D, ncb), _vec_spec(D), _vec_spec(D)])
    extra = list(after or ())
    return pl.pallas_call(
        body, grid=(nblk,), in_specs=in_specs + [_ANY] * len(extra),
        out_specs=[_row_spec(D), _row_spec(D), _part_spec(D)],
        out_shape=[jax.ShapeDtypeStruct((T, D), F32), jax.ShapeDtypeStruct((T, D), BF16),
                   jax.ShapeDtypeStruct((nblk, SUBLANE, D), F32)],
        name=name, compiler_params=_cp(("parallel",)),
    )(dxa, *dhs, sc, x, y, gate, g, b, *extra)


def _mod_bwd(dxa, dhs, sc, x, ncb, name, after=()):
    T, D = x.shape
    nblk = T // ROW_TM
    ndh = len(dhs)

    def body(*refs):
        dxa_ref = refs[0]
        dh_refs = refs[1:1 + ndh]
        sc_ref, x_ref = refs[1 + ndh:3 + ndh]
        dx_ref, part_ref = refs[-2:]
        dh = dh_refs[0][...]
        for r in dh_refs[1:]:
            dh = dh + r[...]
        dx_ref[...] = dxa_ref[...] + dh * (1.0 + sc_ref[...])
        _write_parts(part_ref, [dh * x_ref[...], dh], D)

    return pl.pallas_call(
        body, grid=(nblk,),
        in_specs=[_row_spec(D)] * (1 + ndh) + [_mod_spec(D, ncb), _row_spec(D)] + [_ANY] * len(after),
        out_specs=[_row_spec(D), _part_spec(D)],
        out_shape=[jax.ShapeDtypeStruct((T, D), F32), jax.ShapeDtypeStruct((nblk, SUBLANE, D), F32)],
        name=name, compiler_params=_cp(("parallel",)),
    )(dxa, *dhs, sc, x, *after)


def _merge_fwd(gates, ts, name):
    T, D = ts[0].shape
    tm = 128

    def body(g_ref, t0, t1, t2, t3, o_ref):
        acc = g_ref[:, 0:D].astype(F32) * t0[...].astype(F32)
        for k, t in enumerate((t1, t2, t3), start=1):
            acc = acc + g_ref[:, k * D:(k + 1) * D].astype(F32) * t[...].astype(F32)
        o_ref[...] = acc.astype(BF16)

    rs = pl.BlockSpec((tm, D), lambda i: (i, 0))
    return pl.pallas_call(
        body, grid=(T // tm,),
        in_specs=[pl.BlockSpec((tm, 4 * D), lambda i: (i, 0)), rs, rs, rs, rs],
        out_specs=rs, out_shape=jax.ShapeDtypeStruct((T, D), BF16),
        name=name, compiler_params=_cp(("parallel",)),
    )(gates, *ts)


def _merge_bwd(dmg, gates, ts, name):
    T, D = dmg.shape
    tm = 128
    nblk = T // tm

    def body(d_ref, g_ref, t0, t1, t2, t3, dpre_ref, dt0, dt1, dt2, dt3, part_ref):
        d = d_ref[...]
        for k, (t, dt) in enumerate(zip((t0, t1, t2, t3), (dt0, dt1, dt2, dt3))):
            gk = g_ref[:, k * D:(k + 1) * D].astype(F32)
            dt[...] = (d * gk).astype(BF16)
            dpre = d * t[...].astype(F32) * gk * (1.0 - gk)
            dpre_ref[:, k * D:(k + 1) * D] = dpre.astype(BF16)
            part_ref[:, k * D:(k + 1) * D] = jnp.sum(dpre, axis=0, keepdims=True)

    rs = pl.BlockSpec((tm, D), lambda i: (i, 0))
    wide = pl.BlockSpec((tm, 4 * D), lambda i: (i, 0))
    return pl.pallas_call(
        body, grid=(nblk,),
        in_specs=[rs, wide, rs, rs, rs, rs],
        out_specs=[wide, rs, rs, rs, rs, pl.BlockSpec((None, 1, 4 * D), lambda i: (i, 0, 0))],
        out_shape=[jax.ShapeDtypeStruct((T, 4 * D), BF16)] + [jax.ShapeDtypeStruct((T, D), BF16)] * 4
                  + [jax.ShapeDtypeStruct((nblk, 1, 4 * D), F32)],
        name=name, compiler_params=_cp(("parallel",)),
    )(dmg, gates, *ts)


def _ffn_up(a, wg, wu, name, tm_pref=384):
    T, K = a.shape
    nq, _, ns = wg.shape
    tm = _tile(T, tm_pref, SUBLANE * 2)

    def body(a_ref, g_ref, u_ref, gg_ref, uu_ref, act_ref):
        av = a_ref[...]
        g = jnp.dot(av, g_ref[...], preferred_element_type=F32)
        u = jnp.dot(av, u_ref[...], preferred_element_type=F32)
        gg_ref[...] = g.astype(BF16)
        uu_ref[...] = u.astype(BF16)
        act_ref[...] = (g * jax.nn.sigmoid(g) * u).astype(BF16)

    ws = pl.BlockSpec((None, K, ns), lambda q, i: (q, 0, 0))
    os = pl.BlockSpec((tm, ns), lambda q, i: (i, q))
    return pl.pallas_call(
        body, grid=(nq, T // tm),
        in_specs=[pl.BlockSpec((tm, K), lambda q, i: (i, 0)), ws, ws], out_specs=[os, os, os],
        out_shape=[jax.ShapeDtypeStruct((T, nq * ns), BF16)] * 3,
        name=name, compiler_params=_cp(("parallel", "parallel"), VMEM_BIG),
    )(a, wg, wu)


def _ffn_down_bwd(dff, wd, gg, uu, name, tm_pref=384):
    T, N = dff.shape
    nq, kc, _ = wd.shape
    tm = _tile(T, tm_pref, SUBLANE * 2)

    def body(d_ref, w_ref, g_ref, u_ref, dg_ref, du_ref):
        da = lax.dot_general(d_ref[...], w_ref[...], (((1,), (1,)), ((), ())), preferred_element_type=F32)
        g = g_ref[...].astype(F32)
        sig = jax.nn.sigmoid(g)
        dg_ref[...] = (da * u_ref[...].astype(F32) * sig * (1.0 + g * (1.0 - sig))).astype(BF16)
        du_ref[...] = (da * g * sig).astype(BF16)

    ts = pl.BlockSpec((tm, kc), lambda q, i: (i, q))
    return pl.pallas_call(
        body, grid=(nq, T // tm),
        in_specs=[pl.BlockSpec((tm, N), lambda q, i: (i, 0)), pl.BlockSpec((None, kc, N), lambda q, i: (q, 0, 0)),
                  ts, ts],
        out_specs=[ts, ts], out_shape=[jax.ShapeDtypeStruct((T, nq * kc), BF16)] * 2,
        name=name, compiler_params=_cp(("parallel", "parallel"), VMEM_BIG),
    )(dff, wd, gg, uu)


def _swap_halves(v):
    lane = lax.broadcasted_iota(jnp.int32, v.shape, 1)
    return jnp.where((lane % 64) < 32, pltpu.roll(v, 96, 1), pltpu.roll(v, 32, 1))


def _rope_tables(n, nc):
    rows = n // GRID_W
    row = jnp.repeat(jnp.arange(rows), GRID_W).astype(F32)
    col = jnp.tile(jnp.arange(GRID_W), rows).astype(F32)
    inv = ROPE_THETA ** (-jnp.arange(0, 64, 2, dtype=F32) / 64)
    ang_r = row[:, None] * inv
    ang_c = col[:, None] * inv
    cos = jnp.concatenate([jnp.cos(ang_r), jnp.cos(ang_r), jnp.cos(ang_c), jnp.cos(ang_c)], axis=-1)
    sin = jnp.concatenate([-jnp.sin(ang_r), jnp.sin(ang_r), -jnp.sin(ang_c), jnp.sin(ang_c)], axis=-1)
    cos = jnp.concatenate([jnp.ones((nc, HEAD_DIM), F32), cos], axis=0)
    sin = jnp.concatenate([jnp.zeros((nc, HEAD_DIM), F32), sin], axis=0)
    return cos, sin


def _qk_fwd(p, cos, sin, qg, kg, name):
    T = p.shape[0]

    def body(p_ref, c_ref, s_ref, qg_ref, kg_ref, q_ref, k_ref, v_ref):
        c, s = c_ref[...], s_ref[...]
        for h in range(N_HEADS + N_KV):
            x = p_ref[:, h * HEAD_DIM:(h + 1) * HEAD_DIM]
            rs = lax.rsqrt(jnp.mean(x * x, axis=-1, keepdims=True) + RMS_EPS)
            gain = qg_ref[...] if h < N_HEADS else kg_ref[...]
            yv = x * rs * gain
            out = yv * c + _swap_halves(yv) * s
            if h < N_HEADS:
                out = out * Q_PRESCALE
            out = out.astype(BF16)
            if h < N_HEADS:
                q_ref[:, h * HEAD_DIM:(h + 1) * HEAD_DIM] = out
            else:
                k_ref[:, (h - N_HEADS) * HEAD_DIM:(h - N_HEADS + 1) * HEAD_DIM] = out
        v_ref[...] = p_ref[:, Q_W + KV_W:QKV_W].astype(BF16)

    return pl.pallas_call(
        body, grid=(T // ROW_TM,),
        in_specs=[_row_spec(QKV_W), _row_spec(HEAD_DIM), _row_spec(HEAD_DIM), _vec_spec(HEAD_DIM), _vec_spec(HEAD_DIM)],
        out_specs=[_row_spec(Q_W), _row_spec(KV_W), _row_spec(KV_W)],
        out_shape=[jax.ShapeDtypeStruct((T, Q_W), BF16), jax.ShapeDtypeStruct((T, KV_W), BF16),
                   jax.ShapeDtypeStruct((T, KV_W), BF16)],
        name=name, compiler_params=_cp(("parallel",)),
    )(p, cos, sin, qg, kg)


def _qk_bwd(p, dq, dk, dv, cos, sin, qg, kg, name):
    T = p.shape[0]
    nblk = T // ROW_TM

    def body(p_ref, dq_ref, dk_ref, dv_ref, c_ref, s_ref, qg_ref, kg_ref, dp_ref, part_ref):
        c, s = c_ref[...], s_ref[...]
        dgq = jnp.zeros((1, HEAD_DIM), F32)
        dgk = jnp.zeros((1, HEAD_DIM), F32)
        for h in range(N_HEADS + N_KV):
            x = p_ref[:, h * HEAD_DIM:(h + 1) * HEAD_DIM]
            if h < N_HEADS:
                d = dq_ref[:, h * HEAD_DIM:(h + 1) * HEAD_DIM]
                gain = qg_ref[...]
            else:
                d = dk_ref[:, (h - N_HEADS) * HEAD_DIM:(h - N_HEADS + 1) * HEAD_DIM]
                gain = kg_ref[...]
            dyv = d * c + _swap_halves(d * s)
            rs = lax.rsqrt(jnp.mean(x * x, axis=-1, keepdims=True) + RMS_EPS)
            xn = x * rs
            dgsum = jnp.sum(dyv * xn, axis=0, keepdims=True)
            if h < N_HEADS:
                dgq = dgq + dgsum
            else:
                dgk = dgk + dgsum
            dxg = dyv * gain
            dx = rs * (dxg - xn * jnp.mean(dxg * xn, axis=-1, keepdims=True))
            dp_ref[:, h * HEAD_DIM:(h + 1) * HEAD_DIM] = dx.astype(BF16)
        dp_ref[:, Q_W + KV_W:QKV_W] = dv_ref[...].astype(BF16)
        part_ref[pl.ds(0, 1), :] = dgq
        part_ref[pl.ds(1, 1), :] = dgk
        part_ref[pl.ds(2, SUBLANE - 2), :] = jnp.zeros((SUBLANE - 2, HEAD_DIM), F32)

    return pl.pallas_call(
        body, grid=(nblk,),
        in_specs=[_row_spec(QKV_W), _row_spec(Q_W), _row_spec(KV_W), _row_spec(KV_W),
                  _row_spec(HEAD_DIM), _row_spec(HEAD_DIM), _vec_spec(HEAD_DIM), _vec_spec(HEAD_DIM)],
        out_specs=[_row_spec(QKV_W), _part_spec(HEAD_DIM)],
        out_shape=[jax.ShapeDtypeStruct((T, QKV_W), BF16), jax.ShapeDtypeStruct((nblk, SUBLANE, HEAD_DIM), F32)],
        name=name, compiler_params=_cp(("parallel",)),
    )(p, dq, dk, dv, cos, sin, qg, kg)


ATTN_SCALE = HEAD_DIM ** -0.5
LOG2E = 1.4426950408889634
Q_PRESCALE = ATTN_SCALE * LOG2E


def _attn_weights(q, k):
    s = lax.dot_general(q, k, (((1,), (1,)), ((), ())), preferred_element_type=F32)
    e = jnp.exp2(s - jnp.max(s, axis=-1, keepdims=True))
    return e, 1.0 / jnp.sum(e, axis=-1, keepdims=True)


def _attn_fwd(q, k, v, nc, name):
    T = q.shape[0]

    def heads(q_ref, k_ref, v_ref, o_ref, nkeys):
        for h in range(N_HEADS):
            g = h // KV_GROUP
            kk = k_ref[0:nkeys, g * HEAD_DIM:(g + 1) * HEAD_DIM]
            vv = v_ref[0:nkeys, g * HEAD_DIM:(g + 1) * HEAD_DIM]
            e, rl = _attn_weights(q_ref[:, h * HEAD_DIM:(h + 1) * HEAD_DIM], kk)
            o = jnp.dot(e.astype(BF16), vv, preferred_element_type=F32) * rl
            o_ref[:, h * HEAD_DIM:(h + 1) * HEAD_DIM] = o.astype(BF16)

    def body(q_ref, k_ref, v_ref, o_ref):
        i = pl.program_id(0)

        @pl.when(i < nc // Q_BLOCK)
        def _():
            heads(q_ref, k_ref, v_ref, o_ref, nc)

        @pl.when(i >= nc // Q_BLOCK)
        def _():
            heads(q_ref, k_ref, v_ref, o_ref, T)

    whole = pl.BlockSpec((T, KV_W), lambda i: (0, 0))
    qs = pl.BlockSpec((Q_BLOCK, Q_W), lambda i: (i, 0))
    return pl.pallas_call(
        body, grid=(T // Q_BLOCK,), in_specs=[qs, whole, whole], out_specs=qs,
        out_shape=jax.ShapeDtypeStruct((T, Q_W), BF16),
        name=name, compiler_params=_cp(("parallel",)),
    )(q, k, v)


def _attn_bwd(q, k, v, do, nc, name):
    T = q.shape[0]

    def heads(q_ref, k_ref, v_ref, do_ref, dq_ref, dk_ref, dv_ref, nkeys):
        for h in range(N_HEADS):
            g = h // KV_GROUP
            cols = slice(g * HEAD_DIM, (g + 1) * HEAD_DIM)
            hc = slice(h * HEAD_DIM, (h + 1) * HEAD_DIM)
            qh, kk, vv, doh = q_ref[:, hc], k_ref[0:nkeys, cols], v_ref[0:nkeys, cols], do_ref[:, hc]
            e, rl = _attn_weights(qh, kk)
            dpr = lax.dot_general(doh, vv, (((1,), (1,)), ((), ())), preferred_element_type=F32)
            delta = jnp.sum(e * dpr, axis=-1, keepdims=True) * rl
            dsu = (e * (dpr - delta)).astype(BF16)
            dq_ref[:, hc] = jnp.dot(dsu, kk, preferred_element_type=F32) * (rl * ATTN_SCALE)
            q_rows = (qh.astype(F32) * (rl * (1.0 / LOG2E))).astype(BF16)
            dk_ref[0:nkeys, cols] += lax.dot_general(dsu, q_rows, (((0,), (0,)), ((), ())),
                                                     preferred_element_type=F32)
            do_rows = (doh.astype(F32) * rl).astype(BF16)
            dv_ref[0:nkeys, cols] += lax.dot_general(e.astype(BF16), do_rows, (((0,), (0,)), ((), ())),
                                                     preferred_element_type=F32)

    def body(q_ref, k_ref, v_ref, do_ref, dq_ref, dk_ref, dv_ref):
        i = pl.program_id(0)

        @pl.when(i == 0)
        def _():
            dk_ref[...] = jnp.zeros_like(dk_ref)
            dv_ref[...] = jnp.zeros_like(dv_ref)

        @pl.when(i < nc // Q_BLOCK)
        def _():
            heads(q_ref, k_ref, v_ref, do_ref, dq_ref, dk_ref, dv_ref, nc)

        @pl.when(i >= nc // Q_BLOCK)
        def _():
            heads(q_ref, k_ref, v_ref, do_ref, dq_ref, dk_ref, dv_ref, T)

    whole = pl.BlockSpec((T, KV_W), lambda i: (0, 0))
    qs = pl.BlockSpec((Q_BLOCK, Q_W), lambda i: (i, 0))
    return pl.pallas_call(
        body, grid=(T // Q_BLOCK,), in_specs=[qs, whole, whole, qs], out_specs=[qs, whole, whole],
        out_shape=[jax.ShapeDtypeStruct((T, Q_W), F32), jax.ShapeDtypeStruct((T, KV_W), F32),
                   jax.ShapeDtypeStruct((T, KV_W), F32)],
        name=name, compiler_params=_cp(("arbitrary",)),
    )(q, k, v, do)


def _shift_rows(z, o, nc):
    T = z.shape[0]
    t = lax.broadcasted_iota(jnp.int32, (T, 1), 0)
    lo = jnp.where(t < nc, 0, nc)
    hi = jnp.where(t < nc, nc, T)
    ok = jnp.logical_and(t + o >= lo, t + o < hi)
    rolled = z if o == 0 else pltpu.roll(z, (-o) % T, 0)
    return jnp.where(ok, rolled, 0.0), ok


def _pool_window(w):
    left = w // 2
    return -left, w - 1 - left


def _pool_d(z, w, nc):
    o0, o1 = _pool_window(w)
    tot = jnp.zeros_like(z)
    cnt = jnp.zeros((z.shape[0], 1), F32)
    for o in range(o0, o1 + 1):
        sh, ok = _shift_rows(z, o, nc)
        tot = tot + sh
        cnt = cnt + ok.astype(F32)
    return tot / cnt - z, cnt


def _pool_fwd(p, pool_w, pool_scale, nc, name):
    T = p.shape[0]

    def body(z_ref, w_ref, s_ref, o_ref):
        for g, w in enumerate(POOL_WINDOWS):
            cs = slice(g * GC, (g + 1) * GC)
            d, _ = _pool_d(z_ref[:, cs], w, nc)
            yv = jnp.dot(d.astype(BF16), w_ref[g].astype(BF16), preferred_element_type=F32)
            o_ref[:, cs] = (yv * s_ref[:, cs]).astype(BF16)

    return pl.pallas_call(
        body, grid=(1,),
        in_specs=[pl.BlockSpec((T, BR_W), lambda i: (0, OFF_POOL // BR_W)),
                  pl.BlockSpec((N_GROUP, GC, GC), lambda i: (0, 0, 0)), _vec_spec(BR_W)],
        out_specs=pl.BlockSpec((T, BR_W), lambda i: (0, 0)),
        out_shape=jax.ShapeDtypeStruct((T, BR_W), BF16),
        name=name, compiler_params=_cp(("arbitrary",), VMEM_BIG),
    )(p, pool_w, pool_scale)


def _pool_bwd(p, dy, pool_w, pool_scale, nc, name):
    T = p.shape[0]

    def body(z_ref, dy_ref, w_ref, s_ref, dz_ref, dw_ref, ds_ref):
        for g, w in enumerate(POOL_WINDOWS):
            cs = slice(g * GC, (g + 1) * GC)
            d, cnt = _pool_d(z_ref[:, cs], w, nc)
            db = d.astype(BF16)
            wb = w_ref[g].astype(BF16)
            dyv = dy_ref[:, cs]
            e = (dyv * s_ref[:, cs]).astype(BF16)
            dw_ref[g] = lax.dot_general(db, e, (((0,), (0,)), ((), ())), preferred_element_type=F32)
            yraw = jnp.dot(db, wb, preferred_element_type=F32)
            ds_ref[:, cs] = jnp.sum(dyv * yraw, axis=0, keepdims=True)
            dd = lax.dot_general(e, wb, (((1,), (1,)), ((), ())), preferred_element_type=F32)
            ec = dd / cnt
            o0, o1 = _pool_window(w)
            tot = jnp.zeros_like(dd)
            for o in range(o0, o1 + 1):
                tot = tot + _shift_rows(ec, -o, nc)[0]
            dz_ref[:, cs] = (tot - dd).astype(BF16)

    return pl.pallas_call(
        body, grid=(1,),
        in_specs=[pl.BlockSpec((T, BR_W), lambda i: (0, OFF_POOL // BR_W)),
                  pl.BlockSpec((T, BR_W), lambda i: (0, 0)),
                  pl.BlockSpec((N_GROUP, GC, GC), lambda i: (0, 0, 0)), _vec_spec(BR_W)],
        out_specs=[pl.BlockSpec((T, BR_W), lambda i: (0, 0)),
                   pl.BlockSpec((N_GROUP, GC, GC), lambda i: (0, 0, 0)), _vec_spec(BR_W)],
        out_shape=[jax.ShapeDtypeStruct((T, BR_W), BF16), jax.ShapeDtypeStruct((N_GROUP, GC, GC), F32),
                   jax.ShapeDtypeStruct((1, BR_W), F32)],
        name=name, compiler_params=_cp(("arbitrary",), VMEM_BIG),
    )(p, dy, pool_w, pool_scale)


_GELU_C = math.sqrt(2.0 / math.pi)


def _gelu(x):
    return 0.5 * x * (1.0 + jnp.tanh(_GELU_C * (x + 0.044715 * x * x * x)))


def _gelu_grad(x):
    th = jnp.tanh(_GELU_C * (x + 0.044715 * x * x * x))
    return 0.5 * (1.0 + th) + 0.5 * x * (1.0 - th * th) * _GELU_C * (1.0 + 3.0 * 0.044715 * x * x)


def _sgu_fwd(p, ln_g, ln_b, w_s, b_st, name):
    T = p.shape[0]

    def body(pu_ref, pv_ref, g_ref, b_ref, w_ref, bs_ref, o_ref):
        u = _gelu(pu_ref[...])
        vhat, _ = _ln_stats(_gelu(pv_ref[...]))
        vn = (vhat * g_ref[...] + b_ref[...]).astype(BF16)
        for g in range(N_GROUP):
            cs = slice(g * GC, (g + 1) * GC)
            s = jnp.dot(w_ref[g].astype(BF16), vn[:, cs], preferred_element_type=F32) + bs_ref[:, g:g + 1]
            o_ref[:, cs] = (u[:, cs] * s).astype(BF16)

    chunk = lambda off: pl.BlockSpec((GC, BR_W), lambda i: (i, off // BR_W))
    return pl.pallas_call(
        body, grid=(T // GC,),
        in_specs=[chunk(OFF_U), chunk(OFF_VG), _vec_spec(BR_W), _vec_spec(BR_W),
                  pl.BlockSpec((N_GROUP, GC, GC), lambda i: (0, 0, 0)),
                  pl.BlockSpec((GC, N_GROUP), lambda i: (0, 0))],
        out_specs=pl.BlockSpec((GC, BR_W), lambda i: (i, 0)),
        out_shape=jax.ShapeDtypeStruct((T, BR_W), BF16),
        name=name, compiler_params=_cp(("parallel",)),
    )(p, p, ln_g, ln_b, w_s, b_st)


def _sgu_bwd(p, dy, ln_g, ln_b, w_s, b_st, name):
    T = p.shape[0]

    def body(pu_ref, pv_ref, dy_ref, g_ref, b_ref, w_ref, bs_ref, dp_ref, dw_ref, dsacc_ref, dln_ref):
        i = pl.program_id(0)

        @pl.when(i == 0)
        def _():
            dw_ref[...] = jnp.zeros_like(dw_ref)
            dsacc_ref[...] = jnp.zeros_like(dsacc_ref)
            dln_ref[...] = jnp.zeros_like(dln_ref)

        pu, pv, dyv = pu_ref[...], pv_ref[...], dy_ref[...]
        u = _gelu(pu)
        vhat, rstd = _ln_stats(_gelu(pv))
        vn = (vhat * g_ref[...] + b_ref[...]).astype(BF16)
        ds = dyv * u
        dsb = ds.astype(BF16)
        dsacc_ref[...] += ds
        dvn_parts = []
        for g in range(N_GROUP):
            cs = slice(g * GC, (g + 1) * GC)
            wb = w_ref[g].astype(BF16)
            s = jnp.dot(wb, vn[:, cs], preferred_element_type=F32) + bs_ref[:, g:g + 1]
            dp_ref[:, cs] = (dyv[:, cs] * s * _gelu_grad(pu[:, cs])).astype(BF16)
            dvn_parts.append(lax.dot_general(wb, dsb[:, cs], (((0,), (0,)), ((), ())),
                                             preferred_element_type=F32))
            dw_ref[g] += lax.dot_general(dsb[:, cs], vn[:, cs], (((1,), (1,)), ((), ())),
                                         preferred_element_type=F32)
        dvn = jnp.concatenate(dvn_parts, axis=-1)
        dln_ref[pl.ds(0, 1), :] += jnp.sum(dvn * vhat, axis=0, keepdims=True)
        dln_ref[pl.ds(1, 1), :] += jnp.sum(dvn, axis=0, keepdims=True)
        dvhat = dvn * g_ref[...]
        m1 = jnp.mean(dvhat, axis=-1, keepdims=True)
        m2 = jnp.mean(dvhat * vhat, axis=-1, keepdims=True)
        dv = rstd * (dvhat - m1 - vhat * m2)
        dp_ref[:, BR_W:2 * BR_W] = (dv * _gelu_grad(pv)).astype(BF16)

    chunk = lambda off: pl.BlockSpec((GC, BR_W), lambda i: (i, off // BR_W))
    return pl.pallas_call(
        body, grid=(T // GC,),
        in_specs=[chunk(OFF_U), chunk(OFF_VG), pl.BlockSpec((GC, BR_W), lambda i: (i, 0)),
                  _vec_spec(BR_W), _vec_spec(BR_W),
                  pl.BlockSpec((N_GROUP, GC, GC), lambda i: (0, 0, 0)),
                  pl.BlockSpec((GC, N_GROUP), lambda i: (0, 0))],
        out_specs=[pl.BlockSpec((GC, 2 * BR_W), lambda i: (i, 0)),
                   pl.BlockSpec((N_GROUP, GC, GC), lambda i: (0, 0, 0)),
                   pl.BlockSpec((GC, BR_W), lambda i: (0, 0)),
                   pl.BlockSpec((SUBLANE, BR_W), lambda i: (0, 0))],
        out_shape=[jax.ShapeDtypeStruct((T, 2 * BR_W), BF16), jax.ShapeDtypeStruct((N_GROUP, GC, GC), F32),
                   jax.ShapeDtypeStruct((GC, BR_W), F32), jax.ShapeDtypeStruct((SUBLANE, BR_W), F32)],
        name=name, compiler_params=_cp(("arbitrary",)),
    )(p, p, dy, ln_g, ln_b, w_s, b_st)


def _conv_fwd(p, conv_w, nc, name):
    T = p.shape[0]

    def body(cb_ref, cc_ref, cx_ref, w_ref, o_ref):
        zz = cc_ref[...] * cx_ref[...]
        conv = (w_ref[0:1, :] * _shift_rows(zz, -1, nc)[0] + w_ref[1:2, :] * zz
                + w_ref[2:3, :] * _shift_rows(zz, 1, nc)[0])
        o_ref[...] = (cb_ref[...] * conv).astype(BF16)

    col = lambda off: pl.BlockSpec((T, GC), lambda j: (0, off // GC + j))
    return pl.pallas_call(
        body, grid=(N_GROUP,),
        in_specs=[col(OFF_CB), col(OFF_CC), col(OFF_CX), pl.BlockSpec((3, GC), lambda j: (0, j))],
        out_specs=pl.BlockSpec((T, GC), lambda j: (0, j)),
        out_shape=jax.ShapeDtypeStruct((T, BR_W), BF16),
        name=name, compiler_params=_cp(("parallel",)),
    )(p, p, p, conv_w)


def _conv_bwd(p, dy, conv_w, nc, name):
    T = p.shape[0]

    def body(cb_ref, cc_ref, cx_ref, dy_ref, w_ref, dp_ref, dw_ref):
        cc, cx, dyv = cc_ref[...], cx_ref[...], dy_ref[...]
        zz = cc * cx
        zm, zp = _shift_rows(zz, -1, nc)[0], _shift_rows(zz, 1, nc)[0]
        conv = w_ref[0:1, :] * zm + w_ref[1:2, :] * zz + w_ref[2:3, :] * zp
        dconv = dyv * cb_ref[...]
        dw_ref[0:1, :] = jnp.sum(dconv * zm, axis=0, keepdims=True)
        dw_ref[1:2, :] = jnp.sum(dconv * zz, axis=0, keepdims=True)
        dw_ref[2:3, :] = jnp.sum(dconv * zp, axis=0, keepdims=True)
        dzz = (w_ref[0:1, :] * _shift_rows(dconv, 1, nc)[0] + w_ref[1:2, :] * dconv
               + w_ref[2:3, :] * _shift_rows(dconv, -1, nc)[0])
        dp_ref[0] = (dyv * conv).astype(BF16)
        dp_ref[1] = (dzz * cx).astype(BF16)
        dp_ref[2] = (dzz * cc).astype(BF16)

    col = lambda off: pl.BlockSpec((T, GC), lambda j: (0, off // GC + j))
    return pl.pallas_call(
        body, grid=(N_GROUP,),
        in_specs=[col(OFF_CB), col(OFF_CC), col(OFF_CX), pl.BlockSpec((T, GC), lambda j: (0, j)),
                  pl.BlockSpec((3, GC), lambda j: (0, j))],
        out_specs=[pl.BlockSpec((3, T, GC), lambda j: (0, 0, j)), pl.BlockSpec((3, GC), lambda j: (0, j))],
        out_shape=[jax.ShapeDtypeStruct((3, T, BR_W), BF16), jax.ShapeDtypeStruct((3, BR_W), F32)],
        name=name, compiler_params=_cp(("parallel",)),
    )(p, p, p, dy, conv_w)


def _rows_tile(rows, cols, n_arrays):
    budget = 24 * 1024 * 1024 // (2 * 4 * n_arrays * cols)
    return _tile(rows, max(SUBLANE * 2, min(budget, 1024)), SUBLANE * 2) if rows % (SUBLANE * 2) == 0 else rows


def _cast_into(chip, w, l, name, after=()):
    _, K, cols = w.shape
    tr = _rows_tile(K, cols, 2)

    def body(q_ref, w_ref, *rest):
        rest[-1][...] = w_ref[...].astype(BF16)

    return pl.pallas_call(
        body,
        grid_spec=pltpu.PrefetchScalarGridSpec(
            num_scalar_prefetch=1, grid=(K // tr,),
            in_specs=[pl.BlockSpec((None, tr, cols), lambda i, q: (l, i, 0))] + [_ANY] * len(after),
            out_specs=pl.BlockSpec((None, tr, cols), lambda i, q: (q[0], i, 0))),
        out_shape=jax.ShapeDtypeStruct((N_CHIP, K, cols), BF16),
        name=name, compiler_params=_cp(("parallel",)),
    )(chip, w, *after)


def _adamw(w, g, m, v, name):
    shape = w.shape
    cols = shape[-1]
    rows = w.size // cols
    tr = _rows_tile(rows, cols, 7)
    c1 = 1.0 - ADAM_B1 ** ADAM_STEP
    c2 = 1.0 - ADAM_B2 ** ADAM_STEP

    def body(w_ref, g_ref, m_ref, v_ref, d_ref, mo_ref, vo_ref):
        gv = g_ref[...]
        mn = ADAM_B1 * m_ref[...] + (1.0 - ADAM_B1) * gv
        vn = ADAM_B2 * v_ref[...] + (1.0 - ADAM_B2) * (gv * gv)
        mo_ref[...] = mn
        vo_ref[...] = vn
        d_ref[...] = -ADAM_LR * ((mn / c1) / (jnp.sqrt(vn / c2) + ADAM_EPS) + ADAM_WD * w_ref[...])

    rs = pl.BlockSpec((tr, cols), lambda i: (i, 0))
    outs = pl.pallas_call(
        body, grid=(rows // tr,), in_specs=[rs] * 4, out_specs=[rs] * 3,
        out_shape=[jax.ShapeDtypeStruct((rows, cols), F32)] * 3,
        name=name, compiler_params=_cp(("parallel",)),
    )(*[a.reshape(rows, cols) for a in (w, g, m, v)])
    return [o.reshape(shape) for o in outs]


def _adamw_layer(w, g, m, v, l, prev, after, name):
    L, K, cols = w.shape
    tr = _rows_tile(K, cols, 7)
    c1 = 1.0 - ADAM_B1 ** ADAM_STEP
    c2 = 1.0 - ADAM_B2 ** ADAM_STEP
    nprev = 0 if prev is None else 3

    def body(*refs):
        w_ref, g_ref, m_ref, v_ref = refs[:4]
        d_ref, mo_ref, vo_ref = refs[-3:]
        gv = g_ref[...]
        mn = ADAM_B1 * m_ref[...] + (1.0 - ADAM_B1) * gv
        vn = ADAM_B2 * v_ref[...] + (1.0 - ADAM_B2) * (gv * gv)
        mo_ref[...] = mn
        vo_ref[...] = vn
        d_ref[...] = -ADAM_LR * ((mn / c1) / (jnp.sqrt(vn / c2) + ADAM_EPS) + ADAM_WD * w_ref[...])

    rs = pl.BlockSpec((None, tr, cols), lambda i: (l, i, 0))
    extra = list(prev or ()) + list(after)
    return pl.pallas_call(
        body, grid=(K // tr,), in_specs=[rs] * 4 + [_ANY] * len(extra), out_specs=[rs] * 3,
        out_shape=[jax.ShapeDtypeStruct((L, K, cols), F32)] * 3,
        input_output_aliases={4 + k: k for k in range(nprev)},
        name=name, compiler_params=_cp(("parallel",)),
    )(w, g, m, v, *extra)


def _pair_sum(core, dw, recv, name):
    nq, half, cols = recv.shape
    tr = _rows_tile(half, cols, 3)
    nb = half // tr

    def body(c_ref, a_ref, r_ref, o_ref):
        o_ref[...] = (a_ref[...].astype(F32) + r_ref[...].astype(F32)).astype(BF16)

    rs = pl.BlockSpec((None, tr, cols), lambda q, i, c: (q, i, 0))
    return pl.pallas_call(
        body,
        grid_spec=pltpu.PrefetchScalarGridSpec(
            num_scalar_prefetch=1, grid=(nq, nb),
            in_specs=[pl.BlockSpec((None, tr, cols), lambda q, i, c: (q, c[0] * nb + i, 0)), rs],
            out_specs=rs),
        out_shape=jax.ShapeDtypeStruct((nq, half, cols), BF16),
        name=name, compiler_params=_cp(("parallel", "parallel")),
    )(core, dw, recv)


def _chip_sum(chip, core, psum, land, l, n_layers, prev, name):
    _, half, cols = psum.shape
    tr = _rows_tile(half, cols, 5)
    nb = half // tr

    def body(*refs):
        p_ref, r_ref, o_ref = refs[2], refs[3], refs[-1]
        o_ref[...] = (p_ref[...].astype(F32) + r_ref[0].astype(F32) + r_ref[1].astype(F32)
                      + r_ref[2].astype(F32))

    extra = [] if prev is None else [prev]
    return pl.pallas_call(
        body,
        grid_spec=pltpu.PrefetchScalarGridSpec(
            num_scalar_prefetch=2, grid=(nb,),
            in_specs=[pl.BlockSpec((None, tr, cols), lambda i, q, c: (q[0], i, 0)),
                      pl.BlockSpec((3, tr, cols), lambda i, q, c: (0, i, 0))] + [_ANY] * len(extra),
            out_specs=pl.BlockSpec((None, tr, cols), lambda i, q, c: (l, c[0] * nb + i, 0))),
        out_shape=jax.ShapeDtypeStruct((n_layers, 2 * half, cols), F32),
        input_output_aliases={4: 0} if prev is not None else {},
        name=name, compiler_params=_cp(("parallel",)),
    )(chip, core, psum, land, *extra)


def _slab_sum(gathered, name, after=()):
    n, rows, cols = gathered.shape
    tr = _tile(rows, 512, SUBLANE)

    def body(g_ref, *rest):
        acc = g_ref[0]
        for d in range(1, n):
            acc = acc + g_ref[d]
        rest[-1][...] = acc

    return pl.pallas_call(
        body, grid=(rows // tr,),
        in_specs=[pl.BlockSpec((n, tr, cols), lambda i: (0, i, 0))] + [_ANY] * len(after),
        out_specs=pl.BlockSpec((tr, cols), lambda i: (i, 0)),
        out_shape=jax.ShapeDtypeStruct((rows, cols), F32),
        name=name, compiler_params=_cp(("parallel",)),
    )(gathered, *after)


def _silu(x):
    return x * jax.nn.sigmoid(x)


def _ada_fwd(cvec, w_ada, b_cols, name):
    L, D, ns = w_ada.shape
    tn = _tile(ns, 768)

    def body(c_ref, w_ref, b_ref, o_ref):
        s = _silu(c_ref[...]).astype(BF16)
        o_ref[...] = jnp.dot(s, w_ref[...].astype(BF16), preferred_element_type=F32) + b_ref[...]

    return pl.pallas_call(
        body, grid=(L, ns // tn),
        in_specs=[pl.BlockSpec((16, D), lambda l, j: (0, 0)),
                  pl.BlockSpec((None, D, tn), lambda l, j: (l, 0, j)),
                  pl.BlockSpec((None, 1, tn), lambda l, j: (l, 0, j))],
        out_specs=pl.BlockSpec((None, 16, tn), lambda l, j: (l, 0, j)),
        out_shape=jax.ShapeDtypeStruct((L, 16, ns), F32),
        name=name, compiler_params=_cp(("parallel", "parallel")),
    )(cvec, w_ada, b_cols)


def _ada_bwd(cvec, dmod, w_ada, name):
    L, D, ns = w_ada.shape
    tn = _tile(ns, 768)

    def body(c_ref, d_ref, w_ref, gw_ref, ds_ref):
        first = jnp.logical_and(pl.program_id(0) == 0, pl.program_id(1) == 0)

        @pl.when(first)
        def _():
            ds_ref[...] = jnp.zeros_like(ds_ref)

        s = _silu(c_ref[...]).astype(BF16)
        db = d_ref[...].astype(BF16)
        gw_ref[...] = lax.dot_general(s, db, (((0,), (0,)), ((), ())), preferred_element_type=F32)
        ds_ref[...] += lax.dot_general(db, w_ref[...].astype(BF16), (((1,), (1,)), ((), ())),
                                       preferred_element_type=F32)

    return pl.pallas_call(
        body, grid=(L, ns // tn),
        in_specs=[pl.BlockSpec((16, D), lambda l, j: (0, 0)),
                  pl.BlockSpec((None, 16, tn), lambda l, j: (l, 0, j)),
                  pl.BlockSpec((None, D, tn), lambda l, j: (l, 0, j))],
        out_specs=[pl.BlockSpec((None, D, tn), lambda l, j: (l, 0, j)),
                   pl.BlockSpec((16, D), lambda l, j: (0, 0))],
        out_shape=[jax.ShapeDtypeStruct((L, D, ns), F32), jax.ShapeDtypeStruct((16, D), F32)],
        name=name, compiler_params=_cp(("arbitrary", "arbitrary")),
    )(cvec, dmod, w_ada)


def _cctx_grad(gathered, c_ctx, name):
    rows = c_ctx.shape[0]

    def body(g_ref, c_ref, o_ref):
        ds = g_ref[0] + g_ref[2] + g_ref[4] + g_ref[6]
        cv = c_ref[...]
        sig = jax.nn.sigmoid(cv)
        o_ref[...] = ds * sig * (1.0 + cv * (1.0 - sig))

    return pl.pallas_call(
        body, out_shape=jax.ShapeDtypeStruct((rows, LANE), F32), name=name,
    )(gathered, c_ctx)


def _place():
    x, y, c = lax.axis_index("x"), lax.axis_index("y"), lax.axis_index("c")
    chips = [(1 - x, y), (x, 1 - y), (1 - x, 1 - y)]
    return x, y, c, chips


def _all_gather_small(slab, name, after=()):
    rows, cols = slab.shape

    def body(x_ref, *rest):
        out_ref, send_sems, recv_sems, local_sem = rest[len(after):]
        x, y, c, chips = _place()
        me, sibling = (x, y, c), (x, y, 1 - c)

        def blk(px, py, pc):
            return out_ref.at[4 * px + 2 * py + pc]

        def copy(k, block, to, src=None):
            return pltpu.make_async_remote_copy(
                src_ref=blk(*block) if src is None else src, dst_ref=blk(*block),
                send_sem=send_sems.at[k], recv_sem=recv_sems.at[k], device_id=to, device_id_type=MESH)

        mine = pltpu.make_async_copy(x_ref, blk(*me), local_sem)
        mine.start()
        first = [copy(0, me, sibling, src=x_ref)]
        first += [copy(1 + j, me, (*chip, c), src=x_ref) for j, chip in enumerate(chips)]
        for cp in first:
            cp.start()
        passed = [copy(4 + j, (*chip, c), sibling) for j, chip in enumerate(chips)]
        for j, chip in enumerate(chips):
            copy(1 + j, (*chip, c), me).wait_recv()
            passed[j].start()
        copy(0, sibling, me).wait_recv()
        for j, chip in enumerate(chips):
            copy(4 + j, (*chip, 1 - c), me).wait_recv()
        for cp in first + passed:
            cp.wait_send()
        mine.wait()

    return pl.pallas_call(
        body, out_shape=jax.ShapeDtypeStruct((N_DEV, rows, cols), slab.dtype),
        in_specs=[pl.BlockSpec(memory_space=pltpu.VMEM)] + [_ANY] * len(after),
        out_specs=pl.BlockSpec(memory_space=pltpu.VMEM),
        scratch_shapes=[pltpu.SemaphoreType.DMA((7,)), pltpu.SemaphoreType.DMA((7,)), pltpu.SemaphoreType.DMA],
        name=name, compiler_params=pltpu.CompilerParams(vmem_limit_bytes=VMEM_MID),
    )(slab, *after)


def _hbm(a):
    return pltpu.with_memory_space_constraint(a, pltpu.HBM)


def _half_rows(ref, q, c):
    half = ref.shape[1] // 2
    return ref.at[q, pl.ds(c * half, half)]


def _ici_copy(src, dst, send_sems, recv_sems, k, to):
    return pltpu.make_async_remote_copy(src_ref=src, dst_ref=dst, send_sem=send_sems.at[k], recv_sem=recv_sems.at[k],
                                        device_id=to, device_id_type=MESH)


def _gather_start(bufs, sizes, name):
    n = len(bufs)
    ng = len(sizes)

    def body(*refs):
        ins = refs[:n]
        sems = refs[n:n + 2 * ng]
        token = refs[-1]
        x, y, c, chips = _place()
        a = 0
        for g, sz in enumerate(sizes):
            for i in range(sz):
                part = _half_rows(ins[a], 2 * x + y, c)
                for j, chip in enumerate(chips):
                    _ici_copy(part, part, sems[2 * g], sems[2 * g + 1], 3 * i + j, (*chip, c)).start()
                a += 1
        token[...] = jnp.zeros_like(token)

    sem_shapes = []
    for sz in sizes:
        sem_shapes += [pltpu.SemaphoreType.DMA((3 * sz,)), pltpu.SemaphoreType.DMA((3 * sz,))]
    outs = pl.pallas_call(
        body, name=name,
        out_shape=tuple(sem_shapes) + tuple(pltpu.HBM(b.shape, b.dtype) for b in bufs)
                  + (jax.ShapeDtypeStruct((SUBLANE, LANE), F32),),
        in_specs=[_HBM] * n,
        out_specs=tuple([_SEM] * (2 * ng) + [_HBM] * n + [pl.BlockSpec(memory_space=pltpu.VMEM)]),
        input_output_aliases={i: 2 * ng + i for i in range(n)},
        compiler_params=pltpu.CompilerParams(has_side_effects=_EFFECT),
    )(*[_hbm(b) for b in bufs])
    sems = [(outs[2 * g], outs[2 * g + 1]) for g in range(ng)]
    return sems, list(outs[2 * ng:2 * ng + n]), outs[-1]


def _gather_wait(bufs, sems, after, name):
    n = len(bufs)
    na = len(after)

    def body(*refs):
        ins = refs[:n]
        send_sems, recv_sems = refs[n], refs[n + 1]
        x, y, c, chips = _place()
        for i in range(n):
            sent = _half_rows(ins[i], 2 * x + y, c)
            for j, (cx, cy) in enumerate(chips):
                cp = _ici_copy(sent, _half_rows(ins[i], 2 * cx + cy, c), send_sems, recv_sems, 3 * i + j, (cx, cy, c))
                cp.wait_send()
                cp.wait_recv()

    outs = pl.pallas_call(
        body, name=name,
        out_shape=tuple(pltpu.HBM(b.shape, b.dtype) for b in bufs),
        in_specs=[_HBM] * n + [_SEM, _SEM] + [_ANY] * na,
        out_specs=tuple([_HBM] * n),
        input_output_aliases={i: i for i in range(n)},
        compiler_params=pltpu.CompilerParams(has_side_effects=_EFFECT),
    )(*bufs, sems[0], sems[1], *after)
    return list(outs)


def _exchange_halves(bufs, name):
    n = len(bufs)

    def body(*refs):
        ins = refs[:n]
        send_sems, recv_sems = refs[-2:]
        x, y, c, chips = _place()
        sends = []
        for i in range(n):
            for j, (cx, cy) in enumerate(chips):
                part = _half_rows(ins[i], 2 * cx + cy, c)
                cp = _ici_copy(part, part, send_sems, recv_sems, 3 * i + j, (x, y, 1 - c))
                cp.start()
                sends.append(cp)
        for i in range(n):
            for j, (cx, cy) in enumerate(chips):
                part = _half_rows(ins[i], 2 * cx + cy, 1 - c)
                _ici_copy(part, part, send_sems, recv_sems, 3 * i + j, (x, y, 1 - c)).wait_recv()
        for cp in sends:
            cp.wait_send()

    outs = pl.pallas_call(
        body, name=name,
        out_shape=[jax.ShapeDtypeStruct(b.shape, b.dtype) for b in bufs],
        in_specs=[_ANY] * n, out_specs=[_ANY] * n,
        input_output_aliases={i: i for i in range(n)},
        scratch_shapes=[pltpu.SemaphoreType.DMA((3 * n,)), pltpu.SemaphoreType.DMA((3 * n,))],
    )(*bufs)
    return list(outs)


def _other_half(ref, c):
    half = ref.shape[1] // 2
    return ref.at[:, pl.ds((1 - c) * half, half)]


def _prereduce_plan(src, land, x, y, c, chips):
    return [(_other_half(src, c), land, (x, y, 1 - c))]


def _prereduce_land(d):
    return (d.shape[0], d.shape[1] // 2, d.shape[2])


def _scatter_plan(src, land, x, y, c, chips):
    return [(src.at[2 * cx + cy], land.at[j], (cx, cy, c)) for j, (cx, cy) in enumerate(chips)]


def _scatter_land(p):
    return (3,) + p.shape[1:]


_COPIES_PER_SOURCE = {_prereduce_plan: 1, _scatter_plan: 3}


def _split_start(srcs, plan, land_shape, name):
    n = len(srcs)
    land_shapes = [land_shape(s) for s in srcs]
    per = _COPIES_PER_SOURCE[plan]

    def body(*refs):
        src_refs, lands = refs[:n], refs[n:2 * n]
        send_sems, recv_sems = refs[2 * n], refs[2 * n + 1]
        token = refs[-1]
        x, y, c, chips = _place()
        for i in range(n):
            for j, (s, d, to) in enumerate(plan(src_refs[i], lands[i], x, y, c, chips)):
                _ici_copy(s, d, send_sems, recv_sems, per * i + j, to).start()
        token[...] = jnp.zeros_like(token)

    outs = pl.pallas_call(
        body, name=name,
        out_shape=(pltpu.SemaphoreType.DMA((per * n,)), pltpu.SemaphoreType.DMA((per * n,)))
                  + tuple(pltpu.HBM(p.shape, p.dtype) for p in srcs)
                  + tuple(pltpu.HBM(s, BF16) for s in land_shapes)
                  + (jax.ShapeDtypeStruct((SUBLANE, LANE), F32),),
        in_specs=[_HBM] * (2 * n),
        out_specs=tuple([_SEM, _SEM] + [_HBM] * (2 * n) + [pl.BlockSpec(memory_space=pltpu.VMEM)]),
        input_output_aliases={i: 2 + i for i in range(2 * n)},
        compiler_params=pltpu.CompilerParams(has_side_effects=_EFFECT),
    )(*[_hbm(p) for p in srcs], *[_hbm(lax.empty(s, BF16)) for s in land_shapes])
    return (outs[0], outs[1]), list(outs[2:2 + n]), list(outs[2 + n:2 + 2 * n]), outs[-1]


def _split_wait(psums, lands, sems, plan, after, name):
    n = len(psums)
    na = len(after)
    per = _COPIES_PER_SOURCE[plan]

    def body(*refs):
        srcs, lnds = refs[:n], refs[n:2 * n]
        send_sems, recv_sems = refs[2 * n], refs[2 * n + 1]
        x, y, c, chips = _place()
        for i in range(n):
            for j, (s, d, to) in enumerate(plan(srcs[i], lnds[i], x, y, c, chips)):
                cp = _ici_copy(s, d, send_sems, recv_sems, per * i + j, to)
                cp.wait_send()
                cp.wait_recv()

    outs = pl.pallas_call(
        body, name=name,
        out_shape=tuple(pltpu.HBM(a.shape, a.dtype) for a in list(psums) + list(lands)),
        in_specs=[_HBM] * (2 * n) + [_SEM, _SEM] + [_ANY] * na,
        out_specs=tuple([_HBM] * (2 * n)),
        input_output_aliases={i: i for i in range(2 * n)},
        compiler_params=pltpu.CompilerParams(has_side_effects=_EFFECT),
    )(*psums, *lands, sems[0], sems[1], *after)
    return list(outs[:n]), list(outs[n:])


def _exchange_grads(grads, l, name):
    n = len(grads)

    def body(*refs):
        ins = refs[:n]
        send_sems, recv_sems = refs[-2:]
        x, y, c, _ = _place()

        def rows(ref, core):
            half = ref.shape[1] // 2
            return ref.at[l, pl.ds(core * half, half)]

        cps = []
        for i in range(n):
            cp = _ici_copy(rows(ins[i], c), rows(ins[i], c), send_sems, recv_sems, i, (x, y, 1 - c))
            cp.start()
            cps.append(cp)
        for i, cp in enumerate(cps):
            cp.wait_send()
            _ici_copy(rows(ins[i], 1 - c), rows(ins[i], 1 - c), send_sems, recv_sems, i, (x, y, 1 - c)).wait_recv()

    outs = pl.pallas_call(
        body, name=name,
        out_shape=[jax.ShapeDtypeStruct(g.shape, g.dtype) for g in grads],
        in_specs=[_ANY] * n, out_specs=[_ANY] * n,
        input_output_aliases={i: i for i in range(n)},
        scratch_shapes=[pltpu.SemaphoreType.DMA((n,)), pltpu.SemaphoreType.DMA((n,))],
    )(*grads)
    return list(outs)


def _pack(arrs):
    flat = jnp.concatenate([a.reshape(-1).astype(F32) for a in arrs])
    pad = (-flat.shape[0]) % (SUBLANE * LANE)
    return jnp.pad(flat, (0, pad)).reshape(-1, LANE)


def _unpack(slab, shapes):
    flat = slab.reshape(-1)
    out, off = [], 0
    for s in shapes:
        n = math.prod(s)
        out.append(flat[off:off + n].reshape(s))
        off += n
    return out


def _split6(v):
    d = v.shape[-1] // 6
    return [v[:, k * d:(k + 1) * d].reshape(2, 1, d) for k in range(6)]


BIG = ("w_in", "w_br_attn", "w_br_pool", "w_br_sgu", "w_br_conv", "w_gate", "w_o", "w_ff_gate", "w_ff_up", "w_ff_down")
KIND = {"w_in": "col", "w_br_attn": "col", "w_br_pool": "col", "w_br_sgu": "col", "w_br_conv": "col",
        "w_gate": "col", "w_o": "row", "w_ff_gate": "col", "w_ff_up": "col", "w_ff_down": "row"}
SMALL = ("c_ctx", "b_ada", "q_norm_g", "k_norm_g", "pool_w", "pool_scale", "sgu_ln_g", "sgu_ln_b", "sgu_w",
         "sgu_b", "conv_w", "b_gate", "ln1_g", "ln1_b", "ln2_g", "ln2_b")
WEIGHTS = ("c_ctx", "w_ada", "b_ada", "w_in", "q_norm_g", "k_norm_g", "pool_w", "pool_scale", "sgu_ln_g", "sgu_ln_b",
           "sgu_w", "sgu_b", "conv_w", "w_br_attn", "w_br_pool", "w_br_sgu", "w_br_conv", "w_gate", "b_gate", "w_o",
           "ln1_g", "ln1_b", "w_ff_gate", "w_ff_up", "w_ff_down", "ln2_g", "ln2_b")


def _step(x, c, ctx, loss_target, W, M, V):
    L = W["w_ada"].shape[0]
    assert L == 2, "core c of a chip carries layer c of the weight traffic"
    N, D = x.shape[1], x.shape[2]
    NC = ctx.shape[1]
    T = NC + N
    FF = W["w_ff_down"].shape[1] * N_CHIP
    assert NC % ROW_TM == 0 and N % ROW_TM == 0 and N % GRID_W == 0 and D % LANE == 0
    ncb = NC // ROW_TM
    nblk = T // ROW_TM
    alpha = (2 * L) ** 0.25
    ax, ay, ac = lax.axis_index("x"), lax.axis_index("y"), lax.axis_index("c")
    chip = 2 * ax + ay
    dev = 2 * chip + ac
    chip_arr = jnp.reshape(chip, (1,)).astype(jnp.int32)
    core_arr = jnp.reshape(ac, (1,)).astype(jnp.int32)
    ns_ada = W["w_ada"].shape[2]
    chip_devs = (0, 2, 4, 6)

    conv_shape = W["conv_w"].shape
    g0 = _all_gather_small(_pack([c, W["conv_w"]]), "gather_c")
    c_all = g0[:, :D // LANE, :].reshape(N_DEV, D)
    conv_parts = [g0[d].reshape(-1)[D:D + math.prod(conv_shape)].reshape(conv_shape) for d in chip_devs]
    conv_full = jnp.concatenate(conv_parts, axis=-1)
    cvec = jnp.concatenate([c_all, W["c_ctx"][None], jnp.zeros((16 - N_DEV - 1, D), F32)], axis=0)
    b_cols = lax.dynamic_slice_in_dim(W["b_ada"], chip * ns_ada, ns_ada, axis=1).reshape(L, 1, ns_ada)
    mod_part = _ada_fwd(cvec, W["w_ada"], b_cols, "ada_fwd")
    g1 = _all_gather_small(mod_part.reshape(-1, LANE), "gather_mod")
    mod_all = jnp.concatenate([g1[d].reshape(L, 16, ns_ada) for d in chip_devs], axis=-1)
    mod_lat = lax.dynamic_index_in_dim(mod_all, dev, axis=1, keepdims=True)
    mods = jnp.concatenate([mod_all[:, N_DEV:N_DEV + 1], mod_lat], axis=1)

    ffn_keys = ("w_ff_gate", "w_ff_up", "w_ff_down")
    mid_keys = ("w_br_attn", "w_br_pool", "w_br_sgu", "w_br_conv", "w_o")
    groups = [grp for l in range(L) for grp in ([("w_in", l)], [("w_gate", l)], [(k, l) for k in mid_keys],
                                                [(k, l) for k in ffn_keys])]
    flying, gsems, gtokens = [None] * len(groups), [None] * len(groups), []
    for n_start, gs in enumerate(((0, 1), (2, 3), (4, 5, 6, 7))):
        members = [kl for g in gs for kl in groups[g]]
        casts = [_cast_into(chip_arr, W[k], l, "cast_%s_l%d" % (k, l), after=gtokens[-1:] or [g1])
                 for k, l in members]
        sems, bufs, token = _gather_start(casts, [len(groups[g]) for g in gs], "gather_start_%d" % n_start)
        gtokens.append(token)
        lo = 0
        for g, sem in zip(gs, sems):
            flying[g], gsems[g] = bufs[lo:lo + len(groups[g])], sem
            lo += len(groups[g])
    full = {k: [None] * L for k in BIG}

    def weights_ready(g, after):
        got = _gather_wait(flying[g], gsems[g], after, "gather_wait_%d" % g)
        for (k, l), buf in zip(groups[g], _exchange_halves(got, "exchange_halves_%d" % g)):
            full[k][l] = buf

    cos, sin = _rope_tables(N, NC)
    x0 = jnp.concatenate([ctx[0], x[0]], axis=0)

    saved = []
    xin = x0
    h1 = None
    dy_top = loss_parts = None
    for l in range(L):
        sh1, sc1, g1m, sh2, sc2, g2m = _split6(mods[l])
        tag = "_l%d" % l
        if l == 0:
            h1 = _modulate(xin, sc1, sh1, ncb, "modulate" + tag)
        weights_ready(4 * l, [h1])
        p = _mm_nn(h1, full["w_in"], l, "col", name="mm_in" + tag)
        qg, kg = W["q_norm_g"][l][None], W["k_norm_g"][l][None]
        q, k, v = _qk_fwd(p, cos, sin, qg, kg, "qk_fwd" + tag)
        att = _attn_fwd(q, k, v, NC, "attn_fwd" + tag)
        ps = W["pool_scale"][l][None]
        ypool = _pool_fwd(p, W["pool_w"][l], ps, NC, "pool_fwd" + tag)
        lg, lb = W["sgu_ln_g"][l][None], W["sgu_ln_b"][l][None]
        b_st = W["sgu_b"][l].T
        ysgu = _sgu_fwd(p, lg, lb, W["sgu_w"][l], b_st, "sgu_fwd" + tag)
        yconv = _conv_fwd(p, conv_full[l], NC, "conv_fwd" + tag)
        brs = (att, ypool, ysgu, yconv)
        weights_ready(4 * l + 1, list(brs))
        gates = _mm_nn(h1, full["w_gate"], l, "col", name="mm_gate" + tag, bias=W["b_gate"][l][None],
                       act=jax.nn.sigmoid, out_dtype=BF16)
        weights_ready(4 * l + 2, [gates])
        ts = [_mm_nn(b, full[wk], l, "col", name="mm_" + wk + tag, out_dtype=BF16)
              for b, wk in zip(brs, ("w_br_attn", "w_br_pool", "w_br_sgu", "w_br_conv"))]
        mg = _merge_fwd(gates, ts, "merge_fwd" + tag)
        m = _mm_nn(mg, full["w_o"], l, "row", name="mm_o" + tag, tn_pref=1024)
        ln1g, ln1b = W["ln1_g"][l][None], W["ln1_b"][l][None]
        x1, h2 = _resid_ln(xin, m, g1m, ln1g, ln1b, sc2, sh2, ncb, alpha, "resid_ln1" + tag)
        weights_ready(4 * l + 3, [h2])
        gg, uu, act = _ffn_up(h2, full["w_ff_gate"][l], full["w_ff_up"][l], "ffn_up" + tag)
        ff = _mm_nn(act, full["w_ff_down"], l, "row", name="mm_ffd" + tag, tm_pref=384, tn_pref=512)
        ln2g, ln2b = W["ln2_g"][l][None], W["ln2_b"][l][None]
        saved.append(dict(xin=xin, h1=h1, p=p, q=q, k=k, v=v, brs=brs, gates=gates, ts=ts, mg=mg, m=m, x1=x1, h2=h2,
                          gg=gg, uu=uu, act=act, ff=ff))
        if l + 1 < L:
            nsh1, nsc1 = _split6(mods[l + 1])[:2]
            xin, h1 = _resid_ln(x1, ff, g2m, ln2g, ln2b, nsc1, nsh1, ncb, alpha, "resid_ln2" + tag)
        else:
            dy_top, loss_parts = _resid_ln_loss(x1, ff, g2m, ln2g, ln2b, loss_target[0], ncb, alpha,
                                                "resid_ln2_loss" + tag)
    loss_dev = jnp.sum(loss_parts[:, 0, 0])

    def lat_ctx(part_rows):
        return jnp.stack([jnp.sum(part_rows[:ncb], axis=0), jnp.sum(part_rows[ncb:], axis=0)])

    dW = {k: [None] * L for k in BIG}
    small = {k: [None] * L for k in ("q_norm_g", "k_norm_g", "pool_w", "pool_scale", "sgu_ln_g", "sgu_ln_b", "sgu_w",
                                     "sgu_b", "conv_w", "b_gate", "ln1_g", "ln1_b", "ln2_g", "ln2_b")}
    dmods = [None] * L
    mix_keys = tuple(k for k in BIG if k not in ffn_keys)
    grads_big = {k: None for k in BIG}
    adam_out = {k: None for k in BIG}

    def reduce_begin(keys, l, tg):
        sems, src, land, token = _split_start([dW[k][l] for k in keys], _prereduce_plan, _prereduce_land,
                                              "prereduce_start_" + tg)
        return dict(keys=keys, l=l, sems=sems, src=src, land=land, token=token, tg=tg)

    def reduce_scatter(st, after):
        dws, recv = _split_wait(st["src"], st["land"], st["sems"], _prereduce_plan, after,
                                "prereduce_wait_" + st["tg"])
        psums = [_pair_sum(core_arr, d, r, "pair_sum_%s_l%d" % (k, st["l"]))
                 for k, d, r in zip(st["keys"], dws, recv)]
        st["sems"], st["src"], st["land"], st["token"] = _split_start(psums, _scatter_plan, _scatter_land,
                                                                      "scatter_start_" + st["tg"])

    def reduce_finish(st, after):
        src, land = _split_wait(st["src"], st["land"], st["sems"], _scatter_plan, after, "scatter_wait_" + st["tg"])
        for k, p_, r in zip(st["keys"], src, land):
            grads_big[k] = _chip_sum(chip_arr, core_arr, p_, r, st["l"], L, grads_big[k],
                                     "chip_sum_%s_l%d" % (k, st["l"]))
        done = _exchange_grads([grads_big[k] for k in st["keys"]], st["l"], "exchange_grads_" + st["tg"])
        grads_big.update(zip(st["keys"], done))

    def adam(keys, l, after):
        for k in keys:
            adam_out[k] = _adamw_layer(W[k], grads_big[k], M[k], V[k], l, adam_out[k], after,
                                       "adamw_%s_l%d" % (k, l))

    dxa, dhs, sc_prev = dy_top, [], None
    red_l1 = red_ffn = None
    for l in reversed(range(L)):
        s = saved[l]
        sh1, sc1, g1m, sh2, sc2, g2m = _split6(mods[l])
        tag = "_l%d" % l
        ln1g, ln1b = W["ln1_g"][l][None], W["ln1_b"][l][None]
        ln2g, ln2b = W["ln2_g"][l][None], W["ln2_b"][l][None]
        dx1a, dff, part2 = _ln_bwd(dxa, dhs, sc_prev if dhs else sc1, s["x1"], s["ff"], g2m, ln2g, ln2b, ncb, alpha,
                                   "ln2_bwd" + tag, after=[] if red_l1 is None else [red_l1["token"]])
        small["ln2_g"][l] = jnp.sum(part2[:, 0], axis=0)
        small["ln2_b"][l] = jnp.sum(part2[:, 1], axis=0)
        dg2 = lat_ctx(part2[:, 2])
        if dhs:
            dmods[l + 1][1], dmods[l + 1][0] = lat_ctx(part2[:, 3]), lat_ctx(part2[:, 4])
        dgg, duu = _ffn_down_bwd(dff, full["w_ff_down"][l], s["gg"], s["uu"], "ffn_down_bwd" + tag)
        if l == 0:
            reduce_scatter(red_l1, [dgg])
        dW["w_ff_down"][l] = _mm_tn(s["act"], dff, "row", nq=N_CHIP, kdim=FF // N_CHIP, ndim=D, name="tn_ffd" + tag,
                                    tk_pref=1408, tn_pref=512)
        dh2a = _mm_nt(dgg, full["w_ff_gate"], l, "col", name="nt_ffg" + tag)
        dh2b = _mm_nt(duu, full["w_ff_up"], l, "col", name="nt_ffu" + tag)
        dW["w_ff_gate"][l] = _mm_tn(s["h2"], dgg, "col", nq=N_CHIP, kdim=D, ndim=FF // N_CHIP, name="tn_ffg" + tag,
                                    tn_pref=1408)
        dW["w_ff_up"][l] = _mm_tn(s["h2"], duu, "col", nq=N_CHIP, kdim=D, ndim=FF // N_CHIP, name="tn_ffu" + tag,
                                  tn_pref=1408)
        ties = []
        if l == 0:
            red_ffn = reduce_begin(ffn_keys, 0, "l0_ffn")
            ties = [red_l1["token"], red_ffn["token"]]
        dx0a, dm, part1 = _ln_bwd(dx1a, [dh2a, dh2b], sc2, s["xin"], s["m"], g1m, ln1g, ln1b, ncb, alpha,
                                  "ln1_bwd" + tag, after=ties)
        small["ln1_g"][l] = jnp.sum(part1[:, 0], axis=0)
        small["ln1_b"][l] = jnp.sum(part1[:, 1], axis=0)
        dg1 = lat_ctx(part1[:, 2])
        dsc2, dsh2 = lat_ctx(part1[:, 3]), lat_ctx(part1[:, 4])
        dmods[l] = [None, None, dg1, dsh2, dsc2, dg2]
        dmg = _mm_nt(dm, full["w_o"], l, "row", name="nt_o" + tag)
        dW["w_o"][l] = _mm_tn(s["mg"], dm, "row", nq=N_CHIP, kdim=D // N_CHIP, ndim=D, name="tn_o" + tag, tn_pref=1024)
        dpre, dt0, dt1, dt2, dt3, bpart = _merge_bwd(dmg, s["gates"], s["ts"], "merge_bwd" + tag)
        small["b_gate"][l] = jnp.sum(bpart[:, 0], axis=0)
        dh1a = _mm_nt(dpre, full["w_gate"], l, "col", name="nt_gate" + tag)
        dW["w_gate"][l] = _mm_tn(s["h1"], dpre, "col", nq=N_CHIP, kdim=D, ndim=D, name="tn_gate" + tag, tn_pref=1024)
        dbrs = []
        for b, dt, wk, odt in zip(s["brs"], (dt0, dt1, dt2, dt3), ("w_br_attn", "w_br_pool", "w_br_sgu", "w_br_conv"),
                                  (BF16, F32, F32, F32)):
            dbrs.append(_mm_nt(dt, full[wk], l, "col", name="nt_" + wk + tag, out_dtype=odt))
            dW[wk][l] = _mm_tn(b, dt, "col", nq=N_CHIP, kdim=b.shape[1], ndim=D // N_CHIP, name="tn_" + wk + tag)
        qg, kg = W["q_norm_g"][l][None], W["k_norm_g"][l][None]
        dq, dk, dv = _attn_bwd(s["q"], s["k"], s["v"], dbrs[0], NC, "attn_bwd" + tag)
        dp_qkv, qkpart = _qk_bwd(s["p"], dq, dk, dv, cos, sin, qg, kg, "qk_bwd" + tag)
        small["q_norm_g"][l] = jnp.sum(qkpart[:, 0], axis=0)
        small["k_norm_g"][l] = jnp.sum(qkpart[:, 1], axis=0)
        ps = W["pool_scale"][l][None]
        dp_pool, small["pool_w"][l], dps = _pool_bwd(s["p"], dbrs[1], W["pool_w"][l], ps, NC, "pool_bwd" + tag)
        small["pool_scale"][l] = dps[0]
        lg, lb = W["sgu_ln_g"][l][None], W["sgu_ln_b"][l][None]
        dp_sgu, small["sgu_w"][l], dsacc, dln = _sgu_bwd(s["p"], dbrs[2], lg, lb, W["sgu_w"][l], W["sgu_b"][l].T,
                                                         "sgu_bwd" + tag)
        small["sgu_b"][l] = jnp.sum(dsacc.reshape(GC, N_GROUP, GC), axis=-1).T
        small["sgu_ln_g"][l], small["sgu_ln_b"][l] = dln[0], dln[1]
        dp_conv, small["conv_w"][l] = _conv_bwd(s["p"], dbrs[3], conv_full[l], NC, "conv_bwd" + tag)
        dp = jnp.concatenate([dp_qkv, dp_pool, dp_sgu, dp_conv[0], dp_conv[1], dp_conv[2]], axis=-1)
        dh1b = _mm_nt(dp, full["w_in"], l, "col", name="nt_in" + tag)
        dW["w_in"][l] = _mm_tn(s["h1"], dp, "col", nq=N_CHIP, kdim=D, ndim=IN_W // N_CHIP, name="tn_in" + tag,
                               tn_pref=1152)
        dxa, dhs, sc_prev = dx0a, [dh1a, dh1b], sc1
        if l == 1:
            red_l1 = reduce_begin(BIG, 1, "l1")
    dx0, part0 = _mod_bwd(dxa, dhs, sc_prev, x0, ncb, "mod_bwd")
    dmods[0][1], dmods[0][0] = lat_ctx(part0[:, 0]), lat_ctx(part0[:, 1])
    grad_x = dx0[NC:][None]
    dmod = jnp.stack([jnp.concatenate(dmods[l], axis=-1) for l in range(L)])

    small_names = [k for k in SMALL if k not in ("c_ctx", "b_ada")]
    small_arrs = [jnp.stack(small[k]) for k in small_names]
    small_shapes = [a.shape for a in small_arrs]
    slab = _pack([loss_dev.reshape(1), jnp.zeros((LANE - 1,), F32), dmod] + small_arrs)
    g2 = _all_gather_small(slab, "gather_small")
    total = _slab_sum(g2, "slab_sum")
    flat_total = total.reshape(-1)
    loss = flat_total[0]
    nmod = L * 2 * 6 * D
    dmod_sum = flat_total[LANE:LANE + nmod].reshape(L, 2, 6 * D)
    small_grads = dict(zip(small_names, _unpack(flat_total[LANE + nmod:], small_shapes)))
    dmod_lat = g2.reshape(N_DEV, -1)[:, LANE:LANE + nmod].reshape(N_DEV, L, 2, 6 * D)[:, :, 1]
    dm16 = jnp.concatenate([jnp.transpose(dmod_lat, (1, 0, 2)), dmod_sum[:, 0:1],
                            jnp.zeros((L, 16 - N_DEV - 1, 6 * D), F32)], axis=1)
    small_grads["b_ada"] = dmod_sum[:, 0] + dmod_sum[:, 1]
    dm16_cols = lax.dynamic_slice_in_dim(dm16, chip * ns_ada, ns_ada, axis=2)
    grad_w_ada, ds16 = _ada_bwd(cvec, dm16_cols, W["w_ada"], "ada_bwd")
    g3 = _all_gather_small(ds16[N_DEV].reshape(-1, LANE), "gather_dsilu")
    small_grads["c_ctx"] = _cctx_grad(g3, W["c_ctx"].reshape(-1, LANE), "cctx_grad").reshape(D)
    conv_grad_full = small_grads["conv_w"]
    small_grads["conv_w"] = lax.dynamic_slice_in_dim(conv_grad_full, chip * GC, GC, axis=2)

    reduce_scatter(red_ffn, [g3])
    red_mix = reduce_begin(mix_keys, 0, "l0_mix")
    reduce_finish(red_l1, [red_ffn["token"], red_mix["token"]])
    reduce_scatter(red_mix, [grads_big[k] for k in BIG])
    adam(BIG, 1, [red_mix["token"]])
    reduce_finish(red_ffn, [red_mix["token"]])
    adam(ffn_keys, 0, [])

    delta, new_m, new_v = {}, {}, {}
    delta["w_ada"], new_m["w_ada"], new_v["w_ada"] = _adamw(W["w_ada"], grad_w_ada, M["w_ada"], V["w_ada"],
                                                            "adamw_w_ada")
    shapes = [W[k].shape for k in SMALL]
    sd, sm, sv = _adamw(_pack([W[k] for k in SMALL]), _pack([small_grads[k] for k in SMALL]),
                        _pack([M[k] for k in SMALL]), _pack([V[k] for k in SMALL]), "adamw_small")
    for k, d_, m_, v_ in zip(SMALL, _unpack(sd, shapes), _unpack(sm, shapes), _unpack(sv, shapes)):
        delta[k], new_m[k], new_v[k] = d_, m_, v_
    reduce_finish(red_mix, [delta["w_ada"], sd] + [adam_out[k][0] for k in BIG])
    adam(mix_keys, 0, [])
    for k in BIG:
        delta[k], new_m[k], new_v[k] = adam_out[k]
    grads = dict(grads_big)
    grads["w_ada"] = grad_w_ada
    grads.update(small_grads)
    return (loss, grad_x, *[grads[k] for k in WEIGHTS], *[delta[k] for k in WEIGHTS],
            *[new_m[k] for k in WEIGHTS], *[new_v[k] for k in WEIGHTS])


def kernel(x, c, ctx, c_ctx, w_ada, b_ada, w_in, q_norm_g, k_norm_g, pool_w, pool_scale, sgu_ln_g, sgu_ln_b, sgu_w, sgu_b, conv_w, w_br_attn, w_br_pool, w_br_sgu, w_br_conv, w_gate, b_gate, w_o, ln1_g, ln1_b, w_ff_gate, w_ff_up, w_ff_down, ln2_g, ln2_b, loss_target, m_c_ctx, m_w_ada, m_b_ada, m_w_in, m_q_norm_g, m_k_norm_g, m_pool_w, m_pool_scale, m_sgu_ln_g, m_sgu_ln_b, m_sgu_w, m_sgu_b, m_conv_w, m_w_br_attn, m_w_br_pool, m_w_br_sgu, m_w_br_conv, m_w_gate, m_b_gate, m_w_o, m_ln1_g, m_ln1_b, m_w_ff_gate, m_w_ff_up, m_w_ff_down, m_ln2_g, m_ln2_b, v_c_ctx, v_w_ada, v_b_ada, v_w_in, v_q_norm_g, v_k_norm_g, v_pool_w, v_pool_scale, v_sgu_ln_g, v_sgu_ln_b, v_sgu_w, v_sgu_b, v_conv_w, v_w_br_attn, v_w_br_pool, v_w_br_sgu, v_w_br_conv, v_w_gate, v_b_gate, v_w_o, v_ln1_g, v_ln1_b, v_w_ff_gate, v_w_ff_up, v_w_ff_down, v_ln2_g, v_ln2_b):
    args = locals()
    W = {k: args[k] for k in WEIGHTS}
    M = {k: args["m_" + k] for k in WEIGHTS}
    V = {k: args["v_" + k] for k in WEIGHTS}
    return _step(x, c, ctx, loss_target, W, M, V)
```

```python
import functools
import math

import jax
import jax.numpy as jnp
from jax import lax
from jax.experimental import pallas as pl
from jax.experimental.pallas import tpu as pltpu

F32 = jnp.float32
BF16 = jnp.bfloat16
MESH = pl.DeviceIdType.MESH

N_DEV = 8
N_CHIP = 4
GRID_W = 64
HEAD_DIM = 128
N_HEADS = 8
N_KV = 2
KV_GROUP = N_HEADS // N_KV
Q_W = N_HEADS * HEAD_DIM
KV_W = N_KV * HEAD_DIM
QKV_W = Q_W + 2 * KV_W
GC = 128
N_GROUP = 4
BR_W = N_GROUP * GC
POOL_WINDOWS = (2, 4, 8, 16)
OFF_POOL = QKV_W
OFF_U = OFF_POOL + BR_W
OFF_VG = OFF_U + BR_W
OFF_CB = OFF_VG + BR_W
OFF_CC = OFF_CB + BR_W
OFF_CX = OFF_CC + BR_W
IN_W = OFF_CX + BR_W
Q_BLOCK = 128
ROPE_THETA = 10000.0
LN_EPS = 1e-5
RMS_EPS = 1e-6
ADAM_LR = 0.001
ADAM_B1 = 0.9
ADAM_B2 = 0.999
ADAM_EPS = 1e-08
ADAM_WD = 0.01
ADAM_STEP = 10

ROW_TM = 256
LANE = 128
SUBLANE = 8
VMEM_BIG = 56 * 1024 * 1024
VMEM_MID = 40 * 1024 * 1024
NT_WHOLE_BYTES = 16 * 1024 * 1024


def _cp(sem=None, vmem=VMEM_MID):
    return pltpu.CompilerParams(dimension_semantics=sem, vmem_limit_bytes=vmem)


_ANY = pl.BlockSpec(memory_space=pl.ANY)
_HBM = pl.BlockSpec(memory_space=pltpu.HBM)
_SEM = pl.BlockSpec(memory_space=pltpu.SEMAPHORE)
_EFFECT = pltpu.SideEffectType.DATAFLOW_SIDE_EFFECTING


def _tile(n, pref, unit=LANE, whole_ok=False):
    if n <= pref:
        return n
    best = None
    for t in range(unit, pref + 1, unit):
        if n % t == 0:
            best = t
    if whole_ok and (best is None or best * 4 < pref):
        return n
    assert best is not None, (n, pref, unit)
    return best


def _layer_of(w, l):
    if isinstance(w, (list, tuple)):
        return w[l][:, None], 0
    return w, l


def _mm_nn(a, w, l, kind, *, name, out_dtype=F32, bias=None, act=None, tm_pref=768, tn_pref=1152):
    w, l = _layer_of(w, l)
    T, K = a.shape
    tm = _tile(T, tm_pref, SUBLANE * 2)
    if kind == "col":
        nq, _, kw, ns = w.shape
        assert kw == K
        tn = _tile(ns, tn_pref, whole_ok=True)
        nj = ns // tn
        n_total = nq * ns
        w_spec = pl.BlockSpec((None, None, K, tn), lambda j, i: (j // nj, l, 0, j % nj))
        grid_n = nq * nj
    else:
        nq, _, kc, n_total = w.shape
        assert nq * kc == K
        tn = _tile(n_total, tn_pref)
        w_spec = pl.BlockSpec((nq, None, kc, tn), lambda j, i: (0, l, 0, j))
        grid_n = n_total // tn

    def body(*refs):
        if bias is not None:
            a_ref, w_ref, b_ref, o_ref = refs
        else:
            a_ref, w_ref, o_ref = refs
        wv = w_ref[...]
        if kind == "row":
            wv = wv.reshape(K, tn)
        acc = jnp.dot(a_ref[...], wv, preferred_element_type=F32)
        if bias is not None:
            acc = acc + b_ref[...]
        if act is not None:
            acc = act(acc)
        o_ref[...] = acc.astype(out_dtype)

    in_specs = [pl.BlockSpec((tm, K), lambda j, i: (i, 0)), w_spec]
    args = [a, w]
    if bias is not None:
        in_specs.append(pl.BlockSpec((1, tn), lambda j, i: (0, j)))
        args.append(bias)
    return pl.pallas_call(
        body, grid=(grid_n, T // tm), in_specs=in_specs,
        out_specs=pl.BlockSpec((tm, tn), lambda j, i: (i, j)),
        out_shape=jax.ShapeDtypeStruct((T, n_total), out_dtype),
        name=name, compiler_params=_cp(("parallel", "parallel"), VMEM_BIG),
    )(*args)


def _mm_nt(dc, w, l, kind, *, name, out_dtype=F32, dc_off=0, tm_pref=768, tk_pref=1024):
    w, l = _layer_of(w, l)
    T = dc.shape[0]
    tm = _tile(T, tm_pref, SUBLANE * 2)
    if kind == "col":
        nq, _, K, ns = w.shape
        assert dc_off % ns == 0
        offb = dc_off // ns
        tks = _tile(K, 512)
        whole_bytes = 2 * (tm * nq * ns + nq * tks * ns)
        if dc_off == 0 and dc.shape[1] == nq * ns and whole_bytes <= NT_WHOLE_BYTES:
            def body_whole(dc_ref, w_ref, o_ref):
                acc = None
                for q in range(nq):
                    part = lax.dot_general(dc_ref[:, q * ns:(q + 1) * ns], w_ref[q], (((1,), (1,)), ((), ())),
                                           preferred_element_type=F32)
                    acc = part if acc is None else acc + part
                o_ref[...] = acc.astype(out_dtype)

            return pl.pallas_call(
                body_whole, grid=(T // tm, K // tks),
                in_specs=[pl.BlockSpec((tm, nq * ns), lambda i, kk: (i, 0)),
                          pl.BlockSpec((nq, None, tks, ns), lambda i, kk: (0, l, kk, 0))],
                out_specs=pl.BlockSpec((tm, tks), lambda i, kk: (i, kk)),
                out_shape=jax.ShapeDtypeStruct((T, K), out_dtype),
                name=name, compiler_params=_cp(("parallel", "parallel"), VMEM_BIG),
            )(dc, w)
        tk = _tile(K, tk_pref)

        def body(dc_ref, w_ref, o_ref, acc_ref):
            q = pl.program_id(2)
            part = lax.dot_general(dc_ref[...], w_ref[...], (((1,), (1,)), ((), ())),
                                   preferred_element_type=F32)

            @pl.when(q == 0)
            def _():
                acc_ref[...] = part

            @pl.when(q > 0)
            def _():
                acc_ref[...] += part

            @pl.when(q == nq - 1)
            def _():
                o_ref[...] = acc_ref[...].astype(out_dtype)

        return pl.pallas_call(
            body, grid=(K // tk, T // tm, nq),
            in_specs=[pl.BlockSpec((tm, ns), lambda kk, i, q: (i, offb + q)),
                      pl.BlockSpec((None, None, tk, ns), lambda kk, i, q: (q, l, kk, 0))],
            out_specs=pl.BlockSpec((tm, tk), lambda kk, i, q: (i, kk)),
            out_shape=jax.ShapeDtypeStruct((T, K), out_dtype),
            scratch_shapes=[pltpu.VMEM((tm, tk), F32)],
            name=name, compiler_params=_cp(("parallel", "parallel", "arbitrary"), VMEM_BIG),
        )(dc, w)

    nq, _, kc, n = w.shape
    assert dc_off % n == 0
    offb = dc_off // n
    tk = _tile(kc, tk_pref, whole_ok=True)
    nkk = kc // tk

    def body(dc_ref, w_ref, o_ref):
        o_ref[...] = lax.dot_general(dc_ref[...], w_ref[...], (((1,), (1,)), ((), ())),
                                     preferred_element_type=F32).astype(out_dtype)

    return pl.pallas_call(
        body, grid=(nq * nkk, T // tm),
        in_specs=[pl.BlockSpec((tm, n), lambda j, i: (i, offb)),
                  pl.BlockSpec((None, None, tk, n), lambda j, i: (j // nkk, l, j % nkk, 0))],
        out_specs=pl.BlockSpec((tm, tk), lambda j, i: (i, j)),
        out_shape=jax.ShapeDtypeStruct((T, nq * kc), out_dtype),
        name=name, compiler_params=_cp(("parallel", "parallel"), VMEM_BIG),
    )(dc, w)


def _mm_tn(a, dc, kind, *, nq, kdim, ndim, name, a_off=0, dc_off=0, tk_pref=512, tn_pref=1152, out_dtype=BF16):
    T = a.shape[0]
    assert dc.shape[0] == T
    tk = _tile(kdim, tk_pref, whole_ok=True)
    tn = _tile(ndim, tn_pref, whole_ok=True)
    nkk, njn = kdim // tk, ndim // tn
    assert a_off % tk == 0 and dc_off % tn == 0
    aoffb, doffb = a_off // tk, dc_off // tn
    if kind == "col":
        a_map = lambda q, kk, jn: (0, aoffb + kk)
        d_map = lambda q, kk, jn: (0, doffb + q * njn + jn)
    else:
        a_map = lambda q, kk, jn: (0, aoffb + q * nkk + kk)
        d_map = lambda q, kk, jn: (0, doffb + jn)

    def body(a_ref, d_ref, o_ref):
        o_ref[...] = lax.dot_general(a_ref[...], d_ref[...], (((0,), (0,)), ((), ())),
                                     preferred_element_type=F32).astype(out_dtype)

    return pl.pallas_call(
        body, grid=(nq, nkk, njn),
        in_specs=[pl.BlockSpec((T, tk), a_map), pl.BlockSpec((T, tn), d_map)],
        out_specs=pl.BlockSpec((None, tk, tn), lambda q, kk, jn: (q, kk, jn)),
        out_shape=jax.ShapeDtypeStruct((nq, kdim, ndim), out_dtype),
        name=name, compiler_params=_cp(("parallel", "parallel", "parallel"), VMEM_BIG),
    )(a, dc)


def _row_spec(d):
    return pl.BlockSpec((ROW_TM, d), lambda i: (i, 0))


def _mod_spec(d, ncb):
    return pl.BlockSpec((None, 1, d), lambda i: (jnp.where(i >= ncb, 1, 0), 0, 0))


def _vec_spec(d):
    return pl.BlockSpec((1, d), lambda i: (0, 0))


def _part_spec(d):
    return pl.BlockSpec((None, SUBLANE, d), lambda i: (i, 0, 0))


def _modulate(x, sc, sh, ncb, name):
    T, D = x.shape

    def body(x_ref, sc_ref, sh_ref, o_ref):
        o_ref[...] = (x_ref[...] * (1.0 + sc_ref[...]) + sh_ref[...]).astype(BF16)

    return pl.pallas_call(
        body, grid=(T // ROW_TM,),
        in_specs=[_row_spec(D), _mod_spec(D, ncb), _mod_spec(D, ncb)],
        out_specs=_row_spec(D), out_shape=jax.ShapeDtypeStruct((T, D), BF16),
        name=name, compiler_params=_cp(("parallel",)),
    )(x, sc, sh)


def _ln_stats(r):
    mu = jnp.mean(r, axis=-1, keepdims=True)
    rc = r - mu
    var = jnp.mean(rc * rc, axis=-1, keepdims=True)
    rstd = lax.rsqrt(var + LN_EPS)
    return rc * rstd, rstd


def _resid_ln(x, y, gate, g, b, sc, sh, ncb, alpha, name):
    T, D = x.shape

    def body(x_ref, y_ref, gate_ref, g_ref, b_ref, sc_ref, sh_ref, xo_ref, h_ref):
        xhat, _ = _ln_stats(alpha * x_ref[...] + gate_ref[...] * y_ref[...])
        xo = xhat * g_ref[...] + b_ref[...]
        xo_ref[...] = xo
        h_ref[...] = (xo * (1.0 + sc_ref[...]) + sh_ref[...]).astype(BF16)

    return pl.pallas_call(
        body, grid=(T // ROW_TM,),
        in_specs=[_row_spec(D), _row_spec(D), _mod_spec(D, ncb), _vec_spec(D), _vec_spec(D),
                  _mod_spec(D, ncb), _mod_spec(D, ncb)],
        out_specs=[_row_spec(D), _row_spec(D)],
        out_shape=[jax.ShapeDtypeStruct((T, D), F32), jax.ShapeDtypeStruct((T, D), BF16)],
        name=name, compiler_params=_cp(("parallel",)),
    )(x, y, gate, g, b, sc, sh)


def _resid_ln_loss(x, y, gate, g, b, target, ncb, alpha, name):
    T, D = x.shape
    nblk = T // ROW_TM

    def body(x_ref, y_ref, gate_ref, g_ref, b_ref, t_ref, dy_ref, loss_ref):
        i = pl.program_id(0)
        xhat, _ = _ln_stats(alpha * x_ref[...] + gate_ref[...] * y_ref[...])
        xo = xhat * g_ref[...] + b_ref[...]
        live = (i >= ncb).astype(F32)
        err = (xo - t_ref[...]) * live
        dy_ref[...] = err * (1.0 / D)
        loss_ref[...] = jnp.full((SUBLANE, LANE), 0.5 / D, F32) * jnp.sum(err * err)

    return pl.pallas_call(
        body, grid=(nblk,),
        in_specs=[_row_spec(D), _row_spec(D), _mod_spec(D, ncb), _vec_spec(D), _vec_spec(D),
                  pl.BlockSpec((ROW_TM, D), lambda i: (jnp.maximum(i - ncb, 0), 0))],
        out_specs=[_row_spec(D), _part_spec(LANE)],
        out_shape=[jax.ShapeDtypeStruct((T, D), F32), jax.ShapeDtypeStruct((nblk, SUBLANE, LANE), F32)],
        name=name, compiler_params=_cp(("parallel",)),
    )(x, y, gate, g, b, target)


def _write_parts(part_ref, rows, d):
    for k, r in enumerate(rows):
        part_ref[pl.ds(k, 1), :] = jnp.sum(r, axis=0, keepdims=True)
    if len(rows) < SUBLANE:
        part_ref[pl.ds(len(rows), SUBLANE - len(rows)), :] = jnp.zeros((SUBLANE - len(rows), d), F32)


def _ln_bwd(dxa, dhs, sc, x, y, gate, g, b, ncb, alpha, name, after=None):
    T, D = x.shape
    nblk = T // ROW_TM
    ndh = len(dhs)

    def body(*refs):
        dxa_ref = refs[0]
        dh_refs = refs[1:1 + ndh]
        sc_ref, x_ref, y_ref, gate_ref, g_ref, b_ref = refs[1 + ndh:7 + ndh]
        dx_ref, dy_ref, part_ref = refs[-3:]
        yv = y_ref[...]
        xhat, rstd = _ln_stats(alpha * x_ref[...] + gate_ref[...] * yv)
        dxo = dxa_ref[...]
        rows = []
        if ndh:
            dh = dh_refs[0][...]
            for r in dh_refs[1:]:
                dh = dh + r[...]
            dxo = dxo + dh * (1.0 + sc_ref[...])
            xo = xhat * g_ref[...] + b_ref[...]
            rows = [dh * xo, dh]
        dxhat = dxo * g_ref[...]
        m1 = jnp.mean(dxhat, axis=-1, keepdims=True)
        m2 = jnp.mean(dxhat * xhat, axis=-1, keepdims=True)
        dr = rstd * (dxhat - m1 - xhat * m2)
        dx_ref[...] = alpha * dr
        dy_ref[...] = (gate_ref[...] * dr).astype(BF16)
        _write_parts(part_ref, [dxo * xhat, dxo, dr * yv] + rows, D)

    in_specs = ([_row_spec(D)] * (1 + ndh)
                + [_mod_spec(D, ncb), _row_spec(D), _row_spec(D), _mod_spec(D, ncb), _vec_spec(D), _vec_spec(D)])
    extra = list(after or ())
    return pl.pallas_call(
        body, grid=(nblk,), in_specs=in_specs + [_ANY] * len(extra),
        out_specs=[_row_spec(D), _row_spec(D), _part_spec(D)],
        out_shape=[jax.ShapeDtypeStruct((T, D), F32), jax.ShapeDtypeStruct((T, D), BF16),
                   jax.ShapeDtypeStruct((nblk, SUBLANE, D), F32)],
        name=name, compiler_params=_cp(("parallel",)),
    )(dxa, *dhs, sc, x, y, gate, g, b, *extra)


def _mod_bwd(dxa, dhs, sc, x, ncb, name, after=()):
    T, D = x.shape
    nblk = T // ROW_TM
    ndh = len(dhs)

    def body(*refs):
        dxa_ref = refs[0]
        dh_refs = refs[1:1 + ndh]
        sc_ref, x_ref = refs[1 + ndh:3 + ndh]
        dx_ref, part_ref = refs[-2:]
        dh = dh_refs[0][...]
        for r in dh_refs[1:]:
            dh = dh + r[...]
        dx_ref[...] = dxa_ref[...] + dh * (1.0 + sc_ref[...])
        _write_parts(part_ref, [dh * x_ref[...], dh], D)

    return pl.pallas_call(
        body, grid=(nblk,),
        in_specs=[_row_spec(D)] * (1 + ndh) + [_mod_spec(D, ncb), _row_spec(D)] + [_ANY] * len(after),
        out_specs=[_row_spec(D), _part_spec(D)],
        out_shape=[jax.ShapeDtypeStruct((T, D), F32), jax.ShapeDtypeStruct((nblk, SUBLANE, D), F32)],
        name=name, compiler_params=_cp(("parallel",)),
    )(dxa, *dhs, sc, x, *after)


def _merge_fwd(gates, ts, name):
    T, D = ts[0].shape
    tm = 128

    def body(g_ref, t0, t1, t2, t3, o_ref):
        acc = g_ref[:, 0:D].astype(F32) * t0[...].astype(F32)
        for k, t in enumerate((t1, t2, t3), start=1):
            acc = acc + g_ref[:, k * D:(k + 1) * D].astype(F32) * t[...].astype(F32)
        o_ref[...] = acc.astype(BF16)

    rs = pl.BlockSpec((tm, D), lambda i: (i, 0))
    return pl.pallas_call(
        body, grid=(T // tm,),
        in_specs=[pl.BlockSpec((tm, 4 * D), lambda i: (i, 0)), rs, rs, rs, rs],
        out_specs=rs, out_shape=jax.ShapeDtypeStruct((T, D), BF16),
        name=name, compiler_params=_cp(("parallel",)),
    )(gates, *ts)


def _merge_bwd(dmg, gates, ts, name):
    T, D = dmg.shape
    tm = 128
    nblk = T // tm

    def body(d_ref, g_ref, t0, t1, t2, t3, dpre_ref, dt0, dt1, dt2, dt3, part_ref):
        d = d_ref[...]
        for k, (t, dt) in enumerate(zip((t0, t1, t2, t3), (dt0, dt1, dt2, dt3))):
            gk = g_ref[:, k * D:(k + 1) * D].astype(F32)
            dt[...] = (d * gk).astype(BF16)
            dpre = d * t[...].astype(F32) * gk * (1.0 - gk)
            dpre_ref[:, k * D:(k + 1) * D] = dpre.astype(BF16)
            part_ref[:, k * D:(k + 1) * D] = jnp.sum(dpre, axis=0, keepdims=True)

    rs = pl.BlockSpec((tm, D), lambda i: (i, 0))
    wide = pl.BlockSpec((tm, 4 * D), lambda i: (i, 0))
    return pl.pallas_call(
        body, grid=(nblk,),
        in_specs=[rs, wide, rs, rs, rs, rs],
        out_specs=[wide, rs, rs, rs, rs, pl.BlockSpec((None, 1, 4 * D), lambda i: (i, 0, 0))],
        out_shape=[jax.ShapeDtypeStruct((T, 4 * D), BF16)] + [jax.ShapeDtypeStruct((T, D), BF16)] * 4
                  + [jax.ShapeDtypeStruct((nblk, 1, 4 * D), F32)],
        name=name, compiler_params=_cp(("parallel",)),
    )(dmg, gates, *ts)


def _ffn_up(a, wg, wu, name, tm_pref=384):
    T, K = a.shape
    nq, _, ns = wg.shape
    tm = _tile(T, tm_pref, SUBLANE * 2)

    def body(a_ref, g_ref, u_ref, gg_ref, uu_ref, act_ref):
        av = a_ref[...]
        g = jnp.dot(av, g_ref[...], preferred_element_type=F32)
        u = jnp.dot(av, u_ref[...], preferred_element_type=F32)
        gg_ref[...] = g.astype(BF16)
        uu_ref[...] = u.astype(BF16)
        act_ref[...] = (g * jax.nn.sigmoid(g) * u).astype(BF16)

    ws = pl.BlockSpec((None, K, ns), lambda q, i: (q, 0, 0))
    os = pl.BlockSpec((tm, ns), lambda q, i: (i, q))
    return pl.pallas_call(
        body, grid=(nq, T // tm),
        in_specs=[pl.BlockSpec((tm, K), lambda q, i: (i, 0)), ws, ws], out_specs=[os, os, os],
        out_shape=[jax.ShapeDtypeStruct((T, nq * ns), BF16)] * 3,
        name=name, compiler_params=_cp(("parallel", "parallel"), VMEM_BIG),
    )(a, wg, wu)


def _ffn_down_bwd(dff, wd, gg, uu, name, tm_pref=384):
    T, N = dff.shape
    nq, kc, _ = wd.shape
    tm = _tile(T, tm_pref, SUBLANE * 2)

    def body(d_ref, w_ref, g_ref, u_ref, dg_ref, du_ref):
        da = lax.dot_general(d_ref[...], w_ref[...], (((1,), (1,)), ((), ())), preferred_element_type=F32)
        g = g_ref[...].astype(F32)
        sig = jax.nn.sigmoid(g)
        dg_ref[...] = (da * u_ref[...].astype(F32) * sig * (1.0 + g * (1.0 - sig))).astype(BF16)
        du_ref[...] = (da * g * sig).astype(BF16)

    ts = pl.BlockSpec((tm, kc), lambda q, i: (i, q))
    return pl.pallas_call(
        body, grid=(nq, T // tm),
        in_specs=[pl.BlockSpec((tm, N), lambda q, i: (i, 0)), pl.BlockSpec((None, kc, N), lambda q, i: (q, 0, 0)),
                  ts, ts],
        out_specs=[ts, ts], out_shape=[jax.ShapeDtypeStruct((T, nq * kc), BF16)] * 2,
        name=name, compiler_params=_cp(("parallel", "parallel"), VMEM_BIG),
    )(dff, wd, gg, uu)


def _swap_halves(v):
    lane = lax.broadcasted_iota(jnp.int32, v.shape, 1)
    return jnp.where((lane % 64) < 32, pltpu.roll(v, 96, 1), pltpu.roll(v, 32, 1))


def _rope_tables(n, nc):
    rows = n // GRID_W
    row = jnp.repeat(jnp.arange(rows), GRID_W).astype(F32)
    col = jnp.tile(jnp.arange(GRID_W), rows).astype(F32)
    inv = ROPE_THETA ** (-jnp.arange(0, 64, 2, dtype=F32) / 64)
    ang_r = row[:, None] * inv
    ang_c = col[:, None] * inv
    cos = jnp.concatenate([jnp.cos(ang_r), jnp.cos(ang_r), jnp.cos(ang_c), jnp.cos(ang_c)], axis=-1)
    sin = jnp.concatenate([-jnp.sin(ang_r), jnp.sin(ang_r), -jnp.sin(ang_c), jnp.sin(ang_c)], axis=-1)
    cos = jnp.concatenate([jnp.ones((nc, HEAD_DIM), F32), cos], axis=0)
    sin = jnp.concatenate([jnp.zeros((nc, HEAD_DIM), F32), sin], axis=0)
    return cos, sin


def _qk_fwd(p, cos, sin, qg, kg, name):
    T = p.shape[0]

    def body(p_ref, c_ref, s_ref, qg_ref, kg_ref, q_ref, k_ref, v_ref):
        c, s = c_ref[...], s_ref[...]
        for h in range(N_HEADS + N_KV):
            x = p_ref[:, h * HEAD_DIM:(h + 1) * HEAD_DIM]
            rs = lax.rsqrt(jnp.mean(x * x, axis=-1, keepdims=True) + RMS_EPS)
            gain = qg_ref[...] if h < N_HEADS else kg_ref[...]
            yv = x * rs * gain
            out = yv * c + _swap_halves(yv) * s
            if h < N_HEADS:
                out = out * Q_PRESCALE
            out = out.astype(BF16)
            if h < N_HEADS:
                q_ref[:, h * HEAD_DIM:(h + 1) * HEAD_DIM] = out
            else:
                k_ref[:, (h - N_HEADS) * HEAD_DIM:(h - N_HEADS + 1) * HEAD_DIM] = out
        v_ref[...] = p_ref[:, Q_W + KV_W:QKV_W].astype(BF16)

    return pl.pallas_call(
        body, grid=(T // ROW_TM,),
        in_specs=[_row_spec(QKV_W), _row_spec(HEAD_DIM), _row_spec(HEAD_DIM), _vec_spec(HEAD_DIM), _vec_spec(HEAD_DIM)],
        out_specs=[_row_spec(Q_W), _row_spec(KV_W), _row_spec(KV_W)],
        out_shape=[jax.ShapeDtypeStruct((T, Q_W), BF16), jax.ShapeDtypeStruct((T, KV_W), BF16),
                   jax.ShapeDtypeStruct((T, KV_W), BF16)],
        name=name, compiler_params=_cp(("parallel",)),
    )(p, cos, sin, qg, kg)


def _qk_bwd(p, dq, dk, dv, cos, sin, qg, kg, name):
    T = p.shape[0]
    nblk = T // ROW_TM

    def body(p_ref, dq_ref, dk_ref, dv_ref, c_ref, s_ref, qg_ref, kg_ref, dp_ref, part_ref):
        c, s = c_ref[...], s_ref[...]
        dgq = jnp.zeros((1, HEAD_DIM), F32)
        dgk = jnp.zeros((1, HEAD_DIM), F32)
        for h in range(N_HEADS + N_KV):
            x = p_ref[:, h * HEAD_DIM:(h + 1) * HEAD_DIM]
            if h < N_HEADS:
                d = dq_ref[:, h * HEAD_DIM:(h + 1) * HEAD_DIM]
                gain = qg_ref[...]
            else:
                d = dk_ref[:, (h - N_HEADS) * HEAD_DIM:(h - N_HEADS + 1) * HEAD_DIM]
                gain = kg_ref[...]
            dyv = d * c + _swap_halves(d * s)
            rs = lax.rsqrt(jnp.mean(x * x, axis=-1, keepdims=True) + RMS_EPS)
            xn = x * rs
            dgsum = jnp.sum(dyv * xn, axis=0, keepdims=True)
            if h < N_HEADS:
                dgq = dgq + dgsum
            else:
                dgk = dgk + dgsum
            dxg = dyv * gain
            dx = rs * (dxg - xn * jnp.mean(dxg * xn, axis=-1, keepdims=True))
            dp_ref[:, h * HEAD_DIM:(h + 1) * HEAD_DIM] = dx.astype(BF16)
        dp_ref[:, Q_W + KV_W:QKV_W] = dv_ref[...].astype(BF16)
        part_ref[pl.ds(0, 1), :] = dgq
        part_ref[pl.ds(1, 1), :] = dgk
        part_ref[pl.ds(2, SUBLANE - 2), :] = jnp.zeros((SUBLANE - 2, HEAD_DIM), F32)

    return pl.pallas_call(
        body, grid=(nblk,),
        in_specs=[_row_spec(QKV_W), _row_spec(Q_W), _row_spec(KV_W), _row_spec(KV_W),
                  _row_spec(HEAD_DIM), _row_spec(HEAD_DIM), _vec_spec(HEAD_DIM), _vec_spec(HEAD_DIM)],
        out_specs=[_row_spec(QKV_W), _part_spec(HEAD_DIM)],
        out_shape=[jax.ShapeDtypeStruct((T, IN_W), BF16), jax.ShapeDtypeStruct((nblk, SUBLANE, HEAD_DIM), F32)],
        name=name, compiler_params=_cp(("parallel",)),
    )(p, dq, dk, dv, cos, sin, qg, kg)


ATTN_SCALE = HEAD_DIM ** -0.5
LOG2E = 1.4426950408889634
Q_PRESCALE = ATTN_SCALE * LOG2E


def _attn_weights(q, k):
    s = lax.dot_general(q, k, (((1,), (1,)), ((), ())), preferred_element_type=F32)
    e = jnp.exp2(s - jnp.max(s, axis=-1, keepdims=True))
    return e, 1.0 / jnp.sum(e, axis=-1, keepdims=True)


def _attn_fwd(q, k, v, nc, name):
    T = q.shape[0]

    def heads(q_ref, k_ref, v_ref, o_ref, nkeys):
        for h in range(N_HEADS):
            g = h // KV_GROUP
            kk = k_ref[0:nkeys, g * HEAD_DIM:(g + 1) * HEAD_DIM]
            vv = v_ref[0:nkeys, g * HEAD_DIM:(g + 1) * HEAD_DIM]
            e, rl = _attn_weights(q_ref[:, h * HEAD_DIM:(h + 1) * HEAD_DIM], kk)
            o = jnp.dot(e.astype(BF16), vv, preferred_element_type=F32) * rl
            o_ref[:, h * HEAD_DIM:(h + 1) * HEAD_DIM] = o.astype(BF16)

    def body(q_ref, k_ref, v_ref, o_ref):
        i = pl.program_id(0)

        @pl.when(i < nc // Q_BLOCK)
        def _():
            heads(q_ref, k_ref, v_ref, o_ref, nc)

        @pl.when(i >= nc // Q_BLOCK)
        def _():
            heads(q_ref, k_ref, v_ref, o_ref, T)

    whole = pl.BlockSpec((T, KV_W), lambda i: (0, 0))
    qs = pl.BlockSpec((Q_BLOCK, Q_W), lambda i: (i, 0))
    return pl.pallas_call(
        body, grid=(T // Q_BLOCK,), in_specs=[qs, whole, whole], out_specs=qs,
        out_shape=jax.ShapeDtypeStruct((T, Q_W), BF16),
        name=name, compiler_params=_cp(("parallel",)),
    )(q, k, v)


def _attn_bwd(q, k, v, do, nc, name):
    T = q.shape[0]

    def heads(q_ref, k_ref, v_ref, do_ref, dq_ref, dk_ref, dv_ref, nkeys):
        for h in range(N_HEADS):
            g = h // KV_GROUP
            cols = slice(g * HEAD_DIM, (g + 1) * HEAD_DIM)
            hc = slice(h * HEAD_DIM, (h + 1) * HEAD_DIM)
            qh, kk, vv, doh = q_ref[:, hc], k_ref[0:nkeys, cols], v_ref[0:nkeys, cols], do_ref[:, hc]
            e, rl = _attn_weights(qh, kk)
            dpr = lax.dot_general(doh, vv, (((1,), (1,)), ((), ())), preferred_element_type=F32)
            delta = jnp.sum(e * dpr, axis=-1, keepdims=True) * rl
            dsu = (e * (dpr - delta)).astype(BF16)
            dq_ref[:, hc] = jnp.dot(dsu, kk, preferred_element_type=F32) * (rl * ATTN_SCALE)
            q_rows = (qh.astype(F32) * (rl * (1.0 / LOG2E))).astype(BF16)
            dk_ref[0:nkeys, cols] += lax.dot_general(dsu, q_rows, (((0,), (0,)), ((), ())),
                                                     preferred_element_type=F32)
            do_rows = (doh.astype(F32) * rl).astype(BF16)
            dv_ref[0:nkeys, cols] += lax.dot_general(e.astype(BF16), do_rows, (((0,), (0,)), ((), ())),
                                                     preferred_element_type=F32)

    def body(q_ref, k_ref, v_ref, do_ref, dq_ref, dk_ref, dv_ref):
        i = pl.program_id(0)

        @pl.when(i == 0)
        def _():
            dk_ref[...] = jnp.zeros_like(dk_ref)
            dv_ref[...] = jnp.zeros_like(dv_ref)

        @pl.when(i < nc // Q_BLOCK)
        def _():
            heads(q_ref, k_ref, v_ref, do_ref, dq_ref, dk_ref, dv_ref, nc)

        @pl.when(i >= nc // Q_BLOCK)
        def _():
            heads(q_ref, k_ref, v_ref, do_ref, dq_ref, dk_ref, dv_ref, T)

    whole = pl.BlockSpec((T, KV_W), lambda i: (0, 0))
    qs = pl.BlockSpec((Q_BLOCK, Q_W), lambda i: (i, 0))
    return pl.pallas_call(
        body, grid=(T // Q_BLOCK,), in_specs=[qs, whole, whole, qs], out_specs=[qs, whole, whole],
        out_shape=[jax.ShapeDtypeStruct((T, Q_W), F32), jax.ShapeDtypeStruct((T, KV_W), F32),
                   jax.ShapeDtypeStruct((T, KV_W), F32)],
        name=name, compiler_params=_cp(("arbitrary",)),
    )(q, k, v, do)


def _shift_rows(z, o, nc):
    T = z.shape[0]
    t = lax.broadcasted_iota(jnp.int32, (T, 1), 0)
    lo = jnp.where(t < nc, 0, nc)
    hi = jnp.where(t < nc, nc, T)
    ok = jnp.logical_and(t + o >= lo, t + o < hi)
    rolled = z if o == 0 else pltpu.roll(z, (-o) % T, 0)
    return jnp.where(ok, rolled, 0.0), ok


def _pool_window(w):
    left = w // 2
    return -left, w - 1 - left


def _pool_d(z, w, nc):
    o0, o1 = _pool_window(w)
    tot = jnp.zeros_like(z)
    cnt = jnp.zeros((z.shape[0], 1), F32)
    for o in range(o0, o1 + 1):
        sh, ok = _shift_rows(z, o, nc)
        tot = tot + sh
        cnt = cnt + ok.astype(F32)
    return tot / cnt - z, cnt


def _pool_fwd(p, pool_w, pool_scale, nc, name):
    T = p.shape[0]

    def body(z_ref, w_ref, s_ref, o_ref):
        for g, w in enumerate(POOL_WINDOWS):
            cs = slice(g * GC, (g + 1) * GC)
            d, _ = _pool_d(z_ref[:, cs], w, nc)
            yv = jnp.dot(d.astype(BF16), w_ref[g].astype(BF16), preferred_element_type=F32)
            o_ref[:, cs] = (yv * s_ref[:, cs]).astype(BF16)

    return pl.pallas_call(
        body, grid=(1,),
        in_specs=[pl.BlockSpec((T, BR_W), lambda i: (0, OFF_POOL // BR_W)),
                  pl.BlockSpec((N_GROUP, GC, GC), lambda i: (0, 0, 0)), _vec_spec(BR_W)],
        out_specs=pl.BlockSpec((T, BR_W), lambda i: (0, 0)),
        out_shape=jax.ShapeDtypeStruct((T, BR_W), BF16),
        name=name, compiler_params=_cp(("arbitrary",), VMEM_BIG),
    )(p, pool_w, pool_scale)


def _pool_bwd(p, dy, pool_w, pool_scale, dp, nc, name):
    T = p.shape[0]

    def body(z_ref, dy_ref, w_ref, s_ref, dp_in_ref, dz_ref, dw_ref, ds_ref):
        for g, w in enumerate(POOL_WINDOWS):
            cs = slice(g * GC, (g + 1) * GC)
            d, cnt = _pool_d(z_ref[:, cs], w, nc)
            db = d.astype(BF16)
            wb = w_ref[g].astype(BF16)
            dyv = dy_ref[:, cs]
            e = (dyv * s_ref[:, cs]).astype(BF16)
            dw_ref[g] = lax.dot_general(db, e, (((0,), (0,)), ((), ())), preferred_element_type=F32)
            yraw = jnp.dot(db, wb, preferred_element_type=F32)
            ds_ref[:, cs] = jnp.sum(dyv * yraw, axis=0, keepdims=True)
            dd = lax.dot_general(e, wb, (((1,), (1,)), ((), ())), preferred_element_type=F32)
            ec = dd / cnt
            o0, o1 = _pool_window(w)
            tot = jnp.zeros_like(dd)
            for o in range(o0, o1 + 1):
                tot = tot + _shift_rows(ec, -o, nc)[0]
            dz_ref[:, cs] = (tot - dd).astype(BF16)

    return pl.pallas_call(
        body, grid=(1,),
        in_specs=[pl.BlockSpec((T, BR_W), lambda i: (0, OFF_POOL // BR_W)),
                  pl.BlockSpec((T, BR_W), lambda i: (0, 0)),
                  pl.BlockSpec((N_GROUP, GC, GC), lambda i: (0, 0, 0)), _vec_spec(BR_W), _ANY],
        out_specs=[pl.BlockSpec((T, BR_W), lambda i: (0, OFF_POOL // BR_W)),
                   pl.BlockSpec((N_GROUP, GC, GC), lambda i: (0, 0, 0)), _vec_spec(BR_W)],
        out_shape=[jax.ShapeDtypeStruct(dp.shape, BF16), jax.ShapeDtypeStruct((N_GROUP, GC, GC), F32),
                   jax.ShapeDtypeStruct((1, BR_W), F32)],
        input_output_aliases={4: 0},
        name=name, compiler_params=_cp(("arbitrary",), VMEM_BIG),
    )(p, dy, pool_w, pool_scale, dp)


_GELU_C = math.sqrt(2.0 / math.pi)


def _gelu(x):
    return 0.5 * x * (1.0 + jnp.tanh(_GELU_C * (x + 0.044715 * x * x * x)))


def _gelu_grad(x):
    th = jnp.tanh(_GELU_C * (x + 0.044715 * x * x * x))
    return 0.5 * (1.0 + th) + 0.5 * x * (1.0 - th * th) * _GELU_C * (1.0 + 3.0 * 0.044715 * x * x)


def _sgu_fwd(p, ln_g, ln_b, w_s, b_st, name):
    T = p.shape[0]

    def body(pu_ref, pv_ref, g_ref, b_ref, w_ref, bs_ref, o_ref):
        u = _gelu(pu_ref[...])
        vhat, _ = _ln_stats(_gelu(pv_ref[...]))
        vn = (vhat * g_ref[...] + b_ref[...]).astype(BF16)
        for g in range(N_GROUP):
            cs = slice(g * GC, (g + 1) * GC)
            s = jnp.dot(w_ref[g].astype(BF16), vn[:, cs], preferred_element_type=F32) + bs_ref[:, g:g + 1]
            o_ref[:, cs] = (u[:, cs] * s).astype(BF16)

    chunk = lambda off: pl.BlockSpec((GC, BR_W), lambda i: (i, off // BR_W))
    return pl.pallas_call(
        body, grid=(T // GC,),
        in_specs=[chunk(OFF_U), chunk(OFF_VG), _vec_spec(BR_W), _vec_spec(BR_W),
                  pl.BlockSpec((N_GROUP, GC, GC), lambda i: (0, 0, 0)),
                  pl.BlockSpec((GC, N_GROUP), lambda i: (0, 0))],
        out_specs=pl.BlockSpec((GC, BR_W), lambda i: (i, 0)),
        out_shape=jax.ShapeDtypeStruct((T, BR_W), BF16),
        name=name, compiler_params=_cp(("parallel",)),
    )(p, p, ln_g, ln_b, w_s, b_st)


def _sgu_bwd(p, dy, ln_g, ln_b, w_s, b_st, dp, name):
    T = p.shape[0]

    def body(pu_ref, pv_ref, dy_ref, g_ref, b_ref, w_ref, bs_ref, dp_in_ref, dp_ref, dw_ref, dsacc_ref, dln_ref):
        i = pl.program_id(0)

        @pl.when(i == 0)
        def _():
            dw_ref[...] = jnp.zeros_like(dw_ref)
            dsacc_ref[...] = jnp.zeros_like(dsacc_ref)
            dln_ref[...] = jnp.zeros_like(dln_ref)

        pu, pv, dyv = pu_ref[...], pv_ref[...], dy_ref[...]
        u = _gelu(pu)
        vhat, rstd = _ln_stats(_gelu(pv))
        vn = (vhat * g_ref[...] + b_ref[...]).astype(BF16)
        ds = dyv * u
        dsb = ds.astype(BF16)
        dsacc_ref[...] += ds
        dvn_parts = []
        for g in range(N_GROUP):
            cs = slice(g * GC, (g + 1) * GC)
            wb = w_ref[g].astype(BF16)
            s = jnp.dot(wb, vn[:, cs], preferred_element_type=F32) + bs_ref[:, g:g + 1]
            dp_ref[:, cs] = (dyv[:, cs] * s * _gelu_grad(pu[:, cs])).astype(BF16)
            dvn_parts.append(lax.dot_general(wb, dsb[:, cs], (((0,), (0,)), ((), ())),
                                             preferred_element_type=F32))
            dw_ref[g] += lax.dot_general(dsb[:, cs], vn[:, cs], (((1,), (1,)), ((), ())),
                                         preferred_element_type=F32)
        dvn = jnp.concatenate(dvn_parts, axis=-1)
        dln_ref[pl.ds(0, 1), :] += jnp.sum(dvn * vhat, axis=0, keepdims=True)
        dln_ref[pl.ds(1, 1), :] += jnp.sum(dvn, axis=0, keepdims=True)
        dvhat = dvn * g_ref[...]
        m1 = jnp.mean(dvhat, axis=-1, keepdims=True)
        m2 = jnp.mean(dvhat * vhat, axis=-1, keepdims=True)
        dv = rstd * (dvhat - m1 - vhat * m2)
        dp_ref[:, BR_W:2 * BR_W] = (dv * _gelu_grad(pv)).astype(BF16)

    chunk = lambda off: pl.BlockSpec((GC, BR_W), lambda i: (i, off // BR_W))
    return pl.pallas_call(
        body, grid=(T // GC,),
        in_specs=[chunk(OFF_U), chunk(OFF_VG), pl.BlockSpec((GC, BR_W), lambda i: (i, 0)),
                  _vec_spec(BR_W), _vec_spec(BR_W),
                  pl.BlockSpec((N_GROUP, GC, GC), lambda i: (0, 0, 0)),
                  pl.BlockSpec((GC, N_GROUP), lambda i: (0, 0)), _ANY],
        out_specs=[pl.BlockSpec((GC, 2 * BR_W), lambda i: (i, OFF_U // (2 * BR_W))),
                   pl.BlockSpec((N_GROUP, GC, GC), lambda i: (0, 0, 0)),
                   pl.BlockSpec((GC, BR_W), lambda i: (0, 0)),
                   pl.BlockSpec((SUBLANE, BR_W), lambda i: (0, 0))],
        out_shape=[jax.ShapeDtypeStruct(dp.shape, BF16), jax.ShapeDtypeStruct((N_GROUP, GC, GC), F32),
                   jax.ShapeDtypeStruct((GC, BR_W), F32), jax.ShapeDtypeStruct((SUBLANE, BR_W), F32)],
        input_output_aliases={7: 0},
        name=name, compiler_params=_cp(("arbitrary",)),
    )(p, p, dy, ln_g, ln_b, w_s, b_st, dp)


def _conv_fwd(p, conv_w, nc, name):
    T = p.shape[0]

    def body(cb_ref, cc_ref, cx_ref, w_ref, o_ref):
        zz = cc_ref[...] * cx_ref[...]
        conv = (w_ref[0:1, :] * _shift_rows(zz, -1, nc)[0] + w_ref[1:2, :] * zz
                + w_ref[2:3, :] * _shift_rows(zz, 1, nc)[0])
        o_ref[...] = (cb_ref[...] * conv).astype(BF16)

    col = lambda off: pl.BlockSpec((T, GC), lambda j: (0, off // GC + j))
    return pl.pallas_call(
        body, grid=(N_GROUP,),
        in_specs=[col(OFF_CB), col(OFF_CC), col(OFF_CX), pl.BlockSpec((3, GC), lambda j: (0, j))],
        out_specs=pl.BlockSpec((T, GC), lambda j: (0, j)),
        out_shape=jax.ShapeDtypeStruct((T, BR_W), BF16),
        name=name, compiler_params=_cp(("parallel",)),
    )(p, p, p, conv_w)


def _conv_bwd(p, dy, conv_w, dp, nc, name):
    T = p.shape[0]

    def body(cb_ref, cc_ref, cx_ref, dy_ref, w_ref, dp_in_ref, dp_ref, dw_ref):
        part = pl.program_id(1)
        cc, cx = cc_ref[...], cx_ref[...]
        dconv = dy_ref[...] * cb_ref[...]

        @pl.when(part == 0)
        def _():
            zz = cc * cx
            zm, zp = _shift_rows(zz, -1, nc)[0], _shift_rows(zz, 1, nc)[0]
            conv = w_ref[0:1, :] * zm + w_ref[1:2, :] * zz + w_ref[2:3, :] * zp
            dw_ref[0:1, :] = jnp.sum(dconv * zm, axis=0, keepdims=True)
            dw_ref[1:2, :] = jnp.sum(dconv * zz, axis=0, keepdims=True)
            dw_ref[2:3, :] = jnp.sum(dconv * zp, axis=0, keepdims=True)
            dp_ref[...] = (dy_ref[...] * conv).astype(BF16)

        @pl.when(part > 0)
        def _():
            dzz = (w_ref[0:1, :] * _shift_rows(dconv, 1, nc)[0] + w_ref[1:2, :] * dconv
                   + w_ref[2:3, :] * _shift_rows(dconv, -1, nc)[0])
            dp_ref[...] = (dzz * jnp.where(part == 1, cx, cc)).astype(BF16)

    col = lambda off: pl.BlockSpec((T, GC), lambda j, part: (0, off // GC + j))
    return pl.pallas_call(
        body, grid=(N_GROUP, 3),
        in_specs=[col(OFF_CB), col(OFF_CC), col(OFF_CX), pl.BlockSpec((T, GC), lambda j, part: (0, j)),
                  pl.BlockSpec((3, GC), lambda j, part: (0, j)), _ANY],
        out_specs=[pl.BlockSpec((T, GC), lambda j, part: (0, OFF_CB // GC + part * N_GROUP + j)),
                   pl.BlockSpec((3, GC), lambda j, part: (0, j))],
        out_shape=[jax.ShapeDtypeStruct(dp.shape, BF16), jax.ShapeDtypeStruct((3, BR_W), F32)],
        input_output_aliases={5: 0},
        name=name, compiler_params=_cp(("parallel", "arbitrary")),
    )(p, p, p, dy, conv_w, dp)


def _rows_tile(rows, cols, n_arrays):
    budget = 24 * 1024 * 1024 // (2 * 4 * n_arrays * cols)
    return _tile(rows, max(SUBLANE * 2, min(budget, 1024)), SUBLANE * 2) if rows % (SUBLANE * 2) == 0 else rows


def _cast_into(chip, w, l, name, after=()):
    _, K, cols = w.shape
    tr = _rows_tile(K, cols, 2)

    def body(q_ref, w_ref, *rest):
        rest[-1][...] = w_ref[...].astype(BF16)

    return pl.pallas_call(
        body,
        grid_spec=pltpu.PrefetchScalarGridSpec(
            num_scalar_prefetch=1, grid=(K // tr,),
            in_specs=[pl.BlockSpec((None, tr, cols), lambda i, q: (l, i, 0))] + [_ANY] * len(after),
            out_specs=pl.BlockSpec((None, tr, cols), lambda i, q: (q[0], i, 0))),
        out_shape=jax.ShapeDtypeStruct((N_CHIP, K, cols), BF16),
        name=name, compiler_params=_cp(("parallel",)),
    )(chip, w, *after)


def _adamw(w, g, m, v, name):
    shape = w.shape
    cols = shape[-1]
    rows = w.size // cols
    tr = _rows_tile(rows, cols, 7)
    c1 = 1.0 - ADAM_B1 ** ADAM_STEP
    c2 = 1.0 - ADAM_B2 ** ADAM_STEP

    def body(w_ref, g_ref, m_ref, v_ref, d_ref, mo_ref, vo_ref):
        gv = g_ref[...]
        mn = ADAM_B1 * m_ref[...] + (1.0 - ADAM_B1) * gv
        vn = ADAM_B2 * v_ref[...] + (1.0 - ADAM_B2) * (gv * gv)
        mo_ref[...] = mn
        vo_ref[...] = vn
        d_ref[...] = -ADAM_LR * ((mn / c1) / (jnp.sqrt(vn / c2) + ADAM_EPS) + ADAM_WD * w_ref[...])

    rs = pl.BlockSpec((tr, cols), lambda i: (i, 0))
    outs = pl.pallas_call(
        body, grid=(rows // tr,), in_specs=[rs] * 4, out_specs=[rs] * 3,
        out_shape=[jax.ShapeDtypeStruct((rows, cols), F32)] * 3,
        name=name, compiler_params=_cp(("parallel",)),
    )(*[a.reshape(rows, cols) for a in (w, g, m, v)])
    return [o.reshape(shape) for o in outs]


def _adamw_layer(w, g, m, v, l, prev, after, name):
    L, K, cols = w.shape
    tr = _rows_tile(K, cols, 7)
    c1 = 1.0 - ADAM_B1 ** ADAM_STEP
    c2 = 1.0 - ADAM_B2 ** ADAM_STEP
    nprev = 0 if prev is None else 3

    def body(*refs):
        w_ref, g_ref, m_ref, v_ref = refs[:4]
        d_ref, mo_ref, vo_ref = refs[-3:]
        gv = g_ref[...]
        mn = ADAM_B1 * m_ref[...] + (1.0 - ADAM_B1) * gv
        vn = ADAM_B2 * v_ref[...] + (1.0 - ADAM_B2) * (gv * gv)
        mo_ref[...] = mn
        vo_ref[...] = vn
        d_ref[...] = -ADAM_LR * ((mn / c1) / (jnp.sqrt(vn / c2) + ADAM_EPS) + ADAM_WD * w_ref[...])

    rs = pl.BlockSpec((None, tr, cols), lambda i: (l, i, 0))
    extra = list(prev or ()) + list(after)
    return pl.pallas_call(
        body, grid=(K // tr,), in_specs=[rs] * 4 + [_ANY] * len(extra), out_specs=[rs] * 3,
        out_shape=[jax.ShapeDtypeStruct((L, K, cols), F32)] * 3,
        input_output_aliases={4 + k: k for k in range(nprev)},
        name=name, compiler_params=_cp(("parallel",)),
    )(w, g, m, v, *extra)


def _pair_sum(core, dw, recv, name):
    nq, half, cols = recv.shape
    tr = _rows_tile(half, cols, 3)
    nb = half // tr

    def body(c_ref, a_ref, r_ref, o_ref):
        o_ref[...] = (a_ref[...].astype(F32) + r_ref[...].astype(F32)).astype(BF16)

    rs = pl.BlockSpec((None, tr, cols), lambda q, i, c: (q, i, 0))
    return pl.pallas_call(
        body,
        grid_spec=pltpu.PrefetchScalarGridSpec(
            num_scalar_prefetch=1, grid=(nq, nb),
            in_specs=[pl.BlockSpec((None, tr, cols), lambda q, i, c: (q, c[0] * nb + i, 0)), rs],
            out_specs=rs),
        out_shape=jax.ShapeDtypeStruct((nq, half, cols), BF16),
        name=name, compiler_params=_cp(("parallel", "parallel")),
    )(core, dw, recv)


def _chip_sum(chip, core, psum, land, l, n_layers, prev, name):
    _, half, cols = psum.shape
    tr = _rows_tile(half, cols, 5)
    nb = half // tr

    def body(*refs):
        p_ref, r_ref, o_ref = refs[2], refs[3], refs[-1]
        o_ref[...] = (p_ref[...].astype(F32) + r_ref[0].astype(F32) + r_ref[1].astype(F32)
                      + r_ref[2].astype(F32))

    extra = [] if prev is None else [prev]
    return pl.pallas_call(
        body,
        grid_spec=pltpu.PrefetchScalarGridSpec(
            num_scalar_prefetch=2, grid=(nb,),
            in_specs=[pl.BlockSpec((None, tr, cols), lambda i, q, c: (q[0], i, 0)),
                      pl.BlockSpec((3, tr, cols), lambda i, q, c: (0, i, 0))] + [_ANY] * len(extra),
            out_specs=pl.BlockSpec((None, tr, cols), lambda i, q, c: (l, c[0] * nb + i, 0))),
        out_shape=jax.ShapeDtypeStruct((n_layers, 2 * half, cols), F32),
        input_output_aliases={4: 0} if prev is not None else {},
        name=name, compiler_params=_cp(("parallel",)),
    )(chip, core, psum, land, *extra)


def _slab_sum(gathered, name, after=()):
    n, rows, cols = gathered.shape
    tr = _tile(rows, 512, SUBLANE)

    def body(g_ref, *rest):
        acc = g_ref[0]
        for d in range(1, n):
            acc = acc + g_ref[d]
        rest[-1][...] = acc

    return pl.pallas_call(
        body, grid=(rows // tr,),
        in_specs=[pl.BlockSpec((n, tr, cols), lambda i: (0, i, 0))] + [_ANY] * len(after),
        out_specs=pl.BlockSpec((tr, cols), lambda i: (i, 0)),
        out_shape=jax.ShapeDtypeStruct((rows, cols), F32),
        name=name, compiler_params=_cp(("parallel",)),
    )(gathered, *after)


def _silu(x):
    return x * jax.nn.sigmoid(x)


def _ada_fwd(cvec, w_ada, b_cols, name):
    L, D, ns = w_ada.shape
    tn = _tile(ns, 768)

    def body(c_ref, w_ref, b_ref, o_ref):
        s = _silu(c_ref[...]).astype(BF16)
        o_ref[...] = jnp.dot(s, w_ref[...].astype(BF16), preferred_element_type=F32) + b_ref[...]

    return pl.pallas_call(
        body, grid=(L, ns // tn),
        in_specs=[pl.BlockSpec((16, D), lambda l, j: (0, 0)),
                  pl.BlockSpec((None, D, tn), lambda l, j: (l, 0, j)),
                  pl.BlockSpec((None, 1, tn), lambda l, j: (l, 0, j))],
        out_specs=pl.BlockSpec((None, 16, tn), lambda l, j: (l, 0, j)),
        out_shape=jax.ShapeDtypeStruct((L, 16, ns), F32),
        name=name, compiler_params=_cp(("parallel", "parallel")),
    )(cvec, w_ada, b_cols)


def _ada_bwd(cvec, dmod, w_ada, name):
    L, D, ns = w_ada.shape
    tn = _tile(ns, 768)

    def body(c_ref, d_ref, w_ref, gw_ref, ds_ref):
        first = jnp.logical_and(pl.program_id(0) == 0, pl.program_id(1) == 0)

        @pl.when(first)
        def _():
            ds_ref[...] = jnp.zeros_like(ds_ref)

        s = _silu(c_ref[...]).astype(BF16)
        db = d_ref[...].astype(BF16)
        gw_ref[...] = lax.dot_general(s, db, (((0,), (0,)), ((), ())), preferred_element_type=F32)
        ds_ref[...] += lax.dot_general(db, w_ref[...].astype(BF16), (((1,), (1,)), ((), ())),
                                       preferred_element_type=F32)

    return pl.pallas_call(
        body, grid=(L, ns // tn),
        in_specs=[pl.BlockSpec((16, D), lambda l, j: (0, 0)),
                  pl.BlockSpec((None, 16, tn), lambda l, j: (l, 0, j)),
                  pl.BlockSpec((None, D, tn), lambda l, j: (l, 0, j))],
        out_specs=[pl.BlockSpec((None, D, tn), lambda l, j: (l, 0, j)),
                   pl.BlockSpec((16, D), lambda l, j: (0, 0))],
        out_shape=[jax.ShapeDtypeStruct((L, D, ns), F32), jax.ShapeDtypeStruct((16, D), F32)],
        name=name, compiler_params=_cp(("arbitrary", "arbitrary")),
    )(cvec, dmod, w_ada)


def _cctx_grad(gathered, c_ctx, name):
    rows = c_ctx.shape[0]

    def body(g_ref, c_ref, o_ref):
        ds = g_ref[0] + g_ref[2] + g_ref[4] + g_ref[6]
        cv = c_ref[...]
        sig = jax.nn.sigmoid(cv)
        o_ref[...] = ds * sig * (1.0 + cv * (1.0 - sig))

    return pl.pallas_call(
        body, out_shape=jax.ShapeDtypeStruct((rows, LANE), F32), name=name,
    )(gathered, c_ctx)


def _place():
    x, y, c = lax.axis_index("x"), lax.axis_index("y"), lax.axis_index("c")
    chips = [(1 - x, y), (x, 1 - y), (1 - x, 1 - y)]
    return x, y, c, chips


def _all_gather_small(slab, name, after=()):
    rows, cols = slab.shape

    def body(x_ref, *rest):
        out_ref, send_sems, recv_sems, local_sem = rest[len(after):]
        x, y, c, chips = _place()
        me, sibling = (x, y, c), (x, y, 1 - c)

        def blk(px, py, pc):
            return out_ref.at[4 * px + 2 * py + pc]

        def copy(k, block, to, src=None):
            return pltpu.make_async_remote_copy(
                src_ref=blk(*block) if src is None else src, dst_ref=blk(*block),
                send_sem=send_sems.at[k], recv_sem=recv_sems.at[k], device_id=to, device_id_type=MESH)

        mine = pltpu.make_async_copy(x_ref, blk(*me), local_sem)
        mine.start()
        first = [copy(0, me, sibling, src=x_ref)]
        first += [copy(1 + j, me, (*chip, c), src=x_ref) for j, chip in enumerate(chips)]
        for cp in first:
            cp.start()
        passed = [copy(4 + j, (*chip, c), sibling) for j, chip in enumerate(chips)]
        for j, chip in enumerate(chips):
            copy(1 + j, (*chip, c), me).wait_recv()
            passed[j].start()
        copy(0, sibling, me).wait_recv()
        for j, chip in enumerate(chips):
            copy(4 + j, (*chip, 1 - c), me).wait_recv()
        for cp in first + passed:
            cp.wait_send()
        mine.wait()

    return pl.pallas_call(
        body, out_shape=jax.ShapeDtypeStruct((N_DEV, rows, cols), slab.dtype),
        in_specs=[pl.BlockSpec(memory_space=pltpu.VMEM)] + [_ANY] * len(after),
        out_specs=pl.BlockSpec(memory_space=pltpu.VMEM),
        scratch_shapes=[pltpu.SemaphoreType.DMA((7,)), pltpu.SemaphoreType.DMA((7,)), pltpu.SemaphoreType.DMA],
        name=name, compiler_params=pltpu.CompilerParams(vmem_limit_bytes=VMEM_MID),
    )(slab, *after)


def _hbm(a):
    return pltpu.with_memory_space_constraint(a, pltpu.HBM)


def _half_rows(ref, q, c):
    half = ref.shape[1] // 2
    return ref.at[q, pl.ds(c * half, half)]


def _ici_copy(src, dst, send_sems, recv_sems, k, to):
    return pltpu.make_async_remote_copy(src_ref=src, dst_ref=dst, send_sem=send_sems.at[k], recv_sem=recv_sems.at[k],
                                        device_id=to, device_id_type=MESH)


def _gather_start(bufs, sizes, name):
    n = len(bufs)
    ng = len(sizes)

    def body(*refs):
        ins = refs[:n]
        sems = refs[n:n + 2 * ng]
        token = refs[-1]
        x, y, c, chips = _place()
        a = 0
        for g, sz in enumerate(sizes):
            for i in range(sz):
                part = _half_rows(ins[a], 2 * x + y, c)
                for j, chip in enumerate(chips):
                    _ici_copy(part, part, sems[2 * g], sems[2 * g + 1], 3 * i + j, (*chip, c)).start()
                a += 1
        token[...] = jnp.zeros_like(token)

    sem_shapes = []
    for sz in sizes:
        sem_shapes += [pltpu.SemaphoreType.DMA((3 * sz,)), pltpu.SemaphoreType.DMA((3 * sz,))]
    outs = pl.pallas_call(
        body, name=name,
        out_shape=tuple(sem_shapes) + tuple(pltpu.HBM(b.shape, b.dtype) for b in bufs)
                  + (jax.ShapeDtypeStruct((SUBLANE, LANE), F32),),
        in_specs=[_HBM] * n,
        out_specs=tuple([_SEM] * (2 * ng) + [_HBM] * n + [pl.BlockSpec(memory_space=pltpu.VMEM)]),
        input_output_aliases={i: 2 * ng + i for i in range(n)},
        compiler_params=pltpu.CompilerParams(has_side_effects=_EFFECT),
    )(*[_hbm(b) for b in bufs])
    sems = [(outs[2 * g], outs[2 * g + 1]) for g in range(ng)]
    return sems, list(outs[2 * ng:2 * ng + n]), outs[-1]


def _gather_wait(bufs, sems, after, name):
    n = len(bufs)
    na = len(after)

    def body(*refs):
        ins = refs[:n]
        send_sems, recv_sems = refs[n], refs[n + 1]
        x, y, c, chips = _place()
        for i in range(n):
            sent = _half_rows(ins[i], 2 * x + y, c)
            for j, (cx, cy) in enumerate(chips):
                cp = _ici_copy(sent, _half_rows(ins[i], 2 * cx + cy, c), send_sems, recv_sems, 3 * i + j, (cx, cy, c))
                cp.wait_send()
                cp.wait_recv()

    outs = pl.pallas_call(
        body, name=name,
        out_shape=tuple(pltpu.HBM(b.shape, b.dtype) for b in bufs),
        in_specs=[_HBM] * n + [_SEM, _SEM] + [_ANY] * na,
        out_specs=tuple([_HBM] * n),
        input_output_aliases={i: i for i in range(n)},
        compiler_params=pltpu.CompilerParams(has_side_effects=_EFFECT),
    )(*bufs, sems[0], sems[1], *after)
    return list(outs)


def _exchange_halves(bufs, name):
    n = len(bufs)

    def body(*refs):
        ins = refs[:n]
        send_sems, recv_sems = refs[-2:]
        x, y, c, chips = _place()
        sends = []
        for i in range(n):
            for j, (cx, cy) in enumerate(chips):
                part = _half_rows(ins[i], 2 * cx + cy, c)
                cp = _ici_copy(part, part, send_sems, recv_sems, 3 * i + j, (x, y, 1 - c))
                cp.start()
                sends.append(cp)
        for i in range(n):
            for j, (cx, cy) in enumerate(chips):
                part = _half_rows(ins[i], 2 * cx + cy, 1 - c)
                _ici_copy(part, part, send_sems, recv_sems, 3 * i + j, (x, y, 1 - c)).wait_recv()
        for cp in sends:
            cp.wait_send()

    outs = pl.pallas_call(
        body, name=name,
        out_shape=[jax.ShapeDtypeStruct(b.shape, b.dtype) for b in bufs],
        in_specs=[_ANY] * n, out_specs=[_ANY] * n,
        input_output_aliases={i: i for i in range(n)},
        scratch_shapes=[pltpu.SemaphoreType.DMA((3 * n,)), pltpu.SemaphoreType.DMA((3 * n,))],
    )(*bufs)
    return list(outs)


def _other_half(ref, c):
    half = ref.shape[1] // 2
    return ref.at[:, pl.ds((1 - c) * half, half)]


def _prereduce_plan(src, land, x, y, c, chips):
    return [(_other_half(src, c), land, (x, y, 1 - c))]


def _prereduce_land(d):
    return (d.shape[0], d.shape[1] // 2, d.shape[2])


def _scatter_plan(src, land, x, y, c, chips):
    return [(src.at[2 * cx + cy], land.at[j], (cx, cy, c)) for j, (cx, cy) in enumerate(chips)]


def _scatter_land(p):
    return (3,) + p.shape[1:]


_COPIES_PER_SOURCE = {_prereduce_plan: 1, _scatter_plan: 3}


def _split_start(srcs, plan, land_shape, name):
    n = len(srcs)
    land_shapes = [land_shape(s) for s in srcs]
    per = _COPIES_PER_SOURCE[plan]

    def body(*refs):
        src_refs, lands = refs[:n], refs[n:2 * n]
        send_sems, recv_sems = refs[2 * n], refs[2 * n + 1]
        token = refs[-1]
        x, y, c, chips = _place()
        for i in range(n):
            for j, (s, d, to) in enumerate(plan(src_refs[i], lands[i], x, y, c, chips)):
                _ici_copy(s, d, send_sems, recv_sems, per * i + j, to).start()
        token[...] = jnp.zeros_like(token)

    outs = pl.pallas_call(
        body, name=name,
        out_shape=(pltpu.SemaphoreType.DMA((per * n,)), pltpu.SemaphoreType.DMA((per * n,)))
                  + tuple(pltpu.HBM(p.shape, p.dtype) for p in srcs)
                  + tuple(pltpu.HBM(s, BF16) for s in land_shapes)
                  + (jax.ShapeDtypeStruct((SUBLANE, LANE), F32),),
        in_specs=[_HBM] * (2 * n),
        out_specs=tuple([_SEM, _SEM] + [_HBM] * (2 * n) + [pl.BlockSpec(memory_space=pltpu.VMEM)]),
        input_output_aliases={i: 2 + i for i in range(2 * n)},
        compiler_params=pltpu.CompilerParams(has_side_effects=_EFFECT),
    )(*[_hbm(p) for p in srcs], *[_hbm(lax.empty(s, BF16)) for s in land_shapes])
    return (outs[0], outs[1]), list(outs[2:2 + n]), list(outs[2 + n:2 + 2 * n]), outs[-1]


def _split_wait(psums, lands, sems, plan, after, name):
    n = len(psums)
    na = len(after)
    per = _COPIES_PER_SOURCE[plan]

    def body(*refs):
        srcs, lnds = refs[:n], refs[n:2 * n]
        send_sems, recv_sems = refs[2 * n], refs[2 * n + 1]
        x, y, c, chips = _place()
        for i in range(n):
            for j, (s, d, to) in enumerate(plan(srcs[i], lnds[i], x, y, c, chips)):
                cp = _ici_copy(s, d, send_sems, recv_sems, per * i + j, to)
                cp.wait_send()
                cp.wait_recv()

    outs = pl.pallas_call(
        body, name=name,
        out_shape=tuple(pltpu.HBM(a.shape, a.dtype) for a in list(psums) + list(lands)),
        in_specs=[_HBM] * (2 * n) + [_SEM, _SEM] + [_ANY] * na,
        out_specs=tuple([_HBM] * (2 * n)),
        input_output_aliases={i: i for i in range(2 * n)},
        compiler_params=pltpu.CompilerParams(has_side_effects=_EFFECT),
    )(*psums, *lands, sems[0], sems[1], *after)
    return list(outs[:n]), list(outs[n:])


def _exchange_grads(grads, l, name):
    n = len(grads)

    def body(*refs):
        ins = refs[:n]
        send_sems, recv_sems = refs[-2:]
        x, y, c, _ = _place()

        def rows(ref, core):
            half = ref.shape[1] // 2
            return ref.at[l, pl.ds(core * half, half)]

        cps = []
        for i in range(n):
            cp = _ici_copy(rows(ins[i], c), rows(ins[i], c), send_sems, recv_sems, i, (x, y, 1 - c))
            cp.start()
            cps.append(cp)
        for i, cp in enumerate(cps):
            cp.wait_send()
            _ici_copy(rows(ins[i], 1 - c), rows(ins[i], 1 - c), send_sems, recv_sems, i, (x, y, 1 - c)).wait_recv()

    outs = pl.pallas_call(
        body, name=name,
        out_shape=[jax.ShapeDtypeStruct(g.shape, g.dtype) for g in grads],
        in_specs=[_ANY] * n, out_specs=[_ANY] * n,
        input_output_aliases={i: i for i in range(n)},
        scratch_shapes=[pltpu.SemaphoreType.DMA((n,)), pltpu.SemaphoreType.DMA((n,))],
    )(*grads)
    return list(outs)


def _pack(arrs):
    flat = jnp.concatenate([a.reshape(-1).astype(F32) for a in arrs])
    pad = (-flat.shape[0]) % (SUBLANE * LANE)
    return jnp.pad(flat, (0, pad)).reshape(-1, LANE)


def _unpack(slab, shapes):
    flat = slab.reshape(-1)
    out, off = [], 0
    for s in shapes:
        n = math.prod(s)
        out.append(flat[off:off + n].reshape(s))
        off += n
    return out


def _split6(v):
    d = v.shape[-1] // 6
    return [v[:, k * d:(k + 1) * d].reshape(2, 1, d) for k in range(6)]


BIG = ("w_in", "w_br_attn", "w_br_pool", "w_br_sgu", "w_br_conv", "w_gate", "w_o", "w_ff_gate", "w_ff_up", "w_ff_down")
KIND = {"w_in": "col", "w_br_attn": "col", "w_br_pool": "col", "w_br_sgu": "col", "w_br_conv": "col",
        "w_gate": "col", "w_o": "row", "w_ff_gate": "col", "w_ff_up": "col", "w_ff_down": "row"}
SMALL = ("c_ctx", "b_ada", "q_norm_g", "k_norm_g", "pool_w", "pool_scale", "sgu_ln_g", "sgu_ln_b", "sgu_w",
         "sgu_b", "conv_w", "b_gate", "ln1_g", "ln1_b", "ln2_g", "ln2_b")
WEIGHTS = ("c_ctx", "w_ada", "b_ada", "w_in", "q_norm_g", "k_norm_g", "pool_w", "pool_scale", "sgu_ln_g", "sgu_ln_b",
           "sgu_w", "sgu_b", "conv_w", "w_br_attn", "w_br_pool", "w_br_sgu", "w_br_conv", "w_gate", "b_gate", "w_o",
           "ln1_g", "ln1_b", "w_ff_gate", "w_ff_up", "w_ff_down", "ln2_g", "ln2_b")


def _step(x, c, ctx, loss_target, W, M, V):
    L = W["w_ada"].shape[0]
    assert L == 2, "core c of a chip carries layer c of the weight traffic"
    N, D = x.shape[1], x.shape[2]
    NC = ctx.shape[1]
    T = NC + N
    FF = W["w_ff_down"].shape[1] * N_CHIP
    assert NC % ROW_TM == 0 and N % ROW_TM == 0 and N % GRID_W == 0 and D % LANE == 0
    ncb = NC // ROW_TM
    nblk = T // ROW_TM
    alpha = (2 * L) ** 0.25
    ax, ay, ac = lax.axis_index("x"), lax.axis_index("y"), lax.axis_index("c")
    chip = 2 * ax + ay
    dev = 2 * chip + ac
    chip_arr = jnp.reshape(chip, (1,)).astype(jnp.int32)
    core_arr = jnp.reshape(ac, (1,)).astype(jnp.int32)
    ns_ada = W["w_ada"].shape[2]
    chip_devs = (0, 2, 4, 6)

    conv_shape = W["conv_w"].shape
    g0 = _all_gather_small(_pack([c, W["conv_w"]]), "gather_c")
    c_all = g0[:, :D // LANE, :].reshape(N_DEV, D)
    conv_parts = [g0[d].reshape(-1)[D:D + math.prod(conv_shape)].reshape(conv_shape) for d in chip_devs]
    conv_full = jnp.concatenate(conv_parts, axis=-1)
    cvec = jnp.concatenate([c_all, W["c_ctx"][None], jnp.zeros((16 - N_DEV - 1, D), F32)], axis=0)
    b_cols = lax.dynamic_slice_in_dim(W["b_ada"], chip * ns_ada, ns_ada, axis=1).reshape(L, 1, ns_ada)
    mod_part = _ada_fwd(cvec, W["w_ada"], b_cols, "ada_fwd")
    g1 = _all_gather_small(mod_part.reshape(-1, LANE), "gather_mod")
    mod_all = jnp.concatenate([g1[d].reshape(L, 16, ns_ada) for d in chip_devs], axis=-1)
    mod_lat = lax.dynamic_index_in_dim(mod_all, dev, axis=1, keepdims=True)
    mods = jnp.concatenate([mod_all[:, N_DEV:N_DEV + 1], mod_lat], axis=1)

    ffn_keys = ("w_ff_gate", "w_ff_up", "w_ff_down")
    mid_keys = ("w_br_attn", "w_br_pool", "w_br_sgu", "w_br_conv", "w_o")
    groups = [grp for l in range(L) for grp in ([("w_in", l)], [("w_gate", l)], [(k, l) for k in mid_keys],
                                                [(k, l) for k in ffn_keys])]
    flying, gsems, gtokens = [None] * len(groups), [None] * len(groups), []
    for n_start, gs in enumerate(((0, 1), (2, 3), (4, 5, 6, 7))):
        members = [kl for g in gs for kl in groups[g]]
        casts = [_cast_into(chip_arr, W[k], l, "cast_%s_l%d" % (k, l), after=gtokens[-1:] or [g1])
                 for k, l in members]
        sems, bufs, token = _gather_start(casts, [len(groups[g]) for g in gs], "gather_start_%d" % n_start)
        gtokens.append(token)
        lo = 0
        for g, sem in zip(gs, sems):
            flying[g], gsems[g] = bufs[lo:lo + len(groups[g])], sem
            lo += len(groups[g])
    full = {k: [None] * L for k in BIG}

    def weights_ready(g, after):
        got = _gather_wait(flying[g], gsems[g], after, "gather_wait_%d" % g)
        for (k, l), buf in zip(groups[g], _exchange_halves(got, "exchange_halves_%d" % g)):
            full[k][l] = buf

    cos, sin = _rope_tables(N, NC)
    x0 = jnp.concatenate([ctx[0], x[0]], axis=0)

    saved = []
    xin = x0
    h1 = None
    dy_top = loss_parts = None
    for l in range(L):
        sh1, sc1, g1m, sh2, sc2, g2m = _split6(mods[l])
        tag = "_l%d" % l
        if l == 0:
            h1 = _modulate(xin, sc1, sh1, ncb, "modulate" + tag)
        weights_ready(4 * l, [h1])
        p = _mm_nn(h1, full["w_in"], l, "col", name="mm_in" + tag)
        qg, kg = W["q_norm_g"][l][None], W["k_norm_g"][l][None]
        q, k, v = _qk_fwd(p, cos, sin, qg, kg, "qk_fwd" + tag)
        att = _attn_fwd(q, k, v, NC, "attn_fwd" + tag)
        ps = W["pool_scale"][l][None]
        ypool = _pool_fwd(p, W["pool_w"][l], ps, NC, "pool_fwd" + tag)
        lg, lb = W["sgu_ln_g"][l][None], W["sgu_ln_b"][l][None]
        b_st = W["sgu_b"][l].T
        ysgu = _sgu_fwd(p, lg, lb, W["sgu_w"][l], b_st, "sgu_fwd" + tag)
        yconv = _conv_fwd(p, conv_full[l], NC, "conv_fwd" + tag)
        brs = (att, ypool, ysgu, yconv)
        weights_ready(4 * l + 1, list(brs))
        gates = _mm_nn(h1, full["w_gate"], l, "col", name="mm_gate" + tag, bias=W["b_gate"][l][None],
                       act=jax.nn.sigmoid, out_dtype=BF16)
        weights_ready(4 * l + 2, [gates])
        ts = [_mm_nn(b, full[wk], l, "col", name="mm_" + wk + tag, out_dtype=BF16)
              for b, wk in zip(brs, ("w_br_attn", "w_br_pool", "w_br_sgu", "w_br_conv"))]
        mg = _merge_fwd(gates, ts, "merge_fwd" + tag)
        m = _mm_nn(mg, full["w_o"], l, "row", name="mm_o" + tag, tn_pref=1024)
        ln1g, ln1b = W["ln1_g"][l][None], W["ln1_b"][l][None]
        x1, h2 = _resid_ln(xin, m, g1m, ln1g, ln1b, sc2, sh2, ncb, alpha, "resid_ln1" + tag)
        weights_ready(4 * l + 3, [h2])
        gg, uu, act = _ffn_up(h2, full["w_ff_gate"][l], full["w_ff_up"][l], "ffn_up" + tag)
        ff = _mm_nn(act, full["w_ff_down"], l, "row", name="mm_ffd" + tag, tm_pref=384, tn_pref=512)
        ln2g, ln2b = W["ln2_g"][l][None], W["ln2_b"][l][None]
        saved.append(dict(xin=xin, h1=h1, p=p, q=q, k=k, v=v, brs=brs, gates=gates, ts=ts, mg=mg, m=m, x1=x1, h2=h2,
                          gg=gg, uu=uu, act=act, ff=ff))
        if l + 1 < L:
            nsh1, nsc1 = _split6(mods[l + 1])[:2]
            xin, h1 = _resid_ln(x1, ff, g2m, ln2g, ln2b, nsc1, nsh1, ncb, alpha, "resid_ln2" + tag)
        else:
            dy_top, loss_parts = _resid_ln_loss(x1, ff, g2m, ln2g, ln2b, loss_target[0], ncb, alpha,
                                                "resid_ln2_loss" + tag)
    loss_dev = jnp.sum(loss_parts[:, 0, 0])

    def lat_ctx(part_rows):
        return jnp.stack([jnp.sum(part_rows[:ncb], axis=0), jnp.sum(part_rows[ncb:], axis=0)])

    dW = {k: [None] * L for k in BIG}
    small = {k: [None] * L for k in ("q_norm_g", "k_norm_g", "pool_w", "pool_scale", "sgu_ln_g", "sgu_ln_b", "sgu_w",
                                     "sgu_b", "conv_w", "b_gate", "ln1_g", "ln1_b", "ln2_g", "ln2_b")}
    dmods = [None] * L
    mix_keys = tuple(k for k in BIG if k not in ffn_keys)
    grads_big = {k: None for k in BIG}
    adam_out = {k: None for k in BIG}

    def reduce_begin(keys, l, tg):
        sems, src, land, token = _split_start([dW[k][l] for k in keys], _prereduce_plan, _prereduce_land,
                                              "prereduce_start_" + tg)
        return dict(keys=keys, l=l, sems=sems, src=src, land=land, token=token, tg=tg)

    def reduce_scatter(st, after):
        dws, recv = _split_wait(st["src"], st["land"], st["sems"], _prereduce_plan, after,
                                "prereduce_wait_" + st["tg"])
        psums = [_pair_sum(core_arr, d, r, "pair_sum_%s_l%d" % (k, st["l"]))
                 for k, d, r in zip(st["keys"], dws, recv)]
        st["sems"], st["src"], st["land"], st["token"] = _split_start(psums, _scatter_plan, _scatter_land,
                                                                      "scatter_start_" + st["tg"])

    def reduce_finish(st, after):
        src, land = _split_wait(st["src"], st["land"], st["sems"], _scatter_plan, after, "scatter_wait_" + st["tg"])
        for k, p_, r in zip(st["keys"], src, land):
            grads_big[k] = _chip_sum(chip_arr, core_arr, p_, r, st["l"], L, grads_big[k],
                                     "chip_sum_%s_l%d" % (k, st["l"]))
        done = _exchange_grads([grads_big[k] for k in st["keys"]], st["l"], "exchange_grads_" + st["tg"])
        grads_big.update(zip(st["keys"], done))

    def adam(keys, l, after):
        for k in keys:
            adam_out[k] = _adamw_layer(W[k], grads_big[k], M[k], V[k], l, adam_out[k], after,
                                       "adamw_%s_l%d" % (k, l))

    dxa, dhs, sc_prev = dy_top, [], None
    red_l1 = red_ffn = None
    for l in reversed(range(L)):
        s = saved[l]
        sh1, sc1, g1m, sh2, sc2, g2m = _split6(mods[l])
        tag = "_l%d" % l
        ln1g, ln1b = W["ln1_g"][l][None], W["ln1_b"][l][None]
        ln2g, ln2b = W["ln2_g"][l][None], W["ln2_b"][l][None]
        dx1a, dff, part2 = _ln_bwd(dxa, dhs, sc_prev if dhs else sc1, s["x1"], s["ff"], g2m, ln2g, ln2b, ncb, alpha,
                                   "ln2_bwd" + tag, after=[] if red_l1 is None else [red_l1["token"]])
        small["ln2_g"][l] = jnp.sum(part2[:, 0], axis=0)
        small["ln2_b"][l] = jnp.sum(part2[:, 1], axis=0)
        dg2 = lat_ctx(part2[:, 2])
        if dhs:
            dmods[l + 1][1], dmods[l + 1][0] = lat_ctx(part2[:, 3]), lat_ctx(part2[:, 4])
        dgg, duu = _ffn_down_bwd(dff, full["w_ff_down"][l], s["gg"], s["uu"], "ffn_down_bwd" + tag)
        if l == 0:
            reduce_scatter(red_l1, [dgg])
        dW["w_ff_down"][l] = _mm_tn(s["act"], dff, "row", nq=N_CHIP, kdim=FF // N_CHIP, ndim=D, name="tn_ffd" + tag,
                                    tk_pref=1408, tn_pref=512)
        dh2a = _mm_nt(dgg, full["w_ff_gate"], l, "col", name="nt_ffg" + tag)
        dh2b = _mm_nt(duu, full["w_ff_up"], l, "col", name="nt_ffu" + tag)
        dW["w_ff_gate"][l] = _mm_tn(s["h2"], dgg, "col", nq=N_CHIP, kdim=D, ndim=FF // N_CHIP, name="tn_ffg" + tag,
                                    tn_pref=1408)
        dW["w_ff_up"][l] = _mm_tn(s["h2"], duu, "col", nq=N_CHIP, kdim=D, ndim=FF // N_CHIP, name="tn_ffu" + tag,
                                  tn_pref=1408)
        ties = []
        if l == 0:
            red_ffn = reduce_begin(ffn_keys, 0, "l0_ffn")
            ties = [red_l1["token"], red_ffn["token"]]
        dx0a, dm, part1 = _ln_bwd(dx1a, [dh2a, dh2b], sc2, s["xin"], s["m"], g1m, ln1g, ln1b, ncb, alpha,
                                  "ln1_bwd" + tag, after=ties)
        small["ln1_g"][l] = jnp.sum(part1[:, 0], axis=0)
        small["ln1_b"][l] = jnp.sum(part1[:, 1], axis=0)
        dg1 = lat_ctx(part1[:, 2])
        dsc2, dsh2 = lat_ctx(part1[:, 3]), lat_ctx(part1[:, 4])
        dmods[l] = [None, None, dg1, dsh2, dsc2, dg2]
        dmg = _mm_nt(dm, full["w_o"], l, "row", name="nt_o" + tag)
        dW["w_o"][l] = _mm_tn(s["mg"], dm, "row", nq=N_CHIP, kdim=D // N_CHIP, ndim=D, name="tn_o" + tag, tn_pref=1024)
        dpre, dt0, dt1, dt2, dt3, bpart = _merge_bwd(dmg, s["gates"], s["ts"], "merge_bwd" + tag)
        small["b_gate"][l] = jnp.sum(bpart[:, 0], axis=0)
        dh1a = _mm_nt(dpre, full["w_gate"], l, "col", name="nt_gate" + tag)
        dW["w_gate"][l] = _mm_tn(s["h1"], dpre, "col", nq=N_CHIP, kdim=D, ndim=D, name="tn_gate" + tag, tn_pref=1024)
        dbrs = []
        for b, dt, wk, odt in zip(s["brs"], (dt0, dt1, dt2, dt3), ("w_br_attn", "w_br_pool", "w_br_sgu", "w_br_conv"),
                                  (BF16, F32, F32, F32)):
            dbrs.append(_mm_nt(dt, full[wk], l, "col", name="nt_" + wk + tag, out_dtype=odt))
            dW[wk][l] = _mm_tn(b, dt, "col", nq=N_CHIP, kdim=b.shape[1], ndim=D // N_CHIP, name="tn_" + wk + tag)
        qg, kg = W["q_norm_g"][l][None], W["k_norm_g"][l][None]
        dq, dk, dv = _attn_bwd(s["q"], s["k"], s["v"], dbrs[0], NC, "attn_bwd" + tag)
        dp, qkpart = _qk_bwd(s["p"], dq, dk, dv, cos, sin, qg, kg, "qk_bwd" + tag)
        small["q_norm_g"][l] = jnp.sum(qkpart[:, 0], axis=0)
        small["k_norm_g"][l] = jnp.sum(qkpart[:, 1], axis=0)
        ps = W["pool_scale"][l][None]
        dp, small["pool_w"][l], dps = _pool_bwd(s["p"], dbrs[1], W["pool_w"][l], ps, dp, NC, "pool_bwd" + tag)
        small["pool_scale"][l] = dps[0]
        lg, lb = W["sgu_ln_g"][l][None], W["sgu_ln_b"][l][None]
        dp, small["sgu_w"][l], dsacc, dln = _sgu_bwd(s["p"], dbrs[2], lg, lb, W["sgu_w"][l], W["sgu_b"][l].T, dp,
                                                     "sgu_bwd" + tag)
        small["sgu_b"][l] = jnp.sum(dsacc.reshape(GC, N_GROUP, GC), axis=-1).T
        small["sgu_ln_g"][l], small["sgu_ln_b"][l] = dln[0], dln[1]
        dp, small["conv_w"][l] = _conv_bwd(s["p"], dbrs[3], conv_full[l], dp, NC, "conv_bwd" + tag)
        dh1b = _mm_nt(dp, full["w_in"], l, "col", name="nt_in" + tag)
        dW["w_in"][l] = _mm_tn(s["h1"], dp, "col", nq=N_CHIP, kdim=D, ndim=IN_W // N_CHIP, name="tn_in" + tag,
                               tn_pref=1152)
        dxa, dhs, sc_prev = dx0a, [dh1a, dh1b], sc1
        if l == 1:
            red_l1 = reduce_begin(BIG, 1, "l1")
    dx0, part0 = _mod_bwd(dxa, dhs, sc_prev, x0, ncb, "mod_bwd")
    dmods[0][1], dmods[0][0] = lat_ctx(part0[:, 0]), lat_ctx(part0[:, 1])
    grad_x = dx0[NC:][None]
    dmod = jnp.stack([jnp.concatenate(dmods[l], axis=-1) for l in range(L)])

    small_names = [k for k in SMALL if k not in ("c_ctx", "b_ada")]
    small_arrs = [jnp.stack(small[k]) for k in small_names]
    small_shapes = [a.shape for a in small_arrs]
    slab = _pack([loss_dev.reshape(1), jnp.zeros((LANE - 1,), F32), dmod] + small_arrs)
    g2 = _all_gather_small(slab, "gather_small")
    total = _slab_sum(g2, "slab_sum")
    flat_total = total.reshape(-1)
    loss = flat_total[0]
    nmod = L * 2 * 6 * D
    dmod_sum = flat_total[LANE:LANE + nmod].reshape(L, 2, 6 * D)
    small_grads = dict(zip(small_names, _unpack(flat_total[LANE + nmod:], small_shapes)))
    dmod_lat = g2.reshape(N_DEV, -1)[:, LANE:LANE + nmod].reshape(N_DEV, L, 2, 6 * D)[:, :, 1]
    dm16 = jnp.concatenate([jnp.transpose(dmod_lat, (1, 0, 2)), dmod_sum[:, 0:1],
                            jnp.zeros((L, 16 - N_DEV - 1, 6 * D), F32)], axis=1)
    small_grads["b_ada"] = dmod_sum[:, 0] + dmod_sum[:, 1]
    dm16_cols = lax.dynamic_slice_in_dim(dm16, chip * ns_ada, ns_ada, axis=2)
    grad_w_ada, ds16 = _ada_bwd(cvec, dm16_cols, W["w_ada"], "ada_bwd")
    conv_grad_full = small_grads["conv_w"]
    small_grads["conv_w"] = lax.dynamic_slice_in_dim(conv_grad_full, chip * GC, GC, axis=2)

    reduce_scatter(red_ffn, [g2])
    red_mix = reduce_begin(mix_keys, 0, "l0_mix")
    reduce_finish(red_l1, [red_ffn["token"], red_mix["token"]])
    reduce_scatter(red_mix, [grads_big[k] for k in BIG])
    adam(BIG, 1, [red_mix["token"]])
    reduce_finish(red_ffn, [red_mix["token"]])
    adam(ffn_keys, 0, [])
    delta, new_m, new_v = {}, {}, {}
    delta["w_ada"], new_m["w_ada"], new_v["w_ada"] = _adamw(W["w_ada"], grad_w_ada, M["w_ada"], V["w_ada"],
                                                            "adamw_w_ada")
    reduce_finish(red_mix, [delta["w_ada"]] + [adam_out[k][0] for k in BIG])
    adam(mix_keys, 0, [])
    for k in BIG:
        delta[k], new_m[k], new_v[k] = adam_out[k]

    g3 = _all_gather_small(ds16[N_DEV].reshape(-1, LANE), "gather_dsilu", after=[grads_big[k] for k in mix_keys])
    small_grads["c_ctx"] = _cctx_grad(g3, W["c_ctx"].reshape(-1, LANE), "cctx_grad").reshape(D)
    shapes = [W[k].shape for k in SMALL]
    sd, sm, sv = _adamw(_pack([W[k] for k in SMALL]), _pack([small_grads[k] for k in SMALL]),
                        _pack([M[k] for k in SMALL]), _pack([V[k] for k in SMALL]), "adamw_small")
    for k, d_, m_, v_ in zip(SMALL, _unpack(sd, shapes), _unpack(sm, shapes), _unpack(sv, shapes)):
        delta[k], new_m[k], new_v[k] = d_, m_, v_
    grads = dict(grads_big)
    grads["w_ada"] = grad_w_ada
    grads.update(small_grads)
    return (loss, grad_x, *[grads[k] for k in WEIGHTS], *[delta[k] for k in WEIGHTS],
            *[new_m[k] for k in WEIGHTS], *[new_v[k] for k in WEIGHTS])


def kernel(x, c, ctx, c_ctx, w_ada, b_ada, w_in, q_norm_g, k_norm_g, pool_w, pool_scale, sgu_ln_g, sgu_ln_b, sgu_w, sgu_b, conv_w, w_br_attn, w_br_pool, w_br_sgu, w_br_conv, w_gate, b_gate, w_o, ln1_g, ln1_b, w_ff_gate, w_ff_up, w_ff_down, ln2_g, ln2_b, loss_target, m_c_ctx, m_w_ada, m_b_ada, m_w_in, m_q_norm_g, m_k_norm_g, m_pool_w, m_pool_scale, m_sgu_ln_g, m_sgu_ln_b, m_sgu_w, m_sgu_b, m_conv_w, m_w_br_attn, m_w_br_pool, m_w_br_sgu, m_w_br_conv, m_w_gate, m_b_gate, m_w_o, m_ln1_g, m_ln1_b, m_w_ff_gate, m_w_ff_up, m_w_ff_down, m_ln2_g, m_ln2_b, v_c_ctx, v_w_ada, v_b_ada, v_w_in, v_q_norm_g, v_k_norm_g, v_pool_w, v_pool_scale, v_sgu_ln_g, v_sgu_ln_b, v_sgu_w, v_sgu_b, v_conv_w, v_w_br_attn, v_w_br_pool, v_w_br_sgu, v_w_br_conv, v_w_gate, v_b_gate, v_w_o, v_ln1_g, v_ln1_b, v_w_ff_gate, v_w_ff_up, v_w_ff_down, v_ln2_g, v_ln2_b):
    args = locals()
    W = {k: args[k] for k in WEIGHTS}
    M = {k: args["m_" + k] for k in WEIGHTS}
    V = {k: args["v_" + k] for k in WEIGHTS}
    return _step(x, c, ctx, loss_target, W, M, V)
```

```python
import functools
import math

import jax
import jax.numpy as jnp
from jax import lax
from jax.experimental import pallas as pl
from jax.experimental.pallas import tpu as pltpu

F32 = jnp.float32
BF16 = jnp.bfloat16
MESH = pl.DeviceIdType.MESH

N_DEV = 8
N_CHIP = 4
GRID_W = 64
HEAD_DIM = 128
N_HEADS = 8
N_KV = 2
KV_GROUP = N_HEADS // N_KV
Q_W = N_HEADS * HEAD_DIM
KV_W = N_KV * HEAD_DIM
QKV_W = Q_W + 2 * KV_W
GC = 128
N_GROUP = 4
BR_W = N_GROUP * GC
POOL_WINDOWS = (2, 4, 8, 16)
OFF_POOL = QKV_W
OFF_U = OFF_POOL + BR_W
OFF_VG = OFF_U + BR_W
OFF_CB = OFF_VG + BR_W
OFF_CC = OFF_CB + BR_W
OFF_CX = OFF_CC + BR_W
IN_W = OFF_CX + BR_W
Q_BLOCK = 128
ROPE_THETA = 10000.0
LN_EPS = 1e-5
RMS_EPS = 1e-6
ADAM_LR = 0.001
ADAM_B1 = 0.9
ADAM_B2 = 0.999
ADAM_EPS = 1e-08
ADAM_WD = 0.01
ADAM_STEP = 10

ROW_TM = 256
LANE = 128
SUBLANE = 8
VMEM_BIG = 56 * 1024 * 1024
VMEM_MID = 40 * 1024 * 1024
NT_WHOLE_BYTES = 16 * 1024 * 1024


def _cp(sem=None, vmem=VMEM_MID):
    return pltpu.CompilerParams(dimension_semantics=sem, vmem_limit_bytes=vmem)


_ANY = pl.BlockSpec(memory_space=pl.ANY)
_HBM = pl.BlockSpec(memory_space=pltpu.HBM)
_SEM = pl.BlockSpec(memory_space=pltpu.SEMAPHORE)
_EFFECT = pltpu.SideEffectType.DATAFLOW_SIDE_EFFECTING


def _tile(n, pref, unit=LANE, whole_ok=False):
    if n <= pref:
        return n
    best = None
    for t in range(unit, pref + 1, unit):
        if n % t == 0:
            best = t
    if whole_ok and (best is None or best * 4 < pref):
        return n
    assert best is not None, (n, pref, unit)
    return best


def _layer_of(w, l):
    if isinstance(w, (list, tuple)):
        return w[l][:, None], 0
    return w, l


def _mm_nn(a, w, l, kind, *, name, out_dtype=F32, bias=None, act=None, tm_pref=768, tn_pref=1152):
    w, l = _layer_of(w, l)
    T, K = a.shape
    tm = _tile(T, tm_pref, SUBLANE * 2)
    if kind == "col":
        nq, _, kw, ns = w.shape
        assert kw == K
        tn = _tile(ns, tn_pref, whole_ok=True)
        nj = ns // tn
        n_total = nq * ns
        w_spec = pl.BlockSpec((None, None, K, tn), lambda j, i: (j // nj, l, 0, j % nj))
        grid_n = nq * nj
    else:
        nq, _, kc, n_total = w.shape
        assert nq * kc == K
        tn = _tile(n_total, tn_pref)
        w_spec = pl.BlockSpec((nq, None, kc, tn), lambda j, i: (0, l, 0, j))
        grid_n = n_total // tn

    def body(*refs):
        if bias is not None:
            a_ref, w_ref, b_ref, o_ref = refs
        else:
            a_ref, w_ref, o_ref = refs
        wv = w_ref[...]
        if kind == "row":
            wv = wv.reshape(K, tn)
        acc = jnp.dot(a_ref[...], wv, preferred_element_type=F32)
        if bias is not None:
            acc = acc + b_ref[...]
        if act is not None:
            acc = act(acc)
        o_ref[...] = acc.astype(out_dtype)

    in_specs = [pl.BlockSpec((tm, K), lambda j, i: (i, 0)), w_spec]
    args = [a, w]
    if bias is not None:
        in_specs.append(pl.BlockSpec((1, tn), lambda j, i: (0, j)))
        args.append(bias)
    return pl.pallas_call(
        body, grid=(grid_n, T // tm), in_specs=in_specs,
        out_specs=pl.BlockSpec((tm, tn), lambda j, i: (i, j)),
        out_shape=jax.ShapeDtypeStruct((T, n_total), out_dtype),
        name=name, compiler_params=_cp(("parallel", "parallel"), VMEM_BIG),
    )(*args)


def _mm_nt(dc, w, l, kind, *, name, out_dtype=F32, dc_off=0, tm_pref=768, tk_pref=1024):
    w, l = _layer_of(w, l)
    T = dc.shape[0]
    tm = _tile(T, tm_pref, SUBLANE * 2)
    if kind == "col":
        nq, _, K, ns = w.shape
        assert dc_off % ns == 0
        offb = dc_off // ns
        tks = _tile(K, 512)
        whole_bytes = 2 * (tm * nq * ns + nq * tks * ns)
        if dc_off == 0 and dc.shape[1] == nq * ns and whole_bytes <= NT_WHOLE_BYTES:
            def body_whole(dc_ref, w_ref, o_ref):
                acc = None
                for q in range(nq):
                    part = lax.dot_general(dc_ref[:, q * ns:(q + 1) * ns], w_ref[q], (((1,), (1,)), ((), ())),
                                           preferred_element_type=F32)
                    acc = part if acc is None else acc + part
                o_ref[...] = acc.astype(out_dtype)

            return pl.pallas_call(
                body_whole, grid=(T // tm, K // tks),
                in_specs=[pl.BlockSpec((tm, nq * ns), lambda i, kk: (i, 0)),
                          pl.BlockSpec((nq, None, tks, ns), lambda i, kk: (0, l, kk, 0))],
                out_specs=pl.BlockSpec((tm, tks), lambda i, kk: (i, kk)),
                out_shape=jax.ShapeDtypeStruct((T, K), out_dtype),
                name=name, compiler_params=_cp(("parallel", "parallel"), VMEM_BIG),
            )(dc, w)
        tk = _tile(K, tk_pref)

        def body(dc_ref, w_ref, o_ref, acc_ref):
            q = pl.program_id(2)
            part = lax.dot_general(dc_ref[...], w_ref[...], (((1,), (1,)), ((), ())),
                                   preferred_element_type=F32)

            @pl.when(q == 0)
            def _():
                acc_ref[...] = part

            @pl.when(q > 0)
            def _():
                acc_ref[...] += part

            @pl.when(q == nq - 1)
            def _():
                o_ref[...] = acc_ref[...].astype(out_dtype)

        return pl.pallas_call(
            body, grid=(K // tk, T // tm, nq),
            in_specs=[pl.BlockSpec((tm, ns), lambda kk, i, q: (i, offb + q)),
                      pl.BlockSpec((None, None, tk, ns), lambda kk, i, q: (q, l, kk, 0))],
            out_specs=pl.BlockSpec((tm, tk), lambda kk, i, q: (i, kk)),
            out_shape=jax.ShapeDtypeStruct((T, K), out_dtype),
            scratch_shapes=[pltpu.VMEM((tm, tk), F32)],
            name=name, compiler_params=_cp(("parallel", "parallel", "arbitrary"), VMEM_BIG),
        )(dc, w)

    nq, _, kc, n = w.shape
    assert dc_off % n == 0
    offb = dc_off // n
    tk = _tile(kc, tk_pref, whole_ok=True)
    nkk = kc // tk

    def body(dc_ref, w_ref, o_ref):
        o_ref[...] = lax.dot_general(dc_ref[...], w_ref[...], (((1,), (1,)), ((), ())),
                                     preferred_element_type=F32).astype(out_dtype)

    return pl.pallas_call(
        body, grid=(nq * nkk, T // tm),
        in_specs=[pl.BlockSpec((tm, n), lambda j, i: (i, offb)),
                  pl.BlockSpec((None, None, tk, n), lambda j, i: (j // nkk, l, j % nkk, 0))],
        out_specs=pl.BlockSpec((tm, tk), lambda j, i: (i, j)),
        out_shape=jax.ShapeDtypeStruct((T, nq * kc), out_dtype),
        name=name, compiler_params=_cp(("parallel", "parallel"), VMEM_BIG),
    )(dc, w)


def _mm_tn(a, dc, kind, *, nq, kdim, ndim, name, a_off=0, dc_off=0, tk_pref=512, tn_pref=1152, out_dtype=BF16,
           after=()):
    T = a.shape[0]
    assert dc.shape[0] == T
    tk = _tile(kdim, tk_pref, whole_ok=True)
    tn = _tile(ndim, tn_pref, whole_ok=True)
    nkk, njn = kdim // tk, ndim // tn
    assert a_off % tk == 0 and dc_off % tn == 0
    aoffb, doffb = a_off // tk, dc_off // tn
    if kind == "col":
        a_map = lambda q, kk, jn: (0, aoffb + kk)
        d_map = lambda q, kk, jn: (0, doffb + q * njn + jn)
    else:
        a_map = lambda q, kk, jn: (0, aoffb + q * nkk + kk)
        d_map = lambda q, kk, jn: (0, doffb + jn)

    def body(a_ref, d_ref, *rest):
        rest[-1][...] = lax.dot_general(a_ref[...], d_ref[...], (((0,), (0,)), ((), ())),
                                        preferred_element_type=F32).astype(out_dtype)

    return pl.pallas_call(
        body, grid=(nq, nkk, njn),
        in_specs=[pl.BlockSpec((T, tk), a_map), pl.BlockSpec((T, tn), d_map)] + [_ANY] * len(after),
        out_specs=pl.BlockSpec((None, tk, tn), lambda q, kk, jn: (q, kk, jn)),
        out_shape=jax.ShapeDtypeStruct((nq, kdim, ndim), out_dtype),
        name=name, compiler_params=_cp(("parallel", "parallel", "parallel"), VMEM_BIG),
    )(a, dc, *after)


def _row_spec(d):
    return pl.BlockSpec((ROW_TM, d), lambda i: (i, 0))


def _mod_spec(d, ncb):
    return pl.BlockSpec((None, 1, d), lambda i: (jnp.where(i >= ncb, 1, 0), 0, 0))


def _vec_spec(d):
    return pl.BlockSpec((1, d), lambda i: (0, 0))


def _part_spec(d):
    return pl.BlockSpec((None, SUBLANE, d), lambda i: (i, 0, 0))


def _modulate(x, sc, sh, ncb, name):
    T, D = x.shape

    def body(x_ref, sc_ref, sh_ref, o_ref):
        o_ref[...] = (x_ref[...] * (1.0 + sc_ref[...]) + sh_ref[...]).astype(BF16)

    return pl.pallas_call(
        body, grid=(T // ROW_TM,),
        in_specs=[_row_spec(D), _mod_spec(D, ncb), _mod_spec(D, ncb)],
        out_specs=_row_spec(D), out_shape=jax.ShapeDtypeStruct((T, D), BF16),
        name=name, compiler_params=_cp(("parallel",)),
    )(x, sc, sh)


def _ln_stats(r):
    mu = jnp.mean(r, axis=-1, keepdims=True)
    rc = r - mu
    var = jnp.mean(rc * rc, axis=-1, keepdims=True)
    rstd = lax.rsqrt(var + LN_EPS)
    return rc * rstd, rstd


def _resid_ln(x, y, gate, g, b, sc, sh, ncb, alpha, name):
    T, D = x.shape

    def body(x_ref, y_ref, gate_ref, g_ref, b_ref, sc_ref, sh_ref, xo_ref, h_ref):
        xhat, _ = _ln_stats(alpha * x_ref[...] + gate_ref[...] * y_ref[...])
        xo = xhat * g_ref[...] + b_ref[...]
        xo_ref[...] = xo
        h_ref[...] = (xo * (1.0 + sc_ref[...]) + sh_ref[...]).astype(BF16)

    return pl.pallas_call(
        body, grid=(T // ROW_TM,),
        in_specs=[_row_spec(D), _row_spec(D), _mod_spec(D, ncb), _vec_spec(D), _vec_spec(D),
                  _mod_spec(D, ncb), _mod_spec(D, ncb)],
        out_specs=[_row_spec(D), _row_spec(D)],
        out_shape=[jax.ShapeDtypeStruct((T, D), F32), jax.ShapeDtypeStruct((T, D), BF16)],
        name=name, compiler_params=_cp(("parallel",)),
    )(x, y, gate, g, b, sc, sh)


def _resid_ln_loss(x, y, gate, g, b, target, ncb, alpha, name):
    T, D = x.shape
    nblk = T // ROW_TM

    def body(x_ref, y_ref, gate_ref, g_ref, b_ref, t_ref, dy_ref, loss_ref):
        i = pl.program_id(0)
        xhat, _ = _ln_stats(alpha * x_ref[...] + gate_ref[...] * y_ref[...])
        xo = xhat * g_ref[...] + b_ref[...]
        live = (i >= ncb).astype(F32)
        err = (xo - t_ref[...]) * live
        dy_ref[...] = err * (1.0 / D)
        loss_ref[...] = jnp.full((SUBLANE, LANE), 0.5 / D, F32) * jnp.sum(err * err)

    return pl.pallas_call(
        body, grid=(nblk,),
        in_specs=[_row_spec(D), _row_spec(D), _mod_spec(D, ncb), _vec_spec(D), _vec_spec(D),
                  pl.BlockSpec((ROW_TM, D), lambda i: (jnp.maximum(i - ncb, 0), 0))],
        out_specs=[_row_spec(D), _part_spec(LANE)],
        out_shape=[jax.ShapeDtypeStruct((T, D), F32), jax.ShapeDtypeStruct((nblk, SUBLANE, LANE), F32)],
        name=name, compiler_params=_cp(("parallel",)),
    )(x, y, gate, g, b, target)


def _write_parts(part_ref, rows, d):
    for k, r in enumerate(rows):
        part_ref[pl.ds(k, 1), :] = jnp.sum(r, axis=0, keepdims=True)
    if len(rows) < SUBLANE:
        part_ref[pl.ds(len(rows), SUBLANE - len(rows)), :] = jnp.zeros((SUBLANE - len(rows), d), F32)


def _ln_bwd(dxa, dhs, sc, x, y, gate, g, b, ncb, alpha, name, after=None):
    T, D = x.shape
    nblk = T // ROW_TM
    ndh = len(dhs)

    def body(*refs):
        dxa_ref = refs[0]
        dh_refs = refs[1:1 + ndh]
        sc_ref, x_ref, y_ref, gate_ref, g_ref, b_ref = refs[1 + ndh:7 + ndh]
        dx_ref, dy_ref, part_ref = refs[-3:]
        yv = y_ref[...]
        xhat, rstd = _ln_stats(alpha * x_ref[...] + gate_ref[...] * yv)
        dxo = dxa_ref[...]
        rows = []
        if ndh:
            dh = dh_refs[0][...]
            for r in dh_refs[1:]:
                dh = dh + r[...]
            dxo = dxo + dh * (1.0 + sc_ref[...])
            xo = xhat * g_ref[...] + b_ref[...]
            rows = [dh * xo, dh]
        dxhat = dxo * g_ref[...]
        m1 = jnp.mean(dxhat, axis=-1, keepdims=True)
        m2 = jnp.mean(dxhat * xhat, axis=-1, keepdims=True)
        dr = rstd * (dxhat - m1 - xhat * m2)
        dx_ref[...] = alpha * dr
        dy_ref[...] = (gate_ref[...] * dr).astype(BF16)
        _write_parts(part_ref, [dxo * xhat, dxo, dr * yv] + rows, D)

    in_specs = ([_row_spec(D)] * (1 + ndh)
                + [_mod_spec(D, ncb), _row_spec(D), _row_spec(D), _mod_spec(D, ncb), _vec_spec(D), _vec_spec(D)])
    extra = list(after or ())
    return pl.pallas_call(
        body, grid=(nblk,), in_specs=in_specs + [_ANY] * len(extra),
        out_specs=[_row_spec(D), _row_spec(D), _part_spec(D)],
        out_shape=[jax.ShapeDtypeStruct((T, D), F32), jax.ShapeDtypeStruct((T, D), BF16),
                   jax.ShapeDtypeStruct((nblk, SUBLANE, D), F32)],
        name=name, compiler_params=_cp(("parallel",)),
    )(dxa, *dhs, sc, x, y, gate, g, b, *extra)


def _mod_bwd(dxa, dhs, sc, x, ncb, name, after=()):
    T, D = x.shape
    nblk = T // ROW_TM
    ndh = len(dhs)

    def body(*refs):
        dxa_ref = refs[0]
        dh_refs = refs[1:1 + ndh]
        sc_ref, x_ref = refs[1 + ndh:3 + ndh]
        dx_ref, part_ref = refs[-2:]
        dh = dh_refs[0][...]
        for r in dh_refs[1:]:
            dh = dh + r[...]
        dx_ref[...] = dxa_ref[...] + dh * (1.0 + sc_ref[...])
        _write_parts(part_ref, [dh * x_ref[...], dh], D)

    return pl.pallas_call(
        body, grid=(nblk,),
        in_specs=[_row_spec(D)] * (1 + ndh) + [_mod_spec(D, ncb), _row_spec(D)] + [_ANY] * len(after),
        out_specs=[_row_spec(D), _part_spec(D)],
        out_shape=[jax.ShapeDtypeStruct((T, D), F32), jax.ShapeDtypeStruct((nblk, SUBLANE, D), F32)],
        name=name, compiler_params=_cp(("parallel",)),
    )(dxa, *dhs, sc, x, *after)


def _merge_fwd(gates, ts, name):
    T, D = ts[0].shape
    tm = 128

    def body(g_ref, t0, t1, t2, t3, o_ref):
        acc = g_ref[:, 0:D].astype(F32) * t0[...].astype(F32)
        for k, t in enumerate((t1, t2, t3), start=1):
            acc = acc + g_ref[:, k * D:(k + 1) * D].astype(F32) * t[...].astype(F32)
        o_ref[...] = acc.astype(BF16)

    rs = pl.BlockSpec((tm, D), lambda i: (i, 0))
    return pl.pallas_call(
        body, grid=(T // tm,),
        in_specs=[pl.BlockSpec((tm, 4 * D), lambda i: (i, 0)), rs, rs, rs, rs],
        out_specs=rs, out_shape=jax.ShapeDtypeStruct((T, D), BF16),
        name=name, compiler_params=_cp(("parallel",)),
    )(gates, *ts)


def _merge_bwd(dmg, gates, ts, name):
    T, D = dmg.shape
    tm = 128
    nblk = T // tm

    def body(d_ref, g_ref, t0, t1, t2, t3, dpre_ref, dt0, dt1, dt2, dt3, part_ref):
        d = d_ref[...]
        for k, (t, dt) in enumerate(zip((t0, t1, t2, t3), (dt0, dt1, dt2, dt3))):
            gk = g_ref[:, k * D:(k + 1) * D].astype(F32)
            dt[...] = (d * gk).astype(BF16)
            dpre = d * t[...].astype(F32) * gk * (1.0 - gk)
            dpre_ref[:, k * D:(k + 1) * D] = dpre.astype(BF16)
            part_ref[:, k * D:(k + 1) * D] = jnp.sum(dpre, axis=0, keepdims=True)

    rs = pl.BlockSpec((tm, D), lambda i: (i, 0))
    wide = pl.BlockSpec((tm, 4 * D), lambda i: (i, 0))
    return pl.pallas_call(
        body, grid=(nblk,),
        in_specs=[rs, wide, rs, rs, rs, rs],
        out_specs=[wide, rs, rs, rs, rs, pl.BlockSpec((None, 1, 4 * D), lambda i: (i, 0, 0))],
        out_shape=[jax.ShapeDtypeStruct((T, 4 * D), BF16)] + [jax.ShapeDtypeStruct((T, D), BF16)] * 4
                  + [jax.ShapeDtypeStruct((nblk, 1, 4 * D), F32)],
        name=name, compiler_params=_cp(("parallel",)),
    )(dmg, gates, *ts)


def _ffn_up(a, wg, wu, name, tm_pref=384):
    T, K = a.shape
    nq, _, ns = wg.shape
    tm = _tile(T, tm_pref, SUBLANE * 2)

    def body(a_ref, g_ref, u_ref, gg_ref, uu_ref, act_ref):
        av = a_ref[...]
        g = jnp.dot(av, g_ref[...], preferred_element_type=F32)
        u = jnp.dot(av, u_ref[...], preferred_element_type=F32)
        gg_ref[...] = g.astype(BF16)
        uu_ref[...] = u.astype(BF16)
        act_ref[...] = (g * jax.nn.sigmoid(g) * u).astype(BF16)

    ws = pl.BlockSpec((None, K, ns), lambda q, i: (q, 0, 0))
    os = pl.BlockSpec((tm, ns), lambda q, i: (i, q))
    return pl.pallas_call(
        body, grid=(nq, T // tm),
        in_specs=[pl.BlockSpec((tm, K), lambda q, i: (i, 0)), ws, ws], out_specs=[os, os, os],
        out_shape=[jax.ShapeDtypeStruct((T, nq * ns), BF16)] * 3,
        name=name, compiler_params=_cp(("parallel", "parallel"), VMEM_BIG),
    )(a, wg, wu)


def _ffn_down_bwd(dff, wd, gg, uu, name, tm_pref=384):
    T, N = dff.shape
    nq, kc, _ = wd.shape
    tm = _tile(T, tm_pref, SUBLANE * 2)

    def body(d_ref, w_ref, g_ref, u_ref, dg_ref, du_ref):
        da = lax.dot_general(d_ref[...], w_ref[...], (((1,), (1,)), ((), ())), preferred_element_type=F32)
        g = g_ref[...].astype(F32)
        sig = jax.nn.sigmoid(g)
        dg_ref[...] = (da * u_ref[...].astype(F32) * sig * (1.0 + g * (1.0 - sig))).astype(BF16)
        du_ref[...] = (da * g * sig).astype(BF16)

    ts = pl.BlockSpec((tm, kc), lambda q, i: (i, q))
    return pl.pallas_call(
        body, grid=(nq, T // tm),
        in_specs=[pl.BlockSpec((tm, N), lambda q, i: (i, 0)), pl.BlockSpec((None, kc, N), lambda q, i: (q, 0, 0)),
                  ts, ts],
        out_specs=[ts, ts], out_shape=[jax.ShapeDtypeStruct((T, nq * kc), BF16)] * 2,
        name=name, compiler_params=_cp(("parallel", "parallel"), VMEM_BIG),
    )(dff, wd, gg, uu)


def _swap_halves(v):
    lane = lax.broadcasted_iota(jnp.int32, v.shape, 1)
    return jnp.where((lane % 64) < 32, pltpu.roll(v, 96, 1), pltpu.roll(v, 32, 1))


def _rope_tables(n, nc):
    rows = n // GRID_W
    row = jnp.repeat(jnp.arange(rows), GRID_W).astype(F32)
    col = jnp.tile(jnp.arange(GRID_W), rows).astype(F32)
    inv = ROPE_THETA ** (-jnp.arange(0, 64, 2, dtype=F32) / 64)
    ang_r = row[:, None] * inv
    ang_c = col[:, None] * inv
    cos = jnp.concatenate([jnp.cos(ang_r), jnp.cos(ang_r), jnp.cos(ang_c), jnp.cos(ang_c)], axis=-1)
    sin = jnp.concatenate([-jnp.sin(ang_r), jnp.sin(ang_r), -jnp.sin(ang_c), jnp.sin(ang_c)], axis=-1)
    cos = jnp.concatenate([jnp.ones((nc, HEAD_DIM), F32), cos], axis=0)
    sin = jnp.concatenate([jnp.zeros((nc, HEAD_DIM), F32), sin], axis=0)
    return cos, sin


def _qk_fwd(p, cos, sin, qg, kg, name):
    T = p.shape[0]

    def body(p_ref, c_ref, s_ref, qg_ref, kg_ref, q_ref, k_ref, v_ref):
        c, s = c_ref[...], s_ref[...]
        for h in range(N_HEADS + N_KV):
            x = p_ref[:, h * HEAD_DIM:(h + 1) * HEAD_DIM]
            rs = lax.rsqrt(jnp.mean(x * x, axis=-1, keepdims=True) + RMS_EPS)
            gain = qg_ref[...] if h < N_HEADS else kg_ref[...]
            yv = x * rs * gain
            out = yv * c + _swap_halves(yv) * s
            if h < N_HEADS:
                out = out * Q_PRESCALE
            out = out.astype(BF16)
            if h < N_HEADS:
                q_ref[:, h * HEAD_DIM:(h + 1) * HEAD_DIM] = out
            else:
                k_ref[:, (h - N_HEADS) * HEAD_DIM:(h - N_HEADS + 1) * HEAD_DIM] = out
        v_ref[...] = p_ref[:, Q_W + KV_W:QKV_W].astype(BF16)

    return pl.pallas_call(
        body, grid=(T // ROW_TM,),
        in_specs=[_row_spec(QKV_W), _row_spec(HEAD_DIM), _row_spec(HEAD_DIM), _vec_spec(HEAD_DIM), _vec_spec(HEAD_DIM)],
        out_specs=[_row_spec(Q_W), _row_spec(KV_W), _row_spec(KV_W)],
        out_shape=[jax.ShapeDtypeStruct((T, Q_W), BF16), jax.ShapeDtypeStruct((T, KV_W), BF16),
                   jax.ShapeDtypeStruct((T, KV_W), BF16)],
        name=name, compiler_params=_cp(("parallel",)),
    )(p, cos, sin, qg, kg)


def _qk_bwd(p, dq, dk, dv, cos, sin, qg, kg, name):
    T = p.shape[0]
    nblk = T // ROW_TM

    def body(p_ref, dq_ref, dk_ref, dv_ref, c_ref, s_ref, qg_ref, kg_ref, dp_ref, part_ref):
        c, s = c_ref[...], s_ref[...]
        dgq = jnp.zeros((1, HEAD_DIM), F32)
        dgk = jnp.zeros((1, HEAD_DIM), F32)
        for h in range(N_HEADS + N_KV):
            x = p_ref[:, h * HEAD_DIM:(h + 1) * HEAD_DIM]
            if h < N_HEADS:
                d = dq_ref[:, h * HEAD_DIM:(h + 1) * HEAD_DIM]
                gain = qg_ref[...]
            else:
                d = dk_ref[:, (h - N_HEADS) * HEAD_DIM:(h - N_HEADS + 1) * HEAD_DIM]
                gain = kg_ref[...]
            dyv = d * c + _swap_halves(d * s)
            rs = lax.rsqrt(jnp.mean(x * x, axis=-1, keepdims=True) + RMS_EPS)
            xn = x * rs
            dgsum = jnp.sum(dyv * xn, axis=0, keepdims=True)
            if h < N_HEADS:
                dgq = dgq + dgsum
            else:
                dgk = dgk + dgsum
            dxg = dyv * gain
            dx = rs * (dxg - xn * jnp.mean(dxg * xn, axis=-1, keepdims=True))
            dp_ref[:, h * HEAD_DIM:(h + 1) * HEAD_DIM] = dx.astype(BF16)
        dp_ref[:, Q_W + KV_W:QKV_W] = dv_ref[...].astype(BF16)
        part_ref[pl.ds(0, 1), :] = dgq
        part_ref[pl.ds(1, 1), :] = dgk
        part_ref[pl.ds(2, SUBLANE - 2), :] = jnp.zeros((SUBLANE - 2, HEAD_DIM), F32)

    return pl.pallas_call(
        body, grid=(nblk,),
        in_specs=[_row_spec(QKV_W), _row_spec(Q_W), _row_spec(KV_W), _row_spec(KV_W),
                  _row_spec(HEAD_DIM), _row_spec(HEAD_DIM), _vec_spec(HEAD_DIM), _vec_spec(HEAD_DIM)],
        out_specs=[_row_spec(QKV_W), _part_spec(HEAD_DIM)],
        out_shape=[jax.ShapeDtypeStruct((T, IN_W), BF16), jax.ShapeDtypeStruct((nblk, SUBLANE, HEAD_DIM), F32)],
        name=name, compiler_params=_cp(("parallel",)),
    )(p, dq, dk, dv, cos, sin, qg, kg)


ATTN_SCALE = HEAD_DIM ** -0.5
LOG2E = 1.4426950408889634
Q_PRESCALE = ATTN_SCALE * LOG2E


def _attn_weights(q, k):
    s = lax.dot_general(q, k, (((1,), (1,)), ((), ())), preferred_element_type=F32)
    e = jnp.exp2(s - jnp.max(s, axis=-1, keepdims=True))
    return e, 1.0 / jnp.sum(e, axis=-1, keepdims=True)


def _attn_fwd(q, k, v, nc, name):
    T = q.shape[0]

    def heads(q_ref, k_ref, v_ref, o_ref, nkeys):
        for h in range(N_HEADS):
            g = h // KV_GROUP
            kk = k_ref[0:nkeys, g * HEAD_DIM:(g + 1) * HEAD_DIM]
            vv = v_ref[0:nkeys, g * HEAD_DIM:(g + 1) * HEAD_DIM]
            e, rl = _attn_weights(q_ref[:, h * HEAD_DIM:(h + 1) * HEAD_DIM], kk)
            o = jnp.dot(e.astype(BF16), vv, preferred_element_type=F32) * rl
            o_ref[:, h * HEAD_DIM:(h + 1) * HEAD_DIM] = o.astype(BF16)

    def body(q_ref, k_ref, v_ref, o_ref):
        i = pl.program_id(0)

        @pl.when(i < nc // Q_BLOCK)
        def _():
            heads(q_ref, k_ref, v_ref, o_ref, nc)

        @pl.when(i >= nc // Q_BLOCK)
        def _():
            heads(q_ref, k_ref, v_ref, o_ref, T)

    whole = pl.BlockSpec((T, KV_W), lambda i: (0, 0))
    qs = pl.BlockSpec((Q_BLOCK, Q_W), lambda i: (i, 0))
    return pl.pallas_call(
        body, grid=(T // Q_BLOCK,), in_specs=[qs, whole, whole], out_specs=qs,
        out_shape=jax.ShapeDtypeStruct((T, Q_W), BF16),
        name=name, compiler_params=_cp(("parallel",)),
    )(q, k, v)


def _attn_bwd(q, k, v, do, nc, name):
    T = q.shape[0]

    def heads(q_ref, k_ref, v_ref, do_ref, dq_ref, dk_ref, dv_ref, nkeys):
        for h in range(N_HEADS):
            g = h // KV_GROUP
            cols = slice(g * HEAD_DIM, (g + 1) * HEAD_DIM)
            hc = slice(h * HEAD_DIM, (h + 1) * HEAD_DIM)
            qh, kk, vv, doh = q_ref[:, hc], k_ref[0:nkeys, cols], v_ref[0:nkeys, cols], do_ref[:, hc]
            e, rl = _attn_weights(qh, kk)
            dpr = lax.dot_general(doh, vv, (((1,), (1,)), ((), ())), preferred_element_type=F32)
            delta = jnp.sum(e * dpr, axis=-1, keepdims=True) * rl
            dsu = (e * (dpr - delta)).astype(BF16)
            dq_ref[:, hc] = jnp.dot(dsu, kk, preferred_element_type=F32) * (rl * ATTN_SCALE)
            q_rows = (qh.astype(F32) * (rl * (1.0 / LOG2E))).astype(BF16)
            dk_ref[0:nkeys, cols] += lax.dot_general(dsu, q_rows, (((0,), (0,)), ((), ())),
                                                     preferred_element_type=F32)
            do_rows = (doh.astype(F32) * rl).astype(BF16)
            dv_ref[0:nkeys, cols] += lax.dot_general(e.astype(BF16), do_rows, (((0,), (0,)), ((), ())),
                                                     preferred_element_type=F32)

    def body(q_ref, k_ref, v_ref, do_ref, dq_ref, dk_ref, dv_ref):
        i = pl.program_id(0)

        @pl.when(i == 0)
        def _():
            dk_ref[...] = jnp.zeros_like(dk_ref)
            dv_ref[...] = jnp.zeros_like(dv_ref)

        @pl.when(i < nc // Q_BLOCK)
        def _():
            heads(q_ref, k_ref, v_ref, do_ref, dq_ref, dk_ref, dv_ref, nc)

        @pl.when(i >= nc // Q_BLOCK)
        def _():
            heads(q_ref, k_ref, v_ref, do_ref, dq_ref, dk_ref, dv_ref, T)

    whole = pl.BlockSpec((T, KV_W), lambda i: (0, 0))
    qs = pl.BlockSpec((Q_BLOCK, Q_W), lambda i: (i, 0))
    return pl.pallas_call(
        body, grid=(T // Q_BLOCK,), in_specs=[qs, whole, whole, qs], out_specs=[qs, whole, whole],
        out_shape=[jax.ShapeDtypeStruct((T, Q_W), F32), jax.ShapeDtypeStruct((T, KV_W), F32),
                   jax.ShapeDtypeStruct((T, KV_W), F32)],
        name=name, compiler_params=_cp(("arbitrary",)),
    )(q, k, v, do)


def _shift_rows(z, o, nc):
    T = z.shape[0]
    t = lax.broadcasted_iota(jnp.int32, (T, 1), 0)
    lo = jnp.where(t < nc, 0, nc)
    hi = jnp.where(t < nc, nc, T)
    ok = jnp.logical_and(t + o >= lo, t + o < hi)
    rolled = z if o == 0 else pltpu.roll(z, (-o) % T, 0)
    return jnp.where(ok, rolled, 0.0), ok


def _pool_window(w):
    left = w // 2
    return -left, w - 1 - left


def _pool_d(z, w, nc):
    o0, o1 = _pool_window(w)
    tot = jnp.zeros_like(z)
    cnt = jnp.zeros((z.shape[0], 1), F32)
    for o in range(o0, o1 + 1):
        sh, ok = _shift_rows(z, o, nc)
        tot = tot + sh
        cnt = cnt + ok.astype(F32)
    return tot / cnt - z, cnt


def _pool_fwd(p, pool_w, pool_scale, nc, name):
    T = p.shape[0]

    def body(z_ref, w_ref, s_ref, o_ref):
        for g, w in enumerate(POOL_WINDOWS):
            cs = slice(g * GC, (g + 1) * GC)
            d, _ = _pool_d(z_ref[:, cs], w, nc)
            yv = jnp.dot(d.astype(BF16), w_ref[g].astype(BF16), preferred_element_type=F32)
            o_ref[:, cs] = (yv * s_ref[:, cs]).astype(BF16)

    return pl.pallas_call(
        body, grid=(1,),
        in_specs=[pl.BlockSpec((T, BR_W), lambda i: (0, OFF_POOL // BR_W)),
                  pl.BlockSpec((N_GROUP, GC, GC), lambda i: (0, 0, 0)), _vec_spec(BR_W)],
        out_specs=pl.BlockSpec((T, BR_W), lambda i: (0, 0)),
        out_shape=jax.ShapeDtypeStruct((T, BR_W), BF16),
        name=name, compiler_params=_cp(("arbitrary",), VMEM_BIG),
    )(p, pool_w, pool_scale)


def _pool_bwd(p, dy, pool_w, pool_scale, dp, nc, name):
    T = p.shape[0]

    def body(z_ref, dy_ref, w_ref, s_ref, dp_in_ref, dz_ref, dw_ref, ds_ref):
        for g, w in enumerate(POOL_WINDOWS):
            cs = slice(g * GC, (g + 1) * GC)
            d, cnt = _pool_d(z_ref[:, cs], w, nc)
            db = d.astype(BF16)
            wb = w_ref[g].astype(BF16)
            dyv = dy_ref[:, cs]
            e = (dyv * s_ref[:, cs]).astype(BF16)
            dw_ref[g] = lax.dot_general(db, e, (((0,), (0,)), ((), ())), preferred_element_type=F32)
            yraw = jnp.dot(db, wb, preferred_element_type=F32)
            ds_ref[:, cs] = jnp.sum(dyv * yraw, axis=0, keepdims=True)
            dd = lax.dot_general(e, wb, (((1,), (1,)), ((), ())), preferred_element_type=F32)
            ec = dd / cnt
            o0, o1 = _pool_window(w)
            tot = jnp.zeros_like(dd)
            for o in range(o0, o1 + 1):
                tot = tot + _shift_rows(ec, -o, nc)[0]
            dz_ref[:, cs] = (tot - dd).astype(BF16)

    return pl.pallas_call(
        body, grid=(1,),
        in_specs=[pl.BlockSpec((T, BR_W), lambda i: (0, OFF_POOL // BR_W)),
                  pl.BlockSpec((T, BR_W), lambda i: (0, 0)),
                  pl.BlockSpec((N_GROUP, GC, GC), lambda i: (0, 0, 0)), _vec_spec(BR_W), _ANY],
        out_specs=[pl.BlockSpec((T, BR_W), lambda i: (0, OFF_POOL // BR_W)),
                   pl.BlockSpec((N_GROUP, GC, GC), lambda i: (0, 0, 0)), _vec_spec(BR_W)],
        out_shape=[jax.ShapeDtypeStruct(dp.shape, BF16), jax.ShapeDtypeStruct((N_GROUP, GC, GC), F32),
                   jax.ShapeDtypeStruct((1, BR_W), F32)],
        input_output_aliases={4: 0},
        name=name, compiler_params=_cp(("arbitrary",), VMEM_BIG),
    )(p, dy, pool_w, pool_scale, dp)


_GELU_C = math.sqrt(2.0 / math.pi)


def _gelu(x):
    return 0.5 * x * (1.0 + jnp.tanh(_GELU_C * (x + 0.044715 * x * x * x)))


def _gelu_grad(x):
    th = jnp.tanh(_GELU_C * (x + 0.044715 * x * x * x))
    return 0.5 * (1.0 + th) + 0.5 * x * (1.0 - th * th) * _GELU_C * (1.0 + 3.0 * 0.044715 * x * x)


def _sgu_fwd(p, ln_g, ln_b, w_s, b_st, name):
    T = p.shape[0]

    def body(pu_ref, pv_ref, g_ref, b_ref, w_ref, bs_ref, o_ref):
        u = _gelu(pu_ref[...])
        vhat, _ = _ln_stats(_gelu(pv_ref[...]))
        vn = (vhat * g_ref[...] + b_ref[...]).astype(BF16)
        for g in range(N_GROUP):
            cs = slice(g * GC, (g + 1) * GC)
            s = jnp.dot(w_ref[g].astype(BF16), vn[:, cs], preferred_element_type=F32) + bs_ref[:, g:g + 1]
            o_ref[:, cs] = (u[:, cs] * s).astype(BF16)

    chunk = lambda off: pl.BlockSpec((GC, BR_W), lambda i: (i, off // BR_W))
    return pl.pallas_call(
        body, grid=(T // GC,),
        in_specs=[chunk(OFF_U), chunk(OFF_VG), _vec_spec(BR_W), _vec_spec(BR_W),
                  pl.BlockSpec((N_GROUP, GC, GC), lambda i: (0, 0, 0)),
                  pl.BlockSpec((GC, N_GROUP), lambda i: (0, 0))],
        out_specs=pl.BlockSpec((GC, BR_W), lambda i: (i, 0)),
        out_shape=jax.ShapeDtypeStruct((T, BR_W), BF16),
        name=name, compiler_params=_cp(("parallel",)),
    )(p, p, ln_g, ln_b, w_s, b_st)


def _sgu_bwd(p, dy, ln_g, ln_b, w_s, b_st, dp, name):
    T = p.shape[0]

    def body(pu_ref, pv_ref, dy_ref, g_ref, b_ref, w_ref, bs_ref, dp_in_ref, dp_ref, dw_ref, dsacc_ref, dln_ref):
        i = pl.program_id(0)

        @pl.when(i == 0)
        def _():
            dw_ref[...] = jnp.zeros_like(dw_ref)
            dsacc_ref[...] = jnp.zeros_like(dsacc_ref)
            dln_ref[...] = jnp.zeros_like(dln_ref)

        pu, pv, dyv = pu_ref[...], pv_ref[...], dy_ref[...]
        u = _gelu(pu)
        vhat, rstd = _ln_stats(_gelu(pv))
        vn = (vhat * g_ref[...] + b_ref[...]).astype(BF16)
        ds = dyv * u
        dsb = ds.astype(BF16)
        dsacc_ref[...] += ds
        dvn_parts = []
        for g in range(N_GROUP):
            cs = slice(g * GC, (g + 1) * GC)
            wb = w_ref[g].astype(BF16)
            s = jnp.dot(wb, vn[:, cs], preferred_element_type=F32) + bs_ref[:, g:g + 1]
            dp_ref[:, cs] = (dyv[:, cs] * s * _gelu_grad(pu[:, cs])).astype(BF16)
            dvn_parts.append(lax.dot_general(wb, dsb[:, cs], (((0,), (0,)), ((), ())),
                                             preferred_element_type=F32))
            dw_ref[g] += lax.dot_general(dsb[:, cs], vn[:, cs], (((1,), (1,)), ((), ())),
                                         preferred_element_type=F32)
        dvn = jnp.concatenate(dvn_parts, axis=-1)
        dln_ref[pl.ds(0, 1), :] += jnp.sum(dvn * vhat, axis=0, keepdims=True)
        dln_ref[pl.ds(1, 1), :] += jnp.sum(dvn, axis=0, keepdims=True)
        dvhat = dvn * g_ref[...]
        m1 = jnp.mean(dvhat, axis=-1, keepdims=True)
        m2 = jnp.mean(dvhat * vhat, axis=-1, keepdims=True)
        dv = rstd * (dvhat - m1 - vhat * m2)
        dp_ref[:, BR_W:2 * BR_W] = (dv * _gelu_grad(pv)).astype(BF16)

    chunk = lambda off: pl.BlockSpec((GC, BR_W), lambda i: (i, off // BR_W))
    return pl.pallas_call(
        body, grid=(T // GC,),
        in_specs=[chunk(OFF_U), chunk(OFF_VG), pl.BlockSpec((GC, BR_W), lambda i: (i, 0)),
                  _vec_spec(BR_W), _vec_spec(BR_W),
                  pl.BlockSpec((N_GROUP, GC, GC), lambda i: (0, 0, 0)),
                  pl.BlockSpec((GC, N_GROUP), lambda i: (0, 0)), _ANY],
        out_specs=[pl.BlockSpec((GC, 2 * BR_W), lambda i: (i, OFF_U // (2 * BR_W))),
                   pl.BlockSpec((N_GROUP, GC, GC), lambda i: (0, 0, 0)),
                   pl.BlockSpec((GC, BR_W), lambda i: (0, 0)),
                   pl.BlockSpec((SUBLANE, BR_W), lambda i: (0, 0))],
        out_shape=[jax.ShapeDtypeStruct(dp.shape, BF16), jax.ShapeDtypeStruct((N_GROUP, GC, GC), F32),
                   jax.ShapeDtypeStruct((GC, BR_W), F32), jax.ShapeDtypeStruct((SUBLANE, BR_W), F32)],
        input_output_aliases={7: 0},
        name=name, compiler_params=_cp(("arbitrary",)),
    )(p, p, dy, ln_g, ln_b, w_s, b_st, dp)


def _conv_fwd(p, conv_w, nc, name):
    T = p.shape[0]

    def body(cb_ref, cc_ref, cx_ref, w_ref, o_ref):
        zz = cc_ref[...] * cx_ref[...]
        conv = (w_ref[0:1, :] * _shift_rows(zz, -1, nc)[0] + w_ref[1:2, :] * zz
                + w_ref[2:3, :] * _shift_rows(zz, 1, nc)[0])
        o_ref[...] = (cb_ref[...] * conv).astype(BF16)

    col = lambda off: pl.BlockSpec((T, GC), lambda j: (0, off // GC + j))
    return pl.pallas_call(
        body, grid=(N_GROUP,),
        in_specs=[col(OFF_CB), col(OFF_CC), col(OFF_CX), pl.BlockSpec((3, GC), lambda j: (0, j))],
        out_specs=pl.BlockSpec((T, GC), lambda j: (0, j)),
        out_shape=jax.ShapeDtypeStruct((T, BR_W), BF16),
        name=name, compiler_params=_cp(("parallel",)),
    )(p, p, p, conv_w)


def _conv_bwd(p, dy, conv_w, dp, nc, name):
    T = p.shape[0]

    def body(cb_ref, cc_ref, cx_ref, dy_ref, w_ref, dp_in_ref, dp_ref, dw_ref):
        part = pl.program_id(1)
        cc, cx = cc_ref[...], cx_ref[...]
        dconv = dy_ref[...] * cb_ref[...]

        @pl.when(part == 0)
        def _():
            zz = cc * cx
            zm, zp = _shift_rows(zz, -1, nc)[0], _shift_rows(zz, 1, nc)[0]
            conv = w_ref[0:1, :] * zm + w_ref[1:2, :] * zz + w_ref[2:3, :] * zp
            dw_ref[0:1, :] = jnp.sum(dconv * zm, axis=0, keepdims=True)
            dw_ref[1:2, :] = jnp.sum(dconv * zz, axis=0, keepdims=True)
            dw_ref[2:3, :] = jnp.sum(dconv * zp, axis=0, keepdims=True)
            dp_ref[...] = (dy_ref[...] * conv).astype(BF16)

        @pl.when(part > 0)
        def _():
            dzz = (w_ref[0:1, :] * _shift_rows(dconv, 1, nc)[0] + w_ref[1:2, :] * dconv
                   + w_ref[2:3, :] * _shift_rows(dconv, -1, nc)[0])
            dp_ref[...] = (dzz * jnp.where(part == 1, cx, cc)).astype(BF16)

    col = lambda off: pl.BlockSpec((T, GC), lambda j, part: (0, off // GC + j))
    return pl.pallas_call(
        body, grid=(N_GROUP, 3),
        in_specs=[col(OFF_CB), col(OFF_CC), col(OFF_CX), pl.BlockSpec((T, GC), lambda j, part: (0, j)),
                  pl.BlockSpec((3, GC), lambda j, part: (0, j)), _ANY],
        out_specs=[pl.BlockSpec((T, GC), lambda j, part: (0, OFF_CB // GC + part * N_GROUP + j)),
                   pl.BlockSpec((3, GC), lambda j, part: (0, j))],
        out_shape=[jax.ShapeDtypeStruct(dp.shape, BF16), jax.ShapeDtypeStruct((3, BR_W), F32)],
        input_output_aliases={5: 0},
        name=name, compiler_params=_cp(("parallel", "arbitrary")),
    )(p, p, p, dy, conv_w, dp)


def _rows_tile(rows, cols, n_arrays):
    budget = 24 * 1024 * 1024 // (2 * 4 * n_arrays * cols)
    return _tile(rows, max(SUBLANE * 2, min(budget, 1024)), SUBLANE * 2) if rows % (SUBLANE * 2) == 0 else rows


def _cast_into(chip, w, l, name, after=()):
    _, K, cols = w.shape
    tr = _rows_tile(K, cols, 2)

    def body(q_ref, w_ref, *rest):
        rest[-1][...] = w_ref[...].astype(BF16)

    return pl.pallas_call(
        body,
        grid_spec=pltpu.PrefetchScalarGridSpec(
            num_scalar_prefetch=1, grid=(K // tr,),
            in_specs=[pl.BlockSpec((None, tr, cols), lambda i, q: (l, i, 0))] + [_ANY] * len(after),
            out_specs=pl.BlockSpec((None, tr, cols), lambda i, q: (q[0], i, 0))),
        out_shape=jax.ShapeDtypeStruct((N_CHIP, K, cols), BF16),
        name=name, compiler_params=_cp(("parallel",)),
    )(chip, w, *after)


def _adamw(w, g, m, v, name):
    shape = w.shape
    cols = shape[-1]
    rows = w.size // cols
    tr = _rows_tile(rows, cols, 7)
    c1 = 1.0 - ADAM_B1 ** ADAM_STEP
    c2 = 1.0 - ADAM_B2 ** ADAM_STEP

    def body(w_ref, g_ref, m_ref, v_ref, d_ref, mo_ref, vo_ref):
        gv = g_ref[...]
        mn = ADAM_B1 * m_ref[...] + (1.0 - ADAM_B1) * gv
        vn = ADAM_B2 * v_ref[...] + (1.0 - ADAM_B2) * (gv * gv)
        mo_ref[...] = mn
        vo_ref[...] = vn
        d_ref[...] = -ADAM_LR * ((mn / c1) / (jnp.sqrt(vn / c2) + ADAM_EPS) + ADAM_WD * w_ref[...])

    rs = pl.BlockSpec((tr, cols), lambda i: (i, 0))
    outs = pl.pallas_call(
        body, grid=(rows // tr,), in_specs=[rs] * 4, out_specs=[rs] * 3,
        out_shape=[jax.ShapeDtypeStruct((rows, cols), F32)] * 3,
        name=name, compiler_params=_cp(("parallel",)),
    )(*[a.reshape(rows, cols) for a in (w, g, m, v)])
    return [o.reshape(shape) for o in outs]


def _adamw_layer(w, g, m, v, l, prev, after, name):
    L, K, cols = w.shape
    tr = _rows_tile(K, cols, 7)
    c1 = 1.0 - ADAM_B1 ** ADAM_STEP
    c2 = 1.0 - ADAM_B2 ** ADAM_STEP
    nprev = 0 if prev is None else 3

    def body(*refs):
        w_ref, g_ref, m_ref, v_ref = refs[:4]
        d_ref, mo_ref, vo_ref = refs[-3:]
        gv = g_ref[...]
        mn = ADAM_B1 * m_ref[...] + (1.0 - ADAM_B1) * gv
        vn = ADAM_B2 * v_ref[...] + (1.0 - ADAM_B2) * (gv * gv)
        mo_ref[...] = mn
        vo_ref[...] = vn
        d_ref[...] = -ADAM_LR * ((mn / c1) / (jnp.sqrt(vn / c2) + ADAM_EPS) + ADAM_WD * w_ref[...])

    rs = pl.BlockSpec((None, tr, cols), lambda i: (l, i, 0))
    extra = list(prev or ()) + list(after)
    return pl.pallas_call(
        body, grid=(K // tr,), in_specs=[rs] * 4 + [_ANY] * len(extra), out_specs=[rs] * 3,
        out_shape=[jax.ShapeDtypeStruct((L, K, cols), F32)] * 3,
        input_output_aliases={4 + k: k for k in range(nprev)},
        name=name, compiler_params=_cp(("parallel",)),
    )(w, g, m, v, *extra)


def _pair_sum(core, dw, recv, name):
    nq, half, cols = recv.shape
    tr = _rows_tile(half, cols, 3)
    nb = half // tr

    def body(c_ref, a_ref, r_ref, o_ref):
        o_ref[...] = (a_ref[...].astype(F32) + r_ref[...].astype(F32)).astype(BF16)

    rs = pl.BlockSpec((None, tr, cols), lambda q, i, c: (q, i, 0))
    return pl.pallas_call(
        body,
        grid_spec=pltpu.PrefetchScalarGridSpec(
            num_scalar_prefetch=1, grid=(nq, nb),
            in_specs=[pl.BlockSpec((None, tr, cols), lambda q, i, c: (q, c[0] * nb + i, 0)), rs],
            out_specs=rs),
        out_shape=jax.ShapeDtypeStruct((nq, half, cols), BF16),
        name=name, compiler_params=_cp(("parallel", "parallel")),
    )(core, dw, recv)


def _chip_sum(chip, core, psum, land, l, n_layers, prev, name):
    _, half, cols = psum.shape
    tr = _rows_tile(half, cols, 5)
    nb = half // tr

    def body(*refs):
        p_ref, r_ref, o_ref = refs[2], refs[3], refs[-1]
        o_ref[...] = (p_ref[...].astype(F32) + r_ref[0].astype(F32) + r_ref[1].astype(F32)
                      + r_ref[2].astype(F32))

    extra = [] if prev is None else [prev]
    return pl.pallas_call(
        body,
        grid_spec=pltpu.PrefetchScalarGridSpec(
            num_scalar_prefetch=2, grid=(nb,),
            in_specs=[pl.BlockSpec((None, tr, cols), lambda i, q, c: (q[0], i, 0)),
                      pl.BlockSpec((3, tr, cols), lambda i, q, c: (0, i, 0))] + [_ANY] * len(extra),
            out_specs=pl.BlockSpec((None, tr, cols), lambda i, q, c: (l, c[0] * nb + i, 0))),
        out_shape=jax.ShapeDtypeStruct((n_layers, 2 * half, cols), F32),
        input_output_aliases={4: 0} if prev is not None else {},
        name=name, compiler_params=_cp(("parallel",)),
    )(chip, core, psum, land, *extra)


def _slab_sum(gathered, name, after=()):
    n, rows, cols = gathered.shape
    tr = _tile(rows, 512, SUBLANE)

    def body(g_ref, *rest):
        acc = g_ref[0]
        for d in range(1, n):
            acc = acc + g_ref[d]
        rest[-1][...] = acc

    return pl.pallas_call(
        body, grid=(rows // tr,),
        in_specs=[pl.BlockSpec((n, tr, cols), lambda i: (0, i, 0))] + [_ANY] * len(after),
        out_specs=pl.BlockSpec((tr, cols), lambda i: (i, 0)),
        out_shape=jax.ShapeDtypeStruct((rows, cols), F32),
        name=name, compiler_params=_cp(("parallel",)),
    )(gathered, *after)


def _silu(x):
    return x * jax.nn.sigmoid(x)


def _ada_fwd(cvec, w_ada, b_cols, name):
    L, D, ns = w_ada.shape
    tn = _tile(ns, 768)

    def body(c_ref, w_ref, b_ref, o_ref):
        s = _silu(c_ref[...]).astype(BF16)
        o_ref[...] = jnp.dot(s, w_ref[...].astype(BF16), preferred_element_type=F32) + b_ref[...]

    return pl.pallas_call(
        body, grid=(L, ns // tn),
        in_specs=[pl.BlockSpec((16, D), lambda l, j: (0, 0)),
                  pl.BlockSpec((None, D, tn), lambda l, j: (l, 0, j)),
                  pl.BlockSpec((None, 1, tn), lambda l, j: (l, 0, j))],
        out_specs=pl.BlockSpec((None, 16, tn), lambda l, j: (l, 0, j)),
        out_shape=jax.ShapeDtypeStruct((L, 16, ns), F32),
        name=name, compiler_params=_cp(("parallel", "parallel")),
    )(cvec, w_ada, b_cols)


def _ada_bwd(cvec, dmod, w_ada, name):
    L, D, ns = w_ada.shape
    tn = _tile(ns, 768)

    def body(c_ref, d_ref, w_ref, gw_ref, ds_ref):
        first = jnp.logical_and(pl.program_id(0) == 0, pl.program_id(1) == 0)

        @pl.when(first)
        def _():
            ds_ref[...] = jnp.zeros_like(ds_ref)

        s = _silu(c_ref[...]).astype(BF16)
        db = d_ref[...].astype(BF16)
        gw_ref[...] = lax.dot_general(s, db, (((0,), (0,)), ((), ())), preferred_element_type=F32)
        ds_ref[...] += lax.dot_general(db, w_ref[...].astype(BF16), (((1,), (1,)), ((), ())),
                                       preferred_element_type=F32)

    return pl.pallas_call(
        body, grid=(L, ns // tn),
        in_specs=[pl.BlockSpec((16, D), lambda l, j: (0, 0)),
                  pl.BlockSpec((None, 16, tn), lambda l, j: (l, 0, j)),
                  pl.BlockSpec((None, D, tn), lambda l, j: (l, 0, j))],
        out_specs=[pl.BlockSpec((None, D, tn), lambda l, j: (l, 0, j)),
                   pl.BlockSpec((16, D), lambda l, j: (0, 0))],
        out_shape=[jax.ShapeDtypeStruct((L, D, ns), F32), jax.ShapeDtypeStruct((16, D), F32)],
        name=name, compiler_params=_cp(("arbitrary", "arbitrary")),
    )(cvec, dmod, w_ada)


def _cctx_grad(gathered, c_ctx, name):
    rows = c_ctx.shape[0]

    def body(g_ref, c_ref, o_ref):
        ds = g_ref[0] + g_ref[2] + g_ref[4] + g_ref[6]
        cv = c_ref[...]
        sig = jax.nn.sigmoid(cv)
        o_ref[...] = ds * sig * (1.0 + cv * (1.0 - sig))

    return pl.pallas_call(
        body, out_shape=jax.ShapeDtypeStruct((rows, LANE), F32), name=name,
    )(gathered, c_ctx)


def _place():
    x, y, c = lax.axis_index("x"), lax.axis_index("y"), lax.axis_index("c")
    chips = [(1 - x, y), (x, 1 - y), (1 - x, 1 - y)]
    return x, y, c, chips


def _all_gather_small(slab, name, after=()):
    rows, cols = slab.shape

    def body(x_ref, *rest):
        out_ref, send_sems, recv_sems, local_sem = rest[len(after):]
        x, y, c, chips = _place()
        me, sibling = (x, y, c), (x, y, 1 - c)

        def blk(px, py, pc):
            return out_ref.at[4 * px + 2 * py + pc]

        def copy(k, block, to, src=None):
            return pltpu.make_async_remote_copy(
                src_ref=blk(*block) if src is None else src, dst_ref=blk(*block),
                send_sem=send_sems.at[k], recv_sem=recv_sems.at[k], device_id=to, device_id_type=MESH)

        mine = pltpu.make_async_copy(x_ref, blk(*me), local_sem)
        mine.start()
        first = [copy(0, me, sibling, src=x_ref)]
        first += [copy(1 + j, me, (*chip, c), src=x_ref) for j, chip in enumerate(chips)]
        for cp in first:
            cp.start()
        passed = [copy(4 + j, (*chip, c), sibling) for j, chip in enumerate(chips)]
        for j, chip in enumerate(chips):
            copy(1 + j, (*chip, c), me).wait_recv()
            passed[j].start()
        copy(0, sibling, me).wait_recv()
        for j, chip in enumerate(chips):
            copy(4 + j, (*chip, 1 - c), me).wait_recv()
        for cp in first + passed:
            cp.wait_send()
        mine.wait()

    return pl.pallas_call(
        body, out_shape=jax.ShapeDtypeStruct((N_DEV, rows, cols), slab.dtype),
        in_specs=[pl.BlockSpec(memory_space=pltpu.VMEM)] + [_ANY] * len(after),
        out_specs=pl.BlockSpec(memory_space=pltpu.VMEM),
        scratch_shapes=[pltpu.SemaphoreType.DMA((7,)), pltpu.SemaphoreType.DMA((7,)), pltpu.SemaphoreType.DMA],
        name=name, compiler_params=pltpu.CompilerParams(vmem_limit_bytes=VMEM_MID),
    )(slab, *after)


def _hbm(a):
    return pltpu.with_memory_space_constraint(a, pltpu.HBM)


def _half_rows(ref, q, c):
    half = ref.shape[1] // 2
    return ref.at[q, pl.ds(c * half, half)]


def _ici_copy(src, dst, send_sems, recv_sems, k, to):
    return pltpu.make_async_remote_copy(src_ref=src, dst_ref=dst, send_sem=send_sems.at[k], recv_sem=recv_sems.at[k],
                                        device_id=to, device_id_type=MESH)


def _gather_start(bufs, sizes, name):
    n = len(bufs)
    ng = len(sizes)

    def body(*refs):
        ins = refs[:n]
        sems = refs[n:n + 2 * ng]
        token = refs[-1]
        x, y, c, chips = _place()
        a = 0
        for g, sz in enumerate(sizes):
            for i in range(sz):
                part = _half_rows(ins[a], 2 * x + y, c)
                for j, chip in enumerate(chips):
                    _ici_copy(part, part, sems[2 * g], sems[2 * g + 1], 3 * i + j, (*chip, c)).start()
                a += 1
        token[...] = jnp.zeros_like(token)

    sem_shapes = []
    for sz in sizes:
        sem_shapes += [pltpu.SemaphoreType.DMA((3 * sz,)), pltpu.SemaphoreType.DMA((3 * sz,))]
    outs = pl.pallas_call(
        body, name=name,
        out_shape=tuple(sem_shapes) + tuple(pltpu.HBM(b.shape, b.dtype) for b in bufs)
                  + (jax.ShapeDtypeStruct((SUBLANE, LANE), F32),),
        in_specs=[_HBM] * n,
        out_specs=tuple([_SEM] * (2 * ng) + [_HBM] * n + [pl.BlockSpec(memory_space=pltpu.VMEM)]),
        input_output_aliases={i: 2 * ng + i for i in range(n)},
        compiler_params=pltpu.CompilerParams(has_side_effects=_EFFECT),
    )(*[_hbm(b) for b in bufs])
    sems = [(outs[2 * g], outs[2 * g + 1]) for g in range(ng)]
    return sems, list(outs[2 * ng:2 * ng + n]), outs[-1]


def _gather_wait(bufs, sems, after, name):
    n = len(bufs)
    na = len(after)

    def body(*refs):
        ins = refs[:n]
        send_sems, recv_sems = refs[n], refs[n + 1]
        x, y, c, chips = _place()
        for i in range(n):
            sent = _half_rows(ins[i], 2 * x + y, c)
            for j, (cx, cy) in enumerate(chips):
                cp = _ici_copy(sent, _half_rows(ins[i], 2 * cx + cy, c), send_sems, recv_sems, 3 * i + j, (cx, cy, c))
                cp.wait_send()
                cp.wait_recv()

    outs = pl.pallas_call(
        body, name=name,
        out_shape=tuple(pltpu.HBM(b.shape, b.dtype) for b in bufs),
        in_specs=[_HBM] * n + [_SEM, _SEM] + [_ANY] * na,
        out_specs=tuple([_HBM] * n),
        input_output_aliases={i: i for i in range(n)},
        compiler_params=pltpu.CompilerParams(has_side_effects=_EFFECT),
    )(*bufs, sems[0], sems[1], *after)
    return list(outs)


def _exchange_halves(bufs, name):
    n = len(bufs)

    def body(*refs):
        ins = refs[:n]
        send_sems, recv_sems = refs[-2:]
        x, y, c, chips = _place()
        sends = []
        for i in range(n):
            for j, (cx, cy) in enumerate(chips):
                part = _half_rows(ins[i], 2 * cx + cy, c)
                cp = _ici_copy(part, part, send_sems, recv_sems, 3 * i + j, (x, y, 1 - c))
                cp.start()
                sends.append(cp)
        for i in range(n):
            for j, (cx, cy) in enumerate(chips):
                part = _half_rows(ins[i], 2 * cx + cy, 1 - c)
                _ici_copy(part, part, send_sems, recv_sems, 3 * i + j, (x, y, 1 - c)).wait_recv()
        for cp in sends:
            cp.wait_send()

    outs = pl.pallas_call(
        body, name=name,
        out_shape=[jax.ShapeDtypeStruct(b.shape, b.dtype) for b in bufs],
        in_specs=[_ANY] * n, out_specs=[_ANY] * n,
        input_output_aliases={i: i for i in range(n)},
        scratch_shapes=[pltpu.SemaphoreType.DMA((3 * n,)), pltpu.SemaphoreType.DMA((3 * n,))],
    )(*bufs)
    return list(outs)


def _other_half(ref, c):
    half = ref.shape[1] // 2
    return ref.at[:, pl.ds((1 - c) * half, half)]


def _prereduce_plan(src, land, x, y, c, chips):
    return [(_other_half(src, c), land, (x, y, 1 - c))]


def _prereduce_land(d):
    return (d.shape[0], d.shape[1] // 2, d.shape[2])


def _scatter_plan(src, land, x, y, c, chips):
    return [(src.at[2 * cx + cy], land.at[j], (cx, cy, c)) for j, (cx, cy) in enumerate(chips)]


def _scatter_land(p):
    return (3,) + p.shape[1:]


_COPIES_PER_SOURCE = {_prereduce_plan: 1, _scatter_plan: 3}


def _split_start(srcs, plan, land_shape, name):
    n = len(srcs)
    land_shapes = [land_shape(s) for s in srcs]
    per = _COPIES_PER_SOURCE[plan]

    def body(*refs):
        src_refs, lands = refs[:n], refs[n:2 * n]
        send_sems, recv_sems = refs[2 * n], refs[2 * n + 1]
        token = refs[-1]
        x, y, c, chips = _place()
        for i in range(n):
            for j, (s, d, to) in enumerate(plan(src_refs[i], lands[i], x, y, c, chips)):
                _ici_copy(s, d, send_sems, recv_sems, per * i + j, to).start()
        token[...] = jnp.zeros_like(token)

    outs = pl.pallas_call(
        body, name=name,
        out_shape=(pltpu.SemaphoreType.DMA((per * n,)), pltpu.SemaphoreType.DMA((per * n,)))
                  + tuple(pltpu.HBM(p.shape, p.dtype) for p in srcs)
                  + tuple(pltpu.HBM(s, BF16) for s in land_shapes)
                  + (jax.ShapeDtypeStruct((SUBLANE, LANE), F32),),
        in_specs=[_HBM] * (2 * n),
        out_specs=tuple([_SEM, _SEM] + [_HBM] * (2 * n) + [pl.BlockSpec(memory_space=pltpu.VMEM)]),
        input_output_aliases={i: 2 + i for i in range(2 * n)},
        compiler_params=pltpu.CompilerParams(has_side_effects=_EFFECT),
    )(*[_hbm(p) for p in srcs], *[_hbm(lax.empty(s, BF16)) for s in land_shapes])
    return (outs[0], outs[1]), list(outs[2:2 + n]), list(outs[2 + n:2 + 2 * n]), outs[-1]


def _split_wait(psums, lands, sems, plan, after, name):
    n = len(psums)
    na = len(after)
    per = _COPIES_PER_SOURCE[plan]

    def body(*refs):
        srcs, lnds = refs[:n], refs[n:2 * n]
        send_sems, recv_sems = refs[2 * n], refs[2 * n + 1]
        x, y, c, chips = _place()
        for i in range(n):
            for j, (s, d, to) in enumerate(plan(srcs[i], lnds[i], x, y, c, chips)):
                cp = _ici_copy(s, d, send_sems, recv_sems, per * i + j, to)
                cp.wait_send()
                cp.wait_recv()

    outs = pl.pallas_call(
        body, name=name,
        out_shape=tuple(pltpu.HBM(a.shape, a.dtype) for a in list(psums) + list(lands)),
        in_specs=[_HBM] * (2 * n) + [_SEM, _SEM] + [_ANY] * na,
        out_specs=tuple([_HBM] * (2 * n)),
        input_output_aliases={i: i for i in range(2 * n)},
        compiler_params=pltpu.CompilerParams(has_side_effects=_EFFECT),
    )(*psums, *lands, sems[0], sems[1], *after)
    return list(outs[:n]), list(outs[n:])


def _exchange_grads(grads, l, name):
    n = len(grads)

    def body(*refs):
        ins = refs[:n]
        send_sems, recv_sems = refs[-2:]
        x, y, c, _ = _place()

        def rows(ref, core):
            half = ref.shape[1] // 2
            return ref.at[l, pl.ds(core * half, half)]

        cps = []
        for i in range(n):
            cp = _ici_copy(rows(ins[i], c), rows(ins[i], c), send_sems, recv_sems, i, (x, y, 1 - c))
            cp.start()
            cps.append(cp)
        for i, cp in enumerate(cps):
            cp.wait_send()
            _ici_copy(rows(ins[i], 1 - c), rows(ins[i], 1 - c), send_sems, recv_sems, i, (x, y, 1 - c)).wait_recv()

    outs = pl.pallas_call(
        body, name=name,
        out_shape=[jax.ShapeDtypeStruct(g.shape, g.dtype) for g in grads],
        in_specs=[_ANY] * n, out_specs=[_ANY] * n,
        input_output_aliases={i: i for i in range(n)},
        scratch_shapes=[pltpu.SemaphoreType.DMA((n,)), pltpu.SemaphoreType.DMA((n,))],
    )(*grads)
    return list(outs)


def _pack(arrs):
    flat = jnp.concatenate([a.reshape(-1).astype(F32) for a in arrs])
    pad = (-flat.shape[0]) % (SUBLANE * LANE)
    return jnp.pad(flat, (0, pad)).reshape(-1, LANE)


def _unpack(slab, shapes):
    flat = slab.reshape(-1)
    out, off = [], 0
    for s in shapes:
        n = math.prod(s)
        out.append(flat[off:off + n].reshape(s))
        off += n
    return out


def _split6(v):
    d = v.shape[-1] // 6
    return [v[:, k * d:(k + 1) * d].reshape(2, 1, d) for k in range(6)]


BIG = ("w_in", "w_br_attn", "w_br_pool", "w_br_sgu", "w_br_conv", "w_gate", "w_o", "w_ff_gate", "w_ff_up", "w_ff_down")
KIND = {"w_in": "col", "w_br_attn": "col", "w_br_pool": "col", "w_br_sgu": "col", "w_br_conv": "col",
        "w_gate": "col", "w_o": "row", "w_ff_gate": "col", "w_ff_up": "col", "w_ff_down": "row"}
SMALL = ("c_ctx", "b_ada", "q_norm_g", "k_norm_g", "pool_w", "pool_scale", "sgu_ln_g", "sgu_ln_b", "sgu_w",
         "sgu_b", "conv_w", "b_gate", "ln1_g", "ln1_b", "ln2_g", "ln2_b")
WEIGHTS = ("c_ctx", "w_ada", "b_ada", "w_in", "q_norm_g", "k_norm_g", "pool_w", "pool_scale", "sgu_ln_g", "sgu_ln_b",
           "sgu_w", "sgu_b", "conv_w", "w_br_attn", "w_br_pool", "w_br_sgu", "w_br_conv", "w_gate", "b_gate", "w_o",
           "ln1_g", "ln1_b", "w_ff_gate", "w_ff_up", "w_ff_down", "ln2_g", "ln2_b")


def _step(x, c, ctx, loss_target, W, M, V):
    L = W["w_ada"].shape[0]
    assert L == 2, "core c of a chip carries layer c of the weight traffic"
    N, D = x.shape[1], x.shape[2]
    NC = ctx.shape[1]
    T = NC + N
    FF = W["w_ff_down"].shape[1] * N_CHIP
    assert NC % ROW_TM == 0 and N % ROW_TM == 0 and N % GRID_W == 0 and D % LANE == 0
    ncb = NC // ROW_TM
    nblk = T // ROW_TM
    alpha = (2 * L) ** 0.25
    ax, ay, ac = lax.axis_index("x"), lax.axis_index("y"), lax.axis_index("c")
    chip = 2 * ax + ay
    dev = 2 * chip + ac
    chip_arr = jnp.reshape(chip, (1,)).astype(jnp.int32)
    core_arr = jnp.reshape(ac, (1,)).astype(jnp.int32)
    ns_ada = W["w_ada"].shape[2]
    chip_devs = (0, 2, 4, 6)

    conv_shape = W["conv_w"].shape
    g0 = _all_gather_small(_pack([c, W["conv_w"]]), "gather_c")
    c_all = g0[:, :D // LANE, :].reshape(N_DEV, D)
    conv_parts = [g0[d].reshape(-1)[D:D + math.prod(conv_shape)].reshape(conv_shape) for d in chip_devs]
    conv_full = jnp.concatenate(conv_parts, axis=-1)
    cvec = jnp.concatenate([c_all, W["c_ctx"][None], jnp.zeros((16 - N_DEV - 1, D), F32)], axis=0)
    b_cols = lax.dynamic_slice_in_dim(W["b_ada"], chip * ns_ada, ns_ada, axis=1).reshape(L, 1, ns_ada)
    mod_part = _ada_fwd(cvec, W["w_ada"], b_cols, "ada_fwd")
    g1 = _all_gather_small(mod_part.reshape(-1, LANE), "gather_mod")
    mod_all = jnp.concatenate([g1[d].reshape(L, 16, ns_ada) for d in chip_devs], axis=-1)
    mod_lat = lax.dynamic_index_in_dim(mod_all, dev, axis=1, keepdims=True)
    mods = jnp.concatenate([mod_all[:, N_DEV:N_DEV + 1], mod_lat], axis=1)

    ffn_keys = ("w_ff_gate", "w_ff_up", "w_ff_down")
    mid_keys = ("w_br_attn", "w_br_pool", "w_br_sgu", "w_br_conv", "w_o")
    groups = [grp for l in range(L) for grp in ([("w_in", l)], [("w_gate", l)], [(k, l) for k in mid_keys],
                                                [(k, l) for k in ffn_keys])]
    flying, gsems, gtokens = [None] * len(groups), [None] * len(groups), []
    for n_start, gs in enumerate(((0, 1), (2, 3), (4, 5, 6, 7))):
        members = [kl for g in gs for kl in groups[g]]
        casts = [_cast_into(chip_arr, W[k], l, "cast_%s_l%d" % (k, l), after=gtokens[-1:] or [g1])
                 for k, l in members]
        sems, bufs, token = _gather_start(casts, [len(groups[g]) for g in gs], "gather_start_%d" % n_start)
        gtokens.append(token)
        lo = 0
        for g, sem in zip(gs, sems):
            flying[g], gsems[g] = bufs[lo:lo + len(groups[g])], sem
            lo += len(groups[g])
    full = {k: [None] * L for k in BIG}

    def weights_ready(g, after):
        got = _gather_wait(flying[g], gsems[g], after, "gather_wait_%d" % g)
        for (k, l), buf in zip(groups[g], _exchange_halves(got, "exchange_halves_%d" % g)):
            full[k][l] = buf

    cos, sin = _rope_tables(N, NC)
    x0 = jnp.concatenate([ctx[0], x[0]], axis=0)

    saved = []
    xin = x0
    h1 = None
    dy_top = loss_parts = None
    for l in range(L):
        sh1, sc1, g1m, sh2, sc2, g2m = _split6(mods[l])
        tag = "_l%d" % l
        if l == 0:
            h1 = _modulate(xin, sc1, sh1, ncb, "modulate" + tag)
        weights_ready(4 * l, [h1])
        p = _mm_nn(h1, full["w_in"], l, "col", name="mm_in" + tag)
        qg, kg = W["q_norm_g"][l][None], W["k_norm_g"][l][None]
        q, k, v = _qk_fwd(p, cos, sin, qg, kg, "qk_fwd" + tag)
        att = _attn_fwd(q, k, v, NC, "attn_fwd" + tag)
        ps = W["pool_scale"][l][None]
        ypool = _pool_fwd(p, W["pool_w"][l], ps, NC, "pool_fwd" + tag)
        lg, lb = W["sgu_ln_g"][l][None], W["sgu_ln_b"][l][None]
        b_st = W["sgu_b"][l].T
        ysgu = _sgu_fwd(p, lg, lb, W["sgu_w"][l], b_st, "sgu_fwd" + tag)
        yconv = _conv_fwd(p, conv_full[l], NC, "conv_fwd" + tag)
        brs = (att, ypool, ysgu, yconv)
        weights_ready(4 * l + 1, list(brs))
        gates = _mm_nn(h1, full["w_gate"], l, "col", name="mm_gate" + tag, bias=W["b_gate"][l][None],
                       act=jax.nn.sigmoid, out_dtype=BF16)
        weights_ready(4 * l + 2, [gates])
        ts = [_mm_nn(b, full[wk], l, "col", name="mm_" + wk + tag, out_dtype=BF16)
              for b, wk in zip(brs, ("w_br_attn", "w_br_pool", "w_br_sgu", "w_br_conv"))]
        mg = _merge_fwd(gates, ts, "merge_fwd" + tag)
        m = _mm_nn(mg, full["w_o"], l, "row", name="mm_o" + tag, tn_pref=1024)
        ln1g, ln1b = W["ln1_g"][l][None], W["ln1_b"][l][None]
        x1, h2 = _resid_ln(xin, m, g1m, ln1g, ln1b, sc2, sh2, ncb, alpha, "resid_ln1" + tag)
        weights_ready(4 * l + 3, [h2])
        gg, uu, act = _ffn_up(h2, full["w_ff_gate"][l], full["w_ff_up"][l], "ffn_up" + tag)
        ff = _mm_nn(act, full["w_ff_down"], l, "row", name="mm_ffd" + tag, tm_pref=384, tn_pref=512)
        ln2g, ln2b = W["ln2_g"][l][None], W["ln2_b"][l][None]
        saved.append(dict(xin=xin, h1=h1, p=p, q=q, k=k, v=v, brs=brs, gates=gates, ts=ts, mg=mg, m=m, x1=x1, h2=h2,
                          gg=gg, uu=uu, act=act, ff=ff))
        if l + 1 < L:
            nsh1, nsc1 = _split6(mods[l + 1])[:2]
            xin, h1 = _resid_ln(x1, ff, g2m, ln2g, ln2b, nsc1, nsh1, ncb, alpha, "resid_ln2" + tag)
        else:
            dy_top, loss_parts = _resid_ln_loss(x1, ff, g2m, ln2g, ln2b, loss_target[0], ncb, alpha,
                                                "resid_ln2_loss" + tag)
    loss_dev = jnp.sum(loss_parts[:, 0, 0])

    def lat_ctx(part_rows):
        return jnp.stack([jnp.sum(part_rows[:ncb], axis=0), jnp.sum(part_rows[ncb:], axis=0)])

    dW = {k: [None] * L for k in BIG}
    small = {k: [None] * L for k in ("q_norm_g", "k_norm_g", "pool_w", "pool_scale", "sgu_ln_g", "sgu_ln_b", "sgu_w",
                                     "sgu_b", "conv_w", "b_gate", "ln1_g", "ln1_b", "ln2_g", "ln2_b")}
    dmods = [None] * L
    mix_keys = tuple(k for k in BIG if k not in ffn_keys)
    grads_big = {k: None for k in BIG}
    adam_out = {k: None for k in BIG}

    def reduce_begin(keys, l, tg):
        sems, src, land, token = _split_start([dW[k][l] for k in keys], _prereduce_plan, _prereduce_land,
                                              "prereduce_start_" + tg)
        return dict(keys=keys, l=l, sems=sems, src=src, land=land, token=token, tg=tg)

    def reduce_scatter(st, after):
        dws, recv = _split_wait(st["src"], st["land"], st["sems"], _prereduce_plan, after,
                                "prereduce_wait_" + st["tg"])
        psums = [_pair_sum(core_arr, d, r, "pair_sum_%s_l%d" % (k, st["l"]))
                 for k, d, r in zip(st["keys"], dws, recv)]
        st["sems"], st["src"], st["land"], st["token"] = _split_start(psums, _scatter_plan, _scatter_land,
                                                                      "scatter_start_" + st["tg"])

    def reduce_finish(st, after):
        src, land = _split_wait(st["src"], st["land"], st["sems"], _scatter_plan, after, "scatter_wait_" + st["tg"])
        for k, p_, r in zip(st["keys"], src, land):
            grads_big[k] = _chip_sum(chip_arr, core_arr, p_, r, st["l"], L, grads_big[k],
                                     "chip_sum_%s_l%d" % (k, st["l"]))
        done = _exchange_grads([grads_big[k] for k in st["keys"]], st["l"], "exchange_grads_" + st["tg"])
        grads_big.update(zip(st["keys"], done))

    def adam(keys, l, after):
        for k in keys:
            adam_out[k] = _adamw_layer(W[k], grads_big[k], M[k], V[k], l, adam_out[k], after,
                                       "adamw_%s_l%d" % (k, l))

    deferred = []

    def grad_of(key, l, thunk):
        if l == 0:
            deferred.append((key, thunk))
        else:
            dW[key][l] = thunk(())

    dxa, dhs, sc_prev = dy_top, [], None
    red_l1 = red_ffn = None
    for l in reversed(range(L)):
        s = saved[l]
        sh1, sc1, g1m, sh2, sc2, g2m = _split6(mods[l])
        tag = "_l%d" % l
        ln1g, ln1b = W["ln1_g"][l][None], W["ln1_b"][l][None]
        ln2g, ln2b = W["ln2_g"][l][None], W["ln2_b"][l][None]
        dx1a, dff, part2 = _ln_bwd(dxa, dhs, sc_prev if dhs else sc1, s["x1"], s["ff"], g2m, ln2g, ln2b, ncb, alpha,
                                   "ln2_bwd" + tag, after=[] if red_l1 is None else [red_l1["token"]])
        small["ln2_g"][l] = jnp.sum(part2[:, 0], axis=0)
        small["ln2_b"][l] = jnp.sum(part2[:, 1], axis=0)
        dg2 = lat_ctx(part2[:, 2])
        if dhs:
            dmods[l + 1][1], dmods[l + 1][0] = lat_ctx(part2[:, 3]), lat_ctx(part2[:, 4])
        dgg, duu = _ffn_down_bwd(dff, full["w_ff_down"][l], s["gg"], s["uu"], "ffn_down_bwd" + tag)
        if l == 0:
            reduce_scatter(red_l1, [dgg])
        dW["w_ff_down"][l] = _mm_tn(s["act"], dff, "row", nq=N_CHIP, kdim=FF // N_CHIP, ndim=D, name="tn_ffd" + tag,
                                    tk_pref=1408, tn_pref=512, after=[red_l1["token"]] if l == 0 else [])
        dh2a = _mm_nt(dgg, full["w_ff_gate"], l, "col", name="nt_ffg" + tag)
        dh2b = _mm_nt(duu, full["w_ff_up"], l, "col", name="nt_ffu" + tag)
        dW["w_ff_gate"][l] = _mm_tn(s["h2"], dgg, "col", nq=N_CHIP, kdim=D, ndim=FF // N_CHIP, name="tn_ffg" + tag,
                                    tn_pref=1408)
        dW["w_ff_up"][l] = _mm_tn(s["h2"], duu, "col", nq=N_CHIP, kdim=D, ndim=FF // N_CHIP, name="tn_ffu" + tag,
                                  tn_pref=1408)
        ties = []
        if l == 0:
            red_ffn = reduce_begin(ffn_keys, 0, "l0_ffn")
            ties = [red_l1["token"], red_ffn["token"]]
        dx0a, dm, part1 = _ln_bwd(dx1a, [dh2a, dh2b], sc2, s["xin"], s["m"], g1m, ln1g, ln1b, ncb, alpha,
                                  "ln1_bwd" + tag, after=ties)
        small["ln1_g"][l] = jnp.sum(part1[:, 0], axis=0)
        small["ln1_b"][l] = jnp.sum(part1[:, 1], axis=0)
        dg1 = lat_ctx(part1[:, 2])
        dsc2, dsh2 = lat_ctx(part1[:, 3]), lat_ctx(part1[:, 4])
        dmods[l] = [None, None, dg1, dsh2, dsc2, dg2]
        dmg = _mm_nt(dm, full["w_o"], l, "row", name="nt_o" + tag)
        grad_of("w_o", l, lambda after, a=s["mg"], d=dm, tag=tag: _mm_tn(
            a, d, "row", nq=N_CHIP, kdim=D // N_CHIP, ndim=D, name="tn_o" + tag, tn_pref=1024, after=after))
        dpre, dt0, dt1, dt2, dt3, bpart = _merge_bwd(dmg, s["gates"], s["ts"], "merge_bwd" + tag)
        small["b_gate"][l] = jnp.sum(bpart[:, 0], axis=0)
        dh1a = _mm_nt(dpre, full["w_gate"], l, "col", name="nt_gate" + tag)
        grad_of("w_gate", l, lambda after, a=s["h1"], d=dpre, tag=tag: _mm_tn(
            a, d, "col", nq=N_CHIP, kdim=D, ndim=D, name="tn_gate" + tag, tn_pref=1024, after=after))
        dbrs = []
        for b, dt, wk, odt in zip(s["brs"], (dt0, dt1, dt2, dt3), ("w_br_attn", "w_br_pool", "w_br_sgu", "w_br_conv"),
                                  (BF16, F32, F32, F32)):
            dbrs.append(_mm_nt(dt, full[wk], l, "col", name="nt_" + wk + tag, out_dtype=odt))
            grad_of(wk, l, lambda after, a=b, d=dt, wk=wk, tag=tag: _mm_tn(
                a, d, "col", nq=N_CHIP, kdim=a.shape[1], ndim=D // N_CHIP, name="tn_" + wk + tag, after=after))
        qg, kg = W["q_norm_g"][l][None], W["k_norm_g"][l][None]
        dq, dk, dv = _attn_bwd(s["q"], s["k"], s["v"], dbrs[0], NC, "attn_bwd" + tag)
        dp, qkpart = _qk_bwd(s["p"], dq, dk, dv, cos, sin, qg, kg, "qk_bwd" + tag)
        small["q_norm_g"][l] = jnp.sum(qkpart[:, 0], axis=0)
        small["k_norm_g"][l] = jnp.sum(qkpart[:, 1], axis=0)
        ps = W["pool_scale"][l][None]
        dp, small["pool_w"][l], dps = _pool_bwd(s["p"], dbrs[1], W["pool_w"][l], ps, dp, NC, "pool_bwd" + tag)
        small["pool_scale"][l] = dps[0]
        lg, lb = W["sgu_ln_g"][l][None], W["sgu_ln_b"][l][None]
        dp, small["sgu_w"][l], dsacc, dln = _sgu_bwd(s["p"], dbrs[2], lg, lb, W["sgu_w"][l], W["sgu_b"][l].T, dp,
                                                     "sgu_bwd" + tag)
        small["sgu_b"][l] = jnp.sum(dsacc.reshape(GC, N_GROUP, GC), axis=-1).T
        small["sgu_ln_g"][l], small["sgu_ln_b"][l] = dln[0], dln[1]
        dp, small["conv_w"][l] = _conv_bwd(s["p"], dbrs[3], conv_full[l], dp, NC, "conv_bwd" + tag)
        dh1b = _mm_nt(dp, full["w_in"], l, "col", name="nt_in" + tag)
        grad_of("w_in", l, lambda after, a=s["h1"], d=dp, tag=tag: _mm_tn(
            a, d, "col", nq=N_CHIP, kdim=D, ndim=IN_W // N_CHIP, name="tn_in" + tag, tn_pref=1152, after=after))
        dxa, dhs, sc_prev = dx0a, [dh1a, dh1b], sc1
        if l == 1:
            red_l1 = reduce_begin(BIG, 1, "l1")
    dx0, part0 = _mod_bwd(dxa, dhs, sc_prev, x0, ncb, "mod_bwd")
    dmods[0][1], dmods[0][0] = lat_ctx(part0[:, 0]), lat_ctx(part0[:, 1])
    grad_x = dx0[NC:][None]
    dmod = jnp.stack([jnp.concatenate(dmods[l], axis=-1) for l in range(L)])

    small_names = [k for k in SMALL if k not in ("c_ctx", "b_ada")]
    small_arrs = [jnp.stack(small[k]) for k in small_names]
    small_shapes = [a.shape for a in small_arrs]
    slab = _pack([loss_dev.reshape(1), jnp.zeros((LANE - 1,), F32), dmod] + small_arrs)
    g2 = _all_gather_small(slab, "gather_small")
    total = _slab_sum(g2, "slab_sum")
    flat_total = total.reshape(-1)
    loss = flat_total[0]
    nmod = L * 2 * 6 * D
    dmod_sum = flat_total[LANE:LANE + nmod].reshape(L, 2, 6 * D)
    small_grads = dict(zip(small_names, _unpack(flat_total[LANE + nmod:], small_shapes)))
    dmod_lat = g2.reshape(N_DEV, -1)[:, LANE:LANE + nmod].reshape(N_DEV, L, 2, 6 * D)[:, :, 1]
    dm16 = jnp.concatenate([jnp.transpose(dmod_lat, (1, 0, 2)), dmod_sum[:, 0:1],
                            jnp.zeros((L, 16 - N_DEV - 1, 6 * D), F32)], axis=1)
    small_grads["b_ada"] = dmod_sum[:, 0] + dmod_sum[:, 1]
    dm16_cols = lax.dynamic_slice_in_dim(dm16, chip * ns_ada, ns_ada, axis=2)
    grad_w_ada, ds16 = _ada_bwd(cvec, dm16_cols, W["w_ada"], "ada_bwd")
    conv_grad_full = small_grads["conv_w"]
    small_grads["conv_w"] = lax.dynamic_slice_in_dim(conv_grad_full, chip * GC, GC, axis=2)

    reduce_scatter(red_ffn, [g2])
    for key, thunk in deferred:
        dW[key][0] = thunk([red_ffn["token"]])
    red_mix = reduce_begin(mix_keys, 0, "l0_mix")
    reduce_finish(red_l1, [red_ffn["token"], red_mix["token"]])
    reduce_scatter(red_mix, [grads_big[k] for k in BIG])
    adam(BIG, 1, [red_mix["token"]])
    reduce_finish(red_ffn, [red_mix["token"]])
    adam(ffn_keys, 0, [])
    delta, new_m, new_v = {}, {}, {}
    delta["w_ada"], new_m["w_ada"], new_v["w_ada"] = _adamw(W["w_ada"], grad_w_ada, M["w_ada"], V["w_ada"],
                                                            "adamw_w_ada")
    reduce_finish(red_mix, [delta["w_ada"]] + [adam_out[k][0] for k in BIG])
    adam(mix_keys, 0, [])
    for k in BIG:
        delta[k], new_m[k], new_v[k] = adam_out[k]

    g3 = _all_gather_small(ds16[N_DEV].reshape(-1, LANE), "gather_dsilu", after=[grads_big[k] for k in mix_keys])
    small_grads["c_ctx"] = _cctx_grad(g3, W["c_ctx"].reshape(-1, LANE), "cctx_grad").reshape(D)
    shapes = [W[k].shape for k in SMALL]
    sd, sm, sv = _adamw(_pack([W[k] for k in SMALL]), _pack([small_grads[k] for k in SMALL]),
                        _pack([M[k] for k in SMALL]), _pack([V[k] for k in SMALL]), "adamw_small")
    for k, d_, m_, v_ in zip(SMALL, _unpack(sd, shapes), _unpack(sm, shapes), _unpack(sv, shapes)):
        delta[k], new_m[k], new_v[k] = d_, m_, v_
    grads = dict(grads_big)
    grads["w_ada"] = grad_w_ada
    grads.update(small_grads)
    return (loss, grad_x, *[grads[k] for k in WEIGHTS], *[delta[k] for k in WEIGHTS],
            *[new_m[k] for k in WEIGHTS], *[new_v[k] for k in WEIGHTS])


def kernel(x, c, ctx, c_ctx, w_ada, b_ada, w_in, q_norm_g, k_norm_g, pool_w, pool_scale, sgu_ln_g, sgu_ln_b, sgu_w, sgu_b, conv_w, w_br_attn, w_br_pool, w_br_sgu, w_br_conv, w_gate, b_gate, w_o, ln1_g, ln1_b, w_ff_gate, w_ff_up, w_ff_down, ln2_g, ln2_b, loss_target, m_c_ctx, m_w_ada, m_b_ada, m_w_in, m_q_norm_g, m_k_norm_g, m_pool_w, m_pool_scale, m_sgu_ln_g, m_sgu_ln_b, m_sgu_w, m_sgu_b, m_conv_w, m_w_br_attn, m_w_br_pool, m_w_br_sgu, m_w_br_conv, m_w_gate, m_b_gate, m_w_o, m_ln1_g, m_ln1_b, m_w_ff_gate, m_w_ff_up, m_w_ff_down, m_ln2_g, m_ln2_b, v_c_ctx, v_w_ada, v_b_ada, v_w_in, v_q_norm_g, v_k_norm_g, v_pool_w, v_pool_scale, v_sgu_ln_g, v_sgu_ln_b, v_sgu_w, v_sgu_b, v_conv_w, v_w_br_attn, v_w_br_pool, v_w_br_sgu, v_w_br_conv, v_w_gate, v_b_gate, v_w_o, v_ln1_g, v_ln1_b, v_w_ff_gate, v_w_ff_up, v_w_ff_down, v_ln2_g, v_ln2_b):
    args = locals()
    W = {k: args[k] for k in WEIGHTS}
    M = {k: args["m_" + k] for k in WEIGHTS}
    V = {k: args["v_" + k] for k in WEIGHTS}
    return _step(x, c, ctx, loss_target, W, M, V)
```

```python
import functools
import math

import jax
import jax.numpy as jnp
from jax import lax
from jax.experimental import pallas as pl
from jax.experimental.pallas import tpu as pltpu

F32 = jnp.float32
BF16 = jnp.bfloat16
MESH = pl.DeviceIdType.MESH

N_DEV = 8
N_CHIP = 4
GRID_W = 64
HEAD_DIM = 128
N_HEADS = 8
N_KV = 2
KV_GROUP = N_HEADS // N_KV
Q_W = N_HEADS * HEAD_DIM
KV_W = N_KV * HEAD_DIM
QKV_W = Q_W + 2 * KV_W
GC = 128
N_GROUP = 4
BR_W = N_GROUP * GC
POOL_WINDOWS = (2, 4, 8, 16)
OFF_POOL = QKV_W
OFF_U = OFF_POOL + BR_W
OFF_VG = OFF_U + BR_W
OFF_CB = OFF_VG + BR_W
OFF_CC = OFF_CB + BR_W
OFF_CX = OFF_CC + BR_W
IN_W = OFF_CX + BR_W
Q_BLOCK = 128
ROPE_THETA = 10000.0
LN_EPS = 1e-5
RMS_EPS = 1e-6
ADAM_LR = 0.001
ADAM_B1 = 0.9
ADAM_B2 = 0.999
ADAM_EPS = 1e-08
ADAM_WD = 0.01
ADAM_STEP = 10

ROW_TM = 256
LANE = 128
SUBLANE = 8
VMEM_BIG = 56 * 1024 * 1024
VMEM_MID = 40 * 1024 * 1024
NT_WHOLE_BYTES = 16 * 1024 * 1024


def _cp(sem=None, vmem=VMEM_MID):
    return pltpu.CompilerParams(dimension_semantics=sem, vmem_limit_bytes=vmem)


_ANY = pl.BlockSpec(memory_space=pl.ANY)
_HBM = pl.BlockSpec(memory_space=pltpu.HBM)
_SEM = pl.BlockSpec(memory_space=pltpu.SEMAPHORE)
_EFFECT = pltpu.SideEffectType.DATAFLOW_SIDE_EFFECTING


def _tile(n, pref, unit=LANE, whole_ok=False):
    if n <= pref:
        return n
    best = None
    for t in range(unit, pref + 1, unit):
        if n % t == 0:
            best = t
    if whole_ok and (best is None or best * 4 < pref):
        return n
    assert best is not None, (n, pref, unit)
    return best


def _layer_of(w, l):
    if isinstance(w, (list, tuple)):
        return w[l][:, None], 0
    return w, l


def _mm_nn(a, w, l, kind, *, name, out_dtype=F32, bias=None, act=None, tm_pref=768, tn_pref=1152):
    w, l = _layer_of(w, l)
    T, K = a.shape
    tm = _tile(T, tm_pref, SUBLANE * 2)
    if kind == "col":
        nq, _, kw, ns = w.shape
        assert kw == K
        tn = _tile(ns, tn_pref, whole_ok=True)
        nj = ns // tn
        n_total = nq * ns
        w_spec = pl.BlockSpec((None, None, K, tn), lambda j, i: (j // nj, l, 0, j % nj))
        grid_n = nq * nj
    else:
        nq, _, kc, n_total = w.shape
        assert nq * kc == K
        tn = _tile(n_total, tn_pref)
        w_spec = pl.BlockSpec((nq, None, kc, tn), lambda j, i: (0, l, 0, j))
        grid_n = n_total // tn

    def body(*refs):
        if bias is not None:
            a_ref, w_ref, b_ref, o_ref = refs
        else:
            a_ref, w_ref, o_ref = refs
        wv = w_ref[...]
        if kind == "row":
            wv = wv.reshape(K, tn)
        acc = jnp.dot(a_ref[...], wv, preferred_element_type=F32)
        if bias is not None:
            acc = acc + b_ref[...]
        if act is not None:
            acc = act(acc)
        o_ref[...] = acc.astype(out_dtype)

    in_specs = [pl.BlockSpec((tm, K), lambda j, i: (i, 0)), w_spec]
    args = [a, w]
    if bias is not None:
        in_specs.append(pl.BlockSpec((1, tn), lambda j, i: (0, j)))
        args.append(bias)
    return pl.pallas_call(
        body, grid=(grid_n, T // tm), in_specs=in_specs,
        out_specs=pl.BlockSpec((tm, tn), lambda j, i: (i, j)),
        out_shape=jax.ShapeDtypeStruct((T, n_total), out_dtype),
        name=name, compiler_params=_cp(("parallel", "parallel"), VMEM_BIG),
    )(*args)


def _mm_nt(dc, w, l, kind, *, name, out_dtype=F32, dc_off=0, tm_pref=768, tk_pref=1024):
    w, l = _layer_of(w, l)
    T = dc.shape[0]
    tm = _tile(T, tm_pref, SUBLANE * 2)
    if kind == "col":
        nq, _, K, ns = w.shape
        assert dc_off % ns == 0
        offb = dc_off // ns
        tks = _tile(K, 512)
        whole_bytes = 2 * (tm * nq * ns + nq * tks * ns)
        if dc_off == 0 and dc.shape[1] == nq * ns and whole_bytes <= NT_WHOLE_BYTES:
            def body_whole(dc_ref, w_ref, o_ref):
                acc = None
                for q in range(nq):
                    part = lax.dot_general(dc_ref[:, q * ns:(q + 1) * ns], w_ref[q], (((1,), (1,)), ((), ())),
                                           preferred_element_type=F32)
                    acc = part if acc is None else acc + part
                o_ref[...] = acc.astype(out_dtype)

            return pl.pallas_call(
                body_whole, grid=(T // tm, K // tks),
                in_specs=[pl.BlockSpec((tm, nq * ns), lambda i, kk: (i, 0)),
                          pl.BlockSpec((nq, None, tks, ns), lambda i, kk: (0, l, kk, 0))],
                out_specs=pl.BlockSpec((tm, tks), lambda i, kk: (i, kk)),
                out_shape=jax.ShapeDtypeStruct((T, K), out_dtype),
                name=name, compiler_params=_cp(("parallel", "parallel"), VMEM_BIG),
            )(dc, w)
        tk = _tile(K, tk_pref)

        def body(dc_ref, w_ref, o_ref, acc_ref):
            q = pl.program_id(2)
            part = lax.dot_general(dc_ref[...], w_ref[...], (((1,), (1,)), ((), ())),
                                   preferred_element_type=F32)

            @pl.when(q == 0)
            def _():
                acc_ref[...] = part

            @pl.when(q > 0)
            def _():
                acc_ref[...] += part

            @pl.when(q == nq - 1)
            def _():
                o_ref[...] = acc_ref[...].astype(out_dtype)

        return pl.pallas_call(
            body, grid=(K // tk, T // tm, nq),
            in_specs=[pl.BlockSpec((tm, ns), lambda kk, i, q: (i, offb + q)),
                      pl.BlockSpec((None, None, tk, ns), lambda kk, i, q: (q, l, kk, 0))],
            out_specs=pl.BlockSpec((tm, tk), lambda kk, i, q: (i, kk)),
            out_shape=jax.ShapeDtypeStruct((T, K), out_dtype),
            scratch_shapes=[pltpu.VMEM((tm, tk), F32)],
            name=name, compiler_params=_cp(("parallel", "parallel", "arbitrary"), VMEM_BIG),
        )(dc, w)

    nq, _, kc, n = w.shape
    assert dc_off % n == 0
    offb = dc_off // n
    tk = _tile(kc, tk_pref, whole_ok=True)
    nkk = kc // tk

    def body(dc_ref, w_ref, o_ref):
        o_ref[...] = lax.dot_general(dc_ref[...], w_ref[...], (((1,), (1,)), ((), ())),
                                     preferred_element_type=F32).astype(out_dtype)

    return pl.pallas_call(
        body, grid=(nq * nkk, T // tm),
        in_specs=[pl.BlockSpec((tm, n), lambda j, i: (i, offb)),
                  pl.BlockSpec((None, None, tk, n), lambda j, i: (j // nkk, l, j % nkk, 0))],
        out_specs=pl.BlockSpec((tm, tk), lambda j, i: (i, j)),
        out_shape=jax.ShapeDtypeStruct((T, nq * kc), out_dtype),
        name=name, compiler_params=_cp(("parallel", "parallel"), VMEM_BIG),
    )(dc, w)


def _mm_tn(a, dc, kind, *, nq, kdim, ndim, name, a_off=0, dc_off=0, tk_pref=512, tn_pref=1152, out_dtype=BF16,
           after=()):
    T = a.shape[0]
    assert dc.shape[0] == T
    tk = _tile(kdim, tk_pref, whole_ok=True)
    tn = _tile(ndim, tn_pref, whole_ok=True)
    nkk, njn = kdim // tk, ndim // tn
    assert a_off % tk == 0 and dc_off % tn == 0
    aoffb, doffb = a_off // tk, dc_off // tn
    if kind == "col":
        a_map = lambda q, kk, jn: (0, aoffb + kk)
        d_map = lambda q, kk, jn: (0, doffb + q * njn + jn)
    else:
        a_map = lambda q, kk, jn: (0, aoffb + q * nkk + kk)
        d_map = lambda q, kk, jn: (0, doffb + jn)

    def body(a_ref, d_ref, *rest):
        rest[-1][...] = lax.dot_general(a_ref[...], d_ref[...], (((0,), (0,)), ((), ())),
                                        preferred_element_type=F32).astype(out_dtype)

    return pl.pallas_call(
        body, grid=(nq, nkk, njn),
        in_specs=[pl.BlockSpec((T, tk), a_map), pl.BlockSpec((T, tn), d_map)] + [_ANY] * len(after),
        out_specs=pl.BlockSpec((None, tk, tn), lambda q, kk, jn: (q, kk, jn)),
        out_shape=jax.ShapeDtypeStruct((nq, kdim, ndim), out_dtype),
        name=name, compiler_params=_cp(("parallel", "parallel", "parallel"), VMEM_BIG),
    )(a, dc, *after)


def _row_spec(d):
    return pl.BlockSpec((ROW_TM, d), lambda i: (i, 0))


def _mod_spec(d, ncb):
    return pl.BlockSpec((None, 1, d), lambda i: (jnp.where(i >= ncb, 1, 0), 0, 0))


def _vec_spec(d):
    return pl.BlockSpec((1, d), lambda i: (0, 0))


def _part_spec(d):
    return pl.BlockSpec((None, SUBLANE, d), lambda i: (i, 0, 0))


def _modulate(x, sc, sh, ncb, name):
    T, D = x.shape

    def body(x_ref, sc_ref, sh_ref, o_ref):
        o_ref[...] = (x_ref[...] * (1.0 + sc_ref[...]) + sh_ref[...]).astype(BF16)

    return pl.pallas_call(
        body, grid=(T // ROW_TM,),
        in_specs=[_row_spec(D), _mod_spec(D, ncb), _mod_spec(D, ncb)],
        out_specs=_row_spec(D), out_shape=jax.ShapeDtypeStruct((T, D), BF16),
        name=name, compiler_params=_cp(("parallel",)),
    )(x, sc, sh)


def _ln_stats(r):
    mu = jnp.mean(r, axis=-1, keepdims=True)
    rc = r - mu
    var = jnp.mean(rc * rc, axis=-1, keepdims=True)
    rstd = lax.rsqrt(var + LN_EPS)
    return rc * rstd, rstd


def _resid_ln(x, y, gate, g, b, sc, sh, ncb, alpha, name):
    T, D = x.shape

    def body(x_ref, y_ref, gate_ref, g_ref, b_ref, sc_ref, sh_ref, xo_ref, h_ref):
        xhat, _ = _ln_stats(alpha * x_ref[...] + gate_ref[...] * y_ref[...])
        xo = xhat * g_ref[...] + b_ref[...]
        xo_ref[...] = xo
        h_ref[...] = (xo * (1.0 + sc_ref[...]) + sh_ref[...]).astype(BF16)

    return pl.pallas_call(
        body, grid=(T // ROW_TM,),
        in_specs=[_row_spec(D), _row_spec(D), _mod_spec(D, ncb), _vec_spec(D), _vec_spec(D),
                  _mod_spec(D, ncb), _mod_spec(D, ncb)],
        out_specs=[_row_spec(D), _row_spec(D)],
        out_shape=[jax.ShapeDtypeStruct((T, D), F32), jax.ShapeDtypeStruct((T, D), BF16)],
        name=name, compiler_params=_cp(("parallel",)),
    )(x, y, gate, g, b, sc, sh)


def _resid_ln_loss(x, y, gate, g, b, target, ncb, alpha, name):
    T, D = x.shape
    nblk = T // ROW_TM

    def body(x_ref, y_ref, gate_ref, g_ref, b_ref, t_ref, dy_ref, loss_ref):
        i = pl.program_id(0)
        xhat, _ = _ln_stats(alpha * x_ref[...] + gate_ref[...] * y_ref[...])
        xo = xhat * g_ref[...] + b_ref[...]
        live = (i >= ncb).astype(F32)
        err = (xo - t_ref[...]) * live
        dy_ref[...] = err * (1.0 / D)
        loss_ref[...] = jnp.full((SUBLANE, LANE), 0.5 / D, F32) * jnp.sum(err * err)

    return pl.pallas_call(
        body, grid=(nblk,),
        in_specs=[_row_spec(D), _row_spec(D), _mod_spec(D, ncb), _vec_spec(D), _vec_spec(D),
                  pl.BlockSpec((ROW_TM, D), lambda i: (jnp.maximum(i - ncb, 0), 0))],
        out_specs=[_row_spec(D), _part_spec(LANE)],
        out_shape=[jax.ShapeDtypeStruct((T, D), F32), jax.ShapeDtypeStruct((nblk, SUBLANE, LANE), F32)],
        name=name, compiler_params=_cp(("parallel",)),
    )(x, y, gate, g, b, target)


def _write_parts(part_ref, rows, d):
    for k, r in enumerate(rows):
        part_ref[pl.ds(k, 1), :] = jnp.sum(r, axis=0, keepdims=True)
    if len(rows) < SUBLANE:
        part_ref[pl.ds(len(rows), SUBLANE - len(rows)), :] = jnp.zeros((SUBLANE - len(rows), d), F32)


def _ln_bwd(dxa, dhs, sc, x, y, gate, g, b, ncb, alpha, name, after=None):
    T, D = x.shape
    nblk = T // ROW_TM
    ndh = len(dhs)

    def body(*refs):
        dxa_ref = refs[0]
        dh_refs = refs[1:1 + ndh]
        sc_ref, x_ref, y_ref, gate_ref, g_ref, b_ref = refs[1 + ndh:7 + ndh]
        dx_ref, dy_ref, part_ref = refs[-3:]
        yv = y_ref[...]
        xhat, rstd = _ln_stats(alpha * x_ref[...] + gate_ref[...] * yv)
        dxo = dxa_ref[...]
        rows = []
        if ndh:
            dh = dh_refs[0][...]
            for r in dh_refs[1:]:
                dh = dh + r[...]
            dxo = dxo + dh * (1.0 + sc_ref[...])
            xo = xhat * g_ref[...] + b_ref[...]
            rows = [dh * xo, dh]
        dxhat = dxo * g_ref[...]
        m1 = jnp.mean(dxhat, axis=-1, keepdims=True)
        m2 = jnp.mean(dxhat * xhat, axis=-1, keepdims=True)
        dr = rstd * (dxhat - m1 - xhat * m2)
        dx_ref[...] = alpha * dr
        dy_ref[...] = (gate_ref[...] * dr).astype(BF16)
        _write_parts(part_ref, [dxo * xhat, dxo, dr * yv] + rows, D)

    in_specs = ([_row_spec(D)] * (1 + ndh)
                + [_mod_spec(D, ncb), _row_spec(D), _row_spec(D), _mod_spec(D, ncb), _vec_spec(D), _vec_spec(D)])
    extra = list(after or ())
    return pl.pallas_call(
        body, grid=(nblk,), in_specs=in_specs + [_ANY] * len(extra),
        out_specs=[_row_spec(D), _row_spec(D), _part_spec(D)],
        out_shape=[jax.ShapeDtypeStruct((T, D), F32), jax.ShapeDtypeStruct((T, D), BF16),
                   jax.ShapeDtypeStruct((nblk, SUBLANE, D), F32)],
        name=name, compiler_params=_cp(("parallel",)),
    )(dxa, *dhs, sc, x, y, gate, g, b, *extra)


def _mod_bwd(dxa, dhs, sc, x, ncb, name, after=()):
    T, D = x.shape
    nblk = T // ROW_TM
    ndh = len(dhs)

    def body(*refs):
        dxa_ref = refs[0]
        dh_refs = refs[1:1 + ndh]
        sc_ref, x_ref = refs[1 + ndh:3 + ndh]
        dx_ref, part_ref = refs[-2:]
        dh = dh_refs[0][...]
        for r in dh_refs[1:]:
            dh = dh + r[...]
        dx_ref[...] = dxa_ref[...] + dh * (1.0 + sc_ref[...])
        _write_parts(part_ref, [dh * x_ref[...], dh], D)

    return pl.pallas_call(
        body, grid=(nblk,),
        in_specs=[_row_spec(D)] * (1 + ndh) + [_mod_spec(D, ncb), _row_spec(D)] + [_ANY] * len(after),
        out_specs=[_row_spec(D), _part_spec(D)],
        out_shape=[jax.ShapeDtypeStruct((T, D), F32), jax.ShapeDtypeStruct((nblk, SUBLANE, D), F32)],
        name=name, compiler_params=_cp(("parallel",)),
    )(dxa, *dhs, sc, x, *after)


def _merge_fwd(gates, ts, name):
    T, D = ts[0].shape
    tm = 128

    def body(g_ref, t0, t1, t2, t3, o_ref):
        acc = g_ref[:, 0:D].astype(F32) * t0[...].astype(F32)
        for k, t in enumerate((t1, t2, t3), start=1):
            acc = acc + g_ref[:, k * D:(k + 1) * D].astype(F32) * t[...].astype(F32)
        o_ref[...] = acc.astype(BF16)

    rs = pl.BlockSpec((tm, D), lambda i: (i, 0))
    return pl.pallas_call(
        body, grid=(T // tm,),
        in_specs=[pl.BlockSpec((tm, 4 * D), lambda i: (i, 0)), rs, rs, rs, rs],
        out_specs=rs, out_shape=jax.ShapeDtypeStruct((T, D), BF16),
        name=name, compiler_params=_cp(("parallel",)),
    )(gates, *ts)


def _merge_bwd(dmg, gates, ts, name):
    T, D = dmg.shape
    tm = 128
    nblk = T // tm

    def body(d_ref, g_ref, t0, t1, t2, t3, dpre_ref, dt0, dt1, dt2, dt3, part_ref):
        d = d_ref[...]
        for k, (t, dt) in enumerate(zip((t0, t1, t2, t3), (dt0, dt1, dt2, dt3))):
            gk = g_ref[:, k * D:(k + 1) * D].astype(F32)
            dt[...] = (d * gk).astype(BF16)
            dpre = d * t[...].astype(F32) * gk * (1.0 - gk)
            dpre_ref[:, k * D:(k + 1) * D] = dpre.astype(BF16)
            part_ref[:, k * D:(k + 1) * D] = jnp.sum(dpre, axis=0, keepdims=True)

    rs = pl.BlockSpec((tm, D), lambda i: (i, 0))
    wide = pl.BlockSpec((tm, 4 * D), lambda i: (i, 0))
    return pl.pallas_call(
        body, grid=(nblk,),
        in_specs=[rs, wide, rs, rs, rs, rs],
        out_specs=[wide, rs, rs, rs, rs, pl.BlockSpec((None, 1, 4 * D), lambda i: (i, 0, 0))],
        out_shape=[jax.ShapeDtypeStruct((T, 4 * D), BF16)] + [jax.ShapeDtypeStruct((T, D), BF16)] * 4
                  + [jax.ShapeDtypeStruct((nblk, 1, 4 * D), F32)],
        name=name, compiler_params=_cp(("parallel",)),
    )(dmg, gates, *ts)


def _ffn_up(a, wg, wu, name, tm_pref=384):
    T, K = a.shape
    nq, _, ns = wg.shape
    tm = _tile(T, tm_pref, SUBLANE * 2)

    def body(a_ref, g_ref, u_ref, gg_ref, uu_ref, act_ref):
        av = a_ref[...]
        g = jnp.dot(av, g_ref[...], preferred_element_type=F32)
        u = jnp.dot(av, u_ref[...], preferred_element_type=F32)
        gg_ref[...] = g.astype(BF16)
        uu_ref[...] = u.astype(BF16)
        act_ref[...] = (g * jax.nn.sigmoid(g) * u).astype(BF16)

    ws = pl.BlockSpec((None, K, ns), lambda q, i: (q, 0, 0))
    os = pl.BlockSpec((tm, ns), lambda q, i: (i, q))
    return pl.pallas_call(
        body, grid=(nq, T // tm),
        in_specs=[pl.BlockSpec((tm, K), lambda q, i: (i, 0)), ws, ws], out_specs=[os, os, os],
        out_shape=[jax.ShapeDtypeStruct((T, nq * ns), BF16)] * 3,
        name=name, compiler_params=_cp(("parallel", "parallel"), VMEM_BIG),
    )(a, wg, wu)


def _ffn_down_bwd(dff, wd, gg, uu, name, tm_pref=384):
    T, N = dff.shape
    nq, kc, _ = wd.shape
    tm = _tile(T, tm_pref, SUBLANE * 2)

    def body(d_ref, w_ref, g_ref, u_ref, dg_ref, du_ref):
        da = lax.dot_general(d_ref[...], w_ref[...], (((1,), (1,)), ((), ())), preferred_element_type=F32)
        g = g_ref[...].astype(F32)
        sig = jax.nn.sigmoid(g)
        dg_ref[...] = (da * u_ref[...].astype(F32) * sig * (1.0 + g * (1.0 - sig))).astype(BF16)
        du_ref[...] = (da * g * sig).astype(BF16)

    ts = pl.BlockSpec((tm, kc), lambda q, i: (i, q))
    return pl.pallas_call(
        body, grid=(nq, T // tm),
        in_specs=[pl.BlockSpec((tm, N), lambda q, i: (i, 0)), pl.BlockSpec((None, kc, N), lambda q, i: (q, 0, 0)),
                  ts, ts],
        out_specs=[ts, ts], out_shape=[jax.ShapeDtypeStruct((T, nq * kc), BF16)] * 2,
        name=name, compiler_params=_cp(("parallel", "parallel"), VMEM_BIG),
    )(dff, wd, gg, uu)


def _swap_halves(v):
    lane = lax.broadcasted_iota(jnp.int32, v.shape, 1)
    return jnp.where((lane % 64) < 32, pltpu.roll(v, 96, 1), pltpu.roll(v, 32, 1))


def _rope_tables(n, nc):
    rows = n // GRID_W
    row = jnp.repeat(jnp.arange(rows), GRID_W).astype(F32)
    col = jnp.tile(jnp.arange(GRID_W), rows).astype(F32)
    inv = ROPE_THETA ** (-jnp.arange(0, 64, 2, dtype=F32) / 64)
    ang_r = row[:, None] * inv
    ang_c = col[:, None] * inv
    cos = jnp.concatenate([jnp.cos(ang_r), jnp.cos(ang_r), jnp.cos(ang_c), jnp.cos(ang_c)], axis=-1)
    sin = jnp.concatenate([-jnp.sin(ang_r), jnp.sin(ang_r), -jnp.sin(ang_c), jnp.sin(ang_c)], axis=-1)
    cos = jnp.concatenate([jnp.ones((nc, HEAD_DIM), F32), cos], axis=0)
    sin = jnp.concatenate([jnp.zeros((nc, HEAD_DIM), F32), sin], axis=0)
    return cos, sin


def _qk_fwd(p, cos, sin, qg, kg, name):
    T = p.shape[0]

    def body(p_ref, c_ref, s_ref, qg_ref, kg_ref, q_ref, k_ref, v_ref):
        c, s = c_ref[...], s_ref[...]
        for h in range(N_HEADS + N_KV):
            x = p_ref[:, h * HEAD_DIM:(h + 1) * HEAD_DIM]
            rs = lax.rsqrt(jnp.mean(x * x, axis=-1, keepdims=True) + RMS_EPS)
            gain = qg_ref[...] if h < N_HEADS else kg_ref[...]
            yv = x * rs * gain
            out = yv * c + _swap_halves(yv) * s
            if h < N_HEADS:
                out = out * Q_PRESCALE
            out = out.astype(BF16)
            if h < N_HEADS:
                q_ref[:, h * HEAD_DIM:(h + 1) * HEAD_DIM] = out
            else:
                k_ref[:, (h - N_HEADS) * HEAD_DIM:(h - N_HEADS + 1) * HEAD_DIM] = out
        v_ref[...] = p_ref[:, Q_W + KV_W:QKV_W].astype(BF16)

    return pl.pallas_call(
        body, grid=(T // ROW_TM,),
        in_specs=[_row_spec(QKV_W), _row_spec(HEAD_DIM), _row_spec(HEAD_DIM), _vec_spec(HEAD_DIM), _vec_spec(HEAD_DIM)],
        out_specs=[_row_spec(Q_W), _row_spec(KV_W), _row_spec(KV_W)],
        out_shape=[jax.ShapeDtypeStruct((T, Q_W), BF16), jax.ShapeDtypeStruct((T, KV_W), BF16),
                   jax.ShapeDtypeStruct((T, KV_W), BF16)],
        name=name, compiler_params=_cp(("parallel",)),
    )(p, cos, sin, qg, kg)


def _qk_bwd(p, dq, dk, dv, cos, sin, qg, kg, name):
    T = p.shape[0]
    nblk = T // ROW_TM

    def body(p_ref, dq_ref, dk_ref, dv_ref, c_ref, s_ref, qg_ref, kg_ref, dp_ref, part_ref):
        c, s = c_ref[...], s_ref[...]
        dgq = jnp.zeros((1, HEAD_DIM), F32)
        dgk = jnp.zeros((1, HEAD_DIM), F32)
        for h in range(N_HEADS + N_KV):
            x = p_ref[:, h * HEAD_DIM:(h + 1) * HEAD_DIM]
            if h < N_HEADS:
                d = dq_ref[:, h * HEAD_DIM:(h + 1) * HEAD_DIM]
                gain = qg_ref[...]
            else:
                d = dk_ref[:, (h - N_HEADS) * HEAD_DIM:(h - N_HEADS + 1) * HEAD_DIM]
                gain = kg_ref[...]
            dyv = d * c + _swap_halves(d * s)
            rs = lax.rsqrt(jnp.mean(x * x, axis=-1, keepdims=True) + RMS_EPS)
            xn = x * rs
            dgsum = jnp.sum(dyv * xn, axis=0, keepdims=True)
            if h < N_HEADS:
                dgq = dgq + dgsum
            else:
                dgk = dgk + dgsum
            dxg = dyv * gain
            dx = rs * (dxg - xn * jnp.mean(dxg * xn, axis=-1, keepdims=True))
            dp_ref[:, h * HEAD_DIM:(h + 1) * HEAD_DIM] = dx.astype(BF16)
        dp_ref[:, Q_W + KV_W:QKV_W] = dv_ref[...].astype(BF16)
        part_ref[pl.ds(0, 1), :] = dgq
        part_ref[pl.ds(1, 1), :] = dgk
        part_ref[pl.ds(2, SUBLANE - 2), :] = jnp.zeros((SUBLANE - 2, HEAD_DIM), F32)

    return pl.pallas_call(
        body, grid=(nblk,),
        in_specs=[_row_spec(QKV_W), _row_spec(Q_W), _row_spec(KV_W), _row_spec(KV_W),
                  _row_spec(HEAD_DIM), _row_spec(HEAD_DIM), _vec_spec(HEAD_DIM), _vec_spec(HEAD_DIM)],
        out_specs=[_row_spec(QKV_W), _part_spec(HEAD_DIM)],
        out_shape=[jax.ShapeDtypeStruct((T, IN_W), BF16), jax.ShapeDtypeStruct((nblk, SUBLANE, HEAD_DIM), F32)],
        name=name, compiler_params=_cp(("parallel",)),
    )(p, dq, dk, dv, cos, sin, qg, kg)


ATTN_SCALE = HEAD_DIM ** -0.5
LOG2E = 1.4426950408889634
Q_PRESCALE = ATTN_SCALE * LOG2E


def _attn_weights(q, k):
    s = lax.dot_general(q, k, (((1,), (1,)), ((), ())), preferred_element_type=F32)
    e = jnp.exp2(s - jnp.max(s, axis=-1, keepdims=True))
    return e, 1.0 / jnp.sum(e, axis=-1, keepdims=True)


def _attn_fwd(q, k, v, nc, name):
    T = q.shape[0]

    def heads(q_ref, k_ref, v_ref, o_ref, nkeys):
        for h in range(N_HEADS):
            g = h // KV_GROUP
            kk = k_ref[0:nkeys, g * HEAD_DIM:(g + 1) * HEAD_DIM]
            vv = v_ref[0:nkeys, g * HEAD_DIM:(g + 1) * HEAD_DIM]
            e, rl = _attn_weights(q_ref[:, h * HEAD_DIM:(h + 1) * HEAD_DIM], kk)
            o = jnp.dot(e.astype(BF16), vv, preferred_element_type=F32) * rl
            o_ref[:, h * HEAD_DIM:(h + 1) * HEAD_DIM] = o.astype(BF16)

    def body(q_ref, k_ref, v_ref, o_ref):
        i = pl.program_id(0)

        @pl.when(i < nc // Q_BLOCK)
        def _():
            heads(q_ref, k_ref, v_ref, o_ref, nc)

        @pl.when(i >= nc // Q_BLOCK)
        def _():
            heads(q_ref, k_ref, v_ref, o_ref, T)

    whole = pl.BlockSpec((T, KV_W), lambda i: (0, 0))
    qs = pl.BlockSpec((Q_BLOCK, Q_W), lambda i: (i, 0))
    return pl.pallas_call(
        body, grid=(T // Q_BLOCK,), in_specs=[qs, whole, whole], out_specs=qs,
        out_shape=jax.ShapeDtypeStruct((T, Q_W), BF16),
        name=name, compiler_params=_cp(("parallel",)),
    )(q, k, v)


def _attn_bwd(q, k, v, do, nc, name):
    T = q.shape[0]

    def heads(q_ref, k_ref, v_ref, do_ref, dq_ref, dk_ref, dv_ref, nkeys):
        for h in range(N_HEADS):
            g = h // KV_GROUP
            cols = slice(g * HEAD_DIM, (g + 1) * HEAD_DIM)
            hc = slice(h * HEAD_DIM, (h + 1) * HEAD_DIM)
            qh, kk, vv, doh = q_ref[:, hc], k_ref[0:nkeys, cols], v_ref[0:nkeys, cols], do_ref[:, hc]
            e, rl = _attn_weights(qh, kk)
            dpr = lax.dot_general(doh, vv, (((1,), (1,)), ((), ())), preferred_element_type=F32)
            delta = jnp.sum(e * dpr, axis=-1, keepdims=True) * rl
            dsu = (e * (dpr - delta)).astype(BF16)
            dq_ref[:, hc] = jnp.dot(dsu, kk, preferred_element_type=F32) * (rl * ATTN_SCALE)
            q_rows = (qh.astype(F32) * (rl * (1.0 / LOG2E))).astype(BF16)
            dk_ref[0:nkeys, cols] += lax.dot_general(dsu, q_rows, (((0,), (0,)), ((), ())),
                                                     preferred_element_type=F32)
            do_rows = (doh.astype(F32) * rl).astype(BF16)
            dv_ref[0:nkeys, cols] += lax.dot_general(e.astype(BF16), do_rows, (((0,), (0,)), ((), ())),
                                                     preferred_element_type=F32)

    def body(q_ref, k_ref, v_ref, do_ref, dq_ref, dk_ref, dv_ref):
        i = pl.program_id(0)

        @pl.when(i == 0)
        def _():
            dk_ref[...] = jnp.zeros_like(dk_ref)
            dv_ref[...] = jnp.zeros_like(dv_ref)

        @pl.when(i < nc // Q_BLOCK)
        def _():
            heads(q_ref, k_ref, v_ref, do_ref, dq_ref, dk_ref, dv_ref, nc)

        @pl.when(i >= nc // Q_BLOCK)
        def _():
            heads(q_ref, k_ref, v_ref, do_ref, dq_ref, dk_ref, dv_ref, T)

    whole = pl.BlockSpec((T, KV_W), lambda i: (0, 0))
    qs = pl.BlockSpec((Q_BLOCK, Q_W), lambda i: (i, 0))
    return pl.pallas_call(
        body, grid=(T // Q_BLOCK,), in_specs=[qs, whole, whole, qs], out_specs=[qs, whole, whole],
        out_shape=[jax.ShapeDtypeStruct((T, Q_W), F32), jax.ShapeDtypeStruct((T, KV_W), F32),
                   jax.ShapeDtypeStruct((T, KV_W), F32)],
        name=name, compiler_params=_cp(("arbitrary",)),
    )(q, k, v, do)


def _shift_rows(z, o, nc):
    T = z.shape[0]
    t = lax.broadcasted_iota(jnp.int32, (T, 1), 0)
    lo = jnp.where(t < nc, 0, nc)
    hi = jnp.where(t < nc, nc, T)
    ok = jnp.logical_and(t + o >= lo, t + o < hi)
    rolled = z if o == 0 else pltpu.roll(z, (-o) % T, 0)
    return jnp.where(ok, rolled, 0.0), ok


def _pool_window(w):
    left = w // 2
    return -left, w - 1 - left


def _pool_d(z, w, nc):
    o0, o1 = _pool_window(w)
    tot = jnp.zeros_like(z)
    cnt = jnp.zeros((z.shape[0], 1), F32)
    for o in range(o0, o1 + 1):
        sh, ok = _shift_rows(z, o, nc)
        tot = tot + sh
        cnt = cnt + ok.astype(F32)
    return tot / cnt - z, cnt


def _pool_fwd(p, pool_w, pool_scale, nc, name):
    T = p.shape[0]

    def body(z_ref, w_ref, s_ref, o_ref):
        for g, w in enumerate(POOL_WINDOWS):
            cs = slice(g * GC, (g + 1) * GC)
            d, _ = _pool_d(z_ref[:, cs], w, nc)
            yv = jnp.dot(d.astype(BF16), w_ref[g].astype(BF16), preferred_element_type=F32)
            o_ref[:, cs] = (yv * s_ref[:, cs]).astype(BF16)

    return pl.pallas_call(
        body, grid=(1,),
        in_specs=[pl.BlockSpec((T, BR_W), lambda i: (0, OFF_POOL // BR_W)),
                  pl.BlockSpec((N_GROUP, GC, GC), lambda i: (0, 0, 0)), _vec_spec(BR_W)],
        out_specs=pl.BlockSpec((T, BR_W), lambda i: (0, 0)),
        out_shape=jax.ShapeDtypeStruct((T, BR_W), BF16),
        name=name, compiler_params=_cp(("arbitrary",), VMEM_BIG),
    )(p, pool_w, pool_scale)


def _pool_bwd(p, dy, pool_w, pool_scale, dp, nc, name):
    T = p.shape[0]

    def body(z_ref, dy_ref, w_ref, s_ref, dp_in_ref, dz_ref, dw_ref, ds_ref):
        for g, w in enumerate(POOL_WINDOWS):
            cs = slice(g * GC, (g + 1) * GC)
            d, cnt = _pool_d(z_ref[:, cs], w, nc)
            db = d.astype(BF16)
            wb = w_ref[g].astype(BF16)
            dyv = dy_ref[:, cs]
            e = (dyv * s_ref[:, cs]).astype(BF16)
            dw_ref[g] = lax.dot_general(db, e, (((0,), (0,)), ((), ())), preferred_element_type=F32)
            yraw = jnp.dot(db, wb, preferred_element_type=F32)
            ds_ref[:, cs] = jnp.sum(dyv * yraw, axis=0, keepdims=True)
            dd = lax.dot_general(e, wb, (((1,), (1,)), ((), ())), preferred_element_type=F32)
            ec = dd / cnt
            o0, o1 = _pool_window(w)
            tot = jnp.zeros_like(dd)
            for o in range(o0, o1 + 1):
                tot = tot + _shift_rows(ec, -o, nc)[0]
            dz_ref[:, cs] = (tot - dd).astype(BF16)

    return pl.pallas_call(
        body, grid=(1,),
        in_specs=[pl.BlockSpec((T, BR_W), lambda i: (0, OFF_POOL // BR_W)),
                  pl.BlockSpec((T, BR_W), lambda i: (0, 0)),
                  pl.BlockSpec((N_GROUP, GC, GC), lambda i: (0, 0, 0)), _vec_spec(BR_W), _ANY],
        out_specs=[pl.BlockSpec((T, BR_W), lambda i: (0, OFF_POOL // BR_W)),
                   pl.BlockSpec((N_GROUP, GC, GC), lambda i: (0, 0, 0)), _vec_spec(BR_W)],
        out_shape=[jax.ShapeDtypeStruct(dp.shape, BF16), jax.ShapeDtypeStruct((N_GROUP, GC, GC), F32),
                   jax.ShapeDtypeStruct((1, BR_W), F32)],
        input_output_aliases={4: 0},
        name=name, compiler_params=_cp(("arbitrary",), VMEM_BIG),
    )(p, dy, pool_w, pool_scale, dp)


_GELU_C = math.sqrt(2.0 / math.pi)


def _gelu(x):
    return 0.5 * x * (1.0 + jnp.tanh(_GELU_C * (x + 0.044715 * x * x * x)))


def _gelu_grad(x):
    th = jnp.tanh(_GELU_C * (x + 0.044715 * x * x * x))
    return 0.5 * (1.0 + th) + 0.5 * x * (1.0 - th * th) * _GELU_C * (1.0 + 3.0 * 0.044715 * x * x)


def _sgu_fwd(p, ln_g, ln_b, w_s, b_st, name):
    T = p.shape[0]

    def body(pu_ref, pv_ref, g_ref, b_ref, w_ref, bs_ref, o_ref):
        u = _gelu(pu_ref[...])
        vhat, _ = _ln_stats(_gelu(pv_ref[...]))
        vn = (vhat * g_ref[...] + b_ref[...]).astype(BF16)
        for g in range(N_GROUP):
            cs = slice(g * GC, (g + 1) * GC)
            s = jnp.dot(w_ref[g].astype(BF16), vn[:, cs], preferred_element_type=F32) + bs_ref[:, g:g + 1]
            o_ref[:, cs] = (u[:, cs] * s).astype(BF16)

    chunk = lambda off: pl.BlockSpec((GC, BR_W), lambda i: (i, off // BR_W))
    return pl.pallas_call(
        body, grid=(T // GC,),
        in_specs=[chunk(OFF_U), chunk(OFF_VG), _vec_spec(BR_W), _vec_spec(BR_W),
                  pl.BlockSpec((N_GROUP, GC, GC), lambda i: (0, 0, 0)),
                  pl.BlockSpec((GC, N_GROUP), lambda i: (0, 0))],
        out_specs=pl.BlockSpec((GC, BR_W), lambda i: (i, 0)),
        out_shape=jax.ShapeDtypeStruct((T, BR_W), BF16),
        name=name, compiler_params=_cp(("parallel",)),
    )(p, p, ln_g, ln_b, w_s, b_st)


def _sgu_bwd(p, dy, ln_g, ln_b, w_s, b_st, dp, name):
    T = p.shape[0]

    def body(pu_ref, pv_ref, dy_ref, g_ref, b_ref, w_ref, bs_ref, dp_in_ref, dp_ref, dw_ref, dsacc_ref, dln_ref):
        i = pl.program_id(0)

        @pl.when(i == 0)
        def _():
            dw_ref[...] = jnp.zeros_like(dw_ref)
            dsacc_ref[...] = jnp.zeros_like(dsacc_ref)
            dln_ref[...] = jnp.zeros_like(dln_ref)

        pu, pv, dyv = pu_ref[...], pv_ref[...], dy_ref[...]
        u = _gelu(pu)
        vhat, rstd = _ln_stats(_gelu(pv))
        vn = (vhat * g_ref[...] + b_ref[...]).astype(BF16)
        ds = dyv * u
        dsb = ds.astype(BF16)
        dsacc_ref[...] += ds
        dvn_parts = []
        for g in range(N_GROUP):
            cs = slice(g * GC, (g + 1) * GC)
            wb = w_ref[g].astype(BF16)
            s = jnp.dot(wb, vn[:, cs], preferred_element_type=F32) + bs_ref[:, g:g + 1]
            dp_ref[:, cs] = (dyv[:, cs] * s * _gelu_grad(pu[:, cs])).astype(BF16)
            dvn_parts.append(lax.dot_general(wb, dsb[:, cs], (((0,), (0,)), ((), ())),
                                             preferred_element_type=F32))
            dw_ref[g] += lax.dot_general(dsb[:, cs], vn[:, cs], (((1,), (1,)), ((), ())),
                                         preferred_element_type=F32)
        dvn = jnp.concatenate(dvn_parts, axis=-1)
        dln_ref[pl.ds(0, 1), :] += jnp.sum(dvn * vhat, axis=0, keepdims=True)
        dln_ref[pl.ds(1, 1), :] += jnp.sum(dvn, axis=0, keepdims=True)
        dvhat = dvn * g_ref[...]
        m1 = jnp.mean(dvhat, axis=-1, keepdims=True)
        m2 = jnp.mean(dvhat * vhat, axis=-1, keepdims=True)
        dv = rstd * (dvhat - m1 - vhat * m2)
        dp_ref[:, BR_W:2 * BR_W] = (dv * _gelu_grad(pv)).astype(BF16)

    chunk = lambda off: pl.BlockSpec((GC, BR_W), lambda i: (i, off // BR_W))
    return pl.pallas_call(
        body, grid=(T // GC,),
        in_specs=[chunk(OFF_U), chunk(OFF_VG), pl.BlockSpec((GC, BR_W), lambda i: (i, 0)),
                  _vec_spec(BR_W), _vec_spec(BR_W),
                  pl.BlockSpec((N_GROUP, GC, GC), lambda i: (0, 0, 0)),
                  pl.BlockSpec((GC, N_GROUP), lambda i: (0, 0)), _ANY],
        out_specs=[pl.BlockSpec((GC, 2 * BR_W), lambda i: (i, OFF_U // (2 * BR_W))),
                   pl.BlockSpec((N_GROUP, GC, GC), lambda i: (0, 0, 0)),
                   pl.BlockSpec((GC, BR_W), lambda i: (0, 0)),
                   pl.BlockSpec((SUBLANE, BR_W), lambda i: (0, 0))],
        out_shape=[jax.ShapeDtypeStruct(dp.shape, BF16), jax.ShapeDtypeStruct((N_GROUP, GC, GC), F32),
                   jax.ShapeDtypeStruct((GC, BR_W), F32), jax.ShapeDtypeStruct((SUBLANE, BR_W), F32)],
        input_output_aliases={7: 0},
        name=name, compiler_params=_cp(("arbitrary",)),
    )(p, p, dy, ln_g, ln_b, w_s, b_st, dp)


def _conv_fwd(p, conv_w, nc, name):
    T = p.shape[0]

    def body(cb_ref, cc_ref, cx_ref, w_ref, o_ref):
        zz = cc_ref[...] * cx_ref[...]
        conv = (w_ref[0:1, :] * _shift_rows(zz, -1, nc)[0] + w_ref[1:2, :] * zz
                + w_ref[2:3, :] * _shift_rows(zz, 1, nc)[0])
        o_ref[...] = (cb_ref[...] * conv).astype(BF16)

    col = lambda off: pl.BlockSpec((T, GC), lambda j: (0, off // GC + j))
    return pl.pallas_call(
        body, grid=(N_GROUP,),
        in_specs=[col(OFF_CB), col(OFF_CC), col(OFF_CX), pl.BlockSpec((3, GC), lambda j: (0, j))],
        out_specs=pl.BlockSpec((T, GC), lambda j: (0, j)),
        out_shape=jax.ShapeDtypeStruct((T, BR_W), BF16),
        name=name, compiler_params=_cp(("parallel",)),
    )(p, p, p, conv_w)


def _conv_bwd(p, dy, conv_w, dp, nc, name):
    T = p.shape[0]

    def body(cb_ref, cc_ref, cx_ref, dy_ref, w_ref, dp_in_ref, dp_ref, dw_ref):
        part = pl.program_id(1)
        cc, cx = cc_ref[...], cx_ref[...]
        dconv = dy_ref[...] * cb_ref[...]

        @pl.when(part == 0)
        def _():
            zz = cc * cx
            zm, zp = _shift_rows(zz, -1, nc)[0], _shift_rows(zz, 1, nc)[0]
            conv = w_ref[0:1, :] * zm + w_ref[1:2, :] * zz + w_ref[2:3, :] * zp
            dw_ref[0:1, :] = jnp.sum(dconv * zm, axis=0, keepdims=True)
            dw_ref[1:2, :] = jnp.sum(dconv * zz, axis=0, keepdims=True)
            dw_ref[2:3, :] = jnp.sum(dconv * zp, axis=0, keepdims=True)
            dp_ref[...] = (dy_ref[...] * conv).astype(BF16)

        @pl.when(part > 0)
        def _():
            dzz = (w_ref[0:1, :] * _shift_rows(dconv, 1, nc)[0] + w_ref[1:2, :] * dconv
                   + w_ref[2:3, :] * _shift_rows(dconv, -1, nc)[0])
            dp_ref[...] = (dzz * jnp.where(part == 1, cx, cc)).astype(BF16)

    col = lambda off: pl.BlockSpec((T, GC), lambda j, part: (0, off // GC + j))
    return pl.pallas_call(
        body, grid=(N_GROUP, 3),
        in_specs=[col(OFF_CB), col(OFF_CC), col(OFF_CX), pl.BlockSpec((T, GC), lambda j, part: (0, j)),
                  pl.BlockSpec((3, GC), lambda j, part: (0, j)), _ANY],
        out_specs=[pl.BlockSpec((T, GC), lambda j, part: (0, OFF_CB // GC + part * N_GROUP + j)),
                   pl.BlockSpec((3, GC), lambda j, part: (0, j))],
        out_shape=[jax.ShapeDtypeStruct(dp.shape, BF16), jax.ShapeDtypeStruct((3, BR_W), F32)],
        input_output_aliases={5: 0},
        name=name, compiler_params=_cp(("parallel", "arbitrary")),
    )(p, p, p, dy, conv_w, dp)


def _rows_tile(rows, cols, n_arrays, vmem_bytes=24 * 1024 * 1024):
    budget = vmem_bytes // (2 * 4 * n_arrays * cols)
    return _tile(rows, max(SUBLANE * 2, min(budget, 1024)), SUBLANE * 2) if rows % (SUBLANE * 2) == 0 else rows


def _cast_into(chip, w, l, name, after=()):
    _, K, cols = w.shape
    tr = _rows_tile(K, cols, 2)

    def body(q_ref, w_ref, *rest):
        rest[-1][...] = w_ref[...].astype(BF16)

    return pl.pallas_call(
        body,
        grid_spec=pltpu.PrefetchScalarGridSpec(
            num_scalar_prefetch=1, grid=(K // tr,),
            in_specs=[pl.BlockSpec((None, tr, cols), lambda i, q: (l, i, 0))] + [_ANY] * len(after),
            out_specs=pl.BlockSpec((None, tr, cols), lambda i, q: (q[0], i, 0))),
        out_shape=jax.ShapeDtypeStruct((N_CHIP, K, cols), BF16),
        name=name, compiler_params=_cp(("parallel",)),
    )(chip, w, *after)


ADAM_VMEM = 34 * 1024 * 1024


def _adamw(w, g, m, v, name, after=()):
    shape = w.shape
    cols = shape[-1]
    rows = w.size // cols
    tr = _rows_tile(rows, cols, 7, ADAM_VMEM)
    c1 = 1.0 - ADAM_B1 ** ADAM_STEP
    c2 = 1.0 - ADAM_B2 ** ADAM_STEP

    def body(w_ref, g_ref, m_ref, v_ref, *rest):
        d_ref, mo_ref, vo_ref = rest[-3:]
        gv = g_ref[...]
        mn = ADAM_B1 * m_ref[...] + (1.0 - ADAM_B1) * gv
        vn = ADAM_B2 * v_ref[...] + (1.0 - ADAM_B2) * (gv * gv)
        mo_ref[...] = mn
        vo_ref[...] = vn
        d_ref[...] = -ADAM_LR * ((mn / c1) / (jnp.sqrt(vn / c2) + ADAM_EPS) + ADAM_WD * w_ref[...])

    rs = pl.BlockSpec((tr, cols), lambda i: (i, 0))
    outs = pl.pallas_call(
        body, grid=(rows // tr,), in_specs=[rs] * 4 + [_ANY] * len(after), out_specs=[rs] * 3,
        out_shape=[jax.ShapeDtypeStruct((rows, cols), F32)] * 3,
        name=name, compiler_params=_cp(("parallel",)),
    )(*[a.reshape(rows, cols) for a in (w, g, m, v)], *after)
    return [o.reshape(shape) for o in outs]


def _adamw_layer(w, g, m, v, l, prev, after, name):
    L, K, cols = w.shape
    tr = _rows_tile(K, cols, 7, ADAM_VMEM)
    c1 = 1.0 - ADAM_B1 ** ADAM_STEP
    c2 = 1.0 - ADAM_B2 ** ADAM_STEP
    nprev = 0 if prev is None else 3

    def body(*refs):
        w_ref, g_ref, m_ref, v_ref = refs[:4]
        d_ref, mo_ref, vo_ref = refs[-3:]
        gv = g_ref[...]
        mn = ADAM_B1 * m_ref[...] + (1.0 - ADAM_B1) * gv
        vn = ADAM_B2 * v_ref[...] + (1.0 - ADAM_B2) * (gv * gv)
        mo_ref[...] = mn
        vo_ref[...] = vn
        d_ref[...] = -ADAM_LR * ((mn / c1) / (jnp.sqrt(vn / c2) + ADAM_EPS) + ADAM_WD * w_ref[...])

    rs = pl.BlockSpec((None, tr, cols), lambda i: (l, i, 0))
    extra = list(prev or ()) + list(after)
    return pl.pallas_call(
        body, grid=(K // tr,), in_specs=[rs] * 4 + [_ANY] * len(extra), out_specs=[rs] * 3,
        out_shape=[jax.ShapeDtypeStruct((L, K, cols), F32)] * 3,
        input_output_aliases={4 + k: k for k in range(nprev)},
        name=name, compiler_params=_cp(("parallel",)),
    )(w, g, m, v, *extra)


def _pair_sum(core, dw, recv, name):
    nq, half, cols = recv.shape
    tr = _rows_tile(half, cols, 3)
    nb = half // tr

    def body(c_ref, a_ref, r_ref, o_ref):
        o_ref[...] = (a_ref[...].astype(F32) + r_ref[...].astype(F32)).astype(BF16)

    rs = pl.BlockSpec((None, tr, cols), lambda q, i, c: (q, i, 0))
    return pl.pallas_call(
        body,
        grid_spec=pltpu.PrefetchScalarGridSpec(
            num_scalar_prefetch=1, grid=(nq, nb),
            in_specs=[pl.BlockSpec((None, tr, cols), lambda q, i, c: (q, c[0] * nb + i, 0)), rs],
            out_specs=rs),
        out_shape=jax.ShapeDtypeStruct((nq, half, cols), BF16),
        name=name, compiler_params=_cp(("parallel", "parallel")),
    )(core, dw, recv)


def _chip_sum(chip, core, psum, land, l, n_layers, prev, name):
    _, half, cols = psum.shape
    tr = _rows_tile(half, cols, 5)
    nb = half // tr

    def body(*refs):
        p_ref, r_ref, o_ref = refs[2], refs[3], refs[-1]
        o_ref[...] = (p_ref[...].astype(F32) + r_ref[0].astype(F32) + r_ref[1].astype(F32)
                      + r_ref[2].astype(F32))

    extra = [] if prev is None else [prev]
    return pl.pallas_call(
        body,
        grid_spec=pltpu.PrefetchScalarGridSpec(
            num_scalar_prefetch=2, grid=(nb,),
            in_specs=[pl.BlockSpec((None, tr, cols), lambda i, q, c: (q[0], i, 0)),
                      pl.BlockSpec((3, tr, cols), lambda i, q, c: (0, i, 0))] + [_ANY] * len(extra),
            out_specs=pl.BlockSpec((None, tr, cols), lambda i, q, c: (l, c[0] * nb + i, 0))),
        out_shape=jax.ShapeDtypeStruct((n_layers, 2 * half, cols), F32),
        input_output_aliases={4: 0} if prev is not None else {},
        name=name, compiler_params=_cp(("parallel",)),
    )(chip, core, psum, land, *extra)


def _slab_sum(gathered, name, after=()):
    n, rows, cols = gathered.shape
    tr = _tile(rows, 512, SUBLANE)

    def body(g_ref, *rest):
        acc = g_ref[0]
        for d in range(1, n):
            acc = acc + g_ref[d]
        rest[-1][...] = acc

    return pl.pallas_call(
        body, grid=(rows // tr,),
        in_specs=[pl.BlockSpec((n, tr, cols), lambda i: (0, i, 0))] + [_ANY] * len(after),
        out_specs=pl.BlockSpec((tr, cols), lambda i: (i, 0)),
        out_shape=jax.ShapeDtypeStruct((rows, cols), F32),
        name=name, compiler_params=_cp(("parallel",)),
    )(gathered, *after)


def _silu(x):
    return x * jax.nn.sigmoid(x)


def _ada_fwd(cvec, w_ada, b_cols, name):
    L, D, ns = w_ada.shape
    tn = _tile(ns, 768)

    def body(c_ref, w_ref, b_ref, o_ref):
        s = _silu(c_ref[...]).astype(BF16)
        o_ref[...] = jnp.dot(s, w_ref[...].astype(BF16), preferred_element_type=F32) + b_ref[...]

    return pl.pallas_call(
        body, grid=(L, ns // tn),
        in_specs=[pl.BlockSpec((16, D), lambda l, j: (0, 0)),
                  pl.BlockSpec((None, D, tn), lambda l, j: (l, 0, j)),
                  pl.BlockSpec((None, 1, tn), lambda l, j: (l, 0, j))],
        out_specs=pl.BlockSpec((None, 16, tn), lambda l, j: (l, 0, j)),
        out_shape=jax.ShapeDtypeStruct((L, 16, ns), F32),
        name=name, compiler_params=_cp(("parallel", "parallel")),
    )(cvec, w_ada, b_cols)


def _ada_bwd(cvec, dmod, w_ada, name):
    L, D, ns = w_ada.shape
    tn = _tile(ns, 768)

    def body(c_ref, d_ref, w_ref, gw_ref, ds_ref):
        first = jnp.logical_and(pl.program_id(0) == 0, pl.program_id(1) == 0)

        @pl.when(first)
        def _():
            ds_ref[...] = jnp.zeros_like(ds_ref)

        s = _silu(c_ref[...]).astype(BF16)
        db = d_ref[...].astype(BF16)
        gw_ref[...] = lax.dot_general(s, db, (((0,), (0,)), ((), ())), preferred_element_type=F32)
        ds_ref[...] += lax.dot_general(db, w_ref[...].astype(BF16), (((1,), (1,)), ((), ())),
                                       preferred_element_type=F32)

    return pl.pallas_call(
        body, grid=(L, ns // tn),
        in_specs=[pl.BlockSpec((16, D), lambda l, j: (0, 0)),
                  pl.BlockSpec((None, 16, tn), lambda l, j: (l, 0, j)),
                  pl.BlockSpec((None, D, tn), lambda l, j: (l, 0, j))],
        out_specs=[pl.BlockSpec((None, D, tn), lambda l, j: (l, 0, j)),
                   pl.BlockSpec((16, D), lambda l, j: (0, 0))],
        out_shape=[jax.ShapeDtypeStruct((L, D, ns), F32), jax.ShapeDtypeStruct((16, D), F32)],
        name=name, compiler_params=_cp(("arbitrary", "arbitrary")),
    )(cvec, dmod, w_ada)


def _cctx_grad(gathered, c_ctx, name):
    rows = c_ctx.shape[0]

    def body(g_ref, c_ref, o_ref):
        ds = g_ref[0] + g_ref[2] + g_ref[4] + g_ref[6]
        cv = c_ref[...]
        sig = jax.nn.sigmoid(cv)
        o_ref[...] = ds * sig * (1.0 + cv * (1.0 - sig))

    return pl.pallas_call(
        body, out_shape=jax.ShapeDtypeStruct((rows, LANE), F32), name=name,
    )(gathered, c_ctx)


def _place():
    x, y, c = lax.axis_index("x"), lax.axis_index("y"), lax.axis_index("c")
    chips = [(1 - x, y), (x, 1 - y), (1 - x, 1 - y)]
    return x, y, c, chips


def _all_gather_small(slab, name, after=()):
    rows, cols = slab.shape

    def body(x_ref, *rest):
        out_ref, send_sems, recv_sems, local_sem = rest[len(after):]
        x, y, c, chips = _place()
        me, sibling = (x, y, c), (x, y, 1 - c)

        def blk(px, py, pc):
            return out_ref.at[4 * px + 2 * py + pc]

        def copy(k, block, to, src=None):
            return pltpu.make_async_remote_copy(
                src_ref=blk(*block) if src is None else src, dst_ref=blk(*block),
                send_sem=send_sems.at[k], recv_sem=recv_sems.at[k], device_id=to, device_id_type=MESH)

        mine = pltpu.make_async_copy(x_ref, blk(*me), local_sem)
        mine.start()
        first = [copy(0, me, sibling, src=x_ref)]
        first += [copy(1 + j, me, (*chip, c), src=x_ref) for j, chip in enumerate(chips)]
        for cp in first:
            cp.start()
        passed = [copy(4 + j, (*chip, c), sibling) for j, chip in enumerate(chips)]
        for j, chip in enumerate(chips):
            copy(1 + j, (*chip, c), me).wait_recv()
            passed[j].start()
        copy(0, sibling, me).wait_recv()
        for j, chip in enumerate(chips):
            copy(4 + j, (*chip, 1 - c), me).wait_recv()
        for cp in first + passed:
            cp.wait_send()
        mine.wait()

    return pl.pallas_call(
        body, out_shape=jax.ShapeDtypeStruct((N_DEV, rows, cols), slab.dtype),
        in_specs=[pl.BlockSpec(memory_space=pltpu.VMEM)] + [_ANY] * len(after),
        out_specs=pl.BlockSpec(memory_space=pltpu.VMEM),
        scratch_shapes=[pltpu.SemaphoreType.DMA((7,)), pltpu.SemaphoreType.DMA((7,)), pltpu.SemaphoreType.DMA],
        name=name, compiler_params=pltpu.CompilerParams(vmem_limit_bytes=VMEM_MID),
    )(slab, *after)


def _hbm(a):
    return pltpu.with_memory_space_constraint(a, pltpu.HBM)


def _half_rows(ref, q, c):
    half = ref.shape[1] // 2
    return ref.at[q, pl.ds(c * half, half)]


def _ici_copy(src, dst, send_sems, recv_sems, k, to):
    return pltpu.make_async_remote_copy(src_ref=src, dst_ref=dst, send_sem=send_sems.at[k], recv_sem=recv_sems.at[k],
                                        device_id=to, device_id_type=MESH)


def _gather_start(bufs, sizes, name):
    n = len(bufs)
    ng = len(sizes)

    def body(*refs):
        ins = refs[:n]
        sems = refs[n:n + 2 * ng]
        token = refs[-1]
        x, y, c, chips = _place()
        a = 0
        for g, sz in enumerate(sizes):
            for i in range(sz):
                part = _half_rows(ins[a], 2 * x + y, c)
                for j, chip in enumerate(chips):
                    _ici_copy(part, part, sems[2 * g], sems[2 * g + 1], 3 * i + j, (*chip, c)).start()
                a += 1
        token[...] = jnp.zeros_like(token)

    sem_shapes = []
    for sz in sizes:
        sem_shapes += [pltpu.SemaphoreType.DMA((3 * sz,)), pltpu.SemaphoreType.DMA((3 * sz,))]
    outs = pl.pallas_call(
        body, name=name,
        out_shape=tuple(sem_shapes) + tuple(pltpu.HBM(b.shape, b.dtype) for b in bufs)
                  + (jax.ShapeDtypeStruct((SUBLANE, LANE), F32),),
        in_specs=[_HBM] * n,
        out_specs=tuple([_SEM] * (2 * ng) + [_HBM] * n + [pl.BlockSpec(memory_space=pltpu.VMEM)]),
        input_output_aliases={i: 2 * ng + i for i in range(n)},
        compiler_params=pltpu.CompilerParams(has_side_effects=_EFFECT),
    )(*[_hbm(b) for b in bufs])
    sems = [(outs[2 * g], outs[2 * g + 1]) for g in range(ng)]
    return sems, list(outs[2 * ng:2 * ng + n]), outs[-1]


def _gather_wait(bufs, sems, after, name):
    n = len(bufs)
    na = len(after)

    def body(*refs):
        ins = refs[:n]
        send_sems, recv_sems = refs[n], refs[n + 1]
        x, y, c, chips = _place()
        for i in range(n):
            sent = _half_rows(ins[i], 2 * x + y, c)
            for j, (cx, cy) in enumerate(chips):
                cp = _ici_copy(sent, _half_rows(ins[i], 2 * cx + cy, c), send_sems, recv_sems, 3 * i + j, (cx, cy, c))
                cp.wait_send()
                cp.wait_recv()

    outs = pl.pallas_call(
        body, name=name,
        out_shape=tuple(pltpu.HBM(b.shape, b.dtype) for b in bufs),
        in_specs=[_HBM] * n + [_SEM, _SEM] + [_ANY] * na,
        out_specs=tuple([_HBM] * n),
        input_output_aliases={i: i for i in range(n)},
        compiler_params=pltpu.CompilerParams(has_side_effects=_EFFECT),
    )(*bufs, sems[0], sems[1], *after)
    return list(outs)


def _exchange_halves(bufs, name):
    n = len(bufs)

    def body(*refs):
        ins = refs[:n]
        send_sems, recv_sems = refs[-2:]
        x, y, c, chips = _place()
        sends = []
        for i in range(n):
            for j, (cx, cy) in enumerate(chips):
                part = _half_rows(ins[i], 2 * cx + cy, c)
                cp = _ici_copy(part, part, send_sems, recv_sems, 3 * i + j, (x, y, 1 - c))
                cp.start()
                sends.append(cp)
        for i in range(n):
            for j, (cx, cy) in enumerate(chips):
                part = _half_rows(ins[i], 2 * cx + cy, 1 - c)
                _ici_copy(part, part, send_sems, recv_sems, 3 * i + j, (x, y, 1 - c)).wait_recv()
        for cp in sends:
            cp.wait_send()

    outs = pl.pallas_call(
        body, name=name,
        out_shape=[jax.ShapeDtypeStruct(b.shape, b.dtype) for b in bufs],
        in_specs=[_ANY] * n, out_specs=[_ANY] * n,
        input_output_aliases={i: i for i in range(n)},
        scratch_shapes=[pltpu.SemaphoreType.DMA((3 * n,)), pltpu.SemaphoreType.DMA((3 * n,))],
    )(*bufs)
    return list(outs)


def _other_half(ref, c):
    half = ref.shape[1] // 2
    return ref.at[:, pl.ds((1 - c) * half, half)]


def _prereduce_plan(src, land, x, y, c, chips):
    return [(_other_half(src, c), land, (x, y, 1 - c))]


def _prereduce_land(d):
    return (d.shape[0], d.shape[1] // 2, d.shape[2])


def _scatter_plan(src, land, x, y, c, chips):
    return [(src.at[2 * cx + cy], land.at[j], (cx, cy, c)) for j, (cx, cy) in enumerate(chips)]


def _scatter_land(p):
    return (3,) + p.shape[1:]


_COPIES_PER_SOURCE = {_prereduce_plan: 1, _scatter_plan: 3}


def _split_start(srcs, plan, land_shape, name):
    n = len(srcs)
    land_shapes = [land_shape(s) for s in srcs]
    per = _COPIES_PER_SOURCE[plan]

    def body(*refs):
        src_refs, lands = refs[:n], refs[n:2 * n]
        send_sems, recv_sems = refs[2 * n], refs[2 * n + 1]
        token = refs[-1]
        x, y, c, chips = _place()
        for i in range(n):
            for j, (s, d, to) in enumerate(plan(src_refs[i], lands[i], x, y, c, chips)):
                _ici_copy(s, d, send_sems, recv_sems, per * i + j, to).start()
        token[...] = jnp.zeros_like(token)

    outs = pl.pallas_call(
        body, name=name,
        out_shape=(pltpu.SemaphoreType.DMA((per * n,)), pltpu.SemaphoreType.DMA((per * n,)))
                  + tuple(pltpu.HBM(p.shape, p.dtype) for p in srcs)
                  + tuple(pltpu.HBM(s, BF16) for s in land_shapes)
                  + (jax.ShapeDtypeStruct((SUBLANE, LANE), F32),),
        in_specs=[_HBM] * (2 * n),
        out_specs=tuple([_SEM, _SEM] + [_HBM] * (2 * n) + [pl.BlockSpec(memory_space=pltpu.VMEM)]),
        input_output_aliases={i: 2 + i for i in range(2 * n)},
        compiler_params=pltpu.CompilerParams(has_side_effects=_EFFECT),
    )(*[_hbm(p) for p in srcs], *[_hbm(lax.empty(s, BF16)) for s in land_shapes])
    return (outs[0], outs[1]), list(outs[2:2 + n]), list(outs[2 + n:2 + 2 * n]), outs[-1]


def _split_wait(psums, lands, sems, plan, after, name):
    n = len(psums)
    na = len(after)
    per = _COPIES_PER_SOURCE[plan]

    def body(*refs):
        srcs, lnds = refs[:n], refs[n:2 * n]
        send_sems, recv_sems = refs[2 * n], refs[2 * n + 1]
        x, y, c, chips = _place()
        for i in range(n):
            for j, (s, d, to) in enumerate(plan(srcs[i], lnds[i], x, y, c, chips)):
                cp = _ici_copy(s, d, send_sems, recv_sems, per * i + j, to)
                cp.wait_send()
                cp.wait_recv()

    outs = pl.pallas_call(
        body, name=name,
        out_shape=tuple(pltpu.HBM(a.shape, a.dtype) for a in list(psums) + list(lands)),
        in_specs=[_HBM] * (2 * n) + [_SEM, _SEM] + [_ANY] * na,
        out_specs=tuple([_HBM] * (2 * n)),
        input_output_aliases={i: i for i in range(2 * n)},
        compiler_params=pltpu.CompilerParams(has_side_effects=_EFFECT),
    )(*psums, *lands, sems[0], sems[1], *after)
    return list(outs[:n]), list(outs[n:])


def _layer_half(ref, l, core):
    half = ref.shape[1] // 2
    return ref.at[l, pl.ds(core * half, half)]


def _exchange_grads_start(grads, l, name):
    n = len(grads)

    def body(*refs):
        ins = refs[:n]
        send_sems, recv_sems = refs[n], refs[n + 1]
        token = refs[-1]
        x, y, c, _ = _place()
        for i in range(n):
            mine = _layer_half(ins[i], l, c)
            _ici_copy(mine, mine, send_sems, recv_sems, i, (x, y, 1 - c)).start()
        token[...] = jnp.zeros_like(token)

    outs = pl.pallas_call(
        body, name=name,
        out_shape=(pltpu.SemaphoreType.DMA((n,)), pltpu.SemaphoreType.DMA((n,)))
                  + tuple(pltpu.HBM(g.shape, g.dtype) for g in grads)
                  + (jax.ShapeDtypeStruct((SUBLANE, LANE), F32),),
        in_specs=[_HBM] * n,
        out_specs=tuple([_SEM, _SEM] + [_HBM] * n + [pl.BlockSpec(memory_space=pltpu.VMEM)]),
        input_output_aliases={i: 2 + i for i in range(n)},
        compiler_params=pltpu.CompilerParams(has_side_effects=_EFFECT),
    )(*[_hbm(g) for g in grads])
    return (outs[0], outs[1]), list(outs[2:2 + n]), outs[-1]


def _exchange_grads_wait(grads, l, sems, after, name):
    n = len(grads)

    def body(*refs):
        ins = refs[:n]
        send_sems, recv_sems = refs[n], refs[n + 1]
        x, y, c, _ = _place()
        for i in range(n):
            cp = _ici_copy(_layer_half(ins[i], l, c), _layer_half(ins[i], l, 1 - c), send_sems, recv_sems, i,
                           (x, y, 1 - c))
            cp.wait_send()
            cp.wait_recv()

    outs = pl.pallas_call(
        body, name=name,
        out_shape=tuple(pltpu.HBM(g.shape, g.dtype) for g in grads),
        in_specs=[_HBM] * n + [_SEM, _SEM] + [_ANY] * len(after),
        out_specs=tuple([_HBM] * n),
        input_output_aliases={i: i for i in range(n)},
        compiler_params=pltpu.CompilerParams(has_side_effects=_EFFECT),
    )(*grads, sems[0], sems[1], *after)
    return list(outs)


def _pack(arrs):
    flat = jnp.concatenate([a.reshape(-1).astype(F32) for a in arrs])
    pad = (-flat.shape[0]) % (SUBLANE * LANE)
    return jnp.pad(flat, (0, pad)).reshape(-1, LANE)


def _unpack(slab, shapes):
    flat = slab.reshape(-1)
    out, off = [], 0
    for s in shapes:
        n = math.prod(s)
        out.append(flat[off:off + n].reshape(s))
        off += n
    return out


def _split6(v):
    d = v.shape[-1] // 6
    return [v[:, k * d:(k + 1) * d].reshape(2, 1, d) for k in range(6)]


BIG = ("w_in", "w_br_attn", "w_br_pool", "w_br_sgu", "w_br_conv", "w_gate", "w_o", "w_ff_gate", "w_ff_up", "w_ff_down")
KIND = {"w_in": "col", "w_br_attn": "col", "w_br_pool": "col", "w_br_sgu": "col", "w_br_conv": "col",
        "w_gate": "col", "w_o": "row", "w_ff_gate": "col", "w_ff_up": "col", "w_ff_down": "row"}
SMALL = ("c_ctx", "b_ada", "q_norm_g", "k_norm_g", "pool_w", "pool_scale", "sgu_ln_g", "sgu_ln_b", "sgu_w",
         "sgu_b", "conv_w", "b_gate", "ln1_g", "ln1_b", "ln2_g", "ln2_b")
WEIGHTS = ("c_ctx", "w_ada", "b_ada", "w_in", "q_norm_g", "k_norm_g", "pool_w", "pool_scale", "sgu_ln_g", "sgu_ln_b",
           "sgu_w", "sgu_b", "conv_w", "w_br_attn", "w_br_pool", "w_br_sgu", "w_br_conv", "w_gate", "b_gate", "w_o",
           "ln1_g", "ln1_b", "w_ff_gate", "w_ff_up", "w_ff_down", "ln2_g", "ln2_b")


def _step(x, c, ctx, loss_target, W, M, V):
    L = W["w_ada"].shape[0]
    assert L == 2, "core c of a chip carries layer c of the weight traffic"
    N, D = x.shape[1], x.shape[2]
    NC = ctx.shape[1]
    T = NC + N
    FF = W["w_ff_down"].shape[1] * N_CHIP
    assert NC % ROW_TM == 0 and N % ROW_TM == 0 and N % GRID_W == 0 and D % LANE == 0
    ncb = NC // ROW_TM
    nblk = T // ROW_TM
    alpha = (2 * L) ** 0.25
    ax, ay, ac = lax.axis_index("x"), lax.axis_index("y"), lax.axis_index("c")
    chip = 2 * ax + ay
    dev = 2 * chip + ac
    chip_arr = jnp.reshape(chip, (1,)).astype(jnp.int32)
    core_arr = jnp.reshape(ac, (1,)).astype(jnp.int32)
    ns_ada = W["w_ada"].shape[2]
    chip_devs = (0, 2, 4, 6)

    conv_shape = W["conv_w"].shape
    g0 = _all_gather_small(_pack([c, W["conv_w"]]), "gather_c")
    c_all = g0[:, :D // LANE, :].reshape(N_DEV, D)
    conv_parts = [g0[d].reshape(-1)[D:D + math.prod(conv_shape)].reshape(conv_shape) for d in chip_devs]
    conv_full = jnp.concatenate(conv_parts, axis=-1)
    cvec = jnp.concatenate([c_all, W["c_ctx"][None], jnp.zeros((16 - N_DEV - 1, D), F32)], axis=0)
    b_cols = lax.dynamic_slice_in_dim(W["b_ada"], chip * ns_ada, ns_ada, axis=1).reshape(L, 1, ns_ada)
    mod_part = _ada_fwd(cvec, W["w_ada"], b_cols, "ada_fwd")
    g1 = _all_gather_small(mod_part.reshape(-1, LANE), "gather_mod")
    mod_all = jnp.concatenate([g1[d].reshape(L, 16, ns_ada) for d in chip_devs], axis=-1)
    mod_lat = lax.dynamic_index_in_dim(mod_all, dev, axis=1, keepdims=True)
    mods = jnp.concatenate([mod_all[:, N_DEV:N_DEV + 1], mod_lat], axis=1)

    ffn_keys = ("w_ff_gate", "w_ff_up", "w_ff_down")
    mid_keys = ("w_br_attn", "w_br_pool", "w_br_sgu", "w_br_conv", "w_o")
    groups = [grp for l in range(L) for grp in ([("w_in", l)], [("w_gate", l)], [(k, l) for k in mid_keys],
                                                [("w_ff_gate", l), ("w_ff_up", l)], [("w_ff_down", l)])]
    GPL = 5
    flying, gsems, gtokens = [None] * len(groups), [None] * len(groups), []
    for n_start, gs in enumerate(((0, 1), (2, 3, 4), (5, 6, 7, 8, 9))):
        members = [kl for g in gs for kl in groups[g]]
        casts = [_cast_into(chip_arr, W[k], l, "cast_%s_l%d" % (k, l), after=gtokens[-1:] or [g1])
                 for k, l in members]
        sems, bufs, token = _gather_start(casts, [len(groups[g]) for g in gs], "gather_start_%d" % n_start)
        gtokens.append(token)
        lo = 0
        for g, sem in zip(gs, sems):
            flying[g], gsems[g] = bufs[lo:lo + len(groups[g])], sem
            lo += len(groups[g])
    full = {k: [None] * L for k in BIG}

    def weights_ready(g, after):
        got = _gather_wait(flying[g], gsems[g], after, "gather_wait_%d" % g)
        for (k, l), buf in zip(groups[g], _exchange_halves(got, "exchange_halves_%d" % g)):
            full[k][l] = buf

    cos, sin = _rope_tables(N, NC)
    x0 = jnp.concatenate([ctx[0], x[0]], axis=0)

    saved = []
    xin = x0
    h1 = None
    dy_top = loss_parts = None
    for l in range(L):
        sh1, sc1, g1m, sh2, sc2, g2m = _split6(mods[l])
        tag = "_l%d" % l
        if l == 0:
            h1 = _modulate(xin, sc1, sh1, ncb, "modulate" + tag)
        weights_ready(GPL * l, [h1])
        p = _mm_nn(h1, full["w_in"], l, "col", name="mm_in" + tag)
        qg, kg = W["q_norm_g"][l][None], W["k_norm_g"][l][None]
        q, k, v = _qk_fwd(p, cos, sin, qg, kg, "qk_fwd" + tag)
        att = _attn_fwd(q, k, v, NC, "attn_fwd" + tag)
        ps = W["pool_scale"][l][None]
        ypool = _pool_fwd(p, W["pool_w"][l], ps, NC, "pool_fwd" + tag)
        lg, lb = W["sgu_ln_g"][l][None], W["sgu_ln_b"][l][None]
        b_st = W["sgu_b"][l].T
        ysgu = _sgu_fwd(p, lg, lb, W["sgu_w"][l], b_st, "sgu_fwd" + tag)
        yconv = _conv_fwd(p, conv_full[l], NC, "conv_fwd" + tag)
        brs = (att, ypool, ysgu, yconv)
        weights_ready(GPL * l + 1, list(brs))
        gates = _mm_nn(h1, full["w_gate"], l, "col", name="mm_gate" + tag, bias=W["b_gate"][l][None],
                       act=jax.nn.sigmoid, out_dtype=BF16)
        weights_ready(GPL * l + 2, [gates])
        ts = [_mm_nn(b, full[wk], l, "col", name="mm_" + wk + tag, out_dtype=BF16)
              for b, wk in zip(brs, ("w_br_attn", "w_br_pool", "w_br_sgu", "w_br_conv"))]
        mg = _merge_fwd(gates, ts, "merge_fwd" + tag)
        m = _mm_nn(mg, full["w_o"], l, "row", name="mm_o" + tag, tn_pref=1024)
        ln1g, ln1b = W["ln1_g"][l][None], W["ln1_b"][l][None]
        x1, h2 = _resid_ln(xin, m, g1m, ln1g, ln1b, sc2, sh2, ncb, alpha, "resid_ln1" + tag)
        weights_ready(GPL * l + 3, [h2])
        gg, uu, act = _ffn_up(h2, full["w_ff_gate"][l], full["w_ff_up"][l], "ffn_up" + tag)
        weights_ready(GPL * l + 4, [act])
        ff = _mm_nn(act, full["w_ff_down"], l, "row", name="mm_ffd" + tag, tm_pref=384, tn_pref=512)
        ln2g, ln2b = W["ln2_g"][l][None], W["ln2_b"][l][None]
        saved.append(dict(xin=xin, h1=h1, p=p, q=q, k=k, v=v, brs=brs, gates=gates, ts=ts, mg=mg, m=m, x1=x1, h2=h2,
                          gg=gg, uu=uu, act=act, ff=ff))
        if l + 1 < L:
            nsh1, nsc1 = _split6(mods[l + 1])[:2]
            xin, h1 = _resid_ln(x1, ff, g2m, ln2g, ln2b, nsc1, nsh1, ncb, alpha, "resid_ln2" + tag)
        else:
            dy_top, loss_parts = _resid_ln_loss(x1, ff, g2m, ln2g, ln2b, loss_target[0], ncb, alpha,
                                                "resid_ln2_loss" + tag)
    loss_dev = jnp.sum(loss_parts[:, 0, 0])

    def lat_ctx(part_rows):
        return jnp.stack([jnp.sum(part_rows[:ncb], axis=0), jnp.sum(part_rows[ncb:], axis=0)])

    dW = {k: [None] * L for k in BIG}
    small = {k: [None] * L for k in ("q_norm_g", "k_norm_g", "pool_w", "pool_scale", "sgu_ln_g", "sgu_ln_b", "sgu_w",
                                     "sgu_b", "conv_w", "b_gate", "ln1_g", "ln1_b", "ln2_g", "ln2_b")}
    dmods = [None] * L
    mix_keys = tuple(k for k in BIG if k not in ffn_keys)
    grads_big = {k: None for k in BIG}
    adam_out = {k: None for k in BIG}

    def reduce_begin(keys, l, tg):
        sems, src, land, token = _split_start([dW[k][l] for k in keys], _prereduce_plan, _prereduce_land,
                                              "prereduce_start_" + tg)
        return dict(keys=keys, l=l, sems=sems, src=src, land=land, token=token, tg=tg)

    def reduce_scatter(st, after):
        dws, recv = _split_wait(st["src"], st["land"], st["sems"], _prereduce_plan, after,
                                "prereduce_wait_" + st["tg"])
        psums = [_pair_sum(core_arr, d, r, "pair_sum_%s_l%d" % (k, st["l"]))
                 for k, d, r in zip(st["keys"], dws, recv)]
        st["sems"], st["src"], st["land"], st["token"] = _split_start(psums, _scatter_plan, _scatter_land,
                                                                      "scatter_start_" + st["tg"])

    def reduce_finish(st, after):
        src, land = _split_wait(st["src"], st["land"], st["sems"], _scatter_plan, after, "scatter_wait_" + st["tg"])
        for k, p_, r in zip(st["keys"], src, land):
            grads_big[k] = _chip_sum(chip_arr, core_arr, p_, r, st["l"], L, grads_big[k],
                                     "chip_sum_%s_l%d" % (k, st["l"]))
        st["sems"], st["fly"], st["token"] = _exchange_grads_start([grads_big[k] for k in st["keys"]], st["l"],
                                                                   "exchange_grads_start_" + st["tg"])

    def reduce_done(st, after):
        done = _exchange_grads_wait(st["fly"], st["l"], st["sems"], after, "exchange_grads_wait_" + st["tg"])
        grads_big.update(zip(st["keys"], done))

    def adam(keys, l, after):
        for k in keys:
            adam_out[k] = _adamw_layer(W[k], grads_big[k], M[k], V[k], l, adam_out[k], after,
                                       "adamw_%s_l%d" % (k, l))

    deferred = []

    def grad_of(key, l, thunk):
        if l == 0:
            deferred.append((key, thunk))
        else:
            dW[key][l] = thunk(())

    dxa, dhs, sc_prev = dy_top, [], None
    red_l1 = red_ffn = None
    for l in reversed(range(L)):
        s = saved[l]
        sh1, sc1, g1m, sh2, sc2, g2m = _split6(mods[l])
        tag = "_l%d" % l
        ln1g, ln1b = W["ln1_g"][l][None], W["ln1_b"][l][None]
        ln2g, ln2b = W["ln2_g"][l][None], W["ln2_b"][l][None]
        dx1a, dff, part2 = _ln_bwd(dxa, dhs, sc_prev if dhs else sc1, s["x1"], s["ff"], g2m, ln2g, ln2b, ncb, alpha,
                                   "ln2_bwd" + tag, after=[] if red_l1 is None else [red_l1["token"]])
        small["ln2_g"][l] = jnp.sum(part2[:, 0], axis=0)
        small["ln2_b"][l] = jnp.sum(part2[:, 1], axis=0)
        dg2 = lat_ctx(part2[:, 2])
        if dhs:
            dmods[l + 1][1], dmods[l + 1][0] = lat_ctx(part2[:, 3]), lat_ctx(part2[:, 4])
        dgg, duu = _ffn_down_bwd(dff, full["w_ff_down"][l], s["gg"], s["uu"], "ffn_down_bwd" + tag)
        if l == 0:
            reduce_scatter(red_l1, [dgg])
        dW["w_ff_down"][l] = _mm_tn(s["act"], dff, "row", nq=N_CHIP, kdim=FF // N_CHIP, ndim=D, name="tn_ffd" + tag,
                                    tk_pref=1408, tn_pref=512, after=[red_l1["token"]] if l == 0 else [])
        dh2a = _mm_nt(dgg, full["w_ff_gate"], l, "col", name="nt_ffg" + tag)
        dh2b = _mm_nt(duu, full["w_ff_up"], l, "col", name="nt_ffu" + tag)
        dW["w_ff_gate"][l] = _mm_tn(s["h2"], dgg, "col", nq=N_CHIP, kdim=D, ndim=FF // N_CHIP, name="tn_ffg" + tag,
                                    tn_pref=1408)
        dW["w_ff_up"][l] = _mm_tn(s["h2"], duu, "col", nq=N_CHIP, kdim=D, ndim=FF // N_CHIP, name="tn_ffu" + tag,
                                  tn_pref=1408)
        ties = []
        if l == 0:
            red_ffn = reduce_begin(ffn_keys, 0, "l0_ffn")
            ties = [red_l1["token"], red_ffn["token"]]
        dx0a, dm, part1 = _ln_bwd(dx1a, [dh2a, dh2b], sc2, s["xin"], s["m"], g1m, ln1g, ln1b, ncb, alpha,
                                  "ln1_bwd" + tag, after=ties)
        small["ln1_g"][l] = jnp.sum(part1[:, 0], axis=0)
        small["ln1_b"][l] = jnp.sum(part1[:, 1], axis=0)
        dg1 = lat_ctx(part1[:, 2])
        dsc2, dsh2 = lat_ctx(part1[:, 3]), lat_ctx(part1[:, 4])
        dmods[l] = [None, None, dg1, dsh2, dsc2, dg2]
        dmg = _mm_nt(dm, full["w_o"], l, "row", name="nt_o" + tag)
        grad_of("w_o", l, lambda after, a=s["mg"], d=dm, tag=tag: _mm_tn(
            a, d, "row", nq=N_CHIP, kdim=D // N_CHIP, ndim=D, name="tn_o" + tag, tn_pref=1024, after=after))
        dpre, dt0, dt1, dt2, dt3, bpart = _merge_bwd(dmg, s["gates"], s["ts"], "merge_bwd" + tag)
        small["b_gate"][l] = jnp.sum(bpart[:, 0], axis=0)
        dh1a = _mm_nt(dpre, full["w_gate"], l, "col", name="nt_gate" + tag)
        grad_of("w_gate", l, lambda after, a=s["h1"], d=dpre, tag=tag: _mm_tn(
            a, d, "col", nq=N_CHIP, kdim=D, ndim=D, name="tn_gate" + tag, tn_pref=1024, after=after))
        dbrs = []
        for b, dt, wk, odt in zip(s["brs"], (dt0, dt1, dt2, dt3), ("w_br_attn", "w_br_pool", "w_br_sgu", "w_br_conv"),
                                  (BF16, F32, F32, F32)):
            dbrs.append(_mm_nt(dt, full[wk], l, "col", name="nt_" + wk + tag, out_dtype=odt))
            grad_of(wk, l, lambda after, a=b, d=dt, wk=wk, tag=tag: _mm_tn(
                a, d, "col", nq=N_CHIP, kdim=a.shape[1], ndim=D // N_CHIP, name="tn_" + wk + tag, after=after))
        qg, kg = W["q_norm_g"][l][None], W["k_norm_g"][l][None]
        dq, dk, dv = _attn_bwd(s["q"], s["k"], s["v"], dbrs[0], NC, "attn_bwd" + tag)
        dp, qkpart = _qk_bwd(s["p"], dq, dk, dv, cos, sin, qg, kg, "qk_bwd" + tag)
        small["q_norm_g"][l] = jnp.sum(qkpart[:, 0], axis=0)
        small["k_norm_g"][l] = jnp.sum(qkpart[:, 1], axis=0)
        ps = W["pool_scale"][l][None]
        dp, small["pool_w"][l], dps = _pool_bwd(s["p"], dbrs[1], W["pool_w"][l], ps, dp, NC, "pool_bwd" + tag)
        small["pool_scale"][l] = dps[0]
        lg, lb = W["sgu_ln_g"][l][None], W["sgu_ln_b"][l][None]
        dp, small["sgu_w"][l], dsacc, dln = _sgu_bwd(s["p"], dbrs[2], lg, lb, W["sgu_w"][l], W["sgu_b"][l].T, dp,
                                                     "sgu_bwd" + tag)
        small["sgu_b"][l] = jnp.sum(dsacc.reshape(GC, N_GROUP, GC), axis=-1).T
        small["sgu_ln_g"][l], small["sgu_ln_b"][l] = dln[0], dln[1]
        dp, small["conv_w"][l] = _conv_bwd(s["p"], dbrs[3], conv_full[l], dp, NC, "conv_bwd" + tag)
        dh1b = _mm_nt(dp, full["w_in"], l, "col", name="nt_in" + tag)
        grad_of("w_in", l, lambda after, a=s["h1"], d=dp, tag=tag: _mm_tn(
            a, d, "col", nq=N_CHIP, kdim=D, ndim=IN_W // N_CHIP, name="tn_in" + tag, tn_pref=1152, after=after))
        dxa, dhs, sc_prev = dx0a, [dh1a, dh1b], sc1
        if l == 1:
            red_l1 = reduce_begin(BIG, 1, "l1")
    dx0, part0 = _mod_bwd(dxa, dhs, sc_prev, x0, ncb, "mod_bwd")
    dmods[0][1], dmods[0][0] = lat_ctx(part0[:, 0]), lat_ctx(part0[:, 1])
    grad_x = dx0[NC:][None]
    dmod = jnp.stack([jnp.concatenate(dmods[l], axis=-1) for l in range(L)])

    small_names = [k for k in SMALL if k not in ("c_ctx", "b_ada")]
    small_arrs = [jnp.stack(small[k]) for k in small_names]
    small_shapes = [a.shape for a in small_arrs]
    slab = _pack([loss_dev.reshape(1), jnp.zeros((LANE - 1,), F32), dmod] + small_arrs)
    g2 = _all_gather_small(slab, "gather_small")
    total = _slab_sum(g2, "slab_sum")
    flat_total = total.reshape(-1)
    loss = flat_total[0]
    nmod = L * 2 * 6 * D
    dmod_sum = flat_total[LANE:LANE + nmod].reshape(L, 2, 6 * D)
    small_grads = dict(zip(small_names, _unpack(flat_total[LANE + nmod:], small_shapes)))
    dmod_lat = g2.reshape(N_DEV, -1)[:, LANE:LANE + nmod].reshape(N_DEV, L, 2, 6 * D)[:, :, 1]
    dm16 = jnp.concatenate([jnp.transpose(dmod_lat, (1, 0, 2)), dmod_sum[:, 0:1],
                            jnp.zeros((L, 16 - N_DEV - 1, 6 * D), F32)], axis=1)
    small_grads["b_ada"] = dmod_sum[:, 0] + dmod_sum[:, 1]
    dm16_cols = lax.dynamic_slice_in_dim(dm16, chip * ns_ada, ns_ada, axis=2)
    grad_w_ada, ds16 = _ada_bwd(cvec, dm16_cols, W["w_ada"], "ada_bwd")
    conv_grad_full = small_grads["conv_w"]
    small_grads["conv_w"] = lax.dynamic_slice_in_dim(conv_grad_full, chip * GC, GC, axis=2)

    reduce_scatter(red_ffn, [g2])
    for key, thunk in deferred:
        dW[key][0] = thunk([red_ffn["token"]])
    red_mix = reduce_begin(mix_keys, 0, "l0_mix")
    reduce_finish(red_l1, [red_ffn["token"], red_mix["token"]])
    reduce_scatter(red_mix, [red_l1["token"]])
    reduce_done(red_l1, [red_mix["token"]])
    adam(BIG, 1, [red_mix["token"]])
    reduce_finish(red_ffn, [adam_out[k][0] for k in BIG])
    delta, new_m, new_v = {}, {}, {}
    delta["w_ada"], new_m["w_ada"], new_v["w_ada"] = _adamw(W["w_ada"], grad_w_ada, M["w_ada"], V["w_ada"],
                                                            "adamw_w_ada", after=[red_ffn["token"]])
    reduce_done(red_ffn, [delta["w_ada"]])
    reduce_finish(red_mix, [delta["w_ada"]])
    adam(ffn_keys, 0, [red_mix["token"]])
    reduce_done(red_mix, [adam_out[k][0] for k in ffn_keys])
    adam(mix_keys, 0, [])
    for k in BIG:
        delta[k], new_m[k], new_v[k] = adam_out[k]

    g3 = _all_gather_small(ds16[N_DEV].reshape(-1, LANE), "gather_dsilu", after=[adam_out[k][0] for k in mix_keys])
    small_grads["c_ctx"] = _cctx_grad(g3, W["c_ctx"].reshape(-1, LANE), "cctx_grad").reshape(D)
    shapes = [W[k].shape for k in SMALL]
    sd, sm, sv = _adamw(_pack([W[k] for k in SMALL]), _pack([small_grads[k] for k in SMALL]),
                        _pack([M[k] for k in SMALL]), _pack([V[k] for k in SMALL]), "adamw_small")
    for k, d_, m_, v_ in zip(SMALL, _unpack(sd, shapes), _unpack(sm, shapes), _unpack(sv, shapes)):
        delta[k], new_m[k], new_v[k] = d_, m_, v_
    grads = dict(grads_big)
    grads["w_ada"] = grad_w_ada
    grads.update(small_grads)
    return (loss, grad_x, *[grads[k] for k in WEIGHTS], *[delta[k] for k in WEIGHTS],
            *[new_m[k] for k in WEIGHTS], *[new_v[k] for k in WEIGHTS])


def kernel(x, c, ctx, c_ctx, w_ada, b_ada, w_in, q_norm_g, k_norm_g, pool_w, pool_scale, sgu_ln_g, sgu_ln_b, sgu_w, sgu_b, conv_w, w_br_attn, w_br_pool, w_br_sgu, w_br_conv, w_gate, b_gate, w_o, ln1_g, ln1_b, w_ff_gate, w_ff_up, w_ff_down, ln2_g, ln2_b, loss_target, m_c_ctx, m_w_ada, m_b_ada, m_w_in, m_q_norm_g, m_k_norm_g, m_pool_w, m_pool_scale, m_sgu_ln_g, m_sgu_ln_b, m_sgu_w, m_sgu_b, m_conv_w, m_w_br_attn, m_w_br_pool, m_w_br_sgu, m_w_br_conv, m_w_gate, m_b_gate, m_w_o, m_ln1_g, m_ln1_b, m_w_ff_gate, m_w_ff_up, m_w_ff_down, m_ln2_g, m_ln2_b, v_c_ctx, v_w_ada, v_b_ada, v_w_in, v_q_norm_g, v_k_norm_g, v_pool_w, v_pool_scale, v_sgu_ln_g, v_sgu_ln_b, v_sgu_w, v_sgu_b, v_conv_w, v_w_br_attn, v_w_br_pool, v_w_br_sgu, v_w_br_conv, v_w_gate, v_b_gate, v_w_o, v_ln1_g, v_ln1_b, v_w_ff_gate, v_w_ff_up, v_w_ff_down, v_ln2_g, v_ln2_b):
    args = locals()
    W = {k: args[k] for k in WEIGHTS}
    M = {k: args["m_" + k] for k in WEIGHTS}
    V = {k: args["v_" + k] for k in WEIGHTS}
    return _step(x, c, ctx, loss_target, W, M, V)
```

```python
import functools
import math

import jax
import jax.numpy as jnp
from jax import lax
from jax.experimental import pallas as pl
from jax.experimental.pallas import tpu as pltpu

F32 = jnp.float32
BF16 = jnp.bfloat16
MESH = pl.DeviceIdType.MESH

N_DEV = 8
N_CHIP = 4
GRID_W = 64
HEAD_DIM = 128
N_HEADS = 8
N_KV = 2
KV_GROUP = N_HEADS // N_KV
Q_W = N_HEADS * HEAD_DIM
KV_W = N_KV * HEAD_DIM
QKV_W = Q_W + 2 * KV_W
GC = 128
N_GROUP = 4
BR_W = N_GROUP * GC
POOL_WINDOWS = (2, 4, 8, 16)
OFF_POOL = QKV_W
OFF_U = OFF_POOL + BR_W
OFF_VG = OFF_U + BR_W
OFF_CB = OFF_VG + BR_W
OFF_CC = OFF_CB + BR_W
OFF_CX = OFF_CC + BR_W
IN_W = OFF_CX + BR_W
Q_BLOCK = 128
ROPE_THETA = 10000.0
LN_EPS = 1e-5
RMS_EPS = 1e-6
ADAM_LR = 0.001
ADAM_B1 = 0.9
ADAM_B2 = 0.999
ADAM_EPS = 1e-08
ADAM_WD = 0.01
ADAM_STEP = 10

ROW_TM = 256
LANE = 128
SUBLANE = 8
VMEM_BIG = 56 * 1024 * 1024
VMEM_MID = 40 * 1024 * 1024
NT_WHOLE_BYTES = 16 * 1024 * 1024


def _cp(sem=None, vmem=VMEM_MID):
    return pltpu.CompilerParams(dimension_semantics=sem, vmem_limit_bytes=vmem)


_ANY = pl.BlockSpec(memory_space=pl.ANY)
_HBM = pl.BlockSpec(memory_space=pltpu.HBM)
_SEM = pl.BlockSpec(memory_space=pltpu.SEMAPHORE)
_EFFECT = pltpu.SideEffectType.DATAFLOW_SIDE_EFFECTING


def _tile(n, pref, unit=LANE, whole_ok=False):
    if n <= pref:
        return n
    best = None
    for t in range(unit, pref + 1, unit):
        if n % t == 0:
            best = t
    if whole_ok and (best is None or best * 4 < pref):
        return n
    assert best is not None, (n, pref, unit)
    return best


def _layer_of(w, l):
    if isinstance(w, (list, tuple)):
        return w[l][:, None], 0
    return w, l


def _mm_nn(a, w, l, kind, *, name, out_dtype=F32, bias=None, act=None, tm_pref=768, tn_pref=1152):
    w, l = _layer_of(w, l)
    T, K = a.shape
    tm = _tile(T, tm_pref, SUBLANE * 2)
    if kind == "col":
        nq, _, kw, ns = w.shape
        assert kw == K
        tn = _tile(ns, tn_pref, whole_ok=True)
        nj = ns // tn
        n_total = nq * ns
        w_spec = pl.BlockSpec((None, None, K, tn), lambda j, i: (j // nj, l, 0, j % nj))
        grid_n = nq * nj
    else:
        nq, _, kc, n_total = w.shape
        assert nq * kc == K
        tn = _tile(n_total, tn_pref)
        w_spec = pl.BlockSpec((nq, None, kc, tn), lambda j, i: (0, l, 0, j))
        grid_n = n_total // tn

    def body(*refs):
        if bias is not None:
            a_ref, w_ref, b_ref, o_ref = refs
        else:
            a_ref, w_ref, o_ref = refs
        wv = w_ref[...]
        if kind == "row":
            wv = wv.reshape(K, tn)
        acc = jnp.dot(a_ref[...], wv, preferred_element_type=F32)
        if bias is not None:
            acc = acc + b_ref[...]
        if act is not None:
            acc = act(acc)
        o_ref[...] = acc.astype(out_dtype)

    in_specs = [pl.BlockSpec((tm, K), lambda j, i: (i, 0)), w_spec]
    args = [a, w]
    if bias is not None:
        in_specs.append(pl.BlockSpec((1, tn), lambda j, i: (0, j)))
        args.append(bias)
    return pl.pallas_call(
        body, grid=(grid_n, T // tm), in_specs=in_specs,
        out_specs=pl.BlockSpec((tm, tn), lambda j, i: (i, j)),
        out_shape=jax.ShapeDtypeStruct((T, n_total), out_dtype),
        name=name, compiler_params=_cp(("parallel", "parallel"), VMEM_BIG),
    )(*args)


def _mm_nt(dc, w, l, kind, *, name, out_dtype=F32, dc_off=0, tm_pref=768, tk_pref=1024):
    w, l = _layer_of(w, l)
    T = dc.shape[0]
    tm = _tile(T, tm_pref, SUBLANE * 2)
    if kind == "col":
        nq, _, K, ns = w.shape
        assert dc_off % ns == 0
        offb = dc_off // ns
        tks = _tile(K, 512)
        whole_bytes = 2 * (tm * nq * ns + nq * tks * ns)
        if dc_off == 0 and dc.shape[1] == nq * ns and whole_bytes <= NT_WHOLE_BYTES:
            def body_whole(dc_ref, w_ref, o_ref):
                acc = None
                for q in range(nq):
                    part = lax.dot_general(dc_ref[:, q * ns:(q + 1) * ns], w_ref[q], (((1,), (1,)), ((), ())),
                                           preferred_element_type=F32)
                    acc = part if acc is None else acc + part
                o_ref[...] = acc.astype(out_dtype)

            return pl.pallas_call(
                body_whole, grid=(T // tm, K // tks),
                in_specs=[pl.BlockSpec((tm, nq * ns), lambda i, kk: (i, 0)),
                          pl.BlockSpec((nq, None, tks, ns), lambda i, kk: (0, l, kk, 0))],
                out_specs=pl.BlockSpec((tm, tks), lambda i, kk: (i, kk)),
                out_shape=jax.ShapeDtypeStruct((T, K), out_dtype),
                name=name, compiler_params=_cp(("parallel", "parallel"), VMEM_BIG),
            )(dc, w)
        tk = _tile(K, tk_pref)

        def body(dc_ref, w_ref, o_ref, acc_ref):
            q = pl.program_id(2)
            part = lax.dot_general(dc_ref[...], w_ref[...], (((1,), (1,)), ((), ())),
                                   preferred_element_type=F32)

            @pl.when(q == 0)
            def _():
                acc_ref[...] = part

            @pl.when(q > 0)
            def _():
                acc_ref[...] += part

            @pl.when(q == nq - 1)
            def _():
                o_ref[...] = acc_ref[...].astype(out_dtype)

        return pl.pallas_call(
            body, grid=(K // tk, T // tm, nq),
            in_specs=[pl.BlockSpec((tm, ns), lambda kk, i, q: (i, offb + q)),
                      pl.BlockSpec((None, None, tk, ns), lambda kk, i, q: (q, l, kk, 0))],
            out_specs=pl.BlockSpec((tm, tk), lambda kk, i, q: (i, kk)),
            out_shape=jax.ShapeDtypeStruct((T, K), out_dtype),
            scratch_shapes=[pltpu.VMEM((tm, tk), F32)],
            name=name, compiler_params=_cp(("parallel", "parallel", "arbitrary"), VMEM_BIG),
        )(dc, w)

    nq, _, kc, n = w.shape
    assert dc_off % n == 0
    offb = dc_off // n
    tk = _tile(kc, tk_pref, whole_ok=True)
    nkk = kc // tk

    def body(dc_ref, w_ref, o_ref):
        o_ref[...] = lax.dot_general(dc_ref[...], w_ref[...], (((1,), (1,)), ((), ())),
                                     preferred_element_type=F32).astype(out_dtype)

    return pl.pallas_call(
        body, grid=(nq * nkk, T // tm),
        in_specs=[pl.BlockSpec((tm, n), lambda j, i: (i, offb)),
                  pl.BlockSpec((None, None, tk, n), lambda j, i: (j // nkk, l, j % nkk, 0))],
        out_specs=pl.BlockSpec((tm, tk), lambda j, i: (i, j)),
        out_shape=jax.ShapeDtypeStruct((T, nq * kc), out_dtype),
        name=name, compiler_params=_cp(("parallel", "parallel"), VMEM_BIG),
    )(dc, w)


def _mm_tn(a, dc, kind, *, nq, kdim, ndim, name, a_off=0, dc_off=0, tk_pref=512, tn_pref=1152, out_dtype=BF16,
           after=()):
    T = a.shape[0]
    assert dc.shape[0] == T
    tk = _tile(kdim, tk_pref, whole_ok=True)
    tn = _tile(ndim, tn_pref, whole_ok=True)
    nkk, njn = kdim // tk, ndim // tn
    assert a_off % tk == 0 and dc_off % tn == 0
    aoffb, doffb = a_off // tk, dc_off // tn
    if kind == "col":
        a_map = lambda q, kk, jn: (0, aoffb + kk)
        d_map = lambda q, kk, jn: (0, doffb + q * njn + jn)
    else:
        a_map = lambda q, kk, jn: (0, aoffb + q * nkk + kk)
        d_map = lambda q, kk, jn: (0, doffb + jn)

    def body(a_ref, d_ref, *rest):
        rest[-1][...] = lax.dot_general(a_ref[...], d_ref[...], (((0,), (0,)), ((), ())),
                                        preferred_element_type=F32).astype(out_dtype)

    return pl.pallas_call(
        body, grid=(nq, nkk, njn),
        in_specs=[pl.BlockSpec((T, tk), a_map), pl.BlockSpec((T, tn), d_map)] + [_ANY] * len(after),
        out_specs=pl.BlockSpec((None, tk, tn), lambda q, kk, jn: (q, kk, jn)),
        out_shape=jax.ShapeDtypeStruct((nq, kdim, ndim), out_dtype),
        name=name, compiler_params=_cp(("parallel", "parallel", "parallel"), VMEM_BIG),
    )(a, dc, *after)


def _row_spec(d):
    return pl.BlockSpec((ROW_TM, d), lambda i: (i, 0))


def _mod_spec(d, ncb):
    return pl.BlockSpec((None, 1, d), lambda i: (jnp.where(i >= ncb, 1, 0), 0, 0))


def _vec_spec(d):
    return pl.BlockSpec((1, d), lambda i: (0, 0))


def _part_spec(d):
    return pl.BlockSpec((None, SUBLANE, d), lambda i: (i, 0, 0))


def _modulate(x, sc, sh, ncb, name):
    T, D = x.shape

    def body(x_ref, sc_ref, sh_ref, o_ref):
        o_ref[...] = (x_ref[...] * (1.0 + sc_ref[...]) + sh_ref[...]).astype(BF16)

    return pl.pallas_call(
        body, grid=(T // ROW_TM,),
        in_specs=[_row_spec(D), _mod_spec(D, ncb), _mod_spec(D, ncb)],
        out_specs=_row_spec(D), out_shape=jax.ShapeDtypeStruct((T, D), BF16),
        name=name, compiler_params=_cp(("parallel",)),
    )(x, sc, sh)


def _ln_stats(r):
    mu = jnp.mean(r, axis=-1, keepdims=True)
    rc = r - mu
    var = jnp.mean(rc * rc, axis=-1, keepdims=True)
    rstd = lax.rsqrt(var + LN_EPS)
    return rc * rstd, rstd


def _resid_ln(x, y, gate, g, b, sc, sh, ncb, alpha, name):
    T, D = x.shape

    def body(x_ref, y_ref, gate_ref, g_ref, b_ref, sc_ref, sh_ref, xo_ref, h_ref):
        xhat, _ = _ln_stats(alpha * x_ref[...] + gate_ref[...] * y_ref[...])
        xo = xhat * g_ref[...] + b_ref[...]
        xo_ref[...] = xo
        h_ref[...] = (xo * (1.0 + sc_ref[...]) + sh_ref[...]).astype(BF16)

    return pl.pallas_call(
        body, grid=(T // ROW_TM,),
        in_specs=[_row_spec(D), _row_spec(D), _mod_spec(D, ncb), _vec_spec(D), _vec_spec(D),
                  _mod_spec(D, ncb), _mod_spec(D, ncb)],
        out_specs=[_row_spec(D), _row_spec(D)],
        out_shape=[jax.ShapeDtypeStruct((T, D), F32), jax.ShapeDtypeStruct((T, D), BF16)],
        name=name, compiler_params=_cp(("parallel",)),
    )(x, y, gate, g, b, sc, sh)


def _resid_ln_loss(x, y, gate, g, b, target, ncb, alpha, name):
    T, D = x.shape
    nblk = T // ROW_TM

    def body(x_ref, y_ref, gate_ref, g_ref, b_ref, t_ref, dy_ref, loss_ref):
        i = pl.program_id(0)
        xhat, _ = _ln_stats(alpha * x_ref[...] + gate_ref[...] * y_ref[...])
        xo = xhat * g_ref[...] + b_ref[...]
        live = (i >= ncb).astype(F32)
        err = (xo - t_ref[...]) * live
        dy_ref[...] = err * (1.0 / D)
        loss_ref[...] = jnp.full((SUBLANE, LANE), 0.5 / D, F32) * jnp.sum(err * err)

    return pl.pallas_call(
        body, grid=(nblk,),
        in_specs=[_row_spec(D), _row_spec(D), _mod_spec(D, ncb), _vec_spec(D), _vec_spec(D),
                  pl.BlockSpec((ROW_TM, D), lambda i: (jnp.maximum(i - ncb, 0), 0))],
        out_specs=[_row_spec(D), _part_spec(LANE)],
        out_shape=[jax.ShapeDtypeStruct((T, D), F32), jax.ShapeDtypeStruct((nblk, SUBLANE, LANE), F32)],
        name=name, compiler_params=_cp(("parallel",)),
    )(x, y, gate, g, b, target)


def _write_parts(part_ref, rows, d):
    for k, r in enumerate(rows):
        part_ref[pl.ds(k, 1), :] = jnp.sum(r, axis=0, keepdims=True)
    if len(rows) < SUBLANE:
        part_ref[pl.ds(len(rows), SUBLANE - len(rows)), :] = jnp.zeros((SUBLANE - len(rows), d), F32)


def _ln_bwd(dxa, dhs, sc, x, y, gate, g, b, ncb, alpha, name, after=None):
    T, D = x.shape
    nblk = T // ROW_TM
    ndh = len(dhs)

    def body(*refs):
        dxa_ref = refs[0]
        dh_refs = refs[1:1 + ndh]
        sc_ref, x_ref, y_ref, gate_ref, g_ref, b_ref = refs[1 + ndh:7 + ndh]
        dx_ref, dy_ref, part_ref = refs[-3:]
        yv = y_ref[...]
        xhat, rstd = _ln_stats(alpha * x_ref[...] + gate_ref[...] * yv)
        dxo = dxa_ref[...]
        rows = []
        if ndh:
            dh = dh_refs[0][...]
            for r in dh_refs[1:]:
                dh = dh + r[...]
            dxo = dxo + dh * (1.0 + sc_ref[...])
            xo = xhat * g_ref[...] + b_ref[...]
            rows = [dh * xo, dh]
        dxhat = dxo * g_ref[...]
        m1 = jnp.mean(dxhat, axis=-1, keepdims=True)
        m2 = jnp.mean(dxhat * xhat, axis=-1, keepdims=True)
        dr = rstd * (dxhat - m1 - xhat * m2)
        dx_ref[...] = alpha * dr
        dy_ref[...] = (gate_ref[...] * dr).astype(BF16)
        _write_parts(part_ref, [dxo * xhat, dxo, dr * yv] + rows, D)

    in_specs = ([_row_spec(D)] * (1 + ndh)
                + [_mod_spec(D, ncb), _row_spec(D), _row_spec(D), _mod_spec(D, ncb), _vec_spec(D), _vec_spec(D)])
    extra = list(after or ())
    return pl.pallas_call(
        body, grid=(nblk,), in_specs=in_specs + [_ANY] * len(extra),
        out_specs=[_row_spec(D), _row_spec(D), _part_spec(D)],
        out_shape=[jax.ShapeDtypeStruct((T, D), F32), jax.ShapeDtypeStruct((T, D), BF16),
                   jax.ShapeDtypeStruct((nblk, SUBLANE, D), F32)],
        name=name, compiler_params=_cp(("parallel",)),
    )(dxa, *dhs, sc, x, y, gate, g, b, *extra)


def _mod_bwd(dxa, dhs, sc, x, ncb, name, after=()):
    T, D = x.shape
    nblk = T // ROW_TM
    ndh = len(dhs)

    def body(*refs):
        dxa_ref = refs[0]
        dh_refs = refs[1:1 + ndh]
        sc_ref, x_ref = refs[1 + ndh:3 + ndh]
        dx_ref, part_ref = refs[-2:]
        dh = dh_refs[0][...]
        for r in dh_refs[1:]:
            dh = dh + r[...]
        dx_ref[...] = dxa_ref[...] + dh * (1.0 + sc_ref[...])
        _write_parts(part_ref, [dh * x_ref[...], dh], D)

    return pl.pallas_call(
        body, grid=(nblk,),
        in_specs=[_row_spec(D)] * (1 + ndh) + [_mod_spec(D, ncb), _row_spec(D)] + [_ANY] * len(after),
        out_specs=[_row_spec(D), _part_spec(D)],
        out_shape=[jax.ShapeDtypeStruct((T, D), F32), jax.ShapeDtypeStruct((nblk, SUBLANE, D), F32)],
        name=name, compiler_params=_cp(("parallel",)),
    )(dxa, *dhs, sc, x, *after)


def _merge_fwd(gates, ts, name):
    T, D = ts[0].shape
    tm = 128

    def body(g_ref, t0, t1, t2, t3, o_ref):
        acc = g_ref[:, 0:D].astype(F32) * t0[...].astype(F32)
        for k, t in enumerate((t1, t2, t3), start=1):
            acc = acc + g_ref[:, k * D:(k + 1) * D].astype(F32) * t[...].astype(F32)
        o_ref[...] = acc.astype(BF16)

    rs = pl.BlockSpec((tm, D), lambda i: (i, 0))
    return pl.pallas_call(
        body, grid=(T // tm,),
        in_specs=[pl.BlockSpec((tm, 4 * D), lambda i: (i, 0)), rs, rs, rs, rs],
        out_specs=rs, out_shape=jax.ShapeDtypeStruct((T, D), BF16),
        name=name, compiler_params=_cp(("parallel",)),
    )(gates, *ts)


def _merge_bwd(dmg, gates, ts, name):
    T, D = dmg.shape
    tm = 128
    nblk = T // tm

    def body(d_ref, g_ref, t0, t1, t2, t3, dpre_ref, dt0, dt1, dt2, dt3, part_ref):
        d = d_ref[...]
        for k, (t, dt) in enumerate(zip((t0, t1, t2, t3), (dt0, dt1, dt2, dt3))):
            gk = g_ref[:, k * D:(k + 1) * D].astype(F32)
            dt[...] = (d * gk).astype(BF16)
            dpre = d * t[...].astype(F32) * gk * (1.0 - gk)
            dpre_ref[:, k * D:(k + 1) * D] = dpre.astype(BF16)
            part_ref[:, k * D:(k + 1) * D] = jnp.sum(dpre, axis=0, keepdims=True)

    rs = pl.BlockSpec((tm, D), lambda i: (i, 0))
    wide = pl.BlockSpec((tm, 4 * D), lambda i: (i, 0))
    return pl.pallas_call(
        body, grid=(nblk,),
        in_specs=[rs, wide, rs, rs, rs, rs],
        out_specs=[wide, rs, rs, rs, rs, pl.BlockSpec((None, 1, 4 * D), lambda i: (i, 0, 0))],
        out_shape=[jax.ShapeDtypeStruct((T, 4 * D), BF16)] + [jax.ShapeDtypeStruct((T, D), BF16)] * 4
                  + [jax.ShapeDtypeStruct((nblk, 1, 4 * D), F32)],
        name=name, compiler_params=_cp(("parallel",)),
    )(dmg, gates, *ts)


def _ffn_up(a, wg, wu, name, tm_pref=384):
    T, K = a.shape
    nq, _, ns = wg.shape
    tm = _tile(T, tm_pref, SUBLANE * 2)

    def body(a_ref, g_ref, u_ref, gg_ref, uu_ref, act_ref):
        av = a_ref[...]
        g = jnp.dot(av, g_ref[...], preferred_element_type=F32)
        u = jnp.dot(av, u_ref[...], preferred_element_type=F32)
        gg_ref[...] = g.astype(BF16)
        uu_ref[...] = u.astype(BF16)
        act_ref[...] = (g * jax.nn.sigmoid(g) * u).astype(BF16)

    ws = pl.BlockSpec((None, K, ns), lambda q, i: (q, 0, 0))
    os = pl.BlockSpec((tm, ns), lambda q, i: (i, q))
    return pl.pallas_call(
        body, grid=(nq, T // tm),
        in_specs=[pl.BlockSpec((tm, K), lambda q, i: (i, 0)), ws, ws], out_specs=[os, os, os],
        out_shape=[jax.ShapeDtypeStruct((T, nq * ns), BF16)] * 3,
        name=name, compiler_params=_cp(("parallel", "parallel"), VMEM_BIG),
    )(a, wg, wu)


def _ffn_down_bwd(dff, wd, gg, uu, name, tm_pref=384):
    T, N = dff.shape
    nq, kc, _ = wd.shape
    tm = _tile(T, tm_pref, SUBLANE * 2)

    def body(d_ref, w_ref, g_ref, u_ref, dg_ref, du_ref):
        da = lax.dot_general(d_ref[...], w_ref[...], (((1,), (1,)), ((), ())), preferred_element_type=F32)
        g = g_ref[...].astype(F32)
        sig = jax.nn.sigmoid(g)
        dg_ref[...] = (da * u_ref[...].astype(F32) * sig * (1.0 + g * (1.0 - sig))).astype(BF16)
        du_ref[...] = (da * g * sig).astype(BF16)

    ts = pl.BlockSpec((tm, kc), lambda q, i: (i, q))
    return pl.pallas_call(
        body, grid=(nq, T // tm),
        in_specs=[pl.BlockSpec((tm, N), lambda q, i: (i, 0)), pl.BlockSpec((None, kc, N), lambda q, i: (q, 0, 0)),
                  ts, ts],
        out_specs=[ts, ts], out_shape=[jax.ShapeDtypeStruct((T, nq * kc), BF16)] * 2,
        name=name, compiler_params=_cp(("parallel", "parallel"), VMEM_BIG),
    )(dff, wd, gg, uu)


def _swap_halves(v):
    lane = lax.broadcasted_iota(jnp.int32, v.shape, 1)
    return jnp.where((lane % 64) < 32, pltpu.roll(v, 96, 1), pltpu.roll(v, 32, 1))


def _rope_tables(n, nc):
    rows = n // GRID_W
    row = jnp.repeat(jnp.arange(rows), GRID_W).astype(F32)
    col = jnp.tile(jnp.arange(GRID_W), rows).astype(F32)
    inv = ROPE_THETA ** (-jnp.arange(0, 64, 2, dtype=F32) / 64)
    ang_r = row[:, None] * inv
    ang_c = col[:, None] * inv
    cos = jnp.concatenate([jnp.cos(ang_r), jnp.cos(ang_r), jnp.cos(ang_c), jnp.cos(ang_c)], axis=-1)
    sin = jnp.concatenate([-jnp.sin(ang_r), jnp.sin(ang_r), -jnp.sin(ang_c), jnp.sin(ang_c)], axis=-1)
    cos = jnp.concatenate([jnp.ones((nc, HEAD_DIM), F32), cos], axis=0)
    sin = jnp.concatenate([jnp.zeros((nc, HEAD_DIM), F32), sin], axis=0)
    return cos, sin


def _qk_fwd(p, cos, sin, qg, kg, name):
    T = p.shape[0]

    def body(p_ref, c_ref, s_ref, qg_ref, kg_ref, q_ref, k_ref, v_ref):
        c, s = c_ref[...], s_ref[...]
        for h in range(N_HEADS + N_KV):
            x = p_ref[:, h * HEAD_DIM:(h + 1) * HEAD_DIM]
            rs = lax.rsqrt(jnp.mean(x * x, axis=-1, keepdims=True) + RMS_EPS)
            gain = qg_ref[...] if h < N_HEADS else kg_ref[...]
            yv = x * rs * gain
            out = yv * c + _swap_halves(yv) * s
            if h < N_HEADS:
                out = out * Q_PRESCALE
            out = out.astype(BF16)
            if h < N_HEADS:
                q_ref[:, h * HEAD_DIM:(h + 1) * HEAD_DIM] = out
            else:
                k_ref[:, (h - N_HEADS) * HEAD_DIM:(h - N_HEADS + 1) * HEAD_DIM] = out
        v_ref[...] = p_ref[:, Q_W + KV_W:QKV_W].astype(BF16)

    return pl.pallas_call(
        body, grid=(T // ROW_TM,),
        in_specs=[_row_spec(QKV_W), _row_spec(HEAD_DIM), _row_spec(HEAD_DIM), _vec_spec(HEAD_DIM), _vec_spec(HEAD_DIM)],
        out_specs=[_row_spec(Q_W), _row_spec(KV_W), _row_spec(KV_W)],
        out_shape=[jax.ShapeDtypeStruct((T, Q_W), BF16), jax.ShapeDtypeStruct((T, KV_W), BF16),
                   jax.ShapeDtypeStruct((T, KV_W), BF16)],
        name=name, compiler_params=_cp(("parallel",)),
    )(p, cos, sin, qg, kg)


def _qk_bwd(p, dq, dk, dv, cos, sin, qg, kg, name):
    T = p.shape[0]
    nblk = T // ROW_TM

    def body(p_ref, dq_ref, dk_ref, dv_ref, c_ref, s_ref, qg_ref, kg_ref, dp_ref, part_ref):
        c, s = c_ref[...], s_ref[...]
        dgq = jnp.zeros((1, HEAD_DIM), F32)
        dgk = jnp.zeros((1, HEAD_DIM), F32)
        for h in range(N_HEADS + N_KV):
            x = p_ref[:, h * HEAD_DIM:(h + 1) * HEAD_DIM]
            if h < N_HEADS:
                d = dq_ref[:, h * HEAD_DIM:(h + 1) * HEAD_DIM]
                gain = qg_ref[...]
            else:
                d = dk_ref[:, (h - N_HEADS) * HEAD_DIM:(h - N_HEADS + 1) * HEAD_DIM]
                gain = kg_ref[...]
            dyv = d * c + _swap_halves(d * s)
            rs = lax.rsqrt(jnp.mean(x * x, axis=-1, keepdims=True) + RMS_EPS)
            xn = x * rs
            dgsum = jnp.sum(dyv * xn, axis=0, keepdims=True)
            if h < N_HEADS:
                dgq = dgq + dgsum
            else:
                dgk = dgk + dgsum
            dxg = dyv * gain
            dx = rs * (dxg - xn * jnp.mean(dxg * xn, axis=-1, keepdims=True))
            dp_ref[:, h * HEAD_DIM:(h + 1) * HEAD_DIM] = dx.astype(BF16)
        dp_ref[:, Q_W + KV_W:QKV_W] = dv_ref[...].astype(BF16)
        part_ref[pl.ds(0, 1), :] = dgq
        part_ref[pl.ds(1, 1), :] = dgk
        part_ref[pl.ds(2, SUBLANE - 2), :] = jnp.zeros((SUBLANE - 2, HEAD_DIM), F32)

    return pl.pallas_call(
        body, grid=(nblk,),
        in_specs=[_row_spec(QKV_W), _row_spec(Q_W), _row_spec(KV_W), _row_spec(KV_W),
                  _row_spec(HEAD_DIM), _row_spec(HEAD_DIM), _vec_spec(HEAD_DIM), _vec_spec(HEAD_DIM)],
        out_specs=[_row_spec(QKV_W), _part_spec(HEAD_DIM)],
        out_shape=[jax.ShapeDtypeStruct((T, IN_W), BF16), jax.ShapeDtypeStruct((nblk, SUBLANE, HEAD_DIM), F32)],
        name=name, compiler_params=_cp(("parallel",)),
    )(p, dq, dk, dv, cos, sin, qg, kg)


ATTN_SCALE = HEAD_DIM ** -0.5
LOG2E = 1.4426950408889634
Q_PRESCALE = ATTN_SCALE * LOG2E


def _attn_weights(q, k):
    s = lax.dot_general(q, k, (((1,), (1,)), ((), ())), preferred_element_type=F32)
    e = jnp.exp2(s - jnp.max(s, axis=-1, keepdims=True))
    return e, 1.0 / jnp.sum(e, axis=-1, keepdims=True)


def _attn_fwd(q, k, v, nc, name):
    T = q.shape[0]

    def heads(q_ref, k_ref, v_ref, o_ref, nkeys):
        for h in range(N_HEADS):
            g = h // KV_GROUP
            kk = k_ref[0:nkeys, g * HEAD_DIM:(g + 1) * HEAD_DIM]
            vv = v_ref[0:nkeys, g * HEAD_DIM:(g + 1) * HEAD_DIM]
            e, rl = _attn_weights(q_ref[:, h * HEAD_DIM:(h + 1) * HEAD_DIM], kk)
            o = jnp.dot(e.astype(BF16), vv, preferred_element_type=F32) * rl
            o_ref[:, h * HEAD_DIM:(h + 1) * HEAD_DIM] = o.astype(BF16)

    def body(q_ref, k_ref, v_ref, o_ref):
        i = pl.program_id(0)

        @pl.when(i < nc // Q_BLOCK)
        def _():
            heads(q_ref, k_ref, v_ref, o_ref, nc)

        @pl.when(i >= nc // Q_BLOCK)
        def _():
            heads(q_ref, k_ref, v_ref, o_ref, T)

    whole = pl.BlockSpec((T, KV_W), lambda i: (0, 0))
    qs = pl.BlockSpec((Q_BLOCK, Q_W), lambda i: (i, 0))
    return pl.pallas_call(
        body, grid=(T // Q_BLOCK,), in_specs=[qs, whole, whole], out_specs=qs,
        out_shape=jax.ShapeDtypeStruct((T, Q_W), BF16),
        name=name, compiler_params=_cp(("parallel",)),
    )(q, k, v)


def _attn_bwd(q, k, v, do, nc, name):
    T = q.shape[0]

    def heads(q_ref, k_ref, v_ref, do_ref, dq_ref, dk_ref, dv_ref, nkeys):
        for h in range(N_HEADS):
            g = h // KV_GROUP
            cols = slice(g * HEAD_DIM, (g + 1) * HEAD_DIM)
            hc = slice(h * HEAD_DIM, (h + 1) * HEAD_DIM)
            qh, kk, vv, doh = q_ref[:, hc], k_ref[0:nkeys, cols], v_ref[0:nkeys, cols], do_ref[:, hc]
            e, rl = _attn_weights(qh, kk)
            dpr = lax.dot_general(doh, vv, (((1,), (1,)), ((), ())), preferred_element_type=F32)
            delta = jnp.sum(e * dpr, axis=-1, keepdims=True) * rl
            dsu = (e * (dpr - delta)).astype(BF16)
            dq_ref[:, hc] = jnp.dot(dsu, kk, preferred_element_type=F32) * (rl * ATTN_SCALE)
            q_rows = (qh.astype(F32) * (rl * (1.0 / LOG2E))).astype(BF16)
            dk_ref[0:nkeys, cols] += lax.dot_general(dsu, q_rows, (((0,), (0,)), ((), ())),
                                                     preferred_element_type=F32)
            do_rows = (doh.astype(F32) * rl).astype(BF16)
            dv_ref[0:nkeys, cols] += lax.dot_general(e.astype(BF16), do_rows, (((0,), (0,)), ((), ())),
                                                     preferred_element_type=F32)

    def body(q_ref, k_ref, v_ref, do_ref, dq_ref, dk_ref, dv_ref):
        i = pl.program_id(0)

        @pl.when(i == 0)
        def _():
            dk_ref[...] = jnp.zeros_like(dk_ref)
            dv_ref[...] = jnp.zeros_like(dv_ref)

        @pl.when(i < nc // Q_BLOCK)
        def _():
            heads(q_ref, k_ref, v_ref, do_ref, dq_ref, dk_ref, dv_ref, nc)

        @pl.when(i >= nc // Q_BLOCK)
        def _():
            heads(q_ref, k_ref, v_ref, do_ref, dq_ref, dk_ref, dv_ref, T)

    whole = pl.BlockSpec((T, KV_W), lambda i: (0, 0))
    qs = pl.BlockSpec((Q_BLOCK, Q_W), lambda i: (i, 0))
    return pl.pallas_call(
        body, grid=(T // Q_BLOCK,), in_specs=[qs, whole, whole, qs], out_specs=[qs, whole, whole],
        out_shape=[jax.ShapeDtypeStruct((T, Q_W), F32), jax.ShapeDtypeStruct((T, KV_W), F32),
                   jax.ShapeDtypeStruct((T, KV_W), F32)],
        name=name, compiler_params=_cp(("arbitrary",)),
    )(q, k, v, do)


def _shift_rows(z, o, nc):
    T = z.shape[0]
    t = lax.broadcasted_iota(jnp.int32, (T, 1), 0)
    lo = jnp.where(t < nc, 0, nc)
    hi = jnp.where(t < nc, nc, T)
    ok = jnp.logical_and(t + o >= lo, t + o < hi)
    rolled = z if o == 0 else pltpu.roll(z, (-o) % T, 0)
    return jnp.where(ok, rolled, 0.0), ok


def _pool_window(w):
    left = w // 2
    return -left, w - 1 - left


def _pool_d(z, w, nc):
    o0, o1 = _pool_window(w)
    tot = jnp.zeros_like(z)
    cnt = jnp.zeros((z.shape[0], 1), F32)
    for o in range(o0, o1 + 1):
        sh, ok = _shift_rows(z, o, nc)
        tot = tot + sh
        cnt = cnt + ok.astype(F32)
    return tot / cnt - z, cnt


def _pool_fwd(p, pool_w, pool_scale, nc, name):
    T = p.shape[0]

    def body(z_ref, w_ref, s_ref, o_ref):
        for g, w in enumerate(POOL_WINDOWS):
            cs = slice(g * GC, (g + 1) * GC)
            d, _ = _pool_d(z_ref[:, cs], w, nc)
            yv = jnp.dot(d.astype(BF16), w_ref[g].astype(BF16), preferred_element_type=F32)
            o_ref[:, cs] = (yv * s_ref[:, cs]).astype(BF16)

    return pl.pallas_call(
        body, grid=(1,),
        in_specs=[pl.BlockSpec((T, BR_W), lambda i: (0, OFF_POOL // BR_W)),
                  pl.BlockSpec((N_GROUP, GC, GC), lambda i: (0, 0, 0)), _vec_spec(BR_W)],
        out_specs=pl.BlockSpec((T, BR_W), lambda i: (0, 0)),
        out_shape=jax.ShapeDtypeStruct((T, BR_W), BF16),
        name=name, compiler_params=_cp(("arbitrary",), VMEM_BIG),
    )(p, pool_w, pool_scale)


def _pool_bwd(p, dy, pool_w, pool_scale, dp, nc, name):
    T = p.shape[0]

    def body(z_ref, dy_ref, w_ref, s_ref, dp_in_ref, dz_ref, dw_ref, ds_ref):
        for g, w in enumerate(POOL_WINDOWS):
            cs = slice(g * GC, (g + 1) * GC)
            d, cnt = _pool_d(z_ref[:, cs], w, nc)
            db = d.astype(BF16)
            wb = w_ref[g].astype(BF16)
            dyv = dy_ref[:, cs]
            e = (dyv * s_ref[:, cs]).astype(BF16)
            dw_ref[g] = lax.dot_general(db, e, (((0,), (0,)), ((), ())), preferred_element_type=F32)
            yraw = jnp.dot(db, wb, preferred_element_type=F32)
            ds_ref[:, cs] = jnp.sum(dyv * yraw, axis=0, keepdims=True)
            dd = lax.dot_general(e, wb, (((1,), (1,)), ((), ())), preferred_element_type=F32)
            ec = dd / cnt
            o0, o1 = _pool_window(w)
            tot = jnp.zeros_like(dd)
            for o in range(o0, o1 + 1):
                tot = tot + _shift_rows(ec, -o, nc)[0]
            dz_ref[:, cs] = (tot - dd).astype(BF16)

    return pl.pallas_call(
        body, grid=(1,),
        in_specs=[pl.BlockSpec((T, BR_W), lambda i: (0, OFF_POOL // BR_W)),
                  pl.BlockSpec((T, BR_W), lambda i: (0, 0)),
                  pl.BlockSpec((N_GROUP, GC, GC), lambda i: (0, 0, 0)), _vec_spec(BR_W), _ANY],
        out_specs=[pl.BlockSpec((T, BR_W), lambda i: (0, OFF_POOL // BR_W)),
                   pl.BlockSpec((N_GROUP, GC, GC), lambda i: (0, 0, 0)), _vec_spec(BR_W)],
        out_shape=[jax.ShapeDtypeStruct(dp.shape, BF16), jax.ShapeDtypeStruct((N_GROUP, GC, GC), F32),
                   jax.ShapeDtypeStruct((1, BR_W), F32)],
        input_output_aliases={4: 0},
        name=name, compiler_params=_cp(("arbitrary",), VMEM_BIG),
    )(p, dy, pool_w, pool_scale, dp)


_GELU_C = math.sqrt(2.0 / math.pi)


def _gelu(x):
    return 0.5 * x * (1.0 + jnp.tanh(_GELU_C * (x + 0.044715 * x * x * x)))


def _gelu_grad(x):
    th = jnp.tanh(_GELU_C * (x + 0.044715 * x * x * x))
    return 0.5 * (1.0 + th) + 0.5 * x * (1.0 - th * th) * _GELU_C * (1.0 + 3.0 * 0.044715 * x * x)


def _sgu_fwd(p, ln_g, ln_b, w_s, b_st, name):
    T = p.shape[0]

    def body(pu_ref, pv_ref, g_ref, b_ref, w_ref, bs_ref, o_ref):
        u = _gelu(pu_ref[...])
        vhat, _ = _ln_stats(_gelu(pv_ref[...]))
        vn = (vhat * g_ref[...] + b_ref[...]).astype(BF16)
        for g in range(N_GROUP):
            cs = slice(g * GC, (g + 1) * GC)
            s = jnp.dot(w_ref[g].astype(BF16), vn[:, cs], preferred_element_type=F32) + bs_ref[:, g:g + 1]
            o_ref[:, cs] = (u[:, cs] * s).astype(BF16)

    chunk = lambda off: pl.BlockSpec((GC, BR_W), lambda i: (i, off // BR_W))
    return pl.pallas_call(
        body, grid=(T // GC,),
        in_specs=[chunk(OFF_U), chunk(OFF_VG), _vec_spec(BR_W), _vec_spec(BR_W),
                  pl.BlockSpec((N_GROUP, GC, GC), lambda i: (0, 0, 0)),
                  pl.BlockSpec((GC, N_GROUP), lambda i: (0, 0))],
        out_specs=pl.BlockSpec((GC, BR_W), lambda i: (i, 0)),
        out_shape=jax.ShapeDtypeStruct((T, BR_W), BF16),
        name=name, compiler_params=_cp(("parallel",)),
    )(p, p, ln_g, ln_b, w_s, b_st)


def _sgu_bwd(p, dy, ln_g, ln_b, w_s, b_st, dp, name):
    T = p.shape[0]

    def body(pu_ref, pv_ref, dy_ref, g_ref, b_ref, w_ref, bs_ref, dp_in_ref, dp_ref, dw_ref, dsacc_ref, dln_ref):
        i = pl.program_id(0)

        @pl.when(i == 0)
        def _():
            dw_ref[...] = jnp.zeros_like(dw_ref)
            dsacc_ref[...] = jnp.zeros_like(dsacc_ref)
            dln_ref[...] = jnp.zeros_like(dln_ref)

        pu, pv, dyv = pu_ref[...], pv_ref[...], dy_ref[...]
        u = _gelu(pu)
        vhat, rstd = _ln_stats(_gelu(pv))
        vn = (vhat * g_ref[...] + b_ref[...]).astype(BF16)
        ds = dyv * u
        dsb = ds.astype(BF16)
        dsacc_ref[...] += ds
        dvn_parts = []
        for g in range(N_GROUP):
            cs = slice(g * GC, (g + 1) * GC)
            wb = w_ref[g].astype(BF16)
            s = jnp.dot(wb, vn[:, cs], preferred_element_type=F32) + bs_ref[:, g:g + 1]
            dp_ref[:, cs] = (dyv[:, cs] * s * _gelu_grad(pu[:, cs])).astype(BF16)
            dvn_parts.append(lax.dot_general(wb, dsb[:, cs], (((0,), (0,)), ((), ())),
                                             preferred_element_type=F32))
            dw_ref[g] += lax.dot_general(dsb[:, cs], vn[:, cs], (((1,), (1,)), ((), ())),
                                         preferred_element_type=F32)
        dvn = jnp.concatenate(dvn_parts, axis=-1)
        dln_ref[pl.ds(0, 1), :] += jnp.sum(dvn * vhat, axis=0, keepdims=True)
        dln_ref[pl.ds(1, 1), :] += jnp.sum(dvn, axis=0, keepdims=True)
        dvhat = dvn * g_ref[...]
        m1 = jnp.mean(dvhat, axis=-1, keepdims=True)
        m2 = jnp.mean(dvhat * vhat, axis=-1, keepdims=True)
        dv = rstd * (dvhat - m1 - vhat * m2)
        dp_ref[:, BR_W:2 * BR_W] = (dv * _gelu_grad(pv)).astype(BF16)

    chunk = lambda off: pl.BlockSpec((GC, BR_W), lambda i: (i, off // BR_W))
    return pl.pallas_call(
        body, grid=(T // GC,),
        in_specs=[chunk(OFF_U), chunk(OFF_VG), pl.BlockSpec((GC, BR_W), lambda i: (i, 0)),
                  _vec_spec(BR_W), _vec_spec(BR_W),
                  pl.BlockSpec((N_GROUP, GC, GC), lambda i: (0, 0, 0)),
                  pl.BlockSpec((GC, N_GROUP), lambda i: (0, 0)), _ANY],
        out_specs=[pl.BlockSpec((GC, 2 * BR_W), lambda i: (i, OFF_U // (2 * BR_W))),
                   pl.BlockSpec((N_GROUP, GC, GC), lambda i: (0, 0, 0)),
                   pl.BlockSpec((GC, BR_W), lambda i: (0, 0)),
                   pl.BlockSpec((SUBLANE, BR_W), lambda i: (0, 0))],
        out_shape=[jax.ShapeDtypeStruct(dp.shape, BF16), jax.ShapeDtypeStruct((N_GROUP, GC, GC), F32),
                   jax.ShapeDtypeStruct((GC, BR_W), F32), jax.ShapeDtypeStruct((SUBLANE, BR_W), F32)],
        input_output_aliases={7: 0},
        name=name, compiler_params=_cp(("arbitrary",)),
    )(p, p, dy, ln_g, ln_b, w_s, b_st, dp)


def _conv_fwd(p, conv_w, nc, name):
    T = p.shape[0]

    def body(cb_ref, cc_ref, cx_ref, w_ref, o_ref):
        zz = cc_ref[...] * cx_ref[...]
        conv = (w_ref[0:1, :] * _shift_rows(zz, -1, nc)[0] + w_ref[1:2, :] * zz
                + w_ref[2:3, :] * _shift_rows(zz, 1, nc)[0])
        o_ref[...] = (cb_ref[...] * conv).astype(BF16)

    col = lambda off: pl.BlockSpec((T, GC), lambda j: (0, off // GC + j))
    return pl.pallas_call(
        body, grid=(N_GROUP,),
        in_specs=[col(OFF_CB), col(OFF_CC), col(OFF_CX), pl.BlockSpec((3, GC), lambda j: (0, j))],
        out_specs=pl.BlockSpec((T, GC), lambda j: (0, j)),
        out_shape=jax.ShapeDtypeStruct((T, BR_W), BF16),
        name=name, compiler_params=_cp(("parallel",)),
    )(p, p, p, conv_w)


def _conv_bwd(p, dy, conv_w, dp, nc, name):
    T = p.shape[0]

    def body(cb_ref, cc_ref, cx_ref, dy_ref, w_ref, dp_in_ref, dp_ref, dw_ref):
        part = pl.program_id(1)
        cc, cx = cc_ref[...], cx_ref[...]
        dconv = dy_ref[...] * cb_ref[...]

        @pl.when(part == 0)
        def _():
            zz = cc * cx
            zm, zp = _shift_rows(zz, -1, nc)[0], _shift_rows(zz, 1, nc)[0]
            conv = w_ref[0:1, :] * zm + w_ref[1:2, :] * zz + w_ref[2:3, :] * zp
            dw_ref[0:1, :] = jnp.sum(dconv * zm, axis=0, keepdims=True)
            dw_ref[1:2, :] = jnp.sum(dconv * zz, axis=0, keepdims=True)
            dw_ref[2:3, :] = jnp.sum(dconv * zp, axis=0, keepdims=True)
            dp_ref[...] = (dy_ref[...] * conv).astype(BF16)

        @pl.when(part > 0)
        def _():
            dzz = (w_ref[0:1, :] * _shift_rows(dconv, 1, nc)[0] + w_ref[1:2, :] * dconv
                   + w_ref[2:3, :] * _shift_rows(dconv, -1, nc)[0])
            dp_ref[...] = (dzz * jnp.where(part == 1, cx, cc)).astype(BF16)

    col = lambda off: pl.BlockSpec((T, GC), lambda j, part: (0, off // GC + j))
    return pl.pallas_call(
        body, grid=(N_GROUP, 3),
        in_specs=[col(OFF_CB), col(OFF_CC), col(OFF_CX), pl.BlockSpec((T, GC), lambda j, part: (0, j)),
                  pl.BlockSpec((3, GC), lambda j, part: (0, j)), _ANY],
        out_specs=[pl.BlockSpec((T, GC), lambda j, part: (0, OFF_CB // GC + part * N_GROUP + j)),
                   pl.BlockSpec((3, GC), lambda j, part: (0, j))],
        out_shape=[jax.ShapeDtypeStruct(dp.shape, BF16), jax.ShapeDtypeStruct((3, BR_W), F32)],
        input_output_aliases={5: 0},
        name=name, compiler_params=_cp(("parallel", "arbitrary")),
    )(p, p, p, dy, conv_w, dp)


def _rows_tile(rows, cols, n_arrays, vmem_bytes=24 * 1024 * 1024):
    budget = vmem_bytes // (2 * 4 * n_arrays * cols)
    return _tile(rows, max(SUBLANE * 2, min(budget, 1024)), SUBLANE * 2) if rows % (SUBLANE * 2) == 0 else rows


def _cast_into(chip, w, l, name, after=()):
    _, K, cols = w.shape
    tr = _rows_tile(K, cols, 2)

    def body(q_ref, w_ref, *rest):
        rest[-1][...] = w_ref[...].astype(BF16)

    return pl.pallas_call(
        body,
        grid_spec=pltpu.PrefetchScalarGridSpec(
            num_scalar_prefetch=1, grid=(K // tr,),
            in_specs=[pl.BlockSpec((None, tr, cols), lambda i, q: (l, i, 0))] + [_ANY] * len(after),
            out_specs=pl.BlockSpec((None, tr, cols), lambda i, q: (q[0], i, 0))),
        out_shape=jax.ShapeDtypeStruct((N_CHIP, K, cols), BF16),
        name=name, compiler_params=_cp(("parallel",)),
    )(chip, w, *after)


ADAM_VMEM = 34 * 1024 * 1024


def _adamw(w, g, m, v, name, after=()):
    shape = w.shape
    cols = shape[-1]
    rows = w.size // cols
    tr = _rows_tile(rows, cols, 7, ADAM_VMEM)
    c1 = 1.0 - ADAM_B1 ** ADAM_STEP
    c2 = 1.0 - ADAM_B2 ** ADAM_STEP

    def body(w_ref, g_ref, m_ref, v_ref, *rest):
        d_ref, mo_ref, vo_ref = rest[-3:]
        gv = g_ref[...]
        mn = ADAM_B1 * m_ref[...] + (1.0 - ADAM_B1) * gv
        vn = ADAM_B2 * v_ref[...] + (1.0 - ADAM_B2) * (gv * gv)
        mo_ref[...] = mn
        vo_ref[...] = vn
        d_ref[...] = -ADAM_LR * ((mn / c1) / (jnp.sqrt(vn / c2) + ADAM_EPS) + ADAM_WD * w_ref[...])

    rs = pl.BlockSpec((tr, cols), lambda i: (i, 0))
    outs = pl.pallas_call(
        body, grid=(rows // tr,), in_specs=[rs] * 4 + [_ANY] * len(after), out_specs=[rs] * 3,
        out_shape=[jax.ShapeDtypeStruct((rows, cols), F32)] * 3,
        name=name, compiler_params=_cp(("parallel",)),
    )(*[a.reshape(rows, cols) for a in (w, g, m, v)], *after)
    return [o.reshape(shape) for o in outs]


def _adamw_layer(w, g, m, v, l, prev, after, name):
    L, K, cols = w.shape
    tr = _rows_tile(K, cols, 7, ADAM_VMEM)
    c1 = 1.0 - ADAM_B1 ** ADAM_STEP
    c2 = 1.0 - ADAM_B2 ** ADAM_STEP
    nprev = 0 if prev is None else 3

    def body(*refs):
        w_ref, g_ref, m_ref, v_ref = refs[:4]
        d_ref, mo_ref, vo_ref = refs[-3:]
        gv = g_ref[...]
        mn = ADAM_B1 * m_ref[...] + (1.0 - ADAM_B1) * gv
        vn = ADAM_B2 * v_ref[...] + (1.0 - ADAM_B2) * (gv * gv)
        mo_ref[...] = mn
        vo_ref[...] = vn
        d_ref[...] = -ADAM_LR * ((mn / c1) / (jnp.sqrt(vn / c2) + ADAM_EPS) + ADAM_WD * w_ref[...])

    rs = pl.BlockSpec((None, tr, cols), lambda i: (l, i, 0))
    extra = list(prev or ()) + list(after)
    return pl.pallas_call(
        body, grid=(K // tr,), in_specs=[rs] * 4 + [_ANY] * len(extra), out_specs=[rs] * 3,
        out_shape=[jax.ShapeDtypeStruct((L, K, cols), F32)] * 3,
        input_output_aliases={4 + k: k for k in range(nprev)},
        name=name, compiler_params=_cp(("parallel",)),
    )(w, g, m, v, *extra)


def _pair_sum(core, dw, recv, name):
    nq, half, cols = recv.shape
    tr = _rows_tile(half, cols, 3)
    nb = half // tr

    def body(c_ref, a_ref, r_ref, o_ref):
        o_ref[...] = (a_ref[...].astype(F32) + r_ref[...].astype(F32)).astype(BF16)

    rs = pl.BlockSpec((None, tr, cols), lambda q, i, c: (q, i, 0))
    return pl.pallas_call(
        body,
        grid_spec=pltpu.PrefetchScalarGridSpec(
            num_scalar_prefetch=1, grid=(nq, nb),
            in_specs=[pl.BlockSpec((None, tr, cols), lambda q, i, c: (q, c[0] * nb + i, 0)), rs],
            out_specs=rs),
        out_shape=jax.ShapeDtypeStruct((nq, half, cols), BF16),
        name=name, compiler_params=_cp(("parallel", "parallel")),
    )(core, dw, recv)


def _chip_sum(chip, core, psum, land, l, n_layers, prev, name):
    _, half, cols = psum.shape
    tr = _rows_tile(half, cols, 5)
    nb = half // tr

    def body(*refs):
        p_ref, r_ref, o_ref = refs[2], refs[3], refs[-1]
        o_ref[...] = (p_ref[...].astype(F32) + r_ref[0].astype(F32) + r_ref[1].astype(F32)
                      + r_ref[2].astype(F32))

    extra = [] if prev is None else [prev]
    return pl.pallas_call(
        body,
        grid_spec=pltpu.PrefetchScalarGridSpec(
            num_scalar_prefetch=2, grid=(nb,),
            in_specs=[pl.BlockSpec((None, tr, cols), lambda i, q, c: (q[0], i, 0)),
                      pl.BlockSpec((3, tr, cols), lambda i, q, c: (0, i, 0))] + [_ANY] * len(extra),
            out_specs=pl.BlockSpec((None, tr, cols), lambda i, q, c: (l, c[0] * nb + i, 0))),
        out_shape=jax.ShapeDtypeStruct((n_layers, 2 * half, cols), F32),
        input_output_aliases={4: 0} if prev is not None else {},
        name=name, compiler_params=_cp(("parallel",)),
    )(chip, core, psum, land, *extra)


def _slab_sum(gathered, name, after=()):
    n, rows, cols = gathered.shape
    tr = _tile(rows, 512, SUBLANE)

    def body(g_ref, *rest):
        acc = g_ref[0]
        for d in range(1, n):
            acc = acc + g_ref[d]
        rest[-1][...] = acc

    return pl.pallas_call(
        body, grid=(rows // tr,),
        in_specs=[pl.BlockSpec((n, tr, cols), lambda i: (0, i, 0))] + [_ANY] * len(after),
        out_specs=pl.BlockSpec((tr, cols), lambda i: (i, 0)),
        out_shape=jax.ShapeDtypeStruct((rows, cols), F32),
        name=name, compiler_params=_cp(("parallel",)),
    )(gathered, *after)


def _silu(x):
    return x * jax.nn.sigmoid(x)


def _ada_fwd(cvec, w_ada, b_cols, name):
    L, D, ns = w_ada.shape
    tn = _tile(ns, 768)

    def body(c_ref, w_ref, b_ref, o_ref):
        s = _silu(c_ref[...]).astype(BF16)
        o_ref[...] = jnp.dot(s, w_ref[...].astype(BF16), preferred_element_type=F32) + b_ref[...]

    return pl.pallas_call(
        body, grid=(L, ns // tn),
        in_specs=[pl.BlockSpec((16, D), lambda l, j: (0, 0)),
                  pl.BlockSpec((None, D, tn), lambda l, j: (l, 0, j)),
                  pl.BlockSpec((None, 1, tn), lambda l, j: (l, 0, j))],
        out_specs=pl.BlockSpec((None, 16, tn), lambda l, j: (l, 0, j)),
        out_shape=jax.ShapeDtypeStruct((L, 16, ns), F32),
        name=name, compiler_params=_cp(("parallel", "parallel")),
    )(cvec, w_ada, b_cols)


def _ada_bwd(cvec, dmod, w_ada, name):
    L, D, ns = w_ada.shape
    tn = _tile(ns, 768)

    def body(c_ref, d_ref, w_ref, gw_ref, ds_ref):
        first = jnp.logical_and(pl.program_id(0) == 0, pl.program_id(1) == 0)

        @pl.when(first)
        def _():
            ds_ref[...] = jnp.zeros_like(ds_ref)

        s = _silu(c_ref[...]).astype(BF16)
        db = d_ref[...].astype(BF16)
        gw_ref[...] = lax.dot_general(s, db, (((0,), (0,)), ((), ())), preferred_element_type=F32)
        ds_ref[...] += lax.dot_general(db, w_ref[...].astype(BF16), (((1,), (1,)), ((), ())),
                                       preferred_element_type=F32)

    return pl.pallas_call(
        body, grid=(L, ns // tn),
        in_specs=[pl.BlockSpec((16, D), lambda l, j: (0, 0)),
                  pl.BlockSpec((None, 16, tn), lambda l, j: (l, 0, j)),
                  pl.BlockSpec((None, D, tn), lambda l, j: (l, 0, j))],
        out_specs=[pl.BlockSpec((None, D, tn), lambda l, j: (l, 0, j)),
                   pl.BlockSpec((16, D), lambda l, j: (0, 0))],
        out_shape=[jax.ShapeDtypeStruct((L, D, ns), F32), jax.ShapeDtypeStruct((16, D), F32)],
        name=name, compiler_params=_cp(("arbitrary", "arbitrary")),
    )(cvec, dmod, w_ada)


def _cctx_grad(gathered, c_ctx, name):
    rows = c_ctx.shape[0]

    def body(g_ref, c_ref, o_ref):
        ds = g_ref[0] + g_ref[2] + g_ref[4] + g_ref[6]
        cv = c_ref[...]
        sig = jax.nn.sigmoid(cv)
        o_ref[...] = ds * sig * (1.0 + cv * (1.0 - sig))

    return pl.pallas_call(
        body, out_shape=jax.ShapeDtypeStruct((rows, LANE), F32), name=name,
    )(gathered, c_ctx)


def _place():
    x, y, c = lax.axis_index("x"), lax.axis_index("y"), lax.axis_index("c")
    chips = [(1 - x, y), (x, 1 - y), (1 - x, 1 - y)]
    return x, y, c, chips


def _all_gather_small(slab, name, after=()):
    rows, cols = slab.shape

    def body(x_ref, *rest):
        out_ref, send_sems, recv_sems, local_sem = rest[len(after):]
        x, y, c, chips = _place()
        me, sibling = (x, y, c), (x, y, 1 - c)

        def blk(px, py, pc):
            return out_ref.at[4 * px + 2 * py + pc]

        def copy(k, block, to, src=None):
            return pltpu.make_async_remote_copy(
                src_ref=blk(*block) if src is None else src, dst_ref=blk(*block),
                send_sem=send_sems.at[k], recv_sem=recv_sems.at[k], device_id=to, device_id_type=MESH)

        mine = pltpu.make_async_copy(x_ref, blk(*me), local_sem)
        mine.start()
        first = [copy(0, me, sibling, src=x_ref)]
        first += [copy(1 + j, me, (*chip, c), src=x_ref) for j, chip in enumerate(chips)]
        for cp in first:
            cp.start()
        passed = [copy(4 + j, (*chip, c), sibling) for j, chip in enumerate(chips)]
        for j, chip in enumerate(chips):
            copy(1 + j, (*chip, c), me).wait_recv()
            passed[j].start()
        copy(0, sibling, me).wait_recv()
        for j, chip in enumerate(chips):
            copy(4 + j, (*chip, 1 - c), me).wait_recv()
        for cp in first + passed:
            cp.wait_send()
        mine.wait()

    return pl.pallas_call(
        body, out_shape=jax.ShapeDtypeStruct((N_DEV, rows, cols), slab.dtype),
        in_specs=[pl.BlockSpec(memory_space=pltpu.VMEM)] + [_ANY] * len(after),
        out_specs=pl.BlockSpec(memory_space=pltpu.VMEM),
        scratch_shapes=[pltpu.SemaphoreType.DMA((7,)), pltpu.SemaphoreType.DMA((7,)), pltpu.SemaphoreType.DMA],
        name=name, compiler_params=pltpu.CompilerParams(vmem_limit_bytes=VMEM_MID),
    )(slab, *after)


def _hbm(a):
    return pltpu.with_memory_space_constraint(a, pltpu.HBM)


def _half_rows(ref, q, c):
    half = ref.shape[1] // 2
    return ref.at[q, pl.ds(c * half, half)]


def _ici_copy(src, dst, send_sems, recv_sems, k, to):
    return pltpu.make_async_remote_copy(src_ref=src, dst_ref=dst, send_sem=send_sems.at[k], recv_sem=recv_sems.at[k],
                                        device_id=to, device_id_type=MESH)


def _gather_start(bufs, sizes, name):
    n = len(bufs)
    ng = len(sizes)

    def body(*refs):
        ins = refs[:n]
        sems = refs[n:n + 2 * ng]
        token = refs[-1]
        x, y, c, chips = _place()
        a = 0
        for g, sz in enumerate(sizes):
            for i in range(sz):
                part = _half_rows(ins[a], 2 * x + y, c)
                for j, chip in enumerate(chips):
                    _ici_copy(part, part, sems[2 * g], sems[2 * g + 1], 3 * i + j, (*chip, c)).start()
                a += 1
        token[...] = jnp.zeros_like(token)

    sem_shapes = []
    for sz in sizes:
        sem_shapes += [pltpu.SemaphoreType.DMA((3 * sz,)), pltpu.SemaphoreType.DMA((3 * sz,))]
    outs = pl.pallas_call(
        body, name=name,
        out_shape=tuple(sem_shapes) + tuple(pltpu.HBM(b.shape, b.dtype) for b in bufs)
                  + (jax.ShapeDtypeStruct((SUBLANE, LANE), F32),),
        in_specs=[_HBM] * n,
        out_specs=tuple([_SEM] * (2 * ng) + [_HBM] * n + [pl.BlockSpec(memory_space=pltpu.VMEM)]),
        input_output_aliases={i: 2 * ng + i for i in range(n)},
        compiler_params=pltpu.CompilerParams(has_side_effects=_EFFECT),
    )(*[_hbm(b) for b in bufs])
    sems = [(outs[2 * g], outs[2 * g + 1]) for g in range(ng)]
    return sems, list(outs[2 * ng:2 * ng + n]), outs[-1]


def _gather_wait(bufs, sems, after, name):
    n = len(bufs)
    na = len(after)

    def body(*refs):
        ins = refs[:n]
        send_sems, recv_sems = refs[n], refs[n + 1]
        x, y, c, chips = _place()
        for i in range(n):
            sent = _half_rows(ins[i], 2 * x + y, c)
            for j, (cx, cy) in enumerate(chips):
                cp = _ici_copy(sent, _half_rows(ins[i], 2 * cx + cy, c), send_sems, recv_sems, 3 * i + j, (cx, cy, c))
                cp.wait_send()
                cp.wait_recv()

    outs = pl.pallas_call(
        body, name=name,
        out_shape=tuple(pltpu.HBM(b.shape, b.dtype) for b in bufs),
        in_specs=[_HBM] * n + [_SEM, _SEM] + [_ANY] * na,
        out_specs=tuple([_HBM] * n),
        input_output_aliases={i: i for i in range(n)},
        compiler_params=pltpu.CompilerParams(has_side_effects=_EFFECT),
    )(*bufs, sems[0], sems[1], *after)
    return list(outs)


def _exchange_halves(bufs, name):
    n = len(bufs)

    def body(*refs):
        ins = refs[:n]
        send_sems, recv_sems = refs[-2:]
        x, y, c, chips = _place()
        sends = []
        for i in range(n):
            for j, (cx, cy) in enumerate(chips):
                part = _half_rows(ins[i], 2 * cx + cy, c)
                cp = _ici_copy(part, part, send_sems, recv_sems, 3 * i + j, (x, y, 1 - c))
                cp.start()
                sends.append(cp)
        for i in range(n):
            for j, (cx, cy) in enumerate(chips):
                part = _half_rows(ins[i], 2 * cx + cy, 1 - c)
                _ici_copy(part, part, send_sems, recv_sems, 3 * i + j, (x, y, 1 - c)).wait_recv()
        for cp in sends:
            cp.wait_send()

    outs = pl.pallas_call(
        body, name=name,
        out_shape=[jax.ShapeDtypeStruct(b.shape, b.dtype) for b in bufs],
        in_specs=[_ANY] * n, out_specs=[_ANY] * n,
        input_output_aliases={i: i for i in range(n)},
        scratch_shapes=[pltpu.SemaphoreType.DMA((3 * n,)), pltpu.SemaphoreType.DMA((3 * n,))],
    )(*bufs)
    return list(outs)


def _other_half(ref, c):
    half = ref.shape[1] // 2
    return ref.at[:, pl.ds((1 - c) * half, half)]


def _prereduce_plan(src, land, x, y, c, chips):
    return [(_other_half(src, c), land, (x, y, 1 - c))]


def _prereduce_land(d):
    return (d.shape[0], d.shape[1] // 2, d.shape[2])


def _scatter_plan(src, land, x, y, c, chips):
    return [(src.at[2 * cx + cy], land.at[j], (cx, cy, c)) for j, (cx, cy) in enumerate(chips)]


def _scatter_land(p):
    return (3,) + p.shape[1:]


def _allgather_plan(src, land, x, y, c, chips):
    peers = [(x, y, 1 - c)] + [(cx, cy, cc) for (cx, cy) in chips for cc in (c, 1 - c)]
    return [(src, land.at[4 * x + 2 * y + c], p) for p in peers]


def _allgather_land(s):
    return (N_DEV,) + s.shape


_COPIES_PER_SOURCE = {_prereduce_plan: 1, _scatter_plan: 3, _allgather_plan: N_DEV - 1}


def _split_start(srcs, plan, land_shape, name, land_dtype=BF16):
    n = len(srcs)
    land_shapes = [land_shape(s) for s in srcs]
    per = _COPIES_PER_SOURCE[plan]

    def body(*refs):
        src_refs, lands = refs[:n], refs[n:2 * n]
        send_sems, recv_sems = refs[2 * n], refs[2 * n + 1]
        token = refs[-1]
        x, y, c, chips = _place()
        for i in range(n):
            for j, (s, d, to) in enumerate(plan(src_refs[i], lands[i], x, y, c, chips)):
                _ici_copy(s, d, send_sems, recv_sems, per * i + j, to).start()
        token[...] = jnp.zeros_like(token)

    outs = pl.pallas_call(
        body, name=name,
        out_shape=(pltpu.SemaphoreType.DMA((per * n,)), pltpu.SemaphoreType.DMA((per * n,)))
                  + tuple(pltpu.HBM(p.shape, p.dtype) for p in srcs)
                  + tuple(pltpu.HBM(s, land_dtype) for s in land_shapes)
                  + (jax.ShapeDtypeStruct((SUBLANE, LANE), F32),),
        in_specs=[_HBM] * (2 * n),
        out_specs=tuple([_SEM, _SEM] + [_HBM] * (2 * n) + [pl.BlockSpec(memory_space=pltpu.VMEM)]),
        input_output_aliases={i: 2 + i for i in range(2 * n)},
        compiler_params=pltpu.CompilerParams(has_side_effects=_EFFECT),
    )(*[_hbm(p) for p in srcs], *[_hbm(lax.empty(s, land_dtype)) for s in land_shapes])
    return (outs[0], outs[1]), list(outs[2:2 + n]), list(outs[2 + n:2 + 2 * n]), outs[-1]


def _split_wait(psums, lands, sems, plan, after, name):
    n = len(psums)
    na = len(after)
    per = _COPIES_PER_SOURCE[plan]

    def body(*refs):
        srcs, lnds = refs[:n], refs[n:2 * n]
        send_sems, recv_sems = refs[2 * n], refs[2 * n + 1]
        x, y, c, chips = _place()
        for i in range(n):
            for j, (s, d, to) in enumerate(plan(srcs[i], lnds[i], x, y, c, chips)):
                cp = _ici_copy(s, d, send_sems, recv_sems, per * i + j, to)
                cp.wait_send()
                cp.wait_recv()

    outs = pl.pallas_call(
        body, name=name,
        out_shape=tuple(pltpu.HBM(a.shape, a.dtype) for a in list(psums) + list(lands)),
        in_specs=[_HBM] * (2 * n) + [_SEM, _SEM] + [_ANY] * na,
        out_specs=tuple([_HBM] * (2 * n)),
        input_output_aliases={i: i for i in range(2 * n)},
        compiler_params=pltpu.CompilerParams(has_side_effects=_EFFECT),
    )(*psums, *lands, sems[0], sems[1], *after)
    return list(outs[:n]), list(outs[n:])


def _layer_half(ref, l, core):
    half = ref.shape[1] // 2
    return ref.at[l, pl.ds(core * half, half)]


def _exchange_grads_start(grads, l, name):
    n = len(grads)

    def body(*refs):
        ins = refs[:n]
        send_sems, recv_sems = refs[n], refs[n + 1]
        token = refs[-1]
        x, y, c, _ = _place()
        for i in range(n):
            mine = _layer_half(ins[i], l, c)
            _ici_copy(mine, mine, send_sems, recv_sems, i, (x, y, 1 - c)).start()
        token[...] = jnp.zeros_like(token)

    outs = pl.pallas_call(
        body, name=name,
        out_shape=(pltpu.SemaphoreType.DMA((n,)), pltpu.SemaphoreType.DMA((n,)))
                  + tuple(pltpu.HBM(g.shape, g.dtype) for g in grads)
                  + (jax.ShapeDtypeStruct((SUBLANE, LANE), F32),),
        in_specs=[_HBM] * n,
        out_specs=tuple([_SEM, _SEM] + [_HBM] * n + [pl.BlockSpec(memory_space=pltpu.VMEM)]),
        input_output_aliases={i: 2 + i for i in range(n)},
        compiler_params=pltpu.CompilerParams(has_side_effects=_EFFECT),
    )(*[_hbm(g) for g in grads])
    return (outs[0], outs[1]), list(outs[2:2 + n]), outs[-1]


def _exchange_grads_wait(grads, l, sems, after, name):
    n = len(grads)

    def body(*refs):
        ins = refs[:n]
        send_sems, recv_sems = refs[n], refs[n + 1]
        x, y, c, _ = _place()
        for i in range(n):
            cp = _ici_copy(_layer_half(ins[i], l, c), _layer_half(ins[i], l, 1 - c), send_sems, recv_sems, i,
                           (x, y, 1 - c))
            cp.wait_send()
            cp.wait_recv()

    outs = pl.pallas_call(
        body, name=name,
        out_shape=tuple(pltpu.HBM(g.shape, g.dtype) for g in grads),
        in_specs=[_HBM] * n + [_SEM, _SEM] + [_ANY] * len(after),
        out_specs=tuple([_HBM] * n),
        input_output_aliases={i: i for i in range(n)},
        compiler_params=pltpu.CompilerParams(has_side_effects=_EFFECT),
    )(*grads, sems[0], sems[1], *after)
    return list(outs)


def _pack(arrs):
    flat = jnp.concatenate([a.reshape(-1).astype(F32) for a in arrs])
    pad = (-flat.shape[0]) % (SUBLANE * LANE)
    return jnp.pad(flat, (0, pad)).reshape(-1, LANE)


def _unpack(slab, shapes):
    flat = slab.reshape(-1)
    out, off = [], 0
    for s in shapes:
        n = math.prod(s)
        out.append(flat[off:off + n].reshape(s))
        off += n
    return out


def _split6(v):
    d = v.shape[-1] // 6
    return [v[:, k * d:(k + 1) * d].reshape(2, 1, d) for k in range(6)]


BIG = ("w_in", "w_br_attn", "w_br_pool", "w_br_sgu", "w_br_conv", "w_gate", "w_o", "w_ff_gate", "w_ff_up", "w_ff_down")
KIND = {"w_in": "col", "w_br_attn": "col", "w_br_pool": "col", "w_br_sgu": "col", "w_br_conv": "col",
        "w_gate": "col", "w_o": "row", "w_ff_gate": "col", "w_ff_up": "col", "w_ff_down": "row"}
SMALL = ("c_ctx", "b_ada", "q_norm_g", "k_norm_g", "pool_w", "pool_scale", "sgu_ln_g", "sgu_ln_b", "sgu_w",
         "sgu_b", "conv_w", "b_gate", "ln1_g", "ln1_b", "ln2_g", "ln2_b")
WEIGHTS = ("c_ctx", "w_ada", "b_ada", "w_in", "q_norm_g", "k_norm_g", "pool_w", "pool_scale", "sgu_ln_g", "sgu_ln_b",
           "sgu_w", "sgu_b", "conv_w", "w_br_attn", "w_br_pool", "w_br_sgu", "w_br_conv", "w_gate", "b_gate", "w_o",
           "ln1_g", "ln1_b", "w_ff_gate", "w_ff_up", "w_ff_down", "ln2_g", "ln2_b")


def _step(x, c, ctx, loss_target, W, M, V):
    L = W["w_ada"].shape[0]
    assert L == 2, "core c of a chip carries layer c of the weight traffic"
    N, D = x.shape[1], x.shape[2]
    NC = ctx.shape[1]
    T = NC + N
    FF = W["w_ff_down"].shape[1] * N_CHIP
    assert NC % ROW_TM == 0 and N % ROW_TM == 0 and N % GRID_W == 0 and D % LANE == 0
    ncb = NC // ROW_TM
    nblk = T // ROW_TM
    alpha = (2 * L) ** 0.25
    ax, ay, ac = lax.axis_index("x"), lax.axis_index("y"), lax.axis_index("c")
    chip = 2 * ax + ay
    dev = 2 * chip + ac
    chip_arr = jnp.reshape(chip, (1,)).astype(jnp.int32)
    core_arr = jnp.reshape(ac, (1,)).astype(jnp.int32)
    ns_ada = W["w_ada"].shape[2]
    chip_devs = (0, 2, 4, 6)

    conv_shape = W["conv_w"].shape
    g0 = _all_gather_small(_pack([c, W["conv_w"]]), "gather_c")
    c_all = g0[:, :D // LANE, :].reshape(N_DEV, D)
    conv_parts = [g0[d].reshape(-1)[D:D + math.prod(conv_shape)].reshape(conv_shape) for d in chip_devs]
    conv_full = jnp.concatenate(conv_parts, axis=-1)
    cvec = jnp.concatenate([c_all, W["c_ctx"][None], jnp.zeros((16 - N_DEV - 1, D), F32)], axis=0)
    b_cols = lax.dynamic_slice_in_dim(W["b_ada"], chip * ns_ada, ns_ada, axis=1).reshape(L, 1, ns_ada)
    mod_part = _ada_fwd(cvec, W["w_ada"], b_cols, "ada_fwd")
    g1 = _all_gather_small(mod_part.reshape(-1, LANE), "gather_mod")
    mod_all = jnp.concatenate([g1[d].reshape(L, 16, ns_ada) for d in chip_devs], axis=-1)
    mod_lat = lax.dynamic_index_in_dim(mod_all, dev, axis=1, keepdims=True)
    mods = jnp.concatenate([mod_all[:, N_DEV:N_DEV + 1], mod_lat], axis=1)

    ffn_keys = ("w_ff_gate", "w_ff_up", "w_ff_down")
    mid_keys = ("w_br_attn", "w_br_pool", "w_br_sgu", "w_br_conv", "w_o")
    groups = [grp for l in range(L) for grp in ([("w_in", l)], [("w_gate", l)], [(k, l) for k in mid_keys],
                                                [("w_ff_gate", l), ("w_ff_up", l)], [("w_ff_down", l)])]
    GPL = 5
    flying, gsems, gtokens = [None] * len(groups), [None] * len(groups), []
    for n_start, gs in enumerate(((0, 1), (2, 3, 4), (5, 6, 7, 8, 9))):
        members = [kl for g in gs for kl in groups[g]]
        casts = [_cast_into(chip_arr, W[k], l, "cast_%s_l%d" % (k, l), after=gtokens[-1:] or [g1])
                 for k, l in members]
        sems, bufs, token = _gather_start(casts, [len(groups[g]) for g in gs], "gather_start_%d" % n_start)
        gtokens.append(token)
        lo = 0
        for g, sem in zip(gs, sems):
            flying[g], gsems[g] = bufs[lo:lo + len(groups[g])], sem
            lo += len(groups[g])
    full = {k: [None] * L for k in BIG}

    def weights_ready(g, after):
        got = _gather_wait(flying[g], gsems[g], after, "gather_wait_%d" % g)
        for (k, l), buf in zip(groups[g], _exchange_halves(got, "exchange_halves_%d" % g)):
            full[k][l] = buf

    cos, sin = _rope_tables(N, NC)
    x0 = jnp.concatenate([ctx[0], x[0]], axis=0)

    saved = []
    xin = x0
    h1 = None
    dy_top = loss_parts = None
    for l in range(L):
        sh1, sc1, g1m, sh2, sc2, g2m = _split6(mods[l])
        tag = "_l%d" % l
        if l == 0:
            h1 = _modulate(xin, sc1, sh1, ncb, "modulate" + tag)
        weights_ready(GPL * l, [h1])
        p = _mm_nn(h1, full["w_in"], l, "col", name="mm_in" + tag)
        qg, kg = W["q_norm_g"][l][None], W["k_norm_g"][l][None]
        q, k, v = _qk_fwd(p, cos, sin, qg, kg, "qk_fwd" + tag)
        att = _attn_fwd(q, k, v, NC, "attn_fwd" + tag)
        ps = W["pool_scale"][l][None]
        ypool = _pool_fwd(p, W["pool_w"][l], ps, NC, "pool_fwd" + tag)
        lg, lb = W["sgu_ln_g"][l][None], W["sgu_ln_b"][l][None]
        b_st = W["sgu_b"][l].T
        ysgu = _sgu_fwd(p, lg, lb, W["sgu_w"][l], b_st, "sgu_fwd" + tag)
        yconv = _conv_fwd(p, conv_full[l], NC, "conv_fwd" + tag)
        brs = (att, ypool, ysgu, yconv)
        weights_ready(GPL * l + 1, list(brs))
        gates = _mm_nn(h1, full["w_gate"], l, "col", name="mm_gate" + tag, bias=W["b_gate"][l][None],
                       act=jax.nn.sigmoid, out_dtype=BF16)
        weights_ready(GPL * l + 2, [gates])
        ts = [_mm_nn(b, full[wk], l, "col", name="mm_" + wk + tag, out_dtype=BF16)
              for b, wk in zip(brs, ("w_br_attn", "w_br_pool", "w_br_sgu", "w_br_conv"))]
        mg = _merge_fwd(gates, ts, "merge_fwd" + tag)
        m = _mm_nn(mg, full["w_o"], l, "row", name="mm_o" + tag, tn_pref=1024)
        ln1g, ln1b = W["ln1_g"][l][None], W["ln1_b"][l][None]
        x1, h2 = _resid_ln(xin, m, g1m, ln1g, ln1b, sc2, sh2, ncb, alpha, "resid_ln1" + tag)
        weights_ready(GPL * l + 3, [h2])
        gg, uu, act = _ffn_up(h2, full["w_ff_gate"][l], full["w_ff_up"][l], "ffn_up" + tag)
        weights_ready(GPL * l + 4, [act])
        ff = _mm_nn(act, full["w_ff_down"], l, "row", name="mm_ffd" + tag, tm_pref=384, tn_pref=512)
        ln2g, ln2b = W["ln2_g"][l][None], W["ln2_b"][l][None]
        saved.append(dict(xin=xin, h1=h1, p=p, q=q, k=k, v=v, brs=brs, gates=gates, ts=ts, mg=mg, m=m, x1=x1, h2=h2,
                          gg=gg, uu=uu, act=act, ff=ff))
        if l + 1 < L:
            nsh1, nsc1 = _split6(mods[l + 1])[:2]
            xin, h1 = _resid_ln(x1, ff, g2m, ln2g, ln2b, nsc1, nsh1, ncb, alpha, "resid_ln2" + tag)
        else:
            dy_top, loss_parts = _resid_ln_loss(x1, ff, g2m, ln2g, ln2b, loss_target[0], ncb, alpha,
                                                "resid_ln2_loss" + tag)
    loss_dev = jnp.sum(loss_parts[:, 0, 0])

    def lat_ctx(part_rows):
        return jnp.stack([jnp.sum(part_rows[:ncb], axis=0), jnp.sum(part_rows[ncb:], axis=0)])

    dW = {k: [None] * L for k in BIG}
    small = {k: [None] * L for k in ("q_norm_g", "k_norm_g", "pool_w", "pool_scale", "sgu_ln_g", "sgu_ln_b", "sgu_w",
                                     "sgu_b", "conv_w", "b_gate", "ln1_g", "ln1_b", "ln2_g", "ln2_b")}
    dmods = [None] * L
    mix_keys = tuple(k for k in BIG if k not in ffn_keys)
    grads_big = {k: None for k in BIG}
    adam_out = {k: None for k in BIG}

    def reduce_begin(keys, l, tg):
        sems, src, land, token = _split_start([dW[k][l] for k in keys], _prereduce_plan, _prereduce_land,
                                              "prereduce_start_" + tg)
        return dict(keys=keys, l=l, sems=sems, src=src, land=land, token=token, tg=tg)

    def reduce_scatter(st, after):
        dws, recv = _split_wait(st["src"], st["land"], st["sems"], _prereduce_plan, after,
                                "prereduce_wait_" + st["tg"])
        psums = [_pair_sum(core_arr, d, r, "pair_sum_%s_l%d" % (k, st["l"]))
                 for k, d, r in zip(st["keys"], dws, recv)]
        st["sems"], st["src"], st["land"], st["token"] = _split_start(psums, _scatter_plan, _scatter_land,
                                                                      "scatter_start_" + st["tg"])

    def reduce_finish(st, after):
        src, land = _split_wait(st["src"], st["land"], st["sems"], _scatter_plan, after, "scatter_wait_" + st["tg"])
        for k, p_, r in zip(st["keys"], src, land):
            grads_big[k] = _chip_sum(chip_arr, core_arr, p_, r, st["l"], L, grads_big[k],
                                     "chip_sum_%s_l%d" % (k, st["l"]))
        st["sems"], st["fly"], st["token"] = _exchange_grads_start([grads_big[k] for k in st["keys"]], st["l"],
                                                                   "exchange_grads_start_" + st["tg"])

    def reduce_done(st, after):
        done = _exchange_grads_wait(st["fly"], st["l"], st["sems"], after, "exchange_grads_wait_" + st["tg"])
        grads_big.update(zip(st["keys"], done))

    def adam(keys, l, after):
        for k in keys:
            adam_out[k] = _adamw_layer(W[k], grads_big[k], M[k], V[k], l, adam_out[k], after,
                                       "adamw_%s_l%d" % (k, l))

    deferred = []

    def grad_of(key, l, thunk):
        if l == 0:
            deferred.append((key, thunk))
        else:
            dW[key][l] = thunk(())

    dxa, dhs, sc_prev = dy_top, [], None
    red_l1 = red_ffn = None
    for l in reversed(range(L)):
        s = saved[l]
        sh1, sc1, g1m, sh2, sc2, g2m = _split6(mods[l])
        tag = "_l%d" % l
        ln1g, ln1b = W["ln1_g"][l][None], W["ln1_b"][l][None]
        ln2g, ln2b = W["ln2_g"][l][None], W["ln2_b"][l][None]
        dx1a, dff, part2 = _ln_bwd(dxa, dhs, sc_prev if dhs else sc1, s["x1"], s["ff"], g2m, ln2g, ln2b, ncb, alpha,
                                   "ln2_bwd" + tag, after=[] if red_l1 is None else [red_l1["token"]])
        small["ln2_g"][l] = jnp.sum(part2[:, 0], axis=0)
        small["ln2_b"][l] = jnp.sum(part2[:, 1], axis=0)
        dg2 = lat_ctx(part2[:, 2])
        if dhs:
            dmods[l + 1][1], dmods[l + 1][0] = lat_ctx(part2[:, 3]), lat_ctx(part2[:, 4])
        dgg, duu = _ffn_down_bwd(dff, full["w_ff_down"][l], s["gg"], s["uu"], "ffn_down_bwd" + tag)
        if l == 0:
            reduce_scatter(red_l1, [dgg])
        dW["w_ff_down"][l] = _mm_tn(s["act"], dff, "row", nq=N_CHIP, kdim=FF // N_CHIP, ndim=D, name="tn_ffd" + tag,
                                    tk_pref=1408, tn_pref=512, after=[red_l1["token"]] if l == 0 else [])
        dh2a = _mm_nt(dgg, full["w_ff_gate"], l, "col", name="nt_ffg" + tag)
        dh2b = _mm_nt(duu, full["w_ff_up"], l, "col", name="nt_ffu" + tag)
        dW["w_ff_gate"][l] = _mm_tn(s["h2"], dgg, "col", nq=N_CHIP, kdim=D, ndim=FF // N_CHIP, name="tn_ffg" + tag,
                                    tn_pref=1408)
        dW["w_ff_up"][l] = _mm_tn(s["h2"], duu, "col", nq=N_CHIP, kdim=D, ndim=FF // N_CHIP, name="tn_ffu" + tag,
                                  tn_pref=1408)
        ties = []
        if l == 0:
            red_ffn = reduce_begin(ffn_keys, 0, "l0_ffn")
            ties = [red_l1["token"], red_ffn["token"]]
        dx0a, dm, part1 = _ln_bwd(dx1a, [dh2a, dh2b], sc2, s["xin"], s["m"], g1m, ln1g, ln1b, ncb, alpha,
                                  "ln1_bwd" + tag, after=ties)
        small["ln1_g"][l] = jnp.sum(part1[:, 0], axis=0)
        small["ln1_b"][l] = jnp.sum(part1[:, 1], axis=0)
        dg1 = lat_ctx(part1[:, 2])
        dsc2, dsh2 = lat_ctx(part1[:, 3]), lat_ctx(part1[:, 4])
        dmods[l] = [None, None, dg1, dsh2, dsc2, dg2]
        dmg = _mm_nt(dm, full["w_o"], l, "row", name="nt_o" + tag)
        grad_of("w_o", l, lambda after, a=s["mg"], d=dm, tag=tag: _mm_tn(
            a, d, "row", nq=N_CHIP, kdim=D // N_CHIP, ndim=D, name="tn_o" + tag, tn_pref=1024, after=after))
        dpre, dt0, dt1, dt2, dt3, bpart = _merge_bwd(dmg, s["gates"], s["ts"], "merge_bwd" + tag)
        small["b_gate"][l] = jnp.sum(bpart[:, 0], axis=0)
        dh1a = _mm_nt(dpre, full["w_gate"], l, "col", name="nt_gate" + tag)
        grad_of("w_gate", l, lambda after, a=s["h1"], d=dpre, tag=tag: _mm_tn(
            a, d, "col", nq=N_CHIP, kdim=D, ndim=D, name="tn_gate" + tag, tn_pref=1024, after=after))
        dbrs = []
        for b, dt, wk, odt in zip(s["brs"], (dt0, dt1, dt2, dt3), ("w_br_attn", "w_br_pool", "w_br_sgu", "w_br_conv"),
                                  (BF16, F32, F32, F32)):
            dbrs.append(_mm_nt(dt, full[wk], l, "col", name="nt_" + wk + tag, out_dtype=odt))
            grad_of(wk, l, lambda after, a=b, d=dt, wk=wk, tag=tag: _mm_tn(
                a, d, "col", nq=N_CHIP, kdim=a.shape[1], ndim=D // N_CHIP, name="tn_" + wk + tag, after=after))
        qg, kg = W["q_norm_g"][l][None], W["k_norm_g"][l][None]
        dq, dk, dv = _attn_bwd(s["q"], s["k"], s["v"], dbrs[0], NC, "attn_bwd" + tag)
        dp, qkpart = _qk_bwd(s["p"], dq, dk, dv, cos, sin, qg, kg, "qk_bwd" + tag)
        small["q_norm_g"][l] = jnp.sum(qkpart[:, 0], axis=0)
        small["k_norm_g"][l] = jnp.sum(qkpart[:, 1], axis=0)
        ps = W["pool_scale"][l][None]
        dp, small["pool_w"][l], dps = _pool_bwd(s["p"], dbrs[1], W["pool_w"][l], ps, dp, NC, "pool_bwd" + tag)
        small["pool_scale"][l] = dps[0]
        lg, lb = W["sgu_ln_g"][l][None], W["sgu_ln_b"][l][None]
        dp, small["sgu_w"][l], dsacc, dln = _sgu_bwd(s["p"], dbrs[2], lg, lb, W["sgu_w"][l], W["sgu_b"][l].T, dp,
                                                     "sgu_bwd" + tag)
        small["sgu_b"][l] = jnp.sum(dsacc.reshape(GC, N_GROUP, GC), axis=-1).T
        small["sgu_ln_g"][l], small["sgu_ln_b"][l] = dln[0], dln[1]
        dp, small["conv_w"][l] = _conv_bwd(s["p"], dbrs[3], conv_full[l], dp, NC, "conv_bwd" + tag)
        dh1b = _mm_nt(dp, full["w_in"], l, "col", name="nt_in" + tag)
        grad_of("w_in", l, lambda after, a=s["h1"], d=dp, tag=tag: _mm_tn(
            a, d, "col", nq=N_CHIP, kdim=D, ndim=IN_W // N_CHIP, name="tn_in" + tag, tn_pref=1152, after=after))
        dxa, dhs, sc_prev = dx0a, [dh1a, dh1b], sc1
        if l == 1:
            red_l1 = reduce_begin(BIG, 1, "l1")
    dx0, part0 = _mod_bwd(dxa, dhs, sc_prev, x0, ncb, "mod_bwd")
    dmods[0][1], dmods[0][0] = lat_ctx(part0[:, 0]), lat_ctx(part0[:, 1])
    grad_x = dx0[NC:][None]
    dmod = jnp.stack([jnp.concatenate(dmods[l], axis=-1) for l in range(L)])

    small_names = [k for k in SMALL if k not in ("c_ctx", "b_ada")]
    small_arrs = [jnp.stack(small[k]) for k in small_names]
    small_shapes = [a.shape for a in small_arrs]
    slab = _pack([loss_dev.reshape(1), jnp.zeros((LANE - 1,), F32), dmod] + small_arrs)
    sg_sems, sg_src, sg_land, sg_token = _split_start([slab], _allgather_plan, _allgather_land, "small_start", F32)
    reduce_scatter(red_ffn, [sg_token])
    for key, thunk in deferred:
        dW[key][0] = thunk([red_ffn["token"]])
    red_mix = reduce_begin(mix_keys, 0, "l0_mix")
    sg_src, sg_land = _split_wait(sg_src, sg_land, sg_sems, _allgather_plan, [red_mix["token"]], "small_wait")
    g2 = lax.dynamic_update_index_in_dim(sg_land[0], sg_src[0], dev, 0)
    total = _slab_sum(g2, "slab_sum")
    flat_total = total.reshape(-1)
    loss = flat_total[0]
    nmod = L * 2 * 6 * D
    dmod_sum = flat_total[LANE:LANE + nmod].reshape(L, 2, 6 * D)
    small_grads = dict(zip(small_names, _unpack(flat_total[LANE + nmod:], small_shapes)))
    dmod_lat = g2.reshape(N_DEV, -1)[:, LANE:LANE + nmod].reshape(N_DEV, L, 2, 6 * D)[:, :, 1]
    dm16 = jnp.concatenate([jnp.transpose(dmod_lat, (1, 0, 2)), dmod_sum[:, 0:1],
                            jnp.zeros((L, 16 - N_DEV - 1, 6 * D), F32)], axis=1)
    small_grads["b_ada"] = dmod_sum[:, 0] + dmod_sum[:, 1]
    dm16_cols = lax.dynamic_slice_in_dim(dm16, chip * ns_ada, ns_ada, axis=2)
    grad_w_ada, ds16 = _ada_bwd(cvec, dm16_cols, W["w_ada"], "ada_bwd")
    conv_grad_full = small_grads["conv_w"]
    small_grads["conv_w"] = lax.dynamic_slice_in_dim(conv_grad_full, chip * GC, GC, axis=2)

    reduce_finish(red_l1, [red_ffn["token"], red_mix["token"]])
    reduce_scatter(red_mix, [red_l1["token"]])
    reduce_done(red_l1, [red_mix["token"]])
    adam(BIG, 1, [red_mix["token"]])
    reduce_finish(red_ffn, [adam_out[k][0] for k in BIG])
    delta, new_m, new_v = {}, {}, {}
    delta["w_ada"], new_m["w_ada"], new_v["w_ada"] = _adamw(W["w_ada"], grad_w_ada, M["w_ada"], V["w_ada"],
                                                            "adamw_w_ada", after=[red_ffn["token"]])
    reduce_done(red_ffn, [delta["w_ada"]])
    reduce_finish(red_mix, [delta["w_ada"]])
    adam(ffn_keys, 0, [red_mix["token"]])
    reduce_done(red_mix, [adam_out[k][0] for k in ffn_keys])
    adam(mix_keys, 0, [])
    for k in BIG:
        delta[k], new_m[k], new_v[k] = adam_out[k]

    g3 = _all_gather_small(ds16[N_DEV].reshape(-1, LANE), "gather_dsilu", after=[adam_out[k][0] for k in mix_keys])
    small_grads["c_ctx"] = _cctx_grad(g3, W["c_ctx"].reshape(-1, LANE), "cctx_grad").reshape(D)
    shapes = [W[k].shape for k in SMALL]
    sd, sm, sv = _adamw(_pack([W[k] for k in SMALL]), _pack([small_grads[k] for k in SMALL]),
                        _pack([M[k] for k in SMALL]), _pack([V[k] for k in SMALL]), "adamw_small")
    for k, d_, m_, v_ in zip(SMALL, _unpack(sd, shapes), _unpack(sm, shapes), _unpack(sv, shapes)):
        delta[k], new_m[k], new_v[k] = d_, m_, v_
    grads = dict(grads_big)
    grads["w_ada"] = grad_w_ada
    grads.update(small_grads)
    return (loss, grad_x, *[grads[k] for k in WEIGHTS], *[delta[k] for k in WEIGHTS],
            *[new_m[k] for k in WEIGHTS], *[new_v[k] for k in WEIGHTS])


def kernel(x, c, ctx, c_ctx, w_ada, b_ada, w_in, q_norm_g, k_norm_g, pool_w, pool_scale, sgu_ln_g, sgu_ln_b, sgu_w, sgu_b, conv_w, w_br_attn, w_br_pool, w_br_sgu, w_br_conv, w_gate, b_gate, w_o, ln1_g, ln1_b, w_ff_gate, w_ff_up, w_ff_down, ln2_g, ln2_b, loss_target, m_c_ctx, m_w_ada, m_b_ada, m_w_in, m_q_norm_g, m_k_norm_g, m_pool_w, m_pool_scale, m_sgu_ln_g, m_sgu_ln_b, m_sgu_w, m_sgu_b, m_conv_w, m_w_br_attn, m_w_br_pool, m_w_br_sgu, m_w_br_conv, m_w_gate, m_b_gate, m_w_o, m_ln1_g, m_ln1_b, m_w_ff_gate, m_w_ff_up, m_w_ff_down, m_ln2_g, m_ln2_b, v_c_ctx, v_w_ada, v_b_ada, v_w_in, v_q_norm_g, v_k_norm_g, v_pool_w, v_pool_scale, v_sgu_ln_g, v_sgu_ln_b, v_sgu_w, v_sgu_b, v_conv_w, v_w_br_attn, v_w_br_pool, v_w_br_sgu, v_w_br_conv, v_w_gate, v_b_gate, v_w_o, v_ln1_g, v_ln1_b, v_w_ff_gate, v_w_ff_up, v_w_ff_down, v_ln2_g, v_ln2_b):
    args = locals()
    W = {k: args[k] for k in WEIGHTS}
    M = {k: args["m_" + k] for k in WEIGHTS}
    V = {k: args["v_" + k] for k in WEIGHTS}
    return _step(x, c, ctx, loss_target, W, M, V)
```

```python
import functools
import math

import jax
import jax.numpy as jnp
from jax import lax
from jax.experimental import pallas as pl
from jax.experimental.pallas import tpu as pltpu

F32 = jnp.float32
BF16 = jnp.bfloat16
MESH = pl.DeviceIdType.MESH

N_DEV = 8
N_CHIP = 4
GRID_W = 64
HEAD_DIM = 128
N_HEADS = 8
N_KV = 2
KV_GROUP = N_HEADS // N_KV
Q_W = N_HEADS * HEAD_DIM
KV_W = N_KV * HEAD_DIM
QKV_W = Q_W + 2 * KV_W
GC = 128
N_GROUP = 4
BR_W = N_GROUP * GC
POOL_WINDOWS = (2, 4, 8, 16)
OFF_POOL = QKV_W
OFF_U = OFF_POOL + BR_W
OFF_VG = OFF_U + BR_W
OFF_CB = OFF_VG + BR_W
OFF_CC = OFF_CB + BR_W
OFF_CX = OFF_CC + BR_W
IN_W = OFF_CX + BR_W
Q_BLOCK = 128
ROPE_THETA = 10000.0
LN_EPS = 1e-5
RMS_EPS = 1e-6
ADAM_LR = 0.001
ADAM_B1 = 0.9
ADAM_B2 = 0.999
ADAM_EPS = 1e-08
ADAM_WD = 0.01
ADAM_STEP = 10

ROW_TM = 256
LANE = 128
SUBLANE = 8
VMEM_BIG = 56 * 1024 * 1024
VMEM_MID = 40 * 1024 * 1024
NT_WHOLE_BYTES = 16 * 1024 * 1024


def _cp(sem=None, vmem=VMEM_MID):
    return pltpu.CompilerParams(dimension_semantics=sem, vmem_limit_bytes=vmem)


_ANY = pl.BlockSpec(memory_space=pl.ANY)
_HBM = pl.BlockSpec(memory_space=pltpu.HBM)
_SEM = pl.BlockSpec(memory_space=pltpu.SEMAPHORE)
_EFFECT = pltpu.SideEffectType.DATAFLOW_SIDE_EFFECTING


def _tile(n, pref, unit=LANE, whole_ok=False):
    if n <= pref:
        return n
    best = None
    for t in range(unit, pref + 1, unit):
        if n % t == 0:
            best = t
    if whole_ok and (best is None or best * 4 < pref):
        return n
    assert best is not None, (n, pref, unit)
    return best


def _layer_of(w, l):
    if isinstance(w, (list, tuple)):
        return w[l][:, None], 0
    return w, l


def _mm_nn(a, w, l, kind, *, name, out_dtype=F32, bias=None, act=None, tm_pref=768, tn_pref=1152):
    w, l = _layer_of(w, l)
    T, K = a.shape
    tm = _tile(T, tm_pref, SUBLANE * 2)
    if kind == "col":
        nq, _, kw, ns = w.shape
        assert kw == K
        tn = _tile(ns, tn_pref, whole_ok=True)
        nj = ns // tn
        n_total = nq * ns
        w_spec = pl.BlockSpec((None, None, K, tn), lambda j, i: (j // nj, l, 0, j % nj))
        grid_n = nq * nj
    else:
        nq, _, kc, n_total = w.shape
        assert nq * kc == K
        tn = _tile(n_total, tn_pref)
        w_spec = pl.BlockSpec((nq, None, kc, tn), lambda j, i: (0, l, 0, j))
        grid_n = n_total // tn

    def body(*refs):
        if bias is not None:
            a_ref, w_ref, b_ref, o_ref = refs
        else:
            a_ref, w_ref, o_ref = refs
        wv = w_ref[...]
        if kind == "row":
            wv = wv.reshape(K, tn)
        acc = jnp.dot(a_ref[...], wv, preferred_element_type=F32)
        if bias is not None:
            acc = acc + b_ref[...]
        if act is not None:
            acc = act(acc)
        o_ref[...] = acc.astype(out_dtype)

    in_specs = [pl.BlockSpec((tm, K), lambda j, i: (i, 0)), w_spec]
    args = [a, w]
    if bias is not None:
        in_specs.append(pl.BlockSpec((1, tn), lambda j, i: (0, j)))
        args.append(bias)
    return pl.pallas_call(
        body, grid=(grid_n, T // tm), in_specs=in_specs,
        out_specs=pl.BlockSpec((tm, tn), lambda j, i: (i, j)),
        out_shape=jax.ShapeDtypeStruct((T, n_total), out_dtype),
        name=name, compiler_params=_cp(("parallel", "parallel"), VMEM_BIG),
    )(*args)


def _mm_nt(dc, w, l, kind, *, name, out_dtype=F32, dc_off=0, tm_pref=768, tk_pref=1024):
    w, l = _layer_of(w, l)
    T = dc.shape[0]
    tm = _tile(T, tm_pref, SUBLANE * 2)
    if kind == "col":
        nq, _, K, ns = w.shape
        assert dc_off % ns == 0
        offb = dc_off // ns
        tks = _tile(K, 512)
        whole_bytes = 2 * (tm * nq * ns + nq * tks * ns)
        if dc_off == 0 and dc.shape[1] == nq * ns and whole_bytes <= NT_WHOLE_BYTES:
            def body_whole(dc_ref, w_ref, o_ref):
                acc = None
                for q in range(nq):
                    part = lax.dot_general(dc_ref[:, q * ns:(q + 1) * ns], w_ref[q], (((1,), (1,)), ((), ())),
                                           preferred_element_type=F32)
                    acc = part if acc is None else acc + part
                o_ref[...] = acc.astype(out_dtype)

            return pl.pallas_call(
                body_whole, grid=(T // tm, K // tks),
                in_specs=[pl.BlockSpec((tm, nq * ns), lambda i, kk: (i, 0)),
                          pl.BlockSpec((nq, None, tks, ns), lambda i, kk: (0, l, kk, 0))],
                out_specs=pl.BlockSpec((tm, tks), lambda i, kk: (i, kk)),
                out_shape=jax.ShapeDtypeStruct((T, K), out_dtype),
                name=name, compiler_params=_cp(("parallel", "parallel"), VMEM_BIG),
            )(dc, w)
        tk = _tile(K, tk_pref)

        def body(dc_ref, w_ref, o_ref, acc_ref):
            q = pl.program_id(2)
            part = lax.dot_general(dc_ref[...], w_ref[...], (((1,), (1,)), ((), ())),
                                   preferred_element_type=F32)

            @pl.when(q == 0)
            def _():
                acc_ref[...] = part

            @pl.when(q > 0)
            def _():
                acc_ref[...] += part

            @pl.when(q == nq - 1)
            def _():
                o_ref[...] = acc_ref[...].astype(out_dtype)

        return pl.pallas_call(
            body, grid=(K // tk, T // tm, nq),
            in_specs=[pl.BlockSpec((tm, ns), lambda kk, i, q: (i, offb + q)),
                      pl.BlockSpec((None, None, tk, ns), lambda kk, i, q: (q, l, kk, 0))],
            out_specs=pl.BlockSpec((tm, tk), lambda kk, i, q: (i, kk)),
            out_shape=jax.ShapeDtypeStruct((T, K), out_dtype),
            scratch_shapes=[pltpu.VMEM((tm, tk), F32)],
            name=name, compiler_params=_cp(("parallel", "parallel", "arbitrary"), VMEM_BIG),
        )(dc, w)

    nq, _, kc, n = w.shape
    assert dc_off % n == 0
    offb = dc_off // n
    tk = _tile(kc, tk_pref, whole_ok=True)
    nkk = kc // tk

    def body(dc_ref, w_ref, o_ref):
        o_ref[...] = lax.dot_general(dc_ref[...], w_ref[...], (((1,), (1,)), ((), ())),
                                     preferred_element_type=F32).astype(out_dtype)

    return pl.pallas_call(
        body, grid=(nq * nkk, T // tm),
        in_specs=[pl.BlockSpec((tm, n), lambda j, i: (i, offb)),
                  pl.BlockSpec((None, None, tk, n), lambda j, i: (j // nkk, l, j % nkk, 0))],
        out_specs=pl.BlockSpec((tm, tk), lambda j, i: (i, j)),
        out_shape=jax.ShapeDtypeStruct((T, nq * kc), out_dtype),
        name=name, compiler_params=_cp(("parallel", "parallel"), VMEM_BIG),
    )(dc, w)


def _mm_tn(a, dc, kind, *, nq, kdim, ndim, name, a_off=0, dc_off=0, tk_pref=512, tn_pref=1152, out_dtype=BF16,
           after=()):
    T = a.shape[0]
    assert dc.shape[0] == T
    tk = _tile(kdim, tk_pref, whole_ok=True)
    tn = _tile(ndim, tn_pref, whole_ok=True)
    nkk, njn = kdim // tk, ndim // tn
    assert a_off % tk == 0 and dc_off % tn == 0
    aoffb, doffb = a_off // tk, dc_off // tn
    if kind == "col":
        a_map = lambda q, kk, jn: (0, aoffb + kk)
        d_map = lambda q, kk, jn: (0, doffb + q * njn + jn)
    else:
        a_map = lambda q, kk, jn: (0, aoffb + q * nkk + kk)
        d_map = lambda q, kk, jn: (0, doffb + jn)

    def body(a_ref, d_ref, *rest):
        rest[-1][...] = lax.dot_general(a_ref[...], d_ref[...], (((0,), (0,)), ((), ())),
                                        preferred_element_type=F32).astype(out_dtype)

    return pl.pallas_call(
        body, grid=(nq, nkk, njn),
        in_specs=[pl.BlockSpec((T, tk), a_map), pl.BlockSpec((T, tn), d_map)] + [_ANY] * len(after),
        out_specs=pl.BlockSpec((None, tk, tn), lambda q, kk, jn: (q, kk, jn)),
        out_shape=jax.ShapeDtypeStruct((nq, kdim, ndim), out_dtype),
        name=name, compiler_params=_cp(("parallel", "parallel", "parallel"), VMEM_BIG),
    )(a, dc, *after)


def _row_spec(d):
    return pl.BlockSpec((ROW_TM, d), lambda i: (i, 0))


def _mod_spec(d, ncb):
    return pl.BlockSpec((None, 1, d), lambda i: (jnp.where(i >= ncb, 1, 0), 0, 0))


def _vec_spec(d):
    return pl.BlockSpec((1, d), lambda i: (0, 0))


def _part_spec(d):
    return pl.BlockSpec((None, SUBLANE, d), lambda i: (i, 0, 0))


def _modulate(x, sc, sh, ncb, name):
    T, D = x.shape

    def body(x_ref, sc_ref, sh_ref, o_ref):
        o_ref[...] = (x_ref[...] * (1.0 + sc_ref[...]) + sh_ref[...]).astype(BF16)

    return pl.pallas_call(
        body, grid=(T // ROW_TM,),
        in_specs=[_row_spec(D), _mod_spec(D, ncb), _mod_spec(D, ncb)],
        out_specs=_row_spec(D), out_shape=jax.ShapeDtypeStruct((T, D), BF16),
        name=name, compiler_params=_cp(("parallel",)),
    )(x, sc, sh)


def _ln_stats(r):
    mu = jnp.mean(r, axis=-1, keepdims=True)
    rc = r - mu
    var = jnp.mean(rc * rc, axis=-1, keepdims=True)
    rstd = lax.rsqrt(var + LN_EPS)
    return rc * rstd, rstd


def _resid_ln(x, y, gate, g, b, sc, sh, ncb, alpha, name):
    T, D = x.shape

    def body(x_ref, y_ref, gate_ref, g_ref, b_ref, sc_ref, sh_ref, xo_ref, h_ref):
        xhat, _ = _ln_stats(alpha * x_ref[...] + gate_ref[...] * y_ref[...])
        xo = xhat * g_ref[...] + b_ref[...]
        xo_ref[...] = xo
        h_ref[...] = (xo * (1.0 + sc_ref[...]) + sh_ref[...]).astype(BF16)

    return pl.pallas_call(
        body, grid=(T // ROW_TM,),
        in_specs=[_row_spec(D), _row_spec(D), _mod_spec(D, ncb), _vec_spec(D), _vec_spec(D),
                  _mod_spec(D, ncb), _mod_spec(D, ncb)],
        out_specs=[_row_spec(D), _row_spec(D)],
        out_shape=[jax.ShapeDtypeStruct((T, D), F32), jax.ShapeDtypeStruct((T, D), BF16)],
        name=name, compiler_params=_cp(("parallel",)),
    )(x, y, gate, g, b, sc, sh)


def _resid_ln_loss(x, y, gate, g, b, target, ncb, alpha, name):
    T, D = x.shape
    nblk = T // ROW_TM

    def body(x_ref, y_ref, gate_ref, g_ref, b_ref, t_ref, dy_ref, loss_ref):
        i = pl.program_id(0)
        xhat, _ = _ln_stats(alpha * x_ref[...] + gate_ref[...] * y_ref[...])
        xo = xhat * g_ref[...] + b_ref[...]
        live = (i >= ncb).astype(F32)
        err = (xo - t_ref[...]) * live
        dy_ref[...] = err * (1.0 / D)
        loss_ref[...] = jnp.full((SUBLANE, LANE), 0.5 / D, F32) * jnp.sum(err * err)

    return pl.pallas_call(
        body, grid=(nblk,),
        in_specs=[_row_spec(D), _row_spec(D), _mod_spec(D, ncb), _vec_spec(D), _vec_spec(D),
                  pl.BlockSpec((ROW_TM, D), lambda i: (jnp.maximum(i - ncb, 0), 0))],
        out_specs=[_row_spec(D), _part_spec(LANE)],
        out_shape=[jax.ShapeDtypeStruct((T, D), F32), jax.ShapeDtypeStruct((nblk, SUBLANE, LANE), F32)],
        name=name, compiler_params=_cp(("parallel",)),
    )(x, y, gate, g, b, target)


def _write_parts(part_ref, rows, d):
    for k, r in enumerate(rows):
        part_ref[pl.ds(k, 1), :] = jnp.sum(r, axis=0, keepdims=True)
    if len(rows) < SUBLANE:
        part_ref[pl.ds(len(rows), SUBLANE - len(rows)), :] = jnp.zeros((SUBLANE - len(rows), d), F32)


def _ln_bwd(dxa, dhs, sc, x, y, gate, g, b, ncb, alpha, name, after=None):
    T, D = x.shape
    nblk = T // ROW_TM
    ndh = len(dhs)

    def body(*refs):
        dxa_ref = refs[0]
        dh_refs = refs[1:1 + ndh]
        sc_ref, x_ref, y_ref, gate_ref, g_ref, b_ref = refs[1 + ndh:7 + ndh]
        dx_ref, dy_ref, part_ref = refs[-3:]
        yv = y_ref[...]
        xhat, rstd = _ln_stats(alpha * x_ref[...] + gate_ref[...] * yv)
        dxo = dxa_ref[...]
        rows = []
        if ndh:
            dh = dh_refs[0][...]
            for r in dh_refs[1:]:
                dh = dh + r[...]
            dxo = dxo + dh * (1.0 + sc_ref[...])
            xo = xhat * g_ref[...] + b_ref[...]
            rows = [dh * xo, dh]
        dxhat = dxo * g_ref[...]
        m1 = jnp.mean(dxhat, axis=-1, keepdims=True)
        m2 = jnp.mean(dxhat * xhat, axis=-1, keepdims=True)
        dr = rstd * (dxhat - m1 - xhat * m2)
        dx_ref[...] = alpha * dr
        dy_ref[...] = (gate_ref[...] * dr).astype(BF16)
        _write_parts(part_ref, [dxo * xhat, dxo, dr * yv] + rows, D)

    in_specs = ([_row_spec(D)] * (1 + ndh)
                + [_mod_spec(D, ncb), _row_spec(D), _row_spec(D), _mod_spec(D, ncb), _vec_spec(D), _vec_spec(D)])
    extra = list(after or ())
    return pl.pallas_call(
        body, grid=(nblk,), in_specs=in_specs + [_ANY] * len(extra),
        out_specs=[_row_spec(D), _row_spec(D), _part_spec(D)],
        out_shape=[jax.ShapeDtypeStruct((T, D), F32), jax.ShapeDtypeStruct((T, D), BF16),
                   jax.ShapeDtypeStruct((nblk, SUBLANE, D), F32)],
        name=name, compiler_params=_cp(("parallel",)),
    )(dxa, *dhs, sc, x, y, gate, g, b, *extra)


def _mod_bwd(dxa, dhs, sc, x, ncb, name, after=()):
    T, D = x.shape
    nblk = T // ROW_TM
    ndh = len(dhs)

    def body(*refs):
        dxa_ref = refs[0]
        dh_refs = refs[1:1 + ndh]
        sc_ref, x_ref = refs[1 + ndh:3 + ndh]
        dx_ref, part_ref = refs[-2:]
        dh = dh_refs[0][...]
        for r in dh_refs[1:]:
            dh = dh + r[...]
        dx_ref[...] = dxa_ref[...] + dh * (1.0 + sc_ref[...])
        _write_parts(part_ref, [dh * x_ref[...], dh], D)

    return pl.pallas_call(
        body, grid=(nblk,),
        in_specs=[_row_spec(D)] * (1 + ndh) + [_mod_spec(D, ncb), _row_spec(D)] + [_ANY] * len(after),
        out_specs=[_row_spec(D), _part_spec(D)],
        out_shape=[jax.ShapeDtypeStruct((T, D), F32), jax.ShapeDtypeStruct((nblk, SUBLANE, D), F32)],
        name=name, compiler_params=_cp(("parallel",)),
    )(dxa, *dhs, sc, x, *after)


def _merge_fwd(gates, ts, name):
    T, D = ts[0].shape
    tm = 128

    def body(g_ref, t0, t1, t2, t3, o_ref):
        acc = g_ref[:, 0:D].astype(F32) * t0[...].astype(F32)
        for k, t in enumerate((t1, t2, t3), start=1):
            acc = acc + g_ref[:, k * D:(k + 1) * D].astype(F32) * t[...].astype(F32)
        o_ref[...] = acc.astype(BF16)

    rs = pl.BlockSpec((tm, D), lambda i: (i, 0))
    return pl.pallas_call(
        body, grid=(T // tm,),
        in_specs=[pl.BlockSpec((tm, 4 * D), lambda i: (i, 0)), rs, rs, rs, rs],
        out_specs=rs, out_shape=jax.ShapeDtypeStruct((T, D), BF16),
        name=name, compiler_params=_cp(("parallel",)),
    )(gates, *ts)


def _merge_bwd(dmg, gates, ts, name):
    T, D = dmg.shape
    tm = 128
    nblk = T // tm

    def body(d_ref, g_ref, t0, t1, t2, t3, dpre_ref, dt0, dt1, dt2, dt3, part_ref):
        d = d_ref[...]
        for k, (t, dt) in enumerate(zip((t0, t1, t2, t3), (dt0, dt1, dt2, dt3))):
            gk = g_ref[:, k * D:(k + 1) * D].astype(F32)
            dt[...] = (d * gk).astype(BF16)
            dpre = d * t[...].astype(F32) * gk * (1.0 - gk)
            dpre_ref[:, k * D:(k + 1) * D] = dpre.astype(BF16)
            part_ref[:, k * D:(k + 1) * D] = jnp.sum(dpre, axis=0, keepdims=True)

    rs = pl.BlockSpec((tm, D), lambda i: (i, 0))
    wide = pl.BlockSpec((tm, 4 * D), lambda i: (i, 0))
    return pl.pallas_call(
        body, grid=(nblk,),
        in_specs=[rs, wide, rs, rs, rs, rs],
        out_specs=[wide, rs, rs, rs, rs, pl.BlockSpec((None, 1, 4 * D), lambda i: (i, 0, 0))],
        out_shape=[jax.ShapeDtypeStruct((T, 4 * D), BF16)] + [jax.ShapeDtypeStruct((T, D), BF16)] * 4
                  + [jax.ShapeDtypeStruct((nblk, 1, 4 * D), F32)],
        name=name, compiler_params=_cp(("parallel",)),
    )(dmg, gates, *ts)


def _ffn_up(a, wg, wu, name, tm_pref=384):
    T, K = a.shape
    nq, _, ns = wg.shape
    tm = _tile(T, tm_pref, SUBLANE * 2)

    def body(a_ref, g_ref, u_ref, gg_ref, uu_ref, act_ref):
        av = a_ref[...]
        g = jnp.dot(av, g_ref[...], preferred_element_type=F32)
        u = jnp.dot(av, u_ref[...], preferred_element_type=F32)
        gg_ref[...] = g.astype(BF16)
        uu_ref[...] = u.astype(BF16)
        act_ref[...] = (g * jax.nn.sigmoid(g) * u).astype(BF16)

    ws = pl.BlockSpec((None, K, ns), lambda q, i: (q, 0, 0))
    os = pl.BlockSpec((tm, ns), lambda q, i: (i, q))
    return pl.pallas_call(
        body, grid=(nq, T // tm),
        in_specs=[pl.BlockSpec((tm, K), lambda q, i: (i, 0)), ws, ws], out_specs=[os, os, os],
        out_shape=[jax.ShapeDtypeStruct((T, nq * ns), BF16)] * 3,
        name=name, compiler_params=_cp(("parallel", "parallel"), VMEM_BIG),
    )(a, wg, wu)


def _ffn_down_bwd(dff, wd, gg, uu, name, tm_pref=384):
    T, N = dff.shape
    nq, kc, _ = wd.shape
    tm = _tile(T, tm_pref, SUBLANE * 2)

    def body(d_ref, w_ref, g_ref, u_ref, dg_ref, du_ref):
        da = lax.dot_general(d_ref[...], w_ref[...], (((1,), (1,)), ((), ())), preferred_element_type=F32)
        g = g_ref[...].astype(F32)
        sig = jax.nn.sigmoid(g)
        dg_ref[...] = (da * u_ref[...].astype(F32) * sig * (1.0 + g * (1.0 - sig))).astype(BF16)
        du_ref[...] = (da * g * sig).astype(BF16)

    ts = pl.BlockSpec((tm, kc), lambda q, i: (i, q))
    return pl.pallas_call(
        body, grid=(nq, T // tm),
        in_specs=[pl.BlockSpec((tm, N), lambda q, i: (i, 0)), pl.BlockSpec((None, kc, N), lambda q, i: (q, 0, 0)),
                  ts, ts],
        out_specs=[ts, ts], out_shape=[jax.ShapeDtypeStruct((T, nq * kc), BF16)] * 2,
        name=name, compiler_params=_cp(("parallel", "parallel"), VMEM_BIG),
    )(dff, wd, gg, uu)


def _swap_halves(v):
    lane = lax.broadcasted_iota(jnp.int32, v.shape, 1)
    return jnp.where((lane % 64) < 32, pltpu.roll(v, 96, 1), pltpu.roll(v, 32, 1))


def _rope_tables(n, nc):
    rows = n // GRID_W
    row = jnp.repeat(jnp.arange(rows), GRID_W).astype(F32)
    col = jnp.tile(jnp.arange(GRID_W), rows).astype(F32)
    inv = ROPE_THETA ** (-jnp.arange(0, 64, 2, dtype=F32) / 64)
    ang_r = row[:, None] * inv
    ang_c = col[:, None] * inv
    cos = jnp.concatenate([jnp.cos(ang_r), jnp.cos(ang_r), jnp.cos(ang_c), jnp.cos(ang_c)], axis=-1)
    sin = jnp.concatenate([-jnp.sin(ang_r), jnp.sin(ang_r), -jnp.sin(ang_c), jnp.sin(ang_c)], axis=-1)
    cos = jnp.concatenate([jnp.ones((nc, HEAD_DIM), F32), cos], axis=0)
    sin = jnp.concatenate([jnp.zeros((nc, HEAD_DIM), F32), sin], axis=0)
    return cos, sin


def _qk_fwd(p, cos, sin, qg, kg, name):
    T = p.shape[0]

    def body(p_ref, c_ref, s_ref, qg_ref, kg_ref, q_ref, k_ref, v_ref):
        c, s = c_ref[...], s_ref[...]
        for h in range(N_HEADS + N_KV):
            x = p_ref[:, h * HEAD_DIM:(h + 1) * HEAD_DIM]
            rs = lax.rsqrt(jnp.mean(x * x, axis=-1, keepdims=True) + RMS_EPS)
            gain = qg_ref[...] if h < N_HEADS else kg_ref[...]
            yv = x * rs * gain
            out = yv * c + _swap_halves(yv) * s
            if h < N_HEADS:
                out = out * Q_PRESCALE
            out = out.astype(BF16)
            if h < N_HEADS:
                q_ref[:, h * HEAD_DIM:(h + 1) * HEAD_DIM] = out
            else:
                k_ref[:, (h - N_HEADS) * HEAD_DIM:(h - N_HEADS + 1) * HEAD_DIM] = out
        v_ref[...] = p_ref[:, Q_W + KV_W:QKV_W].astype(BF16)

    return pl.pallas_call(
        body, grid=(T // ROW_TM,),
        in_specs=[_row_spec(QKV_W), _row_spec(HEAD_DIM), _row_spec(HEAD_DIM), _vec_spec(HEAD_DIM), _vec_spec(HEAD_DIM)],
        out_specs=[_row_spec(Q_W), _row_spec(KV_W), _row_spec(KV_W)],
        out_shape=[jax.ShapeDtypeStruct((T, Q_W), BF16), jax.ShapeDtypeStruct((T, KV_W), BF16),
                   jax.ShapeDtypeStruct((T, KV_W), BF16)],
        name=name, compiler_params=_cp(("parallel",)),
    )(p, cos, sin, qg, kg)


def _qk_bwd(p, dq, dk, dv, cos, sin, qg, kg, name):
    T = p.shape[0]
    nblk = T // ROW_TM

    def body(p_ref, dq_ref, dk_ref, dv_ref, c_ref, s_ref, qg_ref, kg_ref, dp_ref, part_ref):
        c, s = c_ref[...], s_ref[...]
        dgq = jnp.zeros((1, HEAD_DIM), F32)
        dgk = jnp.zeros((1, HEAD_DIM), F32)
        for h in range(N_HEADS + N_KV):
            x = p_ref[:, h * HEAD_DIM:(h + 1) * HEAD_DIM]
            if h < N_HEADS:
                d = dq_ref[:, h * HEAD_DIM:(h + 1) * HEAD_DIM]
                gain = qg_ref[...]
            else:
                d = dk_ref[:, (h - N_HEADS) * HEAD_DIM:(h - N_HEADS + 1) * HEAD_DIM]
                gain = kg_ref[...]
            dyv = d * c + _swap_halves(d * s)
            rs = lax.rsqrt(jnp.mean(x * x, axis=-1, keepdims=True) + RMS_EPS)
            xn = x * rs
            dgsum = jnp.sum(dyv * xn, axis=0, keepdims=True)
            if h < N_HEADS:
                dgq = dgq + dgsum
            else:
                dgk = dgk + dgsum
            dxg = dyv * gain
            dx = rs * (dxg - xn * jnp.mean(dxg * xn, axis=-1, keepdims=True))
            dp_ref[:, h * HEAD_DIM:(h + 1) * HEAD_DIM] = dx.astype(BF16)
        dp_ref[:, Q_W + KV_W:QKV_W] = dv_ref[...].astype(BF16)
        part_ref[pl.ds(0, 1), :] = dgq
        part_ref[pl.ds(1, 1), :] = dgk
        part_ref[pl.ds(2, SUBLANE - 2), :] = jnp.zeros((SUBLANE - 2, HEAD_DIM), F32)

    return pl.pallas_call(
        body, grid=(nblk,),
        in_specs=[_row_spec(QKV_W), _row_spec(Q_W), _row_spec(KV_W), _row_spec(KV_W),
                  _row_spec(HEAD_DIM), _row_spec(HEAD_DIM), _vec_spec(HEAD_DIM), _vec_spec(HEAD_DIM)],
        out_specs=[_row_spec(QKV_W), _part_spec(HEAD_DIM)],
        out_shape=[jax.ShapeDtypeStruct((T, IN_W), BF16), jax.ShapeDtypeStruct((nblk, SUBLANE, HEAD_DIM), F32)],
        name=name, compiler_params=_cp(("parallel",)),
    )(p, dq, dk, dv, cos, sin, qg, kg)


ATTN_SCALE = HEAD_DIM ** -0.5
LOG2E = 1.4426950408889634
Q_PRESCALE = ATTN_SCALE * LOG2E


def _attn_weights(q, k):
    s = lax.dot_general(q, k, (((1,), (1,)), ((), ())), preferred_element_type=F32)
    e = jnp.exp2(s - jnp.max(s, axis=-1, keepdims=True))
    return e, 1.0 / jnp.sum(e, axis=-1, keepdims=True)


def _attn_fwd(q, k, v, nc, name):
    T = q.shape[0]

    def heads(q_ref, k_ref, v_ref, o_ref, nkeys):
        for h in range(N_HEADS):
            g = h // KV_GROUP
            kk = k_ref[0:nkeys, g * HEAD_DIM:(g + 1) * HEAD_DIM]
            vv = v_ref[0:nkeys, g * HEAD_DIM:(g + 1) * HEAD_DIM]
            e, rl = _attn_weights(q_ref[:, h * HEAD_DIM:(h + 1) * HEAD_DIM], kk)
            o = jnp.dot(e.astype(BF16), vv, preferred_element_type=F32) * rl
            o_ref[:, h * HEAD_DIM:(h + 1) * HEAD_DIM] = o.astype(BF16)

    def body(q_ref, k_ref, v_ref, o_ref):
        i = pl.program_id(0)

        @pl.when(i < nc // Q_BLOCK)
        def _():
            heads(q_ref, k_ref, v_ref, o_ref, nc)

        @pl.when(i >= nc // Q_BLOCK)
        def _():
            heads(q_ref, k_ref, v_ref, o_ref, T)

    whole = pl.BlockSpec((T, KV_W), lambda i: (0, 0))
    qs = pl.BlockSpec((Q_BLOCK, Q_W), lambda i: (i, 0))
    return pl.pallas_call(
        body, grid=(T // Q_BLOCK,), in_specs=[qs, whole, whole], out_specs=qs,
        out_shape=jax.ShapeDtypeStruct((T, Q_W), BF16),
        name=name, compiler_params=_cp(("parallel",)),
    )(q, k, v)


def _attn_bwd(q, k, v, do, nc, name):
    T = q.shape[0]

    def heads(q_ref, k_ref, v_ref, do_ref, dq_ref, dk_ref, dv_ref, nkeys):
        for g in range(N_KV):
            cols = slice(g * HEAD_DIM, (g + 1) * HEAD_DIM)
            hcs = [slice(h * HEAD_DIM, (h + 1) * HEAD_DIM) for h in range(g * KV_GROUP, (g + 1) * KV_GROUP)]
            qh = jnp.concatenate([q_ref[:, hc] for hc in hcs], axis=0)
            doh = jnp.concatenate([do_ref[:, hc] for hc in hcs], axis=0)
            kk, vv = k_ref[0:nkeys, cols], v_ref[0:nkeys, cols]
            e, rl = _attn_weights(qh, kk)
            dpr = lax.dot_general(doh, vv, (((1,), (1,)), ((), ())), preferred_element_type=F32)
            delta = jnp.sum(e * dpr, axis=-1, keepdims=True) * rl
            dsu = (e * (dpr - delta)).astype(BF16)
            dq = jnp.dot(dsu, kk, preferred_element_type=F32) * (rl * ATTN_SCALE)
            for n, hc in enumerate(hcs):
                dq_ref[:, hc] = dq[n * Q_BLOCK:(n + 1) * Q_BLOCK]
            q_rows = (qh.astype(F32) * (rl * (1.0 / LOG2E))).astype(BF16)
            dk_ref[0:nkeys, cols] += lax.dot_general(dsu, q_rows, (((0,), (0,)), ((), ())),
                                                     preferred_element_type=F32)
            do_rows = (doh.astype(F32) * rl).astype(BF16)
            dv_ref[0:nkeys, cols] += lax.dot_general(e.astype(BF16), do_rows, (((0,), (0,)), ((), ())),
                                                     preferred_element_type=F32)

    def body(q_ref, k_ref, v_ref, do_ref, dq_ref, dk_ref, dv_ref):
        i = pl.program_id(0)

        @pl.when(i == 0)
        def _():
            dk_ref[...] = jnp.zeros_like(dk_ref)
            dv_ref[...] = jnp.zeros_like(dv_ref)

        @pl.when(i < nc // Q_BLOCK)
        def _():
            heads(q_ref, k_ref, v_ref, do_ref, dq_ref, dk_ref, dv_ref, nc)

        @pl.when(i >= nc // Q_BLOCK)
        def _():
            heads(q_ref, k_ref, v_ref, do_ref, dq_ref, dk_ref, dv_ref, T)

    whole = pl.BlockSpec((T, KV_W), lambda i: (0, 0))
    qs = pl.BlockSpec((Q_BLOCK, Q_W), lambda i: (i, 0))
    return pl.pallas_call(
        body, grid=(T // Q_BLOCK,), in_specs=[qs, whole, whole, qs], out_specs=[qs, whole, whole],
        out_shape=[jax.ShapeDtypeStruct((T, Q_W), F32), jax.ShapeDtypeStruct((T, KV_W), F32),
                   jax.ShapeDtypeStruct((T, KV_W), F32)],
        name=name, compiler_params=_cp(("arbitrary",), VMEM_BIG),
    )(q, k, v, do)


def _shift_rows(z, o, nc):
    T = z.shape[0]
    t = lax.broadcasted_iota(jnp.int32, (T, 1), 0)
    lo = jnp.where(t < nc, 0, nc)
    hi = jnp.where(t < nc, nc, T)
    ok = jnp.logical_and(t + o >= lo, t + o < hi)
    rolled = z if o == 0 else pltpu.roll(z, (-o) % T, 0)
    return jnp.where(ok, rolled, 0.0), ok


def _pool_window(w):
    left = w // 2
    return -left, w - 1 - left


def _pool_d(z, w, nc):
    o0, o1 = _pool_window(w)
    tot = jnp.zeros_like(z)
    cnt = jnp.zeros((z.shape[0], 1), F32)
    for o in range(o0, o1 + 1):
        sh, ok = _shift_rows(z, o, nc)
        tot = tot + sh
        cnt = cnt + ok.astype(F32)
    return tot / cnt - z, cnt


def _pool_fwd(p, pool_w, pool_scale, nc, name):
    T = p.shape[0]

    def body(z_ref, w_ref, s_ref, o_ref):
        for g, w in enumerate(POOL_WINDOWS):
            cs = slice(g * GC, (g + 1) * GC)
            d, _ = _pool_d(z_ref[:, cs], w, nc)
            yv = jnp.dot(d.astype(BF16), w_ref[g].astype(BF16), preferred_element_type=F32)
            o_ref[:, cs] = (yv * s_ref[:, cs]).astype(BF16)

    return pl.pallas_call(
        body, grid=(1,),
        in_specs=[pl.BlockSpec((T, BR_W), lambda i: (0, OFF_POOL // BR_W)),
                  pl.BlockSpec((N_GROUP, GC, GC), lambda i: (0, 0, 0)), _vec_spec(BR_W)],
        out_specs=pl.BlockSpec((T, BR_W), lambda i: (0, 0)),
        out_shape=jax.ShapeDtypeStruct((T, BR_W), BF16),
        name=name, compiler_params=_cp(("arbitrary",), VMEM_BIG),
    )(p, pool_w, pool_scale)


def _pool_bwd(p, dy, pool_w, pool_scale, dp, nc, name):
    T = p.shape[0]

    def body(z_ref, dy_ref, w_ref, s_ref, dp_in_ref, dz_ref, dw_ref, ds_ref):
        for g, w in enumerate(POOL_WINDOWS):
            cs = slice(g * GC, (g + 1) * GC)
            d, cnt = _pool_d(z_ref[:, cs], w, nc)
            db = d.astype(BF16)
            wb = w_ref[g].astype(BF16)
            dyv = dy_ref[:, cs]
            e = (dyv * s_ref[:, cs]).astype(BF16)
            dw_ref[g] = lax.dot_general(db, e, (((0,), (0,)), ((), ())), preferred_element_type=F32)
            yraw = jnp.dot(db, wb, preferred_element_type=F32)
            ds_ref[:, cs] = jnp.sum(dyv * yraw, axis=0, keepdims=True)
            dd = lax.dot_general(e, wb, (((1,), (1,)), ((), ())), preferred_element_type=F32)
            ec = dd / cnt
            o0, o1 = _pool_window(w)
            tot = jnp.zeros_like(dd)
            for o in range(o0, o1 + 1):
                tot = tot + _shift_rows(ec, -o, nc)[0]
            dz_ref[:, cs] = (tot - dd).astype(BF16)

    return pl.pallas_call(
        body, grid=(1,),
        in_specs=[pl.BlockSpec((T, BR_W), lambda i: (0, OFF_POOL // BR_W)),
                  pl.BlockSpec((T, BR_W), lambda i: (0, 0)),
                  pl.BlockSpec((N_GROUP, GC, GC), lambda i: (0, 0, 0)), _vec_spec(BR_W), _ANY],
        out_specs=[pl.BlockSpec((T, BR_W), lambda i: (0, OFF_POOL // BR_W)),
                   pl.BlockSpec((N_GROUP, GC, GC), lambda i: (0, 0, 0)), _vec_spec(BR_W)],
        out_shape=[jax.ShapeDtypeStruct(dp.shape, BF16), jax.ShapeDtypeStruct((N_GROUP, GC, GC), F32),
                   jax.ShapeDtypeStruct((1, BR_W), F32)],
        input_output_aliases={4: 0},
        name=name, compiler_params=_cp(("arbitrary",), VMEM_BIG),
    )(p, dy, pool_w, pool_scale, dp)


_GELU_C = math.sqrt(2.0 / math.pi)


def _gelu(x):
    return 0.5 * x * (1.0 + jnp.tanh(_GELU_C * (x + 0.044715 * x * x * x)))


def _gelu_grad(x):
    th = jnp.tanh(_GELU_C * (x + 0.044715 * x * x * x))
    return 0.5 * (1.0 + th) + 0.5 * x * (1.0 - th * th) * _GELU_C * (1.0 + 3.0 * 0.044715 * x * x)


def _sgu_fwd(p, ln_g, ln_b, w_s, b_st, name):
    T = p.shape[0]

    def body(pu_ref, pv_ref, g_ref, b_ref, w_ref, bs_ref, o_ref):
        u = _gelu(pu_ref[...])
        vhat, _ = _ln_stats(_gelu(pv_ref[...]))
        vn = (vhat * g_ref[...] + b_ref[...]).astype(BF16)
        for g in range(N_GROUP):
            cs = slice(g * GC, (g + 1) * GC)
            s = jnp.dot(w_ref[g].astype(BF16), vn[:, cs], preferred_element_type=F32) + bs_ref[:, g:g + 1]
            o_ref[:, cs] = (u[:, cs] * s).astype(BF16)

    chunk = lambda off: pl.BlockSpec((GC, BR_W), lambda i: (i, off // BR_W))
    return pl.pallas_call(
        body, grid=(T // GC,),
        in_specs=[chunk(OFF_U), chunk(OFF_VG), _vec_spec(BR_W), _vec_spec(BR_W),
                  pl.BlockSpec((N_GROUP, GC, GC), lambda i: (0, 0, 0)),
                  pl.BlockSpec((GC, N_GROUP), lambda i: (0, 0))],
        out_specs=pl.BlockSpec((GC, BR_W), lambda i: (i, 0)),
        out_shape=jax.ShapeDtypeStruct((T, BR_W), BF16),
        name=name, compiler_params=_cp(("parallel",)),
    )(p, p, ln_g, ln_b, w_s, b_st)


def _sgu_bwd(p, dy, ln_g, ln_b, w_s, b_st, dp, name):
    T = p.shape[0]

    def body(pu_ref, pv_ref, dy_ref, g_ref, b_ref, w_ref, bs_ref, dp_in_ref, dp_ref, dw_ref, dsacc_ref, dln_ref):
        i = pl.program_id(0)

        @pl.when(i == 0)
        def _():
            dw_ref[...] = jnp.zeros_like(dw_ref)
            dsacc_ref[...] = jnp.zeros_like(dsacc_ref)
            dln_ref[...] = jnp.zeros_like(dln_ref)

        pu, pv, dyv = pu_ref[...], pv_ref[...], dy_ref[...]
        u = _gelu(pu)
        vhat, rstd = _ln_stats(_gelu(pv))
        vn = (vhat * g_ref[...] + b_ref[...]).astype(BF16)
        ds = dyv * u
        dsb = ds.astype(BF16)
        dsacc_ref[...] += ds
        dvn_parts = []
        for g in range(N_GROUP):
            cs = slice(g * GC, (g + 1) * GC)
            wb = w_ref[g].astype(BF16)
            s = jnp.dot(wb, vn[:, cs], preferred_element_type=F32) + bs_ref[:, g:g + 1]
            dp_ref[:, cs] = (dyv[:, cs] * s * _gelu_grad(pu[:, cs])).astype(BF16)
            dvn_parts.append(lax.dot_general(wb, dsb[:, cs], (((0,), (0,)), ((), ())),
                                             preferred_element_type=F32))
            dw_ref[g] += lax.dot_general(dsb[:, cs], vn[:, cs], (((1,), (1,)), ((), ())),
                                         preferred_element_type=F32)
        dvn = jnp.concatenate(dvn_parts, axis=-1)
        dln_ref[pl.ds(0, 1), :] += jnp.sum(dvn * vhat, axis=0, keepdims=True)
        dln_ref[pl.ds(1, 1), :] += jnp.sum(dvn, axis=0, keepdims=True)
        dvhat = dvn * g_ref[...]
        m1 = jnp.mean(dvhat, axis=-1, keepdims=True)
        m2 = jnp.mean(dvhat * vhat, axis=-1, keepdims=True)
        dv = rstd * (dvhat - m1 - vhat * m2)
        dp_ref[:, BR_W:2 * BR_W] = (dv * _gelu_grad(pv)).astype(BF16)

    chunk = lambda off: pl.BlockSpec((GC, BR_W), lambda i: (i, off // BR_W))
    return pl.pallas_call(
        body, grid=(T // GC,),
        in_specs=[chunk(OFF_U), chunk(OFF_VG), pl.BlockSpec((GC, BR_W), lambda i: (i, 0)),
                  _vec_spec(BR_W), _vec_spec(BR_W),
                  pl.BlockSpec((N_GROUP, GC, GC), lambda i: (0, 0, 0)),
                  pl.BlockSpec((GC, N_GROUP), lambda i: (0, 0)), _ANY],
        out_specs=[pl.BlockSpec((GC, 2 * BR_W), lambda i: (i, OFF_U // (2 * BR_W))),
                   pl.BlockSpec((N_GROUP, GC, GC), lambda i: (0, 0, 0)),
                   pl.BlockSpec((GC, BR_W), lambda i: (0, 0)),
                   pl.BlockSpec((SUBLANE, BR_W), lambda i: (0, 0))],
        out_shape=[jax.ShapeDtypeStruct(dp.shape, BF16), jax.ShapeDtypeStruct((N_GROUP, GC, GC), F32),
                   jax.ShapeDtypeStruct((GC, BR_W), F32), jax.ShapeDtypeStruct((SUBLANE, BR_W), F32)],
        input_output_aliases={7: 0},
        name=name, compiler_params=_cp(("arbitrary",)),
    )(p, p, dy, ln_g, ln_b, w_s, b_st, dp)


def _conv_fwd(p, conv_w, nc, name):
    T = p.shape[0]

    def body(cb_ref, cc_ref, cx_ref, w_ref, o_ref):
        zz = cc_ref[...] * cx_ref[...]
        conv = (w_ref[0:1, :] * _shift_rows(zz, -1, nc)[0] + w_ref[1:2, :] * zz
                + w_ref[2:3, :] * _shift_rows(zz, 1, nc)[0])
        o_ref[...] = (cb_ref[...] * conv).astype(BF16)

    col = lambda off: pl.BlockSpec((T, GC), lambda j: (0, off // GC + j))
    return pl.pallas_call(
        body, grid=(N_GROUP,),
        in_specs=[col(OFF_CB), col(OFF_CC), col(OFF_CX), pl.BlockSpec((3, GC), lambda j: (0, j))],
        out_specs=pl.BlockSpec((T, GC), lambda j: (0, j)),
        out_shape=jax.ShapeDtypeStruct((T, BR_W), BF16),
        name=name, compiler_params=_cp(("parallel",)),
    )(p, p, p, conv_w)


def _conv_bwd(p, dy, conv_w, dp, nc, name):
    T = p.shape[0]

    def body(cb_ref, cc_ref, cx_ref, dy_ref, w_ref, dp_in_ref, dp_ref, dw_ref):
        part = pl.program_id(1)
        cc, cx = cc_ref[...], cx_ref[...]
        dconv = dy_ref[...] * cb_ref[...]

        @pl.when(part == 0)
        def _():
            zz = cc * cx
            zm, zp = _shift_rows(zz, -1, nc)[0], _shift_rows(zz, 1, nc)[0]
            conv = w_ref[0:1, :] * zm + w_ref[1:2, :] * zz + w_ref[2:3, :] * zp
            dw_ref[0:1, :] = jnp.sum(dconv * zm, axis=0, keepdims=True)
            dw_ref[1:2, :] = jnp.sum(dconv * zz, axis=0, keepdims=True)
            dw_ref[2:3, :] = jnp.sum(dconv * zp, axis=0, keepdims=True)
            dp_ref[...] = (dy_ref[...] * conv).astype(BF16)

        @pl.when(part > 0)
        def _():
            dzz = (w_ref[0:1, :] * _shift_rows(dconv, 1, nc)[0] + w_ref[1:2, :] * dconv
                   + w_ref[2:3, :] * _shift_rows(dconv, -1, nc)[0])
            dp_ref[...] = (dzz * jnp.where(part == 1, cx, cc)).astype(BF16)

    col = lambda off: pl.BlockSpec((T, GC), lambda j, part: (0, off // GC + j))
    return pl.pallas_call(
        body, grid=(N_GROUP, 3),
        in_specs=[col(OFF_CB), col(OFF_CC), col(OFF_CX), pl.BlockSpec((T, GC), lambda j, part: (0, j)),
                  pl.BlockSpec((3, GC), lambda j, part: (0, j)), _ANY],
        out_specs=[pl.BlockSpec((T, GC), lambda j, part: (0, OFF_CB // GC + part * N_GROUP + j)),
                   pl.BlockSpec((3, GC), lambda j, part: (0, j))],
        out_shape=[jax.ShapeDtypeStruct(dp.shape, BF16), jax.ShapeDtypeStruct((3, BR_W), F32)],
        input_output_aliases={5: 0},
        name=name, compiler_params=_cp(("parallel", "arbitrary")),
    )(p, p, p, dy, conv_w, dp)


def _rows_tile(rows, cols, n_arrays, vmem_bytes=24 * 1024 * 1024):
    budget = vmem_bytes // (2 * 4 * n_arrays * cols)
    return _tile(rows, max(SUBLANE * 2, min(budget, 1024)), SUBLANE * 2) if rows % (SUBLANE * 2) == 0 else rows


def _cast_into(chip, w, l, name, after=()):
    _, K, cols = w.shape
    tr = _rows_tile(K, cols, 2)

    def body(q_ref, w_ref, *rest):
        rest[-1][...] = w_ref[...].astype(BF16)

    return pl.pallas_call(
        body,
        grid_spec=pltpu.PrefetchScalarGridSpec(
            num_scalar_prefetch=1, grid=(K // tr,),
            in_specs=[pl.BlockSpec((None, tr, cols), lambda i, q: (l, i, 0))] + [_ANY] * len(after),
            out_specs=pl.BlockSpec((None, tr, cols), lambda i, q: (q[0], i, 0))),
        out_shape=jax.ShapeDtypeStruct((N_CHIP, K, cols), BF16),
        name=name, compiler_params=_cp(("parallel",)),
    )(chip, w, *after)


ADAM_VMEM = 34 * 1024 * 1024


def _adamw(w, g, m, v, name, after=()):
    shape = w.shape
    cols = shape[-1]
    rows = w.size // cols
    tr = _rows_tile(rows, cols, 7, ADAM_VMEM)
    c1 = 1.0 - ADAM_B1 ** ADAM_STEP
    c2 = 1.0 - ADAM_B2 ** ADAM_STEP

    def body(w_ref, g_ref, m_ref, v_ref, *rest):
        d_ref, mo_ref, vo_ref = rest[-3:]
        gv = g_ref[...]
        mn = ADAM_B1 * m_ref[...] + (1.0 - ADAM_B1) * gv
        vn = ADAM_B2 * v_ref[...] + (1.0 - ADAM_B2) * (gv * gv)
        mo_ref[...] = mn
        vo_ref[...] = vn
        d_ref[...] = -ADAM_LR * ((mn / c1) / (jnp.sqrt(vn / c2) + ADAM_EPS) + ADAM_WD * w_ref[...])

    rs = pl.BlockSpec((tr, cols), lambda i: (i, 0))
    outs = pl.pallas_call(
        body, grid=(rows // tr,), in_specs=[rs] * 4 + [_ANY] * len(after), out_specs=[rs] * 3,
        out_shape=[jax.ShapeDtypeStruct((rows, cols), F32)] * 3,
        name=name, compiler_params=_cp(("parallel",)),
    )(*[a.reshape(rows, cols) for a in (w, g, m, v)], *after)
    return [o.reshape(shape) for o in outs]


def _adamw_layer(w, g, m, v, l, prev, after, name):
    L, K, cols = w.shape
    tr = _rows_tile(K, cols, 7, ADAM_VMEM)
    c1 = 1.0 - ADAM_B1 ** ADAM_STEP
    c2 = 1.0 - ADAM_B2 ** ADAM_STEP
    nprev = 0 if prev is None else 3

    def body(*refs):
        w_ref, g_ref, m_ref, v_ref = refs[:4]
        d_ref, mo_ref, vo_ref = refs[-3:]
        gv = g_ref[...]
        mn = ADAM_B1 * m_ref[...] + (1.0 - ADAM_B1) * gv
        vn = ADAM_B2 * v_ref[...] + (1.0 - ADAM_B2) * (gv * gv)
        mo_ref[...] = mn
        vo_ref[...] = vn
        d_ref[...] = -ADAM_LR * ((mn / c1) / (jnp.sqrt(vn / c2) + ADAM_EPS) + ADAM_WD * w_ref[...])

    rs = pl.BlockSpec((None, tr, cols), lambda i: (l, i, 0))
    extra = list(prev or ()) + list(after)
    return pl.pallas_call(
        body, grid=(K // tr,), in_specs=[rs] * 4 + [_ANY] * len(extra), out_specs=[rs] * 3,
        out_shape=[jax.ShapeDtypeStruct((L, K, cols), F32)] * 3,
        input_output_aliases={4 + k: k for k in range(nprev)},
        name=name, compiler_params=_cp(("parallel",)),
    )(w, g, m, v, *extra)


def _pair_sum(core, dw, recv, name):
    nq, half, cols = recv.shape
    tr = _rows_tile(half, cols, 3)
    nb = half // tr

    def body(c_ref, a_ref, r_ref, o_ref):
        o_ref[...] = (a_ref[...].astype(F32) + r_ref[...].astype(F32)).astype(BF16)

    rs = pl.BlockSpec((None, tr, cols), lambda q, i, c: (q, i, 0))
    return pl.pallas_call(
        body,
        grid_spec=pltpu.PrefetchScalarGridSpec(
            num_scalar_prefetch=1, grid=(nq, nb),
            in_specs=[pl.BlockSpec((None, tr, cols), lambda q, i, c: (q, c[0] * nb + i, 0)), rs],
            out_specs=rs),
        out_shape=jax.ShapeDtypeStruct((nq, half, cols), BF16),
        name=name, compiler_params=_cp(("parallel", "parallel")),
    )(core, dw, recv)


def _chip_sum(chip, core, psum, land, l, n_layers, prev, name):
    _, half, cols = psum.shape
    tr = _rows_tile(half, cols, 5)
    nb = half // tr

    def body(*refs):
        p_ref, r_ref, o_ref = refs[2], refs[3], refs[-1]
        o_ref[...] = (p_ref[...].astype(F32) + r_ref[0].astype(F32) + r_ref[1].astype(F32)
                      + r_ref[2].astype(F32))

    extra = [] if prev is None else [prev]
    return pl.pallas_call(
        body,
        grid_spec=pltpu.PrefetchScalarGridSpec(
            num_scalar_prefetch=2, grid=(nb,),
            in_specs=[pl.BlockSpec((None, tr, cols), lambda i, q, c: (q[0], i, 0)),
                      pl.BlockSpec((3, tr, cols), lambda i, q, c: (0, i, 0))] + [_ANY] * len(extra),
            out_specs=pl.BlockSpec((None, tr, cols), lambda i, q, c: (l, c[0] * nb + i, 0))),
        out_shape=jax.ShapeDtypeStruct((n_layers, 2 * half, cols), F32),
        input_output_aliases={4: 0} if prev is not None else {},
        name=name, compiler_params=_cp(("parallel",)),
    )(chip, core, psum, land, *extra)


def _slab_sum(gathered, name, after=()):
    n, rows, cols = gathered.shape
    tr = _tile(rows, 512, SUBLANE)

    def body(g_ref, *rest):
        acc = g_ref[0]
        for d in range(1, n):
            acc = acc + g_ref[d]
        rest[-1][...] = acc

    return pl.pallas_call(
        body, grid=(rows // tr,),
        in_specs=[pl.BlockSpec((n, tr, cols), lambda i: (0, i, 0))] + [_ANY] * len(after),
        out_specs=pl.BlockSpec((tr, cols), lambda i: (i, 0)),
        out_shape=jax.ShapeDtypeStruct((rows, cols), F32),
        name=name, compiler_params=_cp(("parallel",)),
    )(gathered, *after)


def _silu(x):
    return x * jax.nn.sigmoid(x)


def _ada_fwd(cvec, w_ada, b_cols, name):
    L, D, ns = w_ada.shape
    tn = _tile(ns, 768)

    def body(c_ref, w_ref, b_ref, o_ref):
        s = _silu(c_ref[...]).astype(BF16)
        o_ref[...] = jnp.dot(s, w_ref[...].astype(BF16), preferred_element_type=F32) + b_ref[...]

    return pl.pallas_call(
        body, grid=(L, ns // tn),
        in_specs=[pl.BlockSpec((16, D), lambda l, j: (0, 0)),
                  pl.BlockSpec((None, D, tn), lambda l, j: (l, 0, j)),
                  pl.BlockSpec((None, 1, tn), lambda l, j: (l, 0, j))],
        out_specs=pl.BlockSpec((None, 16, tn), lambda l, j: (l, 0, j)),
        out_shape=jax.ShapeDtypeStruct((L, 16, ns), F32),
        name=name, compiler_params=_cp(("parallel", "parallel")),
    )(cvec, w_ada, b_cols)


def _ada_bwd(cvec, dmod, w_ada, name):
    L, D, ns = w_ada.shape
    tn = _tile(ns, 768)

    def body(c_ref, d_ref, w_ref, gw_ref, ds_ref):
        first = jnp.logical_and(pl.program_id(0) == 0, pl.program_id(1) == 0)

        @pl.when(first)
        def _():
            ds_ref[...] = jnp.zeros_like(ds_ref)

        s = _silu(c_ref[...]).astype(BF16)
        db = d_ref[...].astype(BF16)
        gw_ref[...] = lax.dot_general(s, db, (((0,), (0,)), ((), ())), preferred_element_type=F32)
        ds_ref[...] += lax.dot_general(db, w_ref[...].astype(BF16), (((1,), (1,)), ((), ())),
                                       preferred_element_type=F32)

    return pl.pallas_call(
        body, grid=(L, ns // tn),
        in_specs=[pl.BlockSpec((16, D), lambda l, j: (0, 0)),
                  pl.BlockSpec((None, 16, tn), lambda l, j: (l, 0, j)),
                  pl.BlockSpec((None, D, tn), lambda l, j: (l, 0, j))],
        out_specs=[pl.BlockSpec((None, D, tn), lambda l, j: (l, 0, j)),
                   pl.BlockSpec((16, D), lambda l, j: (0, 0))],
        out_shape=[jax.ShapeDtypeStruct((L, D, ns), F32), jax.ShapeDtypeStruct((16, D), F32)],
        name=name, compiler_params=_cp(("arbitrary", "arbitrary")),
    )(cvec, dmod, w_ada)


def _cctx_grad(gathered, c_ctx, name):
    rows = c_ctx.shape[0]

    def body(g_ref, c_ref, o_ref):
        ds = g_ref[0] + g_ref[2] + g_ref[4] + g_ref[6]
        cv = c_ref[...]
        sig = jax.nn.sigmoid(cv)
        o_ref[...] = ds * sig * (1.0 + cv * (1.0 - sig))

    return pl.pallas_call(
        body, out_shape=jax.ShapeDtypeStruct((rows, LANE), F32), name=name,
    )(gathered, c_ctx)


def _place():
    x, y, c = lax.axis_index("x"), lax.axis_index("y"), lax.axis_index("c")
    chips = [(1 - x, y), (x, 1 - y), (1 - x, 1 - y)]
    return x, y, c, chips


def _all_gather_small(slab, name, after=()):
    rows, cols = slab.shape

    def body(x_ref, *rest):
        out_ref, send_sems, recv_sems, local_sem = rest[len(after):]
        x, y, c, chips = _place()
        me, sibling = (x, y, c), (x, y, 1 - c)

        def blk(px, py, pc):
            return out_ref.at[4 * px + 2 * py + pc]

        def copy(k, block, to, src=None):
            return pltpu.make_async_remote_copy(
                src_ref=blk(*block) if src is None else src, dst_ref=blk(*block),
                send_sem=send_sems.at[k], recv_sem=recv_sems.at[k], device_id=to, device_id_type=MESH)

        mine = pltpu.make_async_copy(x_ref, blk(*me), local_sem)
        mine.start()
        first = [copy(0, me, sibling, src=x_ref)]
        first += [copy(1 + j, me, (*chip, c), src=x_ref) for j, chip in enumerate(chips)]
        for cp in first:
            cp.start()
        passed = [copy(4 + j, (*chip, c), sibling) for j, chip in enumerate(chips)]
        for j, chip in enumerate(chips):
            copy(1 + j, (*chip, c), me).wait_recv()
            passed[j].start()
        copy(0, sibling, me).wait_recv()
        for j, chip in enumerate(chips):
            copy(4 + j, (*chip, 1 - c), me).wait_recv()
        for cp in first + passed:
            cp.wait_send()
        mine.wait()

    return pl.pallas_call(
        body, out_shape=jax.ShapeDtypeStruct((N_DEV, rows, cols), slab.dtype),
        in_specs=[pl.BlockSpec(memory_space=pltpu.VMEM)] + [_ANY] * len(after),
        out_specs=pl.BlockSpec(memory_space=pltpu.VMEM),
        scratch_shapes=[pltpu.SemaphoreType.DMA((7,)), pltpu.SemaphoreType.DMA((7,)), pltpu.SemaphoreType.DMA],
        name=name, compiler_params=pltpu.CompilerParams(vmem_limit_bytes=VMEM_MID),
    )(slab, *after)


def _hbm(a):
    return pltpu.with_memory_space_constraint(a, pltpu.HBM)


def _half_rows(ref, q, c):
    half = ref.shape[1] // 2
    return ref.at[q, pl.ds(c * half, half)]


def _ici_copy(src, dst, send_sems, recv_sems, k, to):
    return pltpu.make_async_remote_copy(src_ref=src, dst_ref=dst, send_sem=send_sems.at[k], recv_sem=recv_sems.at[k],
                                        device_id=to, device_id_type=MESH)


def _gather_start(bufs, sizes, name):
    n = len(bufs)
    ng = len(sizes)

    def body(*refs):
        ins = refs[:n]
        sems = refs[n:n + 2 * ng]
        token = refs[-1]
        x, y, c, chips = _place()
        a = 0
        for g, sz in enumerate(sizes):
            for i in range(sz):
                part = _half_rows(ins[a], 2 * x + y, c)
                for j, chip in enumerate(chips):
                    _ici_copy(part, part, sems[2 * g], sems[2 * g + 1], 3 * i + j, (*chip, c)).start()
                a += 1
        token[...] = jnp.zeros_like(token)

    sem_shapes = []
    for sz in sizes:
        sem_shapes += [pltpu.SemaphoreType.DMA((3 * sz,)), pltpu.SemaphoreType.DMA((3 * sz,))]
    outs = pl.pallas_call(
        body, name=name,
        out_shape=tuple(sem_shapes) + tuple(pltpu.HBM(b.shape, b.dtype) for b in bufs)
                  + (jax.ShapeDtypeStruct((SUBLANE, LANE), F32),),
        in_specs=[_HBM] * n,
        out_specs=tuple([_SEM] * (2 * ng) + [_HBM] * n + [pl.BlockSpec(memory_space=pltpu.VMEM)]),
        input_output_aliases={i: 2 * ng + i for i in range(n)},
        compiler_params=pltpu.CompilerParams(has_side_effects=_EFFECT),
    )(*[_hbm(b) for b in bufs])
    sems = [(outs[2 * g], outs[2 * g + 1]) for g in range(ng)]
    return sems, list(outs[2 * ng:2 * ng + n]), outs[-1]


def _gather_wait(bufs, sems, after, name):
    n = len(bufs)
    na = len(after)

    def body(*refs):
        ins = refs[:n]
        send_sems, recv_sems = refs[n], refs[n + 1]
        x, y, c, chips = _place()
        for i in range(n):
            sent = _half_rows(ins[i], 2 * x + y, c)
            for j, (cx, cy) in enumerate(chips):
                cp = _ici_copy(sent, _half_rows(ins[i], 2 * cx + cy, c), send_sems, recv_sems, 3 * i + j, (cx, cy, c))
                cp.wait_send()
                cp.wait_recv()

    outs = pl.pallas_call(
        body, name=name,
        out_shape=tuple(pltpu.HBM(b.shape, b.dtype) for b in bufs),
        in_specs=[_HBM] * n + [_SEM, _SEM] + [_ANY] * na,
        out_specs=tuple([_HBM] * n),
        input_output_aliases={i: i for i in range(n)},
        compiler_params=pltpu.CompilerParams(has_side_effects=_EFFECT),
    )(*bufs, sems[0], sems[1], *after)
    return list(outs)


def _exchange_halves(bufs, name):
    n = len(bufs)

    def body(*refs):
        ins = refs[:n]
        send_sems, recv_sems = refs[-2:]
        x, y, c, chips = _place()
        sends = []
        for i in range(n):
            for j, (cx, cy) in enumerate(chips):
                part = _half_rows(ins[i], 2 * cx + cy, c)
                cp = _ici_copy(part, part, send_sems, recv_sems, 3 * i + j, (x, y, 1 - c))
                cp.start()
                sends.append(cp)
        for i in range(n):
            for j, (cx, cy) in enumerate(chips):
                part = _half_rows(ins[i], 2 * cx + cy, 1 - c)
                _ici_copy(part, part, send_sems, recv_sems, 3 * i + j, (x, y, 1 - c)).wait_recv()
        for cp in sends:
            cp.wait_send()

    outs = pl.pallas_call(
        body, name=name,
        out_shape=[jax.ShapeDtypeStruct(b.shape, b.dtype) for b in bufs],
        in_specs=[_ANY] * n, out_specs=[_ANY] * n,
        input_output_aliases={i: i for i in range(n)},
        scratch_shapes=[pltpu.SemaphoreType.DMA((3 * n,)), pltpu.SemaphoreType.DMA((3 * n,))],
    )(*bufs)
    return list(outs)


def _other_half(ref, c):
    half = ref.shape[1] // 2
    return ref.at[:, pl.ds((1 - c) * half, half)]


def _prereduce_plan(src, land, x, y, c, chips):
    return [(_other_half(src, c), land, (x, y, 1 - c))]


def _prereduce_land(d):
    return (d.shape[0], d.shape[1] // 2, d.shape[2])


def _scatter_plan(src, land, x, y, c, chips):
    return [(src.at[2 * cx + cy], land.at[j], (cx, cy, c)) for j, (cx, cy) in enumerate(chips)]


def _scatter_land(p):
    return (3,) + p.shape[1:]


def _allgather_plan(src, land, x, y, c, chips):
    peers = [(x, y, 1 - c)] + [(cx, cy, cc) for (cx, cy) in chips for cc in (c, 1 - c)]
    return [(src, land.at[4 * x + 2 * y + c], p) for p in peers]


def _allgather_land(s):
    return (N_DEV,) + s.shape


_COPIES_PER_SOURCE = {_prereduce_plan: 1, _scatter_plan: 3, _allgather_plan: N_DEV - 1}


def _split_start(srcs, plan, land_shape, name, land_dtype=BF16):
    n = len(srcs)
    land_shapes = [land_shape(s) for s in srcs]
    per = _COPIES_PER_SOURCE[plan]

    def body(*refs):
        src_refs, lands = refs[:n], refs[n:2 * n]
        send_sems, recv_sems = refs[2 * n], refs[2 * n + 1]
        token = refs[-1]
        x, y, c, chips = _place()
        for i in range(n):
            for j, (s, d, to) in enumerate(plan(src_refs[i], lands[i], x, y, c, chips)):
                _ici_copy(s, d, send_sems, recv_sems, per * i + j, to).start()
        token[...] = jnp.zeros_like(token)

    outs = pl.pallas_call(
        body, name=name,
        out_shape=(pltpu.SemaphoreType.DMA((per * n,)), pltpu.SemaphoreType.DMA((per * n,)))
                  + tuple(pltpu.HBM(p.shape, p.dtype) for p in srcs)
                  + tuple(pltpu.HBM(s, land_dtype) for s in land_shapes)
                  + (jax.ShapeDtypeStruct((SUBLANE, LANE), F32),),
        in_specs=[_HBM] * (2 * n),
        out_specs=tuple([_SEM, _SEM] + [_HBM] * (2 * n) + [pl.BlockSpec(memory_space=pltpu.VMEM)]),
        input_output_aliases={i: 2 + i for i in range(2 * n)},
        compiler_params=pltpu.CompilerParams(has_side_effects=_EFFECT),
    )(*[_hbm(p) for p in srcs], *[_hbm(lax.empty(s, land_dtype)) for s in land_shapes])
    return (outs[0], outs[1]), list(outs[2:2 + n]), list(outs[2 + n:2 + 2 * n]), outs[-1]


def _split_wait(psums, lands, sems, plan, after, name):
    n = len(psums)
    na = len(after)
    per = _COPIES_PER_SOURCE[plan]

    def body(*refs):
        srcs, lnds = refs[:n], refs[n:2 * n]
        send_sems, recv_sems = refs[2 * n], refs[2 * n + 1]
        x, y, c, chips = _place()
        for i in range(n):
            for j, (s, d, to) in enumerate(plan(srcs[i], lnds[i], x, y, c, chips)):
                cp = _ici_copy(s, d, send_sems, recv_sems, per * i + j, to)
                cp.wait_send()
                cp.wait_recv()

    outs = pl.pallas_call(
        body, name=name,
        out_shape=tuple(pltpu.HBM(a.shape, a.dtype) for a in list(psums) + list(lands)),
        in_specs=[_HBM] * (2 * n) + [_SEM, _SEM] + [_ANY] * na,
        out_specs=tuple([_HBM] * (2 * n)),
        input_output_aliases={i: i for i in range(2 * n)},
        compiler_params=pltpu.CompilerParams(has_side_effects=_EFFECT),
    )(*psums, *lands, sems[0], sems[1], *after)
    return list(outs[:n]), list(outs[n:])


def _layer_half(ref, l, core):
    half = ref.shape[1] // 2
    return ref.at[l, pl.ds(core * half, half)]


def _exchange_grads_start(grads, l, name):
    n = len(grads)

    def body(*refs):
        ins = refs[:n]
        send_sems, recv_sems = refs[n], refs[n + 1]
        token = refs[-1]
        x, y, c, _ = _place()
        for i in range(n):
            mine = _layer_half(ins[i], l, c)
            _ici_copy(mine, mine, send_sems, recv_sems, i, (x, y, 1 - c)).start()
        token[...] = jnp.zeros_like(token)

    outs = pl.pallas_call(
        body, name=name,
        out_shape=(pltpu.SemaphoreType.DMA((n,)), pltpu.SemaphoreType.DMA((n,)))
                  + tuple(pltpu.HBM(g.shape, g.dtype) for g in grads)
                  + (jax.ShapeDtypeStruct((SUBLANE, LANE), F32),),
        in_specs=[_HBM] * n,
        out_specs=tuple([_SEM, _SEM] + [_HBM] * n + [pl.BlockSpec(memory_space=pltpu.VMEM)]),
        input_output_aliases={i: 2 + i for i in range(n)},
        compiler_params=pltpu.CompilerParams(has_side_effects=_EFFECT),
    )(*[_hbm(g) for g in grads])
    return (outs[0], outs[1]), list(outs[2:2 + n]), outs[-1]


def _exchange_grads_wait(grads, l, sems, after, name):
    n = len(grads)

    def body(*refs):
        ins = refs[:n]
        send_sems, recv_sems = refs[n], refs[n + 1]
        x, y, c, _ = _place()
        for i in range(n):
            cp = _ici_copy(_layer_half(ins[i], l, c), _layer_half(ins[i], l, 1 - c), send_sems, recv_sems, i,
                           (x, y, 1 - c))
            cp.wait_send()
            cp.wait_recv()

    outs = pl.pallas_call(
        body, name=name,
        out_shape=tuple(pltpu.HBM(g.shape, g.dtype) for g in grads),
        in_specs=[_HBM] * n + [_SEM, _SEM] + [_ANY] * len(after),
        out_specs=tuple([_HBM] * n),
        input_output_aliases={i: i for i in range(n)},
        compiler_params=pltpu.CompilerParams(has_side_effects=_EFFECT),
    )(*grads, sems[0], sems[1], *after)
    return list(outs)


def _pack(arrs):
    flat = jnp.concatenate([a.reshape(-1).astype(F32) for a in arrs])
    pad = (-flat.shape[0]) % (SUBLANE * LANE)
    return jnp.pad(flat, (0, pad)).reshape(-1, LANE)


def _unpack(slab, shapes):
    flat = slab.reshape(-1)
    out, off = [], 0
    for s in shapes:
        n = math.prod(s)
        out.append(flat[off:off + n].reshape(s))
        off += n
    return out


def _split6(v):
    d = v.shape[-1] // 6
    return [v[:, k * d:(k + 1) * d].reshape(2, 1, d) for k in range(6)]


BIG = ("w_in", "w_br_attn", "w_br_pool", "w_br_sgu", "w_br_conv", "w_gate", "w_o", "w_ff_gate", "w_ff_up", "w_ff_down")
KIND = {"w_in": "col", "w_br_attn": "col", "w_br_pool": "col", "w_br_sgu": "col", "w_br_conv": "col",
        "w_gate": "col", "w_o": "row", "w_ff_gate": "col", "w_ff_up": "col", "w_ff_down": "row"}
SMALL = ("c_ctx", "b_ada", "q_norm_g", "k_norm_g", "pool_w", "pool_scale", "sgu_ln_g", "sgu_ln_b", "sgu_w",
         "sgu_b", "conv_w", "b_gate", "ln1_g", "ln1_b", "ln2_g", "ln2_b")
WEIGHTS = ("c_ctx", "w_ada", "b_ada", "w_in", "q_norm_g", "k_norm_g", "pool_w", "pool_scale", "sgu_ln_g", "sgu_ln_b",
           "sgu_w", "sgu_b", "conv_w", "w_br_attn", "w_br_pool", "w_br_sgu", "w_br_conv", "w_gate", "b_gate", "w_o",
           "ln1_g", "ln1_b", "w_ff_gate", "w_ff_up", "w_ff_down", "ln2_g", "ln2_b")


def _step(x, c, ctx, loss_target, W, M, V):
    L = W["w_ada"].shape[0]
    assert L == 2, "core c of a chip carries layer c of the weight traffic"
    N, D = x.shape[1], x.shape[2]
    NC = ctx.shape[1]
    T = NC + N
    FF = W["w_ff_down"].shape[1] * N_CHIP
    assert NC % ROW_TM == 0 and N % ROW_TM == 0 and N % GRID_W == 0 and D % LANE == 0
    ncb = NC // ROW_TM
    nblk = T // ROW_TM
    alpha = (2 * L) ** 0.25
    ax, ay, ac = lax.axis_index("x"), lax.axis_index("y"), lax.axis_index("c")
    chip = 2 * ax + ay
    dev = 2 * chip + ac
    chip_arr = jnp.reshape(chip, (1,)).astype(jnp.int32)
    core_arr = jnp.reshape(ac, (1,)).astype(jnp.int32)
    ns_ada = W["w_ada"].shape[2]
    chip_devs = (0, 2, 4, 6)

    conv_shape = W["conv_w"].shape
    g0 = _all_gather_small(_pack([c, W["conv_w"]]), "gather_c")
    c_all = g0[:, :D // LANE, :].reshape(N_DEV, D)
    conv_parts = [g0[d].reshape(-1)[D:D + math.prod(conv_shape)].reshape(conv_shape) for d in chip_devs]
    conv_full = jnp.concatenate(conv_parts, axis=-1)
    cvec = jnp.concatenate([c_all, W["c_ctx"][None], jnp.zeros((16 - N_DEV - 1, D), F32)], axis=0)
    b_cols = lax.dynamic_slice_in_dim(W["b_ada"], chip * ns_ada, ns_ada, axis=1).reshape(L, 1, ns_ada)
    mod_part = _ada_fwd(cvec, W["w_ada"], b_cols, "ada_fwd")
    g1 = _all_gather_small(mod_part.reshape(-1, LANE), "gather_mod")
    mod_all = jnp.concatenate([g1[d].reshape(L, 16, ns_ada) for d in chip_devs], axis=-1)
    mod_lat = lax.dynamic_index_in_dim(mod_all, dev, axis=1, keepdims=True)
    mods = jnp.concatenate([mod_all[:, N_DEV:N_DEV + 1], mod_lat], axis=1)

    ffn_keys = ("w_ff_gate", "w_ff_up", "w_ff_down")
    mid_keys = ("w_br_attn", "w_br_pool", "w_br_sgu", "w_br_conv", "w_o")
    groups = [grp for l in range(L) for grp in ([("w_in", l)], [("w_gate", l)], [(k, l) for k in mid_keys],
                                                [("w_ff_gate", l), ("w_ff_up", l)], [("w_ff_down", l)])]
    GPL = 5
    flying, gsems, gtokens = [None] * len(groups), [None] * len(groups), []
    for n_start, gs in enumerate(((0, 1), (2, 3, 4), (5, 6, 7, 8, 9))):
        members = [kl for g in gs for kl in groups[g]]
        casts = [_cast_into(chip_arr, W[k], l, "cast_%s_l%d" % (k, l), after=gtokens[-1:] or [g1])
                 for k, l in members]
        sems, bufs, token = _gather_start(casts, [len(groups[g]) for g in gs], "gather_start_%d" % n_start)
        gtokens.append(token)
        lo = 0
        for g, sem in zip(gs, sems):
            flying[g], gsems[g] = bufs[lo:lo + len(groups[g])], sem
            lo += len(groups[g])
    full = {k: [None] * L for k in BIG}

    def weights_ready(g, after):
        got = _gather_wait(flying[g], gsems[g], after, "gather_wait_%d" % g)
        for (k, l), buf in zip(groups[g], _exchange_halves(got, "exchange_halves_%d" % g)):
            full[k][l] = buf

    cos, sin = _rope_tables(N, NC)
    x0 = jnp.concatenate([ctx[0], x[0]], axis=0)

    saved = []
    xin = x0
    h1 = None
    dy_top = loss_parts = None
    for l in range(L):
        sh1, sc1, g1m, sh2, sc2, g2m = _split6(mods[l])
        tag = "_l%d" % l
        if l == 0:
            h1 = _modulate(xin, sc1, sh1, ncb, "modulate" + tag)
        weights_ready(GPL * l, [h1])
        p = _mm_nn(h1, full["w_in"], l, "col", name="mm_in" + tag)
        qg, kg = W["q_norm_g"][l][None], W["k_norm_g"][l][None]
        q, k, v = _qk_fwd(p, cos, sin, qg, kg, "qk_fwd" + tag)
        att = _attn_fwd(q, k, v, NC, "attn_fwd" + tag)
        ps = W["pool_scale"][l][None]
        ypool = _pool_fwd(p, W["pool_w"][l], ps, NC, "pool_fwd" + tag)
        lg, lb = W["sgu_ln_g"][l][None], W["sgu_ln_b"][l][None]
        b_st = W["sgu_b"][l].T
        ysgu = _sgu_fwd(p, lg, lb, W["sgu_w"][l], b_st, "sgu_fwd" + tag)
        yconv = _conv_fwd(p, conv_full[l], NC, "conv_fwd" + tag)
        brs = (att, ypool, ysgu, yconv)
        weights_ready(GPL * l + 1, list(brs))
        gates = _mm_nn(h1, full["w_gate"], l, "col", name="mm_gate" + tag, bias=W["b_gate"][l][None],
                       act=jax.nn.sigmoid, out_dtype=BF16)
        weights_ready(GPL * l + 2, [gates])
        ts = [_mm_nn(b, full[wk], l, "col", name="mm_" + wk + tag, out_dtype=BF16)
              for b, wk in zip(brs, ("w_br_attn", "w_br_pool", "w_br_sgu", "w_br_conv"))]
        mg = _merge_fwd(gates, ts, "merge_fwd" + tag)
        m = _mm_nn(mg, full["w_o"], l, "row", name="mm_o" + tag, tn_pref=1024)
        ln1g, ln1b = W["ln1_g"][l][None], W["ln1_b"][l][None]
        x1, h2 = _resid_ln(xin, m, g1m, ln1g, ln1b, sc2, sh2, ncb, alpha, "resid_ln1" + tag)
        weights_ready(GPL * l + 3, [h2])
        gg, uu, act = _ffn_up(h2, full["w_ff_gate"][l], full["w_ff_up"][l], "ffn_up" + tag)
        weights_ready(GPL * l + 4, [act])
        ff = _mm_nn(act, full["w_ff_down"], l, "row", name="mm_ffd" + tag, tm_pref=384, tn_pref=512)
        ln2g, ln2b = W["ln2_g"][l][None], W["ln2_b"][l][None]
        saved.append(dict(xin=xin, h1=h1, p=p, q=q, k=k, v=v, brs=brs, gates=gates, ts=ts, mg=mg, m=m, x1=x1, h2=h2,
                          gg=gg, uu=uu, act=act, ff=ff))
        if l + 1 < L:
            nsh1, nsc1 = _split6(mods[l + 1])[:2]
            xin, h1 = _resid_ln(x1, ff, g2m, ln2g, ln2b, nsc1, nsh1, ncb, alpha, "resid_ln2" + tag)
        else:
            dy_top, loss_parts = _resid_ln_loss(x1, ff, g2m, ln2g, ln2b, loss_target[0], ncb, alpha,
                                                "resid_ln2_loss" + tag)
    loss_dev = jnp.sum(loss_parts[:, 0, 0])

    def lat_ctx(part_rows):
        return jnp.stack([jnp.sum(part_rows[:ncb], axis=0), jnp.sum(part_rows[ncb:], axis=0)])

    dW = {k: [None] * L for k in BIG}
    small = {k: [None] * L for k in ("q_norm_g", "k_norm_g", "pool_w", "pool_scale", "sgu_ln_g", "sgu_ln_b", "sgu_w",
                                     "sgu_b", "conv_w", "b_gate", "ln1_g", "ln1_b", "ln2_g", "ln2_b")}
    dmods = [None] * L
    mix_keys = tuple(k for k in BIG if k not in ffn_keys)
    grads_big = {k: None for k in BIG}
    adam_out = {k: None for k in BIG}

    def reduce_begin(keys, l, tg):
        sems, src, land, token = _split_start([dW[k][l] for k in keys], _prereduce_plan, _prereduce_land,
                                              "prereduce_start_" + tg)
        return dict(keys=keys, l=l, sems=sems, src=src, land=land, token=token, tg=tg)

    def reduce_scatter(st, after):
        dws, recv = _split_wait(st["src"], st["land"], st["sems"], _prereduce_plan, after,
                                "prereduce_wait_" + st["tg"])
        psums = [_pair_sum(core_arr, d, r, "pair_sum_%s_l%d" % (k, st["l"]))
                 for k, d, r in zip(st["keys"], dws, recv)]
        st["sems"], st["src"], st["land"], st["token"] = _split_start(psums, _scatter_plan, _scatter_land,
                                                                      "scatter_start_" + st["tg"])

    def reduce_finish(st, after):
        src, land = _split_wait(st["src"], st["land"], st["sems"], _scatter_plan, after, "scatter_wait_" + st["tg"])
        for k, p_, r in zip(st["keys"], src, land):
            grads_big[k] = _chip_sum(chip_arr, core_arr, p_, r, st["l"], L, grads_big[k],
                                     "chip_sum_%s_l%d" % (k, st["l"]))
        st["sems"], st["fly"], st["token"] = _exchange_grads_start([grads_big[k] for k in st["keys"]], st["l"],
                                                                   "exchange_grads_start_" + st["tg"])

    def reduce_done(st, after):
        done = _exchange_grads_wait(st["fly"], st["l"], st["sems"], after, "exchange_grads_wait_" + st["tg"])
        grads_big.update(zip(st["keys"], done))

    def adam(keys, l, after):
        for k in keys:
            adam_out[k] = _adamw_layer(W[k], grads_big[k], M[k], V[k], l, adam_out[k], after,
                                       "adamw_%s_l%d" % (k, l))

    deferred = []

    def grad_of(key, l, thunk):
        if l == 0:
            deferred.append((key, thunk))
        else:
            dW[key][l] = thunk(())

    dxa, dhs, sc_prev = dy_top, [], None
    red_l1 = red_ffn = None
    for l in reversed(range(L)):
        s = saved[l]
        sh1, sc1, g1m, sh2, sc2, g2m = _split6(mods[l])
        tag = "_l%d" % l
        ln1g, ln1b = W["ln1_g"][l][None], W["ln1_b"][l][None]
        ln2g, ln2b = W["ln2_g"][l][None], W["ln2_b"][l][None]
        dx1a, dff, part2 = _ln_bwd(dxa, dhs, sc_prev if dhs else sc1, s["x1"], s["ff"], g2m, ln2g, ln2b, ncb, alpha,
                                   "ln2_bwd" + tag, after=[] if red_l1 is None else [red_l1["token"]])
        small["ln2_g"][l] = jnp.sum(part2[:, 0], axis=0)
        small["ln2_b"][l] = jnp.sum(part2[:, 1], axis=0)
        dg2 = lat_ctx(part2[:, 2])
        if dhs:
            dmods[l + 1][1], dmods[l + 1][0] = lat_ctx(part2[:, 3]), lat_ctx(part2[:, 4])
        dgg, duu = _ffn_down_bwd(dff, full["w_ff_down"][l], s["gg"], s["uu"], "ffn_down_bwd" + tag)
        if l == 0:
            reduce_scatter(red_l1, [dgg])
        dW["w_ff_down"][l] = _mm_tn(s["act"], dff, "row", nq=N_CHIP, kdim=FF // N_CHIP, ndim=D, name="tn_ffd" + tag,
                                    tk_pref=1408, tn_pref=512, after=[red_l1["token"]] if l == 0 else [])
        dh2a = _mm_nt(dgg, full["w_ff_gate"], l, "col", name="nt_ffg" + tag)
        dh2b = _mm_nt(duu, full["w_ff_up"], l, "col", name="nt_ffu" + tag)
        dW["w_ff_gate"][l] = _mm_tn(s["h2"], dgg, "col", nq=N_CHIP, kdim=D, ndim=FF // N_CHIP, name="tn_ffg" + tag,
                                    tn_pref=1408)
        dW["w_ff_up"][l] = _mm_tn(s["h2"], duu, "col", nq=N_CHIP, kdim=D, ndim=FF // N_CHIP, name="tn_ffu" + tag,
                                  tn_pref=1408)
        ties = []
        if l == 0:
            red_ffn = reduce_begin(ffn_keys, 0, "l0_ffn")
            ties = [red_l1["token"], red_ffn["token"]]
        dx0a, dm, part1 = _ln_bwd(dx1a, [dh2a, dh2b], sc2, s["xin"], s["m"], g1m, ln1g, ln1b, ncb, alpha,
                                  "ln1_bwd" + tag, after=ties)
        small["ln1_g"][l] = jnp.sum(part1[:, 0], axis=0)
        small["ln1_b"][l] = jnp.sum(part1[:, 1], axis=0)
        dg1 = lat_ctx(part1[:, 2])
        dsc2, dsh2 = lat_ctx(part1[:, 3]), lat_ctx(part1[:, 4])
        dmods[l] = [None, None, dg1, dsh2, dsc2, dg2]
        dmg = _mm_nt(dm, full["w_o"], l, "row", name="nt_o" + tag)
        grad_of("w_o", l, lambda after, a=s["mg"], d=dm, tag=tag: _mm_tn(
            a, d, "row", nq=N_CHIP, kdim=D // N_CHIP, ndim=D, name="tn_o" + tag, tn_pref=1024, after=after))
        dpre, dt0, dt1, dt2, dt3, bpart = _merge_bwd(dmg, s["gates"], s["ts"], "merge_bwd" + tag)
        small["b_gate"][l] = jnp.sum(bpart[:, 0], axis=0)
        dh1a = _mm_nt(dpre, full["w_gate"], l, "col", name="nt_gate" + tag)
        grad_of("w_gate", l, lambda after, a=s["h1"], d=dpre, tag=tag: _mm_tn(
            a, d, "col", nq=N_CHIP, kdim=D, ndim=D, name="tn_gate" + tag, tn_pref=1024, after=after))
        dbrs = []
        for b, dt, wk, odt in zip(s["brs"], (dt0, dt1, dt2, dt3), ("w_br_attn", "w_br_pool", "w_br_sgu", "w_br_conv"),
                                  (BF16, F32, F32, F32)):
            dbrs.append(_mm_nt(dt, full[wk], l, "col", name="nt_" + wk + tag, out_dtype=odt))
            grad_of(wk, l, lambda after, a=b, d=dt, wk=wk, tag=tag: _mm_tn(
                a, d, "col", nq=N_CHIP, kdim=a.shape[1], ndim=D // N_CHIP, name="tn_" + wk + tag, after=after))
        qg, kg = W["q_norm_g"][l][None], W["k_norm_g"][l][None]
        dq, dk, dv = _attn_bwd(s["q"], s["k"], s["v"], dbrs[0], NC, "attn_bwd" + tag)
        dp, qkpart = _qk_bwd(s["p"], dq, dk, dv, cos, sin, qg, kg, "qk_bwd" + tag)
        small["q_norm_g"][l] = jnp.sum(qkpart[:, 0], axis=0)
        small["k_norm_g"][l] = jnp.sum(qkpart[:, 1], axis=0)
        ps = W["pool_scale"][l][None]
        dp, small["pool_w"][l], dps = _pool_bwd(s["p"], dbrs[1], W["pool_w"][l], ps, dp, NC, "pool_bwd" + tag)
        small["pool_scale"][l] = dps[0]
        lg, lb = W["sgu_ln_g"][l][None], W["sgu_ln_b"][l][None]
        dp, small["sgu_w"][l], dsacc, dln = _sgu_bwd(s["p"], dbrs[2], lg, lb, W["sgu_w"][l], W["sgu_b"][l].T, dp,
                                                     "sgu_bwd" + tag)
        small["sgu_b"][l] = jnp.sum(dsacc.reshape(GC, N_GROUP, GC), axis=-1).T
        small["sgu_ln_g"][l], small["sgu_ln_b"][l] = dln[0], dln[1]
        dp, small["conv_w"][l] = _conv_bwd(s["p"], dbrs[3], conv_full[l], dp, NC, "conv_bwd" + tag)
        dh1b = _mm_nt(dp, full["w_in"], l, "col", name="nt_in" + tag)
        grad_of("w_in", l, lambda after, a=s["h1"], d=dp, tag=tag: _mm_tn(
            a, d, "col", nq=N_CHIP, kdim=D, ndim=IN_W // N_CHIP, name="tn_in" + tag, tn_pref=1152, after=after))
        dxa, dhs, sc_prev = dx0a, [dh1a, dh1b], sc1
        if l == 1:
            red_l1 = reduce_begin(BIG, 1, "l1")
    dx0, part0 = _mod_bwd(dxa, dhs, sc_prev, x0, ncb, "mod_bwd")
    dmods[0][1], dmods[0][0] = lat_ctx(part0[:, 0]), lat_ctx(part0[:, 1])
    grad_x = dx0[NC:][None]
    dmod = jnp.stack([jnp.concatenate(dmods[l], axis=-1) for l in range(L)])

    small_names = [k for k in SMALL if k not in ("c_ctx", "b_ada")]
    small_arrs = [jnp.stack(small[k]) for k in small_names]
    small_shapes = [a.shape for a in small_arrs]
    slab = _pack([loss_dev.reshape(1), jnp.zeros((LANE - 1,), F32), dmod] + small_arrs)
    sg_sems, sg_src, sg_land, sg_token = _split_start([slab], _allgather_plan, _allgather_land, "small_start", F32)
    reduce_scatter(red_ffn, [sg_token])
    for key, thunk in deferred:
        dW[key][0] = thunk([red_ffn["token"]])
    red_mix = reduce_begin(mix_keys, 0, "l0_mix")
    sg_src, sg_land = _split_wait(sg_src, sg_land, sg_sems, _allgather_plan, [red_mix["token"]], "small_wait")
    g2 = lax.dynamic_update_index_in_dim(sg_land[0], sg_src[0], dev, 0)
    total = _slab_sum(g2, "slab_sum")
    flat_total = total.reshape(-1)
    loss = flat_total[0]
    nmod = L * 2 * 6 * D
    dmod_sum = flat_total[LANE:LANE + nmod].reshape(L, 2, 6 * D)
    small_grads = dict(zip(small_names, _unpack(flat_total[LANE + nmod:], small_shapes)))
    dmod_lat = g2.reshape(N_DEV, -1)[:, LANE:LANE + nmod].reshape(N_DEV, L, 2, 6 * D)[:, :, 1]
    dm16 = jnp.concatenate([jnp.transpose(dmod_lat, (1, 0, 2)), dmod_sum[:, 0:1],
                            jnp.zeros((L, 16 - N_DEV - 1, 6 * D), F32)], axis=1)
    small_grads["b_ada"] = dmod_sum[:, 0] + dmod_sum[:, 1]
    dm16_cols = lax.dynamic_slice_in_dim(dm16, chip * ns_ada, ns_ada, axis=2)
    grad_w_ada, ds16 = _ada_bwd(cvec, dm16_cols, W["w_ada"], "ada_bwd")
    conv_grad_full = small_grads["conv_w"]
    small_grads["conv_w"] = lax.dynamic_slice_in_dim(conv_grad_full, chip * GC, GC, axis=2)

    reduce_finish(red_l1, [red_ffn["token"], red_mix["token"]])
    reduce_scatter(red_mix, [red_l1["token"]])
    reduce_done(red_l1, [red_mix["token"]])
    adam(BIG, 1, [red_mix["token"]])
    reduce_finish(red_ffn, [adam_out[k][0] for k in BIG])
    delta, new_m, new_v = {}, {}, {}
    delta["w_ada"], new_m["w_ada"], new_v["w_ada"] = _adamw(W["w_ada"], grad_w_ada, M["w_ada"], V["w_ada"],
                                                            "adamw_w_ada", after=[red_ffn["token"]])
    reduce_done(red_ffn, [delta["w_ada"]])
    reduce_finish(red_mix, [delta["w_ada"]])
    adam(ffn_keys, 0, [red_mix["token"]])
    reduce_done(red_mix, [adam_out[k][0] for k in ffn_keys])
    adam(mix_keys, 0, [])
    for k in BIG:
        delta[k], new_m[k], new_v[k] = adam_out[k]

    g3 = _all_gather_small(ds16[N_DEV].reshape(-1, LANE), "gather_dsilu", after=[adam_out[k][0] for k in mix_keys])
    small_grads["c_ctx"] = _cctx_grad(g3, W["c_ctx"].reshape(-1, LANE), "cctx_grad").reshape(D)
    shapes = [W[k].shape for k in SMALL]
    sd, sm, sv = _adamw(_pack([W[k] for k in SMALL]), _pack([small_grads[k] for k in SMALL]),
                        _pack([M[k] for k in SMALL]), _pack([V[k] for k in SMALL]), "adamw_small")
    for k, d_, m_, v_ in zip(SMALL, _unpack(sd, shapes), _unpack(sm, shapes), _unpack(sv, shapes)):
        delta[k], new_m[k], new_v[k] = d_, m_, v_
    grads = dict(grads_big)
    grads["w_ada"] = grad_w_ada
    grads.update(small_grads)
    return (loss, grad_x, *[grads[k] for k in WEIGHTS], *[delta[k] for k in WEIGHTS],
            *[new_m[k] for k in WEIGHTS], *[new_v[k] for k in WEIGHTS])


def kernel(x, c, ctx, c_ctx, w_ada, b_ada, w_in, q_norm_g, k_norm_g, pool_w, pool_scale, sgu_ln_g, sgu_ln_b, sgu_w, sgu_b, conv_w, w_br_attn, w_br_pool, w_br_sgu, w_br_conv, w_gate, b_gate, w_o, ln1_g, ln1_b, w_ff_gate, w_ff_up, w_ff_down, ln2_g, ln2_b, loss_target, m_c_ctx, m_w_ada, m_b_ada, m_w_in, m_q_norm_g, m_k_norm_g, m_pool_w, m_pool_scale, m_sgu_ln_g, m_sgu_ln_b, m_sgu_w, m_sgu_b, m_conv_w, m_w_br_attn, m_w_br_pool, m_w_br_sgu, m_w_br_conv, m_w_gate, m_b_gate, m_w_o, m_ln1_g, m_ln1_b, m_w_ff_gate, m_w_ff_up, m_w_ff_down, m_ln2_g, m_ln2_b, v_c_ctx, v_w_ada, v_b_ada, v_w_in, v_q_norm_g, v_k_norm_g, v_pool_w, v_pool_scale, v_sgu_ln_g, v_sgu_ln_b, v_sgu_w, v_sgu_b, v_conv_w, v_w_br_attn, v_w_br_pool, v_w_br_sgu, v_w_br_conv, v_w_gate, v_b_gate, v_w_o, v_ln1_g, v_ln1_b, v_w_ff_gate, v_w_ff_up, v_w_ff_down, v_ln2_g, v_ln2_b):
    args = locals()
    W = {k: args[k] for k in WEIGHTS}
    M = {k: args["m_" + k] for k in WEIGHTS}
    V = {k: args["v_" + k] for k in WEIGHTS}
    return _step(x, c, ctx, loss_target, W, M, V)
```
